```python
import jax, jax.numpy as jnp
from jax import lax
import numpy as np

D_MODEL = 1024
BATCH = 8
SEQ = 2048
DEPTH = 2
DEC_BATCH = 128
DEC_SEQ = 4
PAST_LEN = 16384
PAGE_SIZE = 128

N_MIXERS = 2
N_A = (DEPTH + N_MIXERS - 1) // N_MIXERS
N_B = DEPTH // N_MIXERS
N_META = 16
D_RNN = D_MODEL
BLOCK_W = 256
N_BLK = D_RNN // BLOCK_W
CONV_A = 4
RG_C = 8.0
D_CONV = D_MODEL
CONV_B = 3
D_FF = ((8 * D_MODEL // 3 + 255) // 256) * 256
EPS = 1e-6

kernel_name = 'hybrid_rglru_shortconv_decode_step'


def _rmsnorm(x, w):
    xf = x.astype(jnp.float32)
    y = xf * lax.rsqrt(jnp.mean(xf * xf, axis=-1, keepdims=True) + EPS)
    return (y * w.astype(jnp.float32)).astype(x.dtype)


def _causal_dwconv(buf, u, w):
    K = w.shape[0]
    T = u.shape[1]
    full = jnp.concatenate([buf.astype(u.dtype), u], axis=1)
    y = full[:, 0:T] * w[0]
    for k in range(1, K):
        y = y + full[:, k:k + T] * w[k]
    return y, full[:, T:]


def _rglru(xc, h0, ga_w, ga_b, gx_w, gx_b, lam):
    Bn, T, _ = xc.shape
    xb = xc.reshape(Bn, T, N_BLK, BLOCK_W)
    r = jax.nn.sigmoid(jnp.einsum('bthi,hij->bthj', xb, ga_w) + ga_b).reshape(Bn, T, D_RNN)
    ig = jax.nn.sigmoid(jnp.einsum('bthi,hij->bthj', xb, gx_w) + gx_b).reshape(Bn, T, D_RNN)
    log_a = -RG_C * r.astype(jnp.float32) * jax.nn.softplus(-lam.astype(jnp.float32))
    a = jnp.exp(log_a)
    mult = jnp.sqrt(-jnp.expm1(2.0 * log_a))
    u = mult * (ig * xc).astype(jnp.float32)

    def step(h, au):
        a_t, u_t = au
        h = a_t * h + u_t
        return h, h

    hT, hs = lax.scan(step, h0.astype(jnp.float32), (jnp.swapaxes(a, 0, 1), jnp.swapaxes(u, 0, 1)))
    return jnp.swapaxes(hs, 0, 1).astype(xc.dtype), hT.astype(h0.dtype)


def _rglru_block(x, conv_buf, h0, w_in, conv_w, conv_b, ga_w, ga_b, gx_w, gx_b, lam, w_out):
    z = x @ w_in
    gate, xr = z[..., :D_RNN], z[..., D_RNN:]
    xc, new_buf = _causal_dwconv(conv_buf, xr, conv_w)
    xc = xc + conv_b
    hs, hT = _rglru(xc, h0, ga_w, ga_b, gx_w, gx_b, lam)
    y = (hs * jax.nn.gelu(gate)) @ w_out
    return y, new_buf, hT


def _shortconv_block(x, conv_buf, w_in, conv_w, w_out):
    z = x @ w_in
    bg, cg, v = z[..., :D_CONV], z[..., D_CONV:2 * D_CONV], z[..., 2 * D_CONV:]
    c, new_buf = _causal_dwconv(conv_buf, cg * v, conv_w)
    y = (bg * c) @ w_out
    return y, new_buf


def _swiglu(x, wg, wu, wd):
    return (jax.nn.silu(x @ wg) * (x @ wu)) @ wd


def setup_inputs(seed: int = 0) -> dict:
    key = jax.random.key(seed)
    ks = jax.random.split(key, 32)
    f32 = jnp.float32
    nrm = lambda k, shape, s: jax.random.normal(k, shape, f32) * s
    a_init = jax.random.uniform(ks[13], (N_A, D_RNN), f32, 0.9, 0.999)
    return {
        'x_prompt': nrm(ks[0], (BATCH, SEQ, D_MODEL), 1.0),
        'x_sample': nrm(ks[1], (DEC_BATCH, DEC_SEQ, D_MODEL), 1.0),
        'state_rglru_conv': nrm(ks[2], (N_A, DEC_BATCH, CONV_A - 1, D_RNN), 1.0),
        'state_rglru_h': nrm(ks[3], (N_A, DEC_BATCH, D_RNN), 0.5),
        'state_sconv': nrm(ks[4], (N_B, DEC_BATCH, CONV_B - 1, D_CONV), 1.0),
        'meta_tokens': nrm(ks[5], (N_META, D_MODEL), 1.0),
        'norm_mix_pre': 1.0 + nrm(ks[6], (DEPTH, D_MODEL), 0.05),
        'norm_mix_post': 1.0 + nrm(ks[7], (DEPTH, D_MODEL), 0.05),
        'norm_ffn_pre': 1.0 + nrm(ks[8], (DEPTH, D_MODEL), 0.05),
        'norm_ffn_post': 1.0 + nrm(ks[9], (DEPTH, D_MODEL), 0.05),
        'rg_w_in': nrm(ks[10], (N_A, D_MODEL, 2 * D_RNN), D_MODEL ** -0.5),
        'rg_conv_w': nrm(ks[11], (N_A, CONV_A, D_RNN), CONV_A ** -0.5),
        'rg_conv_b': nrm(ks[12], (N_A, D_RNN), 0.02),
        'rg_gate_a_w': nrm(ks[14], (N_A, N_BLK, BLOCK_W, BLOCK_W), BLOCK_W ** -0.5),
        'rg_gate_a_b': nrm(ks[15], (N_A, N_BLK, BLOCK_W), 0.02),
        'rg_gate_x_w': nrm(ks[16], (N_A, N_BLK, BLOCK_W, BLOCK_W), BLOCK_W ** -0.5),
        'rg_gate_x_b': nrm(ks[17], (N_A, N_BLK, BLOCK_W), 0.02),
        'rg_lambda': jnp.log(a_init) - jnp.log1p(-a_init),
        'rg_w_out': nrm(ks[18], (N_A, D_RNN, D_MODEL), D_RNN ** -0.5),
        'sc_w_in': nrm(ks[19], (N_B, D_MODEL, 3 * D_CONV), D_MODEL ** -0.5),
        'sc_conv_w': nrm(ks[20], (N_B, CONV_B, D_CONV), CONV_B ** -0.5),
        'sc_w_out': nrm(ks[21], (N_B, D_CONV, D_MODEL), D_CONV ** -0.5),
        'ffn_w_gate': nrm(ks[22], (DEPTH, D_MODEL, D_FF), D_MODEL ** -0.5),
        'ffn_w_up': nrm(ks[23], (DEPTH, D_MODEL, D_FF), D_MODEL ** -0.5),
        'ffn_w_down': nrm(ks[24], (DEPTH, D_FF, D_MODEL), D_FF ** -0.5),
    }


def reference(x_prompt, x_sample, state_rglru_conv, state_rglru_h, state_sconv, meta_tokens,
              norm_mix_pre, norm_mix_post, norm_ffn_pre, norm_ffn_post,
              rg_w_in, rg_conv_w, rg_conv_b, rg_gate_a_w, rg_gate_a_b, rg_gate_x_w, rg_gate_x_b,
              rg_lambda, rg_w_out, sc_w_in, sc_conv_w, sc_w_out,
              ffn_w_gate, ffn_w_up, ffn_w_down):

    def run_trunk(x, rg_conv_in, rg_h_in, sc_in):
        rg_conv_out, rg_h_out, sc_out = [], [], []
        for i in range(DEPTH):
            j = i // N_MIXERS
            hn = _rmsnorm(x, norm_mix_pre[i])
            if i % N_MIXERS == 0:
                m, cb, hT = _rglru_block(hn, rg_conv_in[j], rg_h_in[j], rg_w_in[j], rg_conv_w[j], rg_conv_b[j],
                                         rg_gate_a_w[j], rg_gate_a_b[j], rg_gate_x_w[j], rg_gate_x_b[j],
                                         rg_lambda[j], rg_w_out[j])
                rg_conv_out.append(cb)
                rg_h_out.append(hT)
            else:
                m, cb = _shortconv_block(hn, sc_in[j], sc_w_in[j], sc_conv_w[j], sc_w_out[j])
                sc_out.append(cb)
            x = x + _rmsnorm(m, norm_mix_post[i])
            f = _swiglu(_rmsnorm(x, norm_ffn_pre[i]), ffn_w_gate[i], ffn_w_up[i], ffn_w_down[i])
            x = x + _rmsnorm(f, norm_ffn_post[i])
        return x, jnp.stack(rg_conv_out), jnp.stack(rg_h_out), jnp.stack(sc_out)

    dt = x_prompt.dtype
    meta = jnp.broadcast_to(meta_tokens[None].astype(dt), (BATCH, N_META, D_MODEL))
    xp = jnp.concatenate([meta, x_prompt], axis=1)
    zc_a = jnp.zeros((N_A, BATCH, CONV_A - 1, D_RNN), dt)
    zh_a = jnp.zeros((N_A, BATCH, D_RNN), dt)
    zc_b = jnp.zeros((N_B, BATCH, CONV_B - 1, D_CONV), dt)
    yp, rg_conv_p, rg_h_p, sc_p = run_trunk(xp, zc_a, zh_a, zc_b)
    y_prompt = yp[:, N_META:]

    y_sample, rg_conv_s, rg_h_s, sc_s = run_trunk(x_sample, state_rglru_conv, state_rglru_h, state_sconv)

    return (y_prompt, y_sample, rg_conv_p, rg_h_p, sc_p, rg_conv_s, rg_h_s, sc_s)
```

```python
import functools

import jax
import jax.numpy as jnp
from jax import lax
from jax.experimental import pallas as pl
from jax.experimental.pallas import tpu as pltpu

D_MODEL = 1024
D_FF = 2816
N_META = 16
N_GATE_BLOCKS = 4
GATE_BLOCK_W = 256
CONV_A = 4
CONV_B = 3
RG_C = 8.0
EPS = 1e-6

ROW_CHUNK = 16
SUBLANES = 8
FF_CHUNK = 256
VMEM_LIMIT_BYTES = 56 * 1024 * 1024

F32 = jnp.float32
BF16 = jnp.bfloat16


def _rms(x, w):
    ms = jnp.mean(x * x, axis=-1, keepdims=True)
    return x * lax.rsqrt(ms + EPS) * w


def _norm_to_bf16(x_ref, w_ref, out_scr, rows):
    w = w_ref[...]

    def body(j, carry):
        r = pl.ds(pl.multiple_of(j * ROW_CHUNK, ROW_CHUNK), ROW_CHUNK)
        out_scr[r, :] = _rms(x_ref[r, :], w).astype(BF16)
        return carry

    lax.fori_loop(0, rows // ROW_CHUNK, body, 0)


def _residual_norm(x_ref, y_scr, w_ref, o_ref, rows):
    w = w_ref[...]

    def body(j, carry):
        r = pl.ds(pl.multiple_of(j * ROW_CHUNK, ROW_CHUNK), ROW_CHUNK)
        o_ref[r, :] = x_ref[r, :] + _rms(y_scr[r, :], w)
        return carry

    lax.fori_loop(0, rows // ROW_CHUNK, body, 0)


def _gelu_tanh(x):
    c = 0.7978845608028654
    return x * (0.5 * (1.0 + jnp.tanh(c * (x + 0.044715 * (x * x * x)))))


def _one_minus_sq(a, log_a):
    a2 = a * a
    return jnp.where(log_a < -0.25, 1.0 - a2, -jnp.tanh(log_a) * (a2 + 1.0))


def _rglru_kernel(S, TB, nblk,
                  x_ref, conv0_ref, h0_ref, npre_ref, npost_ref, win_ref, cw_ref, cb_ref,
                  gw_ref, gab_ref, gxb_ref, lam_ref, wout_ref,
                  o_ref, conv_out_ref, h_out_ref,
                  xn_scr, gate_scr, xr_scr, xc_scr, rp_scr, ip_scr, hs_scr, y_scr, h_scr):
    D = D_MODEL
    R = S * TB
    P = (CONV_A - 1) * S
    i = pl.program_id(0)

    @pl.when(i == 0)
    def _():
        xr_scr[0:P, :] = conv0_ref[...]
        h_scr[...] = h0_ref[...]

    _norm_to_bf16(x_ref, npre_ref, xn_scr, R)
    xn = xn_scr[...]
    gate_scr[...] = jnp.dot(xn, win_ref[:, 0:D], preferred_element_type=F32)
    xr_scr[P:P + R, :] = jnp.dot(xn, win_ref[:, D:2 * D], preferred_element_type=F32)

    taps = [jnp.broadcast_to(cw_ref[k:k + 1, :], (ROW_CHUNK, D)) for k in range(CONV_A)]
    bias = jnp.broadcast_to(cb_ref[...], (ROW_CHUNK, D))

    def conv_body(j, carry):
        r0 = pl.multiple_of(j * ROW_CHUNK, ROW_CHUNK)
        acc = xr_scr[pl.ds(r0, ROW_CHUNK), :] * taps[0]
        for k in range(1, CONV_A):
            acc = acc + xr_scr[pl.ds(r0 + k * S, ROW_CHUNK), :] * taps[k]
        xc_scr[pl.ds(r0, ROW_CHUNK), :] = acc + bias
        return carry

    lax.fori_loop(0, R // ROW_CHUNK, conv_body, 0)

    new_state = xr_scr[R:R + P, :]
    xr_scr[0:P, :] = new_state

    @pl.when(i == nblk - 1)
    def _():
        conv_out_ref[...] = new_state

    for b in range(N_GATE_BLOCKS):
        cols = slice(b * GATE_BLOCK_W, (b + 1) * GATE_BLOCK_W)
        res = jnp.dot(xc_scr[:, cols].astype(BF16), gw_ref[b], preferred_element_type=F32)
        rp_scr[:, cols] = res[:, 0:GATE_BLOCK_W]
        ip_scr[:, cols] = res[:, GATE_BLOCK_W:2 * GATE_BLOCK_W]

    neg_c_sp = jnp.broadcast_to(-RG_C * jax.nn.softplus(-lam_ref[...]), (SUBLANES, D))
    ga_b = jnp.broadcast_to(gab_ref[...], (SUBLANES, D))
    gx_b = jnp.broadcast_to(gxb_ref[...], (SUBLANES, D))

    def seq_body(c, carry):
        s0 = pl.multiple_of(c * SUBLANES, SUBLANES)

        def t_body(t, h):
            r = pl.ds(pl.multiple_of(t * S + s0, SUBLANES), SUBLANES)
            xc = xc_scr[r, :]
            rg = jax.nn.sigmoid(rp_scr[r, :] + ga_b)
            ig = jax.nn.sigmoid(ip_scr[r, :] + gx_b)
            log_a = neg_c_sp * rg
            a = jnp.exp(log_a)
            mult = jnp.sqrt(_one_minus_sq(a, log_a))
            h = a * h + mult * (ig * xc)
            hs_scr[r, :] = h * _gelu_tanh(gate_scr[r, :])
            return h

        h = lax.fori_loop(0, TB, t_body, h_scr[pl.ds(s0, SUBLANES), :])
        h_scr[pl.ds(s0, SUBLANES), :] = h
        return carry

    lax.fori_loop(0, S // SUBLANES, seq_body, 0)

    @pl.when(i == nblk - 1)
    def _():
        h_out_ref[...] = h_scr[...]

    y_scr[...] = jnp.dot(hs_scr[...].astype(BF16), wout_ref[...], preferred_element_type=F32)
    _residual_norm(x_ref, y_scr, npost_ref, o_ref, R)


def _const_spec(shape):
    zeros = (0,) * len(shape)
    return pl.BlockSpec(shape, lambda i: zeros, pipeline_mode=pl.Buffered(1))


def _rglru_layer(x, conv0, h0, npre, npost, w_in, conv_w, conv_b, gate_w, ga_b, gx_b, lam, w_out, *, S, TB):
    N, D = x.shape
    R = S * TB
    nblk = N // R
    P = (CONV_A - 1) * S
    row_spec = pl.BlockSpec((R, D), lambda i: (i, 0))
    kern = functools.partial(_rglru_kernel, S, TB, nblk)
    return pl.pallas_call(
        kern,
        grid=(nblk,),
        in_specs=[row_spec, _const_spec((P, D)), _const_spec((S, D)),
                  _const_spec((1, D)), _const_spec((1, D)), _const_spec((D, 2 * D)),
                  _const_spec((CONV_A, D)), _const_spec((1, D)),
                  _const_spec((N_GATE_BLOCKS, GATE_BLOCK_W, 2 * GATE_BLOCK_W)),
                  _const_spec((1, D)), _const_spec((1, D)), _const_spec((1, D)),
                  _const_spec((D, D))],
        out_specs=[row_spec, pl.BlockSpec((P, D), lambda i: (0, 0)), pl.BlockSpec((S, D), lambda i: (0, 0))],
        out_shape=[jax.ShapeDtypeStruct((N, D), F32), jax.ShapeDtypeStruct((P, D), F32),
                   jax.ShapeDtypeStruct((S, D), F32)],
        scratch_shapes=[pltpu.VMEM((R, D), BF16),
                        pltpu.VMEM((R, D), F32),
                        pltpu.VMEM((R + P, D), F32),
                        pltpu.VMEM((R, D), F32),
                        pltpu.VMEM((R, D), F32),
                        pltpu.VMEM((R, D), F32),
                        pltpu.VMEM((R, D), F32),
                        pltpu.VMEM((R, D), F32),
                        pltpu.VMEM((S, D), F32)],
        compiler_params=pltpu.CompilerParams(dimension_semantics=("arbitrary",),
                                             vmem_limit_bytes=VMEM_LIMIT_BYTES),
        name="rglru_mixer",
    )(x, conv0, h0, npre, npost, w_in, conv_w, conv_b, gate_w, ga_b, gx_b, lam, w_out)


def _sconv_kernel(S, TB, nblk,
                  x_ref, conv0_ref, npre_ref, npost_ref, win_ref, cw_ref, wout_ref,
                  o_ref, conv_out_ref,
                  xn_scr, bg_scr, cg_scr, v_scr, cv_scr, m_scr, y_scr):
    D = D_MODEL
    R = S * TB
    P = (CONV_B - 1) * S
    i = pl.program_id(0)

    @pl.when(i == 0)
    def _():
        cv_scr[0:P, :] = conv0_ref[...]

    _norm_to_bf16(x_ref, npre_ref, xn_scr, R)
    xn = xn_scr[...]
    bg_scr[...] = jnp.dot(xn, win_ref[:, 0:D], preferred_element_type=F32)
    cg_scr[...] = jnp.dot(xn, win_ref[:, D:2 * D], preferred_element_type=F32)
    v_scr[...] = jnp.dot(xn, win_ref[:, 2 * D:3 * D], preferred_element_type=F32)

    def cv_body(j, carry):
        r0 = pl.multiple_of(j * ROW_CHUNK, ROW_CHUNK)
        r = pl.ds(r0, ROW_CHUNK)
        cv_scr[pl.ds(r0 + P, ROW_CHUNK), :] = cg_scr[r, :] * v_scr[r, :]
        return carry

    lax.fori_loop(0, R // ROW_CHUNK, cv_body, 0)

    taps = [jnp.broadcast_to(cw_ref[k:k + 1, :], (ROW_CHUNK, D)) for k in range(CONV_B)]

    def conv_body(j, carry):
        r0 = pl.multiple_of(j * ROW_CHUNK, ROW_CHUNK)
        acc = cv_scr[pl.ds(r0, ROW_CHUNK), :] * taps[0]
        for k in range(1, CONV_B):
            acc = acc + cv_scr[pl.ds(r0 + k * S, ROW_CHUNK), :] * taps[k]
        m_scr[pl.ds(r0, ROW_CHUNK), :] = bg_scr[pl.ds(r0, ROW_CHUNK), :] * acc
        return carry

    lax.fori_loop(0, R // ROW_CHUNK, conv_body, 0)

    new_state = cv_scr[R:R + P, :]
    cv_scr[0:P, :] = new_state

    @pl.when(i == nblk - 1)
    def _():
        conv_out_ref[...] = new_state

    y_scr[...] = jnp.dot(m_scr[...].astype(BF16), wout_ref[...], preferred_element_type=F32)
    _residual_norm(x_ref, y_scr, npost_ref, o_ref, R)


def _sconv_layer(x, conv0, npre, npost, w_in, conv_w, w_out, *, S, TB):
    N, D = x.shape
    R = S * TB
    nblk = N // R
    P = (CONV_B - 1) * S
    row_spec = pl.BlockSpec((R, D), lambda i: (i, 0))
    kern = functools.partial(_sconv_kernel, S, TB, nblk)
    return pl.pallas_call(
        kern,
        grid=(nblk,),
        in_specs=[row_spec, _const_spec((P, D)), _const_spec((1, D)), _const_spec((1, D)),
                  _const_spec((D, 3 * D)), _const_spec((CONV_B, D)), _const_spec((D, D))],
        out_specs=[row_spec, pl.BlockSpec((P, D), lambda i: (0, 0))],
        out_shape=[jax.ShapeDtypeStruct((N, D), F32), jax.ShapeDtypeStruct((P, D), F32)],
        scratch_shapes=[pltpu.VMEM((R, D), BF16),
                        pltpu.VMEM((R, D), F32),
                        pltpu.VMEM((R, D), F32),
                        pltpu.VMEM((R, D), F32),
                        pltpu.VMEM((R + P, D), F32),
                        pltpu.VMEM((R, D), F32),
                        pltpu.VMEM((R, D), F32)],
        compiler_params=pltpu.CompilerParams(dimension_semantics=("arbitrary",),
                                             vmem_limit_bytes=VMEM_LIMIT_BYTES),
        name="sconv_mixer",
    )(x, conv0, npre, npost, w_in, conv_w, w_out)


def _ffn_kernel(R, x_ref, npre_ref, npost_ref, wg_ref, wu_ref, wd_ref, o_ref, xn_scr, act_scr, y_scr):
    _norm_to_bf16(x_ref, npre_ref, xn_scr, R)
    xn = xn_scr[...]
    for c0 in range(0, D_FF, FF_CHUNK):
        cols = slice(c0, c0 + FF_CHUNK)
        g = jnp.dot(xn, wg_ref[:, cols], preferred_element_type=F32)
        u = jnp.dot(xn, wu_ref[:, cols], preferred_element_type=F32)
        act_scr[:, cols] = (g * jax.nn.sigmoid(g) * u).astype(BF16)
    y_scr[...] = jnp.dot(act_scr[...], wd_ref[...], preferred_element_type=F32)
    _residual_norm(x_ref, y_scr, npost_ref, o_ref, R)


def _ffn_layer(x, npre, npost, wg, wu, wd, *, R):
    N, D = x.shape
    row_spec = pl.BlockSpec((R, D), lambda i: (i, 0))
    return pl.pallas_call(
        functools.partial(_ffn_kernel, R),
        grid=(N // R,),
        in_specs=[row_spec, _const_spec((1, D)), _const_spec((1, D)),
                  _const_spec((D, D_FF)), _const_spec((D, D_FF)), _const_spec((D_FF, D))],
        out_specs=row_spec,
        out_shape=jax.ShapeDtypeStruct((N, D), F32),
        scratch_shapes=[pltpu.VMEM((R, D), BF16), pltpu.VMEM((R, D_FF), BF16), pltpu.VMEM((R, D), F32)],
        compiler_params=pltpu.CompilerParams(dimension_semantics=("arbitrary",),
                                             vmem_limit_bytes=VMEM_LIMIT_BYTES),
        name="swiglu_ffn",
    )(x, npre, npost, wg, wu, wd)


def _to_time_major(a):
    S, K, D = a.shape
    return jnp.swapaxes(a, 0, 1).reshape(K * S, D)


def _from_time_major(a, S):
    KS, D = a.shape
    return jnp.swapaxes(a.reshape(KS // S, S, D), 0, 1)


def kernel(x_prompt, x_sample, state_rglru_conv, state_rglru_h, state_sconv, meta_tokens, norm_mix_pre, norm_mix_post, norm_ffn_pre, norm_ffn_post, rg_w_in, rg_conv_w, rg_conv_b, rg_gate_a_w, rg_gate_a_b, rg_gate_x_w, rg_gate_x_b, rg_lambda, rg_w_out, sc_w_in, sc_conv_w, sc_w_out, ffn_w_gate, ffn_w_up, ffn_w_down):
    D = D_MODEL
    depth = norm_mix_pre.shape[0]
    batch, seq, _ = x_prompt.shape
    dec_batch, dec_seq, _ = x_sample.shape
    row = lambda v: v.reshape(1, D)

    rg_w_in_b = rg_w_in.astype(BF16)
    rg_w_out_b = rg_w_out.astype(BF16)
    gate_w_b = jnp.concatenate([rg_gate_a_w, rg_gate_x_w], axis=-1).astype(BF16)
    sc_w_in_b = sc_w_in.astype(BF16)
    sc_w_out_b = sc_w_out.astype(BF16)
    wg_b = ffn_w_gate.astype(BF16)
    wu_b = ffn_w_up.astype(BF16)
    wd_b = ffn_w_down.astype(BF16)

    def run_trunk(x, rg_conv_in, rg_h_in, sc_in, *, S, TB, ffn_rows):
        rg_conv_out, rg_h_out, sc_out = [], [], []
        for i in range(depth):
            j = i // 2
            if i % 2 == 0:
                x, cb, hT = _rglru_layer(
                    x, rg_conv_in[j], rg_h_in[j], row(norm_mix_pre[i]), row(norm_mix_post[i]),
                    rg_w_in_b[j], rg_conv_w[j], row(rg_conv_b[j]), gate_w_b[j],
                    row(rg_gate_a_b[j]), row(rg_gate_x_b[j]), row(rg_lambda[j]), rg_w_out_b[j], S=S, TB=TB)
                rg_conv_out.append(cb)
                rg_h_out.append(hT)
            else:
                x, cb = _sconv_layer(x, sc_in[j], row(norm_mix_pre[i]), row(norm_mix_post[i]),
                                     sc_w_in_b[j], sc_conv_w[j], sc_w_out_b[j], S=S, TB=TB)
                sc_out.append(cb)
            x = _ffn_layer(x, row(norm_ffn_pre[i]), row(norm_ffn_post[i]), wg_b[i], wu_b[i], wd_b[i], R=ffn_rows)
        return x, rg_conv_out, rg_h_out, sc_out

    n_a = rg_w_in.shape[0]
    n_b = sc_w_in.shape[0]

    total = N_META + seq
    meta = jnp.broadcast_to(meta_tokens[:, None, :], (N_META, batch, D))
    xp = jnp.concatenate([meta, jnp.swapaxes(x_prompt, 0, 1)], axis=0).reshape(total * batch, D)
    zc_a = [jnp.zeros(((CONV_A - 1) * batch, D), F32)] * n_a
    zh_a = [jnp.zeros((batch, D), F32)] * n_a
    zc_b = [jnp.zeros(((CONV_B - 1) * batch, D), F32)] * n_b
    tb_p = 86
    yp, rg_conv_p, rg_h_p, sc_p = run_trunk(xp, zc_a, zh_a, zc_b, S=batch, TB=tb_p, ffn_rows=batch * tb_p)
    y_prompt = jnp.swapaxes(yp.reshape(total, batch, D)[N_META:], 0, 1)

    xs = _to_time_major(x_sample)
    sc_a = [_to_time_major(state_rglru_conv[j]) for j in range(n_a)]
    sh_a = [state_rglru_h[j] for j in range(n_a)]
    sc_b = [_to_time_major(state_sconv[j]) for j in range(n_b)]
    ys, rg_conv_s, rg_h_s, sc_s = run_trunk(xs, sc_a, sh_a, sc_b, S=dec_batch, TB=dec_seq,
                                            ffn_rows=dec_batch * dec_seq)
    y_sample = _from_time_major(ys, dec_batch)

    return (y_prompt, y_sample,
            jnp.stack([_from_time_major(c, batch) for c in rg_conv_p]), jnp.stack(rg_h_p),
            jnp.stack([_from_time_major(c, batch) for c in sc_p]),
            jnp.stack([_from_time_major(c, dec_batch) for c in rg_conv_s]), jnp.stack(rg_h_s),
            jnp.stack([_from_time_major(c, dec_batch) for c in sc_s]))
```

```python
import functools

import jax
import jax.numpy as jnp
from jax import lax
from jax.experimental import pallas as pl
from jax.experimental.pallas import tpu as pltpu

D_MODEL = 1024
D_FF = 2816
N_META = 16
N_GATE_BLOCKS = 4
GATE_BLOCK_W = 256
CONV_A = 4
CONV_B = 3
RG_C = 8.0
EPS = 1e-6

ROW_CHUNK = 16
SUBLANES = 8
FF_CHUNK = 256
VMEM_LIMIT_BYTES = 56 * 1024 * 1024

F32 = jnp.float32
BF16 = jnp.bfloat16


def _rms(x, w):
    ms = jnp.mean(x * x, axis=-1, keepdims=True)
    return x * lax.rsqrt(ms + EPS) * w


def _norm_to_bf16(x_ref, w_ref, out_scr):
    out_scr[...] = _rms(x_ref[...], w_ref[...]).astype(BF16)


def _residual_norm(x_ref, y, w_ref, o_ref):
    o_ref[...] = x_ref[...] + _rms(y, w_ref[...])


def _gelu_tanh(x):
    c = 0.7978845608028654
    return x * (0.5 * (1.0 + jnp.tanh(c * (x + 0.044715 * (x * x * x)))))


def _one_minus_sq(a, log_a):
    a2 = a * a
    return jnp.where(log_a < -0.25, 1.0 - a2, -jnp.tanh(log_a) * (a2 + 1.0))


def _rglru_kernel(S, TB, nblk,
                  x_ref, conv0_ref, h0_ref, npre_ref, npost_ref, win_ref, cw_ref, cb_ref,
                  gw_ref, gab_ref, gxb_ref, lam_ref, wout_ref,
                  o_ref, conv_out_ref, h_out_ref,
                  xn_scr, gate_scr, xr_scr, xc_scr, rp_scr, ip_scr, hs_scr, h_scr):
    D = D_MODEL
    R = S * TB
    P = (CONV_A - 1) * S
    i = pl.program_id(0)

    @pl.when(i == 0)
    def _():
        xr_scr[0:P, :] = conv0_ref[...]
        h_scr[...] = h0_ref[...]

    _norm_to_bf16(x_ref, npre_ref, xn_scr)
    xn = xn_scr[...]
    gate_scr[...] = jnp.dot(xn, win_ref[:, 0:D], preferred_element_type=F32)
    xr_scr[P:P + R, :] = jnp.dot(xn, win_ref[:, D:2 * D], preferred_element_type=F32)

    taps = [jnp.broadcast_to(cw_ref[k:k + 1, :], (ROW_CHUNK, D)) for k in range(CONV_A)]
    bias = jnp.broadcast_to(cb_ref[...], (ROW_CHUNK, D))

    def conv_body(j, carry):
        r0 = pl.multiple_of(j * ROW_CHUNK, ROW_CHUNK)
        acc = xr_scr[pl.ds(r0, ROW_CHUNK), :] * taps[0]
        for k in range(1, CONV_A):
            acc = acc + xr_scr[pl.ds(r0 + k * S, ROW_CHUNK), :] * taps[k]
        xc_scr[pl.ds(r0, ROW_CHUNK), :] = acc + bias
        return carry

    lax.fori_loop(0, R // ROW_CHUNK, conv_body, 0)

    new_state = xr_scr[R:R + P, :]
    xr_scr[0:P, :] = new_state

    @pl.when(i == nblk - 1)
    def _():
        conv_out_ref[...] = new_state

    for b in range(N_GATE_BLOCKS):
        cols = slice(b * GATE_BLOCK_W, (b + 1) * GATE_BLOCK_W)
        res = jnp.dot(xc_scr[:, cols].astype(BF16), gw_ref[b], preferred_element_type=F32)
        rp_scr[:, cols] = res[:, 0:GATE_BLOCK_W]
        ip_scr[:, cols] = res[:, GATE_BLOCK_W:2 * GATE_BLOCK_W]

    neg_c_sp = jnp.broadcast_to(-RG_C * jax.nn.softplus(-lam_ref[...]), (SUBLANES, D))
    ga_b = jnp.broadcast_to(gab_ref[...], (SUBLANES, D))
    gx_b = jnp.broadcast_to(gxb_ref[...], (SUBLANES, D))

    def seq_body(c, carry):
        s0 = pl.multiple_of(c * SUBLANES, SUBLANES)

        def t_body(t, h):
            r = pl.ds(pl.multiple_of(t * S + s0, SUBLANES), SUBLANES)
            xc = xc_scr[r, :]
            rg = jax.nn.sigmoid(rp_scr[r, :] + ga_b)
            ig = jax.nn.sigmoid(ip_scr[r, :] + gx_b)
            log_a = neg_c_sp * rg
            a = jnp.exp(log_a)
            mult = jnp.sqrt(_one_minus_sq(a, log_a))
            h = a * h + mult * (ig * xc)
            hs_scr[r, :] = h * _gelu_tanh(gate_scr[r, :])
            return h

        h = lax.fori_loop(0, TB, t_body, h_scr[pl.ds(s0, SUBLANES), :])
        h_scr[pl.ds(s0, SUBLANES), :] = h
        return carry

    lax.fori_loop(0, S // SUBLANES, seq_body, 0)

    @pl.when(i == nblk - 1)
    def _():
        h_out_ref[...] = h_scr[...]

    y = jnp.dot(hs_scr[...].astype(BF16), wout_ref[...], preferred_element_type=F32)
    _residual_norm(x_ref, y, npost_ref, o_ref)


def _const_spec(shape):
    zeros = (0,) * len(shape)
    return pl.BlockSpec(shape, lambda i: zeros, pipeline_mode=pl.Buffered(1))


def _rglru_layer(x, conv0, h0, npre, npost, w_in, conv_w, conv_b, gate_w, ga_b, gx_b, lam, w_out, *, S, TB):
    N, D = x.shape
    R = S * TB
    nblk = N // R
    P = (CONV_A - 1) * S
    row_spec = pl.BlockSpec((R, D), lambda i: (i, 0))
    kern = functools.partial(_rglru_kernel, S, TB, nblk)
    return pl.pallas_call(
        kern,
        grid=(nblk,),
        in_specs=[row_spec, _const_spec((P, D)), _const_spec((S, D)),
                  _const_spec((1, D)), _const_spec((1, D)), _const_spec((D, 2 * D)),
                  _const_spec((CONV_A, D)), _const_spec((1, D)),
                  _const_spec((N_GATE_BLOCKS, GATE_BLOCK_W, 2 * GATE_BLOCK_W)),
                  _const_spec((1, D)), _const_spec((1, D)), _const_spec((1, D)),
                  _const_spec((D, D))],
        out_specs=[row_spec, pl.BlockSpec((P, D), lambda i: (0, 0)), pl.BlockSpec((S, D), lambda i: (0, 0))],
        out_shape=[jax.ShapeDtypeStruct((N, D), F32), jax.ShapeDtypeStruct((P, D), F32),
                   jax.ShapeDtypeStruct((S, D), F32)],
        scratch_shapes=[pltpu.VMEM((R, D), BF16),
                        pltpu.VMEM((R, D), F32),
                        pltpu.VMEM((R + P, D), F32),
                        pltpu.VMEM((R, D), F32),
                        pltpu.VMEM((R, D), F32),
                        pltpu.VMEM((R, D), F32),
                        pltpu.VMEM((R, D), F32),
                        pltpu.VMEM((S, D), F32)],
        compiler_params=pltpu.CompilerParams(dimension_semantics=("arbitrary",),
                                             vmem_limit_bytes=VMEM_LIMIT_BYTES),
        name="rglru_mixer",
    )(x, conv0, h0, npre, npost, w_in, conv_w, conv_b, gate_w, ga_b, gx_b, lam, w_out)


def _sconv_kernel(S, TB, nblk,
                  x_ref, conv0_ref, npre_ref, npost_ref, win_ref, cw_ref, wout_ref,
                  o_ref, conv_out_ref,
                  xn_scr, bg_scr, cg_scr, v_scr, cv_scr, m_scr):
    D = D_MODEL
    R = S * TB
    P = (CONV_B - 1) * S
    i = pl.program_id(0)

    @pl.when(i == 0)
    def _():
        cv_scr[0:P, :] = conv0_ref[...]

    _norm_to_bf16(x_ref, npre_ref, xn_scr)
    xn = xn_scr[...]
    bg_scr[...] = jnp.dot(xn, win_ref[:, 0:D], preferred_element_type=F32)
    cg_scr[...] = jnp.dot(xn, win_ref[:, D:2 * D], preferred_element_type=F32)
    v_scr[...] = jnp.dot(xn, win_ref[:, 2 * D:3 * D], preferred_element_type=F32)

    def cv_body(j, carry):
        r0 = pl.multiple_of(j * ROW_CHUNK, ROW_CHUNK)
        r = pl.ds(r0, ROW_CHUNK)
        cv_scr[pl.ds(r0 + P, ROW_CHUNK), :] = cg_scr[r, :] * v_scr[r, :]
        return carry

    lax.fori_loop(0, R // ROW_CHUNK, cv_body, 0)

    taps = [jnp.broadcast_to(cw_ref[k:k + 1, :], (ROW_CHUNK, D)) for k in range(CONV_B)]

    def conv_body(j, carry):
        r0 = pl.multiple_of(j * ROW_CHUNK, ROW_CHUNK)
        acc = cv_scr[pl.ds(r0, ROW_CHUNK), :] * taps[0]
        for k in range(1, CONV_B):
            acc = acc + cv_scr[pl.ds(r0 + k * S, ROW_CHUNK), :] * taps[k]
        m_scr[pl.ds(r0, ROW_CHUNK), :] = bg_scr[pl.ds(r0, ROW_CHUNK), :] * acc
        return carry

    lax.fori_loop(0, R // ROW_CHUNK, conv_body, 0)

    new_state = cv_scr[R:R + P, :]
    cv_scr[0:P, :] = new_state

    @pl.when(i == nblk - 1)
    def _():
        conv_out_ref[...] = new_state

    y = jnp.dot(m_scr[...].astype(BF16), wout_ref[...], preferred_element_type=F32)
    _residual_norm(x_ref, y, npost_ref, o_ref)


def _sconv_layer(x, conv0, npre, npost, w_in, conv_w, w_out, *, S, TB):
    N, D = x.shape
    R = S * TB
    nblk = N // R
    P = (CONV_B - 1) * S
    row_spec = pl.BlockSpec((R, D), lambda i: (i, 0))
    kern = functools.partial(_sconv_kernel, S, TB, nblk)
    return pl.pallas_call(
        kern,
        grid=(nblk,),
        in_specs=[row_spec, _const_spec((P, D)), _const_spec((1, D)), _const_spec((1, D)),
                  _const_spec((D, 3 * D)), _const_spec((CONV_B, D)), _const_spec((D, D))],
        out_specs=[row_spec, pl.BlockSpec((P, D), lambda i: (0, 0))],
        out_shape=[jax.ShapeDtypeStruct((N, D), F32), jax.ShapeDtypeStruct((P, D), F32)],
        scratch_shapes=[pltpu.VMEM((R, D), BF16),
                        pltpu.VMEM((R, D), F32),
                        pltpu.VMEM((R, D), F32),
                        pltpu.VMEM((R, D), F32),
                        pltpu.VMEM((R + P, D), F32),
                        pltpu.VMEM((R, D), F32)],
        compiler_params=pltpu.CompilerParams(dimension_semantics=("arbitrary",),
                                             vmem_limit_bytes=VMEM_LIMIT_BYTES),
        name="sconv_mixer",
    )(x, conv0, npre, npost, w_in, conv_w, w_out)


def _ffn_kernel(x_ref, npre_ref, npost_ref, wg_ref, wu_ref, wd_ref, o_ref, xn_scr, act_scr):
    _norm_to_bf16(x_ref, npre_ref, xn_scr)
    xn = xn_scr[...]
    for c0 in range(0, D_FF, FF_CHUNK):
        cols = slice(c0, c0 + FF_CHUNK)
        g = jnp.dot(xn, wg_ref[:, cols], preferred_element_type=F32)
        u = jnp.dot(xn, wu_ref[:, cols], preferred_element_type=F32)
        act_scr[:, cols] = (g * jax.nn.sigmoid(g) * u).astype(BF16)
    y = jnp.dot(act_scr[...], wd_ref[...], preferred_element_type=F32)
    _residual_norm(x_ref, y, npost_ref, o_ref)


def _ffn_layer(x, npre, npost, wg, wu, wd, *, R):
    N, D = x.shape
    row_spec = pl.BlockSpec((R, D), lambda i: (i, 0))
    return pl.pallas_call(
        _ffn_kernel,
        grid=(N // R,),
        in_specs=[row_spec, _const_spec((1, D)), _const_spec((1, D)),
                  _const_spec((D, D_FF)), _const_spec((D, D_FF)), _const_spec((D_FF, D))],
        out_specs=row_spec,
        out_shape=jax.ShapeDtypeStruct((N, D), F32),
        scratch_shapes=[pltpu.VMEM((R, D), BF16), pltpu.VMEM((R, D_FF), BF16)],
        compiler_params=pltpu.CompilerParams(dimension_semantics=("arbitrary",),
                                             vmem_limit_bytes=VMEM_LIMIT_BYTES),
        name="swiglu_ffn",
    )(x, npre, npost, wg, wu, wd)


def _to_time_major(a):
    S, K, D = a.shape
    return jnp.swapaxes(a, 0, 1).reshape(K * S, D)


def _from_time_major(a, S):
    KS, D = a.shape
    return jnp.swapaxes(a.reshape(KS // S, S, D), 0, 1)


def kernel(x_prompt, x_sample, state_rglru_conv, state_rglru_h, state_sconv, meta_tokens, norm_mix_pre, norm_mix_post, norm_ffn_pre, norm_ffn_post, rg_w_in, rg_conv_w, rg_conv_b, rg_gate_a_w, rg_gate_a_b, rg_gate_x_w, rg_gate_x_b, rg_lambda, rg_w_out, sc_w_in, sc_conv_w, sc_w_out, ffn_w_gate, ffn_w_up, ffn_w_down):
    D = D_MODEL
    depth = norm_mix_pre.shape[0]
    batch, seq, _ = x_prompt.shape
    dec_batch, dec_seq, _ = x_sample.shape
    row = lambda v: v.reshape(1, D)

    rg_w_in_b = rg_w_in.astype(BF16)
    rg_w_out_b = rg_w_out.astype(BF16)
    gate_w_b = jnp.concatenate([rg_gate_a_w, rg_gate_x_w], axis=-1).astype(BF16)
    sc_w_in_b = sc_w_in.astype(BF16)
    sc_w_out_b = sc_w_out.astype(BF16)
    wg_b = ffn_w_gate.astype(BF16)
    wu_b = ffn_w_up.astype(BF16)
    wd_b = ffn_w_down.astype(BF16)

    def run_trunk(x, rg_conv_in, rg_h_in, sc_in, *, S, TB, ffn_rows):
        rg_conv_out, rg_h_out, sc_out = [], [], []
        for i in range(depth):
            j = i // 2
            if i % 2 == 0:
                x, cb, hT = _rglru_layer(
                    x, rg_conv_in[j], rg_h_in[j], row(norm_mix_pre[i]), row(norm_mix_post[i]),
                    rg_w_in_b[j], rg_conv_w[j], row(rg_conv_b[j]), gate_w_b[j],
                    row(rg_gate_a_b[j]), row(rg_gate_x_b[j]), row(rg_lambda[j]), rg_w_out_b[j], S=S, TB=TB)
                rg_conv_out.append(cb)
                rg_h_out.append(hT)
            else:
                x, cb = _sconv_layer(x, sc_in[j], row(norm_mix_pre[i]), row(norm_mix_post[i]),
                                     sc_w_in_b[j], sc_conv_w[j], sc_w_out_b[j], S=S, TB=TB)
                sc_out.append(cb)
            x = _ffn_layer(x, row(norm_ffn_pre[i]), row(norm_ffn_post[i]), wg_b[i], wu_b[i], wd_b[i], R=ffn_rows)
        return x, rg_conv_out, rg_h_out, sc_out

    n_a = rg_w_in.shape[0]
    n_b = sc_w_in.shape[0]

    total = N_META + seq
    meta = jnp.broadcast_to(meta_tokens[:, None, :], (N_META, batch, D))
    xp = jnp.concatenate([meta, jnp.swapaxes(x_prompt, 0, 1)], axis=0).reshape(total * batch, D)
    zc_a = [jnp.zeros(((CONV_A - 1) * batch, D), F32)] * n_a
    zh_a = [jnp.zeros((batch, D), F32)] * n_a
    zc_b = [jnp.zeros(((CONV_B - 1) * batch, D), F32)] * n_b
    tb_p = 86
    yp, rg_conv_p, rg_h_p, sc_p = run_trunk(xp, zc_a, zh_a, zc_b, S=batch, TB=tb_p, ffn_rows=batch * tb_p)
    y_prompt = jnp.swapaxes(yp.reshape(total, batch, D)[N_META:], 0, 1)

    xs = _to_time_major(x_sample)
    sc_a = [_to_time_major(state_rglru_conv[j]) for j in range(n_a)]
    sh_a = [state_rglru_h[j] for j in range(n_a)]
    sc_b = [_to_time_major(state_sconv[j]) for j in range(n_b)]
    ys, rg_conv_s, rg_h_s, sc_s = run_trunk(xs, sc_a, sh_a, sc_b, S=dec_batch, TB=dec_seq,
                                            ffn_rows=dec_batch * dec_seq)
    y_sample = _from_time_major(ys, dec_batch)

    return (y_prompt, y_sample,
            jnp.stack([_from_time_major(c, batch) for c in rg_conv_p]), jnp.stack(rg_h_p),
            jnp.stack([_from_time_major(c, batch) for c in sc_p]),
            jnp.stack([_from_time_major(c, dec_batch) for c in rg_conv_s]), jnp.stack(rg_h_s),
            jnp.stack([_from_time_major(c, dec_batch) for c in sc_s]))
```

```python
import functools

import jax
import jax.numpy as jnp
from jax import lax
from jax.experimental import pallas as pl
from jax.experimental.pallas import tpu as pltpu

D_MODEL = 1024
D_FF = 2816
N_META = 16
N_GATE_BLOCKS = 4
GATE_BLOCK_W = 256
CONV_A = 4
CONV_B = 3
RG_C = 8.0
EPS = 1e-6

SUBLANES = 8
FF_CHUNK = 256
SCONV_CHUNK = 256
VMEM_LIMIT_BYTES = 56 * 1024 * 1024

F32 = jnp.float32
BF16 = jnp.bfloat16


def _rms(x, w):
    ms = jnp.mean(x * x, axis=-1, keepdims=True)
    return x * lax.rsqrt(ms + EPS) * w


def _norm_to_bf16(x_ref, w_ref, out_scr):
    out_scr[...] = _rms(x_ref[...], w_ref[...]).astype(BF16)


def _residual_norm(x_ref, y, w_ref, o_ref):
    o_ref[...] = x_ref[...] + _rms(y, w_ref[...])


def _gelu_tanh(x):
    c = 0.7978845608028654
    hx = 0.5 * x
    return hx + hx * jnp.tanh(x * (c + (c * 0.044715) * (x * x)))


def _rglru_kernel(S, TB, nblk,
                  x_ref, conv0_ref, h0_ref, npre_ref, npost_ref, win_ref, cw_ref, cb_ref,
                  gw_ref, gab_ref, gxb_ref, lam_ref, wout_ref,
                  o_ref, conv_out_ref, h_out_ref,
                  xn_scr, xr_scr, a_scr, uh_scr, g_scr, h_scr):
    D = D_MODEL
    R = S * TB
    P = (CONV_A - 1) * S
    i = pl.program_id(0)

    @pl.when(i == 0)
    def _():
        xr_scr[0:P, :] = conv0_ref[...]
        h_scr[...] = h0_ref[...]

    _norm_to_bf16(x_ref, npre_ref, xn_scr)
    xn = xn_scr[...]

    def recurrent_branch_in(b):
        cols = slice(b * GATE_BLOCK_W, (b + 1) * GATE_BLOCK_W)
        xr_cols = slice(D + b * GATE_BLOCK_W, D + (b + 1) * GATE_BLOCK_W)
        xr_scr[P:P + R, cols] = jnp.dot(xn, win_ref[:, xr_cols], preferred_element_type=F32)

    recurrent_branch_in(0)
    for b in range(N_GATE_BLOCKS):
        cols = slice(b * GATE_BLOCK_W, (b + 1) * GATE_BLOCK_W)
        if b + 1 < N_GATE_BLOCKS:
            recurrent_branch_in(b + 1)

        xc = xr_scr[0:R, cols] * cw_ref[0:1, cols]
        for k in range(1, CONV_A):
            xc = xc + xr_scr[k * S:k * S + R, cols] * cw_ref[k:k + 1, cols]
        xc = xc + cb_ref[:, cols]

        res = jnp.dot(xc.astype(BF16), gw_ref[b], preferred_element_type=F32)
        half_c_sp = (-0.5 * RG_C) * jax.nn.softplus(-lam_ref[:, cols])
        tr = jnp.tanh(res[:, 0:GATE_BLOCK_W] + gab_ref[:, cols])
        log_a = half_c_sp * tr + half_c_sp
        ig = 0.5 * jnp.tanh(res[:, GATE_BLOCK_W:2 * GATE_BLOCK_W] + gxb_ref[:, cols]) + 0.5
        a = jnp.exp(log_a)
        m2 = jnp.tanh(log_a) * (-1.0 - a * a)
        mult = jnp.where(m2 > 0.0, m2 * lax.rsqrt(m2), 0.0)
        a_scr[:, cols] = a
        uh_scr[:, cols] = mult * (ig * xc)

        gate = jnp.dot(xn, win_ref[:, cols], preferred_element_type=F32)
        g_scr[:, cols] = _gelu_tanh(gate)

    new_state = xr_scr[R:R + P, :]
    xr_scr[0:P, :] = new_state

    @pl.when(i == nblk - 1)
    def _():
        conv_out_ref[...] = new_state

    def seq_body(c, carry):
        s0 = pl.multiple_of(c * SUBLANES, SUBLANES)

        def t_body(t, h):
            r = pl.ds(pl.multiple_of(t * S + s0, SUBLANES), SUBLANES)
            h = a_scr[r, :] * h + uh_scr[r, :]
            uh_scr[r, :] = h * g_scr[r, :]
            return h

        h = lax.fori_loop(0, TB, t_body, h_scr[pl.ds(s0, SUBLANES), :], unroll=2)
        h_scr[pl.ds(s0, SUBLANES), :] = h
        return carry

    lax.fori_loop(0, S // SUBLANES, seq_body, 0)

    @pl.when(i == nblk - 1)
    def _():
        h_out_ref[...] = h_scr[...]

    y = jnp.dot(uh_scr[...].astype(BF16), wout_ref[...], preferred_element_type=F32)
    _residual_norm(x_ref, y, npost_ref, o_ref)


def _const_spec(shape):
    zeros = (0,) * len(shape)
    return pl.BlockSpec(shape, lambda i: zeros, pipeline_mode=pl.Buffered(1))


def _rglru_layer(x, conv0, h0, npre, npost, w_in, conv_w, conv_b, gate_w, ga_b, gx_b, lam, w_out, *, S, TB):
    N, D = x.shape
    R = S * TB
    nblk = N // R
    P = (CONV_A - 1) * S
    row_spec = pl.BlockSpec((R, D), lambda i: (i, 0))
    kern = functools.partial(_rglru_kernel, S, TB, nblk)
    return pl.pallas_call(
        kern,
        grid=(nblk,),
        in_specs=[row_spec, _const_spec((P, D)), _const_spec((S, D)),
                  _const_spec((1, D)), _const_spec((1, D)), _const_spec((D, 2 * D)),
                  _const_spec((CONV_A, D)), _const_spec((1, D)),
                  _const_spec((N_GATE_BLOCKS, GATE_BLOCK_W, 2 * GATE_BLOCK_W)),
                  _const_spec((1, D)), _const_spec((1, D)), _const_spec((1, D)),
                  _const_spec((D, D))],
        out_specs=[row_spec, pl.BlockSpec((P, D), lambda i: (0, 0)), pl.BlockSpec((S, D), lambda i: (0, 0))],
        out_shape=[jax.ShapeDtypeStruct((N, D), F32), jax.ShapeDtypeStruct((P, D), F32),
                   jax.ShapeDtypeStruct((S, D), F32)],
        scratch_shapes=[pltpu.VMEM((R, D), BF16),
                        pltpu.VMEM((R + P, D), F32),
                        pltpu.VMEM((R, D), F32),
                        pltpu.VMEM((R, D), F32),
                        pltpu.VMEM((R, D), F32),
                        pltpu.VMEM((S, D), F32)],
        compiler_params=pltpu.CompilerParams(dimension_semantics=("arbitrary",),
                                             vmem_limit_bytes=VMEM_LIMIT_BYTES),
        name="rglru_mixer",
    )(x, conv0, h0, npre, npost, w_in, conv_w, conv_b, gate_w, ga_b, gx_b, lam, w_out)


def _sconv_kernel(S, TB, nblk,
                  x_ref, conv0_ref, npre_ref, npost_ref, win_ref, cw_ref, wout_ref,
                  o_ref, conv_out_ref,
                  xn_scr, cv_scr, m_scr):
    D = D_MODEL
    R = S * TB
    P = (CONV_B - 1) * S
    i = pl.program_id(0)

    @pl.when(i == 0)
    def _():
        cv_scr[0:P, :] = conv0_ref[...]

    _norm_to_bf16(x_ref, npre_ref, xn_scr)
    xn = xn_scr[...]

    for c0 in range(0, D, SCONV_CHUNK):
        cols = slice(c0, c0 + SCONV_CHUNK)
        cg = jnp.dot(xn, win_ref[:, D + c0:D + c0 + SCONV_CHUNK], preferred_element_type=F32)
        v = jnp.dot(xn, win_ref[:, 2 * D + c0:2 * D + c0 + SCONV_CHUNK], preferred_element_type=F32)
        cv_scr[P:P + R, cols] = cg * v
        conv = cv_scr[0:R, cols] * cw_ref[0:1, cols]
        for k in range(1, CONV_B):
            conv = conv + cv_scr[k * S:k * S + R, cols] * cw_ref[k:k + 1, cols]
        bg = jnp.dot(xn, win_ref[:, cols], preferred_element_type=F32)
        m_scr[:, cols] = (bg * conv).astype(BF16)

    new_state = cv_scr[R:R + P, :]
    cv_scr[0:P, :] = new_state

    @pl.when(i == nblk - 1)
    def _():
        conv_out_ref[...] = new_state

    y = jnp.dot(m_scr[...], wout_ref[...], preferred_element_type=F32)
    _residual_norm(x_ref, y, npost_ref, o_ref)


def _sconv_layer(x, conv0, npre, npost, w_in, conv_w, w_out, *, S, TB):
    N, D = x.shape
    R = S * TB
    nblk = N // R
    P = (CONV_B - 1) * S
    row_spec = pl.BlockSpec((R, D), lambda i: (i, 0))
    kern = functools.partial(_sconv_kernel, S, TB, nblk)
    return pl.pallas_call(
        kern,
        grid=(nblk,),
        in_specs=[row_spec, _const_spec((P, D)), _const_spec((1, D)), _const_spec((1, D)),
                  _const_spec((D, 3 * D)), _const_spec((CONV_B, D)), _const_spec((D, D))],
        out_specs=[row_spec, pl.BlockSpec((P, D), lambda i: (0, 0))],
        out_shape=[jax.ShapeDtypeStruct((N, D), F32), jax.ShapeDtypeStruct((P, D), F32)],
        scratch_shapes=[pltpu.VMEM((R, D), BF16),
                        pltpu.VMEM((R + P, D), F32),
                        pltpu.VMEM((R, D), BF16)],
        compiler_params=pltpu.CompilerParams(dimension_semantics=("arbitrary",),
                                             vmem_limit_bytes=VMEM_LIMIT_BYTES),
        name="sconv_mixer",
    )(x, conv0, npre, npost, w_in, conv_w, w_out)


def _ffn_kernel(x_ref, npre_ref, npost_ref, wg_ref, wu_ref, wd_ref, o_ref, xn_scr, act_scr):
    _norm_to_bf16(x_ref, npre_ref, xn_scr)
    xn = xn_scr[...]
    for c0 in range(0, D_FF, FF_CHUNK):
        cols = slice(c0, c0 + FF_CHUNK)
        g = jnp.dot(xn, wg_ref[:, cols], preferred_element_type=F32)
        u = jnp.dot(xn, wu_ref[:, cols], preferred_element_type=F32)
        act_scr[:, cols] = (g * jax.nn.sigmoid(g) * u).astype(BF16)
    y = jnp.dot(act_scr[...], wd_ref[...], preferred_element_type=F32)
    _residual_norm(x_ref, y, npost_ref, o_ref)


def _ffn_layer(x, npre, npost, wg, wu, wd, *, R):
    N, D = x.shape
    row_spec = pl.BlockSpec((R, D), lambda i: (i, 0))
    return pl.pallas_call(
        _ffn_kernel,
        grid=(N // R,),
        in_specs=[row_spec, _const_spec((1, D)), _const_spec((1, D)),
                  _const_spec((D, D_FF)), _const_spec((D, D_FF)), _const_spec((D_FF, D))],
        out_specs=row_spec,
        out_shape=jax.ShapeDtypeStruct((N, D), F32),
        scratch_shapes=[pltpu.VMEM((R, D), BF16), pltpu.VMEM((R, D_FF), BF16)],
        compiler_params=pltpu.CompilerParams(dimension_semantics=("arbitrary",),
                                             vmem_limit_bytes=VMEM_LIMIT_BYTES),
        name="swiglu_ffn",
    )(x, npre, npost, wg, wu, wd)


def _to_time_major(a):
    S, K, D = a.shape
    return jnp.swapaxes(a, 0, 1).reshape(K * S, D)


def _from_time_major(a, S):
    KS, D = a.shape
    return jnp.swapaxes(a.reshape(KS // S, S, D), 0, 1)


def kernel(x_prompt, x_sample, state_rglru_conv, state_rglru_h, state_sconv, meta_tokens, norm_mix_pre, norm_mix_post, norm_ffn_pre, norm_ffn_post, rg_w_in, rg_conv_w, rg_conv_b, rg_gate_a_w, rg_gate_a_b, rg_gate_x_w, rg_gate_x_b, rg_lambda, rg_w_out, sc_w_in, sc_conv_w, sc_w_out, ffn_w_gate, ffn_w_up, ffn_w_down):
    D = D_MODEL
    depth = norm_mix_pre.shape[0]
    batch, seq, _ = x_prompt.shape
    dec_batch, dec_seq, _ = x_sample.shape
    row = lambda v: v.reshape(1, D)

    rg_w_in_b = rg_w_in.astype(BF16)
    rg_w_out_b = rg_w_out.astype(BF16)
    gate_w_b = (0.5 * jnp.concatenate([rg_gate_a_w, rg_gate_x_w], axis=-1)).astype(BF16)
    sc_w_in_b = sc_w_in.astype(BF16)
    sc_w_out_b = sc_w_out.astype(BF16)
    wg_b = ffn_w_gate.astype(BF16)
    wu_b = ffn_w_up.astype(BF16)
    wd_b = ffn_w_down.astype(BF16)

    def run_trunk(x, rg_conv_in, rg_h_in, sc_in, *, S, TB, ffn_rows):
        rg_conv_out, rg_h_out, sc_out = [], [], []
        for i in range(depth):
            j = i // 2
            if i % 2 == 0:
                x, cb, hT = _rglru_layer(
                    x, rg_conv_in[j], rg_h_in[j], row(norm_mix_pre[i]), row(norm_mix_post[i]),
                    rg_w_in_b[j], rg_conv_w[j], row(rg_conv_b[j]), gate_w_b[j],
                    row(0.5 * rg_gate_a_b[j]), row(0.5 * rg_gate_x_b[j]), row(rg_lambda[j]), rg_w_out_b[j],
                    S=S, TB=TB)
                rg_conv_out.append(cb)
                rg_h_out.append(hT)
            else:
                x, cb = _sconv_layer(x, sc_in[j], row(norm_mix_pre[i]), row(norm_mix_post[i]),
                                     sc_w_in_b[j], sc_conv_w[j], sc_w_out_b[j], S=S, TB=TB)
                sc_out.append(cb)
            x = _ffn_layer(x, row(norm_ffn_pre[i]), row(norm_ffn_post[i]), wg_b[i], wu_b[i], wd_b[i], R=ffn_rows)
        return x, rg_conv_out, rg_h_out, sc_out

    n_a = rg_w_in.shape[0]
    n_b = sc_w_in.shape[0]

    total = N_META + seq
    meta = jnp.broadcast_to(meta_tokens[:, None, :], (N_META, batch, D))
    xp = jnp.concatenate([meta, jnp.swapaxes(x_prompt, 0, 1)], axis=0).reshape(total * batch, D)
    zc_a = [jnp.zeros(((CONV_A - 1) * batch, D), F32)] * n_a
    zh_a = [jnp.zeros((batch, D), F32)] * n_a
    zc_b = [jnp.zeros(((CONV_B - 1) * batch, D), F32)] * n_b
    tb_p = 86
    yp, rg_conv_p, rg_h_p, sc_p = run_trunk(xp, zc_a, zh_a, zc_b, S=batch, TB=tb_p, ffn_rows=batch * tb_p)
    y_prompt = jnp.swapaxes(yp.reshape(total, batch, D)[N_META:], 0, 1)

    xs = _to_time_major(x_sample)
    sc_a = [_to_time_major(state_rglru_conv[j]) for j in range(n_a)]
    sh_a = [state_rglru_h[j] for j in range(n_a)]
    sc_b = [_to_time_major(state_sconv[j]) for j in range(n_b)]
    ys, rg_conv_s, rg_h_s, sc_s = run_trunk(xs, sc_a, sh_a, sc_b, S=dec_batch, TB=dec_seq,
                                            ffn_rows=dec_batch * dec_seq)
    y_sample = _from_time_major(ys, dec_batch)

    return (y_prompt, y_sample,
            jnp.stack([_from_time_major(c, batch) for c in rg_conv_p]), jnp.stack(rg_h_p),
            jnp.stack([_from_time_major(c, batch) for c in sc_p]),
            jnp.stack([_from_time_major(c, dec_batch) for c in rg_conv_s]), jnp.stack(rg_h_s),
            jnp.stack([_from_time_major(c, dec_batch) for c in sc_s]))
```

```python
import functools

import jax
import jax.numpy as jnp
from jax import lax
from jax.experimental import pallas as pl
from jax.experimental.pallas import tpu as pltpu

D_MODEL = 1024
D_FF = 2816
N_META = 16
N_GATE_BLOCKS = 4
GATE_BLOCK_W = 256
CONV_A = 4
CONV_B = 3
RG_C = 8.0
EPS = 1e-6

SUBLANES = 8
LANES = 128
FF_CHUNK = 256
SCONV_CHUNK = 256
PROMPT_TB = 128
VMEM_LIMIT_BYTES = 56 * 1024 * 1024

F32 = jnp.float32
BF16 = jnp.bfloat16


def _rms(x, w):
    ms = jnp.mean(x * x, axis=-1, keepdims=True)
    return x * lax.rsqrt(ms + EPS) * w


def _scatter_time_major(x_ref, slab_scr, S, TB):
    for s in range(S):
        for j in range(D_MODEL // LANES):
            slab_scr[j, pl.ds(s, TB, stride=S), :] = x_ref[s, :, j * LANES:(j + 1) * LANES]


def _read_slabs(slab_scr):
    return jnp.concatenate([slab_scr[j] for j in range(D_MODEL // LANES)], axis=1)


def _store_batch_major(y, slab_scr, o_ref, S, TB):
    for j in range(D_MODEL // LANES):
        slab_scr[j] = y[:, j * LANES:(j + 1) * LANES]
    for s in range(S):
        for j in range(D_MODEL // LANES):
            o_ref[s, :, j * LANES:(j + 1) * LANES] = slab_scr[j, pl.ds(s, TB, stride=S), :]


def _gelu_tanh(x):
    c = 0.7978845608028654
    hx = 0.5 * x
    return hx + hx * jnp.tanh(x * (c + (c * 0.044715) * (x * x)))


def _rglru_kernel(S, TB, nblk, batch_major_in,
                  x_ref, conv0_ref, h0_ref, npre_ref, npost_ref, win_ref, cw_ref, cb_ref,
                  gw_ref, gab_ref, gxb_ref, lam_ref, wout_ref,
                  o_ref, conv_out_ref, h_out_ref,
                  xn_scr, xr_scr, a_scr, uh_scr, g_scr, h_scr, *slab_scr):
    D = D_MODEL
    R = S * TB
    P = (CONV_A - 1) * S
    i = pl.program_id(0)

    @pl.when(i == 0)
    def _():
        xr_scr[0:P, :] = conv0_ref[...]
        h_scr[...] = h0_ref[...]

    if batch_major_in:
        _scatter_time_major(x_ref, slab_scr[0], S, TB)
        read_x = lambda: _read_slabs(slab_scr[0])
    else:
        read_x = lambda: x_ref[...]

    xn_scr[...] = _rms(read_x(), npre_ref[...]).astype(BF16)
    xn = xn_scr[...]

    def recurrent_branch_in(b):
        cols = slice(b * GATE_BLOCK_W, (b + 1) * GATE_BLOCK_W)
        xr_cols = slice(D + b * GATE_BLOCK_W, D + (b + 1) * GATE_BLOCK_W)
        xr_scr[P:P + R, cols] = jnp.dot(xn, win_ref[:, xr_cols], preferred_element_type=F32)

    recurrent_branch_in(0)
    for b in range(N_GATE_BLOCKS):
        cols = slice(b * GATE_BLOCK_W, (b + 1) * GATE_BLOCK_W)
        if b + 1 < N_GATE_BLOCKS:
            recurrent_branch_in(b + 1)

        xc = xr_scr[0:R, cols] * cw_ref[0:1, cols]
        for k in range(1, CONV_A):
            xc = xc + xr_scr[k * S:k * S + R, cols] * cw_ref[k:k + 1, cols]
        xc = xc + cb_ref[:, cols]

        res = jnp.dot(xc.astype(BF16), gw_ref[b], preferred_element_type=F32)
        half_c_sp = (-0.5 * RG_C) * jax.nn.softplus(-lam_ref[:, cols])
        tr = jnp.tanh(res[:, 0:GATE_BLOCK_W] + gab_ref[:, cols])
        log_a = half_c_sp * tr + half_c_sp
        ig = 0.5 * jnp.tanh(res[:, GATE_BLOCK_W:2 * GATE_BLOCK_W] + gxb_ref[:, cols]) + 0.5
        a = jnp.exp(log_a)
        m2 = jnp.tanh(log_a) * (-1.0 - a * a)
        mult = jnp.where(m2 > 0.0, m2 * lax.rsqrt(m2), 0.0)
        a_scr[:, cols] = a
        uh_scr[:, cols] = mult * (ig * xc)

        gate = jnp.dot(xn, win_ref[:, cols], preferred_element_type=F32)
        g_scr[:, cols] = _gelu_tanh(gate)

    new_state = xr_scr[R:R + P, :]
    xr_scr[0:P, :] = new_state

    @pl.when(i == nblk - 1)
    def _():
        conv_out_ref[...] = new_state

    def seq_body(c, carry):
        s0 = pl.multiple_of(c * SUBLANES, SUBLANES)

        def t_body(t, h):
            r = pl.ds(pl.multiple_of(t * S + s0, SUBLANES), SUBLANES)
            h = a_scr[r, :] * h + uh_scr[r, :]
            uh_scr[r, :] = h * g_scr[r, :]
            return h

        h = lax.fori_loop(0, TB, t_body, h_scr[pl.ds(s0, SUBLANES), :], unroll=2)
        h_scr[pl.ds(s0, SUBLANES), :] = h
        return carry

    lax.fori_loop(0, S // SUBLANES, seq_body, 0)

    @pl.when(i == nblk - 1)
    def _():
        h_out_ref[...] = h_scr[...]

    y = jnp.dot(uh_scr[...].astype(BF16), wout_ref[...], preferred_element_type=F32)
    o_ref[...] = read_x() + _rms(y, npost_ref[...])


def _const_spec(shape):
    zeros = (0,) * len(shape)
    return pl.BlockSpec(shape, lambda i: zeros, pipeline_mode=pl.Buffered(1))


def _slab_scratch(R):
    return pltpu.VMEM((D_MODEL // LANES, R, LANES), F32)


def _rglru_layer(x, conv0, h0, npre, npost, w_in, conv_w, conv_b, gate_w, ga_b, gx_b, lam, w_out, *, S, TB):
    D = D_MODEL
    batch_major_in = x.ndim == 3
    N = x.size // D
    R = S * TB
    nblk = N // R
    P = (CONV_A - 1) * S
    row_spec = pl.BlockSpec((R, D), lambda i: (i, 0))
    x_spec = pl.BlockSpec((S, TB, D), lambda i: (0, i, 0)) if batch_major_in else row_spec
    kern = functools.partial(_rglru_kernel, S, TB, nblk, batch_major_in)
    return pl.pallas_call(
        kern,
        grid=(nblk,),
        in_specs=[x_spec, _const_spec((P, D)), _const_spec((S, D)),
                  _const_spec((1, D)), _const_spec((1, D)), _const_spec((D, 2 * D)),
                  _const_spec((CONV_A, D)), _const_spec((1, D)),
                  _const_spec((N_GATE_BLOCKS, GATE_BLOCK_W, 2 * GATE_BLOCK_W)),
                  _const_spec((1, D)), _const_spec((1, D)), _const_spec((1, D)),
                  _const_spec((D, D))],
        out_specs=[row_spec, pl.BlockSpec((P, D), lambda i: (0, 0)), pl.BlockSpec((S, D), lambda i: (0, 0))],
        out_shape=[jax.ShapeDtypeStruct((N, D), F32), jax.ShapeDtypeStruct((P, D), F32),
                   jax.ShapeDtypeStruct((S, D), F32)],
        scratch_shapes=[pltpu.VMEM((R, D), BF16),
                        pltpu.VMEM((R + P, D), F32),
                        pltpu.VMEM((R, D), F32),
                        pltpu.VMEM((R, D), F32),
                        pltpu.VMEM((R, D), F32),
                        pltpu.VMEM((S, D), F32)]
                       + ([_slab_scratch(R)] if batch_major_in else []),
        compiler_params=pltpu.CompilerParams(dimension_semantics=("arbitrary",),
                                             vmem_limit_bytes=VMEM_LIMIT_BYTES),
        name="rglru_mixer",
    )(x, conv0, h0, npre, npost, w_in, conv_w, conv_b, gate_w, ga_b, gx_b, lam, w_out)


def _sconv_kernel(S, TB, nblk,
                  x_ref, conv0_ref, npre_ref, npost_ref, win_ref, cw_ref, wout_ref,
                  o_ref, conv_out_ref,
                  xn_scr, cv_scr, m_scr):
    D = D_MODEL
    R = S * TB
    P = (CONV_B - 1) * S
    i = pl.program_id(0)

    @pl.when(i == 0)
    def _():
        cv_scr[0:P, :] = conv0_ref[...]

    xn_scr[...] = _rms(x_ref[...], npre_ref[...]).astype(BF16)
    xn = xn_scr[...]

    for c0 in range(0, D, SCONV_CHUNK):
        cols = slice(c0, c0 + SCONV_CHUNK)
        cg = jnp.dot(xn, win_ref[:, D + c0:D + c0 + SCONV_CHUNK], preferred_element_type=F32)
        v = jnp.dot(xn, win_ref[:, 2 * D + c0:2 * D + c0 + SCONV_CHUNK], preferred_element_type=F32)
        cv_scr[P:P + R, cols] = cg * v
        conv = cv_scr[0:R, cols] * cw_ref[0:1, cols]
        for k in range(1, CONV_B):
            conv = conv + cv_scr[k * S:k * S + R, cols] * cw_ref[k:k + 1, cols]
        bg = jnp.dot(xn, win_ref[:, cols], preferred_element_type=F32)
        m_scr[:, cols] = (bg * conv).astype(BF16)

    new_state = cv_scr[R:R + P, :]
    cv_scr[0:P, :] = new_state

    @pl.when(i == nblk - 1)
    def _():
        conv_out_ref[...] = new_state

    y = jnp.dot(m_scr[...], wout_ref[...], preferred_element_type=F32)
    o_ref[...] = x_ref[...] + _rms(y, npost_ref[...])


def _sconv_layer(x, conv0, npre, npost, w_in, conv_w, w_out, *, S, TB):
    N, D = x.shape
    R = S * TB
    nblk = N // R
    P = (CONV_B - 1) * S
    row_spec = pl.BlockSpec((R, D), lambda i: (i, 0))
    kern = functools.partial(_sconv_kernel, S, TB, nblk)
    return pl.pallas_call(
        kern,
        grid=(nblk,),
        in_specs=[row_spec, _const_spec((P, D)), _const_spec((1, D)), _const_spec((1, D)),
                  _const_spec((D, 3 * D)), _const_spec((CONV_B, D)), _const_spec((D, D))],
        out_specs=[row_spec, pl.BlockSpec((P, D), lambda i: (0, 0))],
        out_shape=[jax.ShapeDtypeStruct((N, D), F32), jax.ShapeDtypeStruct((P, D), F32)],
        scratch_shapes=[pltpu.VMEM((R, D), BF16),
                        pltpu.VMEM((R + P, D), F32),
                        pltpu.VMEM((R, D), BF16)],
        compiler_params=pltpu.CompilerParams(dimension_semantics=("arbitrary",),
                                             vmem_limit_bytes=VMEM_LIMIT_BYTES),
        name="sconv_mixer",
    )(x, conv0, npre, npost, w_in, conv_w, w_out)


def _ffn_kernel(batch_major_out, x_ref, npre_ref, npost_ref, wg_ref, wu_ref, wd_ref, o_ref,
                xn_scr, act_scr, *slab_scr):
    xn_scr[...] = _rms(x_ref[...], npre_ref[...]).astype(BF16)
    xn = xn_scr[...]
    for c0 in range(0, D_FF, FF_CHUNK):
        cols = slice(c0, c0 + FF_CHUNK)
        g = jnp.dot(xn, wg_ref[:, cols], preferred_element_type=F32)
        u = jnp.dot(xn, wu_ref[:, cols], preferred_element_type=F32)
        act_scr[:, cols] = (g * jax.nn.sigmoid(g) * u).astype(BF16)
    y = jnp.dot(act_scr[...], wd_ref[...], preferred_element_type=F32)
    out = x_ref[...] + _rms(y, npost_ref[...])
    if batch_major_out:
        S, TB, _ = o_ref.shape
        _store_batch_major(out, slab_scr[0], o_ref, S, TB)
    else:
        o_ref[...] = out


def _ffn_layer(x, npre, npost, wg, wu, wd, *, R, batch_major_seqs=None):
    N, D = x.shape
    row_spec = pl.BlockSpec((R, D), lambda i: (i, 0))
    if batch_major_seqs is None:
        out_spec, out_shape, slabs = row_spec, jax.ShapeDtypeStruct((N, D), F32), []
    else:
        S = batch_major_seqs
        out_spec = pl.BlockSpec((S, R // S, D), lambda i: (0, i, 0))
        out_shape, slabs = jax.ShapeDtypeStruct((S, N // S, D), F32), [_slab_scratch(R)]
    return pl.pallas_call(
        functools.partial(_ffn_kernel, batch_major_seqs is not None),
        grid=(N // R,),
        in_specs=[row_spec, _const_spec((1, D)), _const_spec((1, D)),
                  _const_spec((D, D_FF)), _const_spec((D, D_FF)), _const_spec((D_FF, D))],
        out_specs=out_spec,
        out_shape=out_shape,
        scratch_shapes=[pltpu.VMEM((R, D), BF16), pltpu.VMEM((R, D_FF), BF16)] + slabs,
        compiler_params=pltpu.CompilerParams(dimension_semantics=("arbitrary",),
                                             vmem_limit_bytes=VMEM_LIMIT_BYTES),
        name="swiglu_ffn",
    )(x, npre, npost, wg, wu, wd)


def _to_time_major(a):
    S, K, D = a.shape
    return jnp.swapaxes(a, 0, 1).reshape(K * S, D)


def _from_time_major(a, S):
    KS, D = a.shape
    return jnp.swapaxes(a.reshape(KS // S, S, D), 0, 1)


def kernel(x_prompt, x_sample, state_rglru_conv, state_rglru_h, state_sconv, meta_tokens, norm_mix_pre, norm_mix_post, norm_ffn_pre, norm_ffn_post, rg_w_in, rg_conv_w, rg_conv_b, rg_gate_a_w, rg_gate_a_b, rg_gate_x_w, rg_gate_x_b, rg_lambda, rg_w_out, sc_w_in, sc_conv_w, sc_w_out, ffn_w_gate, ffn_w_up, ffn_w_down):
    D = D_MODEL
    depth = norm_mix_pre.shape[0]
    batch, seq, _ = x_prompt.shape
    dec_batch, dec_seq, _ = x_sample.shape
    row = lambda v: v.reshape(1, D)

    rg_w_in_b = rg_w_in.astype(BF16)
    rg_w_out_b = rg_w_out.astype(BF16)
    gate_w_b = (0.5 * jnp.concatenate([rg_gate_a_w, rg_gate_x_w], axis=-1)).astype(BF16)
    sc_w_in_b = sc_w_in.astype(BF16)
    sc_w_out_b = sc_w_out.astype(BF16)
    wg_b = ffn_w_gate.astype(BF16)
    wu_b = ffn_w_up.astype(BF16)
    wd_b = ffn_w_down.astype(BF16)

    def run_trunk(x, rg_conv_in, rg_h_in, sc_in, *, S, TB, batch_major_out=False):
        rg_conv_out, rg_h_out, sc_out = [], [], []
        for i in range(depth):
            j = i // 2
            if i % 2 == 0:
                x, cb, hT = _rglru_layer(
                    x, rg_conv_in[j], rg_h_in[j], row(norm_mix_pre[i]), row(norm_mix_post[i]),
                    rg_w_in_b[j], rg_conv_w[j], row(rg_conv_b[j]), gate_w_b[j],
                    row(0.5 * rg_gate_a_b[j]), row(0.5 * rg_gate_x_b[j]), row(rg_lambda[j]), rg_w_out_b[j],
                    S=S, TB=TB)
                rg_conv_out.append(cb)
                rg_h_out.append(hT)
            else:
                x, cb = _sconv_layer(x, sc_in[j], row(norm_mix_pre[i]), row(norm_mix_post[i]),
                                     sc_w_in_b[j], sc_conv_w[j], sc_w_out_b[j], S=S, TB=TB)
                sc_out.append(cb)
            last = batch_major_out and i == depth - 1
            x = _ffn_layer(x, row(norm_ffn_pre[i]), row(norm_ffn_post[i]), wg_b[i], wu_b[i], wd_b[i],
                           R=S * TB, batch_major_seqs=S if last else None)
        return x, rg_conv_out, rg_h_out, sc_out

    n_a = rg_w_in.shape[0]
    n_b = sc_w_in.shape[0]

    xm = jnp.broadcast_to(meta_tokens[:, None, :], (N_META, batch, D)).reshape(N_META * batch, D)
    zc_a = [jnp.zeros(((CONV_A - 1) * batch, D), F32)] * n_a
    zh_a = [jnp.zeros((batch, D), F32)] * n_a
    zc_b = [jnp.zeros(((CONV_B - 1) * batch, D), F32)] * n_b
    _, rg_conv_m, rg_h_m, sc_m = run_trunk(xm, zc_a, zh_a, zc_b, S=batch, TB=N_META)

    y_prompt, rg_conv_p, rg_h_p, sc_p = run_trunk(x_prompt, rg_conv_m, rg_h_m, sc_m, S=batch, TB=PROMPT_TB,
                                                  batch_major_out=True)

    xs = _to_time_major(x_sample)
    sc_a = [_to_time_major(state_rglru_conv[j]) for j in range(n_a)]
    sh_a = [state_rglru_h[j] for j in range(n_a)]
    sc_b = [_to_time_major(state_sconv[j]) for j in range(n_b)]
    ys, rg_conv_s, rg_h_s, sc_s = run_trunk(xs, sc_a, sh_a, sc_b, S=dec_batch, TB=dec_seq)
    y_sample = _from_time_major(ys, dec_batch)

    return (y_prompt, y_sample,
            jnp.stack([_from_time_major(c, batch) for c in rg_conv_p]), jnp.stack(rg_h_p),
            jnp.stack([_from_time_major(c, batch) for c in sc_p]),
            jnp.stack([_from_time_major(c, dec_batch) for c in rg_conv_s]), jnp.stack(rg_h_s),
            jnp.stack([_from_time_major(c, dec_batch) for c in sc_s]))
```

```python
import functools
from typing import NamedTuple

import jax
import jax.numpy as jnp
from jax import lax
from jax.experimental import pallas as pl
from jax.experimental.pallas import tpu as pltpu

D_MODEL = 1024
D_FF = 2816
N_META = 16
N_GATE_BLOCKS = 4
GATE_BLOCK_W = 256
CONV_A = 4
CONV_B = 3
RG_C = 8.0
EPS = 1e-6

SUBLANES = 8
LANES = 128
FF_CHUNK = 256
SCONV_CHUNK = 256
PROMPT_TB = 128
VMEM_LIMIT_BYTES = 56 * 1024 * 1024

F32 = jnp.float32
BF16 = jnp.bfloat16


class _Seg(NamedTuple):
    S: int
    TB: int
    row0: int
    conv0: int
    h0: int

    @property
    def rows(self):
        return self.S * self.TB


def _rms(x, w):
    ms = jnp.mean(x * x, axis=-1, keepdims=True)
    return x * lax.rsqrt(ms + EPS) * w


def _scatter_time_major(x_ref, slab_scr, S, TB):
    for s in range(S):
        for j in range(D_MODEL // LANES):
            slab_scr[j, pl.ds(s, TB, stride=S), :] = x_ref[s, :, j * LANES:(j + 1) * LANES]


def _read_slabs(slab_scr):
    return jnp.concatenate([slab_scr[j] for j in range(D_MODEL // LANES)], axis=1)


def _store_batch_major(y, slab_scr, o_ref, S, TB):
    for j in range(D_MODEL // LANES):
        slab_scr[j] = y[:, j * LANES:(j + 1) * LANES]
    for s in range(S):
        for j in range(D_MODEL // LANES):
            o_ref[s, :, j * LANES:(j + 1) * LANES] = slab_scr[j, pl.ds(s, TB, stride=S), :]


def _gelu_tanh(x):
    c = 0.7978845608028654
    hx = 0.5 * x
    return hx + hx * jnp.tanh(x * (c + (c * 0.044715) * (x * x)))


def _layer_spec(arr, layer):
    tail = arr.shape[1:]
    zeros = (0,) * len(tail)
    return pl.BlockSpec((None,) + tail, lambda i: (layer,) + zeros, pipeline_mode=pl.Buffered(1))


def _const_spec(shape):
    zeros = (0,) * len(shape)
    return pl.BlockSpec(shape, lambda i: zeros, pipeline_mode=pl.Buffered(1))


def _slab_scratch(R):
    return pltpu.VMEM((D_MODEL // LANES, R, LANES), F32)


def _params(**kw):
    return pltpu.CompilerParams(dimension_semantics=("arbitrary",), vmem_limit_bytes=VMEM_LIMIT_BYTES, **kw)


def _rglru_block(seg, read_x, write_out, npre_ref, npost_ref, win_ref, cw_ref, cb_ref, gw_ref,
                 gab_ref, gxb_ref, lam_ref, wout_ref, xn_scr, xr_scr, h_scr):
    D = D_MODEL
    S, TB, R = seg.S, seg.TB, seg.rows
    P = (CONV_A - 1) * S
    rows = slice(seg.row0, seg.row0 + R)
    c0 = seg.conv0
    groups = S // SUBLANES

    xn_scr[rows, :] = _rms(read_x(), npre_ref[...]).astype(BF16)
    xn = xn_scr[rows, :]

    def recurrent_branch_in(b):
        cols = slice(b * GATE_BLOCK_W, (b + 1) * GATE_BLOCK_W)
        xr_cols = slice(D + b * GATE_BLOCK_W, D + (b + 1) * GATE_BLOCK_W)
        xr_scr[c0 + P:c0 + P + R, cols] = jnp.dot(xn, win_ref[:, xr_cols], preferred_element_type=F32)

    y = None
    recurrent_branch_in(0)
    for b in range(N_GATE_BLOCKS):
        cols = slice(b * GATE_BLOCK_W, (b + 1) * GATE_BLOCK_W)
        if b + 1 < N_GATE_BLOCKS:
            recurrent_branch_in(b + 1)

        xc = xr_scr[c0:c0 + R, cols] * cw_ref[0:1, cols]
        for k in range(1, CONV_A):
            xc = xc + xr_scr[c0 + k * S:c0 + k * S + R, cols] * cw_ref[k:k + 1, cols]
        xc = xc + cb_ref[:, cols]

        res = jnp.dot(xc.astype(BF16), gw_ref[b], preferred_element_type=F32)
        half_c_sp = (-0.5 * RG_C) * jax.nn.softplus(-lam_ref[:, cols])
        tr = jnp.tanh(res[:, 0:GATE_BLOCK_W] + gab_ref[:, cols])
        log_a = half_c_sp * tr + half_c_sp
        ig = 0.5 * jnp.tanh(res[:, GATE_BLOCK_W:2 * GATE_BLOCK_W] + gxb_ref[:, cols]) + 0.5
        a = jnp.exp(log_a)
        m2 = jnp.tanh(log_a) * (-1.0 - a * a)
        u = jnp.where(m2 > 0.0, m2 * lax.rsqrt(m2), 0.0) * (ig * xc)

        g = _gelu_tanh(jnp.dot(xn, win_ref[:, cols], preferred_element_type=F32))

        pieces = [None] * (TB * groups)
        for c in range(groups):
            hrows = slice(seg.h0 + c * SUBLANES, seg.h0 + (c + 1) * SUBLANES)
            h = h_scr[hrows, cols]
            for t in range(TB):
                r = t * S + c * SUBLANES
                h = a[r:r + SUBLANES] * h + u[r:r + SUBLANES]
                pieces[t * groups + c] = h * g[r:r + SUBLANES]
            h_scr[hrows, cols] = h
        hs = jnp.concatenate(pieces, axis=0)

        part = jnp.dot(hs.astype(BF16), wout_ref[cols, :], preferred_element_type=F32)
        y = part if y is None else y + part

    xr_scr[c0:c0 + P, :] = xr_scr[c0 + R:c0 + R + P, :]
    write_out(read_x() + _rms(y, npost_ref[...]))


def _rglru_kernel(segs, nblk, batch_major_in, *refs):
    n = len(segs)
    x_ref = refs[0]
    conv_in = refs[1:1 + n]
    h_in = refs[1 + n:1 + 2 * n]
    weights = refs[1 + 2 * n:11 + 2 * n]
    o_ref = refs[11 + 2 * n]
    conv_out = refs[12 + 2 * n:12 + 3 * n]
    h_out = refs[12 + 3 * n:12 + 4 * n]
    xn_scr, xr_scr, h_scr, *slab_scr = refs[12 + 4 * n:]
    i = pl.program_id(0)

    @pl.when(i == 0)
    def _():
        for seg, c_ref, h_ref in zip(segs, conv_in, h_in):
            xr_scr[seg.conv0:seg.conv0 + (CONV_A - 1) * seg.S, :] = c_ref[...]
            h_scr[seg.h0:seg.h0 + seg.S, :] = h_ref[...]

    for seg in segs:
        rows = slice(seg.row0, seg.row0 + seg.rows)
        if batch_major_in:
            _scatter_time_major(x_ref, slab_scr[0], seg.S, seg.TB)
            read_x = lambda: _read_slabs(slab_scr[0])
        else:
            read_x = lambda rows=rows: x_ref[rows, :]

        def write_out(v, rows=rows):
            o_ref[rows, :] = v

        _rglru_block(seg, read_x, write_out, *weights, xn_scr, xr_scr, h_scr)

    @pl.when(i == nblk - 1)
    def _():
        for seg, c_ref, h_ref in zip(segs, conv_out, h_out):
            c_ref[...] = xr_scr[seg.conv0:seg.conv0 + (CONV_A - 1) * seg.S, :]
            h_ref[...] = h_scr[seg.h0:seg.h0 + seg.S, :]


def _segments(seqs_and_steps, taps):
    segs, row0, conv0, h0 = [], 0, 0, 0
    for S, TB in seqs_and_steps:
        segs.append(_Seg(S, TB, row0, conv0, h0))
        row0 += S * TB
        conv0 += (taps - 1) * S + S * TB
        h0 += S
    return tuple(segs), row0, conv0, h0


def _rglru_layer(x, conv_in, h_in, layer, j, p, *, seqs_and_steps):
    D = D_MODEL
    batch_major_in = x.ndim == 3
    segs, R, conv_rows, h_rows = _segments(seqs_and_steps, CONV_A)
    nblk = x.size // D // R
    row_spec = pl.BlockSpec((R, D), lambda i: (i, 0))
    if batch_major_in:
        (S, TB), = seqs_and_steps
        x_spec = pl.BlockSpec((S, TB, D), lambda i: (0, i, 0))
    else:
        x_spec = row_spec
    state_specs = ([_const_spec(c.shape) for c in conv_in] + [_const_spec(h.shape) for h in h_in])
    weights = [(p["norm_mix_pre"], layer), (p["norm_mix_post"], layer), (p["rg_w_in"], j),
               (p["rg_conv_w"], j), (p["rg_conv_b"], j), (p["rg_gate_w"], j), (p["rg_gate_a_b"], j),
               (p["rg_gate_x_b"], j), (p["rg_lambda"], j), (p["rg_w_out"], j)]
    outs = pl.pallas_call(
        functools.partial(_rglru_kernel, segs, nblk, batch_major_in),
        grid=(nblk,),
        in_specs=[x_spec] + state_specs + [_layer_spec(w, l) for w, l in weights],
        out_specs=[row_spec] + [pl.BlockSpec(s.shape, lambda i: (0, 0)) for s in conv_in + h_in],
        out_shape=[jax.ShapeDtypeStruct((nblk * R, D), F32)]
                  + [jax.ShapeDtypeStruct(s.shape, F32) for s in conv_in + h_in],
        scratch_shapes=[pltpu.VMEM((R, D), BF16),
                        pltpu.VMEM((conv_rows, D), F32),
                        pltpu.VMEM((h_rows, D), F32)]
                       + ([_slab_scratch(R)] if batch_major_in else []),
        compiler_params=_params(),
        name="rglru_mixer",
    )(x, *conv_in, *h_in, *[w for w, _ in weights])
    n = len(segs)
    return outs[0], list(outs[1:1 + n]), list(outs[1 + n:1 + 2 * n])


def _sconv_block(seg, x_ref, o_ref, npre_ref, npost_ref, win_ref, cw_ref, wout_ref, xn_scr, cv_scr, m_scr):
    D = D_MODEL
    S, R = seg.S, seg.rows
    P = (CONV_B - 1) * S
    rows = slice(seg.row0, seg.row0 + R)
    c0 = seg.conv0

    xn_scr[rows, :] = _rms(x_ref[rows, :], npre_ref[...]).astype(BF16)
    xn = xn_scr[rows, :]

    for col in range(0, D, SCONV_CHUNK):
        cols = slice(col, col + SCONV_CHUNK)
        cg = jnp.dot(xn, win_ref[:, D + col:D + col + SCONV_CHUNK], preferred_element_type=F32)
        v = jnp.dot(xn, win_ref[:, 2 * D + col:2 * D + col + SCONV_CHUNK], preferred_element_type=F32)
        cv_scr[c0 + P:c0 + P + R, cols] = cg * v
        conv = cv_scr[c0:c0 + R, cols] * cw_ref[0:1, cols]
        for k in range(1, CONV_B):
            conv = conv + cv_scr[c0 + k * S:c0 + k * S + R, cols] * cw_ref[k:k + 1, cols]
        bg = jnp.dot(xn, win_ref[:, cols], preferred_element_type=F32)
        m_scr[rows, cols] = (bg * conv).astype(BF16)

    cv_scr[c0:c0 + P, :] = cv_scr[c0 + R:c0 + R + P, :]
    y = jnp.dot(m_scr[rows, :], wout_ref[...], preferred_element_type=F32)
    o_ref[rows, :] = x_ref[rows, :] + _rms(y, npost_ref[...])


def _sconv_kernel(segs, nblk, *refs):
    n = len(segs)
    x_ref = refs[0]
    conv_in = refs[1:1 + n]
    weights = refs[1 + n:6 + n]
    o_ref = refs[6 + n]
    conv_out = refs[7 + n:7 + 2 * n]
    xn_scr, cv_scr, m_scr = refs[7 + 2 * n:]
    i = pl.program_id(0)

    @pl.when(i == 0)
    def _():
        for seg, c_ref in zip(segs, conv_in):
            cv_scr[seg.conv0:seg.conv0 + (CONV_B - 1) * seg.S, :] = c_ref[...]

    for seg in segs:
        _sconv_block(seg, x_ref, o_ref, *weights, xn_scr, cv_scr, m_scr)

    @pl.when(i == nblk - 1)
    def _():
        for seg, c_ref in zip(segs, conv_out):
            c_ref[...] = cv_scr[seg.conv0:seg.conv0 + (CONV_B - 1) * seg.S, :]


def _sconv_layer(x, conv_in, layer, j, p, *, seqs_and_steps):
    N, D = x.shape
    segs, R, conv_rows, _ = _segments(seqs_and_steps, CONV_B)
    nblk = N // R
    row_spec = pl.BlockSpec((R, D), lambda i: (i, 0))
    weights = [(p["norm_mix_pre"], layer), (p["norm_mix_post"], layer), (p["sc_w_in"], j),
               (p["sc_conv_w"], j), (p["sc_w_out"], j)]
    outs = pl.pallas_call(
        functools.partial(_sconv_kernel, segs, nblk),
        grid=(nblk,),
        in_specs=[row_spec] + [_const_spec(c.shape) for c in conv_in] + [_layer_spec(w, l) for w, l in weights],
        out_specs=[row_spec] + [pl.BlockSpec(c.shape, lambda i: (0, 0)) for c in conv_in],
        out_shape=[jax.ShapeDtypeStruct((N, D), F32)] + [jax.ShapeDtypeStruct(c.shape, F32) for c in conv_in],
        scratch_shapes=[pltpu.VMEM((R, D), BF16),
                        pltpu.VMEM((conv_rows, D), F32),
                        pltpu.VMEM((R, D), BF16)],
        compiler_params=_params(),
        name="sconv_mixer",
    )(x, *conv_in, *[w for w, _ in weights])
    return outs[0], list(outs[1:])


def _ffn_kernel(batch_major_out, x_ref, npre_ref, npost_ref, wg_ref, wu_ref, wd_ref, o_ref,
                xn_scr, act_scr, *slab_scr):
    xn_scr[...] = _rms(x_ref[...], npre_ref[...]).astype(BF16)
    xn = xn_scr[...]
    for c0 in range(0, D_FF, FF_CHUNK):
        cols = slice(c0, c0 + FF_CHUNK)
        g = jnp.dot(xn, wg_ref[:, cols], preferred_element_type=F32)
        u = jnp.dot(xn, wu_ref[:, cols], preferred_element_type=F32)
        act_scr[:, cols] = (g * jax.nn.sigmoid(g) * u).astype(BF16)
    y = jnp.dot(act_scr[...], wd_ref[...], preferred_element_type=F32)
    out = x_ref[...] + _rms(y, npost_ref[...])
    if batch_major_out:
        S, TB, _ = o_ref.shape
        _store_batch_major(out, slab_scr[0], o_ref, S, TB)
    else:
        o_ref[...] = out


def _ffn_layer(x, layer, p, *, R, batch_major_seqs=None):
    N, D = x.shape
    row_spec = pl.BlockSpec((R, D), lambda i: (i, 0))
    if batch_major_seqs is None:
        out_spec, out_shape, slabs = row_spec, jax.ShapeDtypeStruct((N, D), F32), []
    else:
        S = batch_major_seqs
        out_spec = pl.BlockSpec((S, R // S, D), lambda i: (0, i, 0))
        out_shape, slabs = jax.ShapeDtypeStruct((S, N // S, D), F32), [_slab_scratch(R)]
    weights = [p["norm_ffn_pre"], p["norm_ffn_post"], p["ffn_w_gate"], p["ffn_w_up"], p["ffn_w_down"]]
    return pl.pallas_call(
        functools.partial(_ffn_kernel, batch_major_seqs is not None),
        grid=(N // R,),
        in_specs=[row_spec] + [_layer_spec(w, layer) for w in weights],
        out_specs=out_spec,
        out_shape=out_shape,
        scratch_shapes=[pltpu.VMEM((R, D), BF16), pltpu.VMEM((R, D_FF), BF16)] + slabs,
        compiler_params=_params(),
        name="swiglu_ffn",
    )(x, *weights)


def _to_time_major(a):
    S, K, D = a.shape
    return jnp.swapaxes(a, 0, 1).reshape(K * S, D)


def _from_time_major(a, S):
    KS, D = a.shape
    return jnp.swapaxes(a.reshape(KS // S, S, D), 0, 1)


def kernel(x_prompt, x_sample, state_rglru_conv, state_rglru_h, state_sconv, meta_tokens, norm_mix_pre, norm_mix_post, norm_ffn_pre, norm_ffn_post, rg_w_in, rg_conv_w, rg_conv_b, rg_gate_a_w, rg_gate_a_b, rg_gate_x_w, rg_gate_x_b, rg_lambda, rg_w_out, sc_w_in, sc_conv_w, sc_w_out, ffn_w_gate, ffn_w_up, ffn_w_down):
    D = D_MODEL
    depth = norm_mix_pre.shape[0]
    batch, seq, _ = x_prompt.shape
    dec_batch, dec_seq, _ = x_sample.shape
    n_a = rg_w_in.shape[0]
    n_b = sc_w_in.shape[0]
    rows3 = lambda v: v.reshape(v.shape[0], 1, D)

    p = dict(
        norm_mix_pre=rows3(norm_mix_pre), norm_mix_post=rows3(norm_mix_post),
        norm_ffn_pre=rows3(norm_ffn_pre), norm_ffn_post=rows3(norm_ffn_post),
        rg_w_in=rg_w_in.astype(BF16), rg_conv_w=rg_conv_w, rg_conv_b=rows3(rg_conv_b),
        rg_gate_w=(0.5 * jnp.concatenate([rg_gate_a_w, rg_gate_x_w], axis=-1)).astype(BF16),
        rg_gate_a_b=rows3(0.5 * rg_gate_a_b), rg_gate_x_b=rows3(0.5 * rg_gate_x_b),
        rg_lambda=rows3(rg_lambda), rg_w_out=rg_w_out.astype(BF16),
        sc_w_in=sc_w_in.astype(BF16), sc_conv_w=sc_conv_w, sc_w_out=sc_w_out.astype(BF16),
        ffn_w_gate=ffn_w_gate.astype(BF16), ffn_w_up=ffn_w_up.astype(BF16), ffn_w_down=ffn_w_down.astype(BF16),
    )

    def run_trunk(x, rg_conv_in, rg_h_in, sc_in, *, seqs_and_steps, batch_major_out=False):
        R = sum(S * TB for S, TB in seqs_and_steps)
        rg_conv_out, rg_h_out, sc_out = [], [], []
        for i in range(depth):
            j = i // 2
            if i % 2 == 0:
                x, cb, hT = _rglru_layer(x, rg_conv_in[j], rg_h_in[j], i, j, p, seqs_and_steps=seqs_and_steps)
                rg_conv_out.append(cb)
                rg_h_out.append(hT)
            else:
                x, cb = _sconv_layer(x, sc_in[j], i, j, p, seqs_and_steps=seqs_and_steps)
                sc_out.append(cb)
            last = batch_major_out and i == depth - 1
            x = _ffn_layer(x, i, p, R=R, batch_major_seqs=seqs_and_steps[0][0] if last else None)
        return x, rg_conv_out, rg_h_out, sc_out

    xm = jnp.broadcast_to(meta_tokens[:, None, :], (N_META, batch, D)).reshape(N_META * batch, D)
    x_small = jnp.concatenate([xm, _to_time_major(x_sample)], axis=0)
    small = ((batch, N_META), (dec_batch, dec_seq))
    rg_conv0 = [[jnp.zeros(((CONV_A - 1) * batch, D), F32), _to_time_major(state_rglru_conv[j])]
                for j in range(n_a)]
    rg_h0 = [[jnp.zeros((batch, D), F32), state_rglru_h[j]] for j in range(n_a)]
    sc0 = [[jnp.zeros(((CONV_B - 1) * batch, D), F32), _to_time_major(state_sconv[j])] for j in range(n_b)]
    y_small, rg_conv_s, rg_h_s, sc_s = run_trunk(x_small, rg_conv0, rg_h0, sc0, seqs_and_steps=small)
    y_sample = _from_time_major(y_small[N_META * batch:], dec_batch)

    y_prompt, rg_conv_p, rg_h_p, sc_p = run_trunk(
        x_prompt, [[c[0]] for c in rg_conv_s], [[h[0]] for h in rg_h_s], [[c[0]] for c in sc_s],
        seqs_and_steps=((batch, PROMPT_TB),), batch_major_out=True)

    return (y_prompt, y_sample,
            jnp.stack([_from_time_major(c[0], batch) for c in rg_conv_p]), jnp.stack([h[0] for h in rg_h_p]),
            jnp.stack([_from_time_major(c[0], batch) for c in sc_p]),
            jnp.stack([_from_time_major(c[1], dec_batch) for c in rg_conv_s]), jnp.stack([h[1] for h in rg_h_s]),
            jnp.stack([_from_time_major(c[1], dec_batch) for c in sc_s]))
```

```python
import functools
from typing import NamedTuple

import jax
import jax.numpy as jnp
from jax import lax
from jax.experimental import pallas as pl
from jax.experimental.pallas import tpu as pltpu

D_MODEL = 1024
D_FF = 2816
N_META = 16
N_GATE_BLOCKS = 4
GATE_BLOCK_W = 256
CONV_A = 4
CONV_B = 3
RG_C = 8.0
EPS = 1e-6

SUBLANES = 8
LANES = 128
COL_BLOCK = 256
N_COL_BLOCKS = D_MODEL // COL_BLOCK
FF_CHUNK = 256
PROMPT_TB = 128
VMEM_LIMIT_BYTES = 56 * 1024 * 1024

F32 = jnp.float32
BF16 = jnp.bfloat16


class _Seg(NamedTuple):
    S: int
    TB: int
    row0: int
    conv0: int
    h0: int

    @property
    def rows(self):
        return self.S * self.TB


def _segments(seqs_and_steps, taps):
    segs, row0, conv0, h0 = [], 0, 0, 0
    for S, TB in seqs_and_steps:
        segs.append(_Seg(S, TB, row0, conv0, h0))
        row0 += S * TB
        conv0 += (taps - 1) * S + S * TB
        h0 += S
    return tuple(segs), row0, conv0, h0


def _rms(x, w):
    ms = jnp.mean(x * x, axis=-1, keepdims=True)
    return x * lax.rsqrt(ms + EPS) * w


def _dot(a, b):
    return jnp.dot(a, b, preferred_element_type=F32)


def _scatter_time_major(x_ref, slab_scr, S, TB):
    for s in range(S):
        for j in range(D_MODEL // LANES):
            slab_scr[j, pl.ds(s, TB, stride=S), :] = x_ref[s, :, j * LANES:(j + 1) * LANES]


def _read_slabs(slab_scr):
    return jnp.concatenate([slab_scr[j] for j in range(D_MODEL // LANES)], axis=1)


def _store_batch_major(y, slab_scr, o_ref, S, TB):
    for j in range(D_MODEL // LANES):
        slab_scr[j] = y[:, j * LANES:(j + 1) * LANES]
    for s in range(S):
        for j in range(D_MODEL // LANES):
            o_ref[s, :, j * LANES:(j + 1) * LANES] = slab_scr[j, pl.ds(s, TB, stride=S), :]


def _gelu_tanh(x):
    c = 0.7978845608028654
    hx = 0.5 * x
    return hx + hx * jnp.tanh(x * (c + (c * 0.044715) * (x * x)))


def _resident(shape):
    zeros = (0,) * len(shape)
    return pl.BlockSpec(shape, lambda i: zeros, pipeline_mode=pl.Buffered(1))


def _slab_scratch(R):
    return pltpu.VMEM((D_MODEL // LANES, R, LANES), F32)


def _params():
    return pltpu.CompilerParams(dimension_semantics=("arbitrary",), vmem_limit_bytes=VMEM_LIMIT_BYTES)


def _accumulate(y_scr, rows, part, step):
    @pl.when(step == 0)
    def _():
        y_scr[rows, :] = part

    @pl.when(step > 0)
    def _():
        y_scr[rows, :] += part


def _rglru_branch_in(seg, xn, w_xr, region):
    P = (CONV_A - 1) * seg.S
    region[seg.conv0 + P:seg.conv0 + P + seg.rows, :] = _dot(xn, w_xr)


def _rglru_colblock(seg, xn, w_gate, gw, cw, cb, half_gab, half_gxb, lam, w_out_rows,
                    region, h_read, h_write, h_row0):
    S, TB, R = seg.S, seg.TB, seg.rows
    P = (CONV_A - 1) * S
    c0 = seg.conv0
    groups = S // SUBLANES

    xc = region[c0:c0 + R, :] * cw[0:1]
    for k in range(1, CONV_A):
        xc = xc + region[c0 + k * S:c0 + k * S + R, :] * cw[k:k + 1]
    xc = xc + cb
    region[c0:c0 + P, :] = region[c0 + R:c0 + R + P, :]

    res = _dot(xc.astype(BF16), gw)
    half_c_sp = (-0.5 * RG_C) * jax.nn.softplus(-lam)
    tr = jnp.tanh(res[:, 0:COL_BLOCK] + half_gab)
    log_a = half_c_sp * tr + half_c_sp
    ig = 0.5 * jnp.tanh(res[:, COL_BLOCK:2 * COL_BLOCK] + half_gxb) + 0.5
    a = jnp.exp(log_a)
    m2 = jnp.tanh(log_a) * (-1.0 - a * a)
    u = jnp.where(m2 > 0.0, m2 * lax.rsqrt(m2), 0.0) * (ig * xc)

    g = _gelu_tanh(_dot(xn, w_gate))

    pieces = [None] * (TB * groups)
    for c in range(groups):
        hrows = slice(h_row0 + c * SUBLANES, h_row0 + (c + 1) * SUBLANES)
        h = h_read[hrows, :]
        for t in range(TB):
            r = t * S + c * SUBLANES
            h = a[r:r + SUBLANES] * h + u[r:r + SUBLANES]
            pieces[t * groups + c] = h * g[r:r + SUBLANES]
        h_write[hrows, :] = h
    hs = jnp.concatenate(pieces, axis=0)
    return _dot(hs.astype(BF16), w_out_rows)


def _rglru_stream_kernel(segs, layer, j, x_ref, *refs):
    n = len(segs)
    conv_in, h_in = refs[0:n], refs[n:2 * n]
    (npre_ref, npost_ref, wxr_ref, wgate_ref, cw_ref, cb_ref, gaw_ref, gxw_ref, gab_ref, gxb_ref,
     lam_ref, wout_ref) = refs[2 * n:2 * n + 12]
    o_ref = refs[2 * n + 12]
    conv_out, h_out = refs[2 * n + 13:3 * n + 13], refs[3 * n + 13:4 * n + 13]
    wxr_b_ref, wgate_b_ref, gw_b_ref, wout_b_ref = refs[4 * n + 13:4 * n + 17]
    xn_scr, xr_scr, y_scr = refs[4 * n + 17:]
    b = pl.program_id(0)

    @pl.when(b == 0)
    def _():
        xn_scr[...] = _rms(x_ref[...], npre_ref[layer:layer + 1, :]).astype(BF16)

    w_xr = wxr_ref[...].astype(BF16)
    w_gate = wgate_ref[...].astype(BF16)
    gw = (0.5 * jnp.concatenate([gaw_ref[...], gxw_ref[...]], axis=1)).astype(BF16)
    w_out_rows = wout_ref[...].astype(BF16)
    wxr_b_ref[...] = w_xr
    wgate_b_ref[...] = w_gate
    gw_b_ref[...] = gw
    wout_b_ref[...] = w_out_rows

    for seg, c_in, c_out, hi, ho in zip(segs, conv_in, conv_out, h_in, h_out):
        P = (CONV_A - 1) * seg.S
        rows = slice(seg.row0, seg.row0 + seg.rows)
        xn = xn_scr[rows, :]
        xr_scr[seg.conv0:seg.conv0 + P, :] = c_in[...]
        _rglru_branch_in(seg, xn, w_xr, xr_scr)
        part = _rglru_colblock(
            seg, xn, w_gate, gw, cw_ref[...], cb_ref[j:j + 1, :], 0.5 * gab_ref[pl.ds(b, 1), :],
            0.5 * gxb_ref[pl.ds(b, 1), :], lam_ref[j:j + 1, :], w_out_rows, xr_scr, hi, ho, 0)
        c_out[...] = xr_scr[seg.conv0:seg.conv0 + P, :]
        _accumulate(y_scr, rows, part, b)

    @pl.when(b == N_COL_BLOCKS - 1)
    def _():
        o_ref[...] = x_ref[...] + _rms(y_scr[...], npost_ref[layer:layer + 1, :])


def _rglru_stream_layer(x, conv_in, h_in, layer, j, w, *, seqs_and_steps):
    R, D = x.shape
    C = COL_BLOCK
    segs, _, conv_rows, _ = _segments(seqs_and_steps, CONV_A)
    n_layers, n_a = w["norm_mix_pre"].shape[0], w["rg_conv_b"].shape[0]
    col = lambda rows: pl.BlockSpec((rows, C), lambda b: (0, b))
    in_specs = (
        [_resident((R, D))] + [col(c.shape[0]) for c in conv_in] + [col(h.shape[0]) for h in h_in]
        + [_resident((n_layers, D)), _resident((n_layers, D)),
           pl.BlockSpec((None, D, C), lambda b: (j, 0, N_COL_BLOCKS + b)),
           pl.BlockSpec((None, D, C), lambda b: (j, 0, b)),
           pl.BlockSpec((None, CONV_A, C), lambda b: (j, 0, b)),
           col(n_a),
           pl.BlockSpec((None, None, C, C), lambda b: (j, b, 0, 0)),
           pl.BlockSpec((None, None, C, C), lambda b: (j, b, 0, 0)),
           pl.BlockSpec((None, N_GATE_BLOCKS, C), lambda b: (j, 0, 0)),
           pl.BlockSpec((None, N_GATE_BLOCKS, C), lambda b: (j, 0, 0)),
           col(n_a),
           pl.BlockSpec((None, C, D), lambda b: (j, b, 0))])
    out_specs = ([pl.BlockSpec((R, D), lambda b: (0, 0))] + [col(c.shape[0]) for c in conv_in]
                 + [col(h.shape[0]) for h in h_in]
                 + [col(D), col(D), pl.BlockSpec((None, C, 2 * C), lambda b: (b, 0, 0)),
                    pl.BlockSpec((C, D), lambda b: (b, 0))])
    out_shape = ([jax.ShapeDtypeStruct((R, D), F32)] + [jax.ShapeDtypeStruct(s.shape, F32) for s in conv_in + h_in]
                 + [jax.ShapeDtypeStruct((D, D), BF16), jax.ShapeDtypeStruct((D, D), BF16),
                    jax.ShapeDtypeStruct((N_GATE_BLOCKS, C, 2 * C), BF16), jax.ShapeDtypeStruct((D, D), BF16)])
    outs = pl.pallas_call(
        functools.partial(_rglru_stream_kernel, segs, layer, j),
        grid=(N_COL_BLOCKS,),
        in_specs=in_specs, out_specs=out_specs, out_shape=out_shape,
        scratch_shapes=[pltpu.VMEM((R, D), BF16),
                        pltpu.VMEM((conv_rows, C), F32),
                        pltpu.VMEM((R, D), F32)],
        compiler_params=_params(),
        name="rglru_mixer_small",
    )(x, *conv_in, *h_in, w["norm_mix_pre"], w["norm_mix_post"], w["rg_w_in"], w["rg_w_in"], w["rg_conv_w"],
      w["rg_conv_b"], w["rg_gate_a_w"], w["rg_gate_x_w"], w["rg_gate_a_b"], w["rg_gate_x_b"], w["rg_lambda"],
      w["rg_w_out"])
    n = len(segs)
    bf16 = dict(w_xr=outs[2 * n + 1], w_gate=outs[2 * n + 2], gw=outs[2 * n + 3], w_out=outs[2 * n + 4])
    return outs[0], list(outs[1:1 + n]), list(outs[1 + n:1 + 2 * n]), bf16


def _rglru_kernel(seg, nblk, layer, j, batch_major_in,
                  x_ref, conv_in_ref, h_in_ref, npre_ref, npost_ref, wxr_ref, wgate_ref, cw_ref, cb_ref,
                  gw_ref, gab_ref, gxb_ref, lam_ref, wout_ref,
                  o_ref, conv_out_ref, h_out_ref, xn_scr, xr_scr, h_scr, *slab_scr):
    P = (CONV_A - 1) * seg.S
    i = pl.program_id(0)
    blocks = [slice(b * COL_BLOCK, (b + 1) * COL_BLOCK) for b in range(N_COL_BLOCKS)]

    @pl.when(i == 0)
    def _():
        for b, cols in enumerate(blocks):
            xr_scr[b, 0:P, :] = conv_in_ref[:, cols]
            h_scr[b] = h_in_ref[:, cols]

    if batch_major_in:
        _scatter_time_major(x_ref, slab_scr[0], seg.S, seg.TB)
        read_x = lambda: _read_slabs(slab_scr[0])
    else:
        read_x = lambda: x_ref[...]

    xn_scr[...] = _rms(read_x(), npre_ref[layer:layer + 1, :]).astype(BF16)
    xn = xn_scr[...]

    y = None
    _rglru_branch_in(seg, xn, wxr_ref[:, blocks[0]], xr_scr.at[0])
    for b, cols in enumerate(blocks):
        if b + 1 < N_COL_BLOCKS:
            _rglru_branch_in(seg, xn, wxr_ref[:, blocks[b + 1]], xr_scr.at[b + 1])
        part = _rglru_colblock(
            seg, xn, wgate_ref[:, cols], gw_ref[b], cw_ref[:, cols], cb_ref[j:j + 1, cols],
            0.5 * gab_ref[b:b + 1, :], 0.5 * gxb_ref[b:b + 1, :], lam_ref[j:j + 1, cols], wout_ref[cols, :],
            xr_scr.at[b], h_scr.at[b], h_scr.at[b], 0)
        y = part if y is None else y + part

    o_ref[...] = read_x() + _rms(y, npost_ref[layer:layer + 1, :])

    @pl.when(i == nblk - 1)
    def _():
        for b, cols in enumerate(blocks):
            conv_out_ref[:, cols] = xr_scr[b, 0:P, :]
            h_out_ref[:, cols] = h_scr[b]


def _rglru_layer(x, conv_in, h_in, layer, j, w, bf16, *, S, TB):
    D = D_MODEL
    batch_major_in = x.ndim == 3
    (seg,), R, conv_rows, _ = _segments(((S, TB),), CONV_A)
    nblk = x.size // D // R
    P = (CONV_A - 1) * S
    n_layers, n_a = w["norm_mix_pre"].shape[0], w["rg_conv_b"].shape[0]
    row_spec = pl.BlockSpec((R, D), lambda i: (i, 0))
    x_spec = pl.BlockSpec((S, TB, D), lambda i: (0, i, 0)) if batch_major_in else row_spec
    layer_of = lambda arr: pl.BlockSpec((None,) + arr.shape[1:], lambda i: (j,) + (0,) * (arr.ndim - 1),
                                        pipeline_mode=pl.Buffered(1))
    return pl.pallas_call(
        functools.partial(_rglru_kernel, seg, nblk, layer, j, batch_major_in),
        grid=(nblk,),
        in_specs=[x_spec, _resident((P, D)), _resident((S, D)),
                  _resident((n_layers, D)), _resident((n_layers, D)), _resident((D, D)), _resident((D, D)),
                  layer_of(w["rg_conv_w"]), _resident((n_a, D)), _resident(bf16["gw"].shape),
                  layer_of(w["rg_gate_a_b"]), layer_of(w["rg_gate_x_b"]), _resident((n_a, D)),
                  _resident((D, D))],
        out_specs=[row_spec, pl.BlockSpec((P, D), lambda i: (0, 0)), pl.BlockSpec((S, D), lambda i: (0, 0))],
        out_shape=[jax.ShapeDtypeStruct((nblk * R, D), F32), jax.ShapeDtypeStruct((P, D), F32),
                   jax.ShapeDtypeStruct((S, D), F32)],
        scratch_shapes=[pltpu.VMEM((R, D), BF16),
                        pltpu.VMEM((N_COL_BLOCKS, conv_rows, COL_BLOCK), F32),
                        pltpu.VMEM((N_COL_BLOCKS, S, COL_BLOCK), F32)]
                       + ([_slab_scratch(R)] if batch_major_in else []),
        compiler_params=_params(),
        name="rglru_mixer",
    )(x, conv_in, h_in, w["norm_mix_pre"], w["norm_mix_post"], bf16["w_xr"], bf16["w_gate"], w["rg_conv_w"],
      w["rg_conv_b"], bf16["gw"], w["rg_gate_a_b"], w["rg_gate_x_b"], w["rg_lambda"], bf16["w_out"])


def _sconv_colblock(seg, xn, w_bg, w_cg, w_v, cw, region):
    S, R = seg.S, seg.rows
    P = (CONV_B - 1) * S
    c0 = seg.conv0
    region[c0 + P:c0 + P + R, :] = _dot(xn, w_cg) * _dot(xn, w_v)
    conv = region[c0:c0 + R, :] * cw[0:1]
    for k in range(1, CONV_B):
        conv = conv + region[c0 + k * S:c0 + k * S + R, :] * cw[k:k + 1]
    region[c0:c0 + P, :] = region[c0 + R:c0 + R + P, :]
    return (_dot(xn, w_bg) * conv).astype(BF16)


def _sconv_stream_kernel(segs, layer, j, x_ref, *refs):
    n = len(segs)
    conv_in = refs[0:n]
    npre_ref, npost_ref, wbg_ref, wcg_ref, wv_ref, cw_ref, wout_ref = refs[n:n + 7]
    o_ref = refs[n + 7]
    conv_out = refs[n + 8:2 * n + 8]
    wbg_b_ref, wcg_b_ref, wv_b_ref, wout_b_ref = refs[2 * n + 8:2 * n + 12]
    xn_scr, cv_scr, y_scr = refs[2 * n + 12:]
    b = pl.program_id(0)

    @pl.when(b == 0)
    def _():
        xn_scr[...] = _rms(x_ref[...], npre_ref[layer:layer + 1, :]).astype(BF16)

    w_bg = wbg_ref[...].astype(BF16)
    w_cg = wcg_ref[...].astype(BF16)
    w_v = wv_ref[...].astype(BF16)
    w_out_rows = wout_ref[...].astype(BF16)
    wbg_b_ref[...] = w_bg
    wcg_b_ref[...] = w_cg
    wv_b_ref[...] = w_v
    wout_b_ref[...] = w_out_rows

    for seg, c_in, c_out in zip(segs, conv_in, conv_out):
        P = (CONV_B - 1) * seg.S
        rows = slice(seg.row0, seg.row0 + seg.rows)
        cv_scr[seg.conv0:seg.conv0 + P, :] = c_in[...]
        m = _sconv_colblock(seg, xn_scr[rows, :], w_bg, w_cg, w_v, cw_ref[...], cv_scr)
        c_out[...] = cv_scr[seg.conv0:seg.conv0 + P, :]
        _accumulate(y_scr, rows, _dot(m, w_out_rows), b)

    @pl.when(b == N_COL_BLOCKS - 1)
    def _():
        o_ref[...] = x_ref[...] + _rms(y_scr[...], npost_ref[layer:layer + 1, :])


def _sconv_stream_layer(x, conv_in, layer, j, w, *, seqs_and_steps):
    R, D = x.shape
    C = COL_BLOCK
    segs, _, conv_rows, _ = _segments(seqs_and_steps, CONV_B)
    n_layers = w["norm_mix_pre"].shape[0]
    col = lambda rows: pl.BlockSpec((rows, C), lambda b: (0, b))
    w_in_part = lambda k: pl.BlockSpec((None, D, C), lambda b: (j, 0, k * N_COL_BLOCKS + b))
    outs = pl.pallas_call(
        functools.partial(_sconv_stream_kernel, segs, layer, j),
        grid=(N_COL_BLOCKS,),
        in_specs=[_resident((R, D))] + [col(c.shape[0]) for c in conv_in]
                 + [_resident((n_layers, D)), _resident((n_layers, D)), w_in_part(0), w_in_part(1), w_in_part(2),
                    pl.BlockSpec((None, CONV_B, C), lambda b: (j, 0, b)),
                    pl.BlockSpec((None, C, D), lambda b: (j, b, 0))],
        out_specs=[pl.BlockSpec((R, D), lambda b: (0, 0))] + [col(c.shape[0]) for c in conv_in]
                  + [col(D), col(D), col(D), pl.BlockSpec((C, D), lambda b: (b, 0))],
        out_shape=[jax.ShapeDtypeStruct((R, D), F32)] + [jax.ShapeDtypeStruct(c.shape, F32) for c in conv_in]
                  + [jax.ShapeDtypeStruct((D, D), BF16)] * 4,
        scratch_shapes=[pltpu.VMEM((R, D), BF16),
                        pltpu.VMEM((conv_rows, C), F32),
                        pltpu.VMEM((R, D), F32)],
        compiler_params=_params(),
        name="sconv_mixer_small",
    )(x, *conv_in, w["norm_mix_pre"], w["norm_mix_post"], w["sc_w_in"], w["sc_w_in"], w["sc_w_in"],
      w["sc_conv_w"], w["sc_w_out"])
    n = len(segs)
    bf16 = dict(w_bg=outs[n + 1], w_cg=outs[n + 2], w_v=outs[n + 3], w_out=outs[n + 4])
    return outs[0], list(outs[1:1 + n]), bf16


def _sconv_kernel(seg, nblk, layer, x_ref, conv_in_ref, npre_ref, npost_ref, wbg_ref, wcg_ref, wv_ref,
                  cw_ref, wout_ref, o_ref, conv_out_ref, xn_scr, cv_scr, m_scr):
    P = (CONV_B - 1) * seg.S
    i = pl.program_id(0)
    blocks = [slice(b * COL_BLOCK, (b + 1) * COL_BLOCK) for b in range(N_COL_BLOCKS)]

    @pl.when(i == 0)
    def _():
        for b, cols in enumerate(blocks):
            cv_scr[b, 0:P, :] = conv_in_ref[:, cols]

    xn_scr[...] = _rms(x_ref[...], npre_ref[layer:layer + 1, :]).astype(BF16)
    xn = xn_scr[...]
    for b, cols in enumerate(blocks):
        m_scr[:, cols] = _sconv_colblock(seg, xn, wbg_ref[:, cols], wcg_ref[:, cols], wv_ref[:, cols],
                                         cw_ref[:, cols], cv_scr.at[b])
    y = _dot(m_scr[...], wout_ref[...])
    o_ref[...] = x_ref[...] + _rms(y, npost_ref[layer:layer + 1, :])

    @pl.when(i == nblk - 1)
    def _():
        for b, cols in enumerate(blocks):
            conv_out_ref[:, cols] = cv_scr[b, 0:P, :]


def _sconv_layer(x, conv_in, layer, j, w, bf16, *, S, TB):
    N, D = x.shape
    (seg,), R, conv_rows, _ = _segments(((S, TB),), CONV_B)
    nblk = N // R
    P = (CONV_B - 1) * S
    n_layers = w["norm_mix_pre"].shape[0]
    row_spec = pl.BlockSpec((R, D), lambda i: (i, 0))
    return pl.pallas_call(
        functools.partial(_sconv_kernel, seg, nblk, layer),
        grid=(nblk,),
        in_specs=[row_spec, _resident((P, D)), _resident((n_layers, D)), _resident((n_layers, D)),
                  _resident((D, D)), _resident((D, D)), _resident((D, D)),
                  pl.BlockSpec((None, CONV_B, D), lambda i: (j, 0, 0), pipeline_mode=pl.Buffered(1)),
                  _resident((D, D))],
        out_specs=[row_spec, pl.BlockSpec((P, D), lambda i: (0, 0))],
        out_shape=[jax.ShapeDtypeStruct((N, D), F32), jax.ShapeDtypeStruct((P, D), F32)],
        scratch_shapes=[pltpu.VMEM((R, D), BF16),
                        pltpu.VMEM((N_COL_BLOCKS, conv_rows, COL_BLOCK), F32),
                        pltpu.VMEM((R, D), BF16)],
        compiler_params=_params(),
        name="sconv_mixer",
    )(x, conv_in, w["norm_mix_pre"], w["norm_mix_post"], bf16["w_bg"], bf16["w_cg"], bf16["w_v"],
      w["sc_conv_w"], bf16["w_out"])


def _swiglu(g, u):
    return (g * jax.nn.sigmoid(g) * u).astype(BF16)


def _ffn_stream_kernel(layer, nchunk, x_ref, npre_ref, npost_ref, wg_ref, wu_ref, wd_ref,
                       o_ref, wg_b_ref, wu_b_ref, wd_b_ref, xn_scr, y_scr):
    c = pl.program_id(0)

    @pl.when(c == 0)
    def _():
        xn_scr[...] = _rms(x_ref[...], npre_ref[layer:layer + 1, :]).astype(BF16)

    wg = wg_ref[...].astype(BF16)
    wu = wu_ref[...].astype(BF16)
    wd = wd_ref[...].astype(BF16)
    wg_b_ref[...] = wg
    wu_b_ref[...] = wu
    wd_b_ref[...] = wd
    xn = xn_scr[...]
    part = _dot(_swiglu(_dot(xn, wg), _dot(xn, wu)), wd)
    _accumulate(y_scr, slice(None), part, c)

    @pl.when(c == nchunk - 1)
    def _():
        o_ref[...] = x_ref[...] + _rms(y_scr[...], npost_ref[layer:layer + 1, :])


def _ffn_stream_layer(x, layer, w):
    R, D = x.shape
    n_layers = w["norm_ffn_pre"].shape[0]
    nchunk = D_FF // FF_CHUNK
    cols = lambda: pl.BlockSpec((None, D, FF_CHUNK), lambda c: (layer, 0, c))
    outs = pl.pallas_call(
        functools.partial(_ffn_stream_kernel, layer, nchunk),
        grid=(nchunk,),
        in_specs=[_resident((R, D)), _resident((n_layers, D)), _resident((n_layers, D)), cols(), cols(),
                  pl.BlockSpec((None, FF_CHUNK, D), lambda c: (layer, c, 0))],
        out_specs=[pl.BlockSpec((R, D), lambda c: (0, 0)),
                   pl.BlockSpec((D, FF_CHUNK), lambda c: (0, c)), pl.BlockSpec((D, FF_CHUNK), lambda c: (0, c)),
                   pl.BlockSpec((FF_CHUNK, D), lambda c: (c, 0))],
        out_shape=[jax.ShapeDtypeStruct((R, D), F32), jax.ShapeDtypeStruct((D, D_FF), BF16),
                   jax.ShapeDtypeStruct((D, D_FF), BF16), jax.ShapeDtypeStruct((D_FF, D), BF16)],
        scratch_shapes=[pltpu.VMEM((R, D), BF16), pltpu.VMEM((R, D), F32)],
        compiler_params=_params(),
        name="swiglu_ffn_small",
    )(x, w["norm_ffn_pre"], w["norm_ffn_post"], w["ffn_w_gate"], w["ffn_w_up"], w["ffn_w_down"])
    return outs[0], dict(wg=outs[1], wu=outs[2], wd=outs[3])


def _ffn_kernel(layer, batch_major_out, x_ref, npre_ref, npost_ref, wg_ref, wu_ref, wd_ref, o_ref,
                xn_scr, act_scr, *slab_scr):
    xn_scr[...] = _rms(x_ref[...], npre_ref[layer:layer + 1, :]).astype(BF16)
    xn = xn_scr[...]
    for c0 in range(0, D_FF, FF_CHUNK):
        cols = slice(c0, c0 + FF_CHUNK)
        act_scr[:, cols] = _swiglu(_dot(xn, wg_ref[:, cols]), _dot(xn, wu_ref[:, cols]))
    y = _dot(act_scr[...], wd_ref[...])
    out = x_ref[...] + _rms(y, npost_ref[layer:layer + 1, :])
    if batch_major_out:
        S, TB, _ = o_ref.shape
        _store_batch_major(out, slab_scr[0], o_ref, S, TB)
    else:
        o_ref[...] = out


def _ffn_layer(x, layer, w, bf16, *, R, batch_major_seqs=None):
    N, D = x.shape
    n_layers = w["norm_ffn_pre"].shape[0]
    row_spec = pl.BlockSpec((R, D), lambda i: (i, 0))
    if batch_major_seqs is None:
        out_spec, out_shape, slabs = row_spec, jax.ShapeDtypeStruct((N, D), F32), []
    else:
        S = batch_major_seqs
        out_spec = pl.BlockSpec((S, R // S, D), lambda i: (0, i, 0))
        out_shape, slabs = jax.ShapeDtypeStruct((S, N // S, D), F32), [_slab_scratch(R)]
    return pl.pallas_call(
        functools.partial(_ffn_kernel, layer, batch_major_seqs is not None),
        grid=(N // R,),
        in_specs=[row_spec, _resident((n_layers, D)), _resident((n_layers, D)),
                  _resident((D, D_FF)), _resident((D, D_FF)), _resident((D_FF, D))],
        out_specs=out_spec,
        out_shape=out_shape,
        scratch_shapes=[pltpu.VMEM((R, D), BF16), pltpu.VMEM((R, D_FF), BF16)] + slabs,
        compiler_params=_params(),
        name="swiglu_ffn",
    )(x, w["norm_ffn_pre"], w["norm_ffn_post"], bf16["wg"], bf16["wu"], bf16["wd"])


def _to_time_major(a):
    S, K, D = a.shape
    return jnp.swapaxes(a, 0, 1).reshape(K * S, D)


def _from_time_major(a, S):
    KS, D = a.shape
    return jnp.swapaxes(a.reshape(KS // S, S, D), 0, 1)


def kernel(x_prompt, x_sample, state_rglru_conv, state_rglru_h, state_sconv, meta_tokens, norm_mix_pre, norm_mix_post, norm_ffn_pre, norm_ffn_post, rg_w_in, rg_conv_w, rg_conv_b, rg_gate_a_w, rg_gate_a_b, rg_gate_x_w, rg_gate_x_b, rg_lambda, rg_w_out, sc_w_in, sc_conv_w, sc_w_out, ffn_w_gate, ffn_w_up, ffn_w_down):
    D = D_MODEL
    depth = norm_mix_pre.shape[0]
    batch, seq, _ = x_prompt.shape
    dec_batch, dec_seq, _ = x_sample.shape
    w = dict(norm_mix_pre=norm_mix_pre, norm_mix_post=norm_mix_post, norm_ffn_pre=norm_ffn_pre,
             norm_ffn_post=norm_ffn_post, rg_w_in=rg_w_in, rg_conv_w=rg_conv_w, rg_conv_b=rg_conv_b,
             rg_gate_a_w=rg_gate_a_w, rg_gate_a_b=rg_gate_a_b, rg_gate_x_w=rg_gate_x_w, rg_gate_x_b=rg_gate_x_b,
             rg_lambda=rg_lambda, rg_w_out=rg_w_out, sc_w_in=sc_w_in, sc_conv_w=sc_conv_w, sc_w_out=sc_w_out,
             ffn_w_gate=ffn_w_gate, ffn_w_up=ffn_w_up, ffn_w_down=ffn_w_down)

    xm = jnp.broadcast_to(meta_tokens[:, None, :], (N_META, batch, D)).reshape(N_META * batch, D)
    x = jnp.concatenate([xm, _to_time_major(x_sample)], axis=0)
    small = ((batch, N_META), (dec_batch, dec_seq))
    mixer_bf16, ffn_bf16 = [], []
    rg_conv_s, rg_h_s, sc_s = [], [], []
    for i in range(depth):
        j = i // 2
        if i % 2 == 0:
            conv0 = [jnp.zeros(((CONV_A - 1) * batch, D), F32), _to_time_major(state_rglru_conv[j])]
            h0 = [jnp.zeros((batch, D), F32), state_rglru_h[j]]
            x, cb, hT, wb = _rglru_stream_layer(x, conv0, h0, i, j, w, seqs_and_steps=small)
            rg_conv_s.append(cb)
            rg_h_s.append(hT)
        else:
            conv0 = [jnp.zeros(((CONV_B - 1) * batch, D), F32), _to_time_major(state_sconv[j])]
            x, cb, wb = _sconv_stream_layer(x, conv0, i, j, w, seqs_and_steps=small)
            sc_s.append(cb)
        mixer_bf16.append(wb)
        x, wb = _ffn_stream_layer(x, i, w)
        ffn_bf16.append(wb)
    y_sample = _from_time_major(x[N_META * batch:], dec_batch)

    x = x_prompt
    rg_conv_p, rg_h_p, sc_p = [], [], []
    for i in range(depth):
        j = i // 2
        if i % 2 == 0:
            x, cb, hT = _rglru_layer(x, rg_conv_s[j][0], rg_h_s[j][0], i, j, w, mixer_bf16[i], S=batch, TB=PROMPT_TB)
            rg_conv_p.append(cb)
            rg_h_p.append(hT)
        else:
            x, cb = _sconv_layer(x, sc_s[j][0], i, j, w, mixer_bf16[i], S=batch, TB=PROMPT_TB)
            sc_p.append(cb)
        x = _ffn_layer(x, i, w, ffn_bf16[i], R=batch * PROMPT_TB,
                       batch_major_seqs=batch if i == depth - 1 else None)
    y_prompt = x

    return (y_prompt, y_sample,
            jnp.stack([_from_time_major(c, batch) for c in rg_conv_p]), jnp.stack(rg_h_p),
            jnp.stack([_from_time_major(c, batch) for c in sc_p]),
            jnp.stack([_from_time_major(c[1], dec_batch) for c in rg_conv_s]), jnp.stack([h[1] for h in rg_h_s]),
            jnp.stack([_from_time_major(c[1], dec_batch) for c in sc_s]))
```

```python
import functools
from typing import NamedTuple

import jax
import jax.numpy as jnp
from jax import lax
from jax.experimental import pallas as pl
from jax.experimental.pallas import tpu as pltpu

D_MODEL = 1024
D_FF = 2816
N_META = 16
N_GATE_BLOCKS = 4
GATE_BLOCK_W = 256
CONV_A = 4
CONV_B = 3
RG_C = 8.0
EPS = 1e-6

SUBLANES = 8
LANES = 128
COL_BLOCK = 256
N_COL_BLOCKS = D_MODEL // COL_BLOCK
FF_CHUNK = 256
PROMPT_TB = 128
VMEM_LIMIT_BYTES = 56 * 1024 * 1024

F32 = jnp.float32
BF16 = jnp.bfloat16


class _Seg(NamedTuple):
    S: int
    TB: int
    row0: int
    conv0: int
    h0: int

    @property
    def rows(self):
        return self.S * self.TB


def _segments(seqs_and_steps, taps):
    segs, row0, conv0, h0 = [], 0, 0, 0
    for S, TB in seqs_and_steps:
        segs.append(_Seg(S, TB, row0, conv0, h0))
        row0 += S * TB
        conv0 += (taps - 1) * S + S * TB
        h0 += S
    return tuple(segs), row0, conv0, h0


def _rms(x, w):
    ms = jnp.mean(x * x, axis=-1, keepdims=True)
    return x * lax.rsqrt(ms + EPS) * w


def _dot(a, b):
    return jnp.dot(a, b, preferred_element_type=F32)


def _scatter_time_major(x_ref, slab_scr, S, TB):
    for s in range(S):
        for j in range(D_MODEL // LANES):
            slab_scr[j, pl.ds(s, TB, stride=S), :] = x_ref[s, :, j * LANES:(j + 1) * LANES]


def _read_slabs(slab_scr):
    return jnp.concatenate([slab_scr[j] for j in range(D_MODEL // LANES)], axis=1)


def _store_batch_major(y, slab_scr, o_ref, S, TB):
    for j in range(D_MODEL // LANES):
        slab_scr[j] = y[:, j * LANES:(j + 1) * LANES]
    for s in range(S):
        for j in range(D_MODEL // LANES):
            o_ref[s, :, j * LANES:(j + 1) * LANES] = slab_scr[j, pl.ds(s, TB, stride=S), :]


def _gelu_tanh(x):
    c = 0.7978845608028654
    hx = 0.5 * x
    return hx + hx * jnp.tanh(x * (c + (c * 0.044715) * (x * x)))


def _resident(shape):
    zeros = (0,) * len(shape)
    return pl.BlockSpec(shape, lambda i: zeros, pipeline_mode=pl.Buffered(1))


def _slab_scratch(R):
    return pltpu.VMEM((D_MODEL // LANES, R, LANES), F32)


def _params():
    return pltpu.CompilerParams(dimension_semantics=("arbitrary",), vmem_limit_bytes=VMEM_LIMIT_BYTES)


def _accumulate(y_scr, rows, part, step):
    @pl.when(step == 0)
    def _():
        y_scr[rows, :] = part

    @pl.when(step > 0)
    def _():
        y_scr[rows, :] += part


def _rglru_branch_in(seg, xn, w_xr, region):
    P = (CONV_A - 1) * seg.S
    region[seg.conv0 + P:seg.conv0 + P + seg.rows, :] = _dot(xn, w_xr)


def _rglru_colblock(seg, xn, w_gate, gw, cw, cb, half_gab, half_gxb, lam, w_out_rows,
                    region, h_read, h_write, h_row0):
    S, TB, R = seg.S, seg.TB, seg.rows
    P = (CONV_A - 1) * S
    c0 = seg.conv0
    groups = S // SUBLANES

    xc = region[c0:c0 + R, :] * cw[0:1]
    for k in range(1, CONV_A):
        xc = xc + region[c0 + k * S:c0 + k * S + R, :] * cw[k:k + 1]
    xc = xc + cb
    region[c0:c0 + P, :] = region[c0 + R:c0 + R + P, :]

    res = _dot(xc.astype(BF16), gw)
    half_c_sp = (-0.5 * RG_C) * jax.nn.softplus(-lam)
    tr = jnp.tanh(res[:, 0:COL_BLOCK] + half_gab)
    log_a = half_c_sp * tr + half_c_sp
    ig = 0.5 * jnp.tanh(res[:, COL_BLOCK:2 * COL_BLOCK] + half_gxb) + 0.5
    a = jnp.exp(log_a)
    m2 = jnp.tanh(log_a) * (-1.0 - a * a)
    u = jnp.where(m2 > 0.0, m2 * lax.rsqrt(m2), 0.0) * (ig * xc)

    g = _gelu_tanh(_dot(xn, w_gate))

    pieces = [None] * (TB * groups)
    for c in range(groups):
        hrows = slice(h_row0 + c * SUBLANES, h_row0 + (c + 1) * SUBLANES)
        h = h_read[hrows, :]
        for t in range(TB):
            r = t * S + c * SUBLANES
            h = a[r:r + SUBLANES] * h + u[r:r + SUBLANES]
            pieces[t * groups + c] = h * g[r:r + SUBLANES]
        h_write[hrows, :] = h
    hs = jnp.concatenate(pieces, axis=0)
    return _dot(hs.astype(BF16), w_out_rows)


def _rglru_stream_kernel(segs, layer, j, x_ref, *refs):
    n = len(segs)
    conv_in, h_in = refs[0:n], refs[n:2 * n]
    (npre_ref, npost_ref, wxr_ref, wgate_ref, cw_ref, cb_ref, gaw_ref, gxw_ref, gab_ref, gxb_ref,
     lam_ref, wout_ref) = refs[2 * n:2 * n + 12]
    o_ref = refs[2 * n + 12]
    conv_out, h_out = refs[2 * n + 13:3 * n + 13], refs[3 * n + 13:4 * n + 13]
    wxr_b_ref, wgate_b_ref, gw_b_ref, wout_b_ref = refs[4 * n + 13:4 * n + 17]
    xn_scr, xr_scr, y_scr = refs[4 * n + 17:]
    b = pl.program_id(0)

    @pl.when(b == 0)
    def _():
        xn_scr[...] = _rms(x_ref[...], npre_ref[layer:layer + 1, :]).astype(BF16)

    w_xr = wxr_ref[...].astype(BF16)
    w_gate = wgate_ref[...].astype(BF16)
    gw = (0.5 * jnp.concatenate([gaw_ref[...], gxw_ref[...]], axis=1)).astype(BF16)
    w_out_rows = wout_ref[...].astype(BF16)
    wxr_b_ref[...] = w_xr
    wgate_b_ref[...] = w_gate
    gw_b_ref[...] = gw
    wout_b_ref[...] = w_out_rows

    for seg, c_in, c_out, hi, ho in zip(segs, conv_in, conv_out, h_in, h_out):
        P = (CONV_A - 1) * seg.S
        rows = slice(seg.row0, seg.row0 + seg.rows)
        xn = xn_scr[rows, :]
        xr_scr[seg.conv0:seg.conv0 + P, :] = c_in[...]
        _rglru_branch_in(seg, xn, w_xr, xr_scr)
        part = _rglru_colblock(
            seg, xn, w_gate, gw, cw_ref[...], cb_ref[j:j + 1, :], 0.5 * gab_ref[pl.ds(b, 1), :],
            0.5 * gxb_ref[pl.ds(b, 1), :], lam_ref[j:j + 1, :], w_out_rows, xr_scr, hi, ho, 0)
        c_out[...] = xr_scr[seg.conv0:seg.conv0 + P, :]
        _accumulate(y_scr, rows, part, b)

    @pl.when(b == N_COL_BLOCKS - 1)
    def _():
        o_ref[...] = x_ref[...] + _rms(y_scr[...], npost_ref[layer:layer + 1, :])


def _rglru_stream_layer(x, conv_in, h_in, layer, j, w, *, seqs_and_steps):
    R, D = x.shape
    C = COL_BLOCK
    segs, _, conv_rows, _ = _segments(seqs_and_steps, CONV_A)
    n_layers, n_a = w["norm_mix_pre"].shape[0], w["rg_conv_b"].shape[0]
    col = lambda rows: pl.BlockSpec((rows, C), lambda b: (0, b))
    in_specs = (
        [_resident((R, D))] + [col(c.shape[0]) for c in conv_in] + [col(h.shape[0]) for h in h_in]
        + [_resident((n_layers, D)), _resident((n_layers, D)),
           pl.BlockSpec((None, D, C), lambda b: (j, 0, N_COL_BLOCKS + b)),
           pl.BlockSpec((None, D, C), lambda b: (j, 0, b)),
           pl.BlockSpec((None, CONV_A, C), lambda b: (j, 0, b)),
           col(n_a),
           pl.BlockSpec((None, None, C, C), lambda b: (j, b, 0, 0)),
           pl.BlockSpec((None, None, C, C), lambda b: (j, b, 0, 0)),
           pl.BlockSpec((None, N_GATE_BLOCKS, C), lambda b: (j, 0, 0)),
           pl.BlockSpec((None, N_GATE_BLOCKS, C), lambda b: (j, 0, 0)),
           col(n_a),
           pl.BlockSpec((None, C, D), lambda b: (j, b, 0))])
    out_specs = ([pl.BlockSpec((R, D), lambda b: (0, 0))] + [col(c.shape[0]) for c in conv_in]
                 + [col(h.shape[0]) for h in h_in]
                 + [col(D), col(D), pl.BlockSpec((None, C, 2 * C), lambda b: (b, 0, 0)),
                    pl.BlockSpec((C, D), lambda b: (b, 0))])
    out_shape = ([jax.ShapeDtypeStruct((R, D), F32)] + [jax.ShapeDtypeStruct(s.shape, F32) for s in conv_in + h_in]
                 + [jax.ShapeDtypeStruct((D, D), BF16), jax.ShapeDtypeStruct((D, D), BF16),
                    jax.ShapeDtypeStruct((N_GATE_BLOCKS, C, 2 * C), BF16), jax.ShapeDtypeStruct((D, D), BF16)])
    outs = pl.pallas_call(
        functools.partial(_rglru_stream_kernel, segs, layer, j),
        grid=(N_COL_BLOCKS,),
        in_specs=in_specs, out_specs=out_specs, out_shape=out_shape,
        scratch_shapes=[pltpu.VMEM((R, D), BF16),
                        pltpu.VMEM((conv_rows, C), F32),
                        pltpu.VMEM((R, D), F32)],
        compiler_params=_params(),
        name="rglru_mixer_small",
    )(x, *conv_in, *h_in, w["norm_mix_pre"], w["norm_mix_post"], w["rg_w_in"], w["rg_w_in"], w["rg_conv_w"],
      w["rg_conv_b"], w["rg_gate_a_w"], w["rg_gate_x_w"], w["rg_gate_a_b"], w["rg_gate_x_b"], w["rg_lambda"],
      w["rg_w_out"])
    n = len(segs)
    bf16 = dict(w_xr=outs[2 * n + 1], w_gate=outs[2 * n + 2], gw=outs[2 * n + 3], w_out=outs[2 * n + 4])
    return outs[0], list(outs[1:1 + n]), list(outs[1 + n:1 + 2 * n]), bf16


def _rglru_kernel(seg, nblk, layer, j, batch_major_in,
                  x_ref, conv_in_ref, h_in_ref, npre_ref, npost_ref, wxr_ref, wgate_ref, cw_ref, cb_ref,
                  gw_ref, gab_ref, gxb_ref, lam_ref, wout_ref,
                  o_ref, conv_out_ref, h_out_ref, xn_scr, xr_scr, h_scr, *slab_scr):
    P = (CONV_A - 1) * seg.S
    i = pl.program_id(0)
    blocks = [slice(b * COL_BLOCK, (b + 1) * COL_BLOCK) for b in range(N_COL_BLOCKS)]

    @pl.when(i == 0)
    def _():
        for b, cols in enumerate(blocks):
            xr_scr[b, 0:P, :] = conv_in_ref[:, cols]
            h_scr[b] = h_in_ref[:, cols]

    if batch_major_in:
        _scatter_time_major(x_ref, slab_scr[0], seg.S, seg.TB)
        read_x = lambda: _read_slabs(slab_scr[0])
    else:
        read_x = lambda: x_ref[...]

    xn_scr[...] = _rms(read_x(), npre_ref[layer:layer + 1, :]).astype(BF16)
    xn = xn_scr[...]

    y = None
    _rglru_branch_in(seg, xn, wxr_ref[:, blocks[0]], xr_scr.at[0])
    for b, cols in enumerate(blocks):
        if b + 1 < N_COL_BLOCKS:
            _rglru_branch_in(seg, xn, wxr_ref[:, blocks[b + 1]], xr_scr.at[b + 1])
        part = _rglru_colblock(
            seg, xn, wgate_ref[:, cols], gw_ref[b], cw_ref[:, cols], cb_ref[j:j + 1, cols],
            0.5 * gab_ref[b:b + 1, :], 0.5 * gxb_ref[b:b + 1, :], lam_ref[j:j + 1, cols], wout_ref[cols, :],
            xr_scr.at[b], h_scr.at[b], h_scr.at[b], 0)
        y = part if y is None else y + part

    o_ref[...] = read_x() + _rms(y, npost_ref[layer:layer + 1, :])

    @pl.when(i == nblk - 1)
    def _():
        for b, cols in enumerate(blocks):
            conv_out_ref[:, cols] = xr_scr[b, 0:P, :]
            h_out_ref[:, cols] = h_scr[b]


def _rglru_layer(x, conv_in, h_in, layer, j, w, bf16, *, S, TB):
    D = D_MODEL
    batch_major_in = x.ndim == 3
    (seg,), R, conv_rows, _ = _segments(((S, TB),), CONV_A)
    nblk = x.size // D // R
    P = (CONV_A - 1) * S
    n_layers, n_a = w["norm_mix_pre"].shape[0], w["rg_conv_b"].shape[0]
    row_spec = pl.BlockSpec((R, D), lambda i: (i, 0))
    x_spec = pl.BlockSpec((S, TB, D), lambda i: (0, i, 0)) if batch_major_in else row_spec
    layer_of = lambda arr: pl.BlockSpec((None,) + arr.shape[1:], lambda i: (j,) + (0,) * (arr.ndim - 1),
                                        pipeline_mode=pl.Buffered(1))
    return pl.pallas_call(
        functools.partial(_rglru_kernel, seg, nblk, layer, j, batch_major_in),
        grid=(nblk,),
        in_specs=[x_spec, _resident((P, D)), _resident((S, D)),
                  _resident((n_layers, D)), _resident((n_layers, D)), _resident((D, D)), _resident((D, D)),
                  layer_of(w["rg_conv_w"]), _resident((n_a, D)), _resident(bf16["gw"].shape),
                  layer_of(w["rg_gate_a_b"]), layer_of(w["rg_gate_x_b"]), _resident((n_a, D)),
                  _resident((D, D))],
        out_specs=[row_spec, pl.BlockSpec((P, D), lambda i: (0, 0)), pl.BlockSpec((S, D), lambda i: (0, 0))],
        out_shape=[jax.ShapeDtypeStruct((nblk * R, D), F32), jax.ShapeDtypeStruct((P, D), F32),
                   jax.ShapeDtypeStruct((S, D), F32)],
        scratch_shapes=[pltpu.VMEM((R, D), BF16),
                        pltpu.VMEM((N_COL_BLOCKS, conv_rows, COL_BLOCK), F32),
                        pltpu.VMEM((N_COL_BLOCKS, S, COL_BLOCK), F32)]
                       + ([_slab_scratch(R)] if batch_major_in else []),
        compiler_params=_params(),
        name="rglru_mixer",
    )(x, conv_in, h_in, w["norm_mix_pre"], w["norm_mix_post"], bf16["w_xr"], bf16["w_gate"], w["rg_conv_w"],
      w["rg_conv_b"], bf16["gw"], w["rg_gate_a_b"], w["rg_gate_x_b"], w["rg_lambda"], bf16["w_out"])


def _sconv_colblock(seg, xn, w_bg, w_cg, w_v, cw, region):
    S, R = seg.S, seg.rows
    P = (CONV_B - 1) * S
    c0 = seg.conv0
    region[c0 + P:c0 + P + R, :] = _dot(xn, w_cg) * _dot(xn, w_v)
    conv = region[c0:c0 + R, :] * cw[0:1]
    for k in range(1, CONV_B):
        conv = conv + region[c0 + k * S:c0 + k * S + R, :] * cw[k:k + 1]
    region[c0:c0 + P, :] = region[c0 + R:c0 + R + P, :]
    return (_dot(xn, w_bg) * conv).astype(BF16)


def _sconv_stream_kernel(segs, layer, j, x_ref, *refs):
    n = len(segs)
    conv_in = refs[0:n]
    npre_ref, npost_ref, wbg_ref, wcg_ref, wv_ref, cw_ref, wout_ref = refs[n:n + 7]
    o_ref = refs[n + 7]
    conv_out = refs[n + 8:2 * n + 8]
    wbg_b_ref, wcg_b_ref, wv_b_ref, wout_b_ref = refs[2 * n + 8:2 * n + 12]
    xn_scr, cv_scr, y_scr = refs[2 * n + 12:]
    b = pl.program_id(0)

    @pl.when(b == 0)
    def _():
        xn_scr[...] = _rms(x_ref[...], npre_ref[layer:layer + 1, :]).astype(BF16)

    w_bg = wbg_ref[...].astype(BF16)
    w_cg = wcg_ref[...].astype(BF16)
    w_v = wv_ref[...].astype(BF16)
    w_out_rows = wout_ref[...].astype(BF16)
    wbg_b_ref[...] = w_bg
    wcg_b_ref[...] = w_cg
    wv_b_ref[...] = w_v
    wout_b_ref[...] = w_out_rows

    for seg, c_in, c_out in zip(segs, conv_in, conv_out):
        P = (CONV_B - 1) * seg.S
        rows = slice(seg.row0, seg.row0 + seg.rows)
        cv_scr[seg.conv0:seg.conv0 + P, :] = c_in[...]
        m = _sconv_colblock(seg, xn_scr[rows, :], w_bg, w_cg, w_v, cw_ref[...], cv_scr)
        c_out[...] = cv_scr[seg.conv0:seg.conv0 + P, :]
        _accumulate(y_scr, rows, _dot(m, w_out_rows), b)

    @pl.when(b == N_COL_BLOCKS - 1)
    def _():
        o_ref[...] = x_ref[...] + _rms(y_scr[...], npost_ref[layer:layer + 1, :])


def _sconv_stream_layer(x, conv_in, layer, j, w, *, seqs_and_steps):
    R, D = x.shape
    C = COL_BLOCK
    segs, _, conv_rows, _ = _segments(seqs_and_steps, CONV_B)
    n_layers = w["norm_mix_pre"].shape[0]
    col = lambda rows: pl.BlockSpec((rows, C), lambda b: (0, b))
    w_in_part = lambda k: pl.BlockSpec((None, D, C), lambda b: (j, 0, k * N_COL_BLOCKS + b))
    outs = pl.pallas_call(
        functools.partial(_sconv_stream_kernel, segs, layer, j),
        grid=(N_COL_BLOCKS,),
        in_specs=[_resident((R, D))] + [col(c.shape[0]) for c in conv_in]
                 + [_resident((n_layers, D)), _resident((n_layers, D)), w_in_part(0), w_in_part(1), w_in_part(2),
                    pl.BlockSpec((None, CONV_B, C), lambda b: (j, 0, b)),
                    pl.BlockSpec((None, C, D), lambda b: (j, b, 0))],
        out_specs=[pl.BlockSpec((R, D), lambda b: (0, 0))] + [col(c.shape[0]) for c in conv_in]
                  + [col(D), col(D), col(D), pl.BlockSpec((C, D), lambda b: (b, 0))],
        out_shape=[jax.ShapeDtypeStruct((R, D), F32)] + [jax.ShapeDtypeStruct(c.shape, F32) for c in conv_in]
                  + [jax.ShapeDtypeStruct((D, D), BF16)] * 4,
        scratch_shapes=[pltpu.VMEM((R, D), BF16),
                        pltpu.VMEM((conv_rows, C), F32),
                        pltpu.VMEM((R, D), F32)],
        compiler_params=_params(),
        name="sconv_mixer_small",
    )(x, *conv_in, w["norm_mix_pre"], w["norm_mix_post"], w["sc_w_in"], w["sc_w_in"], w["sc_w_in"],
      w["sc_conv_w"], w["sc_w_out"])
    n = len(segs)
    bf16 = dict(w_bg=outs[n + 1], w_cg=outs[n + 2], w_v=outs[n + 3], w_out=outs[n + 4])
    return outs[0], list(outs[1:1 + n]), bf16


def _sconv_kernel(seg, nblk, layer, x_ref, conv_in_ref, npre_ref, npost_ref, wbg_ref, wcg_ref, wv_ref,
                  cw_ref, wout_ref, o_ref, conv_out_ref, xn_scr, cv_scr, m_scr):
    P = (CONV_B - 1) * seg.S
    i = pl.program_id(0)
    blocks = [slice(b * COL_BLOCK, (b + 1) * COL_BLOCK) for b in range(N_COL_BLOCKS)]

    @pl.when(i == 0)
    def _():
        for b, cols in enumerate(blocks):
            cv_scr[b, 0:P, :] = conv_in_ref[:, cols]

    xn_scr[...] = _rms(x_ref[...], npre_ref[layer:layer + 1, :]).astype(BF16)
    xn = xn_scr[...]
    for b, cols in enumerate(blocks):
        m_scr[:, cols] = _sconv_colblock(seg, xn, wbg_ref[:, cols], wcg_ref[:, cols], wv_ref[:, cols],
                                         cw_ref[:, cols], cv_scr.at[b])
    y = _dot(m_scr[...], wout_ref[...])
    o_ref[...] = x_ref[...] + _rms(y, npost_ref[layer:layer + 1, :])

    @pl.when(i == nblk - 1)
    def _():
        for b, cols in enumerate(blocks):
            conv_out_ref[:, cols] = cv_scr[b, 0:P, :]


def _sconv_layer(x, conv_in, layer, j, w, bf16, *, S, TB):
    N, D = x.shape
    (seg,), R, conv_rows, _ = _segments(((S, TB),), CONV_B)
    nblk = N // R
    P = (CONV_B - 1) * S
    n_layers = w["norm_mix_pre"].shape[0]
    row_spec = pl.BlockSpec((R, D), lambda i: (i, 0))
    return pl.pallas_call(
        functools.partial(_sconv_kernel, seg, nblk, layer),
        grid=(nblk,),
        in_specs=[row_spec, _resident((P, D)), _resident((n_layers, D)), _resident((n_layers, D)),
                  _resident((D, D)), _resident((D, D)), _resident((D, D)),
                  pl.BlockSpec((None, CONV_B, D), lambda i: (j, 0, 0), pipeline_mode=pl.Buffered(1)),
                  _resident((D, D))],
        out_specs=[row_spec, pl.BlockSpec((P, D), lambda i: (0, 0))],
        out_shape=[jax.ShapeDtypeStruct((N, D), F32), jax.ShapeDtypeStruct((P, D), F32)],
        scratch_shapes=[pltpu.VMEM((R, D), BF16),
                        pltpu.VMEM((N_COL_BLOCKS, conv_rows, COL_BLOCK), F32),
                        pltpu.VMEM((R, D), BF16)],
        compiler_params=_params(),
        name="sconv_mixer",
    )(x, conv_in, w["norm_mix_pre"], w["norm_mix_post"], bf16["w_bg"], bf16["w_cg"], bf16["w_v"],
      w["sc_conv_w"], bf16["w_out"])


def _swiglu(g, u):
    return (g * jax.nn.sigmoid(g) * u).astype(BF16)


def _ffn_stream_kernel(layer, nchunk, x_ref, npre_ref, npost_ref, wg_ref, wu_ref, wd_ref,
                       o_ref, wg_b_ref, wu_b_ref, wd_b_ref, xn_scr, y_scr):
    c = pl.program_id(0)

    @pl.when(c == 0)
    def _():
        xn_scr[...] = _rms(x_ref[...], npre_ref[layer:layer + 1, :]).astype(BF16)

    wg = wg_ref[...].astype(BF16)
    wu = wu_ref[...].astype(BF16)
    wd = wd_ref[...].astype(BF16)
    wg_b_ref[...] = wg
    wu_b_ref[...] = wu
    wd_b_ref[...] = wd
    xn = xn_scr[...]
    part = _dot(_swiglu(_dot(xn, wg), _dot(xn, wu)), wd)
    _accumulate(y_scr, slice(None), part, c)

    @pl.when(c == nchunk - 1)
    def _():
        o_ref[...] = x_ref[...] + _rms(y_scr[...], npost_ref[layer:layer + 1, :])


def _ffn_stream_layer(x, layer, w):
    R, D = x.shape
    n_layers = w["norm_ffn_pre"].shape[0]
    nchunk = D_FF // FF_CHUNK
    cols = lambda: pl.BlockSpec((None, D, FF_CHUNK), lambda c: (layer, 0, c))
    outs = pl.pallas_call(
        functools.partial(_ffn_stream_kernel, layer, nchunk),
        grid=(nchunk,),
        in_specs=[_resident((R, D)), _resident((n_layers, D)), _resident((n_layers, D)), cols(), cols(),
                  pl.BlockSpec((None, FF_CHUNK, D), lambda c: (layer, c, 0))],
        out_specs=[pl.BlockSpec((R, D), lambda c: (0, 0)),
                   pl.BlockSpec((D, FF_CHUNK), lambda c: (0, c)), pl.BlockSpec((D, FF_CHUNK), lambda c: (0, c)),
                   pl.BlockSpec((FF_CHUNK, D), lambda c: (c, 0))],
        out_shape=[jax.ShapeDtypeStruct((R, D), F32), jax.ShapeDtypeStruct((D, D_FF), BF16),
                   jax.ShapeDtypeStruct((D, D_FF), BF16), jax.ShapeDtypeStruct((D_FF, D), BF16)],
        scratch_shapes=[pltpu.VMEM((R, D), BF16), pltpu.VMEM((R, D), F32)],
        compiler_params=_params(),
        name="swiglu_ffn_small",
    )(x, w["norm_ffn_pre"], w["norm_ffn_post"], w["ffn_w_gate"], w["ffn_w_up"], w["ffn_w_down"])
    return outs[0], dict(wg=outs[1], wu=outs[2], wd=outs[3])


def _ffn_kernel(layer, batch_major_out, x_ref, npre_ref, npost_ref, wg_ref, wu_ref, wd_ref, o_ref,
                xn_scr, act_scr, *slab_scr):
    R = x_ref.shape[0]
    halves = [slice(0, R // 2), slice(R // 2, R)]
    for rows in halves:
        xn_scr[rows, :] = _rms(x_ref[rows, :], npre_ref[layer:layer + 1, :]).astype(BF16)
    for rows in halves:
        xn = xn_scr[rows, :]
        for c0 in range(0, D_FF, FF_CHUNK):
            cols = slice(c0, c0 + FF_CHUNK)
            act_scr[rows, cols] = _swiglu(_dot(xn, wg_ref[:, cols]), _dot(xn, wu_ref[:, cols]))
    outs = []
    for rows in halves:
        y = _dot(act_scr[rows, :], wd_ref[...])
        outs.append(x_ref[rows, :] + _rms(y, npost_ref[layer:layer + 1, :]))
    if batch_major_out:
        S, TB, _ = o_ref.shape
        _store_batch_major(jnp.concatenate(outs, axis=0), slab_scr[0], o_ref, S, TB)
    else:
        for rows, out in zip(halves, outs):
            o_ref[rows, :] = out


def _ffn_layer(x, layer, w, bf16, *, R, batch_major_seqs=None):
    N, D = x.shape
    n_layers = w["norm_ffn_pre"].shape[0]
    row_spec = pl.BlockSpec((R, D), lambda i: (i, 0))
    if batch_major_seqs is None:
        out_spec, out_shape, slabs = row_spec, jax.ShapeDtypeStruct((N, D), F32), []
    else:
        S = batch_major_seqs
        out_spec = pl.BlockSpec((S, R // S, D), lambda i: (0, i, 0))
        out_shape, slabs = jax.ShapeDtypeStruct((S, N // S, D), F32), [_slab_scratch(R)]
    return pl.pallas_call(
        functools.partial(_ffn_kernel, layer, batch_major_seqs is not None),
        grid=(N // R,),
        in_specs=[row_spec, _resident((n_layers, D)), _resident((n_layers, D)),
                  _resident((D, D_FF)), _resident((D, D_FF)), _resident((D_FF, D))],
        out_specs=out_spec,
        out_shape=out_shape,
        scratch_shapes=[pltpu.VMEM((R, D), BF16), pltpu.VMEM((R, D_FF), BF16)] + slabs,
        compiler_params=_params(),
        name="swiglu_ffn",
    )(x, w["norm_ffn_pre"], w["norm_ffn_post"], bf16["wg"], bf16["wu"], bf16["wd"])


def _to_time_major(a):
    S, K, D = a.shape
    return jnp.swapaxes(a, 0, 1).reshape(K * S, D)


def _from_time_major(a, S):
    KS, D = a.shape
    return jnp.swapaxes(a.reshape(KS // S, S, D), 0, 1)


def kernel(x_prompt, x_sample, state_rglru_conv, state_rglru_h, state_sconv, meta_tokens, norm_mix_pre, norm_mix_post, norm_ffn_pre, norm_ffn_post, rg_w_in, rg_conv_w, rg_conv_b, rg_gate_a_w, rg_gate_a_b, rg_gate_x_w, rg_gate_x_b, rg_lambda, rg_w_out, sc_w_in, sc_conv_w, sc_w_out, ffn_w_gate, ffn_w_up, ffn_w_down):
    D = D_MODEL
    depth = norm_mix_pre.shape[0]
    batch, seq, _ = x_prompt.shape
    dec_batch, dec_seq, _ = x_sample.shape
    w = dict(norm_mix_pre=norm_mix_pre, norm_mix_post=norm_mix_post, norm_ffn_pre=norm_ffn_pre,
             norm_ffn_post=norm_ffn_post, rg_w_in=rg_w_in, rg_conv_w=rg_conv_w, rg_conv_b=rg_conv_b,
             rg_gate_a_w=rg_gate_a_w, rg_gate_a_b=rg_gate_a_b, rg_gate_x_w=rg_gate_x_w, rg_gate_x_b=rg_gate_x_b,
             rg_lambda=rg_lambda, rg_w_out=rg_w_out, sc_w_in=sc_w_in, sc_conv_w=sc_conv_w, sc_w_out=sc_w_out,
             ffn_w_gate=ffn_w_gate, ffn_w_up=ffn_w_up, ffn_w_down=ffn_w_down)

    xm = jnp.broadcast_to(meta_tokens[:, None, :], (N_META, batch, D)).reshape(N_META * batch, D)
    x = jnp.concatenate([xm, _to_time_major(x_sample)], axis=0)
    small = ((batch, N_META), (dec_batch, dec_seq))
    mixer_bf16, ffn_bf16 = [], []
    rg_conv_s, rg_h_s, sc_s = [], [], []
    for i in range(depth):
        j = i // 2
        if i % 2 == 0:
            conv0 = [jnp.zeros(((CONV_A - 1) * batch, D), F32), _to_time_major(state_rglru_conv[j])]
            h0 = [jnp.zeros((batch, D), F32), state_rglru_h[j]]
            x, cb, hT, wb = _rglru_stream_layer(x, conv0, h0, i, j, w, seqs_and_steps=small)
            rg_conv_s.append(cb)
            rg_h_s.append(hT)
        else:
            conv0 = [jnp.zeros(((CONV_B - 1) * batch, D), F32), _to_time_major(state_sconv[j])]
            x, cb, wb = _sconv_stream_layer(x, conv0, i, j, w, seqs_and_steps=small)
            sc_s.append(cb)
        mixer_bf16.append(wb)
        x, wb = _ffn_stream_layer(x, i, w)
        ffn_bf16.append(wb)
    y_sample = _from_time_major(x[N_META * batch:], dec_batch)

    x = x_prompt
    rg_conv_p, rg_h_p, sc_p = [], [], []
    for i in range(depth):
        j = i // 2
        if i % 2 == 0:
            x, cb, hT = _rglru_layer(x, rg_conv_s[j][0], rg_h_s[j][0], i, j, w, mixer_bf16[i], S=batch, TB=PROMPT_TB)
            rg_conv_p.append(cb)
            rg_h_p.append(hT)
        else:
            x, cb = _sconv_layer(x, sc_s[j][0], i, j, w, mixer_bf16[i], S=batch, TB=PROMPT_TB)
            sc_p.append(cb)
        x = _ffn_layer(x, i, w, ffn_bf16[i], R=batch * PROMPT_TB,
                       batch_major_seqs=batch if i == depth - 1 else None)
    y_prompt = x

    return (y_prompt, y_sample,
            jnp.stack([_from_time_major(c, batch) for c in rg_conv_p]), jnp.stack(rg_h_p),
            jnp.stack([_from_time_major(c, batch) for c in sc_p]),
            jnp.stack([_from_time_major(c[1], dec_batch) for c in rg_conv_s]), jnp.stack([h[1] for h in rg_h_s]),
            jnp.stack([_from_time_major(c[1], dec_batch) for c in sc_s]))
```

```python
import functools
from typing import NamedTuple

import jax
import jax.numpy as jnp
from jax import lax
from jax.experimental import pallas as pl
from jax.experimental.pallas import tpu as pltpu

D_MODEL = 1024
D_FF = 2816
N_META = 16
N_GATE_BLOCKS = 4
GATE_BLOCK_W = 256
CONV_A = 4
CONV_B = 3
RG_C = 8.0
EPS = 1e-6

SUBLANES = 8
LANES = 128
COL_BLOCK = 256
N_COL_BLOCKS = D_MODEL // COL_BLOCK
FF_CHUNK = 256
PROMPT_TB = 128
FUSED_TB = 64
VMEM_LIMIT_BYTES = 56 * 1024 * 1024

F32 = jnp.float32
BF16 = jnp.bfloat16


class _Seg(NamedTuple):
    S: int
    TB: int
    row0: int
    conv0: int
    h0: int

    @property
    def rows(self):
        return self.S * self.TB


def _segments(seqs_and_steps, taps):
    segs, row0, conv0, h0 = [], 0, 0, 0
    for S, TB in seqs_and_steps:
        segs.append(_Seg(S, TB, row0, conv0, h0))
        row0 += S * TB
        conv0 += (taps - 1) * S + S * TB
        h0 += S
    return tuple(segs), row0, conv0, h0


def _rms(x, w):
    ms = jnp.mean(x * x, axis=-1, keepdims=True)
    return x * lax.rsqrt(ms + EPS) * w


def _dot(a, b):
    return jnp.dot(a, b, preferred_element_type=F32)


def _scatter_time_major(x_ref, slab_scr, S, TB):
    for s in range(S):
        for j in range(D_MODEL // LANES):
            slab_scr[j, pl.ds(s, TB, stride=S), :] = x_ref[s, :, j * LANES:(j + 1) * LANES]


def _read_slabs(slab_scr):
    return jnp.concatenate([slab_scr[j] for j in range(D_MODEL // LANES)], axis=1)


def _store_batch_major(y, slab_scr, o_ref, S, TB):
    for j in range(D_MODEL // LANES):
        slab_scr[j] = y[:, j * LANES:(j + 1) * LANES]
    for s in range(S):
        for j in range(D_MODEL // LANES):
            o_ref[s, :, j * LANES:(j + 1) * LANES] = slab_scr[j, pl.ds(s, TB, stride=S), :]


def _gelu_tanh(x):
    c = 0.7978845608028654
    hx = 0.5 * x
    return hx + hx * jnp.tanh(x * (c + (c * 0.044715) * (x * x)))


def _resident(shape):
    zeros = (0,) * len(shape)
    return pl.BlockSpec(shape, lambda i: zeros, pipeline_mode=pl.Buffered(1))


def _slab_scratch(R):
    return pltpu.VMEM((D_MODEL // LANES, R, LANES), F32)


def _params():
    return pltpu.CompilerParams(dimension_semantics=("arbitrary",), vmem_limit_bytes=VMEM_LIMIT_BYTES)


def _accumulate(y_scr, rows, part, step):
    @pl.when(step == 0)
    def _():
        y_scr[rows, :] = part

    @pl.when(step > 0)
    def _():
        y_scr[rows, :] += part


def _rglru_branch_in(seg, xn, w_xr, region):
    P = (CONV_A - 1) * seg.S
    region[seg.conv0 + P:seg.conv0 + P + seg.rows, :] = _dot(xn, w_xr)


def _rglru_conv(seg, cw, cb, region):
    S, R = seg.S, seg.rows
    P = (CONV_A - 1) * S
    c0 = seg.conv0
    xc = region[c0:c0 + R, :] * cw[0:1]
    for k in range(1, CONV_A):
        xc = xc + region[c0 + k * S:c0 + k * S + R, :] * cw[k:k + 1]
    xc = xc + cb
    region[c0:c0 + P, :] = region[c0 + R:c0 + R + P, :]
    return xc


def _rglru_gate_math(res, xc, half_gab, half_gxb, lam):
    half_c_sp = (-0.5 * RG_C) * jax.nn.softplus(-lam)
    tr = jnp.tanh(res[:, 0:COL_BLOCK] + half_gab)
    log_a = half_c_sp * tr + half_c_sp
    ig = 0.5 * jnp.tanh(res[:, COL_BLOCK:2 * COL_BLOCK] + half_gxb) + 0.5
    a = jnp.exp(log_a)
    m2 = jnp.tanh(log_a) * (-1.0 - a * a)
    u = jnp.where(m2 > 0.0, m2 * lax.rsqrt(m2), 0.0) * (ig * xc)
    return a, u


def _rglru_scan(seg, a, u, g, h_read, h_write, h_row0):
    S, TB = seg.S, seg.TB
    groups = S // SUBLANES
    pieces = [None] * (TB * groups)
    for c in range(groups):
        hrows = slice(h_row0 + c * SUBLANES, h_row0 + (c + 1) * SUBLANES)
        h = h_read[hrows, :]
        for t in range(TB):
            r = t * S + c * SUBLANES
            h = a[r:r + SUBLANES] * h + u[r:r + SUBLANES]
            pieces[t * groups + c] = h * g[r:r + SUBLANES]
        h_write[hrows, :] = h
    return jnp.concatenate(pieces, axis=0)


def _rglru_colblock(seg, xn, w_gate, gw, cw, cb, half_gab, half_gxb, lam, w_out_rows,
                    region, h_read, h_write, h_row0):
    xc = _rglru_conv(seg, cw, cb, region)
    a, u = _rglru_gate_math(_dot(xc.astype(BF16), gw), xc, half_gab, half_gxb, lam)
    g = _gelu_tanh(_dot(xn, w_gate))
    hs = _rglru_scan(seg, a, u, g, h_read, h_write, h_row0)
    return _dot(hs.astype(BF16), w_out_rows)


def _rglru_stream_kernel(segs, layer, j, x_ref, *refs):
    n = len(segs)
    conv_in, h_in = refs[0:n], refs[n:2 * n]
    (npre_ref, npost_ref, wxr_ref, wgate_ref, cw_ref, cb_ref, gaw_ref, gxw_ref, gab_ref, gxb_ref,
     lam_ref, wout_ref) = refs[2 * n:2 * n + 12]
    o_ref = refs[2 * n + 12]
    conv_out, h_out = refs[2 * n + 13:3 * n + 13], refs[3 * n + 13:4 * n + 13]
    wxr_b_ref, wgate_b_ref, gw_b_ref, wout_b_ref = refs[4 * n + 13:4 * n + 17]
    xn_scr, xr_scr, y_scr = refs[4 * n + 17:]
    b = pl.program_id(0)

    @pl.when(b == 0)
    def _():
        xn_scr[...] = _rms(x_ref[...], npre_ref[layer:layer + 1, :]).astype(BF16)

    w_xr = wxr_ref[...].astype(BF16)
    w_gate = wgate_ref[...].astype(BF16)
    gw = (0.5 * jnp.concatenate([gaw_ref[...], gxw_ref[...]], axis=1)).astype(BF16)
    w_out_rows = wout_ref[...].astype(BF16)
    wxr_b_ref[...] = w_xr
    wgate_b_ref[...] = w_gate
    gw_b_ref[...] = gw
    wout_b_ref[...] = w_out_rows

    for seg, c_in, c_out, hi, ho in zip(segs, conv_in, conv_out, h_in, h_out):
        P = (CONV_A - 1) * seg.S
        rows = slice(seg.row0, seg.row0 + seg.rows)
        xn = xn_scr[rows, :]
        xr_scr[seg.conv0:seg.conv0 + P, :] = c_in[...]
        _rglru_branch_in(seg, xn, w_xr, xr_scr)
        part = _rglru_colblock(
            seg, xn, w_gate, gw, cw_ref[...], cb_ref[j:j + 1, :], 0.5 * gab_ref[pl.ds(b, 1), :],
            0.5 * gxb_ref[pl.ds(b, 1), :], lam_ref[j:j + 1, :], w_out_rows, xr_scr, hi, ho, 0)
        c_out[...] = xr_scr[seg.conv0:seg.conv0 + P, :]
        _accumulate(y_scr, rows, part, b)

    @pl.when(b == N_COL_BLOCKS - 1)
    def _():
        o_ref[...] = x_ref[...] + _rms(y_scr[...], npost_ref[layer:layer + 1, :])


def _rglru_stream_layer(x, conv_in, h_in, layer, j, w, *, seqs_and_steps):
    R, D = x.shape
    C = COL_BLOCK
    segs, _, conv_rows, _ = _segments(seqs_and_steps, CONV_A)
    n_layers, n_a = w["norm_mix_pre"].shape[0], w["rg_conv_b"].shape[0]
    col = lambda rows: pl.BlockSpec((rows, C), lambda b: (0, b))
    in_specs = (
        [_resident((R, D))] + [col(c.shape[0]) for c in conv_in] + [col(h.shape[0]) for h in h_in]
        + [_resident((n_layers, D)), _resident((n_layers, D)),
           pl.BlockSpec((None, D, C), lambda b: (j, 0, N_COL_BLOCKS + b)),
           pl.BlockSpec((None, D, C), lambda b: (j, 0, b)),
           pl.BlockSpec((None, CONV_A, C), lambda b: (j, 0, b)),
           col(n_a),
           pl.BlockSpec((None, None, C, C), lambda b: (j, b, 0, 0)),
           pl.BlockSpec((None, None, C, C), lambda b: (j, b, 0, 0)),
           pl.BlockSpec((None, N_GATE_BLOCKS, C), lambda b: (j, 0, 0)),
           pl.BlockSpec((None, N_GATE_BLOCKS, C), lambda b: (j, 0, 0)),
           col(n_a),
           pl.BlockSpec((None, C, D), lambda b: (j, b, 0))])
    out_specs = ([pl.BlockSpec((R, D), lambda b: (0, 0))] + [col(c.shape[0]) for c in conv_in]
                 + [col(h.shape[0]) for h in h_in]
                 + [col(D), col(D), pl.BlockSpec((None, C, 2 * C), lambda b: (b, 0, 0)),
                    pl.BlockSpec((C, D), lambda b: (b, 0))])
    out_shape = ([jax.ShapeDtypeStruct((R, D), F32)] + [jax.ShapeDtypeStruct(s.shape, F32) for s in conv_in + h_in]
                 + [jax.ShapeDtypeStruct((D, D), BF16), jax.ShapeDtypeStruct((D, D), BF16),
                    jax.ShapeDtypeStruct((N_GATE_BLOCKS, C, 2 * C), BF16), jax.ShapeDtypeStruct((D, D), BF16)])
    outs = pl.pallas_call(
        functools.partial(_rglru_stream_kernel, segs, layer, j),
        grid=(N_COL_BLOCKS,),
        in_specs=in_specs, out_specs=out_specs, out_shape=out_shape,
        scratch_shapes=[pltpu.VMEM((R, D), BF16),
                        pltpu.VMEM((conv_rows, C), F32),
                        pltpu.VMEM((R, D), F32)],
        compiler_params=_params(),
        name="rglru_mixer_small",
    )(x, *conv_in, *h_in, w["norm_mix_pre"], w["norm_mix_post"], w["rg_w_in"], w["rg_w_in"], w["rg_conv_w"],
      w["rg_conv_b"], w["rg_gate_a_w"], w["rg_gate_x_w"], w["rg_gate_a_b"], w["rg_gate_x_b"], w["rg_lambda"],
      w["rg_w_out"])
    n = len(segs)
    bf16 = dict(w_xr=outs[2 * n + 1], w_gate=outs[2 * n + 2], gw=outs[2 * n + 3], w_out=outs[2 * n + 4])
    return outs[0], list(outs[1:1 + n]), list(outs[1 + n:1 + 2 * n]), bf16


def _rglru_kernel(seg, nblk, layer, j, batch_major_in,
                  x_ref, conv_in_ref, h_in_ref, npre_ref, npost_ref, wxr_ref, wgate_ref, cw_ref, cb_ref,
                  gw_ref, gab_ref, gxb_ref, lam_ref, wout_ref,
                  o_ref, conv_out_ref, h_out_ref, xn_scr, xr_scr, h_scr, *slab_scr):
    P = (CONV_A - 1) * seg.S
    i = pl.program_id(0)
    blocks = [slice(b * COL_BLOCK, (b + 1) * COL_BLOCK) for b in range(N_COL_BLOCKS)]

    @pl.when(i == 0)
    def _():
        for b, cols in enumerate(blocks):
            xr_scr[b, 0:P, :] = conv_in_ref[:, cols]
            h_scr[b] = h_in_ref[:, cols]

    if batch_major_in:
        _scatter_time_major(x_ref, slab_scr[0], seg.S, seg.TB)
        read_x = lambda: _read_slabs(slab_scr[0])
    else:
        read_x = lambda: x_ref[...]

    xn_scr[...] = _rms(read_x(), npre_ref[layer:layer + 1, :]).astype(BF16)
    xn = xn_scr[...]

    gate_pre, xc, res = {}, {}, {}

    def input_matmuls(b):
        _rglru_branch_in(seg, xn, wxr_ref[:, blocks[b]], xr_scr.at[b])
        gate_pre[b] = _dot(xn, wgate_ref[:, blocks[b]])

    def conv_and_gate_matmul(b):
        xc[b] = _rglru_conv(seg, cw_ref[:, blocks[b]], cb_ref[j:j + 1, blocks[b]], xr_scr.at[b])
        res[b] = _dot(xc[b].astype(BF16), gw_ref[b])

    y = None
    input_matmuls(0)
    conv_and_gate_matmul(0)
    input_matmuls(1)
    for b, cols in enumerate(blocks):
        if b + 1 < N_COL_BLOCKS:
            conv_and_gate_matmul(b + 1)
        if b + 2 < N_COL_BLOCKS:
            input_matmuls(b + 2)
        g = _gelu_tanh(gate_pre.pop(b))
        a, u = _rglru_gate_math(res.pop(b), xc.pop(b), 0.5 * gab_ref[b:b + 1, :], 0.5 * gxb_ref[b:b + 1, :],
                                lam_ref[j:j + 1, cols])
        hs = _rglru_scan(seg, a, u, g, h_scr.at[b], h_scr.at[b], 0)
        part = _dot(hs.astype(BF16), wout_ref[cols, :])
        y = part if y is None else y + part

    o_ref[...] = read_x() + _rms(y, npost_ref[layer:layer + 1, :])

    @pl.when(i == nblk - 1)
    def _():
        for b, cols in enumerate(blocks):
            conv_out_ref[:, cols] = xr_scr[b, 0:P, :]
            h_out_ref[:, cols] = h_scr[b]


def _rglru_layer(x, conv_in, h_in, layer, j, w, bf16, *, S, TB):
    D = D_MODEL
    batch_major_in = x.ndim == 3
    (seg,), R, conv_rows, _ = _segments(((S, TB),), CONV_A)
    nblk = x.size // D // R
    P = (CONV_A - 1) * S
    n_layers, n_a = w["norm_mix_pre"].shape[0], w["rg_conv_b"].shape[0]
    row_spec = pl.BlockSpec((R, D), lambda i: (i, 0))
    x_spec = pl.BlockSpec((S, TB, D), lambda i: (0, i, 0)) if batch_major_in else row_spec
    layer_of = lambda arr: pl.BlockSpec((None,) + arr.shape[1:], lambda i: (j,) + (0,) * (arr.ndim - 1),
                                        pipeline_mode=pl.Buffered(1))
    return pl.pallas_call(
        functools.partial(_rglru_kernel, seg, nblk, layer, j, batch_major_in),
        grid=(nblk,),
        in_specs=[x_spec, _resident((P, D)), _resident((S, D)),
                  _resident((n_layers, D)), _resident((n_layers, D)), _resident((D, D)), _resident((D, D)),
                  layer_of(w["rg_conv_w"]), _resident((n_a, D)), _resident(bf16["gw"].shape),
                  layer_of(w["rg_gate_a_b"]), layer_of(w["rg_gate_x_b"]), _resident((n_a, D)),
                  _resident((D, D))],
        out_specs=[row_spec, pl.BlockSpec((P, D), lambda i: (0, 0)), pl.BlockSpec((S, D), lambda i: (0, 0))],
        out_shape=[jax.ShapeDtypeStruct((nblk * R, D), F32), jax.ShapeDtypeStruct((P, D), F32),
                   jax.ShapeDtypeStruct((S, D), F32)],
        scratch_shapes=[pltpu.VMEM((R, D), BF16),
                        pltpu.VMEM((N_COL_BLOCKS, conv_rows, COL_BLOCK), F32),
                        pltpu.VMEM((N_COL_BLOCKS, S, COL_BLOCK), F32)]
                       + ([_slab_scratch(R)] if batch_major_in else []),
        compiler_params=_params(),
        name="rglru_mixer",
    )(x, conv_in, h_in, w["norm_mix_pre"], w["norm_mix_post"], bf16["w_xr"], bf16["w_gate"], w["rg_conv_w"],
      w["rg_conv_b"], bf16["gw"], w["rg_gate_a_b"], w["rg_gate_x_b"], w["rg_lambda"], bf16["w_out"])


def _rglru_ffn_kernel(seg, nblk, layer, j, batch_major_in,
                      x_ref, conv_in_ref, h_in_ref, npre_ref, npost_ref, wxr_ref, wgate_ref, cw_ref, cb_ref,
                      gw_ref, gab_ref, gxb_ref, lam_ref, wout_ref,
                      fpre_ref, fpost_ref, wg_ref, wu_ref, wd_ref,
                      o_ref, conv_out_ref, h_out_ref,
                      xn_scr, xr_scr, h_scr, x1_scr, xn2_scr, *slab_scr):
    P = (CONV_A - 1) * seg.S
    i = pl.program_id(0)
    slot = i % 2
    blocks = [slice(b * COL_BLOCK, (b + 1) * COL_BLOCK) for b in range(N_COL_BLOCKS)]
    ff_chunks = [slice(c0, c0 + FF_CHUNK) for c0 in range(0, D_FF, FF_CHUNK)]
    chunks_per_block = -(-len(ff_chunks) // N_COL_BLOCKS)

    @pl.when(i == 0)
    def _():
        for b, cols in enumerate(blocks):
            xr_scr[b, 0:P, :] = conv_in_ref[:, cols]
            h_scr[b] = h_in_ref[:, cols]
        x1_scr[1] = jnp.zeros(x1_scr.shape[1:], F32)

    if batch_major_in:
        _scatter_time_major(x_ref, slab_scr[0], seg.S, seg.TB)
        read_x = lambda: _read_slabs(slab_scr[0])
    else:
        read_x = lambda: x_ref[...]

    xn_scr[...] = _rms(read_x(), npre_ref[layer:layer + 1, :]).astype(BF16)
    xn = xn_scr[...]
    x1_prev = x1_scr.at[1 - slot]
    xn2_scr[...] = _rms(x1_prev[...], fpre_ref[layer:layer + 1, :]).astype(BF16)
    xn2 = xn2_scr[...]

    y = f = None
    _rglru_branch_in(seg, xn, wxr_ref[:, blocks[0]], xr_scr.at[0])
    for b, cols in enumerate(blocks):
        if b + 1 < N_COL_BLOCKS:
            _rglru_branch_in(seg, xn, wxr_ref[:, blocks[b + 1]], xr_scr.at[b + 1])
        part = _rglru_colblock(
            seg, xn, wgate_ref[:, cols], gw_ref[b], cw_ref[:, cols], cb_ref[j:j + 1, cols],
            0.5 * gab_ref[b:b + 1, :], 0.5 * gxb_ref[b:b + 1, :], lam_ref[j:j + 1, cols], wout_ref[cols, :],
            xr_scr.at[b], h_scr.at[b], h_scr.at[b], 0)
        y = part if y is None else y + part
        for fc in ff_chunks[b * chunks_per_block:(b + 1) * chunks_per_block]:
            act = _swiglu(_dot(xn2, wg_ref[:, fc]), _dot(xn2, wu_ref[:, fc]))
            part = _dot(act, wd_ref[fc, :])
            f = part if f is None else f + part

    @pl.when(i == nblk - 1)
    def _():
        for b, cols in enumerate(blocks):
            conv_out_ref[:, cols] = xr_scr[b, 0:P, :]
            h_out_ref[:, cols] = h_scr[b]

    x1_scr[slot] = read_x() + _rms(y, npost_ref[layer:layer + 1, :])
    o_ref[...] = x1_prev[...] + _rms(f, fpost_ref[layer:layer + 1, :])


def _rglru_ffn_layer(x, conv_in, h_in, layer, j, w, bf16, ffn_bf16, *, S, TB):
    D = D_MODEL
    batch_major_in = x.ndim == 3
    (seg,), R, conv_rows, _ = _segments(((S, TB),), CONV_A)
    nblk = x.size // D // R
    P = (CONV_A - 1) * S
    n_layers, n_a = w["norm_mix_pre"].shape[0], w["rg_conv_b"].shape[0]
    last = nblk - 1
    if batch_major_in:
        x_spec = pl.BlockSpec((S, TB, D), lambda i: (0, jnp.minimum(i, last), 0))
    else:
        x_spec = pl.BlockSpec((R, D), lambda i: (jnp.minimum(i, last), 0))
    layer_of = lambda arr: pl.BlockSpec((None,) + arr.shape[1:], lambda i: (j,) + (0,) * (arr.ndim - 1),
                                        pipeline_mode=pl.Buffered(1))
    return pl.pallas_call(
        functools.partial(_rglru_ffn_kernel, seg, nblk, layer, j, batch_major_in),
        grid=(nblk + 1,),
        in_specs=[x_spec, _resident((P, D)), _resident((S, D)),
                  _resident((n_layers, D)), _resident((n_layers, D)), _resident((D, D)), _resident((D, D)),
                  layer_of(w["rg_conv_w"]), _resident((n_a, D)), _resident(bf16["gw"].shape),
                  layer_of(w["rg_gate_a_b"]), layer_of(w["rg_gate_x_b"]), _resident((n_a, D)),
                  _resident((D, D)),
                  _resident((n_layers, D)), _resident((n_layers, D)),
                  _resident((D, D_FF)), _resident((D, D_FF)), _resident((D_FF, D))],
        out_specs=[pl.BlockSpec((R, D), lambda i: (jnp.maximum(i - 1, 0), 0)),
                   pl.BlockSpec((P, D), lambda i: (0, 0)), pl.BlockSpec((S, D), lambda i: (0, 0))],
        out_shape=[jax.ShapeDtypeStruct((nblk * R, D), F32), jax.ShapeDtypeStruct((P, D), F32),
                   jax.ShapeDtypeStruct((S, D), F32)],
        scratch_shapes=[pltpu.VMEM((R, D), BF16),
                        pltpu.VMEM((N_COL_BLOCKS, conv_rows, COL_BLOCK), F32),
                        pltpu.VMEM((N_COL_BLOCKS, S, COL_BLOCK), F32),
                        pltpu.VMEM((2, R, D), F32),
                        pltpu.VMEM((R, D), BF16)]
                       + ([_slab_scratch(R)] if batch_major_in else []),
        compiler_params=_params(),
        name="rglru_mixer_ffn",
    )(x, conv_in, h_in, w["norm_mix_pre"], w["norm_mix_post"], bf16["w_xr"], bf16["w_gate"], w["rg_conv_w"],
      w["rg_conv_b"], bf16["gw"], w["rg_gate_a_b"], w["rg_gate_x_b"], w["rg_lambda"], bf16["w_out"],
      w["norm_ffn_pre"], w["norm_ffn_post"], ffn_bf16["wg"], ffn_bf16["wu"], ffn_bf16["wd"])


def _sconv_colblock(seg, xn, w_bg, w_cg, w_v, cw, region):
    S, R = seg.S, seg.rows
    P = (CONV_B - 1) * S
    c0 = seg.conv0
    region[c0 + P:c0 + P + R, :] = _dot(xn, w_cg) * _dot(xn, w_v)
    conv = region[c0:c0 + R, :] * cw[0:1]
    for k in range(1, CONV_B):
        conv = conv + region[c0 + k * S:c0 + k * S + R, :] * cw[k:k + 1]
    region[c0:c0 + P, :] = region[c0 + R:c0 + R + P, :]
    return (_dot(xn, w_bg) * conv).astype(BF16)


def _sconv_stream_kernel(segs, layer, j, x_ref, *refs):
    n = len(segs)
    conv_in = refs[0:n]
    npre_ref, npost_ref, wbg_ref, wcg_ref, wv_ref, cw_ref, wout_ref = refs[n:n + 7]
    o_ref = refs[n + 7]
    conv_out = refs[n + 8:2 * n + 8]
    wbg_b_ref, wcg_b_ref, wv_b_ref, wout_b_ref = refs[2 * n + 8:2 * n + 12]
    xn_scr, cv_scr, y_scr = refs[2 * n + 12:]
    b = pl.program_id(0)

    @pl.when(b == 0)
    def _():
        xn_scr[...] = _rms(x_ref[...], npre_ref[layer:layer + 1, :]).astype(BF16)

    w_bg = wbg_ref[...].astype(BF16)
    w_cg = wcg_ref[...].astype(BF16)
    w_v = wv_ref[...].astype(BF16)
    w_out_rows = wout_ref[...].astype(BF16)
    wbg_b_ref[...] = w_bg
    wcg_b_ref[...] = w_cg
    wv_b_ref[...] = w_v
    wout_b_ref[...] = w_out_rows

    for seg, c_in, c_out in zip(segs, conv_in, conv_out):
        P = (CONV_B - 1) * seg.S
        rows = slice(seg.row0, seg.row0 + seg.rows)
        cv_scr[seg.conv0:seg.conv0 + P, :] = c_in[...]
        m = _sconv_colblock(seg, xn_scr[rows, :], w_bg, w_cg, w_v, cw_ref[...], cv_scr)
        c_out[...] = cv_scr[seg.conv0:seg.conv0 + P, :]
        _accumulate(y_scr, rows, _dot(m, w_out_rows), b)

    @pl.when(b == N_COL_BLOCKS - 1)
    def _():
        o_ref[...] = x_ref[...] + _rms(y_scr[...], npost_ref[layer:layer + 1, :])


def _sconv_stream_layer(x, conv_in, layer, j, w, *, seqs_and_steps):
    R, D = x.shape
    C = COL_BLOCK
    segs, _, conv_rows, _ = _segments(seqs_and_steps, CONV_B)
    n_layers = w["norm_mix_pre"].shape[0]
    col = lambda rows: pl.BlockSpec((rows, C), lambda b: (0, b))
    w_in_part = lambda k: pl.BlockSpec((None, D, C), lambda b: (j, 0, k * N_COL_BLOCKS + b))
    outs = pl.pallas_call(
        functools.partial(_sconv_stream_kernel, segs, layer, j),
        grid=(N_COL_BLOCKS,),
        in_specs=[_resident((R, D))] + [col(c.shape[0]) for c in conv_in]
                 + [_resident((n_layers, D)), _resident((n_layers, D)), w_in_part(0), w_in_part(1), w_in_part(2),
                    pl.BlockSpec((None, CONV_B, C), lambda b: (j, 0, b)),
                    pl.BlockSpec((None, C, D), lambda b: (j, b, 0))],
        out_specs=[pl.BlockSpec((R, D), lambda b: (0, 0))] + [col(c.shape[0]) for c in conv_in]
                  + [col(D), col(D), col(D), pl.BlockSpec((C, D), lambda b: (b, 0))],
        out_shape=[jax.ShapeDtypeStruct((R, D), F32)] + [jax.ShapeDtypeStruct(c.shape, F32) for c in conv_in]
                  + [jax.ShapeDtypeStruct((D, D), BF16)] * 4,
        scratch_shapes=[pltpu.VMEM((R, D), BF16),
                        pltpu.VMEM((conv_rows, C), F32),
                        pltpu.VMEM((R, D), F32)],
        compiler_params=_params(),
        name="sconv_mixer_small",
    )(x, *conv_in, w["norm_mix_pre"], w["norm_mix_post"], w["sc_w_in"], w["sc_w_in"], w["sc_w_in"],
      w["sc_conv_w"], w["sc_w_out"])
    n = len(segs)
    bf16 = dict(w_bg=outs[n + 1], w_cg=outs[n + 2], w_v=outs[n + 3], w_out=outs[n + 4])
    return outs[0], list(outs[1:1 + n]), bf16


def _sconv_kernel(seg, nblk, layer, x_ref, conv_in_ref, npre_ref, npost_ref, wbg_ref, wcg_ref, wv_ref,
                  cw_ref, wout_ref, o_ref, conv_out_ref, xn_scr, cv_scr, m_scr):
    P = (CONV_B - 1) * seg.S
    i = pl.program_id(0)
    blocks = [slice(b * COL_BLOCK, (b + 1) * COL_BLOCK) for b in range(N_COL_BLOCKS)]

    @pl.when(i == 0)
    def _():
        for b, cols in enumerate(blocks):
            cv_scr[b, 0:P, :] = conv_in_ref[:, cols]

    xn_scr[...] = _rms(x_ref[...], npre_ref[layer:layer + 1, :]).astype(BF16)
    xn = xn_scr[...]
    for b, cols in enumerate(blocks):
        m_scr[:, cols] = _sconv_colblock(seg, xn, wbg_ref[:, cols], wcg_ref[:, cols], wv_ref[:, cols],
                                         cw_ref[:, cols], cv_scr.at[b])
    y = _dot(m_scr[...], wout_ref[...])
    o_ref[...] = x_ref[...] + _rms(y, npost_ref[layer:layer + 1, :])

    @pl.when(i == nblk - 1)
    def _():
        for b, cols in enumerate(blocks):
            conv_out_ref[:, cols] = cv_scr[b, 0:P, :]


def _sconv_layer(x, conv_in, layer, j, w, bf16, *, S, TB):
    N, D = x.shape
    (seg,), R, conv_rows, _ = _segments(((S, TB),), CONV_B)
    nblk = N // R
    P = (CONV_B - 1) * S
    n_layers = w["norm_mix_pre"].shape[0]
    row_spec = pl.BlockSpec((R, D), lambda i: (i, 0))
    return pl.pallas_call(
        functools.partial(_sconv_kernel, seg, nblk, layer),
        grid=(nblk,),
        in_specs=[row_spec, _resident((P, D)), _resident((n_layers, D)), _resident((n_layers, D)),
                  _resident((D, D)), _resident((D, D)), _resident((D, D)),
                  pl.BlockSpec((None, CONV_B, D), lambda i: (j, 0, 0), pipeline_mode=pl.Buffered(1)),
                  _resident((D, D))],
        out_specs=[row_spec, pl.BlockSpec((P, D), lambda i: (0, 0))],
        out_shape=[jax.ShapeDtypeStruct((N, D), F32), jax.ShapeDtypeStruct((P, D), F32)],
        scratch_shapes=[pltpu.VMEM((R, D), BF16),
                        pltpu.VMEM((N_COL_BLOCKS, conv_rows, COL_BLOCK), F32),
                        pltpu.VMEM((R, D), BF16)],
        compiler_params=_params(),
        name="sconv_mixer",
    )(x, conv_in, w["norm_mix_pre"], w["norm_mix_post"], bf16["w_bg"], bf16["w_cg"], bf16["w_v"],
      w["sc_conv_w"], bf16["w_out"])


def _swiglu(g, u):
    return (g * jax.nn.sigmoid(g) * u).astype(BF16)


def _ffn_stream_kernel(layer, nchunk, x_ref, npre_ref, npost_ref, wg_ref, wu_ref, wd_ref,
                       o_ref, wg_b_ref, wu_b_ref, wd_b_ref, xn_scr, y_scr):
    c = pl.program_id(0)

    @pl.when(c == 0)
    def _():
        xn_scr[...] = _rms(x_ref[...], npre_ref[layer:layer + 1, :]).astype(BF16)

    wg = wg_ref[...].astype(BF16)
    wu = wu_ref[...].astype(BF16)
    wd = wd_ref[...].astype(BF16)
    wg_b_ref[...] = wg
    wu_b_ref[...] = wu
    wd_b_ref[...] = wd
    xn = xn_scr[...]
    part = _dot(_swiglu(_dot(xn, wg), _dot(xn, wu)), wd)
    _accumulate(y_scr, slice(None), part, c)

    @pl.when(c == nchunk - 1)
    def _():
        o_ref[...] = x_ref[...] + _rms(y_scr[...], npost_ref[layer:layer + 1, :])


def _ffn_stream_layer(x, layer, w):
    R, D = x.shape
    n_layers = w["norm_ffn_pre"].shape[0]
    nchunk = D_FF // FF_CHUNK
    cols = lambda: pl.BlockSpec((None, D, FF_CHUNK), lambda c: (layer, 0, c))
    outs = pl.pallas_call(
        functools.partial(_ffn_stream_kernel, layer, nchunk),
        grid=(nchunk,),
        in_specs=[_resident((R, D)), _resident((n_layers, D)), _resident((n_layers, D)), cols(), cols(),
                  pl.BlockSpec((None, FF_CHUNK, D), lambda c: (layer, c, 0))],
        out_specs=[pl.BlockSpec((R, D), lambda c: (0, 0)),
                   pl.BlockSpec((D, FF_CHUNK), lambda c: (0, c)), pl.BlockSpec((D, FF_CHUNK), lambda c: (0, c)),
                   pl.BlockSpec((FF_CHUNK, D), lambda c: (c, 0))],
        out_shape=[jax.ShapeDtypeStruct((R, D), F32), jax.ShapeDtypeStruct((D, D_FF), BF16),
                   jax.ShapeDtypeStruct((D, D_FF), BF16), jax.ShapeDtypeStruct((D_FF, D), BF16)],
        scratch_shapes=[pltpu.VMEM((R, D), BF16), pltpu.VMEM((R, D), F32)],
        compiler_params=_params(),
        name="swiglu_ffn_small",
    )(x, w["norm_ffn_pre"], w["norm_ffn_post"], w["ffn_w_gate"], w["ffn_w_up"], w["ffn_w_down"])
    return outs[0], dict(wg=outs[1], wu=outs[2], wd=outs[3])


def _ffn_kernel(layer, batch_major_out, x_ref, npre_ref, npost_ref, wg_ref, wu_ref, wd_ref, o_ref,
                xn_scr, act_scr, *slab_scr):
    R = x_ref.shape[0]
    halves = [slice(0, R // 2), slice(R // 2, R)]
    for rows in halves:
        xn_scr[rows, :] = _rms(x_ref[rows, :], npre_ref[layer:layer + 1, :]).astype(BF16)
    for rows in halves:
        xn = xn_scr[rows, :]
        for c0 in range(0, D_FF, FF_CHUNK):
            cols = slice(c0, c0 + FF_CHUNK)
            act_scr[rows, cols] = _swiglu(_dot(xn, wg_ref[:, cols]), _dot(xn, wu_ref[:, cols]))
    outs = []
    for rows in halves:
        y = _dot(act_scr[rows, :], wd_ref[...])
        outs.append(x_ref[rows, :] + _rms(y, npost_ref[layer:layer + 1, :]))
    if batch_major_out:
        S, TB, _ = o_ref.shape
        _store_batch_major(jnp.concatenate(outs, axis=0), slab_scr[0], o_ref, S, TB)
    else:
        for rows, out in zip(halves, outs):
            o_ref[rows, :] = out


def _ffn_layer(x, layer, w, bf16, *, R, batch_major_seqs=None):
    N, D = x.shape
    n_layers = w["norm_ffn_pre"].shape[0]
    row_spec = pl.BlockSpec((R, D), lambda i: (i, 0))
    if batch_major_seqs is None:
        out_spec, out_shape, slabs = row_spec, jax.ShapeDtypeStruct((N, D), F32), []
    else:
        S = batch_major_seqs
        out_spec = pl.BlockSpec((S, R // S, D), lambda i: (0, i, 0))
        out_shape, slabs = jax.ShapeDtypeStruct((S, N // S, D), F32), [_slab_scratch(R)]
    return pl.pallas_call(
        functools.partial(_ffn_kernel, layer, batch_major_seqs is not None),
        grid=(N // R,),
        in_specs=[row_spec, _resident((n_layers, D)), _resident((n_layers, D)),
                  _resident((D, D_FF)), _resident((D, D_FF)), _resident((D_FF, D))],
        out_specs=out_spec,
        out_shape=out_shape,
        scratch_shapes=[pltpu.VMEM((R, D), BF16), pltpu.VMEM((R, D_FF), BF16)] + slabs,
        compiler_params=_params(),
        name="swiglu_ffn",
    )(x, w["norm_ffn_pre"], w["norm_ffn_post"], bf16["wg"], bf16["wu"], bf16["wd"])


def _to_time_major(a):
    S, K, D = a.shape
    return jnp.swapaxes(a, 0, 1).reshape(K * S, D)


def _from_time_major(a, S):
    KS, D = a.shape
    return jnp.swapaxes(a.reshape(KS // S, S, D), 0, 1)


def kernel(x_prompt, x_sample, state_rglru_conv, state_rglru_h, state_sconv, meta_tokens, norm_mix_pre, norm_mix_post, norm_ffn_pre, norm_ffn_post, rg_w_in, rg_conv_w, rg_conv_b, rg_gate_a_w, rg_gate_a_b, rg_gate_x_w, rg_gate_x_b, rg_lambda, rg_w_out, sc_w_in, sc_conv_w, sc_w_out, ffn_w_gate, ffn_w_up, ffn_w_down):
    D = D_MODEL
    depth = norm_mix_pre.shape[0]
    batch, seq, _ = x_prompt.shape
    dec_batch, dec_seq, _ = x_sample.shape
    w = dict(norm_mix_pre=norm_mix_pre, norm_mix_post=norm_mix_post, norm_ffn_pre=norm_ffn_pre,
             norm_ffn_post=norm_ffn_post, rg_w_in=rg_w_in, rg_conv_w=rg_conv_w, rg_conv_b=rg_conv_b,
             rg_gate_a_w=rg_gate_a_w, rg_gate_a_b=rg_gate_a_b, rg_gate_x_w=rg_gate_x_w, rg_gate_x_b=rg_gate_x_b,
             rg_lambda=rg_lambda, rg_w_out=rg_w_out, sc_w_in=sc_w_in, sc_conv_w=sc_conv_w, sc_w_out=sc_w_out,
             ffn_w_gate=ffn_w_gate, ffn_w_up=ffn_w_up, ffn_w_down=ffn_w_down)

    xm = jnp.broadcast_to(meta_tokens[:, None, :], (N_META, batch, D)).reshape(N_META * batch, D)
    x = jnp.concatenate([xm, _to_time_major(x_sample)], axis=0)
    small = ((batch, N_META), (dec_batch, dec_seq))
    mixer_bf16, ffn_bf16 = [], []
    rg_conv_s, rg_h_s, sc_s = [], [], []
    for i in range(depth):
        j = i // 2
        if i % 2 == 0:
            conv0 = [jnp.zeros(((CONV_A - 1) * batch, D), F32), _to_time_major(state_rglru_conv[j])]
            h0 = [jnp.zeros((batch, D), F32), state_rglru_h[j]]
            x, cb, hT, wb = _rglru_stream_layer(x, conv0, h0, i, j, w, seqs_and_steps=small)
            rg_conv_s.append(cb)
            rg_h_s.append(hT)
        else:
            conv0 = [jnp.zeros(((CONV_B - 1) * batch, D), F32), _to_time_major(state_sconv[j])]
            x, cb, wb = _sconv_stream_layer(x, conv0, i, j, w, seqs_and_steps=small)
            sc_s.append(cb)
        mixer_bf16.append(wb)
        x, wb = _ffn_stream_layer(x, i, w)
        ffn_bf16.append(wb)
    y_sample = _from_time_major(x[N_META * batch:], dec_batch)

    x = x_prompt
    rg_conv_p, rg_h_p, sc_p = [], [], []
    for i in range(depth):
        j = i // 2
        if i % 2 == 0:
            x, cb, hT = _rglru_layer(x, rg_conv_s[j][0], rg_h_s[j][0], i, j, w, mixer_bf16[i], S=batch, TB=PROMPT_TB)
            rg_conv_p.append(cb)
            rg_h_p.append(hT)
        else:
            x, cb = _sconv_layer(x, sc_s[j][0], i, j, w, mixer_bf16[i], S=batch, TB=PROMPT_TB)
            sc_p.append(cb)
        x = _ffn_layer(x, i, w, ffn_bf16[i], R=batch * PROMPT_TB,
                       batch_major_seqs=batch if i == depth - 1 else None)
    y_prompt = x

    return (y_prompt, y_sample,
            jnp.stack([_from_time_major(c, batch) for c in rg_conv_p]), jnp.stack(rg_h_p),
            jnp.stack([_from_time_major(c, batch) for c in sc_p]),
            jnp.stack([_from_time_major(c[1], dec_batch) for c in rg_conv_s]), jnp.stack([h[1] for h in rg_h_s]),
            jnp.stack([_from_time_major(c[1], dec_batch) for c in sc_s]))
```

```python
import functools
from typing import NamedTuple

import jax
import jax.numpy as jnp
from jax import lax
from jax.experimental import pallas as pl
from jax.experimental.pallas import tpu as pltpu

D_MODEL = 1024
D_FF = 2816
N_META = 16
COL_BLOCK = 256
N_COL_BLOCKS = D_MODEL // COL_BLOCK
CONV_A = 4
CONV_B = 3
RG_C = 8.0
EPS = 1e-6

SUBLANES = 8
LANES = 128
FF_CHUNK = 256
PROMPT_TB = 128
VMEM_LIMIT_BYTES = 56 * 1024 * 1024

F32 = jnp.float32
BF16 = jnp.bfloat16


class _Seg(NamedTuple):
    S: int
    TB: int
    row0: int
    conv0: int
    h0: int

    @property
    def rows(self):
        return self.S * self.TB


def _segments(seqs_and_steps, taps):
    segs, row0, conv0, h0 = [], 0, 0, 0
    for S, TB in seqs_and_steps:
        segs.append(_Seg(S, TB, row0, conv0, h0))
        row0 += S * TB
        conv0 += (taps - 1) * S + S * TB
        h0 += S
    return tuple(segs), row0, conv0, h0


def _rms(x, w):
    ms = jnp.mean(x * x, axis=-1, keepdims=True)
    return x * lax.rsqrt(ms + EPS) * w


def _dot(a, b):
    return jnp.dot(a, b, preferred_element_type=F32)


def _scatter_time_major(x_ref, slab_scr, S, TB):
    for s in range(S):
        for j in range(D_MODEL // LANES):
            slab_scr[j, pl.ds(s, TB, stride=S), :] = x_ref[s, :, j * LANES:(j + 1) * LANES]


def _read_slabs(slab_scr):
    return jnp.concatenate([slab_scr[j] for j in range(D_MODEL // LANES)], axis=1)


def _store_batch_major(y, rows, slab_scr, o_ref):
    S = o_ref.shape[0]
    t0, steps = rows.start // S, (rows.stop - rows.start) // S
    for j in range(D_MODEL // LANES):
        slab_scr[j, rows, :] = y[:, j * LANES:(j + 1) * LANES]
    for s in range(S):
        for j in range(D_MODEL // LANES):
            o_ref[s, t0:t0 + steps, j * LANES:(j + 1) * LANES] = slab_scr[j, pl.ds(rows.start + s, steps, stride=S), :]


def _gelu_tanh(x):
    c = 0.7978845608028654
    hx = 0.5 * x
    return hx + hx * jnp.tanh(x * (c + (c * 0.044715) * (x * x)))


def _resident(shape):
    zeros = (0,) * len(shape)
    return pl.BlockSpec(shape, lambda i: zeros, pipeline_mode=pl.Buffered(1))


def _chunk_major_out(rows, width=COL_BLOCK):
    return pl.BlockSpec((None, rows, width), lambda i: (i, 0, 0))


def _slab_scratch(R):
    return pltpu.VMEM((D_MODEL // LANES, R, LANES), F32)


def _params():
    return pltpu.CompilerParams(dimension_semantics=("arbitrary",), vmem_limit_bytes=VMEM_LIMIT_BYTES)


def _accumulate(y_scr, rows, part, step):
    @pl.when(step == 0)
    def _():
        y_scr[rows, :] = part

    @pl.when(step > 0)
    def _():
        y_scr[rows, :] += part


def _rglru_branch_in(seg, xn, w_xr, region):
    P = (CONV_A - 1) * seg.S
    region[seg.conv0 + P:seg.conv0 + P + seg.rows, :] = _dot(xn, w_xr)


def _rglru_conv(seg, cw, cb, region, carry_to):
    S, R = seg.S, seg.rows
    P = (CONV_A - 1) * S
    c0 = seg.conv0
    xc = region[c0:c0 + R, :] * cw[0:1]
    for k in range(1, CONV_A):
        xc = xc + region[c0 + k * S:c0 + k * S + R, :] * cw[k:k + 1]
    xc = xc + cb
    if carry_to is not None:
        region[carry_to:carry_to + P, :] = region[c0 + R:c0 + R + P, :]
    return xc


def _rglru_gate_math(res, xc, half_gab, half_gxb, lam):
    half_c_sp = (-0.5 * RG_C) * jax.nn.softplus(-lam)
    tr = jnp.tanh(res[:, 0:COL_BLOCK] + half_gab)
    log_a = half_c_sp * tr + half_c_sp
    ig = 0.5 * jnp.tanh(res[:, COL_BLOCK:2 * COL_BLOCK] + half_gxb) + 0.5
    a = jnp.exp(log_a)
    m2 = jnp.tanh(log_a) * (-1.0 - a * a)
    u = jnp.where(m2 > 0.0, m2 * lax.rsqrt(m2), 0.0) * (ig * xc)
    return a, u


def _rglru_scan(seg, a, u, g, h_read, h_write, h_row0):
    S, TB = seg.S, seg.TB
    groups = S // SUBLANES
    pieces = [None] * (TB * groups)
    for c in range(groups):
        hrows = slice(h_row0 + c * SUBLANES, h_row0 + (c + 1) * SUBLANES)
        h = h_read[hrows, :]
        for t in range(TB):
            r = t * S + c * SUBLANES
            h = a[r:r + SUBLANES] * h + u[r:r + SUBLANES]
            pieces[t * groups + c] = h * g[r:r + SUBLANES]
        h_write[hrows, :] = h
    return jnp.concatenate(pieces, axis=0)


def _rglru_stream_kernel(segs, layer, j, x_ref, *refs):
    n = len(segs)
    conv_in, h_in = refs[0:n], refs[n:2 * n]
    (npre_ref, npost_ref, wxr_ref, wgate_ref, cw_ref, cb_ref, gaw_ref, gxw_ref, gab_ref, gxb_ref,
     lam_ref, wout_ref) = refs[2 * n:2 * n + 12]
    o_ref = refs[2 * n + 12]
    conv_out, h_out = refs[2 * n + 13:3 * n + 13], refs[3 * n + 13:4 * n + 13]
    wxr_b_ref, wgate_b_ref, gw_b_ref, wout_b_ref = refs[4 * n + 13:4 * n + 17]
    xn_scr, xr_scr, y_scr = refs[4 * n + 17:]
    b = pl.program_id(0)

    @pl.when(b == 0)
    def _():
        xn_scr[...] = _rms(x_ref[...], npre_ref[layer:layer + 1, :]).astype(BF16)

    w_xr = wxr_ref[...].astype(BF16)
    w_gate = wgate_ref[...].astype(BF16)
    gw = (0.5 * jnp.concatenate([gaw_ref[...], gxw_ref[...]], axis=1)).astype(BF16)
    w_out_rows = wout_ref[...].astype(BF16)
    wxr_b_ref[...] = w_xr
    wgate_b_ref[...] = w_gate
    gw_b_ref[...] = gw
    wout_b_ref[...] = w_out_rows

    rows_of = lambda seg: slice(seg.row0, seg.row0 + seg.rows)
    xn = xn_scr[...]
    xr = _dot(xn, w_xr)
    g = _gelu_tanh(_dot(xn, w_gate))
    xcs = []
    for seg, c_in, c_out in zip(segs, conv_in, conv_out):
        P = (CONV_A - 1) * seg.S
        xr_scr[seg.conv0:seg.conv0 + P, :] = c_in[...]
        xr_scr[seg.conv0 + P:seg.conv0 + P + seg.rows, :] = xr[rows_of(seg)]
        xcs.append(_rglru_conv(seg, cw_ref[...], cb_ref[j:j + 1, :], xr_scr, seg.conv0))
        c_out[...] = xr_scr[seg.conv0:seg.conv0 + P, :]
    xc = jnp.concatenate(xcs, axis=0)
    a, u = _rglru_gate_math(_dot(xc.astype(BF16), gw), xc, 0.5 * gab_ref[pl.ds(b, 1), :],
                            0.5 * gxb_ref[pl.ds(b, 1), :], lam_ref[j:j + 1, :])
    hs = [_rglru_scan(seg, a[rows_of(seg)], u[rows_of(seg)], g[rows_of(seg)], hi, ho, 0)
          for seg, hi, ho in zip(segs, h_in, h_out)]
    part = _dot(jnp.concatenate(hs, axis=0).astype(BF16), w_out_rows)
    _accumulate(y_scr, slice(None), part, b)

    @pl.when(b == N_COL_BLOCKS - 1)
    def _():
        o_ref[...] = x_ref[...] + _rms(y_scr[...], npost_ref[layer:layer + 1, :])


def _rglru_stream_layer(x, conv_in, h_in, layer, j, w, *, seqs_and_steps):
    R, D = x.shape
    C = COL_BLOCK
    segs, _, conv_rows, _ = _segments(seqs_and_steps, CONV_A)
    n_layers, n_a = w["norm_mix_pre"].shape[0], w["rg_conv_b"].shape[0]
    col = lambda rows: pl.BlockSpec((rows, C), lambda b: (0, b))
    in_specs = (
        [_resident((R, D))] + [col(c.shape[0]) for c in conv_in] + [col(h.shape[0]) for h in h_in]
        + [_resident((n_layers, D)), _resident((n_layers, D)),
           pl.BlockSpec((None, D, C), lambda b: (j, 0, N_COL_BLOCKS + b)),
           pl.BlockSpec((None, D, C), lambda b: (j, 0, b)),
           pl.BlockSpec((None, CONV_A, C), lambda b: (j, 0, b)),
           col(n_a),
           pl.BlockSpec((None, None, C, C), lambda b: (j, b, 0, 0)),
           pl.BlockSpec((None, None, C, C), lambda b: (j, b, 0, 0)),
           pl.BlockSpec((None, N_COL_BLOCKS, C), lambda b: (j, 0, 0)),
           pl.BlockSpec((None, N_COL_BLOCKS, C), lambda b: (j, 0, 0)),
           col(n_a),
           pl.BlockSpec((None, C, D), lambda b: (j, b, 0))])
    out_specs = ([pl.BlockSpec((R, D), lambda b: (0, 0))] + [col(c.shape[0]) for c in conv_in]
                 + [col(h.shape[0]) for h in h_in]
                 + [_chunk_major_out(D), _chunk_major_out(D), pl.BlockSpec((None, C, 2 * C), lambda b: (b, 0, 0)),
                    pl.BlockSpec((C, D), lambda b: (b, 0))])
    out_shape = ([jax.ShapeDtypeStruct((R, D), F32)] + [jax.ShapeDtypeStruct(s.shape, F32) for s in conv_in + h_in]
                 + [jax.ShapeDtypeStruct((N_COL_BLOCKS, D, C), BF16), jax.ShapeDtypeStruct((N_COL_BLOCKS, D, C), BF16),
                    jax.ShapeDtypeStruct((N_COL_BLOCKS, C, 2 * C), BF16), jax.ShapeDtypeStruct((D, D), BF16)])
    outs = pl.pallas_call(
        functools.partial(_rglru_stream_kernel, segs, layer, j),
        grid=(N_COL_BLOCKS,),
        in_specs=in_specs, out_specs=out_specs, out_shape=out_shape,
        scratch_shapes=[pltpu.VMEM((R, D), BF16),
                        pltpu.VMEM((conv_rows, C), F32),
                        pltpu.VMEM((R, D), F32)],
        compiler_params=_params(),
        name="rglru_mixer_small",
    )(x, *conv_in, *h_in, w["norm_mix_pre"], w["norm_mix_post"], w["rg_w_in"], w["rg_w_in"], w["rg_conv_w"],
      w["rg_conv_b"], w["rg_gate_a_w"], w["rg_gate_x_w"], w["rg_gate_a_b"], w["rg_gate_x_b"], w["rg_lambda"],
      w["rg_w_out"])
    n = len(segs)
    bf16 = dict(w_xr=outs[2 * n + 1], w_gate=outs[2 * n + 2], gw=outs[2 * n + 3], w_out=outs[2 * n + 4])
    return outs[0], list(outs[1:1 + n]), list(outs[1 + n:1 + 2 * n]), bf16


def _rglru_kernel(seg, nblk, layer, j, batch_major_in,
                  x_ref, conv_in_ref, h_in_ref, npre_ref, npost_ref, wxr_ref, wgate_ref, cw_ref, cb_ref,
                  gw_ref, gab_ref, gxb_ref, lam_ref, wout_ref,
                  o_ref, conv_out_ref, h_out_ref, xn_scr, xr_scr, h_scr, *slab_scr):
    P = (CONV_A - 1) * seg.S
    i = pl.program_id(0)
    blocks = [slice(b * COL_BLOCK, (b + 1) * COL_BLOCK) for b in range(N_COL_BLOCKS)]

    @pl.when(i == 0)
    def _():
        for b, cols in enumerate(blocks):
            xr_scr[b, 0:P, :] = conv_in_ref[:, cols]
            h_scr[b] = h_in_ref[:, cols]

    if batch_major_in:
        _scatter_time_major(x_ref, slab_scr[0], seg.S, seg.TB)
        read_x = lambda: _read_slabs(slab_scr[0])
    else:
        read_x = lambda: x_ref[...]

    xn_scr[...] = _rms(read_x(), npre_ref[layer:layer + 1, :]).astype(BF16)
    xn = xn_scr[...]

    gate_pre, xc, res = {}, {}, {}

    def input_matmuls(b):
        _rglru_branch_in(seg, xn, wxr_ref[b], xr_scr.at[b])
        gate_pre[b] = _dot(xn, wgate_ref[b])

    def conv_and_gate_matmul(b):
        xc[b] = _rglru_conv(seg, cw_ref[:, blocks[b]], cb_ref[j:j + 1, blocks[b]], xr_scr.at[b], 0)
        res[b] = _dot(xc[b].astype(BF16), gw_ref[b])

    y = None
    input_matmuls(0)
    conv_and_gate_matmul(0)
    input_matmuls(1)
    for b, cols in enumerate(blocks):
        if b + 1 < N_COL_BLOCKS:
            conv_and_gate_matmul(b + 1)
        if b + 2 < N_COL_BLOCKS:
            input_matmuls(b + 2)
        g = _gelu_tanh(gate_pre.pop(b))
        a, u = _rglru_gate_math(res.pop(b), xc.pop(b), 0.5 * gab_ref[b:b + 1, :], 0.5 * gxb_ref[b:b + 1, :],
                                lam_ref[j:j + 1, cols])
        hs = _rglru_scan(seg, a, u, g, h_scr.at[b], h_scr.at[b], 0)
        part = _dot(hs.astype(BF16), wout_ref[cols, :])
        y = part if y is None else y + part

    o_ref[...] = read_x() + _rms(y, npost_ref[layer:layer + 1, :])

    @pl.when(i == nblk - 1)
    def _():
        for b, cols in enumerate(blocks):
            conv_out_ref[:, cols] = xr_scr[b, 0:P, :]
            h_out_ref[:, cols] = h_scr[b]


def _rglru_layer(x, conv_in, h_in, layer, j, w, bf16, *, S, TB):
    D = D_MODEL
    batch_major_in = x.ndim == 3
    (seg,), R, conv_rows, _ = _segments(((S, TB),), CONV_A)
    nblk = x.size // D // R
    P = (CONV_A - 1) * S
    n_layers, n_a = w["norm_mix_pre"].shape[0], w["rg_conv_b"].shape[0]
    row_spec = pl.BlockSpec((R, D), lambda i: (i, 0))
    x_spec = pl.BlockSpec((S, TB, D), lambda i: (0, i, 0)) if batch_major_in else row_spec
    layer_of = lambda arr: pl.BlockSpec((None,) + arr.shape[1:], lambda i: (j,) + (0,) * (arr.ndim - 1),
                                        pipeline_mode=pl.Buffered(1))
    return pl.pallas_call(
        functools.partial(_rglru_kernel, seg, nblk, layer, j, batch_major_in),
        grid=(nblk,),
        in_specs=[x_spec, _resident((P, D)), _resident((S, D)),
                  _resident((n_layers, D)), _resident((n_layers, D)),
                  _resident(bf16["w_xr"].shape), _resident(bf16["w_gate"].shape),
                  layer_of(w["rg_conv_w"]), _resident((n_a, D)), _resident(bf16["gw"].shape),
                  layer_of(w["rg_gate_a_b"]), layer_of(w["rg_gate_x_b"]), _resident((n_a, D)),
                  _resident((D, D))],
        out_specs=[row_spec, pl.BlockSpec((P, D), lambda i: (0, 0)), pl.BlockSpec((S, D), lambda i: (0, 0))],
        out_shape=[jax.ShapeDtypeStruct((nblk * R, D), F32), jax.ShapeDtypeStruct((P, D), F32),
                   jax.ShapeDtypeStruct((S, D), F32)],
        scratch_shapes=[pltpu.VMEM((R, D), BF16),
                        pltpu.VMEM((N_COL_BLOCKS, conv_rows, COL_BLOCK), F32),
                        pltpu.VMEM((N_COL_BLOCKS, S, COL_BLOCK), F32)]
                       + ([_slab_scratch(R)] if batch_major_in else []),
        compiler_params=_params(),
        name="rglru_mixer",
    )(x, conv_in, h_in, w["norm_mix_pre"], w["norm_mix_post"], bf16["w_xr"], bf16["w_gate"], w["rg_conv_w"],
      w["rg_conv_b"], bf16["gw"], w["rg_gate_a_b"], w["rg_gate_x_b"], w["rg_lambda"], bf16["w_out"])


def _sconv_conv(seg, cv, cw, region):
    S, R = seg.S, seg.rows
    P = (CONV_B - 1) * S
    c0 = seg.conv0
    region[c0 + P:c0 + P + R, :] = cv
    conv = region[c0:c0 + R, :] * cw[0:1]
    for k in range(1, CONV_B):
        conv = conv + region[c0 + k * S:c0 + k * S + R, :] * cw[k:k + 1]
    region[c0:c0 + P, :] = region[c0 + R:c0 + R + P, :]
    return conv


def _sconv_stream_kernel(segs, layer, j, x_ref, *refs):
    n = len(segs)
    conv_in = refs[0:n]
    npre_ref, npost_ref, wbg_ref, wcg_ref, wv_ref, cw_ref, wout_ref = refs[n:n + 7]
    o_ref = refs[n + 7]
    conv_out = refs[n + 8:2 * n + 8]
    wbg_b_ref, wcg_b_ref, wv_b_ref, wout_b_ref = refs[2 * n + 8:2 * n + 12]
    xn_scr, cv_scr, y_scr = refs[2 * n + 12:]
    b = pl.program_id(0)

    @pl.when(b == 0)
    def _():
        xn_scr[...] = _rms(x_ref[...], npre_ref[layer:layer + 1, :]).astype(BF16)

    w_bg = wbg_ref[...].astype(BF16)
    w_cg = wcg_ref[...].astype(BF16)
    w_v = wv_ref[...].astype(BF16)
    w_out_rows = wout_ref[...].astype(BF16)
    wbg_b_ref[...] = w_bg
    wcg_b_ref[...] = w_cg
    wv_b_ref[...] = w_v
    wout_b_ref[...] = w_out_rows

    xn = xn_scr[...]
    cv = _dot(xn, w_cg) * _dot(xn, w_v)
    convs = []
    for seg, c_in, c_out in zip(segs, conv_in, conv_out):
        P = (CONV_B - 1) * seg.S
        cv_scr[seg.conv0:seg.conv0 + P, :] = c_in[...]
        convs.append(_sconv_conv(seg, cv[seg.row0:seg.row0 + seg.rows], cw_ref[...], cv_scr))
        c_out[...] = cv_scr[seg.conv0:seg.conv0 + P, :]
    m = (_dot(xn, w_bg) * jnp.concatenate(convs, axis=0)).astype(BF16)
    _accumulate(y_scr, slice(None), _dot(m, w_out_rows), b)

    @pl.when(b == N_COL_BLOCKS - 1)
    def _():
        o_ref[...] = x_ref[...] + _rms(y_scr[...], npost_ref[layer:layer + 1, :])


def _sconv_stream_layer(x, conv_in, layer, j, w, *, seqs_and_steps):
    R, D = x.shape
    C = COL_BLOCK
    segs, _, conv_rows, _ = _segments(seqs_and_steps, CONV_B)
    n_layers = w["norm_mix_pre"].shape[0]
    col = lambda rows: pl.BlockSpec((rows, C), lambda b: (0, b))
    w_in_part = lambda k: pl.BlockSpec((None, D, C), lambda b: (j, 0, k * N_COL_BLOCKS + b))
    outs = pl.pallas_call(
        functools.partial(_sconv_stream_kernel, segs, layer, j),
        grid=(N_COL_BLOCKS,),
        in_specs=[_resident((R, D))] + [col(c.shape[0]) for c in conv_in]
                 + [_resident((n_layers, D)), _resident((n_layers, D)), w_in_part(0), w_in_part(1), w_in_part(2),
                    pl.BlockSpec((None, CONV_B, C), lambda b: (j, 0, b)),
                    pl.BlockSpec((None, C, D), lambda b: (j, b, 0))],
        out_specs=[pl.BlockSpec((R, D), lambda b: (0, 0))] + [col(c.shape[0]) for c in conv_in]
                  + [_chunk_major_out(D)] * 3 + [pl.BlockSpec((C, D), lambda b: (b, 0))],
        out_shape=[jax.ShapeDtypeStruct((R, D), F32)] + [jax.ShapeDtypeStruct(c.shape, F32) for c in conv_in]
                  + [jax.ShapeDtypeStruct((N_COL_BLOCKS, D, C), BF16)] * 3 + [jax.ShapeDtypeStruct((D, D), BF16)],
        scratch_shapes=[pltpu.VMEM((R, D), BF16),
                        pltpu.VMEM((conv_rows, C), F32),
                        pltpu.VMEM((R, D), F32)],
        compiler_params=_params(),
        name="sconv_mixer_small",
    )(x, *conv_in, w["norm_mix_pre"], w["norm_mix_post"], w["sc_w_in"], w["sc_w_in"], w["sc_w_in"],
      w["sc_conv_w"], w["sc_w_out"])
    n = len(segs)
    bf16 = dict(w_bg=outs[n + 1], w_cg=outs[n + 2], w_v=outs[n + 3], w_out=outs[n + 4])
    return outs[0], list(outs[1:1 + n]), bf16


def _sconv_kernel(seg, nblk, layer, x_ref, conv_in_ref, npre_ref, npost_ref, wbg_ref, wcg_ref, wv_ref,
                  cw_ref, wout_ref, o_ref, conv_out_ref, xn_scr, cv_scr, m_scr):
    P = (CONV_B - 1) * seg.S
    i = pl.program_id(0)
    blocks = [slice(b * COL_BLOCK, (b + 1) * COL_BLOCK) for b in range(N_COL_BLOCKS)]

    @pl.when(i == 0)
    def _():
        for b, cols in enumerate(blocks):
            cv_scr[b, 0:P, :] = conv_in_ref[:, cols]

    xn_scr[...] = _rms(x_ref[...], npre_ref[layer:layer + 1, :]).astype(BF16)
    xn = xn_scr[...]
    for b, cols in enumerate(blocks):
        cv = _dot(xn, wcg_ref[b]) * _dot(xn, wv_ref[b])
        conv = _sconv_conv(seg, cv, cw_ref[:, cols], cv_scr.at[b])
        m_scr[:, cols] = (_dot(xn, wbg_ref[b]) * conv).astype(BF16)
    y = _dot(m_scr[...], wout_ref[...])
    o_ref[...] = x_ref[...] + _rms(y, npost_ref[layer:layer + 1, :])

    @pl.when(i == nblk - 1)
    def _():
        for b, cols in enumerate(blocks):
            conv_out_ref[:, cols] = cv_scr[b, 0:P, :]


def _sconv_layer(x, conv_in, layer, j, w, bf16, *, S, TB):
    N, D = x.shape
    (seg,), R, conv_rows, _ = _segments(((S, TB),), CONV_B)
    nblk = N // R
    P = (CONV_B - 1) * S
    n_layers = w["norm_mix_pre"].shape[0]
    row_spec = pl.BlockSpec((R, D), lambda i: (i, 0))
    return pl.pallas_call(
        functools.partial(_sconv_kernel, seg, nblk, layer),
        grid=(nblk,),
        in_specs=[row_spec, _resident((P, D)), _resident((n_layers, D)), _resident((n_layers, D)),
                  _resident(bf16["w_bg"].shape), _resident(bf16["w_cg"].shape), _resident(bf16["w_v"].shape),
                  pl.BlockSpec((None, CONV_B, D), lambda i: (j, 0, 0), pipeline_mode=pl.Buffered(1)),
                  _resident((D, D))],
        out_specs=[row_spec, pl.BlockSpec((P, D), lambda i: (0, 0))],
        out_shape=[jax.ShapeDtypeStruct((N, D), F32), jax.ShapeDtypeStruct((P, D), F32)],
        scratch_shapes=[pltpu.VMEM((R, D), BF16),
                        pltpu.VMEM((N_COL_BLOCKS, conv_rows, COL_BLOCK), F32),
                        pltpu.VMEM((R, D), BF16)],
        compiler_params=_params(),
        name="sconv_mixer",
    )(x, conv_in, w["norm_mix_pre"], w["norm_mix_post"], bf16["w_bg"], bf16["w_cg"], bf16["w_v"],
      w["sc_conv_w"], bf16["w_out"])


def _swiglu(g, u):
    return (g * jax.nn.sigmoid(g) * u).astype(BF16)


def _ffn_stream_kernel(layer, nchunk, x_ref, npre_ref, npost_ref, wg_ref, wu_ref, wd_ref,
                       o_ref, wg_b_ref, wu_b_ref, wd_b_ref, xn_scr, y_scr):
    c = pl.program_id(0)

    @pl.when(c == 0)
    def _():
        xn_scr[...] = _rms(x_ref[...], npre_ref[layer:layer + 1, :]).astype(BF16)

    wg = wg_ref[...].astype(BF16)
    wu = wu_ref[...].astype(BF16)
    wd = wd_ref[...].astype(BF16)
    wg_b_ref[...] = wg
    wu_b_ref[...] = wu
    wd_b_ref[...] = wd
    xn = xn_scr[...]
    part = _dot(_swiglu(_dot(xn, wg), _dot(xn, wu)), wd)
    _accumulate(y_scr, slice(None), part, c)

    @pl.when(c == nchunk - 1)
    def _():
        o_ref[...] = x_ref[...] + _rms(y_scr[...], npost_ref[layer:layer + 1, :])


def _ffn_stream_layer(x, layer, w):
    R, D = x.shape
    n_layers = w["norm_ffn_pre"].shape[0]
    nchunk = D_FF // FF_CHUNK
    cols = lambda: pl.BlockSpec((None, D, FF_CHUNK), lambda c: (layer, 0, c))
    outs = pl.pallas_call(
        functools.partial(_ffn_stream_kernel, layer, nchunk),
        grid=(nchunk,),
        in_specs=[_resident((R, D)), _resident((n_layers, D)), _resident((n_layers, D)), cols(), cols(),
                  pl.BlockSpec((None, FF_CHUNK, D), lambda c: (layer, c, 0))],
        out_specs=[pl.BlockSpec((R, D), lambda c: (0, 0)),
                   _chunk_major_out(D, FF_CHUNK), _chunk_major_out(D, FF_CHUNK),
                   pl.BlockSpec((FF_CHUNK, D), lambda c: (c, 0))],
        out_shape=[jax.ShapeDtypeStruct((R, D), F32), jax.ShapeDtypeStruct((nchunk, D, FF_CHUNK), BF16),
                   jax.ShapeDtypeStruct((nchunk, D, FF_CHUNK), BF16), jax.ShapeDtypeStruct((D_FF, D), BF16)],
        scratch_shapes=[pltpu.VMEM((R, D), BF16), pltpu.VMEM((R, D), F32)],
        compiler_params=_params(),
        name="swiglu_ffn_small",
    )(x, w["norm_ffn_pre"], w["norm_ffn_post"], w["ffn_w_gate"], w["ffn_w_up"], w["ffn_w_down"])
    return outs[0], dict(wg=outs[1], wu=outs[2], wd=outs[3])


def _ffn_kernel(layer, batch_major_out, x_ref, npre_ref, npost_ref, wg_ref, wu_ref, wd_ref, o_ref,
                xn_scr, act_scr, *slab_scr):
    R = x_ref.shape[0]
    halves = [slice(0, R // 2), slice(R // 2, R)]
    for rows in halves:
        xn_scr[rows, :] = _rms(x_ref[rows, :], npre_ref[layer:layer + 1, :]).astype(BF16)
    for rows in halves:
        xn = xn_scr[rows, :]
        for c in range(D_FF // FF_CHUNK):
            cols = slice(c * FF_CHUNK, (c + 1) * FF_CHUNK)
            act_scr[rows, cols] = _swiglu(_dot(xn, wg_ref[c]), _dot(xn, wu_ref[c]))
    ys = [_dot(act_scr[rows, :], wd_ref[...]) for rows in halves]
    for rows, y in zip(halves, ys):
        out = x_ref[rows, :] + _rms(y, npost_ref[layer:layer + 1, :])
        if batch_major_out:
            _store_batch_major(out, rows, slab_scr[0], o_ref)
        else:
            o_ref[rows, :] = out


def _ffn_layer(x, layer, w, bf16, *, R, batch_major_seqs=None):
    N, D = x.shape
    n_layers = w["norm_ffn_pre"].shape[0]
    row_spec = pl.BlockSpec((R, D), lambda i: (i, 0))
    if batch_major_seqs is None:
        out_spec, out_shape, slabs = row_spec, jax.ShapeDtypeStruct((N, D), F32), []
    else:
        S = batch_major_seqs
        out_spec = pl.BlockSpec((S, R // S, D), lambda i: (0, i, 0))
        out_shape, slabs = jax.ShapeDtypeStruct((S, N // S, D), F32), [_slab_scratch(R)]
    return pl.pallas_call(
        functools.partial(_ffn_kernel, layer, batch_major_seqs is not None),
        grid=(N // R,),
        in_specs=[row_spec, _resident((n_layers, D)), _resident((n_layers, D)),
                  _resident(bf16["wg"].shape), _resident(bf16["wu"].shape), _resident((D_FF, D))],
        out_specs=out_spec,
        out_shape=out_shape,
        scratch_shapes=[pltpu.VMEM((R, D), BF16), pltpu.VMEM((R, D_FF), BF16)] + slabs,
        compiler_params=_params(),
        name="swiglu_ffn",
    )(x, w["norm_ffn_pre"], w["norm_ffn_post"], bf16["wg"], bf16["wu"], bf16["wd"])


def _to_time_major(a):
    S, K, D = a.shape
    return jnp.swapaxes(a, 0, 1).reshape(K * S, D)


def _from_time_major(a, S):
    KS, D = a.shape
    return jnp.swapaxes(a.reshape(KS // S, S, D), 0, 1)


def kernel(x_prompt, x_sample, state_rglru_conv, state_rglru_h, state_sconv, meta_tokens, norm_mix_pre, norm_mix_post, norm_ffn_pre, norm_ffn_post, rg_w_in, rg_conv_w, rg_conv_b, rg_gate_a_w, rg_gate_a_b, rg_gate_x_w, rg_gate_x_b, rg_lambda, rg_w_out, sc_w_in, sc_conv_w, sc_w_out, ffn_w_gate, ffn_w_up, ffn_w_down):
    D = D_MODEL
    depth = norm_mix_pre.shape[0]
    batch, seq, _ = x_prompt.shape
    dec_batch, dec_seq, _ = x_sample.shape
    w = dict(norm_mix_pre=norm_mix_pre, norm_mix_post=norm_mix_post, norm_ffn_pre=norm_ffn_pre,
             norm_ffn_post=norm_ffn_post, rg_w_in=rg_w_in, rg_conv_w=rg_conv_w, rg_conv_b=rg_conv_b,
             rg_gate_a_w=rg_gate_a_w, rg_gate_a_b=rg_gate_a_b, rg_gate_x_w=rg_gate_x_w, rg_gate_x_b=rg_gate_x_b,
             rg_lambda=rg_lambda, rg_w_out=rg_w_out, sc_w_in=sc_w_in, sc_conv_w=sc_conv_w, sc_w_out=sc_w_out,
             ffn_w_gate=ffn_w_gate, ffn_w_up=ffn_w_up, ffn_w_down=ffn_w_down)

    xm = jnp.broadcast_to(meta_tokens[:, None, :], (N_META, batch, D)).reshape(N_META * batch, D)
    x = jnp.concatenate([xm, _to_time_major(x_sample)], axis=0)
    small = ((batch, N_META), (dec_batch, dec_seq))
    mixer_bf16, ffn_bf16 = [], []
    rg_conv_s, rg_h_s, sc_s = [], [], []
    for i in range(depth):
        j = i // 2
        if i % 2 == 0:
            conv0 = [jnp.zeros(((CONV_A - 1) * batch, D), F32), _to_time_major(state_rglru_conv[j])]
            h0 = [jnp.zeros((batch, D), F32), state_rglru_h[j]]
            x, cb, hT, wb = _rglru_stream_layer(x, conv0, h0, i, j, w, seqs_and_steps=small)
            rg_conv_s.append(cb)
            rg_h_s.append(hT)
        else:
            conv0 = [jnp.zeros(((CONV_B - 1) * batch, D), F32), _to_time_major(state_sconv[j])]
            x, cb, wb = _sconv_stream_layer(x, conv0, i, j, w, seqs_and_steps=small)
            sc_s.append(cb)
        mixer_bf16.append(wb)
        x, wb = _ffn_stream_layer(x, i, w)
        ffn_bf16.append(wb)
    y_sample = _from_time_major(x[N_META * batch:], dec_batch)

    x = x_prompt
    rg_conv_p, rg_h_p, sc_p = [], [], []
    for i in range(depth):
        j = i // 2
        if i % 2 == 0:
            x, cb, hT = _rglru_layer(x, rg_conv_s[j][0], rg_h_s[j][0], i, j, w, mixer_bf16[i], S=batch, TB=PROMPT_TB)
            rg_conv_p.append(cb)
            rg_h_p.append(hT)
        else:
            x, cb = _sconv_layer(x, sc_s[j][0], i, j, w, mixer_bf16[i], S=batch, TB=PROMPT_TB)
            sc_p.append(cb)
        x = _ffn_layer(x, i, w, ffn_bf16[i], R=batch * PROMPT_TB,
                       batch_major_seqs=batch if i == depth - 1 else None)
    y_prompt = x

    return (y_prompt, y_sample,
            jnp.stack([_from_time_major(c, batch) for c in rg_conv_p]), jnp.stack(rg_h_p),
            jnp.stack([_from_time_major(c, batch) for c in sc_p]),
            jnp.stack([_from_time_major(c[1], dec_batch) for c in rg_conv_s]), jnp.stack([h[1] for h in rg_h_s]),
            jnp.stack([_from_time_major(c[1], dec_batch) for c in sc_s]))
```

```python
import functools
from typing import NamedTuple

import jax
import jax.numpy as jnp
from jax import lax
from jax.experimental import pallas as pl
from jax.experimental.pallas import tpu as pltpu

D_MODEL = 1024
D_FF = 2816
N_META = 16
COL_BLOCK = 256
N_COL_BLOCKS = D_MODEL // COL_BLOCK
CONV_A = 4
CONV_B = 3
RG_C = 8.0
EPS = 1e-6

SUBLANES = 8
LANES = 128
FF_CHUNK = 256
PROMPT_TB = 128
VMEM_LIMIT_BYTES = 56 * 1024 * 1024

F32 = jnp.float32
BF16 = jnp.bfloat16


class _Seg(NamedTuple):
    S: int
    TB: int
    row0: int
    conv0: int
    h0: int

    @property
    def rows(self):
        return self.S * self.TB


def _segments(seqs_and_steps, taps):
    segs, row0, conv0, h0 = [], 0, 0, 0
    for S, TB in seqs_and_steps:
        segs.append(_Seg(S, TB, row0, conv0, h0))
        row0 += S * TB
        conv0 += (taps - 1) * S + S * TB
        h0 += S
    return tuple(segs), row0, conv0, h0


def _rms(x, w):
    ms = jnp.mean(x * x, axis=-1, keepdims=True)
    return x * lax.rsqrt(ms + EPS) * w


def _dot(a, b):
    return jnp.dot(a, b, preferred_element_type=F32)


def _scatter_time_major(x_ref, slab_scr, S, TB):
    for s in range(S):
        for j in range(D_MODEL // LANES):
            slab_scr[j, pl.ds(s, TB, stride=S), :] = x_ref[s, :, j * LANES:(j + 1) * LANES]


def _read_slabs(slab_scr):
    return jnp.concatenate([slab_scr[j] for j in range(D_MODEL // LANES)], axis=1)


def _store_batch_major(y, rows, slab_scr, o_ref):
    S = o_ref.shape[0]
    t0, steps = rows.start // S, (rows.stop - rows.start) // S
    for j in range(D_MODEL // LANES):
        slab_scr[j, rows, :] = y[:, j * LANES:(j + 1) * LANES]
    for s in range(S):
        for j in range(D_MODEL // LANES):
            o_ref[s, t0:t0 + steps, j * LANES:(j + 1) * LANES] = slab_scr[j, pl.ds(rows.start + s, steps, stride=S), :]


def _gelu_tanh(x):
    c = 0.7978845608028654
    hx = 0.5 * x
    return hx + hx * jnp.tanh(x * (c + (c * 0.044715) * (x * x)))


def _resident(shape):
    zeros = (0,) * len(shape)
    return pl.BlockSpec(shape, lambda i: zeros, pipeline_mode=pl.Buffered(1))


def _chunk_major_out(rows, width=COL_BLOCK):
    return pl.BlockSpec((None, rows, width), lambda i: (i, 0, 0))


def _slab_scratch(R):
    return pltpu.VMEM((D_MODEL // LANES, R, LANES), F32)


def _params():
    return pltpu.CompilerParams(dimension_semantics=("arbitrary",), vmem_limit_bytes=VMEM_LIMIT_BYTES)


def _accumulate(y_scr, rows, part, step):
    @pl.when(step == 0)
    def _():
        y_scr[rows, :] = part

    @pl.when(step > 0)
    def _():
        y_scr[rows, :] += part


def _rglru_branch_in(seg, xn, w_xr, region):
    P = (CONV_A - 1) * seg.S
    region[seg.conv0 + P:seg.conv0 + P + seg.rows, :] = _dot(xn, w_xr)


def _rglru_conv(seg, cw, cb, region, carry_to):
    S, R = seg.S, seg.rows
    P = (CONV_A - 1) * S
    c0 = seg.conv0
    xc = region[c0:c0 + R, :] * cw[0:1]
    for k in range(1, CONV_A):
        xc = xc + region[c0 + k * S:c0 + k * S + R, :] * cw[k:k + 1]
    xc = xc + cb
    if carry_to is not None:
        region[carry_to:carry_to + P, :] = region[c0 + R:c0 + R + P, :]
    return xc


def _rglru_gate_math(res, xc, half_gab, half_gxb, lam):
    half_c_sp = (-0.5 * RG_C) * jax.nn.softplus(-lam)
    tr = jnp.tanh(res[:, 0:COL_BLOCK] + half_gab)
    log_a = half_c_sp * tr + half_c_sp
    ig = 0.5 * jnp.tanh(res[:, COL_BLOCK:2 * COL_BLOCK] + half_gxb) + 0.5
    a = jnp.exp(log_a)
    m2 = jnp.tanh(log_a) * (-1.0 - a * a)
    u = jnp.where(m2 > 0.0, m2 * lax.rsqrt(m2), 0.0) * (ig * xc)
    return a, u


def _rglru_scan(seg, a, u, g, h_read, h_write, h_row0):
    S, TB = seg.S, seg.TB
    groups = S // SUBLANES
    pieces = [None] * (TB * groups)
    for c in range(groups):
        hrows = slice(h_row0 + c * SUBLANES, h_row0 + (c + 1) * SUBLANES)
        h = h_read[hrows, :]
        for t in range(TB):
            r = t * S + c * SUBLANES
            h = a[r:r + SUBLANES] * h + u[r:r + SUBLANES]
            pieces[t * groups + c] = h * g[r:r + SUBLANES]
        h_write[hrows, :] = h
    return jnp.concatenate(pieces, axis=0)


def _rglru_stream_kernel(segs, layer, j, x_ref, *refs):
    n = len(segs)
    conv_in, h_in = refs[0:n], refs[n:2 * n]
    (npre_ref, npost_ref, wxr_ref, wgate_ref, cw_ref, cb_ref, gaw_ref, gxw_ref, gab_ref, gxb_ref,
     lam_ref, wout_ref) = refs[2 * n:2 * n + 12]
    o_ref = refs[2 * n + 12]
    conv_out, h_out = refs[2 * n + 13:3 * n + 13], refs[3 * n + 13:4 * n + 13]
    wxr_b_ref, wgate_b_ref, gw_b_ref, wout_b_ref = refs[4 * n + 13:4 * n + 17]
    xn_scr, xr_scr, y_scr = refs[4 * n + 17:]
    b = pl.program_id(0)

    @pl.when(b == 0)
    def _():
        xn_scr[...] = _rms(x_ref[...], npre_ref[layer:layer + 1, :]).astype(BF16)

    w_xr = wxr_ref[...].astype(BF16)
    w_gate = wgate_ref[...].astype(BF16)
    gw = (0.5 * jnp.concatenate([gaw_ref[...], gxw_ref[...]], axis=1)).astype(BF16)
    w_out_rows = wout_ref[...].astype(BF16)
    wxr_b_ref[...] = w_xr
    wgate_b_ref[...] = w_gate
    gw_b_ref[...] = gw
    wout_b_ref[...] = w_out_rows

    rows_of = lambda seg: slice(seg.row0, seg.row0 + seg.rows)
    xn = xn_scr[...]
    xr = _dot(xn, w_xr)
    g = _gelu_tanh(_dot(xn, w_gate))
    xcs = []
    for seg, c_in, c_out in zip(segs, conv_in, conv_out):
        P = (CONV_A - 1) * seg.S
        xr_scr[seg.conv0:seg.conv0 + P, :] = c_in[...]
        xr_scr[seg.conv0 + P:seg.conv0 + P + seg.rows, :] = xr[rows_of(seg)]
        xcs.append(_rglru_conv(seg, cw_ref[...], cb_ref[j:j + 1, :], xr_scr, seg.conv0))
        c_out[...] = xr_scr[seg.conv0:seg.conv0 + P, :]
    xc = jnp.concatenate(xcs, axis=0)
    a, u = _rglru_gate_math(_dot(xc.astype(BF16), gw), xc, 0.5 * gab_ref[pl.ds(b, 1), :],
                            0.5 * gxb_ref[pl.ds(b, 1), :], lam_ref[j:j + 1, :])
    hs = [_rglru_scan(seg, a[rows_of(seg)], u[rows_of(seg)], g[rows_of(seg)], hi, ho, 0)
          for seg, hi, ho in zip(segs, h_in, h_out)]
    part = _dot(jnp.concatenate(hs, axis=0).astype(BF16), w_out_rows)
    _accumulate(y_scr, slice(None), part, b)

    @pl.when(b == N_COL_BLOCKS - 1)
    def _():
        o_ref[...] = x_ref[...] + _rms(y_scr[...], npost_ref[layer:layer + 1, :])


def _rglru_stream_layer(x, conv_in, h_in, layer, j, w, *, seqs_and_steps):
    R, D = x.shape
    C = COL_BLOCK
    segs, _, conv_rows, _ = _segments(seqs_and_steps, CONV_A)
    n_layers, n_a = w["norm_mix_pre"].shape[0], w["rg_conv_b"].shape[0]
    col = lambda rows: pl.BlockSpec((rows, C), lambda b: (0, b))
    in_specs = (
        [_resident((R, D))] + [col(c.shape[0]) for c in conv_in] + [col(h.shape[0]) for h in h_in]
        + [_resident((n_layers, D)), _resident((n_layers, D)),
           pl.BlockSpec((None, D, C), lambda b: (j, 0, N_COL_BLOCKS + b)),
           pl.BlockSpec((None, D, C), lambda b: (j, 0, b)),
           pl.BlockSpec((None, CONV_A, C), lambda b: (j, 0, b)),
           col(n_a),
           pl.BlockSpec((None, None, C, C), lambda b: (j, b, 0, 0)),
           pl.BlockSpec((None, None, C, C), lambda b: (j, b, 0, 0)),
           pl.BlockSpec((None, N_COL_BLOCKS, C), lambda b: (j, 0, 0)),
           pl.BlockSpec((None, N_COL_BLOCKS, C), lambda b: (j, 0, 0)),
           col(n_a),
           pl.BlockSpec((None, C, D), lambda b: (j, b, 0))])
    out_specs = ([pl.BlockSpec((R, D), lambda b: (0, 0))] + [col(c.shape[0]) for c in conv_in]
                 + [col(h.shape[0]) for h in h_in]
                 + [_chunk_major_out(D), _chunk_major_out(D), pl.BlockSpec((None, C, 2 * C), lambda b: (b, 0, 0)),
                    pl.BlockSpec((C, D), lambda b: (b, 0))])
    out_shape = ([jax.ShapeDtypeStruct((R, D), F32)] + [jax.ShapeDtypeStruct(s.shape, F32) for s in conv_in + h_in]
                 + [jax.ShapeDtypeStruct((N_COL_BLOCKS, D, C), BF16), jax.ShapeDtypeStruct((N_COL_BLOCKS, D, C), BF16),
                    jax.ShapeDtypeStruct((N_COL_BLOCKS, C, 2 * C), BF16), jax.ShapeDtypeStruct((D, D), BF16)])
    outs = pl.pallas_call(
        functools.partial(_rglru_stream_kernel, segs, layer, j),
        grid=(N_COL_BLOCKS,),
        in_specs=in_specs, out_specs=out_specs, out_shape=out_shape,
        scratch_shapes=[pltpu.VMEM((R, D), BF16),
                        pltpu.VMEM((conv_rows, C), F32),
                        pltpu.VMEM((R, D), F32)],
        compiler_params=_params(),
        name="rglru_mixer_small",
    )(x, *conv_in, *h_in, w["norm_mix_pre"], w["norm_mix_post"], w["rg_w_in"], w["rg_w_in"], w["rg_conv_w"],
      w["rg_conv_b"], w["rg_gate_a_w"], w["rg_gate_x_w"], w["rg_gate_a_b"], w["rg_gate_x_b"], w["rg_lambda"],
      w["rg_w_out"])
    n = len(segs)
    bf16 = dict(w_xr=outs[2 * n + 1], w_gate=outs[2 * n + 2], gw=outs[2 * n + 3], w_out=outs[2 * n + 4])
    return outs[0], list(outs[1:1 + n]), list(outs[1 + n:1 + 2 * n]), bf16


def _rglru_kernel(seg, nblk, layer, j, batch_major_in,
                  x_ref, conv_in_ref, h_in_ref, npre_ref, npost_ref, wxr_ref, wgate_ref, cw_ref, cb_ref,
                  gw_ref, gab_ref, gxb_ref, lam_ref, wout_ref,
                  o_ref, conv_out_ref, h_out_ref, xn_scr, xr_scr, h_scr, *slab_scr):
    P = (CONV_A - 1) * seg.S
    i = pl.program_id(0)
    blocks = [slice(b * COL_BLOCK, (b + 1) * COL_BLOCK) for b in range(N_COL_BLOCKS)]

    @pl.when(i == 0)
    def _():
        for b, cols in enumerate(blocks):
            xr_scr[b, 0:P, :] = conv_in_ref[:, cols]
            h_scr[b] = h_in_ref[:, cols]

    if batch_major_in:
        _scatter_time_major(x_ref, slab_scr[0], seg.S, seg.TB)
        read_x = lambda: _read_slabs(slab_scr[0])
    else:
        read_x = lambda: x_ref[...]

    xn_scr[...] = _rms(read_x(), npre_ref[layer:layer + 1, :]).astype(BF16)
    xn = xn_scr[...]

    gate_pre, xc, res = {}, {}, {}

    def input_matmuls(b):
        _rglru_branch_in(seg, xn, wxr_ref[b], xr_scr.at[b])
        gate_pre[b] = _dot(xn, wgate_ref[b])

    def conv_and_gate_matmul(b):
        xc[b] = _rglru_conv(seg, cw_ref[:, blocks[b]], cb_ref[j:j + 1, blocks[b]], xr_scr.at[b], 0)
        res[b] = _dot(xc[b].astype(BF16), gw_ref[b])

    y = None
    input_matmuls(0)
    conv_and_gate_matmul(0)
    input_matmuls(1)
    for b, cols in enumerate(blocks):
        if b + 1 < N_COL_BLOCKS:
            conv_and_gate_matmul(b + 1)
        if b + 2 < N_COL_BLOCKS:
            input_matmuls(b + 2)
        g = _gelu_tanh(gate_pre.pop(b))
        a, u = _rglru_gate_math(res.pop(b), xc.pop(b), 0.5 * gab_ref[b:b + 1, :], 0.5 * gxb_ref[b:b + 1, :],
                                lam_ref[j:j + 1, cols])
        hs = _rglru_scan(seg, a, u, g, h_scr.at[b], h_scr.at[b], 0)
        part = _dot(hs.astype(BF16), wout_ref[cols, :])
        y = part if y is None else y + part

    o_ref[...] = read_x() + _rms(y, npost_ref[layer:layer + 1, :])

    @pl.when(i == nblk - 1)
    def _():
        for b, cols in enumerate(blocks):
            conv_out_ref[:, cols] = xr_scr[b, 0:P, :]
            h_out_ref[:, cols] = h_scr[b]


def _rglru_layer(x, conv_in, h_in, layer, j, w, bf16, *, S, TB):
    D = D_MODEL
    batch_major_in = x.ndim == 3
    (seg,), R, conv_rows, _ = _segments(((S, TB),), CONV_A)
    nblk = x.size // D // R
    P = (CONV_A - 1) * S
    n_layers, n_a = w["norm_mix_pre"].shape[0], w["rg_conv_b"].shape[0]
    row_spec = pl.BlockSpec((R, D), lambda i: (i, 0))
    x_spec = pl.BlockSpec((S, TB, D), lambda i: (0, i, 0)) if batch_major_in else row_spec
    layer_of = lambda arr: pl.BlockSpec((None,) + arr.shape[1:], lambda i: (j,) + (0,) * (arr.ndim - 1),
                                        pipeline_mode=pl.Buffered(1))
    return pl.pallas_call(
        functools.partial(_rglru_kernel, seg, nblk, layer, j, batch_major_in),
        grid=(nblk,),
        in_specs=[x_spec, _resident((P, D)), _resident((S, D)),
                  _resident((n_layers, D)), _resident((n_layers, D)),
                  _resident(bf16["w_xr"].shape), _resident(bf16["w_gate"].shape),
                  layer_of(w["rg_conv_w"]), _resident((n_a, D)), _resident(bf16["gw"].shape),
                  layer_of(w["rg_gate_a_b"]), layer_of(w["rg_gate_x_b"]), _resident((n_a, D)),
                  _resident((D, D))],
        out_specs=[row_spec, pl.BlockSpec((P, D), lambda i: (0, 0)), pl.BlockSpec((S, D), lambda i: (0, 0))],
        out_shape=[jax.ShapeDtypeStruct((nblk * R, D), F32), jax.ShapeDtypeStruct((P, D), F32),
                   jax.ShapeDtypeStruct((S, D), F32)],
        scratch_shapes=[pltpu.VMEM((R, D), BF16),
                        pltpu.VMEM((N_COL_BLOCKS, conv_rows, COL_BLOCK), F32),
                        pltpu.VMEM((N_COL_BLOCKS, S, COL_BLOCK), F32)]
                       + ([_slab_scratch(R)] if batch_major_in else []),
        compiler_params=_params(),
        name="rglru_mixer",
    )(x, conv_in, h_in, w["norm_mix_pre"], w["norm_mix_post"], bf16["w_xr"], bf16["w_gate"], w["rg_conv_w"],
      w["rg_conv_b"], bf16["gw"], w["rg_gate_a_b"], w["rg_gate_x_b"], w["rg_lambda"], bf16["w_out"])


def _sconv_conv(seg, cv, cw, region):
    S, R = seg.S, seg.rows
    P = (CONV_B - 1) * S
    c0 = seg.conv0
    region[c0 + P:c0 + P + R, :] = cv
    conv = region[c0:c0 + R, :] * cw[0:1]
    for k in range(1, CONV_B):
        conv = conv + region[c0 + k * S:c0 + k * S + R, :] * cw[k:k + 1]
    region[c0:c0 + P, :] = region[c0 + R:c0 + R + P, :]
    return conv


def _sconv_stream_kernel(segs, layer, j, x_ref, *refs):
    n = len(segs)
    conv_in = refs[0:n]
    npre_ref, npost_ref, wbg_ref, wcg_ref, wv_ref, cw_ref, wout_ref = refs[n:n + 7]
    o_ref = refs[n + 7]
    conv_out = refs[n + 8:2 * n + 8]
    wbg_b_ref, wcg_b_ref, wv_b_ref, wout_b_ref = refs[2 * n + 8:2 * n + 12]
    xn_scr, cv_scr, y_scr = refs[2 * n + 12:]
    b = pl.program_id(0)

    @pl.when(b == 0)
    def _():
        xn_scr[...] = _rms(x_ref[...], npre_ref[layer:layer + 1, :]).astype(BF16)

    w_bg = wbg_ref[...].astype(BF16)
    w_cg = wcg_ref[...].astype(BF16)
    w_v = wv_ref[...].astype(BF16)
    w_out_rows = wout_ref[...].astype(BF16)
    wbg_b_ref[...] = w_bg
    wcg_b_ref[...] = w_cg
    wv_b_ref[...] = w_v
    wout_b_ref[...] = w_out_rows

    xn = xn_scr[...]
    cv = _dot(xn, w_cg) * _dot(xn, w_v)
    convs = []
    for seg, c_in, c_out in zip(segs, conv_in, conv_out):
        P = (CONV_B - 1) * seg.S
        cv_scr[seg.conv0:seg.conv0 + P, :] = c_in[...]
        convs.append(_sconv_conv(seg, cv[seg.row0:seg.row0 + seg.rows], cw_ref[...], cv_scr))
        c_out[...] = cv_scr[seg.conv0:seg.conv0 + P, :]
    m = (_dot(xn, w_bg) * jnp.concatenate(convs, axis=0)).astype(BF16)
    _accumulate(y_scr, slice(None), _dot(m, w_out_rows), b)

    @pl.when(b == N_COL_BLOCKS - 1)
    def _():
        o_ref[...] = x_ref[...] + _rms(y_scr[...], npost_ref[layer:layer + 1, :])


def _sconv_stream_layer(x, conv_in, layer, j, w, *, seqs_and_steps):
    R, D = x.shape
    C = COL_BLOCK
    segs, _, conv_rows, _ = _segments(seqs_and_steps, CONV_B)
    n_layers = w["norm_mix_pre"].shape[0]
    col = lambda rows: pl.BlockSpec((rows, C), lambda b: (0, b))
    w_in_part = lambda k: pl.BlockSpec((None, D, C), lambda b: (j, 0, k * N_COL_BLOCKS + b))
    outs = pl.pallas_call(
        functools.partial(_sconv_stream_kernel, segs, layer, j),
        grid=(N_COL_BLOCKS,),
        in_specs=[_resident((R, D))] + [col(c.shape[0]) for c in conv_in]
                 + [_resident((n_layers, D)), _resident((n_layers, D)), w_in_part(0), w_in_part(1), w_in_part(2),
                    pl.BlockSpec((None, CONV_B, C), lambda b: (j, 0, b)),
                    pl.BlockSpec((None, C, D), lambda b: (j, b, 0))],
        out_specs=[pl.BlockSpec((R, D), lambda b: (0, 0))] + [col(c.shape[0]) for c in conv_in]
                  + [_chunk_major_out(D)] * 3 + [pl.BlockSpec((C, D), lambda b: (b, 0))],
        out_shape=[jax.ShapeDtypeStruct((R, D), F32)] + [jax.ShapeDtypeStruct(c.shape, F32) for c in conv_in]
                  + [jax.ShapeDtypeStruct((N_COL_BLOCKS, D, C), BF16)] * 3 + [jax.ShapeDtypeStruct((D, D), BF16)],
        scratch_shapes=[pltpu.VMEM((R, D), BF16),
                        pltpu.VMEM((conv_rows, C), F32),
                        pltpu.VMEM((R, D), F32)],
        compiler_params=_params(),
        name="sconv_mixer_small",
    )(x, *conv_in, w["norm_mix_pre"], w["norm_mix_post"], w["sc_w_in"], w["sc_w_in"], w["sc_w_in"],
      w["sc_conv_w"], w["sc_w_out"])
    n = len(segs)
    bf16 = dict(w_bg=outs[n + 1], w_cg=outs[n + 2], w_v=outs[n + 3], w_out=outs[n + 4])
    return outs[0], list(outs[1:1 + n]), bf16


def _sconv_kernel(seg, nblk, layer, x_ref, conv_in_ref, npre_ref, npost_ref, wbg_ref, wcg_ref, wv_ref,
                  cw_ref, wout_ref, o_ref, conv_out_ref, xn_scr, cv_scr, m_scr):
    P = (CONV_B - 1) * seg.S
    i = pl.program_id(0)
    blocks = [slice(b * COL_BLOCK, (b + 1) * COL_BLOCK) for b in range(N_COL_BLOCKS)]

    @pl.when(i == 0)
    def _():
        for b, cols in enumerate(blocks):
            cv_scr[b, 0:P, :] = conv_in_ref[:, cols]

    xn_scr[...] = _rms(x_ref[...], npre_ref[layer:layer + 1, :]).astype(BF16)
    xn = xn_scr[...]
    for b, cols in enumerate(blocks):
        cv = _dot(xn, wcg_ref[b]) * _dot(xn, wv_ref[b])
        conv = _sconv_conv(seg, cv, cw_ref[:, cols], cv_scr.at[b])
        m_scr[:, cols] = (_dot(xn, wbg_ref[b]) * conv).astype(BF16)
    y = _dot(m_scr[...], wout_ref[...])
    o_ref[...] = x_ref[...] + _rms(y, npost_ref[layer:layer + 1, :])

    @pl.when(i == nblk - 1)
    def _():
        for b, cols in enumerate(blocks):
            conv_out_ref[:, cols] = cv_scr[b, 0:P, :]


def _sconv_layer(x, conv_in, layer, j, w, bf16, *, S, TB):
    N, D = x.shape
    (seg,), R, conv_rows, _ = _segments(((S, TB),), CONV_B)
    nblk = N // R
    P = (CONV_B - 1) * S
    n_layers = w["norm_mix_pre"].shape[0]
    row_spec = pl.BlockSpec((R, D), lambda i: (i, 0))
    return pl.pallas_call(
        functools.partial(_sconv_kernel, seg, nblk, layer),
        grid=(nblk,),
        in_specs=[row_spec, _resident((P, D)), _resident((n_layers, D)), _resident((n_layers, D)),
                  _resident(bf16["w_bg"].shape), _resident(bf16["w_cg"].shape), _resident(bf16["w_v"].shape),
                  pl.BlockSpec((None, CONV_B, D), lambda i: (j, 0, 0), pipeline_mode=pl.Buffered(1)),
                  _resident((D, D))],
        out_specs=[row_spec, pl.BlockSpec((P, D), lambda i: (0, 0))],
        out_shape=[jax.ShapeDtypeStruct((N, D), F32), jax.ShapeDtypeStruct((P, D), F32)],
        scratch_shapes=[pltpu.VMEM((R, D), BF16),
                        pltpu.VMEM((N_COL_BLOCKS, conv_rows, COL_BLOCK), F32),
                        pltpu.VMEM((R, D), BF16)],
        compiler_params=_params(),
        name="sconv_mixer",
    )(x, conv_in, w["norm_mix_pre"], w["norm_mix_post"], bf16["w_bg"], bf16["w_cg"], bf16["w_v"],
      w["sc_conv_w"], bf16["w_out"])


def _swiglu(g, u):
    return (g * jax.nn.sigmoid(g) * u).astype(BF16)


def _ffn_stream_kernel(layer, nchunk, x_ref, npre_ref, npost_ref, wg_ref, wu_ref, wd_ref,
                       o_ref, wg_b_ref, wu_b_ref, wd_b_ref, xn_scr, y_scr):
    c = pl.program_id(0)

    @pl.when(c == 0)
    def _():
        xn_scr[...] = _rms(x_ref[...], npre_ref[layer:layer + 1, :]).astype(BF16)

    wg = wg_ref[...].astype(BF16)
    wu = wu_ref[...].astype(BF16)
    wd = wd_ref[...].astype(BF16)
    wg_b_ref[...] = wg
    wu_b_ref[...] = wu
    wd_b_ref[...] = wd
    xn = xn_scr[...]
    part = _dot(_swiglu(_dot(xn, wg), _dot(xn, wu)), wd)
    _accumulate(y_scr, slice(None), part, c)

    @pl.when(c == nchunk - 1)
    def _():
        o_ref[...] = x_ref[...] + _rms(y_scr[...], npost_ref[layer:layer + 1, :])


def _ffn_stream_layer(x, layer, w):
    R, D = x.shape
    n_layers = w["norm_ffn_pre"].shape[0]
    nchunk = D_FF // FF_CHUNK
    cols = lambda: pl.BlockSpec((None, D, FF_CHUNK), lambda c: (layer, 0, c))
    outs = pl.pallas_call(
        functools.partial(_ffn_stream_kernel, layer, nchunk),
        grid=(nchunk,),
        in_specs=[_resident((R, D)), _resident((n_layers, D)), _resident((n_layers, D)), cols(), cols(),
                  pl.BlockSpec((None, FF_CHUNK, D), lambda c: (layer, c, 0))],
        out_specs=[pl.BlockSpec((R, D), lambda c: (0, 0)),
                   _chunk_major_out(D, FF_CHUNK), _chunk_major_out(D, FF_CHUNK),
                   pl.BlockSpec((FF_CHUNK, D), lambda c: (c, 0))],
        out_shape=[jax.ShapeDtypeStruct((R, D), F32), jax.ShapeDtypeStruct((nchunk, D, FF_CHUNK), BF16),
                   jax.ShapeDtypeStruct((nchunk, D, FF_CHUNK), BF16), jax.ShapeDtypeStruct((D_FF, D), BF16)],
        scratch_shapes=[pltpu.VMEM((R, D), BF16), pltpu.VMEM((R, D), F32)],
        compiler_params=_params(),
        name="swiglu_ffn_small",
    )(x, w["norm_ffn_pre"], w["norm_ffn_post"], w["ffn_w_gate"], w["ffn_w_up"], w["ffn_w_down"])
    return outs[0], dict(wg=outs[1], wu=outs[2], wd=outs[3])


def _ffn_kernel(layer, nblk, batch_major_out, x_ref, npre_ref, npost_ref, wg_ref, wu_ref, wd_ref, o_ref,
                xn_scr, act_scr, *out_staging):
    R = x_ref.shape[0]
    i = pl.program_id(0)
    slot = i % 2
    if batch_major_out:
        obuf, sems = out_staging
        S, TB = o_ref.shape[0], R // o_ref.shape[0]

        def copies(which, step):
            t0 = pl.multiple_of(step * TB, TB)
            return [pltpu.make_async_copy(obuf.at[which, :, s, :],
                                          o_ref.at[s, pl.ds(t0, TB), :], sems.at[which, s]) for s in range(S)]

        @pl.when(i >= 2)
        def _():
            for c in copies(slot, i - 2):
                c.wait()

    halves = [slice(0, R // 2), slice(R // 2, R)]
    for rows in halves:
        xn_scr[rows, :] = _rms(x_ref[rows, :], npre_ref[layer:layer + 1, :]).astype(BF16)
    for rows in halves:
        xn = xn_scr[rows, :]
        for c in range(D_FF // FF_CHUNK):
            cols = slice(c * FF_CHUNK, (c + 1) * FF_CHUNK)
            act_scr[rows, cols] = _swiglu(_dot(xn, wg_ref[c]), _dot(xn, wu_ref[c]))
    ys = [_dot(act_scr[rows, :], wd_ref[...]) for rows in halves]
    for rows, y in zip(halves, ys):
        out = x_ref[rows, :] + _rms(y, npost_ref[layer:layer + 1, :])
        if batch_major_out:
            obuf[slot, rows.start // S:rows.stop // S] = out.reshape(-1, S, out.shape[-1])
        else:
            o_ref[rows, :] = out

    if batch_major_out:
        for c in copies(slot, i):
            c.start()

        @pl.when(i == nblk - 1)
        def _():
            if nblk >= 2:
                for c in copies(1 - slot, i - 1):
                    c.wait()
            for c in copies(slot, i):
                c.wait()


def _ffn_layer(x, layer, w, bf16, *, R, batch_major_seqs=None):
    N, D = x.shape
    n_layers = w["norm_ffn_pre"].shape[0]
    nblk = N // R
    row_spec = pl.BlockSpec((R, D), lambda i: (i, 0))
    if batch_major_seqs is None:
        out_spec, out_shape, staging = row_spec, jax.ShapeDtypeStruct((N, D), F32), []
    else:
        S = batch_major_seqs
        out_spec = pl.BlockSpec(memory_space=pl.ANY)
        out_shape = jax.ShapeDtypeStruct((S, N // S, D), F32)
        staging = [pltpu.VMEM((2, R // S, S, D), F32), pltpu.SemaphoreType.DMA((2, S))]
    return pl.pallas_call(
        functools.partial(_ffn_kernel, layer, nblk, batch_major_seqs is not None),
        grid=(nblk,),
        in_specs=[row_spec, _resident((n_layers, D)), _resident((n_layers, D)),
                  _resident(bf16["wg"].shape), _resident(bf16["wu"].shape), _resident((D_FF, D))],
        out_specs=out_spec,
        out_shape=out_shape,
        scratch_shapes=[pltpu.VMEM((R, D), BF16), pltpu.VMEM((R, D_FF), BF16)] + staging,
        compiler_params=_params(),
        name="swiglu_ffn",
    )(x, w["norm_ffn_pre"], w["norm_ffn_post"], bf16["wg"], bf16["wu"], bf16["wd"])


def _to_time_major(a):
    S, K, D = a.shape
    return jnp.swapaxes(a, 0, 1).reshape(K * S, D)


def _from_time_major(a, S):
    KS, D = a.shape
    return jnp.swapaxes(a.reshape(KS // S, S, D), 0, 1)


def kernel(x_prompt, x_sample, state_rglru_conv, state_rglru_h, state_sconv, meta_tokens, norm_mix_pre, norm_mix_post, norm_ffn_pre, norm_ffn_post, rg_w_in, rg_conv_w, rg_conv_b, rg_gate_a_w, rg_gate_a_b, rg_gate_x_w, rg_gate_x_b, rg_lambda, rg_w_out, sc_w_in, sc_conv_w, sc_w_out, ffn_w_gate, ffn_w_up, ffn_w_down):
    D = D_MODEL
    depth = norm_mix_pre.shape[0]
    batch, seq, _ = x_prompt.shape
    dec_batch, dec_seq, _ = x_sample.shape
    w = dict(norm_mix_pre=norm_mix_pre, norm_mix_post=norm_mix_post, norm_ffn_pre=norm_ffn_pre,
             norm_ffn_post=norm_ffn_post, rg_w_in=rg_w_in, rg_conv_w=rg_conv_w, rg_conv_b=rg_conv_b,
             rg_gate_a_w=rg_gate_a_w, rg_gate_a_b=rg_gate_a_b, rg_gate_x_w=rg_gate_x_w, rg_gate_x_b=rg_gate_x_b,
             rg_lambda=rg_lambda, rg_w_out=rg_w_out, sc_w_in=sc_w_in, sc_conv_w=sc_conv_w, sc_w_out=sc_w_out,
             ffn_w_gate=ffn_w_gate, ffn_w_up=ffn_w_up, ffn_w_down=ffn_w_down)

    xm = jnp.broadcast_to(meta_tokens[:, None, :], (N_META, batch, D)).reshape(N_META * batch, D)
    x = jnp.concatenate([xm, _to_time_major(x_sample)], axis=0)
    small = ((batch, N_META), (dec_batch, dec_seq))
    mixer_bf16, ffn_bf16 = [], []
    rg_conv_s, rg_h_s, sc_s = [], [], []
    for i in range(depth):
        j = i // 2
        if i % 2 == 0:
            conv0 = [jnp.zeros(((CONV_A - 1) * batch, D), F32), _to_time_major(state_rglru_conv[j])]
            h0 = [jnp.zeros((batch, D), F32), state_rglru_h[j]]
            x, cb, hT, wb = _rglru_stream_layer(x, conv0, h0, i, j, w, seqs_and_steps=small)
            rg_conv_s.append(cb)
            rg_h_s.append(hT)
        else:
            conv0 = [jnp.zeros(((CONV_B - 1) * batch, D), F32), _to_time_major(state_sconv[j])]
            x, cb, wb = _sconv_stream_layer(x, conv0, i, j, w, seqs_and_steps=small)
            sc_s.append(cb)
        mixer_bf16.append(wb)
        x, wb = _ffn_stream_layer(x, i, w)
        ffn_bf16.append(wb)
    y_sample = _from_time_major(x[N_META * batch:], dec_batch)

    x = x_prompt
    rg_conv_p, rg_h_p, sc_p = [], [], []
    for i in range(depth):
        j = i // 2
        if i % 2 == 0:
            x, cb, hT = _rglru_layer(x, rg_conv_s[j][0], rg_h_s[j][0], i, j, w, mixer_bf16[i], S=batch, TB=PROMPT_TB)
            rg_conv_p.append(cb)
            rg_h_p.append(hT)
        else:
            x, cb = _sconv_layer(x, sc_s[j][0], i, j, w, mixer_bf16[i], S=batch, TB=PROMPT_TB)
            sc_p.append(cb)
        x = _ffn_layer(x, i, w, ffn_bf16[i], R=batch * PROMPT_TB,
                       batch_major_seqs=batch if i == depth - 1 else None)
    y_prompt = x

    return (y_prompt, y_sample,
            jnp.stack([_from_time_major(c, batch) for c in rg_conv_p]), jnp.stack(rg_h_p),
            jnp.stack([_from_time_major(c, batch) for c in sc_p]),
            jnp.stack([_from_time_major(c[1], dec_batch) for c in rg_conv_s]), jnp.stack([h[1] for h in rg_h_s]),
            jnp.stack([_from_time_major(c[1], dec_batch) for c in sc_s]))
```

```python
import functools
from typing import NamedTuple

import jax
import jax.numpy as jnp
from jax import lax
from jax.experimental import pallas as pl
from jax.experimental.pallas import tpu as pltpu

D_MODEL = 1024
D_FF = 2816
N_META = 16
COL_BLOCK = 256
N_COL_BLOCKS = D_MODEL // COL_BLOCK
CONV_A = 4
CONV_B = 3
RG_C = 8.0
EPS = 1e-6

SUBLANES = 8
LANES = 128
FF_CHUNK = 256
PROMPT_TB = 128
VMEM_LIMIT_BYTES = 56 * 1024 * 1024

F32 = jnp.float32
BF16 = jnp.bfloat16


class _Seg(NamedTuple):
    S: int
    TB: int
    row0: int
    conv0: int
    h0: int

    @property
    def rows(self):
        return self.S * self.TB


def _segments(seqs_and_steps, taps):
    segs, row0, conv0, h0 = [], 0, 0, 0
    for S, TB in seqs_and_steps:
        segs.append(_Seg(S, TB, row0, conv0, h0))
        row0 += S * TB
        conv0 += (taps - 1) * S + S * TB
        h0 += S
    return tuple(segs), row0, conv0, h0


def _rms(x, w):
    ms = jnp.mean(x * x, axis=-1, keepdims=True)
    return x * lax.rsqrt(ms + EPS) * w


def _dot(a, b):
    return jnp.dot(a, b, preferred_element_type=F32)


def _scatter_time_major(x_ref, slab_scr, S, TB):
    for s in range(S):
        for j in range(D_MODEL // LANES):
            slab_scr[j, pl.ds(s, TB, stride=S), :] = x_ref[s, :, j * LANES:(j + 1) * LANES]


def _read_slabs(slab_scr):
    return jnp.concatenate([slab_scr[j] for j in range(D_MODEL // LANES)], axis=1)


def _store_batch_major(y, rows, slab_scr, o_ref):
    S = o_ref.shape[0]
    t0, steps = rows.start // S, (rows.stop - rows.start) // S
    for j in range(D_MODEL // LANES):
        slab_scr[j, rows, :] = y[:, j * LANES:(j + 1) * LANES]
    for s in range(S):
        for j in range(D_MODEL // LANES):
            o_ref[s, t0:t0 + steps, j * LANES:(j + 1) * LANES] = slab_scr[j, pl.ds(rows.start + s, steps, stride=S), :]


def _gelu_tanh(x):
    c = 0.7978845608028654
    hx = 0.5 * x
    return hx + hx * jnp.tanh(x * (c + (c * 0.044715) * (x * x)))


def _resident(shape):
    zeros = (0,) * len(shape)
    return pl.BlockSpec(shape, lambda i: zeros, pipeline_mode=pl.Buffered(1))


def _chunk_major_out(rows, width=COL_BLOCK):
    return pl.BlockSpec((None, rows, width), lambda i: (i, 0, 0))


def _slab_scratch(R):
    return pltpu.VMEM((D_MODEL // LANES, R, LANES), F32)


def _params():
    return pltpu.CompilerParams(dimension_semantics=("arbitrary",), vmem_limit_bytes=VMEM_LIMIT_BYTES)


def _accumulate(y_scr, rows, part, step):
    @pl.when(step == 0)
    def _():
        y_scr[rows, :] = part

    @pl.when(step > 0)
    def _():
        y_scr[rows, :] += part


def _rglru_branch_in(seg, xn, w_xr, region):
    P = (CONV_A - 1) * seg.S
    region[seg.conv0 + P:seg.conv0 + P + seg.rows, :] = _dot(xn, w_xr)


def _rglru_conv(seg, cw, cb, region, carry_to):
    S, R = seg.S, seg.rows
    P = (CONV_A - 1) * S
    c0 = seg.conv0
    xc = region[c0:c0 + R, :] * cw[0:1]
    for k in range(1, CONV_A):
        xc = xc + region[c0 + k * S:c0 + k * S + R, :] * cw[k:k + 1]
    xc = xc + cb
    if carry_to is not None:
        region[carry_to:carry_to + P, :] = region[c0 + R:c0 + R + P, :]
    return xc


def _rglru_gate_math(res, xc, half_gab, half_gxb, lam):
    half_c_sp = (-0.5 * RG_C) * jax.nn.softplus(-lam)
    tr = jnp.tanh(res[:, 0:COL_BLOCK] + half_gab)
    log_a = half_c_sp * tr + half_c_sp
    ig = 0.5 * jnp.tanh(res[:, COL_BLOCK:2 * COL_BLOCK] + half_gxb) + 0.5
    a = jnp.exp(log_a)
    m2 = jnp.tanh(log_a) * (-1.0 - a * a)
    u = jnp.where(m2 > 0.0, m2 * lax.rsqrt(m2), 0.0) * (ig * xc)
    return a, u


def _rglru_scan(seg, a, u, g, h_read, h_write, h_row0):
    S, TB = seg.S, seg.TB
    groups = S // SUBLANES
    pieces = [None] * (TB * groups)
    for c in range(groups):
        hrows = slice(h_row0 + c * SUBLANES, h_row0 + (c + 1) * SUBLANES)
        h = h_read[hrows, :]
        for t in range(TB):
            r = t * S + c * SUBLANES
            h = a[r:r + SUBLANES] * h + u[r:r + SUBLANES]
            pieces[t * groups + c] = h * g[r:r + SUBLANES]
        h_write[hrows, :] = h
    return jnp.concatenate(pieces, axis=0)


def _rglru_stream_kernel(segs, layer, j, x_ref, *refs):
    n = len(segs)
    conv_in, h_in = refs[0:n], refs[n:2 * n]
    (npre_ref, npost_ref, wxr_ref, wgate_ref, cw_ref, cb_ref, gaw_ref, gxw_ref, gab_ref, gxb_ref,
     lam_ref, wout_ref) = refs[2 * n:2 * n + 12]
    o_ref = refs[2 * n + 12]
    conv_out, h_out = refs[2 * n + 13:3 * n + 13], refs[3 * n + 13:4 * n + 13]
    wxr_b_ref, wgate_b_ref, gw_b_ref, wout_b_ref = refs[4 * n + 13:4 * n + 17]
    xn_scr, xr_scr, y_scr = refs[4 * n + 17:]
    b = pl.program_id(0)

    @pl.when(b == 0)
    def _():
        xn_scr[...] = _rms(x_ref[...], npre_ref[layer:layer + 1, :]).astype(BF16)

    w_xr = wxr_ref[...].astype(BF16)
    w_gate = wgate_ref[...].astype(BF16)
    gw = (0.5 * jnp.concatenate([gaw_ref[...], gxw_ref[...]], axis=1)).astype(BF16)
    w_out_rows = wout_ref[...].astype(BF16)
    wxr_b_ref[...] = w_xr
    wgate_b_ref[...] = w_gate
    gw_b_ref[...] = gw
    wout_b_ref[...] = w_out_rows

    rows_of = lambda seg: slice(seg.row0, seg.row0 + seg.rows)
    xn = xn_scr[...]
    xr = _dot(xn, w_xr)
    g = _gelu_tanh(_dot(xn, w_gate))
    xcs = []
    for seg, c_in, c_out in zip(segs, conv_in, conv_out):
        P = (CONV_A - 1) * seg.S
        xr_scr[seg.conv0:seg.conv0 + P, :] = c_in[...]
        xr_scr[seg.conv0 + P:seg.conv0 + P + seg.rows, :] = xr[rows_of(seg)]
        xcs.append(_rglru_conv(seg, cw_ref[...], cb_ref[j:j + 1, :], xr_scr, seg.conv0))
        c_out[...] = xr_scr[seg.conv0:seg.conv0 + P, :]
    xc = jnp.concatenate(xcs, axis=0)
    a, u = _rglru_gate_math(_dot(xc.astype(BF16), gw), xc, 0.5 * gab_ref[pl.ds(b, 1), :],
                            0.5 * gxb_ref[pl.ds(b, 1), :], lam_ref[j:j + 1, :])
    hs = [_rglru_scan(seg, a[rows_of(seg)], u[rows_of(seg)], g[rows_of(seg)], hi, ho, 0)
          for seg, hi, ho in zip(segs, h_in, h_out)]
    part = _dot(jnp.concatenate(hs, axis=0).astype(BF16), w_out_rows)
    _accumulate(y_scr, slice(None), part, b)

    @pl.when(b == N_COL_BLOCKS - 1)
    def _():
        o_ref[...] = x_ref[...] + _rms(y_scr[...], npost_ref[layer:layer + 1, :])


def _rglru_stream_layer(x, conv_in, h_in, layer, j, w, *, seqs_and_steps):
    R, D = x.shape
    C = COL_BLOCK
    segs, _, conv_rows, _ = _segments(seqs_and_steps, CONV_A)
    n_layers, n_a = w["norm_mix_pre"].shape[0], w["rg_conv_b"].shape[0]
    col = lambda rows: pl.BlockSpec((rows, C), lambda b: (0, b))
    in_specs = (
        [_resident((R, D))] + [col(c.shape[0]) for c in conv_in] + [col(h.shape[0]) for h in h_in]
        + [_resident((n_layers, D)), _resident((n_layers, D)),
           pl.BlockSpec((None, D, C), lambda b: (j, 0, N_COL_BLOCKS + b)),
           pl.BlockSpec((None, D, C), lambda b: (j, 0, b)),
           pl.BlockSpec((None, CONV_A, C), lambda b: (j, 0, b)),
           col(n_a),
           pl.BlockSpec((None, None, C, C), lambda b: (j, b, 0, 0)),
           pl.BlockSpec((None, None, C, C), lambda b: (j, b, 0, 0)),
           pl.BlockSpec((None, N_COL_BLOCKS, C), lambda b: (j, 0, 0)),
           pl.BlockSpec((None, N_COL_BLOCKS, C), lambda b: (j, 0, 0)),
           col(n_a),
           pl.BlockSpec((None, C, D), lambda b: (j, b, 0))])
    out_specs = ([pl.BlockSpec((R, D), lambda b: (0, 0))] + [col(c.shape[0]) for c in conv_in]
                 + [col(h.shape[0]) for h in h_in]
                 + [_chunk_major_out(D), _chunk_major_out(D), pl.BlockSpec((None, C, 2 * C), lambda b: (b, 0, 0)),
                    pl.BlockSpec((C, D), lambda b: (b, 0))])
    out_shape = ([jax.ShapeDtypeStruct((R, D), F32)] + [jax.ShapeDtypeStruct(s.shape, F32) for s in conv_in + h_in]
                 + [jax.ShapeDtypeStruct((N_COL_BLOCKS, D, C), BF16), jax.ShapeDtypeStruct((N_COL_BLOCKS, D, C), BF16),
                    jax.ShapeDtypeStruct((N_COL_BLOCKS, C, 2 * C), BF16), jax.ShapeDtypeStruct((D, D), BF16)])
    outs = pl.pallas_call(
        functools.partial(_rglru_stream_kernel, segs, layer, j),
        grid=(N_COL_BLOCKS,),
        in_specs=in_specs, out_specs=out_specs, out_shape=out_shape,
        scratch_shapes=[pltpu.VMEM((R, D), BF16),
                        pltpu.VMEM((conv_rows, C), F32),
                        pltpu.VMEM((R, D), F32)],
        compiler_params=_params(),
        name="rglru_mixer_small",
    )(x, *conv_in, *h_in, w["norm_mix_pre"], w["norm_mix_post"], w["rg_w_in"], w["rg_w_in"], w["rg_conv_w"],
      w["rg_conv_b"], w["rg_gate_a_w"], w["rg_gate_x_w"], w["rg_gate_a_b"], w["rg_gate_x_b"], w["rg_lambda"],
      w["rg_w_out"])
    n = len(segs)
    bf16 = dict(w_xr=outs[2 * n + 1], w_gate=outs[2 * n + 2], gw=outs[2 * n + 3], w_out=outs[2 * n + 4])
    return outs[0], list(outs[1:1 + n]), list(outs[1 + n:1 + 2 * n]), bf16


def _rglru_kernel(seg, nblk, layer, j, batch_major_in,
                  x_ref, conv_in_ref, h_in_ref, npre_ref, npost_ref, wxr_ref, wgate_ref, cw_ref, cb_ref,
                  gw_ref, gab_ref, gxb_ref, lam_ref, wout_ref,
                  o_ref, conv_out_ref, h_out_ref, xn_scr, xr_scr, h_scr, *in_staging):
    S, TB, R = seg.S, seg.TB, seg.rows
    P = (CONV_A - 1) * S
    i = pl.program_id(0)
    slot = i % 2
    blocks = [slice(b * COL_BLOCK, (b + 1) * COL_BLOCK) for b in range(N_COL_BLOCKS)]

    if batch_major_in:
        xbuf, sems = in_staging

        def copies(which, step):
            t0 = pl.multiple_of(step * TB, TB)
            return [pltpu.make_async_copy(x_ref.at[s, pl.ds(t0, TB), :], xbuf.at[which, :, s, :],
                                          sems.at[which, s]) for s in range(S)]

    @pl.when(i == 0)
    def _():
        for b, cols in enumerate(blocks):
            xr_scr[b, 0:P, :] = conv_in_ref[:, cols]
            h_scr[b] = h_in_ref[:, cols]
        if batch_major_in:
            for c in copies(0, 0):
                c.start()

    if batch_major_in:
        @pl.when(i + 1 < nblk)
        def _():
            for c in copies(1 - slot, i + 1):
                c.start()

        for c in copies(slot, i):
            c.wait()
        read_x = lambda: xbuf[slot].reshape(R, D_MODEL)
    else:
        read_x = lambda: x_ref[...]

    xn_scr[...] = _rms(read_x(), npre_ref[layer:layer + 1, :]).astype(BF16)
    xn = xn_scr[...]

    gate_pre, xc, res = {}, {}, {}

    def input_matmuls(b):
        _rglru_branch_in(seg, xn, wxr_ref[b], xr_scr.at[b])
        gate_pre[b] = _dot(xn, wgate_ref[b])

    def conv_and_gate_matmul(b):
        xc[b] = _rglru_conv(seg, cw_ref[:, blocks[b]], cb_ref[j:j + 1, blocks[b]], xr_scr.at[b], 0)
        res[b] = _dot(xc[b].astype(BF16), gw_ref[b])

    y = None
    input_matmuls(0)
    conv_and_gate_matmul(0)
    input_matmuls(1)
    for b, cols in enumerate(blocks):
        if b + 1 < N_COL_BLOCKS:
            conv_and_gate_matmul(b + 1)
        if b + 2 < N_COL_BLOCKS:
            input_matmuls(b + 2)
        g = _gelu_tanh(gate_pre.pop(b))
        a, u = _rglru_gate_math(res.pop(b), xc.pop(b), 0.5 * gab_ref[b:b + 1, :], 0.5 * gxb_ref[b:b + 1, :],
                                lam_ref[j:j + 1, cols])
        hs = _rglru_scan(seg, a, u, g, h_scr.at[b], h_scr.at[b], 0)
        part = _dot(hs.astype(BF16), wout_ref[cols, :])
        y = part if y is None else y + part

    o_ref[...] = read_x() + _rms(y, npost_ref[layer:layer + 1, :])

    @pl.when(i == nblk - 1)
    def _():
        for b, cols in enumerate(blocks):
            conv_out_ref[:, cols] = xr_scr[b, 0:P, :]
            h_out_ref[:, cols] = h_scr[b]


def _rglru_layer(x, conv_in, h_in, layer, j, w, bf16, *, S, TB):
    D = D_MODEL
    batch_major_in = x.ndim == 3
    (seg,), R, conv_rows, _ = _segments(((S, TB),), CONV_A)
    nblk = x.size // D // R
    P = (CONV_A - 1) * S
    n_layers, n_a = w["norm_mix_pre"].shape[0], w["rg_conv_b"].shape[0]
    row_spec = pl.BlockSpec((R, D), lambda i: (i, 0))
    if batch_major_in:
        x_spec = pl.BlockSpec(memory_space=pl.ANY)
        staging = [pltpu.VMEM((2, TB, S, D), F32), pltpu.SemaphoreType.DMA((2, S))]
    else:
        x_spec, staging = row_spec, []
    layer_of = lambda arr: pl.BlockSpec((None,) + arr.shape[1:], lambda i: (j,) + (0,) * (arr.ndim - 1),
                                        pipeline_mode=pl.Buffered(1))
    return pl.pallas_call(
        functools.partial(_rglru_kernel, seg, nblk, layer, j, batch_major_in),
        grid=(nblk,),
        in_specs=[x_spec, _resident((P, D)), _resident((S, D)),
                  _resident((n_layers, D)), _resident((n_layers, D)),
                  _resident(bf16["w_xr"].shape), _resident(bf16["w_gate"].shape),
                  layer_of(w["rg_conv_w"]), _resident((n_a, D)), _resident(bf16["gw"].shape),
                  layer_of(w["rg_gate_a_b"]), layer_of(w["rg_gate_x_b"]), _resident((n_a, D)),
                  _resident((D, D))],
        out_specs=[row_spec, pl.BlockSpec((P, D), lambda i: (0, 0)), pl.BlockSpec((S, D), lambda i: (0, 0))],
        out_shape=[jax.ShapeDtypeStruct((nblk * R, D), F32), jax.ShapeDtypeStruct((P, D), F32),
                   jax.ShapeDtypeStruct((S, D), F32)],
        scratch_shapes=[pltpu.VMEM((R, D), BF16),
                        pltpu.VMEM((N_COL_BLOCKS, conv_rows, COL_BLOCK), F32),
                        pltpu.VMEM((N_COL_BLOCKS, S, COL_BLOCK), F32)]
                       + staging,
        compiler_params=_params(),
        name="rglru_mixer",
    )(x, conv_in, h_in, w["norm_mix_pre"], w["norm_mix_post"], bf16["w_xr"], bf16["w_gate"], w["rg_conv_w"],
      w["rg_conv_b"], bf16["gw"], w["rg_gate_a_b"], w["rg_gate_x_b"], w["rg_lambda"], bf16["w_out"])


def _sconv_conv(seg, cv, cw, region):
    S, R = seg.S, seg.rows
    P = (CONV_B - 1) * S
    c0 = seg.conv0
    region[c0 + P:c0 + P + R, :] = cv
    conv = region[c0:c0 + R, :] * cw[0:1]
    for k in range(1, CONV_B):
        conv = conv + region[c0 + k * S:c0 + k * S + R, :] * cw[k:k + 1]
    region[c0:c0 + P, :] = region[c0 + R:c0 + R + P, :]
    return conv


def _sconv_stream_kernel(segs, layer, j, x_ref, *refs):
    n = len(segs)
    conv_in = refs[0:n]
    npre_ref, npost_ref, wbg_ref, wcg_ref, wv_ref, cw_ref, wout_ref = refs[n:n + 7]
    o_ref = refs[n + 7]
    conv_out = refs[n + 8:2 * n + 8]
    wbg_b_ref, wcg_b_ref, wv_b_ref, wout_b_ref = refs[2 * n + 8:2 * n + 12]
    xn_scr, cv_scr, y_scr = refs[2 * n + 12:]
    b = pl.program_id(0)

    @pl.when(b == 0)
    def _():
        xn_scr[...] = _rms(x_ref[...], npre_ref[layer:layer + 1, :]).astype(BF16)

    w_bg = wbg_ref[...].astype(BF16)
    w_cg = wcg_ref[...].astype(BF16)
    w_v = wv_ref[...].astype(BF16)
    w_out_rows = wout_ref[...].astype(BF16)
    wbg_b_ref[...] = w_bg
    wcg_b_ref[...] = w_cg
    wv_b_ref[...] = w_v
    wout_b_ref[...] = w_out_rows

    xn = xn_scr[...]
    cv = _dot(xn, w_cg) * _dot(xn, w_v)
    convs = []
    for seg, c_in, c_out in zip(segs, conv_in, conv_out):
        P = (CONV_B - 1) * seg.S
        cv_scr[seg.conv0:seg.conv0 + P, :] = c_in[...]
        convs.append(_sconv_conv(seg, cv[seg.row0:seg.row0 + seg.rows], cw_ref[...], cv_scr))
        c_out[...] = cv_scr[seg.conv0:seg.conv0 + P, :]
    m = (_dot(xn, w_bg) * jnp.concatenate(convs, axis=0)).astype(BF16)
    _accumulate(y_scr, slice(None), _dot(m, w_out_rows), b)

    @pl.when(b == N_COL_BLOCKS - 1)
    def _():
        o_ref[...] = x_ref[...] + _rms(y_scr[...], npost_ref[layer:layer + 1, :])


def _sconv_stream_layer(x, conv_in, layer, j, w, *, seqs_and_steps):
    R, D = x.shape
    C = COL_BLOCK
    segs, _, conv_rows, _ = _segments(seqs_and_steps, CONV_B)
    n_layers = w["norm_mix_pre"].shape[0]
    col = lambda rows: pl.BlockSpec((rows, C), lambda b: (0, b))
    w_in_part = lambda k: pl.BlockSpec((None, D, C), lambda b: (j, 0, k * N_COL_BLOCKS + b))
    outs = pl.pallas_call(
        functools.partial(_sconv_stream_kernel, segs, layer, j),
        grid=(N_COL_BLOCKS,),
        in_specs=[_resident((R, D))] + [col(c.shape[0]) for c in conv_in]
                 + [_resident((n_layers, D)), _resident((n_layers, D)), w_in_part(0), w_in_part(1), w_in_part(2),
                    pl.BlockSpec((None, CONV_B, C), lambda b: (j, 0, b)),
                    pl.BlockSpec((None, C, D), lambda b: (j, b, 0))],
        out_specs=[pl.BlockSpec((R, D), lambda b: (0, 0))] + [col(c.shape[0]) for c in conv_in]
                  + [_chunk_major_out(D)] * 3 + [pl.BlockSpec((C, D), lambda b: (b, 0))],
        out_shape=[jax.ShapeDtypeStruct((R, D), F32)] + [jax.ShapeDtypeStruct(c.shape, F32) for c in conv_in]
                  + [jax.ShapeDtypeStruct((N_COL_BLOCKS, D, C), BF16)] * 3 + [jax.ShapeDtypeStruct((D, D), BF16)],
        scratch_shapes=[pltpu.VMEM((R, D), BF16),
                        pltpu.VMEM((conv_rows, C), F32),
                        pltpu.VMEM((R, D), F32)],
        compiler_params=_params(),
        name="sconv_mixer_small",
    )(x, *conv_in, w["norm_mix_pre"], w["norm_mix_post"], w["sc_w_in"], w["sc_w_in"], w["sc_w_in"],
      w["sc_conv_w"], w["sc_w_out"])
    n = len(segs)
    bf16 = dict(w_bg=outs[n + 1], w_cg=outs[n + 2], w_v=outs[n + 3], w_out=outs[n + 4])
    return outs[0], list(outs[1:1 + n]), bf16


def _sconv_kernel(seg, nblk, layer, x_ref, conv_in_ref, npre_ref, npost_ref, wbg_ref, wcg_ref, wv_ref,
                  cw_ref, wout_ref, o_ref, conv_out_ref, xn_scr, cv_scr, m_scr):
    P = (CONV_B - 1) * seg.S
    i = pl.program_id(0)
    blocks = [slice(b * COL_BLOCK, (b + 1) * COL_BLOCK) for b in range(N_COL_BLOCKS)]

    @pl.when(i == 0)
    def _():
        for b, cols in enumerate(blocks):
            cv_scr[b, 0:P, :] = conv_in_ref[:, cols]

    xn_scr[...] = _rms(x_ref[...], npre_ref[layer:layer + 1, :]).astype(BF16)
    xn = xn_scr[...]
    for b, cols in enumerate(blocks):
        cv = _dot(xn, wcg_ref[b]) * _dot(xn, wv_ref[b])
        conv = _sconv_conv(seg, cv, cw_ref[:, cols], cv_scr.at[b])
        m_scr[:, cols] = (_dot(xn, wbg_ref[b]) * conv).astype(BF16)
    y = _dot(m_scr[...], wout_ref[...])
    o_ref[...] = x_ref[...] + _rms(y, npost_ref[layer:layer + 1, :])

    @pl.when(i == nblk - 1)
    def _():
        for b, cols in enumerate(blocks):
            conv_out_ref[:, cols] = cv_scr[b, 0:P, :]


def _sconv_layer(x, conv_in, layer, j, w, bf16, *, S, TB):
    N, D = x.shape
    (seg,), R, conv_rows, _ = _segments(((S, TB),), CONV_B)
    nblk = N // R
    P = (CONV_B - 1) * S
    n_layers = w["norm_mix_pre"].shape[0]
    row_spec = pl.BlockSpec((R, D), lambda i: (i, 0))
    return pl.pallas_call(
        functools.partial(_sconv_kernel, seg, nblk, layer),
        grid=(nblk,),
        in_specs=[row_spec, _resident((P, D)), _resident((n_layers, D)), _resident((n_layers, D)),
                  _resident(bf16["w_bg"].shape), _resident(bf16["w_cg"].shape), _resident(bf16["w_v"].shape),
                  pl.BlockSpec((None, CONV_B, D), lambda i: (j, 0, 0), pipeline_mode=pl.Buffered(1)),
                  _resident((D, D))],
        out_specs=[row_spec, pl.BlockSpec((P, D), lambda i: (0, 0))],
        out_shape=[jax.ShapeDtypeStruct((N, D), F32), jax.ShapeDtypeStruct((P, D), F32)],
        scratch_shapes=[pltpu.VMEM((R, D), BF16),
                        pltpu.VMEM((N_COL_BLOCKS, conv_rows, COL_BLOCK), F32),
                        pltpu.VMEM((R, D), BF16)],
        compiler_params=_params(),
        name="sconv_mixer",
    )(x, conv_in, w["norm_mix_pre"], w["norm_mix_post"], bf16["w_bg"], bf16["w_cg"], bf16["w_v"],
      w["sc_conv_w"], bf16["w_out"])


def _swiglu(g, u):
    return (g * jax.nn.sigmoid(g) * u).astype(BF16)


def _ffn_stream_kernel(layer, nchunk, x_ref, npre_ref, npost_ref, wg_ref, wu_ref, wd_ref,
                       o_ref, wg_b_ref, wu_b_ref, wd_b_ref, xn_scr, y_scr):
    c = pl.program_id(0)

    @pl.when(c == 0)
    def _():
        xn_scr[...] = _rms(x_ref[...], npre_ref[layer:layer + 1, :]).astype(BF16)

    wg = wg_ref[...].astype(BF16)
    wu = wu_ref[...].astype(BF16)
    wd = wd_ref[...].astype(BF16)
    wg_b_ref[...] = wg
    wu_b_ref[...] = wu
    wd_b_ref[...] = wd
    xn = xn_scr[...]
    part = _dot(_swiglu(_dot(xn, wg), _dot(xn, wu)), wd)
    _accumulate(y_scr, slice(None), part, c)

    @pl.when(c == nchunk - 1)
    def _():
        o_ref[...] = x_ref[...] + _rms(y_scr[...], npost_ref[layer:layer + 1, :])


def _ffn_stream_layer(x, layer, w):
    R, D = x.shape
    n_layers = w["norm_ffn_pre"].shape[0]
    nchunk = D_FF // FF_CHUNK
    cols = lambda: pl.BlockSpec((None, D, FF_CHUNK), lambda c: (layer, 0, c))
    outs = pl.pallas_call(
        functools.partial(_ffn_stream_kernel, layer, nchunk),
        grid=(nchunk,),
        in_specs=[_resident((R, D)), _resident((n_layers, D)), _resident((n_layers, D)), cols(), cols(),
                  pl.BlockSpec((None, FF_CHUNK, D), lambda c: (layer, c, 0))],
        out_specs=[pl.BlockSpec((R, D), lambda c: (0, 0)),
                   _chunk_major_out(D, FF_CHUNK), _chunk_major_out(D, FF_CHUNK),
                   pl.BlockSpec((FF_CHUNK, D), lambda c: (c, 0))],
        out_shape=[jax.ShapeDtypeStruct((R, D), F32), jax.ShapeDtypeStruct((nchunk, D, FF_CHUNK), BF16),
                   jax.ShapeDtypeStruct((nchunk, D, FF_CHUNK), BF16), jax.ShapeDtypeStruct((D_FF, D), BF16)],
        scratch_shapes=[pltpu.VMEM((R, D), BF16), pltpu.VMEM((R, D), F32)],
        compiler_params=_params(),
        name="swiglu_ffn_small",
    )(x, w["norm_ffn_pre"], w["norm_ffn_post"], w["ffn_w_gate"], w["ffn_w_up"], w["ffn_w_down"])
    return outs[0], dict(wg=outs[1], wu=outs[2], wd=outs[3])


def _ffn_kernel(layer, nblk, batch_major_out, x_ref, npre_ref, npost_ref, wg_ref, wu_ref, wd_ref, o_ref,
                xn_scr, act_scr, *out_staging):
    R = x_ref.shape[0]
    i = pl.program_id(0)
    slot = i % 2
    if batch_major_out:
        obuf, sems = out_staging
        S, TB = o_ref.shape[0], R // o_ref.shape[0]

        def copies(which, step):
            t0 = pl.multiple_of(step * TB, TB)
            return [pltpu.make_async_copy(obuf.at[which, :, s, :],
                                          o_ref.at[s, pl.ds(t0, TB), :], sems.at[which, s]) for s in range(S)]

        @pl.when(i >= 2)
        def _():
            for c in copies(slot, i - 2):
                c.wait()

    halves = [slice(0, R // 2), slice(R // 2, R)]
    for rows in halves:
        xn_scr[rows, :] = _rms(x_ref[rows, :], npre_ref[layer:layer + 1, :]).astype(BF16)
    for rows in halves:
        xn = xn_scr[rows, :]
        for c in range(D_FF // FF_CHUNK):
            cols = slice(c * FF_CHUNK, (c + 1) * FF_CHUNK)
            act_scr[rows, cols] = _swiglu(_dot(xn, wg_ref[c]), _dot(xn, wu_ref[c]))
    ys = [_dot(act_scr[rows, :], wd_ref[...]) for rows in halves]
    for rows, y in zip(halves, ys):
        out = x_ref[rows, :] + _rms(y, npost_ref[layer:layer + 1, :])
        if batch_major_out:
            obuf[slot, rows.start // S:rows.stop // S] = out.reshape(-1, S, out.shape[-1])
        else:
            o_ref[rows, :] = out

    if batch_major_out:
        for c in copies(slot, i):
            c.start()

        @pl.when(i == nblk - 1)
        def _():
            if nblk >= 2:
                for c in copies(1 - slot, i - 1):
                    c.wait()
            for c in copies(slot, i):
                c.wait()


def _ffn_layer(x, layer, w, bf16, *, R, batch_major_seqs=None):
    N, D = x.shape
    n_layers = w["norm_ffn_pre"].shape[0]
    nblk = N // R
    row_spec = pl.BlockSpec((R, D), lambda i: (i, 0))
    if batch_major_seqs is None:
        out_spec, out_shape, staging = row_spec, jax.ShapeDtypeStruct((N, D), F32), []
    else:
        S = batch_major_seqs
        out_spec = pl.BlockSpec(memory_space=pl.ANY)
        out_shape = jax.ShapeDtypeStruct((S, N // S, D), F32)
        staging = [pltpu.VMEM((2, R // S, S, D), F32), pltpu.SemaphoreType.DMA((2, S))]
    return pl.pallas_call(
        functools.partial(_ffn_kernel, layer, nblk, batch_major_seqs is not None),
        grid=(nblk,),
        in_specs=[row_spec, _resident((n_layers, D)), _resident((n_layers, D)),
                  _resident(bf16["wg"].shape), _resident(bf16["wu"].shape), _resident((D_FF, D))],
        out_specs=out_spec,
        out_shape=out_shape,
        scratch_shapes=[pltpu.VMEM((R, D), BF16), pltpu.VMEM((R, D_FF), BF16)] + staging,
        compiler_params=_params(),
        name="swiglu_ffn",
    )(x, w["norm_ffn_pre"], w["norm_ffn_post"], bf16["wg"], bf16["wu"], bf16["wd"])


def _to_time_major(a):
    S, K, D = a.shape
    return jnp.swapaxes(a, 0, 1).reshape(K * S, D)


def _from_time_major(a, S):
    KS, D = a.shape
    return jnp.swapaxes(a.reshape(KS // S, S, D), 0, 1)


def kernel(x_prompt, x_sample, state_rglru_conv, state_rglru_h, state_sconv, meta_tokens, norm_mix_pre, norm_mix_post, norm_ffn_pre, norm_ffn_post, rg_w_in, rg_conv_w, rg_conv_b, rg_gate_a_w, rg_gate_a_b, rg_gate_x_w, rg_gate_x_b, rg_lambda, rg_w_out, sc_w_in, sc_conv_w, sc_w_out, ffn_w_gate, ffn_w_up, ffn_w_down):
    D = D_MODEL
    depth = norm_mix_pre.shape[0]
    batch, seq, _ = x_prompt.shape
    dec_batch, dec_seq, _ = x_sample.shape
    w = dict(norm_mix_pre=norm_mix_pre, norm_mix_post=norm_mix_post, norm_ffn_pre=norm_ffn_pre,
             norm_ffn_post=norm_ffn_post, rg_w_in=rg_w_in, rg_conv_w=rg_conv_w, rg_conv_b=rg_conv_b,
             rg_gate_a_w=rg_gate_a_w, rg_gate_a_b=rg_gate_a_b, rg_gate_x_w=rg_gate_x_w, rg_gate_x_b=rg_gate_x_b,
             rg_lambda=rg_lambda, rg_w_out=rg_w_out, sc_w_in=sc_w_in, sc_conv_w=sc_conv_w, sc_w_out=sc_w_out,
             ffn_w_gate=ffn_w_gate, ffn_w_up=ffn_w_up, ffn_w_down=ffn_w_down)

    xm = jnp.broadcast_to(meta_tokens[:, None, :], (N_META, batch, D)).reshape(N_META * batch, D)
    x = jnp.concatenate([xm, _to_time_major(x_sample)], axis=0)
    small = ((batch, N_META), (dec_batch, dec_seq))
    mixer_bf16, ffn_bf16 = [], []
    rg_conv_s, rg_h_s, sc_s = [], [], []
    for i in range(depth):
        j = i // 2
        if i % 2 == 0:
            conv0 = [jnp.zeros(((CONV_A - 1) * batch, D), F32), _to_time_major(state_rglru_conv[j])]
            h0 = [jnp.zeros((batch, D), F32), state_rglru_h[j]]
            x, cb, hT, wb = _rglru_stream_layer(x, conv0, h0, i, j, w, seqs_and_steps=small)
            rg_conv_s.append(cb)
            rg_h_s.append(hT)
        else:
            conv0 = [jnp.zeros(((CONV_B - 1) * batch, D), F32), _to_time_major(state_sconv[j])]
            x, cb, wb = _sconv_stream_layer(x, conv0, i, j, w, seqs_and_steps=small)
            sc_s.append(cb)
        mixer_bf16.append(wb)
        x, wb = _ffn_stream_layer(x, i, w)
        ffn_bf16.append(wb)
    y_sample = _from_time_major(x[N_META * batch:], dec_batch)

    x = x_prompt
    rg_conv_p, rg_h_p, sc_p = [], [], []
    for i in range(depth):
        j = i // 2
        if i % 2 == 0:
            x, cb, hT = _rglru_layer(x, rg_conv_s[j][0], rg_h_s[j][0], i, j, w, mixer_bf16[i], S=batch, TB=PROMPT_TB)
            rg_conv_p.append(cb)
            rg_h_p.append(hT)
        else:
            x, cb = _sconv_layer(x, sc_s[j][0], i, j, w, mixer_bf16[i], S=batch, TB=PROMPT_TB)
            sc_p.append(cb)
        x = _ffn_layer(x, i, w, ffn_bf16[i], R=batch * PROMPT_TB,
                       batch_major_seqs=batch if i == depth - 1 else None)
    y_prompt = x

    return (y_prompt, y_sample,
            jnp.stack([_from_time_major(c, batch) for c in rg_conv_p]), jnp.stack(rg_h_p),
            jnp.stack([_from_time_major(c, batch) for c in sc_p]),
            jnp.stack([_from_time_major(c[1], dec_batch) for c in rg_conv_s]), jnp.stack([h[1] for h in rg_h_s]),
            jnp.stack([_from_time_major(c[1], dec_batch) for c in sc_s]))
```

```python
import functools
from typing import NamedTuple

import jax
import jax.numpy as jnp
from jax import lax
from jax.experimental import pallas as pl
from jax.experimental.pallas import tpu as pltpu

D_MODEL = 1024
D_FF = 2816
N_META = 16
COL_BLOCK = 256
N_COL_BLOCKS = D_MODEL // COL_BLOCK
CONV_A = 4
CONV_B = 3
RG_C = 8.0
EPS = 1e-6

SUBLANES = 8
FF_CHUNK = 256
PROMPT_TB = 128
VMEM_LIMIT_BYTES = 56 * 1024 * 1024

F32 = jnp.float32
BF16 = jnp.bfloat16


class _Seg(NamedTuple):
    S: int
    TB: int
    row0: int
    conv0: int
    h0: int

    @property
    def rows(self):
        return self.S * self.TB


def _segments(seqs_and_steps, taps):
    segs, row0, conv0, h0 = [], 0, 0, 0
    for S, TB in seqs_and_steps:
        segs.append(_Seg(S, TB, row0, conv0, h0))
        row0 += S * TB
        conv0 += (taps - 1) * S + S * TB
        h0 += S
    return tuple(segs), row0, conv0, h0


def _rms(x, w):
    ms = jnp.mean(x * x, axis=-1, keepdims=True)
    return x * lax.rsqrt(ms + EPS) * w


def _dot(a, b):
    return jnp.dot(a, b, preferred_element_type=F32)


def _gelu_tanh(x):
    c = 0.7978845608028654
    hx = 0.5 * x
    return hx + hx * jnp.tanh(x * (c + (c * 0.044715) * (x * x)))


def _resident(shape):
    zeros = (0,) * len(shape)
    return pl.BlockSpec(shape, lambda i: zeros, pipeline_mode=pl.Buffered(1))


def _chunk_major_out(rows, width=COL_BLOCK):
    return pl.BlockSpec((None, rows, width), lambda i: (i, 0, 0))


def _params():
    return pltpu.CompilerParams(dimension_semantics=("arbitrary",), vmem_limit_bytes=VMEM_LIMIT_BYTES)


def _accumulate(y_scr, rows, part, step):
    @pl.when(step == 0)
    def _():
        y_scr[rows, :] = part

    @pl.when(step > 0)
    def _():
        y_scr[rows, :] += part


def _rglru_branch_in(seg, xn, w_xr, region):
    P = (CONV_A - 1) * seg.S
    region[seg.conv0 + P:seg.conv0 + P + seg.rows, :] = _dot(xn, w_xr)


def _rglru_conv(seg, cw, cb, region, carry_to):
    S, R = seg.S, seg.rows
    P = (CONV_A - 1) * S
    c0 = seg.conv0
    xc = region[c0:c0 + R, :] * cw[0:1]
    for k in range(1, CONV_A):
        xc = xc + region[c0 + k * S:c0 + k * S + R, :] * cw[k:k + 1]
    xc = xc + cb
    if carry_to is not None:
        region[carry_to:carry_to + P, :] = region[c0 + R:c0 + R + P, :]
    return xc


def _rglru_gate_math(res, xc, half_gab, half_gxb, lam):
    half_c_sp = (-0.5 * RG_C) * jax.nn.softplus(-lam)
    tr = jnp.tanh(res[:, 0:COL_BLOCK] + half_gab)
    log_a = half_c_sp * tr + half_c_sp
    ig = 0.5 * jnp.tanh(res[:, COL_BLOCK:2 * COL_BLOCK] + half_gxb) + 0.5
    a = jnp.exp(log_a)
    m2 = jnp.tanh(log_a) * (-1.0 - a * a)
    u = jnp.where(m2 > 0.0, m2 * lax.rsqrt(m2), 0.0) * (ig * xc)
    return a, u


def _rglru_scan(seg, a, u, g, h_read, h_write, h_row0):
    S, TB = seg.S, seg.TB
    groups = S // SUBLANES
    pieces = [None] * (TB * groups)
    for c in range(groups):
        hrows = slice(h_row0 + c * SUBLANES, h_row0 + (c + 1) * SUBLANES)
        h = h_read[hrows, :]
        for t in range(TB):
            r = t * S + c * SUBLANES
            h = a[r:r + SUBLANES] * h + u[r:r + SUBLANES]
            pieces[t * groups + c] = h * g[r:r + SUBLANES]
        h_write[hrows, :] = h
    return jnp.concatenate(pieces, axis=0)


def _rglru_stream_kernel(segs, sample_in, layer, j, x_ref, *refs):
    if sample_in is not None:
        xs_hbm, refs = refs[0], refs[1:]
    n = len(segs)
    conv_in, h_in = refs[0:n], refs[n:2 * n]
    (npre_ref, npost_ref, wxr_ref, wgate_ref, cw_ref, cb_ref, gaw_ref, gxw_ref, gab_ref, gxb_ref,
     lam_ref, wout_ref) = refs[2 * n:2 * n + 12]
    o_ref = refs[2 * n + 12]
    conv_out, h_out = refs[2 * n + 13:3 * n + 13], refs[3 * n + 13:4 * n + 13]
    wxr_b_ref, wgate_b_ref, gw_b_ref, wout_b_ref = refs[4 * n + 13:4 * n + 17]
    xn_scr, xr_scr, y_scr, *in_staging = refs[4 * n + 17:]
    b = pl.program_id(0)

    if sample_in is None:
        read_x = lambda: x_ref[...]
    else:
        S, T = sample_in
        xs_scr, sems = in_staging
        read_x = lambda: jnp.concatenate([x_ref[...], xs_scr[...].reshape(T * S, D_MODEL)], axis=0)

    @pl.when(b == 0)
    def _():
        if sample_in is not None:
            copies = [pltpu.make_async_copy(xs_hbm.at[:, t, :], xs_scr.at[t], sems.at[t]) for t in range(T)]
            for cp in copies:
                cp.start()
            for cp in copies:
                cp.wait()
        xn_scr[...] = _rms(read_x(), npre_ref[layer:layer + 1, :]).astype(BF16)

    w_xr = wxr_ref[...].astype(BF16)
    w_gate = wgate_ref[...].astype(BF16)
    gw = (0.5 * jnp.concatenate([gaw_ref[...], gxw_ref[...]], axis=1)).astype(BF16)
    w_out_rows = wout_ref[...].astype(BF16)
    wxr_b_ref[...] = w_xr
    wgate_b_ref[...] = w_gate
    gw_b_ref[...] = gw
    wout_b_ref[...] = w_out_rows

    rows_of = lambda seg: slice(seg.row0, seg.row0 + seg.rows)
    xn = xn_scr[...]
    xr = _dot(xn, w_xr)
    g = _gelu_tanh(_dot(xn, w_gate))
    xcs = []
    for seg, c_in, c_out in zip(segs, conv_in, conv_out):
        P = (CONV_A - 1) * seg.S
        xr_scr[seg.conv0:seg.conv0 + P, :] = c_in[...]
        xr_scr[seg.conv0 + P:seg.conv0 + P + seg.rows, :] = xr[rows_of(seg)]
        xcs.append(_rglru_conv(seg, cw_ref[...], cb_ref[j:j + 1, :], xr_scr, seg.conv0))
        c_out[...] = xr_scr[seg.conv0:seg.conv0 + P, :]
    xc = jnp.concatenate(xcs, axis=0)
    a, u = _rglru_gate_math(_dot(xc.astype(BF16), gw), xc, 0.5 * gab_ref[pl.ds(b, 1), :],
                            0.5 * gxb_ref[pl.ds(b, 1), :], lam_ref[j:j + 1, :])
    hs = [_rglru_scan(seg, a[rows_of(seg)], u[rows_of(seg)], g[rows_of(seg)], hi, ho, 0)
          for seg, hi, ho in zip(segs, h_in, h_out)]
    part = _dot(jnp.concatenate(hs, axis=0).astype(BF16), w_out_rows)
    _accumulate(y_scr, slice(None), part, b)

    @pl.when(b == N_COL_BLOCKS - 1)
    def _():
        o_ref[...] = read_x() + _rms(y_scr[...], npost_ref[layer:layer + 1, :])


def _rglru_stream_layer(x, conv_in, h_in, layer, j, w, *, seqs_and_steps, x_sample=None):
    D, C = D_MODEL, COL_BLOCK
    segs, R, conv_rows, _ = _segments(seqs_and_steps, CONV_A)
    n_layers, n_a = w["norm_mix_pre"].shape[0], w["rg_conv_b"].shape[0]
    col = lambda rows: pl.BlockSpec((rows, C), lambda b: (0, b))
    if x_sample is None:
        x_args, x_specs, sample_in, staging = [x], [_resident(x.shape)], None, []
    else:
        S, T, _ = x_sample.shape
        x_args, x_specs, sample_in = [x, x_sample], [_resident(x.shape), pl.BlockSpec(memory_space=pl.ANY)], (S, T)
        staging = [pltpu.VMEM((T, S, D), F32), pltpu.SemaphoreType.DMA((T,))]
    in_specs = (
        x_specs + [col(c.shape[0]) for c in conv_in] + [col(h.shape[0]) for h in h_in]
        + [_resident((n_layers, D)), _resident((n_layers, D)),
           pl.BlockSpec((None, D, C), lambda b: (j, 0, N_COL_BLOCKS + b)),
           pl.BlockSpec((None, D, C), lambda b: (j, 0, b)),
           pl.BlockSpec((None, CONV_A, C), lambda b: (j, 0, b)),
           col(n_a),
           pl.BlockSpec((None, None, C, C), lambda b: (j, b, 0, 0)),
           pl.BlockSpec((None, None, C, C), lambda b: (j, b, 0, 0)),
           pl.BlockSpec((None, N_COL_BLOCKS, C), lambda b: (j, 0, 0)),
           pl.BlockSpec((None, N_COL_BLOCKS, C), lambda b: (j, 0, 0)),
           col(n_a),
           pl.BlockSpec((None, C, D), lambda b: (j, b, 0))])
    out_specs = ([pl.BlockSpec((R, D), lambda b: (0, 0))] + [col(c.shape[0]) for c in conv_in]
                 + [col(h.shape[0]) for h in h_in]
                 + [_chunk_major_out(D), _chunk_major_out(D), pl.BlockSpec((None, C, 2 * C), lambda b: (b, 0, 0)),
                    pl.BlockSpec((C, D), lambda b: (b, 0))])
    out_shape = ([jax.ShapeDtypeStruct((R, D), F32)] + [jax.ShapeDtypeStruct(s.shape, F32) for s in conv_in + h_in]
                 + [jax.ShapeDtypeStruct((N_COL_BLOCKS, D, C), BF16), jax.ShapeDtypeStruct((N_COL_BLOCKS, D, C), BF16),
                    jax.ShapeDtypeStruct((N_COL_BLOCKS, C, 2 * C), BF16), jax.ShapeDtypeStruct((D, D), BF16)])
    outs = pl.pallas_call(
        functools.partial(_rglru_stream_kernel, segs, sample_in, layer, j),
        grid=(N_COL_BLOCKS,),
        in_specs=in_specs, out_specs=out_specs, out_shape=out_shape,
        scratch_shapes=[pltpu.VMEM((R, D), BF16),
                        pltpu.VMEM((conv_rows, C), F32),
                        pltpu.VMEM((R, D), F32)]
                       + staging,
        compiler_params=_params(),
        name="rglru_mixer_small",
    )(*x_args, *conv_in, *h_in, w["norm_mix_pre"], w["norm_mix_post"], w["rg_w_in"], w["rg_w_in"], w["rg_conv_w"],
      w["rg_conv_b"], w["rg_gate_a_w"], w["rg_gate_x_w"], w["rg_gate_a_b"], w["rg_gate_x_b"], w["rg_lambda"],
      w["rg_w_out"])
    n = len(segs)
    bf16 = dict(w_xr=outs[2 * n + 1], w_gate=outs[2 * n + 2], gw=outs[2 * n + 3], w_out=outs[2 * n + 4])
    return outs[0], list(outs[1:1 + n]), list(outs[1 + n:1 + 2 * n]), bf16


def _rglru_kernel(seg, nblk, layer, j, batch_major_in,
                  x_ref, conv_in_ref, h_in_ref, npre_ref, npost_ref, wxr_ref, wgate_ref, cw_ref, cb_ref,
                  gw_ref, gab_ref, gxb_ref, lam_ref, wout_ref,
                  o_ref, conv_out_ref, h_out_ref, xn_scr, xr_scr, h_scr, *in_staging):
    S, TB, R = seg.S, seg.TB, seg.rows
    P = (CONV_A - 1) * S
    i = pl.program_id(0)
    slot = i % 2
    blocks = [slice(b * COL_BLOCK, (b + 1) * COL_BLOCK) for b in range(N_COL_BLOCKS)]

    if batch_major_in:
        xbuf, sems = in_staging

        def copies(which, step):
            t0 = pl.multiple_of(step * TB, TB)
            return [pltpu.make_async_copy(x_ref.at[s, pl.ds(t0, TB), :], xbuf.at[which, :, s, :],
                                          sems.at[which, s]) for s in range(S)]

    @pl.when(i == 0)
    def _():
        for b, cols in enumerate(blocks):
            xr_scr[b, 0:P, :] = conv_in_ref[:, cols]
            h_scr[b] = h_in_ref[:, cols]
        if batch_major_in:
            for c in copies(0, 0):
                c.start()

    if batch_major_in:
        @pl.when(i + 1 < nblk)
        def _():
            for c in copies(1 - slot, i + 1):
                c.start()

        for c in copies(slot, i):
            c.wait()
        read_x = lambda: xbuf[slot].reshape(R, D_MODEL)
    else:
        read_x = lambda: x_ref[...]

    xn_scr[...] = _rms(read_x(), npre_ref[layer:layer + 1, :]).astype(BF16)
    xn = xn_scr[...]

    gate_pre, xc, res = {}, {}, {}

    def input_matmuls(b):
        _rglru_branch_in(seg, xn, wxr_ref[b], xr_scr.at[b])
        gate_pre[b] = _dot(xn, wgate_ref[b])

    def conv_and_gate_matmul(b):
        xc[b] = _rglru_conv(seg, cw_ref[:, blocks[b]], cb_ref[j:j + 1, blocks[b]], xr_scr.at[b], 0)
        res[b] = _dot(xc[b].astype(BF16), gw_ref[b])

    y = None
    input_matmuls(0)
    conv_and_gate_matmul(0)
    input_matmuls(1)
    for b, cols in enumerate(blocks):
        if b + 1 < N_COL_BLOCKS:
            conv_and_gate_matmul(b + 1)
        if b + 2 < N_COL_BLOCKS:
            input_matmuls(b + 2)
        g = _gelu_tanh(gate_pre.pop(b))
        a, u = _rglru_gate_math(res.pop(b), xc.pop(b), 0.5 * gab_ref[b:b + 1, :], 0.5 * gxb_ref[b:b + 1, :],
                                lam_ref[j:j + 1, cols])
        hs = _rglru_scan(seg, a, u, g, h_scr.at[b], h_scr.at[b], 0)
        part = _dot(hs.astype(BF16), wout_ref[cols, :])
        y = part if y is None else y + part

    o_ref[...] = read_x() + _rms(y, npost_ref[layer:layer + 1, :])

    @pl.when(i == nblk - 1)
    def _():
        for b, cols in enumerate(blocks):
            conv_out_ref[:, cols] = xr_scr[b, 0:P, :]
            h_out_ref[:, cols] = h_scr[b]


def _rglru_layer(x, conv_in, h_in, layer, j, w, bf16, *, S, TB):
    D = D_MODEL
    batch_major_in = x.ndim == 3
    (seg,), R, conv_rows, _ = _segments(((S, TB),), CONV_A)
    nblk = x.size // D // R
    P = (CONV_A - 1) * S
    n_layers, n_a = w["norm_mix_pre"].shape[0], w["rg_conv_b"].shape[0]
    row_spec = pl.BlockSpec((R, D), lambda i: (i, 0))
    if batch_major_in:
        x_spec = pl.BlockSpec(memory_space=pl.ANY)
        staging = [pltpu.VMEM((2, TB, S, D), F32), pltpu.SemaphoreType.DMA((2, S))]
    else:
        x_spec, staging = row_spec, []
    layer_of = lambda arr: pl.BlockSpec((None,) + arr.shape[1:], lambda i: (j,) + (0,) * (arr.ndim - 1),
                                        pipeline_mode=pl.Buffered(1))
    return pl.pallas_call(
        functools.partial(_rglru_kernel, seg, nblk, layer, j, batch_major_in),
        grid=(nblk,),
        in_specs=[x_spec, _resident((P, D)), _resident((S, D)),
                  _resident((n_layers, D)), _resident((n_layers, D)),
                  _resident(bf16["w_xr"].shape), _resident(bf16["w_gate"].shape),
                  layer_of(w["rg_conv_w"]), _resident((n_a, D)), _resident(bf16["gw"].shape),
                  layer_of(w["rg_gate_a_b"]), layer_of(w["rg_gate_x_b"]), _resident((n_a, D)),
                  _resident((D, D))],
        out_specs=[row_spec, pl.BlockSpec((P, D), lambda i: (0, 0)), pl.BlockSpec((S, D), lambda i: (0, 0))],
        out_shape=[jax.ShapeDtypeStruct((nblk * R, D), F32), jax.ShapeDtypeStruct((P, D), F32),
                   jax.ShapeDtypeStruct((S, D), F32)],
        scratch_shapes=[pltpu.VMEM((R, D), BF16),
                        pltpu.VMEM((N_COL_BLOCKS, conv_rows, COL_BLOCK), F32),
                        pltpu.VMEM((N_COL_BLOCKS, S, COL_BLOCK), F32)]
                       + staging,
        compiler_params=_params(),
        name="rglru_mixer",
    )(x, conv_in, h_in, w["norm_mix_pre"], w["norm_mix_post"], bf16["w_xr"], bf16["w_gate"], w["rg_conv_w"],
      w["rg_conv_b"], bf16["gw"], w["rg_gate_a_b"], w["rg_gate_x_b"], w["rg_lambda"], bf16["w_out"])


def _sconv_conv(seg, cv, cw, region):
    S, R = seg.S, seg.rows
    P = (CONV_B - 1) * S
    c0 = seg.conv0
    region[c0 + P:c0 + P + R, :] = cv
    conv = region[c0:c0 + R, :] * cw[0:1]
    for k in range(1, CONV_B):
        conv = conv + region[c0 + k * S:c0 + k * S + R, :] * cw[k:k + 1]
    region[c0:c0 + P, :] = region[c0 + R:c0 + R + P, :]
    return conv


def _sconv_stream_kernel(segs, layer, j, x_ref, *refs):
    n = len(segs)
    conv_in = refs[0:n]
    npre_ref, npost_ref, wbg_ref, wcg_ref, wv_ref, cw_ref, wout_ref = refs[n:n + 7]
    o_ref = refs[n + 7]
    conv_out = refs[n + 8:2 * n + 8]
    wbg_b_ref, wcg_b_ref, wv_b_ref, wout_b_ref = refs[2 * n + 8:2 * n + 12]
    xn_scr, cv_scr, y_scr = refs[2 * n + 12:]
    b = pl.program_id(0)

    @pl.when(b == 0)
    def _():
        xn_scr[...] = _rms(x_ref[...], npre_ref[layer:layer + 1, :]).astype(BF16)

    w_bg = wbg_ref[...].astype(BF16)
    w_cg = wcg_ref[...].astype(BF16)
    w_v = wv_ref[...].astype(BF16)
    w_out_rows = wout_ref[...].astype(BF16)
    wbg_b_ref[...] = w_bg
    wcg_b_ref[...] = w_cg
    wv_b_ref[...] = w_v
    wout_b_ref[...] = w_out_rows

    xn = xn_scr[...]
    cv = _dot(xn, w_cg) * _dot(xn, w_v)
    convs = []
    for seg, c_in, c_out in zip(segs, conv_in, conv_out):
        P = (CONV_B - 1) * seg.S
        cv_scr[seg.conv0:seg.conv0 + P, :] = c_in[...]
        convs.append(_sconv_conv(seg, cv[seg.row0:seg.row0 + seg.rows], cw_ref[...], cv_scr))
        c_out[...] = cv_scr[seg.conv0:seg.conv0 + P, :]
    m = (_dot(xn, w_bg) * jnp.concatenate(convs, axis=0)).astype(BF16)
    _accumulate(y_scr, slice(None), _dot(m, w_out_rows), b)

    @pl.when(b == N_COL_BLOCKS - 1)
    def _():
        o_ref[...] = x_ref[...] + _rms(y_scr[...], npost_ref[layer:layer + 1, :])


def _sconv_stream_layer(x, conv_in, layer, j, w, *, seqs_and_steps):
    R, D = x.shape
    C = COL_BLOCK
    segs, _, conv_rows, _ = _segments(seqs_and_steps, CONV_B)
    n_layers = w["norm_mix_pre"].shape[0]
    col = lambda rows: pl.BlockSpec((rows, C), lambda b: (0, b))
    w_in_part = lambda k: pl.BlockSpec((None, D, C), lambda b: (j, 0, k * N_COL_BLOCKS + b))
    outs = pl.pallas_call(
        functools.partial(_sconv_stream_kernel, segs, layer, j),
        grid=(N_COL_BLOCKS,),
        in_specs=[_resident((R, D))] + [col(c.shape[0]) for c in conv_in]
                 + [_resident((n_layers, D)), _resident((n_layers, D)), w_in_part(0), w_in_part(1), w_in_part(2),
                    pl.BlockSpec((None, CONV_B, C), lambda b: (j, 0, b)),
                    pl.BlockSpec((None, C, D), lambda b: (j, b, 0))],
        out_specs=[pl.BlockSpec((R, D), lambda b: (0, 0))] + [col(c.shape[0]) for c in conv_in]
                  + [_chunk_major_out(D)] * 3 + [pl.BlockSpec((C, D), lambda b: (b, 0))],
        out_shape=[jax.ShapeDtypeStruct((R, D), F32)] + [jax.ShapeDtypeStruct(c.shape, F32) for c in conv_in]
                  + [jax.ShapeDtypeStruct((N_COL_BLOCKS, D, C), BF16)] * 3 + [jax.ShapeDtypeStruct((D, D), BF16)],
        scratch_shapes=[pltpu.VMEM((R, D), BF16),
                        pltpu.VMEM((conv_rows, C), F32),
                        pltpu.VMEM((R, D), F32)],
        compiler_params=_params(),
        name="sconv_mixer_small",
    )(x, *conv_in, w["norm_mix_pre"], w["norm_mix_post"], w["sc_w_in"], w["sc_w_in"], w["sc_w_in"],
      w["sc_conv_w"], w["sc_w_out"])
    n = len(segs)
    bf16 = dict(w_bg=outs[n + 1], w_cg=outs[n + 2], w_v=outs[n + 3], w_out=outs[n + 4])
    return outs[0], list(outs[1:1 + n]), bf16


def _sconv_kernel(seg, nblk, layer, x_ref, conv_in_ref, npre_ref, npost_ref, wbg_ref, wcg_ref, wv_ref,
                  cw_ref, wout_ref, o_ref, conv_out_ref, xn_scr, cv_scr, m_scr):
    P = (CONV_B - 1) * seg.S
    i = pl.program_id(0)
    blocks = [slice(b * COL_BLOCK, (b + 1) * COL_BLOCK) for b in range(N_COL_BLOCKS)]

    @pl.when(i == 0)
    def _():
        for b, cols in enumerate(blocks):
            cv_scr[b, 0:P, :] = conv_in_ref[:, cols]

    xn_scr[...] = _rms(x_ref[...], npre_ref[layer:layer + 1, :]).astype(BF16)
    xn = xn_scr[...]
    for b, cols in enumerate(blocks):
        cv = _dot(xn, wcg_ref[b]) * _dot(xn, wv_ref[b])
        conv = _sconv_conv(seg, cv, cw_ref[:, cols], cv_scr.at[b])
        m_scr[:, cols] = (_dot(xn, wbg_ref[b]) * conv).astype(BF16)
    y = _dot(m_scr[...], wout_ref[...])
    o_ref[...] = x_ref[...] + _rms(y, npost_ref[layer:layer + 1, :])

    @pl.when(i == nblk - 1)
    def _():
        for b, cols in enumerate(blocks):
            conv_out_ref[:, cols] = cv_scr[b, 0:P, :]


def _sconv_layer(x, conv_in, layer, j, w, bf16, *, S, TB):
    N, D = x.shape
    (seg,), R, conv_rows, _ = _segments(((S, TB),), CONV_B)
    nblk = N // R
    P = (CONV_B - 1) * S
    n_layers = w["norm_mix_pre"].shape[0]
    row_spec = pl.BlockSpec((R, D), lambda i: (i, 0))
    return pl.pallas_call(
        functools.partial(_sconv_kernel, seg, nblk, layer),
        grid=(nblk,),
        in_specs=[row_spec, _resident((P, D)), _resident((n_layers, D)), _resident((n_layers, D)),
                  _resident(bf16["w_bg"].shape), _resident(bf16["w_cg"].shape), _resident(bf16["w_v"].shape),
                  pl.BlockSpec((None, CONV_B, D), lambda i: (j, 0, 0), pipeline_mode=pl.Buffered(1)),
                  _resident((D, D))],
        out_specs=[row_spec, pl.BlockSpec((P, D), lambda i: (0, 0))],
        out_shape=[jax.ShapeDtypeStruct((N, D), F32), jax.ShapeDtypeStruct((P, D), F32)],
        scratch_shapes=[pltpu.VMEM((R, D), BF16),
                        pltpu.VMEM((N_COL_BLOCKS, conv_rows, COL_BLOCK), F32),
                        pltpu.VMEM((R, D), BF16)],
        compiler_params=_params(),
        name="sconv_mixer",
    )(x, conv_in, w["norm_mix_pre"], w["norm_mix_post"], bf16["w_bg"], bf16["w_cg"], bf16["w_v"],
      w["sc_conv_w"], bf16["w_out"])


def _swiglu(g, u):
    return (g * jax.nn.sigmoid(g) * u).astype(BF16)


def _ffn_stream_kernel(layer, nchunk, sample_out, x_ref, npre_ref, npost_ref, wg_ref, wu_ref, wd_ref,
                       o_ref, wg_b_ref, wu_b_ref, wd_b_ref, xn_scr, y_scr, *out_staging):
    c = pl.program_id(0)

    @pl.when(c == 0)
    def _():
        xn_scr[...] = _rms(x_ref[...], npre_ref[layer:layer + 1, :]).astype(BF16)

    wg = wg_ref[...].astype(BF16)
    wu = wu_ref[...].astype(BF16)
    wd = wd_ref[...].astype(BF16)
    wg_b_ref[...] = wg
    wu_b_ref[...] = wu
    wd_b_ref[...] = wd
    xn = xn_scr[...]
    part = _dot(_swiglu(_dot(xn, wg), _dot(xn, wu)), wd)
    _accumulate(y_scr, slice(None), part, c)

    @pl.when(c == nchunk - 1)
    def _():
        out = x_ref[...] + _rms(y_scr[...], npost_ref[layer:layer + 1, :])
        if sample_out is None:
            o_ref[...] = out
        else:
            row0, S, T = sample_out
            obuf, sems = out_staging
            obuf[...] = out[row0:row0 + T * S].reshape(T, S, out.shape[-1])
            copies = [pltpu.make_async_copy(obuf.at[t], o_ref.at[:, t, :], sems.at[t]) for t in range(T)]
            for cp in copies:
                cp.start()
            for cp in copies:
                cp.wait()


def _ffn_stream_layer(x, layer, w, *, sample_out=None):
    R, D = x.shape
    n_layers = w["norm_ffn_pre"].shape[0]
    nchunk = D_FF // FF_CHUNK
    cols = lambda: pl.BlockSpec((None, D, FF_CHUNK), lambda c: (layer, 0, c))
    if sample_out is None:
        o_spec, o_shape, staging = pl.BlockSpec((R, D), lambda c: (0, 0)), jax.ShapeDtypeStruct((R, D), F32), []
    else:
        _, S, T = sample_out
        o_spec, o_shape = pl.BlockSpec(memory_space=pl.ANY), jax.ShapeDtypeStruct((S, T, D), F32)
        staging = [pltpu.VMEM((T, S, D), F32), pltpu.SemaphoreType.DMA((T,))]
    outs = pl.pallas_call(
        functools.partial(_ffn_stream_kernel, layer, nchunk, sample_out),
        grid=(nchunk,),
        in_specs=[_resident((R, D)), _resident((n_layers, D)), _resident((n_layers, D)), cols(), cols(),
                  pl.BlockSpec((None, FF_CHUNK, D), lambda c: (layer, c, 0))],
        out_specs=[o_spec,
                   _chunk_major_out(D, FF_CHUNK), _chunk_major_out(D, FF_CHUNK),
                   pl.BlockSpec((FF_CHUNK, D), lambda c: (c, 0))],
        out_shape=[o_shape, jax.ShapeDtypeStruct((nchunk, D, FF_CHUNK), BF16),
                   jax.ShapeDtypeStruct((nchunk, D, FF_CHUNK), BF16), jax.ShapeDtypeStruct((D_FF, D), BF16)],
        scratch_shapes=[pltpu.VMEM((R, D), BF16), pltpu.VMEM((R, D), F32)] + staging,
        compiler_params=_params(),
        name="swiglu_ffn_small",
    )(x, w["norm_ffn_pre"], w["norm_ffn_post"], w["ffn_w_gate"], w["ffn_w_up"], w["ffn_w_down"])
    return outs[0], dict(wg=outs[1], wu=outs[2], wd=outs[3])


def _ffn_kernel(layer, nblk, batch_major_out, x_ref, npre_ref, npost_ref, wg_ref, wu_ref, wd_ref, o_ref,
                xn_scr, act_scr, *out_staging):
    R = x_ref.shape[0]
    i = pl.program_id(0)
    slot = i % 2
    if batch_major_out:
        obuf, sems = out_staging
        S, TB = o_ref.shape[0], R // o_ref.shape[0]

        def copies(which, step):
            t0 = pl.multiple_of(step * TB, TB)
            return [pltpu.make_async_copy(obuf.at[which, :, s, :],
                                          o_ref.at[s, pl.ds(t0, TB), :], sems.at[which, s]) for s in range(S)]

        @pl.when(i >= 2)
        def _():
            for c in copies(slot, i - 2):
                c.wait()

    halves = [slice(0, R // 2), slice(R // 2, R)]
    for rows in halves:
        xn_scr[rows, :] = _rms(x_ref[rows, :], npre_ref[layer:layer + 1, :]).astype(BF16)
    for rows in halves:
        xn = xn_scr[rows, :]
        for c in range(D_FF // FF_CHUNK):
            cols = slice(c * FF_CHUNK, (c + 1) * FF_CHUNK)
            act_scr[rows, cols] = _swiglu(_dot(xn, wg_ref[c]), _dot(xn, wu_ref[c]))
    ys = [_dot(act_scr[rows, :], wd_ref[...]) for rows in halves]
    for rows, y in zip(halves, ys):
        out = x_ref[rows, :] + _rms(y, npost_ref[layer:layer + 1, :])
        if batch_major_out:
            obuf[slot, rows.start // S:rows.stop // S] = out.reshape(-1, S, out.shape[-1])
        else:
            o_ref[rows, :] = out

    if batch_major_out:
        for c in copies(slot, i):
            c.start()

        @pl.when(i == nblk - 1)
        def _():
            if nblk >= 2:
                for c in copies(1 - slot, i - 1):
                    c.wait()
            for c in copies(slot, i):
                c.wait()


def _ffn_layer(x, layer, w, bf16, *, R, batch_major_seqs=None):
    N, D = x.shape
    n_layers = w["norm_ffn_pre"].shape[0]
    nblk = N // R
    row_spec = pl.BlockSpec((R, D), lambda i: (i, 0))
    if batch_major_seqs is None:
        out_spec, out_shape, staging = row_spec, jax.ShapeDtypeStruct((N, D), F32), []
    else:
        S = batch_major_seqs
        out_spec = pl.BlockSpec(memory_space=pl.ANY)
        out_shape = jax.ShapeDtypeStruct((S, N // S, D), F32)
        staging = [pltpu.VMEM((2, R // S, S, D), F32), pltpu.SemaphoreType.DMA((2, S))]
    return pl.pallas_call(
        functools.partial(_ffn_kernel, layer, nblk, batch_major_seqs is not None),
        grid=(nblk,),
        in_specs=[row_spec, _resident((n_layers, D)), _resident((n_layers, D)),
                  _resident(bf16["wg"].shape), _resident(bf16["wu"].shape), _resident((D_FF, D))],
        out_specs=out_spec,
        out_shape=out_shape,
        scratch_shapes=[pltpu.VMEM((R, D), BF16), pltpu.VMEM((R, D_FF), BF16)] + staging,
        compiler_params=_params(),
        name="swiglu_ffn",
    )(x, w["norm_ffn_pre"], w["norm_ffn_post"], bf16["wg"], bf16["wu"], bf16["wd"])


def _to_time_major(a):
    S, K, D = a.shape
    return jnp.swapaxes(a, 0, 1).reshape(K * S, D)


def _from_time_major(a, S):
    KS, D = a.shape
    return jnp.swapaxes(a.reshape(KS // S, S, D), 0, 1)


def kernel(x_prompt, x_sample, state_rglru_conv, state_rglru_h, state_sconv, meta_tokens, norm_mix_pre, norm_mix_post, norm_ffn_pre, norm_ffn_post, rg_w_in, rg_conv_w, rg_conv_b, rg_gate_a_w, rg_gate_a_b, rg_gate_x_w, rg_gate_x_b, rg_lambda, rg_w_out, sc_w_in, sc_conv_w, sc_w_out, ffn_w_gate, ffn_w_up, ffn_w_down):
    D = D_MODEL
    depth = norm_mix_pre.shape[0]
    batch, seq, _ = x_prompt.shape
    dec_batch, dec_seq, _ = x_sample.shape
    w = dict(norm_mix_pre=norm_mix_pre, norm_mix_post=norm_mix_post, norm_ffn_pre=norm_ffn_pre,
             norm_ffn_post=norm_ffn_post, rg_w_in=rg_w_in, rg_conv_w=rg_conv_w, rg_conv_b=rg_conv_b,
             rg_gate_a_w=rg_gate_a_w, rg_gate_a_b=rg_gate_a_b, rg_gate_x_w=rg_gate_x_w, rg_gate_x_b=rg_gate_x_b,
             rg_lambda=rg_lambda, rg_w_out=rg_w_out, sc_w_in=sc_w_in, sc_conv_w=sc_conv_w, sc_w_out=sc_w_out,
             ffn_w_gate=ffn_w_gate, ffn_w_up=ffn_w_up, ffn_w_down=ffn_w_down)

    x = jnp.broadcast_to(meta_tokens[:, None, :], (N_META, batch, D)).reshape(N_META * batch, D)
    small = ((batch, N_META), (dec_batch, dec_seq))
    mixer_bf16, ffn_bf16 = [], []
    rg_conv_s, rg_h_s, sc_s = [], [], []
    for i in range(depth):
        j = i // 2
        if i % 2 == 0:
            conv0 = [jnp.zeros(((CONV_A - 1) * batch, D), F32), _to_time_major(state_rglru_conv[j])]
            h0 = [jnp.zeros((batch, D), F32), state_rglru_h[j]]
            x, cb, hT, wb = _rglru_stream_layer(x, conv0, h0, i, j, w, seqs_and_steps=small,
                                                x_sample=x_sample if i == 0 else None)
            rg_conv_s.append(cb)
            rg_h_s.append(hT)
        else:
            conv0 = [jnp.zeros(((CONV_B - 1) * batch, D), F32), _to_time_major(state_sconv[j])]
            x, cb, wb = _sconv_stream_layer(x, conv0, i, j, w, seqs_and_steps=small)
            sc_s.append(cb)
        mixer_bf16.append(wb)
        sample_out = (N_META * batch, dec_batch, dec_seq) if i == depth - 1 else None
        x, wb = _ffn_stream_layer(x, i, w, sample_out=sample_out)
        ffn_bf16.append(wb)
    y_sample = x

    x = x_prompt
    rg_conv_p, rg_h_p, sc_p = [], [], []
    for i in range(depth):
        j = i // 2
        if i % 2 == 0:
            x, cb, hT = _rglru_layer(x, rg_conv_s[j][0], rg_h_s[j][0], i, j, w, mixer_bf16[i], S=batch, TB=PROMPT_TB)
            rg_conv_p.append(cb)
            rg_h_p.append(hT)
        else:
            x, cb = _sconv_layer(x, sc_s[j][0], i, j, w, mixer_bf16[i], S=batch, TB=PROMPT_TB)
            sc_p.append(cb)
        x = _ffn_layer(x, i, w, ffn_bf16[i], R=batch * PROMPT_TB,
                       batch_major_seqs=batch if i == depth - 1 else None)
    y_prompt = x

    return (y_prompt, y_sample,
            jnp.stack([_from_time_major(c, batch) for c in rg_conv_p]), jnp.stack(rg_h_p),
            jnp.stack([_from_time_major(c, batch) for c in sc_p]),
            jnp.stack([_from_time_major(c[1], dec_batch) for c in rg_conv_s]), jnp.stack([h[1] for h in rg_h_s]),
            jnp.stack([_from_time_major(c[1], dec_batch) for c in sc_s]))
```

```python
import functools
from typing import NamedTuple

import jax
import jax.numpy as jnp
from jax import lax
from jax.experimental import pallas as pl
from jax.experimental.pallas import tpu as pltpu

D_MODEL = 1024
D_FF = 2816
N_META = 16
COL_BLOCK = 256
N_COL_BLOCKS = D_MODEL // COL_BLOCK
CONV_A = 4
CONV_B = 3
RG_C = 8.0
EPS = 1e-6

SUBLANES = 8
FF_CHUNK = 256
PROMPT_TB = 128
VMEM_LIMIT_BYTES = 56 * 1024 * 1024

F32 = jnp.float32
BF16 = jnp.bfloat16


class _Seg(NamedTuple):
    S: int
    TB: int
    row0: int
    conv0: int
    h0: int

    @property
    def rows(self):
        return self.S * self.TB


def _segments(seqs_and_steps, taps):
    segs, row0, conv0, h0 = [], 0, 0, 0
    for S, TB in seqs_and_steps:
        segs.append(_Seg(S, TB, row0, conv0, h0))
        row0 += S * TB
        conv0 += (taps - 1) * S + S * TB
        h0 += S
    return tuple(segs), row0, conv0, h0


def _rms(x, w):
    ms = jnp.mean(x * x, axis=-1, keepdims=True)
    return x * lax.rsqrt(ms + EPS) * w


def _dot(a, b):
    return jnp.dot(a, b, preferred_element_type=F32)


def _gelu_tanh(x):
    c = 0.7978845608028654
    hx = 0.5 * x
    return hx + hx * jnp.tanh(x * (c + (c * 0.044715) * (x * x)))


def _resident(shape):
    zeros = (0,) * len(shape)
    return pl.BlockSpec(shape, lambda i: zeros, pipeline_mode=pl.Buffered(1))


def _chunk_major_out(rows, width=COL_BLOCK):
    return pl.BlockSpec((None, rows, width), lambda i: (i, 0, 0))


def _params():
    return pltpu.CompilerParams(dimension_semantics=("arbitrary",), vmem_limit_bytes=VMEM_LIMIT_BYTES)


def _accumulate(y_scr, rows, part, step):
    @pl.when(step == 0)
    def _():
        y_scr[rows, :] = part

    @pl.when(step > 0)
    def _():
        y_scr[rows, :] += part


def _rglru_branch_in(seg, xn, w_xr, region):
    P = (CONV_A - 1) * seg.S
    region[seg.conv0 + P:seg.conv0 + P + seg.rows, :] = _dot(xn, w_xr)


def _rglru_conv(seg, cw, cb, region, carry_to):
    S, R = seg.S, seg.rows
    P = (CONV_A - 1) * S
    c0 = seg.conv0
    xc = region[c0:c0 + R, :] * cw[0:1]
    for k in range(1, CONV_A):
        xc = xc + region[c0 + k * S:c0 + k * S + R, :] * cw[k:k + 1]
    xc = xc + cb
    if carry_to is not None:
        region[carry_to:carry_to + P, :] = region[c0 + R:c0 + R + P, :]
    return xc


def _rglru_gate_math(res, xc, half_gab, half_gxb, lam):
    half_c_sp = (-0.5 * RG_C) * jax.nn.softplus(-lam)
    tr = jnp.tanh(res[:, 0:COL_BLOCK] + half_gab)
    log_a = half_c_sp * tr + half_c_sp
    ig = 0.5 * jnp.tanh(res[:, COL_BLOCK:2 * COL_BLOCK] + half_gxb) + 0.5
    a = jnp.exp(log_a)
    m2 = jnp.tanh(log_a) * (-1.0 - a * a)
    u = jnp.where(m2 > 0.0, m2 * lax.rsqrt(m2), 0.0) * (ig * xc)
    return a, u


def _rglru_scan(seg, a, u, g, h_read, h_write, h_row0):
    S, TB = seg.S, seg.TB
    groups = S // SUBLANES
    pieces = [None] * (TB * groups)
    for c in range(groups):
        hrows = slice(h_row0 + c * SUBLANES, h_row0 + (c + 1) * SUBLANES)
        h = h_read[hrows, :]
        for t in range(TB):
            r = t * S + c * SUBLANES
            h = a[r:r + SUBLANES] * h + u[r:r + SUBLANES]
            pieces[t * groups + c] = h * g[r:r + SUBLANES]
        h_write[hrows, :] = h
    return jnp.concatenate(pieces, axis=0)


def _rglru_stream_kernel(segs, sample_in, layer, j, x_ref, *refs):
    if sample_in is not None:
        xs_hbm, refs = refs[0], refs[1:]
    n = len(segs)
    conv_in, h_in = refs[0:n], refs[n:2 * n]
    (npre_ref, npost_ref, wxr_ref, wgate_ref, cw_ref, cb_ref, gaw_ref, gxw_ref, gab_ref, gxb_ref,
     lam_ref, wout_ref) = refs[2 * n:2 * n + 12]
    o_ref = refs[2 * n + 12]
    conv_out, h_out = refs[2 * n + 13:3 * n + 13], refs[3 * n + 13:4 * n + 13]
    wxr_b_ref, wgate_b_ref, gw_b_ref, wout_b_ref = refs[4 * n + 13:4 * n + 17]
    xn_scr, xr_scr, y_scr, *in_staging = refs[4 * n + 17:]
    b = pl.program_id(0)

    if sample_in is None:
        read_x = lambda: x_ref[...]
    else:
        S, T = sample_in
        xs_scr, sems = in_staging
        read_x = lambda: jnp.concatenate([x_ref[...], xs_scr[...].reshape(T * S, D_MODEL)], axis=0)

    @pl.when(b == 0)
    def _():
        if sample_in is not None:
            copies = [pltpu.make_async_copy(xs_hbm.at[:, t, :], xs_scr.at[t], sems.at[t]) for t in range(T)]
            for cp in copies:
                cp.start()
            for cp in copies:
                cp.wait()
        xn_scr[...] = _rms(read_x(), npre_ref[layer:layer + 1, :]).astype(BF16)

    w_xr = wxr_ref[...].astype(BF16)
    w_gate = wgate_ref[...].astype(BF16)
    gw = (0.5 * jnp.concatenate([gaw_ref[...], gxw_ref[...]], axis=1)).astype(BF16)
    w_out_rows = wout_ref[...].astype(BF16)
    wxr_b_ref[...] = w_xr
    wgate_b_ref[...] = w_gate
    gw_b_ref[...] = gw
    wout_b_ref[...] = w_out_rows

    rows_of = lambda seg: slice(seg.row0, seg.row0 + seg.rows)
    xn = xn_scr[...]
    xr = _dot(xn, w_xr)
    g = _gelu_tanh(_dot(xn, w_gate))
    xcs = []
    for seg, c_in, c_out in zip(segs, conv_in, conv_out):
        P = (CONV_A - 1) * seg.S
        xr_scr[seg.conv0:seg.conv0 + P, :] = c_in[...]
        xr_scr[seg.conv0 + P:seg.conv0 + P + seg.rows, :] = xr[rows_of(seg)]
        xcs.append(_rglru_conv(seg, cw_ref[...], cb_ref[j:j + 1, :], xr_scr, seg.conv0))
        c_out[...] = xr_scr[seg.conv0:seg.conv0 + P, :]
    xc = jnp.concatenate(xcs, axis=0)
    a, u = _rglru_gate_math(_dot(xc.astype(BF16), gw), xc, 0.5 * gab_ref[pl.ds(b, 1), :],
                            0.5 * gxb_ref[pl.ds(b, 1), :], lam_ref[j:j + 1, :])
    hs = [_rglru_scan(seg, a[rows_of(seg)], u[rows_of(seg)], g[rows_of(seg)], hi, ho, 0)
          for seg, hi, ho in zip(segs, h_in, h_out)]
    part = _dot(jnp.concatenate(hs, axis=0).astype(BF16), w_out_rows)
    _accumulate(y_scr, slice(None), part, b)

    @pl.when(b == N_COL_BLOCKS - 1)
    def _():
        o_ref[...] = read_x() + _rms(y_scr[...], npost_ref[layer:layer + 1, :])


def _rglru_stream_layer(x, conv_in, h_in, layer, j, w, *, seqs_and_steps, x_sample=None):
    D, C = D_MODEL, COL_BLOCK
    segs, R, conv_rows, _ = _segments(seqs_and_steps, CONV_A)
    n_layers, n_a = w["norm_mix_pre"].shape[0], w["rg_conv_b"].shape[0]
    col = lambda rows: pl.BlockSpec((rows, C), lambda b: (0, b))
    if x_sample is None:
        x_args, x_specs, sample_in, staging = [x], [_resident(x.shape)], None, []
    else:
        S, T, _ = x_sample.shape
        x_args, x_specs, sample_in = [x, x_sample], [_resident(x.shape), pl.BlockSpec(memory_space=pl.ANY)], (S, T)
        staging = [pltpu.VMEM((T, S, D), F32), pltpu.SemaphoreType.DMA((T,))]
    in_specs = (
        x_specs + [col(c.shape[0]) for c in conv_in] + [col(h.shape[0]) for h in h_in]
        + [_resident((n_layers, D)), _resident((n_layers, D)),
           pl.BlockSpec((None, D, C), lambda b: (j, 0, N_COL_BLOCKS + b)),
           pl.BlockSpec((None, D, C), lambda b: (j, 0, b)),
           pl.BlockSpec((None, CONV_A, C), lambda b: (j, 0, b)),
           col(n_a),
           pl.BlockSpec((None, None, C, C), lambda b: (j, b, 0, 0)),
           pl.BlockSpec((None, None, C, C), lambda b: (j, b, 0, 0)),
           pl.BlockSpec((None, N_COL_BLOCKS, C), lambda b: (j, 0, 0)),
           pl.BlockSpec((None, N_COL_BLOCKS, C), lambda b: (j, 0, 0)),
           col(n_a),
           pl.BlockSpec((None, C, D), lambda b: (j, b, 0))])
    out_specs = ([pl.BlockSpec((R, D), lambda b: (0, 0))] + [col(c.shape[0]) for c in conv_in]
                 + [col(h.shape[0]) for h in h_in]
                 + [_chunk_major_out(D), _chunk_major_out(D), pl.BlockSpec((None, C, 2 * C), lambda b: (b, 0, 0)),
                    pl.BlockSpec((C, D), lambda b: (b, 0))])
    out_shape = ([jax.ShapeDtypeStruct((R, D), F32)] + [jax.ShapeDtypeStruct(s.shape, F32) for s in conv_in + h_in]
                 + [jax.ShapeDtypeStruct((N_COL_BLOCKS, D, C), BF16), jax.ShapeDtypeStruct((N_COL_BLOCKS, D, C), BF16),
                    jax.ShapeDtypeStruct((N_COL_BLOCKS, C, 2 * C), BF16), jax.ShapeDtypeStruct((D, D), BF16)])
    outs = pl.pallas_call(
        functools.partial(_rglru_stream_kernel, segs, sample_in, layer, j),
        grid=(N_COL_BLOCKS,),
        in_specs=in_specs, out_specs=out_specs, out_shape=out_shape,
        scratch_shapes=[pltpu.VMEM((R, D), BF16),
                        pltpu.VMEM((conv_rows, C), F32),
                        pltpu.VMEM((R, D), F32)]
                       + staging,
        compiler_params=_params(),
        name="rglru_mixer_small",
    )(*x_args, *conv_in, *h_in, w["norm_mix_pre"], w["norm_mix_post"], w["rg_w_in"], w["rg_w_in"], w["rg_conv_w"],
      w["rg_conv_b"], w["rg_gate_a_w"], w["rg_gate_x_w"], w["rg_gate_a_b"], w["rg_gate_x_b"], w["rg_lambda"],
      w["rg_w_out"])
    n = len(segs)
    bf16 = dict(w_xr=outs[2 * n + 1], w_gate=outs[2 * n + 2], gw=outs[2 * n + 3], w_out=outs[2 * n + 4])
    return outs[0], list(outs[1:1 + n]), list(outs[1 + n:1 + 2 * n]), bf16


def _rglru_kernel(seg, nblk, layer, j, batch_major_in,
                  x_ref, conv_in_ref, h_in_ref, npre_ref, npost_ref, wxr_ref, wgate_ref, cw_ref, cb_ref,
                  gw_ref, gab_ref, gxb_ref, lam_ref, wout_ref,
                  o_ref, conv_out_ref, h_out_ref, xn_scr, xr_scr, h_scr, *in_staging):
    S, TB, R = seg.S, seg.TB, seg.rows
    P = (CONV_A - 1) * S
    i = pl.program_id(0)
    slot = i % 2
    blocks = [slice(b * COL_BLOCK, (b + 1) * COL_BLOCK) for b in range(N_COL_BLOCKS)]

    if batch_major_in:
        xbuf, sems = in_staging

        def copies(which, step):
            t0 = pl.multiple_of(step * TB, TB)
            return [pltpu.make_async_copy(x_ref.at[s, pl.ds(t0, TB), :], xbuf.at[which, :, s, :],
                                          sems.at[which, s]) for s in range(S)]

    @pl.when(i == 0)
    def _():
        for b, cols in enumerate(blocks):
            xr_scr[b, 0:P, :] = conv_in_ref[:, cols]
            h_scr[b] = h_in_ref[:, cols]
        if batch_major_in:
            for c in copies(0, 0):
                c.start()

    if batch_major_in:
        @pl.when(i + 1 < nblk)
        def _():
            for c in copies(1 - slot, i + 1):
                c.start()

        for c in copies(slot, i):
            c.wait()
        read_x = lambda: xbuf[slot].reshape(R, D_MODEL)
    else:
        read_x = lambda: x_ref[...]

    xn_scr[...] = _rms(read_x(), npre_ref[layer:layer + 1, :]).astype(BF16)
    xn = xn_scr[...]

    gate_pre, xc, res = {}, {}, {}

    def input_matmuls(b):
        _rglru_branch_in(seg, xn, wxr_ref[b], xr_scr.at[b])
        gate_pre[b] = _dot(xn, wgate_ref[b])

    def conv_and_gate_matmul(b):
        xc[b] = _rglru_conv(seg, cw_ref[:, blocks[b]], cb_ref[j:j + 1, blocks[b]], xr_scr.at[b], 0)
        res[b] = _dot(xc[b].astype(BF16), gw_ref[b])

    y = None
    input_matmuls(0)
    conv_and_gate_matmul(0)
    input_matmuls(1)
    for b, cols in enumerate(blocks):
        if b + 1 < N_COL_BLOCKS:
            conv_and_gate_matmul(b + 1)
        if b + 2 < N_COL_BLOCKS:
            input_matmuls(b + 2)
        g = _gelu_tanh(gate_pre.pop(b))
        a, u = _rglru_gate_math(res.pop(b), xc.pop(b), 0.5 * gab_ref[b:b + 1, :], 0.5 * gxb_ref[b:b + 1, :],
                                lam_ref[j:j + 1, cols])
        hs = _rglru_scan(seg, a, u, g, h_scr.at[b], h_scr.at[b], 0)
        part = _dot(hs.astype(BF16), wout_ref[cols, :])
        y = part if y is None else y + part

    o_ref[...] = read_x() + _rms(y, npost_ref[layer:layer + 1, :])

    @pl.when(i == nblk - 1)
    def _():
        for b, cols in enumerate(blocks):
            conv_out_ref[:, cols] = xr_scr[b, 0:P, :]
            h_out_ref[:, cols] = h_scr[b]


def _rglru_layer(x, conv_in, h_in, layer, j, w, bf16, *, S, TB):
    D = D_MODEL
    batch_major_in = x.ndim == 3
    (seg,), R, conv_rows, _ = _segments(((S, TB),), CONV_A)
    nblk = x.size // D // R
    P = (CONV_A - 1) * S
    n_layers, n_a = w["norm_mix_pre"].shape[0], w["rg_conv_b"].shape[0]
    row_spec = pl.BlockSpec((R, D), lambda i: (i, 0))
    if batch_major_in:
        x_spec = pl.BlockSpec(memory_space=pl.ANY)
        staging = [pltpu.VMEM((2, TB, S, D), F32), pltpu.SemaphoreType.DMA((2, S))]
    else:
        x_spec, staging = row_spec, []
    layer_of = lambda arr: pl.BlockSpec((None,) + arr.shape[1:], lambda i: (j,) + (0,) * (arr.ndim - 1),
                                        pipeline_mode=pl.Buffered(1))
    return pl.pallas_call(
        functools.partial(_rglru_kernel, seg, nblk, layer, j, batch_major_in),
        grid=(nblk,),
        in_specs=[x_spec, _resident((P, D)), _resident((S, D)),
                  _resident((n_layers, D)), _resident((n_layers, D)),
                  _resident(bf16["w_xr"].shape), _resident(bf16["w_gate"].shape),
                  layer_of(w["rg_conv_w"]), _resident((n_a, D)), _resident(bf16["gw"].shape),
                  layer_of(w["rg_gate_a_b"]), layer_of(w["rg_gate_x_b"]), _resident((n_a, D)),
                  _resident((D, D))],
        out_specs=[row_spec, pl.BlockSpec((P, D), lambda i: (0, 0)), pl.BlockSpec((S, D), lambda i: (0, 0))],
        out_shape=[jax.ShapeDtypeStruct((nblk * R, D), F32), jax.ShapeDtypeStruct((P, D), F32),
                   jax.ShapeDtypeStruct((S, D), F32)],
        scratch_shapes=[pltpu.VMEM((R, D), BF16),
                        pltpu.VMEM((N_COL_BLOCKS, conv_rows, COL_BLOCK), F32),
                        pltpu.VMEM((N_COL_BLOCKS, S, COL_BLOCK), F32)]
                       + staging,
        compiler_params=_params(),
        name="rglru_mixer",
    )(x, conv_in, h_in, w["norm_mix_pre"], w["norm_mix_post"], bf16["w_xr"], bf16["w_gate"], w["rg_conv_w"],
      w["rg_conv_b"], bf16["gw"], w["rg_gate_a_b"], w["rg_gate_x_b"], w["rg_lambda"], bf16["w_out"])


def _sconv_conv(seg, cv, cw, region):
    S, R = seg.S, seg.rows
    P = (CONV_B - 1) * S
    c0 = seg.conv0
    region[c0 + P:c0 + P + R, :] = cv
    conv = region[c0:c0 + R, :] * cw[0:1]
    for k in range(1, CONV_B):
        conv = conv + region[c0 + k * S:c0 + k * S + R, :] * cw[k:k + 1]
    region[c0:c0 + P, :] = region[c0 + R:c0 + R + P, :]
    return conv


def _sconv_stream_kernel(segs, layer, j, x_ref, *refs):
    n = len(segs)
    conv_in = refs[0:n]
    npre_ref, npost_ref, wbg_ref, wcg_ref, wv_ref, cw_ref, wout_ref = refs[n:n + 7]
    o_ref = refs[n + 7]
    conv_out = refs[n + 8:2 * n + 8]
    wbg_b_ref, wcg_b_ref, wv_b_ref, wout_b_ref = refs[2 * n + 8:2 * n + 12]
    xn_scr, cv_scr, y_scr = refs[2 * n + 12:]
    b = pl.program_id(0)

    @pl.when(b == 0)
    def _():
        xn_scr[...] = _rms(x_ref[...], npre_ref[layer:layer + 1, :]).astype(BF16)

    w_bg = wbg_ref[...].astype(BF16)
    w_cg = wcg_ref[...].astype(BF16)
    w_v = wv_ref[...].astype(BF16)
    w_out_rows = wout_ref[...].astype(BF16)
    wbg_b_ref[...] = w_bg
    wcg_b_ref[...] = w_cg
    wv_b_ref[...] = w_v
    wout_b_ref[...] = w_out_rows

    xn = xn_scr[...]
    cv = _dot(xn, w_cg) * _dot(xn, w_v)
    convs = []
    for seg, c_in, c_out in zip(segs, conv_in, conv_out):
        P = (CONV_B - 1) * seg.S
        cv_scr[seg.conv0:seg.conv0 + P, :] = c_in[...]
        convs.append(_sconv_conv(seg, cv[seg.row0:seg.row0 + seg.rows], cw_ref[...], cv_scr))
        c_out[...] = cv_scr[seg.conv0:seg.conv0 + P, :]
    m = (_dot(xn, w_bg) * jnp.concatenate(convs, axis=0)).astype(BF16)
    _accumulate(y_scr, slice(None), _dot(m, w_out_rows), b)

    @pl.when(b == N_COL_BLOCKS - 1)
    def _():
        o_ref[...] = x_ref[...] + _rms(y_scr[...], npost_ref[layer:layer + 1, :])


def _sconv_stream_layer(x, conv_in, layer, j, w, *, seqs_and_steps):
    R, D = x.shape
    C = COL_BLOCK
    segs, _, conv_rows, _ = _segments(seqs_and_steps, CONV_B)
    n_layers = w["norm_mix_pre"].shape[0]
    col = lambda rows: pl.BlockSpec((rows, C), lambda b: (0, b))
    w_in_part = lambda k: pl.BlockSpec((None, D, C), lambda b: (j, 0, k * N_COL_BLOCKS + b))
    outs = pl.pallas_call(
        functools.partial(_sconv_stream_kernel, segs, layer, j),
        grid=(N_COL_BLOCKS,),
        in_specs=[_resident((R, D))] + [col(c.shape[0]) for c in conv_in]
                 + [_resident((n_layers, D)), _resident((n_layers, D)), w_in_part(0), w_in_part(1), w_in_part(2),
                    pl.BlockSpec((None, CONV_B, C), lambda b: (j, 0, b)),
                    pl.BlockSpec((None, C, D), lambda b: (j, b, 0))],
        out_specs=[pl.BlockSpec((R, D), lambda b: (0, 0))] + [col(c.shape[0]) for c in conv_in]
                  + [_chunk_major_out(D)] * 3 + [pl.BlockSpec((C, D), lambda b: (b, 0))],
        out_shape=[jax.ShapeDtypeStruct((R, D), F32)] + [jax.ShapeDtypeStruct(c.shape, F32) for c in conv_in]
                  + [jax.ShapeDtypeStruct((N_COL_BLOCKS, D, C), BF16)] * 3 + [jax.ShapeDtypeStruct((D, D), BF16)],
        scratch_shapes=[pltpu.VMEM((R, D), BF16),
                        pltpu.VMEM((conv_rows, C), F32),
                        pltpu.VMEM((R, D), F32)],
        compiler_params=_params(),
        name="sconv_mixer_small",
    )(x, *conv_in, w["norm_mix_pre"], w["norm_mix_post"], w["sc_w_in"], w["sc_w_in"], w["sc_w_in"],
      w["sc_conv_w"], w["sc_w_out"])
    n = len(segs)
    bf16 = dict(w_bg=outs[n + 1], w_cg=outs[n + 2], w_v=outs[n + 3], w_out=outs[n + 4])
    return outs[0], list(outs[1:1 + n]), bf16


def _sconv_kernel(seg, nblk, layer, x_ref, conv_in_ref, npre_ref, npost_ref, wbg_ref, wcg_ref, wv_ref,
                  cw_ref, wout_ref, o_ref, conv_out_ref, xn_scr, cv_scr, m_scr):
    P = (CONV_B - 1) * seg.S
    i = pl.program_id(0)
    blocks = [slice(b * COL_BLOCK, (b + 1) * COL_BLOCK) for b in range(N_COL_BLOCKS)]

    @pl.when(i == 0)
    def _():
        for b, cols in enumerate(blocks):
            cv_scr[b, 0:P, :] = conv_in_ref[:, cols]

    xn_scr[...] = _rms(x_ref[...], npre_ref[layer:layer + 1, :]).astype(BF16)
    xn = xn_scr[...]
    for b, cols in enumerate(blocks):
        cv = _dot(xn, wcg_ref[b]) * _dot(xn, wv_ref[b])
        conv = _sconv_conv(seg, cv, cw_ref[:, cols], cv_scr.at[b])
        m_scr[:, cols] = (_dot(xn, wbg_ref[b]) * conv).astype(BF16)
    y = _dot(m_scr[...], wout_ref[...])
    o_ref[...] = x_ref[...] + _rms(y, npost_ref[layer:layer + 1, :])

    @pl.when(i == nblk - 1)
    def _():
        for b, cols in enumerate(blocks):
            conv_out_ref[:, cols] = cv_scr[b, 0:P, :]


def _sconv_layer(x, conv_in, layer, j, w, bf16, *, S, TB):
    N, D = x.shape
    (seg,), R, conv_rows, _ = _segments(((S, TB),), CONV_B)
    nblk = N // R
    P = (CONV_B - 1) * S
    n_layers = w["norm_mix_pre"].shape[0]
    row_spec = pl.BlockSpec((R, D), lambda i: (i, 0))
    return pl.pallas_call(
        functools.partial(_sconv_kernel, seg, nblk, layer),
        grid=(nblk,),
        in_specs=[row_spec, _resident((P, D)), _resident((n_layers, D)), _resident((n_layers, D)),
                  _resident(bf16["w_bg"].shape), _resident(bf16["w_cg"].shape), _resident(bf16["w_v"].shape),
                  pl.BlockSpec((None, CONV_B, D), lambda i: (j, 0, 0), pipeline_mode=pl.Buffered(1)),
                  _resident((D, D))],
        out_specs=[row_spec, pl.BlockSpec((P, D), lambda i: (0, 0))],
        out_shape=[jax.ShapeDtypeStruct((N, D), F32), jax.ShapeDtypeStruct((P, D), F32)],
        scratch_shapes=[pltpu.VMEM((R, D), BF16),
                        pltpu.VMEM((N_COL_BLOCKS, conv_rows, COL_BLOCK), F32),
                        pltpu.VMEM((R, D), BF16)],
        compiler_params=_params(),
        name="sconv_mixer",
    )(x, conv_in, w["norm_mix_pre"], w["norm_mix_post"], bf16["w_bg"], bf16["w_cg"], bf16["w_v"],
      w["sc_conv_w"], bf16["w_out"])


def _swiglu(g, u):
    return (g * jax.nn.sigmoid(g) * u).astype(BF16)


def _ffn_stream_kernel(layer, nchunk, sample_out, x_ref, npre_ref, npost_ref, wg_hbm, wu_hbm, wd_hbm,
                       o_ref, wg_b_hbm, wu_b_hbm, wd_b_hbm,
                       xn_scr, wg_f, wu_f, wd_f, wg_h, wu_h, wd_h, in_sems, out_sems, *out_staging):
    F = FF_CHUNK

    def fetch(c):
        k, cols = c % 2, pl.ds(c * F, F)
        return [pltpu.make_async_copy(wg_hbm.at[layer, :, cols], wg_f.at[k], in_sems.at[0, k]),
                pltpu.make_async_copy(wu_hbm.at[layer, :, cols], wu_f.at[k], in_sems.at[1, k]),
                pltpu.make_async_copy(wd_hbm.at[layer, cols, :], wd_f.at[k], in_sems.at[2, k])]

    def emit(c):
        k = c % 2
        return [pltpu.make_async_copy(wg_h.at[k], wg_b_hbm.at[c], out_sems.at[0, k]),
                pltpu.make_async_copy(wu_h.at[k], wu_b_hbm.at[c], out_sems.at[1, k]),
                pltpu.make_async_copy(wd_h.at[k], wd_b_hbm.at[pl.ds(c * F, F), :], out_sems.at[2, k])]

    for c in range(min(2, nchunk)):
        for cp in fetch(c):
            cp.start()
    xn_scr[...] = _rms(x_ref[...], npre_ref[layer:layer + 1, :]).astype(BF16)
    xn = xn_scr[...]

    y = None
    for c in range(nchunk):
        k = c % 2
        for cp in fetch(c):
            cp.wait()
        if c >= 2:
            for cp in emit(c - 2):
                cp.wait()
        wg, wu, wd = wg_f[k].astype(BF16), wu_f[k].astype(BF16), wd_f[k].astype(BF16)
        wg_h[k], wu_h[k], wd_h[k] = wg, wu, wd
        for cp in emit(c):
            cp.start()
        if c + 2 < nchunk:
            for cp in fetch(c + 2):
                cp.start()
        part = _dot(_swiglu(_dot(xn, wg), _dot(xn, wu)), wd)
        y = part if y is None else y + part
    for c in range(max(nchunk - 2, 0), nchunk):
        for cp in emit(c):
            cp.wait()

    out = x_ref[...] + _rms(y, npost_ref[layer:layer + 1, :])
    if sample_out is None:
        o_ref[...] = out
    else:
        row0, S, T = sample_out
        obuf, sems = out_staging
        obuf[...] = out[row0:row0 + T * S].reshape(T, S, out.shape[-1])
        copies = [pltpu.make_async_copy(obuf.at[t], o_ref.at[:, t, :], sems.at[t]) for t in range(T)]
        for cp in copies:
            cp.start()
        for cp in copies:
            cp.wait()


def _ffn_stream_layer(x, layer, w, *, sample_out=None):
    R, D = x.shape
    F = FF_CHUNK
    n_layers = w["norm_ffn_pre"].shape[0]
    nchunk = D_FF // F
    hbm = pl.BlockSpec(memory_space=pl.ANY)
    if sample_out is None:
        o_spec, o_shape, staging = pl.BlockSpec((R, D), lambda i: (0, 0)), jax.ShapeDtypeStruct((R, D), F32), []
    else:
        _, S, T = sample_out
        o_spec, o_shape = hbm, jax.ShapeDtypeStruct((S, T, D), F32)
        staging = [pltpu.VMEM((T, S, D), F32), pltpu.SemaphoreType.DMA((T,))]
    outs = pl.pallas_call(
        functools.partial(_ffn_stream_kernel, layer, nchunk, sample_out),
        grid=(1,),
        in_specs=[_resident((R, D)), _resident((n_layers, D)), _resident((n_layers, D)), hbm, hbm, hbm],
        out_specs=[o_spec, hbm, hbm, hbm],
        out_shape=[o_shape, jax.ShapeDtypeStruct((nchunk, D, F), BF16),
                   jax.ShapeDtypeStruct((nchunk, D, F), BF16), jax.ShapeDtypeStruct((D_FF, D), BF16)],
        scratch_shapes=[pltpu.VMEM((R, D), BF16),
                        pltpu.VMEM((2, D, F), F32), pltpu.VMEM((2, D, F), F32), pltpu.VMEM((2, F, D), F32),
                        pltpu.VMEM((2, D, F), BF16), pltpu.VMEM((2, D, F), BF16), pltpu.VMEM((2, F, D), BF16),
                        pltpu.SemaphoreType.DMA((3, 2)), pltpu.SemaphoreType.DMA((3, 2))] + staging,
        compiler_params=_params(),
        name="swiglu_ffn_small",
    )(x, w["norm_ffn_pre"], w["norm_ffn_post"], w["ffn_w_gate"], w["ffn_w_up"], w["ffn_w_down"])
    return outs[0], dict(wg=outs[1], wu=outs[2], wd=outs[3])


def _ffn_kernel(layer, nblk, batch_major_out, x_ref, npre_ref, npost_ref, wg_ref, wu_ref, wd_ref, o_ref,
                xn_scr, act_scr, *out_staging):
    R = x_ref.shape[0]
    i = pl.program_id(0)
    slot = i % 2
    if batch_major_out:
        obuf, sems = out_staging
        S, TB = o_ref.shape[0], R // o_ref.shape[0]

        def copies(which, step):
            t0 = pl.multiple_of(step * TB, TB)
            return [pltpu.make_async_copy(obuf.at[which, :, s, :],
                                          o_ref.at[s, pl.ds(t0, TB), :], sems.at[which, s]) for s in range(S)]

        @pl.when(i >= 2)
        def _():
            for c in copies(slot, i - 2):
                c.wait()

    halves = [slice(0, R // 2), slice(R // 2, R)]
    for rows in halves:
        xn_scr[rows, :] = _rms(x_ref[rows, :], npre_ref[layer:layer + 1, :]).astype(BF16)
    for rows in halves:
        xn = xn_scr[rows, :]
        for c in range(D_FF // FF_CHUNK):
            cols = slice(c * FF_CHUNK, (c + 1) * FF_CHUNK)
            act_scr[rows, cols] = _swiglu(_dot(xn, wg_ref[c]), _dot(xn, wu_ref[c]))
    ys = [_dot(act_scr[rows, :], wd_ref[...]) for rows in halves]
    for rows, y in zip(halves, ys):
        out = x_ref[rows, :] + _rms(y, npost_ref[layer:layer + 1, :])
        if batch_major_out:
            obuf[slot, rows.start // S:rows.stop // S] = out.reshape(-1, S, out.shape[-1])
        else:
            o_ref[rows, :] = out

    if batch_major_out:
        for c in copies(slot, i):
            c.start()

        @pl.when(i == nblk - 1)
        def _():
            if nblk >= 2:
                for c in copies(1 - slot, i - 1):
                    c.wait()
            for c in copies(slot, i):
                c.wait()


def _ffn_layer(x, layer, w, bf16, *, R, batch_major_seqs=None):
    N, D = x.shape
    n_layers = w["norm_ffn_pre"].shape[0]
    nblk = N // R
    row_spec = pl.BlockSpec((R, D), lambda i: (i, 0))
    if batch_major_seqs is None:
        out_spec, out_shape, staging = row_spec, jax.ShapeDtypeStruct((N, D), F32), []
    else:
        S = batch_major_seqs
        out_spec = pl.BlockSpec(memory_space=pl.ANY)
        out_shape = jax.ShapeDtypeStruct((S, N // S, D), F32)
        staging = [pltpu.VMEM((2, R // S, S, D), F32), pltpu.SemaphoreType.DMA((2, S))]
    return pl.pallas_call(
        functools.partial(_ffn_kernel, layer, nblk, batch_major_seqs is not None),
        grid=(nblk,),
        in_specs=[row_spec, _resident((n_layers, D)), _resident((n_layers, D)),
                  _resident(bf16["wg"].shape), _resident(bf16["wu"].shape), _resident((D_FF, D))],
        out_specs=out_spec,
        out_shape=out_shape,
        scratch_shapes=[pltpu.VMEM((R, D), BF16), pltpu.VMEM((R, D_FF), BF16)] + staging,
        compiler_params=_params(),
        name="swiglu_ffn",
    )(x, w["norm_ffn_pre"], w["norm_ffn_post"], bf16["wg"], bf16["wu"], bf16["wd"])


def _to_time_major(a):
    S, K, D = a.shape
    return jnp.swapaxes(a, 0, 1).reshape(K * S, D)


def _from_time_major(a, S):
    KS, D = a.shape
    return jnp.swapaxes(a.reshape(KS // S, S, D), 0, 1)


def kernel(x_prompt, x_sample, state_rglru_conv, state_rglru_h, state_sconv, meta_tokens, norm_mix_pre, norm_mix_post, norm_ffn_pre, norm_ffn_post, rg_w_in, rg_conv_w, rg_conv_b, rg_gate_a_w, rg_gate_a_b, rg_gate_x_w, rg_gate_x_b, rg_lambda, rg_w_out, sc_w_in, sc_conv_w, sc_w_out, ffn_w_gate, ffn_w_up, ffn_w_down):
    D = D_MODEL
    depth = norm_mix_pre.shape[0]
    batch, seq, _ = x_prompt.shape
    dec_batch, dec_seq, _ = x_sample.shape
    w = dict(norm_mix_pre=norm_mix_pre, norm_mix_post=norm_mix_post, norm_ffn_pre=norm_ffn_pre,
             norm_ffn_post=norm_ffn_post, rg_w_in=rg_w_in, rg_conv_w=rg_conv_w, rg_conv_b=rg_conv_b,
             rg_gate_a_w=rg_gate_a_w, rg_gate_a_b=rg_gate_a_b, rg_gate_x_w=rg_gate_x_w, rg_gate_x_b=rg_gate_x_b,
             rg_lambda=rg_lambda, rg_w_out=rg_w_out, sc_w_in=sc_w_in, sc_conv_w=sc_conv_w, sc_w_out=sc_w_out,
             ffn_w_gate=ffn_w_gate, ffn_w_up=ffn_w_up, ffn_w_down=ffn_w_down)

    x = jnp.broadcast_to(meta_tokens[:, None, :], (N_META, batch, D)).reshape(N_META * batch, D)
    small = ((batch, N_META), (dec_batch, dec_seq))
    mixer_bf16, ffn_bf16 = [], []
    rg_conv_s, rg_h_s, sc_s = [], [], []
    for i in range(depth):
        j = i // 2
        if i % 2 == 0:
            conv0 = [jnp.zeros(((CONV_A - 1) * batch, D), F32), _to_time_major(state_rglru_conv[j])]
            h0 = [jnp.zeros((batch, D), F32), state_rglru_h[j]]
            x, cb, hT, wb = _rglru_stream_layer(x, conv0, h0, i, j, w, seqs_and_steps=small,
                                                x_sample=x_sample if i == 0 else None)
            rg_conv_s.append(cb)
            rg_h_s.append(hT)
        else:
            conv0 = [jnp.zeros(((CONV_B - 1) * batch, D), F32), _to_time_major(state_sconv[j])]
            x, cb, wb = _sconv_stream_layer(x, conv0, i, j, w, seqs_and_steps=small)
            sc_s.append(cb)
        mixer_bf16.append(wb)
        sample_out = (N_META * batch, dec_batch, dec_seq) if i == depth - 1 else None
        x, wb = _ffn_stream_layer(x, i, w, sample_out=sample_out)
        ffn_bf16.append(wb)
    y_sample = x

    x = x_prompt
    rg_conv_p, rg_h_p, sc_p = [], [], []
    for i in range(depth):
        j = i // 2
        if i % 2 == 0:
            x, cb, hT = _rglru_layer(x, rg_conv_s[j][0], rg_h_s[j][0], i, j, w, mixer_bf16[i], S=batch, TB=PROMPT_TB)
            rg_conv_p.append(cb)
            rg_h_p.append(hT)
        else:
            x, cb = _sconv_layer(x, sc_s[j][0], i, j, w, mixer_bf16[i], S=batch, TB=PROMPT_TB)
            sc_p.append(cb)
        x = _ffn_layer(x, i, w, ffn_bf16[i], R=batch * PROMPT_TB,
                       batch_major_seqs=batch if i == depth - 1 else None)
    y_prompt = x

    return (y_prompt, y_sample,
            jnp.stack([_from_time_major(c, batch) for c in rg_conv_p]), jnp.stack(rg_h_p),
            jnp.stack([_from_time_major(c, batch) for c in sc_p]),
            jnp.stack([_from_time_major(c[1], dec_batch) for c in rg_conv_s]), jnp.stack([h[1] for h in rg_h_s]),
            jnp.stack([_from_time_major(c[1], dec_batch) for c in sc_s]))
```

```python
import functools
from typing import NamedTuple

import jax
import jax.numpy as jnp
from jax import lax
from jax.experimental import pallas as pl
from jax.experimental.pallas import tpu as pltpu

D_MODEL = 1024
D_FF = 2816
N_META = 16
COL_BLOCK = 256
N_COL_BLOCKS = D_MODEL // COL_BLOCK
CONV_A = 4
CONV_B = 3
RG_C = 8.0
EPS = 1e-6

SUBLANES = 8
FF_CHUNK = 256
STREAM_BUFFERS = 3
PROMPT_TB = 128
VMEM_LIMIT_BYTES = 56 * 1024 * 1024

F32 = jnp.float32
BF16 = jnp.bfloat16


class _Seg(NamedTuple):
    S: int
    TB: int
    row0: int
    conv0: int
    h0: int

    @property
    def rows(self):
        return self.S * self.TB


def _segments(seqs_and_steps, taps):
    segs, row0, conv0, h0 = [], 0, 0, 0
    for S, TB in seqs_and_steps:
        segs.append(_Seg(S, TB, row0, conv0, h0))
        row0 += S * TB
        conv0 += (taps - 1) * S + S * TB
        h0 += S
    return tuple(segs), row0, conv0, h0


def _rms(x, w):
    ms = jnp.mean(x * x, axis=-1, keepdims=True)
    return x * lax.rsqrt(ms + EPS) * w


def _dot(a, b):
    return jnp.dot(a, b, preferred_element_type=F32)


def _gelu_tanh(x):
    c = 0.7978845608028654
    hx = 0.5 * x
    return hx + hx * jnp.tanh(x * (c + (c * 0.044715) * (x * x)))


def _resident(shape):
    zeros = (0,) * len(shape)
    return pl.BlockSpec(shape, lambda i: zeros, pipeline_mode=pl.Buffered(1))


def _chunk_major_out(rows, width=COL_BLOCK):
    return pl.BlockSpec((None, rows, width), lambda i: (i, 0, 0))


def _params():
    return pltpu.CompilerParams(dimension_semantics=("arbitrary",), vmem_limit_bytes=VMEM_LIMIT_BYTES)


def _accumulate(y_scr, rows, part, step):
    @pl.when(step == 0)
    def _():
        y_scr[rows, :] = part

    @pl.when(step > 0)
    def _():
        y_scr[rows, :] += part


def _rglru_branch_in(seg, xn, w_xr, region):
    P = (CONV_A - 1) * seg.S
    region[seg.conv0 + P:seg.conv0 + P + seg.rows, :] = _dot(xn, w_xr)


def _rglru_conv(seg, cw, cb, region, carry_to):
    S, R = seg.S, seg.rows
    P = (CONV_A - 1) * S
    c0 = seg.conv0
    xc = region[c0:c0 + R, :] * cw[0:1]
    for k in range(1, CONV_A):
        xc = xc + region[c0 + k * S:c0 + k * S + R, :] * cw[k:k + 1]
    xc = xc + cb
    if carry_to is not None:
        region[carry_to:carry_to + P, :] = region[c0 + R:c0 + R + P, :]
    return xc


def _rglru_gate_math(res, xc, half_gab, half_gxb, lam):
    half_c_sp = (-0.5 * RG_C) * jax.nn.softplus(-lam)
    tr = jnp.tanh(res[:, 0:COL_BLOCK] + half_gab)
    log_a = half_c_sp * tr + half_c_sp
    ig = 0.5 * jnp.tanh(res[:, COL_BLOCK:2 * COL_BLOCK] + half_gxb) + 0.5
    a = jnp.exp(log_a)
    m2 = jnp.tanh(log_a) * (-1.0 - a * a)
    u = jnp.where(m2 > 0.0, m2 * lax.rsqrt(m2), 0.0) * (ig * xc)
    return a, u


def _rglru_scan(seg, a, u, g, h_read, h_write, h_row0):
    S, TB = seg.S, seg.TB
    groups = S // SUBLANES
    pieces = [None] * (TB * groups)
    for c in range(groups):
        hrows = slice(h_row0 + c * SUBLANES, h_row0 + (c + 1) * SUBLANES)
        h = h_read[hrows, :]
        for t in range(TB):
            r = t * S + c * SUBLANES
            h = a[r:r + SUBLANES] * h + u[r:r + SUBLANES]
            pieces[t * groups + c] = h * g[r:r + SUBLANES]
        h_write[hrows, :] = h
    return jnp.concatenate(pieces, axis=0)


def _rglru_stream_kernel(segs, sample_in, layer, j, x_ref, *refs):
    if sample_in is not None:
        xs_hbm, refs = refs[0], refs[1:]
    n = len(segs)
    conv_in, h_in = refs[0:n], refs[n:2 * n]
    (npre_ref, npost_ref, wxr_ref, wgate_ref, cw_ref, cb_ref, gaw_ref, gxw_ref, gab_ref, gxb_ref,
     lam_ref, wout_ref) = refs[2 * n:2 * n + 12]
    o_ref = refs[2 * n + 12]
    conv_out, h_out = refs[2 * n + 13:3 * n + 13], refs[3 * n + 13:4 * n + 13]
    wxr_b_ref, wgate_b_ref, gw_b_ref, wout_b_ref = refs[4 * n + 13:4 * n + 17]
    xn_scr, xr_scr, y_scr, *in_staging = refs[4 * n + 17:]
    b = pl.program_id(0)

    if sample_in is None:
        read_x = lambda: x_ref[...]
    else:
        S, T = sample_in
        xs_scr, sems = in_staging
        read_x = lambda: jnp.concatenate([x_ref[...], xs_scr[...].reshape(T * S, D_MODEL)], axis=0)

    @pl.when(b == 0)
    def _():
        if sample_in is not None:
            copies = [pltpu.make_async_copy(xs_hbm.at[:, t, :], xs_scr.at[t], sems.at[t]) for t in range(T)]
            for cp in copies:
                cp.start()
            for cp in copies:
                cp.wait()
        xn_scr[...] = _rms(read_x(), npre_ref[layer:layer + 1, :]).astype(BF16)

    w_xr = wxr_ref[...].astype(BF16)
    w_gate = wgate_ref[...].astype(BF16)
    gw = (0.5 * jnp.concatenate([gaw_ref[...], gxw_ref[...]], axis=1)).astype(BF16)
    w_out_rows = wout_ref[...].astype(BF16)
    wxr_b_ref[...] = w_xr
    wgate_b_ref[...] = w_gate
    gw_b_ref[...] = gw
    wout_b_ref[...] = w_out_rows

    rows_of = lambda seg: slice(seg.row0, seg.row0 + seg.rows)
    xn = xn_scr[...]
    xr = _dot(xn, w_xr)
    g = _gelu_tanh(_dot(xn, w_gate))
    xcs = []
    for seg, c_in, c_out in zip(segs, conv_in, conv_out):
        P = (CONV_A - 1) * seg.S
        xr_scr[seg.conv0:seg.conv0 + P, :] = c_in[...]
        xr_scr[seg.conv0 + P:seg.conv0 + P + seg.rows, :] = xr[rows_of(seg)]
        xcs.append(_rglru_conv(seg, cw_ref[...], cb_ref[j:j + 1, :], xr_scr, seg.conv0))
        c_out[...] = xr_scr[seg.conv0:seg.conv0 + P, :]
    xc = jnp.concatenate(xcs, axis=0)
    a, u = _rglru_gate_math(_dot(xc.astype(BF16), gw), xc, 0.5 * gab_ref[pl.ds(b, 1), :],
                            0.5 * gxb_ref[pl.ds(b, 1), :], lam_ref[j:j + 1, :])
    hs = [_rglru_scan(seg, a[rows_of(seg)], u[rows_of(seg)], g[rows_of(seg)], hi, ho, 0)
          for seg, hi, ho in zip(segs, h_in, h_out)]
    part = _dot(jnp.concatenate(hs, axis=0).astype(BF16), w_out_rows)
    _accumulate(y_scr, slice(None), part, b)

    @pl.when(b == N_COL_BLOCKS - 1)
    def _():
        o_ref[...] = read_x() + _rms(y_scr[...], npost_ref[layer:layer + 1, :])


def _rglru_stream_layer(x, conv_in, h_in, layer, j, w, *, seqs_and_steps, x_sample=None):
    D, C = D_MODEL, COL_BLOCK
    segs, R, conv_rows, _ = _segments(seqs_and_steps, CONV_A)
    n_layers, n_a = w["norm_mix_pre"].shape[0], w["rg_conv_b"].shape[0]
    col = lambda rows: pl.BlockSpec((rows, C), lambda b: (0, b))
    if x_sample is None:
        x_args, x_specs, sample_in, staging = [x], [_resident(x.shape)], None, []
    else:
        S, T, _ = x_sample.shape
        x_args, x_specs, sample_in = [x, x_sample], [_resident(x.shape), pl.BlockSpec(memory_space=pl.ANY)], (S, T)
        staging = [pltpu.VMEM((T, S, D), F32), pltpu.SemaphoreType.DMA((T,))]
    in_specs = (
        x_specs + [col(c.shape[0]) for c in conv_in] + [col(h.shape[0]) for h in h_in]
        + [_resident((n_layers, D)), _resident((n_layers, D)),
           pl.BlockSpec((None, D, C), lambda b: (j, 0, N_COL_BLOCKS + b)),
           pl.BlockSpec((None, D, C), lambda b: (j, 0, b)),
           pl.BlockSpec((None, CONV_A, C), lambda b: (j, 0, b)),
           col(n_a),
           pl.BlockSpec((None, None, C, C), lambda b: (j, b, 0, 0)),
           pl.BlockSpec((None, None, C, C), lambda b: (j, b, 0, 0)),
           pl.BlockSpec((None, N_COL_BLOCKS, C), lambda b: (j, 0, 0)),
           pl.BlockSpec((None, N_COL_BLOCKS, C), lambda b: (j, 0, 0)),
           col(n_a),
           pl.BlockSpec((None, C, D), lambda b: (j, b, 0))])
    out_specs = ([pl.BlockSpec((R, D), lambda b: (0, 0))] + [col(c.shape[0]) for c in conv_in]
                 + [col(h.shape[0]) for h in h_in]
                 + [_chunk_major_out(D), _chunk_major_out(D), pl.BlockSpec((None, C, 2 * C), lambda b: (b, 0, 0)),
                    pl.BlockSpec((C, D), lambda b: (b, 0))])
    out_shape = ([jax.ShapeDtypeStruct((R, D), F32)] + [jax.ShapeDtypeStruct(s.shape, F32) for s in conv_in + h_in]
                 + [jax.ShapeDtypeStruct((N_COL_BLOCKS, D, C), BF16), jax.ShapeDtypeStruct((N_COL_BLOCKS, D, C), BF16),
                    jax.ShapeDtypeStruct((N_COL_BLOCKS, C, 2 * C), BF16), jax.ShapeDtypeStruct((D, D), BF16)])
    outs = pl.pallas_call(
        functools.partial(_rglru_stream_kernel, segs, sample_in, layer, j),
        grid=(N_COL_BLOCKS,),
        in_specs=in_specs, out_specs=out_specs, out_shape=out_shape,
        scratch_shapes=[pltpu.VMEM((R, D), BF16),
                        pltpu.VMEM((conv_rows, C), F32),
                        pltpu.VMEM((R, D), F32)]
                       + staging,
        compiler_params=_params(),
        name="rglru_mixer_small",
    )(*x_args, *conv_in, *h_in, w["norm_mix_pre"], w["norm_mix_post"], w["rg_w_in"], w["rg_w_in"], w["rg_conv_w"],
      w["rg_conv_b"], w["rg_gate_a_w"], w["rg_gate_x_w"], w["rg_gate_a_b"], w["rg_gate_x_b"], w["rg_lambda"],
      w["rg_w_out"])
    n = len(segs)
    bf16 = dict(w_xr=outs[2 * n + 1], w_gate=outs[2 * n + 2], gw=outs[2 * n + 3], w_out=outs[2 * n + 4])
    return outs[0], list(outs[1:1 + n]), list(outs[1 + n:1 + 2 * n]), bf16


def _rglru_kernel(seg, nblk, layer, j, batch_major_in,
                  x_ref, conv_in_ref, h_in_ref, npre_ref, npost_ref, wxr_ref, wgate_ref, cw_ref, cb_ref,
                  gw_ref, gab_ref, gxb_ref, lam_ref, wout_ref,
                  o_ref, conv_out_ref, h_out_ref, xn_scr, xr_scr, h_scr, *in_staging):
    S, TB, R = seg.S, seg.TB, seg.rows
    P = (CONV_A - 1) * S
    i = pl.program_id(0)
    slot = i % 2
    blocks = [slice(b * COL_BLOCK, (b + 1) * COL_BLOCK) for b in range(N_COL_BLOCKS)]

    if batch_major_in:
        xbuf, sems = in_staging

        def copies(which, step):
            t0 = pl.multiple_of(step * TB, TB)
            return [pltpu.make_async_copy(x_ref.at[s, pl.ds(t0, TB), :], xbuf.at[which, :, s, :],
                                          sems.at[which, s]) for s in range(S)]

    @pl.when(i == 0)
    def _():
        for b, cols in enumerate(blocks):
            xr_scr[b, 0:P, :] = conv_in_ref[:, cols]
            h_scr[b] = h_in_ref[:, cols]
        if batch_major_in:
            for c in copies(0, 0):
                c.start()

    if batch_major_in:
        @pl.when(i + 1 < nblk)
        def _():
            for c in copies(1 - slot, i + 1):
                c.start()

        for c in copies(slot, i):
            c.wait()
        read_x = lambda: xbuf[slot].reshape(R, D_MODEL)
    else:
        read_x = lambda: x_ref[...]

    xn_scr[...] = _rms(read_x(), npre_ref[layer:layer + 1, :]).astype(BF16)
    xn = xn_scr[...]

    gate_pre, xc, res = {}, {}, {}

    def input_matmuls(b):
        _rglru_branch_in(seg, xn, wxr_ref[b], xr_scr.at[b])
        gate_pre[b] = _dot(xn, wgate_ref[b])

    def conv_and_gate_matmul(b):
        xc[b] = _rglru_conv(seg, cw_ref[:, blocks[b]], cb_ref[j:j + 1, blocks[b]], xr_scr.at[b], 0)
        res[b] = _dot(xc[b].astype(BF16), gw_ref[b])

    y = None
    input_matmuls(0)
    conv_and_gate_matmul(0)
    input_matmuls(1)
    for b, cols in enumerate(blocks):
        if b + 1 < N_COL_BLOCKS:
            conv_and_gate_matmul(b + 1)
        if b + 2 < N_COL_BLOCKS:
            input_matmuls(b + 2)
        g = _gelu_tanh(gate_pre.pop(b))
        a, u = _rglru_gate_math(res.pop(b), xc.pop(b), 0.5 * gab_ref[b:b + 1, :], 0.5 * gxb_ref[b:b + 1, :],
                                lam_ref[j:j + 1, cols])
        hs = _rglru_scan(seg, a, u, g, h_scr.at[b], h_scr.at[b], 0)
        part = _dot(hs.astype(BF16), wout_ref[cols, :])
        y = part if y is None else y + part

    o_ref[...] = read_x() + _rms(y, npost_ref[layer:layer + 1, :])

    @pl.when(i == nblk - 1)
    def _():
        for b, cols in enumerate(blocks):
            conv_out_ref[:, cols] = xr_scr[b, 0:P, :]
            h_out_ref[:, cols] = h_scr[b]


def _rglru_layer(x, conv_in, h_in, layer, j, w, bf16, *, S, TB):
    D = D_MODEL
    batch_major_in = x.ndim == 3
    (seg,), R, conv_rows, _ = _segments(((S, TB),), CONV_A)
    nblk = x.size // D // R
    P = (CONV_A - 1) * S
    n_layers, n_a = w["norm_mix_pre"].shape[0], w["rg_conv_b"].shape[0]
    row_spec = pl.BlockSpec((R, D), lambda i: (i, 0))
    if batch_major_in:
        x_spec = pl.BlockSpec(memory_space=pl.ANY)
        staging = [pltpu.VMEM((2, TB, S, D), F32), pltpu.SemaphoreType.DMA((2, S))]
    else:
        x_spec, staging = row_spec, []
    layer_of = lambda arr: pl.BlockSpec((None,) + arr.shape[1:], lambda i: (j,) + (0,) * (arr.ndim - 1),
                                        pipeline_mode=pl.Buffered(1))
    return pl.pallas_call(
        functools.partial(_rglru_kernel, seg, nblk, layer, j, batch_major_in),
        grid=(nblk,),
        in_specs=[x_spec, _resident((P, D)), _resident((S, D)),
                  _resident((n_layers, D)), _resident((n_layers, D)),
                  _resident(bf16["w_xr"].shape), _resident(bf16["w_gate"].shape),
                  layer_of(w["rg_conv_w"]), _resident((n_a, D)), _resident(bf16["gw"].shape),
                  layer_of(w["rg_gate_a_b"]), layer_of(w["rg_gate_x_b"]), _resident((n_a, D)),
                  _resident((D, D))],
        out_specs=[row_spec, pl.BlockSpec((P, D), lambda i: (0, 0)), pl.BlockSpec((S, D), lambda i: (0, 0))],
        out_shape=[jax.ShapeDtypeStruct((nblk * R, D), F32), jax.ShapeDtypeStruct((P, D), F32),
                   jax.ShapeDtypeStruct((S, D), F32)],
        scratch_shapes=[pltpu.VMEM((R, D), BF16),
                        pltpu.VMEM((N_COL_BLOCKS, conv_rows, COL_BLOCK), F32),
                        pltpu.VMEM((N_COL_BLOCKS, S, COL_BLOCK), F32)]
                       + staging,
        compiler_params=_params(),
        name="rglru_mixer",
    )(x, conv_in, h_in, w["norm_mix_pre"], w["norm_mix_post"], bf16["w_xr"], bf16["w_gate"], w["rg_conv_w"],
      w["rg_conv_b"], bf16["gw"], w["rg_gate_a_b"], w["rg_gate_x_b"], w["rg_lambda"], bf16["w_out"])


def _sconv_conv(seg, cv, cw, region):
    S, R = seg.S, seg.rows
    P = (CONV_B - 1) * S
    c0 = seg.conv0
    region[c0 + P:c0 + P + R, :] = cv
    conv = region[c0:c0 + R, :] * cw[0:1]
    for k in range(1, CONV_B):
        conv = conv + region[c0 + k * S:c0 + k * S + R, :] * cw[k:k + 1]
    region[c0:c0 + P, :] = region[c0 + R:c0 + R + P, :]
    return conv


def _sconv_stream_kernel(segs, layer, j, x_ref, *refs):
    n = len(segs)
    conv_in = refs[0:n]
    npre_ref, npost_ref, wbg_ref, wcg_ref, wv_ref, cw_ref, wout_ref = refs[n:n + 7]
    o_ref = refs[n + 7]
    conv_out = refs[n + 8:2 * n + 8]
    wbg_b_ref, wcg_b_ref, wv_b_ref, wout_b_ref = refs[2 * n + 8:2 * n + 12]
    xn_scr, cv_scr, y_scr = refs[2 * n + 12:]
    b = pl.program_id(0)

    @pl.when(b == 0)
    def _():
        xn_scr[...] = _rms(x_ref[...], npre_ref[layer:layer + 1, :]).astype(BF16)

    w_bg = wbg_ref[...].astype(BF16)
    w_cg = wcg_ref[...].astype(BF16)
    w_v = wv_ref[...].astype(BF16)
    w_out_rows = wout_ref[...].astype(BF16)
    wbg_b_ref[...] = w_bg
    wcg_b_ref[...] = w_cg
    wv_b_ref[...] = w_v
    wout_b_ref[...] = w_out_rows

    xn = xn_scr[...]
    cv = _dot(xn, w_cg) * _dot(xn, w_v)
    convs = []
    for seg, c_in, c_out in zip(segs, conv_in, conv_out):
        P = (CONV_B - 1) * seg.S
        cv_scr[seg.conv0:seg.conv0 + P, :] = c_in[...]
        convs.append(_sconv_conv(seg, cv[seg.row0:seg.row0 + seg.rows], cw_ref[...], cv_scr))
        c_out[...] = cv_scr[seg.conv0:seg.conv0 + P, :]
    m = (_dot(xn, w_bg) * jnp.concatenate(convs, axis=0)).astype(BF16)
    _accumulate(y_scr, slice(None), _dot(m, w_out_rows), b)

    @pl.when(b == N_COL_BLOCKS - 1)
    def _():
        o_ref[...] = x_ref[...] + _rms(y_scr[...], npost_ref[layer:layer + 1, :])


def _sconv_stream_layer(x, conv_in, layer, j, w, *, seqs_and_steps):
    R, D = x.shape
    C = COL_BLOCK
    segs, _, conv_rows, _ = _segments(seqs_and_steps, CONV_B)
    n_layers = w["norm_mix_pre"].shape[0]
    col = lambda rows: pl.BlockSpec((rows, C), lambda b: (0, b))
    w_in_part = lambda k: pl.BlockSpec((None, D, C), lambda b: (j, 0, k * N_COL_BLOCKS + b))
    outs = pl.pallas_call(
        functools.partial(_sconv_stream_kernel, segs, layer, j),
        grid=(N_COL_BLOCKS,),
        in_specs=[_resident((R, D))] + [col(c.shape[0]) for c in conv_in]
                 + [_resident((n_layers, D)), _resident((n_layers, D)), w_in_part(0), w_in_part(1), w_in_part(2),
                    pl.BlockSpec((None, CONV_B, C), lambda b: (j, 0, b)),
                    pl.BlockSpec((None, C, D), lambda b: (j, b, 0))],
        out_specs=[pl.BlockSpec((R, D), lambda b: (0, 0))] + [col(c.shape[0]) for c in conv_in]
                  + [_chunk_major_out(D)] * 3 + [pl.BlockSpec((C, D), lambda b: (b, 0))],
        out_shape=[jax.ShapeDtypeStruct((R, D), F32)] + [jax.ShapeDtypeStruct(c.shape, F32) for c in conv_in]
                  + [jax.ShapeDtypeStruct((N_COL_BLOCKS, D, C), BF16)] * 3 + [jax.ShapeDtypeStruct((D, D), BF16)],
        scratch_shapes=[pltpu.VMEM((R, D), BF16),
                        pltpu.VMEM((conv_rows, C), F32),
                        pltpu.VMEM((R, D), F32)],
        compiler_params=_params(),
        name="sconv_mixer_small",
    )(x, *conv_in, w["norm_mix_pre"], w["norm_mix_post"], w["sc_w_in"], w["sc_w_in"], w["sc_w_in"],
      w["sc_conv_w"], w["sc_w_out"])
    n = len(segs)
    bf16 = dict(w_bg=outs[n + 1], w_cg=outs[n + 2], w_v=outs[n + 3], w_out=outs[n + 4])
    return outs[0], list(outs[1:1 + n]), bf16


def _sconv_kernel(seg, nblk, layer, x_ref, conv_in_ref, npre_ref, npost_ref, wbg_ref, wcg_ref, wv_ref,
                  cw_ref, wout_ref, o_ref, conv_out_ref, xn_scr, cv_scr, m_scr):
    P = (CONV_B - 1) * seg.S
    i = pl.program_id(0)
    blocks = [slice(b * COL_BLOCK, (b + 1) * COL_BLOCK) for b in range(N_COL_BLOCKS)]

    @pl.when(i == 0)
    def _():
        for b, cols in enumerate(blocks):
            cv_scr[b, 0:P, :] = conv_in_ref[:, cols]

    xn_scr[...] = _rms(x_ref[...], npre_ref[layer:layer + 1, :]).astype(BF16)
    xn = xn_scr[...]
    for b, cols in enumerate(blocks):
        cv = _dot(xn, wcg_ref[b]) * _dot(xn, wv_ref[b])
        conv = _sconv_conv(seg, cv, cw_ref[:, cols], cv_scr.at[b])
        m_scr[:, cols] = (_dot(xn, wbg_ref[b]) * conv).astype(BF16)
    y = _dot(m_scr[...], wout_ref[...])
    o_ref[...] = x_ref[...] + _rms(y, npost_ref[layer:layer + 1, :])

    @pl.when(i == nblk - 1)
    def _():
        for b, cols in enumerate(blocks):
            conv_out_ref[:, cols] = cv_scr[b, 0:P, :]


def _sconv_layer(x, conv_in, layer, j, w, bf16, *, S, TB):
    N, D = x.shape
    (seg,), R, conv_rows, _ = _segments(((S, TB),), CONV_B)
    nblk = N // R
    P = (CONV_B - 1) * S
    n_layers = w["norm_mix_pre"].shape[0]
    row_spec = pl.BlockSpec((R, D), lambda i: (i, 0))
    return pl.pallas_call(
        functools.partial(_sconv_kernel, seg, nblk, layer),
        grid=(nblk,),
        in_specs=[row_spec, _resident((P, D)), _resident((n_layers, D)), _resident((n_layers, D)),
                  _resident(bf16["w_bg"].shape), _resident(bf16["w_cg"].shape), _resident(bf16["w_v"].shape),
                  pl.BlockSpec((None, CONV_B, D), lambda i: (j, 0, 0), pipeline_mode=pl.Buffered(1)),
                  _resident((D, D))],
        out_specs=[row_spec, pl.BlockSpec((P, D), lambda i: (0, 0))],
        out_shape=[jax.ShapeDtypeStruct((N, D), F32), jax.ShapeDtypeStruct((P, D), F32)],
        scratch_shapes=[pltpu.VMEM((R, D), BF16),
                        pltpu.VMEM((N_COL_BLOCKS, conv_rows, COL_BLOCK), F32),
                        pltpu.VMEM((R, D), BF16)],
        compiler_params=_params(),
        name="sconv_mixer",
    )(x, conv_in, w["norm_mix_pre"], w["norm_mix_post"], bf16["w_bg"], bf16["w_cg"], bf16["w_v"],
      w["sc_conv_w"], bf16["w_out"])


def _swiglu(g, u):
    return (g * jax.nn.sigmoid(g) * u).astype(BF16)


def _ffn_stream_kernel(layer, nchunk, sample_out, x_ref, npre_ref, npost_ref, wg_hbm, wu_hbm, wd_hbm,
                       o_ref, wg_b_hbm, wu_b_hbm, wd_b_hbm,
                       xn_scr, wg_f, wu_f, wd_f, wg_h, wu_h, wd_h, in_sems, out_sems, *out_staging):
    F = FF_CHUNK

    NB = STREAM_BUFFERS

    def fetch(c):
        k, cols = c % NB, pl.ds(c * F, F)
        return [pltpu.make_async_copy(wg_hbm.at[layer, :, cols], wg_f.at[k], in_sems.at[0, k]),
                pltpu.make_async_copy(wu_hbm.at[layer, :, cols], wu_f.at[k], in_sems.at[1, k]),
                pltpu.make_async_copy(wd_hbm.at[layer, cols, :], wd_f.at[k], in_sems.at[2, k])]

    def emit(c):
        k = c % NB
        return [pltpu.make_async_copy(wg_h.at[k], wg_b_hbm.at[c], out_sems.at[0, k]),
                pltpu.make_async_copy(wu_h.at[k], wu_b_hbm.at[c], out_sems.at[1, k]),
                pltpu.make_async_copy(wd_h.at[k], wd_b_hbm.at[pl.ds(c * F, F), :], out_sems.at[2, k])]

    for c in range(min(NB, nchunk)):
        for cp in fetch(c):
            cp.start()
    xn_scr[...] = _rms(x_ref[...], npre_ref[layer:layer + 1, :]).astype(BF16)
    xn = xn_scr[...]

    y = None
    for c in range(nchunk):
        k = c % NB
        for cp in fetch(c):
            cp.wait()
        if c >= NB:
            for cp in emit(c - NB):
                cp.wait()
        wg, wu, wd = wg_f[k].astype(BF16), wu_f[k].astype(BF16), wd_f[k].astype(BF16)
        wg_h[k], wu_h[k], wd_h[k] = wg, wu, wd
        for cp in emit(c):
            cp.start()
        if c + NB < nchunk:
            for cp in fetch(c + NB):
                cp.start()
        part = _dot(_swiglu(_dot(xn, wg), _dot(xn, wu)), wd)
        y = part if y is None else y + part
    for c in range(max(nchunk - NB, 0), nchunk):
        for cp in emit(c):
            cp.wait()

    out = x_ref[...] + _rms(y, npost_ref[layer:layer + 1, :])
    if sample_out is None:
        o_ref[...] = out
    else:
        row0, S, T = sample_out
        obuf, sems = out_staging
        obuf[...] = out[row0:row0 + T * S].reshape(T, S, out.shape[-1])
        copies = [pltpu.make_async_copy(obuf.at[t], o_ref.at[:, t, :], sems.at[t]) for t in range(T)]
        for cp in copies:
            cp.start()
        for cp in copies:
            cp.wait()


def _ffn_stream_layer(x, layer, w, *, sample_out=None):
    R, D = x.shape
    F, NB = FF_CHUNK, STREAM_BUFFERS
    n_layers = w["norm_ffn_pre"].shape[0]
    nchunk = D_FF // F
    hbm = pl.BlockSpec(memory_space=pl.ANY)
    if sample_out is None:
        o_spec, o_shape, staging = pl.BlockSpec((R, D), lambda i: (0, 0)), jax.ShapeDtypeStruct((R, D), F32), []
    else:
        _, S, T = sample_out
        o_spec, o_shape = hbm, jax.ShapeDtypeStruct((S, T, D), F32)
        staging = [pltpu.VMEM((T, S, D), F32), pltpu.SemaphoreType.DMA((T,))]
    outs = pl.pallas_call(
        functools.partial(_ffn_stream_kernel, layer, nchunk, sample_out),
        grid=(1,),
        in_specs=[_resident((R, D)), _resident((n_layers, D)), _resident((n_layers, D)), hbm, hbm, hbm],
        out_specs=[o_spec, hbm, hbm, hbm],
        out_shape=[o_shape, jax.ShapeDtypeStruct((nchunk, D, F), BF16),
                   jax.ShapeDtypeStruct((nchunk, D, F), BF16), jax.ShapeDtypeStruct((D_FF, D), BF16)],
        scratch_shapes=[pltpu.VMEM((R, D), BF16),
                        pltpu.VMEM((NB, D, F), F32), pltpu.VMEM((NB, D, F), F32), pltpu.VMEM((NB, F, D), F32),
                        pltpu.VMEM((NB, D, F), BF16), pltpu.VMEM((NB, D, F), BF16), pltpu.VMEM((NB, F, D), BF16),
                        pltpu.SemaphoreType.DMA((3, NB)), pltpu.SemaphoreType.DMA((3, NB))] + staging,
        compiler_params=_params(),
        name="swiglu_ffn_small",
    )(x, w["norm_ffn_pre"], w["norm_ffn_post"], w["ffn_w_gate"], w["ffn_w_up"], w["ffn_w_down"])
    return outs[0], dict(wg=outs[1], wu=outs[2], wd=outs[3])


def _ffn_kernel(layer, nblk, batch_major_out, x_ref, npre_ref, npost_ref, wg_ref, wu_ref, wd_ref, o_ref,
                xn_scr, act_scr, *out_staging):
    R = x_ref.shape[0]
    i = pl.program_id(0)
    slot = i % 2
    if batch_major_out:
        obuf, sems = out_staging
        S, TB = o_ref.shape[0], R // o_ref.shape[0]

        def copies(which, step):
            t0 = pl.multiple_of(step * TB, TB)
            return [pltpu.make_async_copy(obuf.at[which, :, s, :],
                                          o_ref.at[s, pl.ds(t0, TB), :], sems.at[which, s]) for s in range(S)]

        @pl.when(i >= 2)
        def _():
            for c in copies(slot, i - 2):
                c.wait()

    halves = [slice(0, R // 2), slice(R // 2, R)]
    for rows in halves:
        xn_scr[rows, :] = _rms(x_ref[rows, :], npre_ref[layer:layer + 1, :]).astype(BF16)
    for rows in halves:
        xn = xn_scr[rows, :]
        for c in range(D_FF // FF_CHUNK):
            cols = slice(c * FF_CHUNK, (c + 1) * FF_CHUNK)
            act_scr[rows, cols] = _swiglu(_dot(xn, wg_ref[c]), _dot(xn, wu_ref[c]))
    ys = [_dot(act_scr[rows, :], wd_ref[...]) for rows in halves]
    for rows, y in zip(halves, ys):
        out = x_ref[rows, :] + _rms(y, npost_ref[layer:layer + 1, :])
        if batch_major_out:
            obuf[slot, rows.start // S:rows.stop // S] = out.reshape(-1, S, out.shape[-1])
        else:
            o_ref[rows, :] = out

    if batch_major_out:
        for c in copies(slot, i):
            c.start()

        @pl.when(i == nblk - 1)
        def _():
            if nblk >= 2:
                for c in copies(1 - slot, i - 1):
                    c.wait()
            for c in copies(slot, i):
                c.wait()


def _ffn_layer(x, layer, w, bf16, *, R, batch_major_seqs=None):
    N, D = x.shape
    n_layers = w["norm_ffn_pre"].shape[0]
    nblk = N // R
    row_spec = pl.BlockSpec((R, D), lambda i: (i, 0))
    if batch_major_seqs is None:
        out_spec, out_shape, staging = row_spec, jax.ShapeDtypeStruct((N, D), F32), []
    else:
        S = batch_major_seqs
        out_spec = pl.BlockSpec(memory_space=pl.ANY)
        out_shape = jax.ShapeDtypeStruct((S, N // S, D), F32)
        staging = [pltpu.VMEM((2, R // S, S, D), F32), pltpu.SemaphoreType.DMA((2, S))]
    return pl.pallas_call(
        functools.partial(_ffn_kernel, layer, nblk, batch_major_seqs is not None),
        grid=(nblk,),
        in_specs=[row_spec, _resident((n_layers, D)), _resident((n_layers, D)),
                  _resident(bf16["wg"].shape), _resident(bf16["wu"].shape), _resident((D_FF, D))],
        out_specs=out_spec,
        out_shape=out_shape,
        scratch_shapes=[pltpu.VMEM((R, D), BF16), pltpu.VMEM((R, D_FF), BF16)] + staging,
        compiler_params=_params(),
        name="swiglu_ffn",
    )(x, w["norm_ffn_pre"], w["norm_ffn_post"], bf16["wg"], bf16["wu"], bf16["wd"])


def _to_time_major(a):
    S, K, D = a.shape
    return jnp.swapaxes(a, 0, 1).reshape(K * S, D)


def _from_time_major(a, S):
    KS, D = a.shape
    return jnp.swapaxes(a.reshape(KS // S, S, D), 0, 1)


def kernel(x_prompt, x_sample, state_rglru_conv, state_rglru_h, state_sconv, meta_tokens, norm_mix_pre, norm_mix_post, norm_ffn_pre, norm_ffn_post, rg_w_in, rg_conv_w, rg_conv_b, rg_gate_a_w, rg_gate_a_b, rg_gate_x_w, rg_gate_x_b, rg_lambda, rg_w_out, sc_w_in, sc_conv_w, sc_w_out, ffn_w_gate, ffn_w_up, ffn_w_down):
    D = D_MODEL
    depth = norm_mix_pre.shape[0]
    batch, seq, _ = x_prompt.shape
    dec_batch, dec_seq, _ = x_sample.shape
    w = dict(norm_mix_pre=norm_mix_pre, norm_mix_post=norm_mix_post, norm_ffn_pre=norm_ffn_pre,
             norm_ffn_post=norm_ffn_post, rg_w_in=rg_w_in, rg_conv_w=rg_conv_w, rg_conv_b=rg_conv_b,
             rg_gate_a_w=rg_gate_a_w, rg_gate_a_b=rg_gate_a_b, rg_gate_x_w=rg_gate_x_w, rg_gate_x_b=rg_gate_x_b,
             rg_lambda=rg_lambda, rg_w_out=rg_w_out, sc_w_in=sc_w_in, sc_conv_w=sc_conv_w, sc_w_out=sc_w_out,
             ffn_w_gate=ffn_w_gate, ffn_w_up=ffn_w_up, ffn_w_down=ffn_w_down)

    x = jnp.broadcast_to(meta_tokens[:, None, :], (N_META, batch, D)).reshape(N_META * batch, D)
    small = ((batch, N_META), (dec_batch, dec_seq))
    mixer_bf16, ffn_bf16 = [], []
    rg_conv_s, rg_h_s, sc_s = [], [], []
    for i in range(depth):
        j = i // 2
        if i % 2 == 0:
            conv0 = [jnp.zeros(((CONV_A - 1) * batch, D), F32), _to_time_major(state_rglru_conv[j])]
            h0 = [jnp.zeros((batch, D), F32), state_rglru_h[j]]
            x, cb, hT, wb = _rglru_stream_layer(x, conv0, h0, i, j, w, seqs_and_steps=small,
                                                x_sample=x_sample if i == 0 else None)
            rg_conv_s.append(cb)
            rg_h_s.append(hT)
        else:
            conv0 = [jnp.zeros(((CONV_B - 1) * batch, D), F32), _to_time_major(state_sconv[j])]
            x, cb, wb = _sconv_stream_layer(x, conv0, i, j, w, seqs_and_steps=small)
            sc_s.append(cb)
        mixer_bf16.append(wb)
        sample_out = (N_META * batch, dec_batch, dec_seq) if i == depth - 1 else None
        x, wb = _ffn_stream_layer(x, i, w, sample_out=sample_out)
        ffn_bf16.append(wb)
    y_sample = x

    x = x_prompt
    rg_conv_p, rg_h_p, sc_p = [], [], []
    for i in range(depth):
        j = i // 2
        if i % 2 == 0:
            x, cb, hT = _rglru_layer(x, rg_conv_s[j][0], rg_h_s[j][0], i, j, w, mixer_bf16[i], S=batch, TB=PROMPT_TB)
            rg_conv_p.append(cb)
            rg_h_p.append(hT)
        else:
            x, cb = _sconv_layer(x, sc_s[j][0], i, j, w, mixer_bf16[i], S=batch, TB=PROMPT_TB)
            sc_p.append(cb)
        x = _ffn_layer(x, i, w, ffn_bf16[i], R=batch * PROMPT_TB,
                       batch_major_seqs=batch if i == depth - 1 else None)
    y_prompt = x

    return (y_prompt, y_sample,
            jnp.stack([_from_time_major(c, batch) for c in rg_conv_p]), jnp.stack(rg_h_p),
            jnp.stack([_from_time_major(c, batch) for c in sc_p]),
            jnp.stack([_from_time_major(c[1], dec_batch) for c in rg_conv_s]), jnp.stack([h[1] for h in rg_h_s]),
            jnp.stack([_from_time_major(c[1], dec_batch) for c in sc_s]))
```

```python
import functools
from typing import NamedTuple

import jax
import jax.numpy as jnp
from jax import lax
from jax.experimental import pallas as pl
from jax.experimental.pallas import tpu as pltpu

D_MODEL = 1024
D_FF = 2816
N_META = 16
COL_BLOCK = 256
N_COL_BLOCKS = D_MODEL // COL_BLOCK
CONV_A = 4
CONV_B = 3
RG_C = 8.0
EPS = 1e-6

SUBLANES = 8
FF_CHUNK = 256
STREAM_BUFFERS = 3
PROMPT_TB = 128
VMEM_LIMIT_BYTES = 56 * 1024 * 1024

F32 = jnp.float32
BF16 = jnp.bfloat16


class _Seg(NamedTuple):
    S: int
    TB: int
    row0: int
    conv0: int
    h0: int

    @property
    def rows(self):
        return self.S * self.TB


def _segments(seqs_and_steps, taps):
    segs, row0, conv0, h0 = [], 0, 0, 0
    for S, TB in seqs_and_steps:
        segs.append(_Seg(S, TB, row0, conv0, h0))
        row0 += S * TB
        conv0 += (taps - 1) * S + S * TB
        h0 += S
    return tuple(segs), row0, conv0, h0


def _rms(x, w):
    ms = jnp.mean(x * x, axis=-1, keepdims=True)
    return x * lax.rsqrt(ms + EPS) * w


def _dot(a, b):
    return jnp.dot(a, b, preferred_element_type=F32)


def _gelu_tanh(x):
    c = 0.7978845608028654
    hx = 0.5 * x
    return hx + hx * jnp.tanh(x * (c + (c * 0.044715) * (x * x)))


def _resident(shape):
    zeros = (0,) * len(shape)
    return pl.BlockSpec(shape, lambda i: zeros, pipeline_mode=pl.Buffered(1))


def _chunk_major_out(rows, width=COL_BLOCK):
    return pl.BlockSpec((None, rows, width), lambda i: (i, 0, 0))


def _params():
    return pltpu.CompilerParams(dimension_semantics=("arbitrary",), vmem_limit_bytes=VMEM_LIMIT_BYTES)


def _accumulate(y_scr, rows, part, step):
    @pl.when(step == 0)
    def _():
        y_scr[rows, :] = part

    @pl.when(step > 0)
    def _():
        y_scr[rows, :] += part


def _rglru_branch_in(seg, xn, w_xr, region):
    P = (CONV_A - 1) * seg.S
    region[seg.conv0 + P:seg.conv0 + P + seg.rows, :] = _dot(xn, w_xr)


def _rglru_conv(seg, cw, cb, region, carry_to):
    S, R = seg.S, seg.rows
    P = (CONV_A - 1) * S
    c0 = seg.conv0
    xc = region[c0:c0 + R, :] * cw[0:1]
    for k in range(1, CONV_A):
        xc = xc + region[c0 + k * S:c0 + k * S + R, :] * cw[k:k + 1]
    xc = xc + cb
    if carry_to is not None:
        region[carry_to:carry_to + P, :] = region[c0 + R:c0 + R + P, :]
    return xc


def _rglru_gate_math(res, xc, half_gab, half_gxb, lam):
    half_c_sp = (-0.5 * RG_C) * jax.nn.softplus(-lam)
    tr = jnp.tanh(res[:, 0:COL_BLOCK] + half_gab)
    log_a = half_c_sp * tr + half_c_sp
    ig = 0.5 * jnp.tanh(res[:, COL_BLOCK:2 * COL_BLOCK] + half_gxb) + 0.5
    a = jnp.exp(log_a)
    m2 = jnp.tanh(log_a) * (-1.0 - a * a)
    u = jnp.where(m2 > 0.0, m2 * lax.rsqrt(m2), 0.0) * (ig * xc)
    return a, u


def _rglru_scan(seg, a, u, g, h_read, h_write, h_row0):
    S, TB = seg.S, seg.TB
    groups = S // SUBLANES
    pieces = [None] * (TB * groups)
    for c in range(groups):
        hrows = slice(h_row0 + c * SUBLANES, h_row0 + (c + 1) * SUBLANES)
        h = h_read[hrows, :]
        for t in range(TB):
            r = t * S + c * SUBLANES
            h = a[r:r + SUBLANES] * h + u[r:r + SUBLANES]
            pieces[t * groups + c] = h * g[r:r + SUBLANES]
        h_write[hrows, :] = h
    return jnp.concatenate(pieces, axis=0)


def _rglru_stream_kernel(segs, sample_in, layer, j, x_ref, *refs):
    if sample_in is not None:
        xs_hbm, refs = refs[0], refs[1:]
    n = len(segs)
    conv_in, h_in = refs[0:n], refs[n:2 * n]
    (npre_ref, npost_ref, wxr_ref, wgate_ref, cw_ref, cb_ref, gaw_ref, gxw_ref, gab_ref, gxb_ref,
     lam_ref, wout_ref) = refs[2 * n:2 * n + 12]
    o_ref = refs[2 * n + 12]
    conv_out, h_out = refs[2 * n + 13:3 * n + 13], refs[3 * n + 13:4 * n + 13]
    wxr_b_ref, wgate_b_ref, gw_b_ref, wout_b_ref = refs[4 * n + 13:4 * n + 17]
    xn_scr, xr_scr, y_scr, *in_staging = refs[4 * n + 17:]
    b = pl.program_id(0)

    if sample_in is None:
        read_x = lambda: x_ref[...]
    else:
        S, T = sample_in
        xs_scr, sems = in_staging
        read_x = lambda: jnp.concatenate([x_ref[...], xs_scr[...].reshape(T * S, D_MODEL)], axis=0)

    @pl.when(b == 0)
    def _():
        if sample_in is not None:
            copies = [pltpu.make_async_copy(xs_hbm.at[:, t, :], xs_scr.at[t], sems.at[t]) for t in range(T)]
            for cp in copies:
                cp.start()
            for cp in copies:
                cp.wait()
        xn_scr[...] = _rms(read_x(), npre_ref[layer:layer + 1, :]).astype(BF16)

    w_xr = wxr_ref[...].astype(BF16)
    w_gate = wgate_ref[...].astype(BF16)
    gw = (0.5 * jnp.concatenate([gaw_ref[...], gxw_ref[...]], axis=1)).astype(BF16)
    w_out_rows = wout_ref[...].astype(BF16)
    wxr_b_ref[...] = w_xr
    wgate_b_ref[...] = w_gate
    gw_b_ref[...] = gw
    wout_b_ref[...] = w_out_rows

    rows_of = lambda seg: slice(seg.row0, seg.row0 + seg.rows)
    xn = xn_scr[...]
    xr = _dot(xn, w_xr)
    g = _gelu_tanh(_dot(xn, w_gate))
    xcs = []
    for seg, c_in, c_out in zip(segs, conv_in, conv_out):
        P = (CONV_A - 1) * seg.S
        xr_scr[seg.conv0:seg.conv0 + P, :] = c_in[...]
        xr_scr[seg.conv0 + P:seg.conv0 + P + seg.rows, :] = xr[rows_of(seg)]
        xcs.append(_rglru_conv(seg, cw_ref[...], cb_ref[j:j + 1, :], xr_scr, seg.conv0))
        c_out[...] = xr_scr[seg.conv0:seg.conv0 + P, :]
    xc = jnp.concatenate(xcs, axis=0)
    a, u = _rglru_gate_math(_dot(xc.astype(BF16), gw), xc, 0.5 * gab_ref[pl.ds(b, 1), :],
                            0.5 * gxb_ref[pl.ds(b, 1), :], lam_ref[j:j + 1, :])
    hs = [_rglru_scan(seg, a[rows_of(seg)], u[rows_of(seg)], g[rows_of(seg)], hi, ho, 0)
          for seg, hi, ho in zip(segs, h_in, h_out)]
    part = _dot(jnp.concatenate(hs, axis=0).astype(BF16), w_out_rows)
    _accumulate(y_scr, slice(None), part, b)

    @pl.when(b == N_COL_BLOCKS - 1)
    def _():
        o_ref[...] = read_x() + _rms(y_scr[...], npost_ref[layer:layer + 1, :])


def _rglru_stream_layer(x, conv_in, h_in, layer, j, w, *, seqs_and_steps, x_sample=None):
    D, C = D_MODEL, COL_BLOCK
    segs, R, conv_rows, _ = _segments(seqs_and_steps, CONV_A)
    n_layers, n_a = w["norm_mix_pre"].shape[0], w["rg_conv_b"].shape[0]
    col = lambda rows: pl.BlockSpec((rows, C), lambda b: (0, b))
    if x_sample is None:
        x_args, x_specs, sample_in, staging = [x], [_resident(x.shape)], None, []
    else:
        S, T, _ = x_sample.shape
        x_args, x_specs, sample_in = [x, x_sample], [_resident(x.shape), pl.BlockSpec(memory_space=pl.ANY)], (S, T)
        staging = [pltpu.VMEM((T, S, D), F32), pltpu.SemaphoreType.DMA((T,))]
    in_specs = (
        x_specs + [col(c.shape[0]) for c in conv_in] + [col(h.shape[0]) for h in h_in]
        + [_resident((n_layers, D)), _resident((n_layers, D)),
           pl.BlockSpec((None, D, C), lambda b: (j, 0, N_COL_BLOCKS + b)),
           pl.BlockSpec((None, D, C), lambda b: (j, 0, b)),
           pl.BlockSpec((None, CONV_A, C), lambda b: (j, 0, b)),
           col(n_a),
           pl.BlockSpec((None, None, C, C), lambda b: (j, b, 0, 0)),
           pl.BlockSpec((None, None, C, C), lambda b: (j, b, 0, 0)),
           pl.BlockSpec((None, N_COL_BLOCKS, C), lambda b: (j, 0, 0)),
           pl.BlockSpec((None, N_COL_BLOCKS, C), lambda b: (j, 0, 0)),
           col(n_a),
           pl.BlockSpec((None, C, D), lambda b: (j, b, 0))])
    out_specs = ([pl.BlockSpec((R, D), lambda b: (0, 0))] + [col(c.shape[0]) for c in conv_in]
                 + [col(h.shape[0]) for h in h_in]
                 + [_chunk_major_out(D), _chunk_major_out(D), pl.BlockSpec((None, C, 2 * C), lambda b: (b, 0, 0)),
                    pl.BlockSpec((C, D), lambda b: (b, 0))])
    out_shape = ([jax.ShapeDtypeStruct((R, D), F32)] + [jax.ShapeDtypeStruct(s.shape, F32) for s in conv_in + h_in]
                 + [jax.ShapeDtypeStruct((N_COL_BLOCKS, D, C), BF16), jax.ShapeDtypeStruct((N_COL_BLOCKS, D, C), BF16),
                    jax.ShapeDtypeStruct((N_COL_BLOCKS, C, 2 * C), BF16), jax.ShapeDtypeStruct((D, D), BF16)])
    outs = pl.pallas_call(
        functools.partial(_rglru_stream_kernel, segs, sample_in, layer, j),
        grid=(N_COL_BLOCKS,),
        in_specs=in_specs, out_specs=out_specs, out_shape=out_shape,
        scratch_shapes=[pltpu.VMEM((R, D), BF16),
                        pltpu.VMEM((conv_rows, C), F32),
                        pltpu.VMEM((R, D), F32)]
                       + staging,
        compiler_params=_params(),
        name="rglru_mixer_small",
    )(*x_args, *conv_in, *h_in, w["norm_mix_pre"], w["norm_mix_post"], w["rg_w_in"], w["rg_w_in"], w["rg_conv_w"],
      w["rg_conv_b"], w["rg_gate_a_w"], w["rg_gate_x_w"], w["rg_gate_a_b"], w["rg_gate_x_b"], w["rg_lambda"],
      w["rg_w_out"])
    n = len(segs)
    bf16 = dict(w_xr=outs[2 * n + 1], w_gate=outs[2 * n + 2], gw=outs[2 * n + 3], w_out=outs[2 * n + 4])
    return outs[0], list(outs[1:1 + n]), list(outs[1 + n:1 + 2 * n]), bf16


def _start_all(copies):
    for cp in copies:
        cp.start()


def _wait_all(copies):
    for cp in copies:
        cp.wait()


def _rglru_small_kernel(segs, sample_in, layer, j, x_ref, *refs):
    D, C, NB = D_MODEL, COL_BLOCK, N_COL_BLOCKS
    if sample_in is not None:
        xs_hbm, refs = refs[0], refs[1:]
    n = len(segs)
    conv_in, h_in = refs[0:n], refs[n:2 * n]
    (npre_ref, npost_ref, cw_ref, cb_ref, gab_ref, gxb_ref, lam_ref,
     win_hbm, gaw_hbm, gxw_hbm, wout_hbm) = refs[2 * n:2 * n + 11]
    o_ref = refs[2 * n + 11]
    conv_out, h_out = refs[2 * n + 12:3 * n + 12], refs[3 * n + 12:4 * n + 12]
    wxr_b_hbm, wgate_b_hbm, gw_b_hbm, wout_b_hbm = refs[4 * n + 12:4 * n + 16]
    (xn_scr, xr_scr, col_f, sq_f, row_f, col_h, gw_h, row_h, in_sems, out_sems,
     *in_staging) = refs[4 * n + 16:]

    def fetch(b):
        cols = pl.ds(b * C, C)
        return [pltpu.make_async_copy(win_hbm.at[j, :, pl.ds(D + b * C, C)], col_f.at[b, 0], in_sems.at[b, 0]),
                pltpu.make_async_copy(win_hbm.at[j, :, cols], col_f.at[b, 1], in_sems.at[b, 1]),
                pltpu.make_async_copy(gaw_hbm.at[j, b], sq_f.at[b, 0], in_sems.at[b, 2]),
                pltpu.make_async_copy(gxw_hbm.at[j, b], sq_f.at[b, 1], in_sems.at[b, 3]),
                pltpu.make_async_copy(wout_hbm.at[j, cols, :], row_f.at[b], in_sems.at[b, 4])]

    def emit(b):
        return [pltpu.make_async_copy(col_h.at[b, 0], wxr_b_hbm.at[b], out_sems.at[b, 0]),
                pltpu.make_async_copy(col_h.at[b, 1], wgate_b_hbm.at[b], out_sems.at[b, 1]),
                pltpu.make_async_copy(gw_h.at[b], gw_b_hbm.at[b], out_sems.at[b, 2]),
                pltpu.make_async_copy(row_h.at[b], wout_b_hbm.at[pl.ds(b * C, C), :], out_sems.at[b, 3])]

    for b in range(NB):
        _start_all(fetch(b))

    if sample_in is None:
        read_x = lambda: x_ref[...]
    else:
        S, T = sample_in
        xs_scr, sems = in_staging
        copies = [pltpu.make_async_copy(xs_hbm.at[:, t, :], xs_scr.at[t], sems.at[t]) for t in range(T)]
        _start_all(copies)
        _wait_all(copies)
        read_x = lambda: jnp.concatenate([x_ref[...], xs_scr[...].reshape(T * S, D)], axis=0)
    xn_scr[...] = _rms(read_x(), npre_ref[layer:layer + 1, :]).astype(BF16)
    xn = xn_scr[...]

    rows_of = lambda seg: slice(seg.row0, seg.row0 + seg.rows)
    y = None
    for b in range(NB):
        cols = slice(b * C, (b + 1) * C)
        _wait_all(fetch(b))
        w_xr, w_gate = col_f[b, 0].astype(BF16), col_f[b, 1].astype(BF16)
        gw = (0.5 * jnp.concatenate([sq_f[b, 0], sq_f[b, 1]], axis=1)).astype(BF16)
        w_out_rows = row_f[b].astype(BF16)
        col_h[b, 0], col_h[b, 1], gw_h[b], row_h[b] = w_xr, w_gate, gw, w_out_rows
        _start_all(emit(b))

        xr = _dot(xn, w_xr)
        g = _gelu_tanh(_dot(xn, w_gate))
        xcs = []
        for seg, c_in, c_out in zip(segs, conv_in, conv_out):
            P = (CONV_A - 1) * seg.S
            xr_scr[seg.conv0:seg.conv0 + P, :] = c_in[:, cols]
            xr_scr[seg.conv0 + P:seg.conv0 + P + seg.rows, :] = xr[rows_of(seg)]
            xcs.append(_rglru_conv(seg, cw_ref[:, cols], cb_ref[j:j + 1, cols], xr_scr, seg.conv0))
            c_out[:, cols] = xr_scr[seg.conv0:seg.conv0 + P, :]
        xc = jnp.concatenate(xcs, axis=0)
        a, u = _rglru_gate_math(_dot(xc.astype(BF16), gw), xc, 0.5 * gab_ref[b:b + 1, :],
                                0.5 * gxb_ref[b:b + 1, :], lam_ref[j:j + 1, cols])
        hs = [_rglru_scan(seg, a[rows_of(seg)], u[rows_of(seg)], g[rows_of(seg)],
                          hi.at[:, cols], ho.at[:, cols], 0)
              for seg, hi, ho in zip(segs, h_in, h_out)]
        part = _dot(jnp.concatenate(hs, axis=0).astype(BF16), w_out_rows)
        y = part if y is None else y + part

    for b in range(NB):
        _wait_all(emit(b))
    o_ref[...] = read_x() + _rms(y, npost_ref[layer:layer + 1, :])


def _rglru_small_layer(x, conv_in, h_in, layer, j, w, *, seqs_and_steps, x_sample=None):
    D, C, NB = D_MODEL, COL_BLOCK, N_COL_BLOCKS
    segs, R, conv_rows, _ = _segments(seqs_and_steps, CONV_A)
    n_layers, n_a = w["norm_mix_pre"].shape[0], w["rg_conv_b"].shape[0]
    hbm = pl.BlockSpec(memory_space=pl.ANY)
    layer_of = lambda arr: pl.BlockSpec((None,) + arr.shape[1:], lambda i: (j,) + (0,) * (arr.ndim - 1),
                                        pipeline_mode=pl.Buffered(1))
    if x_sample is None:
        x_args, x_specs, sample_in, staging = [x], [_resident(x.shape)], None, []
    else:
        S, T, _ = x_sample.shape
        x_args, x_specs, sample_in = [x, x_sample], [_resident(x.shape), hbm], (S, T)
        staging = [pltpu.VMEM((T, S, D), F32), pltpu.SemaphoreType.DMA((T,))]
    states = conv_in + h_in
    outs = pl.pallas_call(
        functools.partial(_rglru_small_kernel, segs, sample_in, layer, j),
        grid=(1,),
        in_specs=x_specs + [_resident(s.shape) for s in states]
                 + [_resident((n_layers, D)), _resident((n_layers, D)), layer_of(w["rg_conv_w"]),
                    _resident((n_a, D)), layer_of(w["rg_gate_a_b"]), layer_of(w["rg_gate_x_b"]),
                    _resident((n_a, D)), hbm, hbm, hbm, hbm],
        out_specs=[pl.BlockSpec((R, D), lambda i: (0, 0))]
                  + [pl.BlockSpec(s.shape, lambda i: (0, 0)) for s in states] + [hbm] * 4,
        out_shape=[jax.ShapeDtypeStruct((R, D), F32)] + [jax.ShapeDtypeStruct(s.shape, F32) for s in states]
                  + [jax.ShapeDtypeStruct((NB, D, C), BF16), jax.ShapeDtypeStruct((NB, D, C), BF16),
                     jax.ShapeDtypeStruct((NB, C, 2 * C), BF16), jax.ShapeDtypeStruct((D, D), BF16)],
        scratch_shapes=[pltpu.VMEM((R, D), BF16),
                        pltpu.VMEM((conv_rows, C), F32),
                        pltpu.VMEM((NB, 2, D, C), F32),
                        pltpu.VMEM((NB, 2, C, C), F32),
                        pltpu.VMEM((NB, C, D), F32),
                        pltpu.VMEM((NB, 2, D, C), BF16), pltpu.VMEM((NB, C, 2 * C), BF16),
                        pltpu.VMEM((NB, C, D), BF16),
                        pltpu.SemaphoreType.DMA((NB, 5)), pltpu.SemaphoreType.DMA((NB, 4))] + staging,
        compiler_params=_params(),
        name="rglru_mixer_small",
    )(*x_args, *states, w["norm_mix_pre"], w["norm_mix_post"], w["rg_conv_w"], w["rg_conv_b"],
      w["rg_gate_a_b"], w["rg_gate_x_b"], w["rg_lambda"], w["rg_w_in"], w["rg_gate_a_w"], w["rg_gate_x_w"],
      w["rg_w_out"])
    n = len(segs)
    bf16 = dict(w_xr=outs[2 * n + 1], w_gate=outs[2 * n + 2], gw=outs[2 * n + 3], w_out=outs[2 * n + 4])
    return outs[0], list(outs[1:1 + n]), list(outs[1 + n:1 + 2 * n]), bf16


def _rglru_kernel(seg, nblk, layer, j, batch_major_in,
                  x_ref, conv_in_ref, h_in_ref, npre_ref, npost_ref, wxr_ref, wgate_ref, cw_ref, cb_ref,
                  gw_ref, gab_ref, gxb_ref, lam_ref, wout_ref,
                  o_ref, conv_out_ref, h_out_ref, xn_scr, xr_scr, h_scr, *in_staging):
    S, TB, R = seg.S, seg.TB, seg.rows
    P = (CONV_A - 1) * S
    i = pl.program_id(0)
    slot = i % 2
    blocks = [slice(b * COL_BLOCK, (b + 1) * COL_BLOCK) for b in range(N_COL_BLOCKS)]

    if batch_major_in:
        xbuf, sems = in_staging

        def copies(which, step):
            t0 = pl.multiple_of(step * TB, TB)
            return [pltpu.make_async_copy(x_ref.at[s, pl.ds(t0, TB), :], xbuf.at[which, :, s, :],
                                          sems.at[which, s]) for s in range(S)]

    @pl.when(i == 0)
    def _():
        for b, cols in enumerate(blocks):
            xr_scr[b, 0:P, :] = conv_in_ref[:, cols]
            h_scr[b] = h_in_ref[:, cols]
        if batch_major_in:
            for c in copies(0, 0):
                c.start()

    if batch_major_in:
        @pl.when(i + 1 < nblk)
        def _():
            for c in copies(1 - slot, i + 1):
                c.start()

        for c in copies(slot, i):
            c.wait()
        read_x = lambda: xbuf[slot].reshape(R, D_MODEL)
    else:
        read_x = lambda: x_ref[...]

    xn_scr[...] = _rms(read_x(), npre_ref[layer:layer + 1, :]).astype(BF16)
    xn = xn_scr[...]

    gate_pre, xc, res = {}, {}, {}

    def input_matmuls(b):
        _rglru_branch_in(seg, xn, wxr_ref[b], xr_scr.at[b])
        gate_pre[b] = _dot(xn, wgate_ref[b])

    def conv_and_gate_matmul(b):
        xc[b] = _rglru_conv(seg, cw_ref[:, blocks[b]], cb_ref[j:j + 1, blocks[b]], xr_scr.at[b], 0)
        res[b] = _dot(xc[b].astype(BF16), gw_ref[b])

    y = None
    input_matmuls(0)
    conv_and_gate_matmul(0)
    input_matmuls(1)
    for b, cols in enumerate(blocks):
        if b + 1 < N_COL_BLOCKS:
            conv_and_gate_matmul(b + 1)
        if b + 2 < N_COL_BLOCKS:
            input_matmuls(b + 2)
        g = _gelu_tanh(gate_pre.pop(b))
        a, u = _rglru_gate_math(res.pop(b), xc.pop(b), 0.5 * gab_ref[b:b + 1, :], 0.5 * gxb_ref[b:b + 1, :],
                                lam_ref[j:j + 1, cols])
        hs = _rglru_scan(seg, a, u, g, h_scr.at[b], h_scr.at[b], 0)
        part = _dot(hs.astype(BF16), wout_ref[cols, :])
        y = part if y is None else y + part

    o_ref[...] = read_x() + _rms(y, npost_ref[layer:layer + 1, :])

    @pl.when(i == nblk - 1)
    def _():
        for b, cols in enumerate(blocks):
            conv_out_ref[:, cols] = xr_scr[b, 0:P, :]
            h_out_ref[:, cols] = h_scr[b]


def _rglru_layer(x, conv_in, h_in, layer, j, w, bf16, *, S, TB):
    D = D_MODEL
    batch_major_in = x.ndim == 3
    (seg,), R, conv_rows, _ = _segments(((S, TB),), CONV_A)
    nblk = x.size // D // R
    P = (CONV_A - 1) * S
    n_layers, n_a = w["norm_mix_pre"].shape[0], w["rg_conv_b"].shape[0]
    row_spec = pl.BlockSpec((R, D), lambda i: (i, 0))
    if batch_major_in:
        x_spec = pl.BlockSpec(memory_space=pl.ANY)
        staging = [pltpu.VMEM((2, TB, S, D), F32), pltpu.SemaphoreType.DMA((2, S))]
    else:
        x_spec, staging = row_spec, []
    layer_of = lambda arr: pl.BlockSpec((None,) + arr.shape[1:], lambda i: (j,) + (0,) * (arr.ndim - 1),
                                        pipeline_mode=pl.Buffered(1))
    return pl.pallas_call(
        functools.partial(_rglru_kernel, seg, nblk, layer, j, batch_major_in),
        grid=(nblk,),
        in_specs=[x_spec, _resident((P, D)), _resident((S, D)),
                  _resident((n_layers, D)), _resident((n_layers, D)),
                  _resident(bf16["w_xr"].shape), _resident(bf16["w_gate"].shape),
                  layer_of(w["rg_conv_w"]), _resident((n_a, D)), _resident(bf16["gw"].shape),
                  layer_of(w["rg_gate_a_b"]), layer_of(w["rg_gate_x_b"]), _resident((n_a, D)),
                  _resident((D, D))],
        out_specs=[row_spec, pl.BlockSpec((P, D), lambda i: (0, 0)), pl.BlockSpec((S, D), lambda i: (0, 0))],
        out_shape=[jax.ShapeDtypeStruct((nblk * R, D), F32), jax.ShapeDtypeStruct((P, D), F32),
                   jax.ShapeDtypeStruct((S, D), F32)],
        scratch_shapes=[pltpu.VMEM((R, D), BF16),
                        pltpu.VMEM((N_COL_BLOCKS, conv_rows, COL_BLOCK), F32),
                        pltpu.VMEM((N_COL_BLOCKS, S, COL_BLOCK), F32)]
                       + staging,
        compiler_params=_params(),
        name="rglru_mixer",
    )(x, conv_in, h_in, w["norm_mix_pre"], w["norm_mix_post"], bf16["w_xr"], bf16["w_gate"], w["rg_conv_w"],
      w["rg_conv_b"], bf16["gw"], w["rg_gate_a_b"], w["rg_gate_x_b"], w["rg_lambda"], bf16["w_out"])


def _sconv_conv(seg, cv, cw, region):
    S, R = seg.S, seg.rows
    P = (CONV_B - 1) * S
    c0 = seg.conv0
    region[c0 + P:c0 + P + R, :] = cv
    conv = region[c0:c0 + R, :] * cw[0:1]
    for k in range(1, CONV_B):
        conv = conv + region[c0 + k * S:c0 + k * S + R, :] * cw[k:k + 1]
    region[c0:c0 + P, :] = region[c0 + R:c0 + R + P, :]
    return conv


def _sconv_stream_kernel(segs, layer, j, x_ref, *refs):
    n = len(segs)
    conv_in = refs[0:n]
    npre_ref, npost_ref, wbg_ref, wcg_ref, wv_ref, cw_ref, wout_ref = refs[n:n + 7]
    o_ref = refs[n + 7]
    conv_out = refs[n + 8:2 * n + 8]
    wbg_b_ref, wcg_b_ref, wv_b_ref, wout_b_ref = refs[2 * n + 8:2 * n + 12]
    xn_scr, cv_scr, y_scr = refs[2 * n + 12:]
    b = pl.program_id(0)

    @pl.when(b == 0)
    def _():
        xn_scr[...] = _rms(x_ref[...], npre_ref[layer:layer + 1, :]).astype(BF16)

    w_bg = wbg_ref[...].astype(BF16)
    w_cg = wcg_ref[...].astype(BF16)
    w_v = wv_ref[...].astype(BF16)
    w_out_rows = wout_ref[...].astype(BF16)
    wbg_b_ref[...] = w_bg
    wcg_b_ref[...] = w_cg
    wv_b_ref[...] = w_v
    wout_b_ref[...] = w_out_rows

    xn = xn_scr[...]
    cv = _dot(xn, w_cg) * _dot(xn, w_v)
    convs = []
    for seg, c_in, c_out in zip(segs, conv_in, conv_out):
        P = (CONV_B - 1) * seg.S
        cv_scr[seg.conv0:seg.conv0 + P, :] = c_in[...]
        convs.append(_sconv_conv(seg, cv[seg.row0:seg.row0 + seg.rows], cw_ref[...], cv_scr))
        c_out[...] = cv_scr[seg.conv0:seg.conv0 + P, :]
    m = (_dot(xn, w_bg) * jnp.concatenate(convs, axis=0)).astype(BF16)
    _accumulate(y_scr, slice(None), _dot(m, w_out_rows), b)

    @pl.when(b == N_COL_BLOCKS - 1)
    def _():
        o_ref[...] = x_ref[...] + _rms(y_scr[...], npost_ref[layer:layer + 1, :])


def _sconv_stream_layer(x, conv_in, layer, j, w, *, seqs_and_steps):
    R, D = x.shape
    C = COL_BLOCK
    segs, _, conv_rows, _ = _segments(seqs_and_steps, CONV_B)
    n_layers = w["norm_mix_pre"].shape[0]
    col = lambda rows: pl.BlockSpec((rows, C), lambda b: (0, b))
    w_in_part = lambda k: pl.BlockSpec((None, D, C), lambda b: (j, 0, k * N_COL_BLOCKS + b))
    outs = pl.pallas_call(
        functools.partial(_sconv_stream_kernel, segs, layer, j),
        grid=(N_COL_BLOCKS,),
        in_specs=[_resident((R, D))] + [col(c.shape[0]) for c in conv_in]
                 + [_resident((n_layers, D)), _resident((n_layers, D)), w_in_part(0), w_in_part(1), w_in_part(2),
                    pl.BlockSpec((None, CONV_B, C), lambda b: (j, 0, b)),
                    pl.BlockSpec((None, C, D), lambda b: (j, b, 0))],
        out_specs=[pl.BlockSpec((R, D), lambda b: (0, 0))] + [col(c.shape[0]) for c in conv_in]
                  + [_chunk_major_out(D)] * 3 + [pl.BlockSpec((C, D), lambda b: (b, 0))],
        out_shape=[jax.ShapeDtypeStruct((R, D), F32)] + [jax.ShapeDtypeStruct(c.shape, F32) for c in conv_in]
                  + [jax.ShapeDtypeStruct((N_COL_BLOCKS, D, C), BF16)] * 3 + [jax.ShapeDtypeStruct((D, D), BF16)],
        scratch_shapes=[pltpu.VMEM((R, D), BF16),
                        pltpu.VMEM((conv_rows, C), F32),
                        pltpu.VMEM((R, D), F32)],
        compiler_params=_params(),
        name="sconv_mixer_small",
    )(x, *conv_in, w["norm_mix_pre"], w["norm_mix_post"], w["sc_w_in"], w["sc_w_in"], w["sc_w_in"],
      w["sc_conv_w"], w["sc_w_out"])
    n = len(segs)
    bf16 = dict(w_bg=outs[n + 1], w_cg=outs[n + 2], w_v=outs[n + 3], w_out=outs[n + 4])
    return outs[0], list(outs[1:1 + n]), bf16


def _sconv_small_kernel(segs, layer, j, x_ref, *refs):
    D, C, NB = D_MODEL, COL_BLOCK, N_COL_BLOCKS
    n = len(segs)
    conv_in = refs[0:n]
    npre_ref, npost_ref, cw_ref, win_hbm, wout_hbm = refs[n:n + 5]
    o_ref = refs[n + 5]
    conv_out = refs[n + 6:2 * n + 6]
    wbg_b_hbm, wcg_b_hbm, wv_b_hbm, wout_b_hbm = refs[2 * n + 6:2 * n + 10]
    xn_scr, cv_scr, col_f, row_f, col_h, row_h, in_sems, out_sems = refs[2 * n + 10:]

    def fetch(b):
        return ([pltpu.make_async_copy(win_hbm.at[j, :, pl.ds(k * D + b * C, C)], col_f.at[b, k], in_sems.at[b, k])
                 for k in range(3)]
                + [pltpu.make_async_copy(wout_hbm.at[j, pl.ds(b * C, C), :], row_f.at[b], in_sems.at[b, 3])])

    def emit(b):
        return ([pltpu.make_async_copy(col_h.at[b, k], dst.at[b], out_sems.at[b, k])
                 for k, dst in enumerate((wbg_b_hbm, wcg_b_hbm, wv_b_hbm))]
                + [pltpu.make_async_copy(row_h.at[b], wout_b_hbm.at[pl.ds(b * C, C), :], out_sems.at[b, 3])])

    for b in range(NB):
        _start_all(fetch(b))
    xn_scr[...] = _rms(x_ref[...], npre_ref[layer:layer + 1, :]).astype(BF16)
    xn = xn_scr[...]

    y = None
    for b in range(NB):
        cols = slice(b * C, (b + 1) * C)
        _wait_all(fetch(b))
        w_bg, w_cg, w_v = (col_f[b, k].astype(BF16) for k in range(3))
        w_out_rows = row_f[b].astype(BF16)
        col_h[b, 0], col_h[b, 1], col_h[b, 2], row_h[b] = w_bg, w_cg, w_v, w_out_rows
        _start_all(emit(b))

        cv = _dot(xn, w_cg) * _dot(xn, w_v)
        convs = []
        for seg, c_in, c_out in zip(segs, conv_in, conv_out):
            P = (CONV_B - 1) * seg.S
            cv_scr[seg.conv0:seg.conv0 + P, :] = c_in[:, cols]
            convs.append(_sconv_conv(seg, cv[seg.row0:seg.row0 + seg.rows], cw_ref[:, cols], cv_scr))
            c_out[:, cols] = cv_scr[seg.conv0:seg.conv0 + P, :]
        m = (_dot(xn, w_bg) * jnp.concatenate(convs, axis=0)).astype(BF16)
        part = _dot(m, w_out_rows)
        y = part if y is None else y + part

    for b in range(NB):
        _wait_all(emit(b))
    o_ref[...] = x_ref[...] + _rms(y, npost_ref[layer:layer + 1, :])


def _sconv_small_layer(x, conv_in, layer, j, w, *, seqs_and_steps):
    R, D = x.shape
    C, NB = COL_BLOCK, N_COL_BLOCKS
    segs, _, conv_rows, _ = _segments(seqs_and_steps, CONV_B)
    n_layers = w["norm_mix_pre"].shape[0]
    hbm = pl.BlockSpec(memory_space=pl.ANY)
    outs = pl.pallas_call(
        functools.partial(_sconv_small_kernel, segs, layer, j),
        grid=(1,),
        in_specs=[_resident((R, D))] + [_resident(c.shape) for c in conv_in]
                 + [_resident((n_layers, D)), _resident((n_layers, D)),
                    pl.BlockSpec((None, CONV_B, D), lambda i: (j, 0, 0), pipeline_mode=pl.Buffered(1)), hbm, hbm],
        out_specs=[pl.BlockSpec((R, D), lambda i: (0, 0))]
                  + [pl.BlockSpec(c.shape, lambda i: (0, 0)) for c in conv_in] + [hbm] * 4,
        out_shape=[jax.ShapeDtypeStruct((R, D), F32)] + [jax.ShapeDtypeStruct(c.shape, F32) for c in conv_in]
                  + [jax.ShapeDtypeStruct((NB, D, C), BF16)] * 3 + [jax.ShapeDtypeStruct((D, D), BF16)],
        scratch_shapes=[pltpu.VMEM((R, D), BF16),
                        pltpu.VMEM((conv_rows, C), F32),
                        pltpu.VMEM((NB, 3, D, C), F32),
                        pltpu.VMEM((NB, C, D), F32),
                        pltpu.VMEM((NB, 3, D, C), BF16), pltpu.VMEM((NB, C, D), BF16),
                        pltpu.SemaphoreType.DMA((NB, 4)), pltpu.SemaphoreType.DMA((NB, 4))],
        compiler_params=_params(),
        name="sconv_mixer_small",
    )(x, *conv_in, w["norm_mix_pre"], w["norm_mix_post"], w["sc_conv_w"], w["sc_w_in"], w["sc_w_out"])
    n = len(segs)
    bf16 = dict(w_bg=outs[n + 1], w_cg=outs[n + 2], w_v=outs[n + 3], w_out=outs[n + 4])
    return outs[0], list(outs[1:1 + n]), bf16


def _sconv_kernel(seg, nblk, layer, x_ref, conv_in_ref, npre_ref, npost_ref, wbg_ref, wcg_ref, wv_ref,
                  cw_ref, wout_ref, o_ref, conv_out_ref, xn_scr, cv_scr, m_scr):
    P = (CONV_B - 1) * seg.S
    i = pl.program_id(0)
    blocks = [slice(b * COL_BLOCK, (b + 1) * COL_BLOCK) for b in range(N_COL_BLOCKS)]

    @pl.when(i == 0)
    def _():
        for b, cols in enumerate(blocks):
            cv_scr[b, 0:P, :] = conv_in_ref[:, cols]

    xn_scr[...] = _rms(x_ref[...], npre_ref[layer:layer + 1, :]).astype(BF16)
    xn = xn_scr[...]
    for b, cols in enumerate(blocks):
        cv = _dot(xn, wcg_ref[b]) * _dot(xn, wv_ref[b])
        conv = _sconv_conv(seg, cv, cw_ref[:, cols], cv_scr.at[b])
        m_scr[:, cols] = (_dot(xn, wbg_ref[b]) * conv).astype(BF16)
    y = _dot(m_scr[...], wout_ref[...])
    o_ref[...] = x_ref[...] + _rms(y, npost_ref[layer:layer + 1, :])

    @pl.when(i == nblk - 1)
    def _():
        for b, cols in enumerate(blocks):
            conv_out_ref[:, cols] = cv_scr[b, 0:P, :]


def _sconv_layer(x, conv_in, layer, j, w, bf16, *, S, TB):
    N, D = x.shape
    (seg,), R, conv_rows, _ = _segments(((S, TB),), CONV_B)
    nblk = N // R
    P = (CONV_B - 1) * S
    n_layers = w["norm_mix_pre"].shape[0]
    row_spec = pl.BlockSpec((R, D), lambda i: (i, 0))
    return pl.pallas_call(
        functools.partial(_sconv_kernel, seg, nblk, layer),
        grid=(nblk,),
        in_specs=[row_spec, _resident((P, D)), _resident((n_layers, D)), _resident((n_layers, D)),
                  _resident(bf16["w_bg"].shape), _resident(bf16["w_cg"].shape), _resident(bf16["w_v"].shape),
                  pl.BlockSpec((None, CONV_B, D), lambda i: (j, 0, 0), pipeline_mode=pl.Buffered(1)),
                  _resident((D, D))],
        out_specs=[row_spec, pl.BlockSpec((P, D), lambda i: (0, 0))],
        out_shape=[jax.ShapeDtypeStruct((N, D), F32), jax.ShapeDtypeStruct((P, D), F32)],
        scratch_shapes=[pltpu.VMEM((R, D), BF16),
                        pltpu.VMEM((N_COL_BLOCKS, conv_rows, COL_BLOCK), F32),
                        pltpu.VMEM((R, D), BF16)],
        compiler_params=_params(),
        name="sconv_mixer",
    )(x, conv_in, w["norm_mix_pre"], w["norm_mix_post"], bf16["w_bg"], bf16["w_cg"], bf16["w_v"],
      w["sc_conv_w"], bf16["w_out"])


def _swiglu(g, u):
    return (g * jax.nn.sigmoid(g) * u).astype(BF16)


def _ffn_stream_kernel(layer, nchunk, sample_out, x_ref, npre_ref, npost_ref, wg_hbm, wu_hbm, wd_hbm,
                       o_ref, wg_b_hbm, wu_b_hbm, wd_b_hbm,
                       xn_scr, wg_f, wu_f, wd_f, wg_h, wu_h, wd_h, in_sems, out_sems, *out_staging):
    F = FF_CHUNK

    NB = STREAM_BUFFERS

    def fetch(c):
        k, cols = c % NB, pl.ds(c * F, F)
        return [pltpu.make_async_copy(wg_hbm.at[layer, :, cols], wg_f.at[k], in_sems.at[0, k]),
                pltpu.make_async_copy(wu_hbm.at[layer, :, cols], wu_f.at[k], in_sems.at[1, k]),
                pltpu.make_async_copy(wd_hbm.at[layer, cols, :], wd_f.at[k], in_sems.at[2, k])]

    def emit(c):
        k = c % NB
        return [pltpu.make_async_copy(wg_h.at[k], wg_b_hbm.at[c], out_sems.at[0, k]),
                pltpu.make_async_copy(wu_h.at[k], wu_b_hbm.at[c], out_sems.at[1, k]),
                pltpu.make_async_copy(wd_h.at[k], wd_b_hbm.at[pl.ds(c * F, F), :], out_sems.at[2, k])]

    for c in range(min(NB, nchunk)):
        for cp in fetch(c):
            cp.start()
    xn_scr[...] = _rms(x_ref[...], npre_ref[layer:layer + 1, :]).astype(BF16)
    xn = xn_scr[...]

    y = None
    for c in range(nchunk):
        k = c % NB
        for cp in fetch(c):
            cp.wait()
        if c >= NB:
            for cp in emit(c - NB):
                cp.wait()
        wg, wu, wd = wg_f[k].astype(BF16), wu_f[k].astype(BF16), wd_f[k].astype(BF16)
        wg_h[k], wu_h[k], wd_h[k] = wg, wu, wd
        for cp in emit(c):
            cp.start()
        if c + NB < nchunk:
            for cp in fetch(c + NB):
                cp.start()
        part = _dot(_swiglu(_dot(xn, wg), _dot(xn, wu)), wd)
        y = part if y is None else y + part
    for c in range(max(nchunk - NB, 0), nchunk):
        for cp in emit(c):
            cp.wait()

    out = x_ref[...] + _rms(y, npost_ref[layer:layer + 1, :])
    if sample_out is None:
        o_ref[...] = out
    else:
        row0, S, T = sample_out
        obuf, sems = out_staging
        obuf[...] = out[row0:row0 + T * S].reshape(T, S, out.shape[-1])
        copies = [pltpu.make_async_copy(obuf.at[t], o_ref.at[:, t, :], sems.at[t]) for t in range(T)]
        for cp in copies:
            cp.start()
        for cp in copies:
            cp.wait()


def _ffn_stream_layer(x, layer, w, *, sample_out=None):
    R, D = x.shape
    F, NB = FF_CHUNK, STREAM_BUFFERS
    n_layers = w["norm_ffn_pre"].shape[0]
    nchunk = D_FF // F
    hbm = pl.BlockSpec(memory_space=pl.ANY)
    if sample_out is None:
        o_spec, o_shape, staging = pl.BlockSpec((R, D), lambda i: (0, 0)), jax.ShapeDtypeStruct((R, D), F32), []
    else:
        _, S, T = sample_out
        o_spec, o_shape = hbm, jax.ShapeDtypeStruct((S, T, D), F32)
        staging = [pltpu.VMEM((T, S, D), F32), pltpu.SemaphoreType.DMA((T,))]
    outs = pl.pallas_call(
        functools.partial(_ffn_stream_kernel, layer, nchunk, sample_out),
        grid=(1,),
        in_specs=[_resident((R, D)), _resident((n_layers, D)), _resident((n_layers, D)), hbm, hbm, hbm],
        out_specs=[o_spec, hbm, hbm, hbm],
        out_shape=[o_shape, jax.ShapeDtypeStruct((nchunk, D, F), BF16),
                   jax.ShapeDtypeStruct((nchunk, D, F), BF16), jax.ShapeDtypeStruct((D_FF, D), BF16)],
        scratch_shapes=[pltpu.VMEM((R, D), BF16),
                        pltpu.VMEM((NB, D, F), F32), pltpu.VMEM((NB, D, F), F32), pltpu.VMEM((NB, F, D), F32),
                        pltpu.VMEM((NB, D, F), BF16), pltpu.VMEM((NB, D, F), BF16), pltpu.VMEM((NB, F, D), BF16),
                        pltpu.SemaphoreType.DMA((3, NB)), pltpu.SemaphoreType.DMA((3, NB))] + staging,
        compiler_params=_params(),
        name="swiglu_ffn_small",
    )(x, w["norm_ffn_pre"], w["norm_ffn_post"], w["ffn_w_gate"], w["ffn_w_up"], w["ffn_w_down"])
    return outs[0], dict(wg=outs[1], wu=outs[2], wd=outs[3])


def _ffn_kernel(layer, nblk, batch_major_out, x_ref, npre_ref, npost_ref, wg_ref, wu_ref, wd_ref, o_ref,
                xn_scr, act_scr, *out_staging):
    R = x_ref.shape[0]
    i = pl.program_id(0)
    slot = i % 2
    if batch_major_out:
        obuf, sems = out_staging
        S, TB = o_ref.shape[0], R // o_ref.shape[0]

        def copies(which, step):
            t0 = pl.multiple_of(step * TB, TB)
            return [pltpu.make_async_copy(obuf.at[which, :, s, :],
                                          o_ref.at[s, pl.ds(t0, TB), :], sems.at[which, s]) for s in range(S)]

        @pl.when(i >= 2)
        def _():
            for c in copies(slot, i - 2):
                c.wait()

    halves = [slice(0, R // 2), slice(R // 2, R)]
    for rows in halves:
        xn_scr[rows, :] = _rms(x_ref[rows, :], npre_ref[layer:layer + 1, :]).astype(BF16)
    for rows in halves:
        xn = xn_scr[rows, :]
        for c in range(D_FF // FF_CHUNK):
            cols = slice(c * FF_CHUNK, (c + 1) * FF_CHUNK)
            act_scr[rows, cols] = _swiglu(_dot(xn, wg_ref[c]), _dot(xn, wu_ref[c]))
    ys = [_dot(act_scr[rows, :], wd_ref[...]) for rows in halves]
    for rows, y in zip(halves, ys):
        out = x_ref[rows, :] + _rms(y, npost_ref[layer:layer + 1, :])
        if batch_major_out:
            obuf[slot, rows.start // S:rows.stop // S] = out.reshape(-1, S, out.shape[-1])
        else:
            o_ref[rows, :] = out

    if batch_major_out:
        for c in copies(slot, i):
            c.start()

        @pl.when(i == nblk - 1)
        def _():
            if nblk >= 2:
                for c in copies(1 - slot, i - 1):
                    c.wait()
            for c in copies(slot, i):
                c.wait()


def _ffn_layer(x, layer, w, bf16, *, R, batch_major_seqs=None):
    N, D = x.shape
    n_layers = w["norm_ffn_pre"].shape[0]
    nblk = N // R
    row_spec = pl.BlockSpec((R, D), lambda i: (i, 0))
    if batch_major_seqs is None:
        out_spec, out_shape, staging = row_spec, jax.ShapeDtypeStruct((N, D), F32), []
    else:
        S = batch_major_seqs
        out_spec = pl.BlockSpec(memory_space=pl.ANY)
        out_shape = jax.ShapeDtypeStruct((S, N // S, D), F32)
        staging = [pltpu.VMEM((2, R // S, S, D), F32), pltpu.SemaphoreType.DMA((2, S))]
    return pl.pallas_call(
        functools.partial(_ffn_kernel, layer, nblk, batch_major_seqs is not None),
        grid=(nblk,),
        in_specs=[row_spec, _resident((n_layers, D)), _resident((n_layers, D)),
                  _resident(bf16["wg"].shape), _resident(bf16["wu"].shape), _resident((D_FF, D))],
        out_specs=out_spec,
        out_shape=out_shape,
        scratch_shapes=[pltpu.VMEM((R, D), BF16), pltpu.VMEM((R, D_FF), BF16)] + staging,
        compiler_params=_params(),
        name="swiglu_ffn",
    )(x, w["norm_ffn_pre"], w["norm_ffn_post"], bf16["wg"], bf16["wu"], bf16["wd"])


def _to_time_major(a):
    S, K, D = a.shape
    return jnp.swapaxes(a, 0, 1).reshape(K * S, D)


def _from_time_major(a, S):
    KS, D = a.shape
    return jnp.swapaxes(a.reshape(KS // S, S, D), 0, 1)


def kernel(x_prompt, x_sample, state_rglru_conv, state_rglru_h, state_sconv, meta_tokens, norm_mix_pre, norm_mix_post, norm_ffn_pre, norm_ffn_post, rg_w_in, rg_conv_w, rg_conv_b, rg_gate_a_w, rg_gate_a_b, rg_gate_x_w, rg_gate_x_b, rg_lambda, rg_w_out, sc_w_in, sc_conv_w, sc_w_out, ffn_w_gate, ffn_w_up, ffn_w_down):
    D = D_MODEL
    depth = norm_mix_pre.shape[0]
    batch, seq, _ = x_prompt.shape
    dec_batch, dec_seq, _ = x_sample.shape
    w = dict(norm_mix_pre=norm_mix_pre, norm_mix_post=norm_mix_post, norm_ffn_pre=norm_ffn_pre,
             norm_ffn_post=norm_ffn_post, rg_w_in=rg_w_in, rg_conv_w=rg_conv_w, rg_conv_b=rg_conv_b,
             rg_gate_a_w=rg_gate_a_w, rg_gate_a_b=rg_gate_a_b, rg_gate_x_w=rg_gate_x_w, rg_gate_x_b=rg_gate_x_b,
             rg_lambda=rg_lambda, rg_w_out=rg_w_out, sc_w_in=sc_w_in, sc_conv_w=sc_conv_w, sc_w_out=sc_w_out,
             ffn_w_gate=ffn_w_gate, ffn_w_up=ffn_w_up, ffn_w_down=ffn_w_down)

    x = jnp.broadcast_to(meta_tokens[:, None, :], (N_META, batch, D)).reshape(N_META * batch, D)
    small = ((batch, N_META), (dec_batch, dec_seq))
    mixer_bf16, ffn_bf16 = [], []
    rg_conv_s, rg_h_s, sc_s = [], [], []
    for i in range(depth):
        j = i // 2
        if i % 2 == 0:
            conv0 = [jnp.zeros(((CONV_A - 1) * batch, D), F32), _to_time_major(state_rglru_conv[j])]
            h0 = [jnp.zeros((batch, D), F32), state_rglru_h[j]]
            x, cb, hT, wb = _rglru_small_layer(x, conv0, h0, i, j, w, seqs_and_steps=small,
                                                x_sample=x_sample if i == 0 else None)
            rg_conv_s.append(cb)
            rg_h_s.append(hT)
        else:
            conv0 = [jnp.zeros(((CONV_B - 1) * batch, D), F32), _to_time_major(state_sconv[j])]
            x, cb, wb = _sconv_small_layer(x, conv0, i, j, w, seqs_and_steps=small)
            sc_s.append(cb)
        mixer_bf16.append(wb)
        sample_out = (N_META * batch, dec_batch, dec_seq) if i == depth - 1 else None
        x, wb = _ffn_stream_layer(x, i, w, sample_out=sample_out)
        ffn_bf16.append(wb)
    y_sample = x

    x = x_prompt
    rg_conv_p, rg_h_p, sc_p = [], [], []
    for i in range(depth):
        j = i // 2
        if i % 2 == 0:
            x, cb, hT = _rglru_layer(x, rg_conv_s[j][0], rg_h_s[j][0], i, j, w, mixer_bf16[i], S=batch, TB=PROMPT_TB)
            rg_conv_p.append(cb)
            rg_h_p.append(hT)
        else:
            x, cb = _sconv_layer(x, sc_s[j][0], i, j, w, mixer_bf16[i], S=batch, TB=PROMPT_TB)
            sc_p.append(cb)
        x = _ffn_layer(x, i, w, ffn_bf16[i], R=batch * PROMPT_TB,
                       batch_major_seqs=batch if i == depth - 1 else None)
    y_prompt = x

    return (y_prompt, y_sample,
            jnp.stack([_from_time_major(c, batch) for c in rg_conv_p]), jnp.stack(rg_h_p),
            jnp.stack([_from_time_major(c, batch) for c in sc_p]),
            jnp.stack([_from_time_major(c[1], dec_batch) for c in rg_conv_s]), jnp.stack([h[1] for h in rg_h_s]),
            jnp.stack([_from_time_major(c[1], dec_batch) for c in sc_s]))
```

```python
import functools
from typing import NamedTuple

import jax
import jax.numpy as jnp
from jax import lax
from jax.experimental import pallas as pl
from jax.experimental.pallas import tpu as pltpu

D_MODEL = 1024
D_FF = 2816
N_META = 16
COL_BLOCK = 256
N_COL_BLOCKS = D_MODEL // COL_BLOCK
CONV_A = 4
CONV_B = 3
RG_C = 8.0
EPS = 1e-6

SUBLANES = 8
FF_CHUNK = 256
STREAM_BUFFERS = 3
PROMPT_TB = 128
VMEM_LIMIT_BYTES = 56 * 1024 * 1024

F32 = jnp.float32
BF16 = jnp.bfloat16


class _Seg(NamedTuple):
    S: int
    TB: int
    row0: int
    conv0: int
    h0: int

    @property
    def rows(self):
        return self.S * self.TB


def _segments(seqs_and_steps, taps):
    segs, row0, conv0, h0 = [], 0, 0, 0
    for S, TB in seqs_and_steps:
        segs.append(_Seg(S, TB, row0, conv0, h0))
        row0 += S * TB
        conv0 += (taps - 1) * S + S * TB
        h0 += S
    return tuple(segs), row0, conv0, h0


def _rms(x, w):
    ms = jnp.mean(x * x, axis=-1, keepdims=True)
    return x * lax.rsqrt(ms + EPS) * w


def _dot(a, b):
    return jnp.dot(a, b, preferred_element_type=F32)


def _gelu_tanh(x):
    c = 0.7978845608028654
    hx = 0.5 * x
    return hx + hx * jnp.tanh(x * (c + (c * 0.044715) * (x * x)))


def _resident(shape):
    zeros = (0,) * len(shape)
    return pl.BlockSpec(shape, lambda i: zeros, pipeline_mode=pl.Buffered(1))


def _chunk_major_out(rows, width=COL_BLOCK):
    return pl.BlockSpec((None, rows, width), lambda i: (i, 0, 0))


def _params():
    return pltpu.CompilerParams(dimension_semantics=("arbitrary",), vmem_limit_bytes=VMEM_LIMIT_BYTES)


def _accumulate(y_scr, rows, part, step):
    @pl.when(step == 0)
    def _():
        y_scr[rows, :] = part

    @pl.when(step > 0)
    def _():
        y_scr[rows, :] += part


def _rglru_branch_in(seg, xn, w_xr, region):
    P = (CONV_A - 1) * seg.S
    region[seg.conv0 + P:seg.conv0 + P + seg.rows, :] = _dot(xn, w_xr)


def _rglru_conv(seg, cw, cb, region, carry_to):
    S, R = seg.S, seg.rows
    P = (CONV_A - 1) * S
    c0 = seg.conv0
    xc = region[c0:c0 + R, :] * cw[0:1]
    for k in range(1, CONV_A):
        xc = xc + region[c0 + k * S:c0 + k * S + R, :] * cw[k:k + 1]
    xc = xc + cb
    if carry_to is not None:
        region[carry_to:carry_to + P, :] = region[c0 + R:c0 + R + P, :]
    return xc


def _rglru_gate_math(res, xc, half_gab, half_gxb, lam):
    half_c_sp = (-0.5 * RG_C) * jax.nn.softplus(-lam)
    tr = jnp.tanh(res[:, 0:COL_BLOCK] + half_gab)
    log_a = half_c_sp * tr + half_c_sp
    ig = 0.5 * jnp.tanh(res[:, COL_BLOCK:2 * COL_BLOCK] + half_gxb) + 0.5
    a = jnp.exp(log_a)
    m2 = jnp.tanh(log_a) * (-1.0 - a * a)
    u = jnp.where(m2 > 0.0, m2 * lax.rsqrt(m2), 0.0) * (ig * xc)
    return a, u


def _rglru_scan(seg, a, u, g, h_read, h_write, h_row0):
    S, TB = seg.S, seg.TB
    groups = S // SUBLANES
    pieces = [None] * (TB * groups)
    for c in range(groups):
        hrows = slice(h_row0 + c * SUBLANES, h_row0 + (c + 1) * SUBLANES)
        h = h_read[hrows, :]
        for t in range(TB):
            r = t * S + c * SUBLANES
            h = a[r:r + SUBLANES] * h + u[r:r + SUBLANES]
            pieces[t * groups + c] = h * g[r:r + SUBLANES]
        h_write[hrows, :] = h
    return jnp.concatenate(pieces, axis=0)


def _rglru_stream_kernel(segs, sample_in, layer, j, x_ref, *refs):
    if sample_in is not None:
        xs_hbm, refs = refs[0], refs[1:]
    n = len(segs)
    conv_in, h_in = refs[0:n], refs[n:2 * n]
    (npre_ref, npost_ref, wxr_ref, wgate_ref, cw_ref, cb_ref, gaw_ref, gxw_ref, gab_ref, gxb_ref,
     lam_ref, wout_ref) = refs[2 * n:2 * n + 12]
    o_ref = refs[2 * n + 12]
    conv_out, h_out = refs[2 * n + 13:3 * n + 13], refs[3 * n + 13:4 * n + 13]
    wxr_b_ref, wgate_b_ref, gw_b_ref, wout_b_ref = refs[4 * n + 13:4 * n + 17]
    xn_scr, xr_scr, y_scr, *in_staging = refs[4 * n + 17:]
    b = pl.program_id(0)

    if sample_in is None:
        read_x = lambda: x_ref[...]
    else:
        S, T = sample_in
        xs_scr, sems = in_staging
        read_x = lambda: jnp.concatenate([x_ref[...], xs_scr[...].reshape(T * S, D_MODEL)], axis=0)

    @pl.when(b == 0)
    def _():
        if sample_in is not None:
            copies = [pltpu.make_async_copy(xs_hbm.at[:, t, :], xs_scr.at[t], sems.at[t]) for t in range(T)]
            for cp in copies:
                cp.start()
            for cp in copies:
                cp.wait()
        xn_scr[...] = _rms(read_x(), npre_ref[layer:layer + 1, :]).astype(BF16)

    w_xr = wxr_ref[...].astype(BF16)
    w_gate = wgate_ref[...].astype(BF16)
    gw = (0.5 * jnp.concatenate([gaw_ref[...], gxw_ref[...]], axis=1)).astype(BF16)
    w_out_rows = wout_ref[...].astype(BF16)
    wxr_b_ref[...] = w_xr
    wgate_b_ref[...] = w_gate
    gw_b_ref[...] = gw
    wout_b_ref[...] = w_out_rows

    rows_of = lambda seg: slice(seg.row0, seg.row0 + seg.rows)
    xn = xn_scr[...]
    xr = _dot(xn, w_xr)
    g = _gelu_tanh(_dot(xn, w_gate))
    xcs = []
    for seg, c_in, c_out in zip(segs, conv_in, conv_out):
        P = (CONV_A - 1) * seg.S
        xr_scr[seg.conv0:seg.conv0 + P, :] = c_in[...]
        xr_scr[seg.conv0 + P:seg.conv0 + P + seg.rows, :] = xr[rows_of(seg)]
        xcs.append(_rglru_conv(seg, cw_ref[...], cb_ref[j:j + 1, :], xr_scr, seg.conv0))
        c_out[...] = xr_scr[seg.conv0:seg.conv0 + P, :]
    xc = jnp.concatenate(xcs, axis=0)
    a, u = _rglru_gate_math(_dot(xc.astype(BF16), gw), xc, 0.5 * gab_ref[pl.ds(b, 1), :],
                            0.5 * gxb_ref[pl.ds(b, 1), :], lam_ref[j:j + 1, :])
    hs = [_rglru_scan(seg, a[rows_of(seg)], u[rows_of(seg)], g[rows_of(seg)], hi, ho, 0)
          for seg, hi, ho in zip(segs, h_in, h_out)]
    part = _dot(jnp.concatenate(hs, axis=0).astype(BF16), w_out_rows)
    _accumulate(y_scr, slice(None), part, b)

    @pl.when(b == N_COL_BLOCKS - 1)
    def _():
        o_ref[...] = read_x() + _rms(y_scr[...], npost_ref[layer:layer + 1, :])


def _rglru_stream_layer(x, conv_in, h_in, layer, j, w, *, seqs_and_steps, x_sample=None):
    D, C = D_MODEL, COL_BLOCK
    segs, R, conv_rows, _ = _segments(seqs_and_steps, CONV_A)
    n_layers, n_a = w["norm_mix_pre"].shape[0], w["rg_conv_b"].shape[0]
    col = lambda rows: pl.BlockSpec((rows, C), lambda b: (0, b))
    if x_sample is None:
        x_args, x_specs, sample_in, staging = [x], [_resident(x.shape)], None, []
    else:
        S, T, _ = x_sample.shape
        x_args, x_specs, sample_in = [x, x_sample], [_resident(x.shape), pl.BlockSpec(memory_space=pl.ANY)], (S, T)
        staging = [pltpu.VMEM((T, S, D), F32), pltpu.SemaphoreType.DMA((T,))]
    in_specs = (
        x_specs + [col(c.shape[0]) for c in conv_in] + [col(h.shape[0]) for h in h_in]
        + [_resident((n_layers, D)), _resident((n_layers, D)),
           pl.BlockSpec((None, D, C), lambda b: (j, 0, N_COL_BLOCKS + b)),
           pl.BlockSpec((None, D, C), lambda b: (j, 0, b)),
           pl.BlockSpec((None, CONV_A, C), lambda b: (j, 0, b)),
           col(n_a),
           pl.BlockSpec((None, None, C, C), lambda b: (j, b, 0, 0)),
           pl.BlockSpec((None, None, C, C), lambda b: (j, b, 0, 0)),
           pl.BlockSpec((None, N_COL_BLOCKS, C), lambda b: (j, 0, 0)),
           pl.BlockSpec((None, N_COL_BLOCKS, C), lambda b: (j, 0, 0)),
           col(n_a),
           pl.BlockSpec((None, C, D), lambda b: (j, b, 0))])
    out_specs = ([pl.BlockSpec((R, D), lambda b: (0, 0))] + [col(c.shape[0]) for c in conv_in]
                 + [col(h.shape[0]) for h in h_in]
                 + [_chunk_major_out(D), _chunk_major_out(D), pl.BlockSpec((None, C, 2 * C), lambda b: (b, 0, 0)),
                    pl.BlockSpec((C, D), lambda b: (b, 0))])
    out_shape = ([jax.ShapeDtypeStruct((R, D), F32)] + [jax.ShapeDtypeStruct(s.shape, F32) for s in conv_in + h_in]
                 + [jax.ShapeDtypeStruct((N_COL_BLOCKS, D, C), BF16), jax.ShapeDtypeStruct((N_COL_BLOCKS, D, C), BF16),
                    jax.ShapeDtypeStruct((N_COL_BLOCKS, C, 2 * C), BF16), jax.ShapeDtypeStruct((D, D), BF16)])
    outs = pl.pallas_call(
        functools.partial(_rglru_stream_kernel, segs, sample_in, layer, j),
        grid=(N_COL_BLOCKS,),
        in_specs=in_specs, out_specs=out_specs, out_shape=out_shape,
        scratch_shapes=[pltpu.VMEM((R, D), BF16),
                        pltpu.VMEM((conv_rows, C), F32),
                        pltpu.VMEM((R, D), F32)]
                       + staging,
        compiler_params=_params(),
        name="rglru_mixer_small",
    )(*x_args, *conv_in, *h_in, w["norm_mix_pre"], w["norm_mix_post"], w["rg_w_in"], w["rg_w_in"], w["rg_conv_w"],
      w["rg_conv_b"], w["rg_gate_a_w"], w["rg_gate_x_w"], w["rg_gate_a_b"], w["rg_gate_x_b"], w["rg_lambda"],
      w["rg_w_out"])
    n = len(segs)
    bf16 = dict(w_xr=outs[2 * n + 1], w_gate=outs[2 * n + 2], gw=outs[2 * n + 3], w_out=outs[2 * n + 4])
    return outs[0], list(outs[1:1 + n]), list(outs[1 + n:1 + 2 * n]), bf16


def _start_all(copies):
    for cp in copies:
        cp.start()


def _wait_all(copies):
    for cp in copies:
        cp.wait()


def _rglru_small_kernel(segs, sample_in, layer, j, x_ref, *refs):
    D, C, NB = D_MODEL, COL_BLOCK, N_COL_BLOCKS
    if sample_in is not None:
        xs_hbm, refs = refs[0], refs[1:]
    n = len(segs)
    conv_in, h_in = refs[0:n], refs[n:2 * n]
    (npre_ref, npost_ref, cw_ref, cb_ref, gab_ref, gxb_ref, lam_ref,
     win_hbm, gaw_hbm, gxw_hbm, wout_hbm) = refs[2 * n:2 * n + 11]
    o_ref = refs[2 * n + 11]
    conv_out, h_out = refs[2 * n + 12:3 * n + 12], refs[3 * n + 12:4 * n + 12]
    wxr_b_hbm, wgate_b_hbm, gw_b_hbm, wout_b_hbm = refs[4 * n + 12:4 * n + 16]
    (xn_scr, xr_scr, col_f, sq_f, row_f, col_h, gw_h, row_h, in_sems, out_sems,
     *in_staging) = refs[4 * n + 16:]

    def fetch(b):
        cols = pl.ds(b * C, C)
        return [pltpu.make_async_copy(win_hbm.at[j, :, pl.ds(D + b * C, C)], col_f.at[b, 0], in_sems.at[b, 0]),
                pltpu.make_async_copy(win_hbm.at[j, :, cols], col_f.at[b, 1], in_sems.at[b, 1]),
                pltpu.make_async_copy(gaw_hbm.at[j, b], sq_f.at[b, 0], in_sems.at[b, 2]),
                pltpu.make_async_copy(gxw_hbm.at[j, b], sq_f.at[b, 1], in_sems.at[b, 3]),
                pltpu.make_async_copy(wout_hbm.at[j, cols, :], row_f.at[b], in_sems.at[b, 4])]

    def emit(b):
        return [pltpu.make_async_copy(col_h.at[b, 0], wxr_b_hbm.at[b], out_sems.at[b, 0]),
                pltpu.make_async_copy(col_h.at[b, 1], wgate_b_hbm.at[b], out_sems.at[b, 1]),
                pltpu.make_async_copy(gw_h.at[b], gw_b_hbm.at[b], out_sems.at[b, 2]),
                pltpu.make_async_copy(row_h.at[b], wout_b_hbm.at[pl.ds(b * C, C), :], out_sems.at[b, 3])]

    if sample_in is None:
        read_x = lambda: x_ref[...]
    else:
        S, T = sample_in
        xs_scr, sems = in_staging
        x_copies = [pltpu.make_async_copy(xs_hbm.at[:, t, :], xs_scr.at[t], sems.at[t]) for t in range(T)]
        _start_all(x_copies)
        read_x = lambda: jnp.concatenate([x_ref[...], xs_scr[...].reshape(T * S, D)], axis=0)
    for b in range(NB):
        _start_all(fetch(b))
    if sample_in is not None:
        _wait_all(x_copies)
    xn_scr[...] = _rms(read_x(), npre_ref[layer:layer + 1, :]).astype(BF16)
    xn = xn_scr[...]

    rows_of = lambda seg: slice(seg.row0, seg.row0 + seg.rows)
    y = None
    for b in range(NB):
        cols = slice(b * C, (b + 1) * C)
        _wait_all(fetch(b))
        w_xr, w_gate = col_f[b, 0].astype(BF16), col_f[b, 1].astype(BF16)
        gw = (0.5 * jnp.concatenate([sq_f[b, 0], sq_f[b, 1]], axis=1)).astype(BF16)
        w_out_rows = row_f[b].astype(BF16)
        col_h[b, 0], col_h[b, 1], gw_h[b], row_h[b] = w_xr, w_gate, gw, w_out_rows
        _start_all(emit(b))

        xr = _dot(xn, w_xr)
        g = _gelu_tanh(_dot(xn, w_gate))
        xcs = []
        for seg, c_in, c_out in zip(segs, conv_in, conv_out):
            P = (CONV_A - 1) * seg.S
            xr_scr[seg.conv0:seg.conv0 + P, :] = c_in[:, cols]
            xr_scr[seg.conv0 + P:seg.conv0 + P + seg.rows, :] = xr[rows_of(seg)]
            xcs.append(_rglru_conv(seg, cw_ref[:, cols], cb_ref[j:j + 1, cols], xr_scr, seg.conv0))
            c_out[:, cols] = xr_scr[seg.conv0:seg.conv0 + P, :]
        xc = jnp.concatenate(xcs, axis=0)
        a, u = _rglru_gate_math(_dot(xc.astype(BF16), gw), xc, 0.5 * gab_ref[b:b + 1, :],
                                0.5 * gxb_ref[b:b + 1, :], lam_ref[j:j + 1, cols])
        hs = [_rglru_scan(seg, a[rows_of(seg)], u[rows_of(seg)], g[rows_of(seg)],
                          hi.at[:, cols], ho.at[:, cols], 0)
              for seg, hi, ho in zip(segs, h_in, h_out)]
        part = _dot(jnp.concatenate(hs, axis=0).astype(BF16), w_out_rows)
        y = part if y is None else y + part

    for b in range(NB):
        _wait_all(emit(b))
    o_ref[...] = read_x() + _rms(y, npost_ref[layer:layer + 1, :])


def _rglru_small_layer(x, conv_in, h_in, layer, j, w, *, seqs_and_steps, x_sample=None):
    D, C, NB = D_MODEL, COL_BLOCK, N_COL_BLOCKS
    segs, R, conv_rows, _ = _segments(seqs_and_steps, CONV_A)
    n_layers, n_a = w["norm_mix_pre"].shape[0], w["rg_conv_b"].shape[0]
    hbm = pl.BlockSpec(memory_space=pl.ANY)
    layer_of = lambda arr: pl.BlockSpec((None,) + arr.shape[1:], lambda i: (j,) + (0,) * (arr.ndim - 1),
                                        pipeline_mode=pl.Buffered(1))
    if x_sample is None:
        x_args, x_specs, sample_in, staging = [x], [_resident(x.shape)], None, []
    else:
        S, T, _ = x_sample.shape
        x_args, x_specs, sample_in = [x, x_sample], [_resident(x.shape), hbm], (S, T)
        staging = [pltpu.VMEM((T, S, D), F32), pltpu.SemaphoreType.DMA((T,))]
    states = conv_in + h_in
    outs = pl.pallas_call(
        functools.partial(_rglru_small_kernel, segs, sample_in, layer, j),
        grid=(1,),
        in_specs=x_specs + [_resident(s.shape) for s in states]
                 + [_resident((n_layers, D)), _resident((n_layers, D)), layer_of(w["rg_conv_w"]),
                    _resident((n_a, D)), layer_of(w["rg_gate_a_b"]), layer_of(w["rg_gate_x_b"]),
                    _resident((n_a, D)), hbm, hbm, hbm, hbm],
        out_specs=[pl.BlockSpec((R, D), lambda i: (0, 0))]
                  + [pl.BlockSpec(s.shape, lambda i: (0, 0)) for s in states] + [hbm] * 4,
        out_shape=[jax.ShapeDtypeStruct((R, D), F32)] + [jax.ShapeDtypeStruct(s.shape, F32) for s in states]
                  + [jax.ShapeDtypeStruct((NB, D, C), BF16), jax.ShapeDtypeStruct((NB, D, C), BF16),
                     jax.ShapeDtypeStruct((NB, C, 2 * C), BF16), jax.ShapeDtypeStruct((D, D), BF16)],
        scratch_shapes=[pltpu.VMEM((R, D), BF16),
                        pltpu.VMEM((conv_rows, C), F32),
                        pltpu.VMEM((NB, 2, D, C), F32),
                        pltpu.VMEM((NB, 2, C, C), F32),
                        pltpu.VMEM((NB, C, D), F32),
                        pltpu.VMEM((NB, 2, D, C), BF16), pltpu.VMEM((NB, C, 2 * C), BF16),
                        pltpu.VMEM((NB, C, D), BF16),
                        pltpu.SemaphoreType.DMA((NB, 5)), pltpu.SemaphoreType.DMA((NB, 4))] + staging,
        compiler_params=_params(),
        name="rglru_mixer_small",
    )(*x_args, *states, w["norm_mix_pre"], w["norm_mix_post"], w["rg_conv_w"], w["rg_conv_b"],
      w["rg_gate_a_b"], w["rg_gate_x_b"], w["rg_lambda"], w["rg_w_in"], w["rg_gate_a_w"], w["rg_gate_x_w"],
      w["rg_w_out"])
    n = len(segs)
    bf16 = dict(w_xr=outs[2 * n + 1], w_gate=outs[2 * n + 2], gw=outs[2 * n + 3], w_out=outs[2 * n + 4])
    return outs[0], list(outs[1:1 + n]), list(outs[1 + n:1 + 2 * n]), bf16


def _rglru_kernel(seg, nblk, layer, j, batch_major_in,
                  x_ref, conv_in_ref, h_in_ref, npre_ref, npost_ref, wxr_ref, wgate_ref, cw_ref, cb_ref,
                  gw_ref, gab_ref, gxb_ref, lam_ref, wout_ref,
                  o_ref, conv_out_ref, h_out_ref, xn_scr, xr_scr, h_scr, *in_staging):
    S, TB, R = seg.S, seg.TB, seg.rows
    P = (CONV_A - 1) * S
    i = pl.program_id(0)
    slot = i % 2
    blocks = [slice(b * COL_BLOCK, (b + 1) * COL_BLOCK) for b in range(N_COL_BLOCKS)]

    if batch_major_in:
        xbuf, sems = in_staging

        def copies(which, step):
            t0 = pl.multiple_of(step * TB, TB)
            return [pltpu.make_async_copy(x_ref.at[s, pl.ds(t0, TB), :], xbuf.at[which, :, s, :],
                                          sems.at[which, s]) for s in range(S)]

    @pl.when(i == 0)
    def _():
        for b, cols in enumerate(blocks):
            xr_scr[b, 0:P, :] = conv_in_ref[:, cols]
            h_scr[b] = h_in_ref[:, cols]
        if batch_major_in:
            for c in copies(0, 0):
                c.start()

    if batch_major_in:
        @pl.when(i + 1 < nblk)
        def _():
            for c in copies(1 - slot, i + 1):
                c.start()

        for c in copies(slot, i):
            c.wait()
        read_x = lambda: xbuf[slot].reshape(R, D_MODEL)
    else:
        read_x = lambda: x_ref[...]

    xn_scr[...] = _rms(read_x(), npre_ref[layer:layer + 1, :]).astype(BF16)
    xn = xn_scr[...]

    gate_pre, xc, res = {}, {}, {}

    def input_matmuls(b):
        _rglru_branch_in(seg, xn, wxr_ref[b], xr_scr.at[b])
        gate_pre[b] = _dot(xn, wgate_ref[b])

    def conv_and_gate_matmul(b):
        xc[b] = _rglru_conv(seg, cw_ref[:, blocks[b]], cb_ref[j:j + 1, blocks[b]], xr_scr.at[b], 0)
        res[b] = _dot(xc[b].astype(BF16), gw_ref[b])

    y = None
    input_matmuls(0)
    conv_and_gate_matmul(0)
    input_matmuls(1)
    for b, cols in enumerate(blocks):
        if b + 1 < N_COL_BLOCKS:
            conv_and_gate_matmul(b + 1)
        if b + 2 < N_COL_BLOCKS:
            input_matmuls(b + 2)
        g = _gelu_tanh(gate_pre.pop(b))
        a, u = _rglru_gate_math(res.pop(b), xc.pop(b), 0.5 * gab_ref[b:b + 1, :], 0.5 * gxb_ref[b:b + 1, :],
                                lam_ref[j:j + 1, cols])
        hs = _rglru_scan(seg, a, u, g, h_scr.at[b], h_scr.at[b], 0)
        part = _dot(hs.astype(BF16), wout_ref[cols, :])
        y = part if y is None else y + part

    o_ref[...] = read_x() + _rms(y, npost_ref[layer:layer + 1, :])

    @pl.when(i == nblk - 1)
    def _():
        for b, cols in enumerate(blocks):
            conv_out_ref[:, cols] = xr_scr[b, 0:P, :]
            h_out_ref[:, cols] = h_scr[b]


def _rglru_layer(x, conv_in, h_in, layer, j, w, bf16, *, S, TB):
    D = D_MODEL
    batch_major_in = x.ndim == 3
    (seg,), R, conv_rows, _ = _segments(((S, TB),), CONV_A)
    nblk = x.size // D // R
    P = (CONV_A - 1) * S
    n_layers, n_a = w["norm_mix_pre"].shape[0], w["rg_conv_b"].shape[0]
    row_spec = pl.BlockSpec((R, D), lambda i: (i, 0))
    if batch_major_in:
        x_spec = pl.BlockSpec(memory_space=pl.ANY)
        staging = [pltpu.VMEM((2, TB, S, D), F32), pltpu.SemaphoreType.DMA((2, S))]
    else:
        x_spec, staging = row_spec, []
    layer_of = lambda arr: pl.BlockSpec((None,) + arr.shape[1:], lambda i: (j,) + (0,) * (arr.ndim - 1),
                                        pipeline_mode=pl.Buffered(1))
    return pl.pallas_call(
        functools.partial(_rglru_kernel, seg, nblk, layer, j, batch_major_in),
        grid=(nblk,),
        in_specs=[x_spec, _resident((P, D)), _resident((S, D)),
                  _resident((n_layers, D)), _resident((n_layers, D)),
                  _resident(bf16["w_xr"].shape), _resident(bf16["w_gate"].shape),
                  layer_of(w["rg_conv_w"]), _resident((n_a, D)), _resident(bf16["gw"].shape),
                  layer_of(w["rg_gate_a_b"]), layer_of(w["rg_gate_x_b"]), _resident((n_a, D)),
                  _resident((D, D))],
        out_specs=[row_spec, pl.BlockSpec((P, D), lambda i: (0, 0)), pl.BlockSpec((S, D), lambda i: (0, 0))],
        out_shape=[jax.ShapeDtypeStruct((nblk * R, D), F32), jax.ShapeDtypeStruct((P, D), F32),
                   jax.ShapeDtypeStruct((S, D), F32)],
        scratch_shapes=[pltpu.VMEM((R, D), BF16),
                        pltpu.VMEM((N_COL_BLOCKS, conv_rows, COL_BLOCK), F32),
                        pltpu.VMEM((N_COL_BLOCKS, S, COL_BLOCK), F32)]
                       + staging,
        compiler_params=_params(),
        name="rglru_mixer",
    )(x, conv_in, h_in, w["norm_mix_pre"], w["norm_mix_post"], bf16["w_xr"], bf16["w_gate"], w["rg_conv_w"],
      w["rg_conv_b"], bf16["gw"], w["rg_gate_a_b"], w["rg_gate_x_b"], w["rg_lambda"], bf16["w_out"])


def _sconv_conv(seg, cv, cw, region):
    S, R = seg.S, seg.rows
    P = (CONV_B - 1) * S
    c0 = seg.conv0
    region[c0 + P:c0 + P + R, :] = cv
    conv = region[c0:c0 + R, :] * cw[0:1]
    for k in range(1, CONV_B):
        conv = conv + region[c0 + k * S:c0 + k * S + R, :] * cw[k:k + 1]
    region[c0:c0 + P, :] = region[c0 + R:c0 + R + P, :]
    return conv


def _sconv_stream_kernel(segs, layer, j, x_ref, *refs):
    n = len(segs)
    conv_in = refs[0:n]
    npre_ref, npost_ref, wbg_ref, wcg_ref, wv_ref, cw_ref, wout_ref = refs[n:n + 7]
    o_ref = refs[n + 7]
    conv_out = refs[n + 8:2 * n + 8]
    wbg_b_ref, wcg_b_ref, wv_b_ref, wout_b_ref = refs[2 * n + 8:2 * n + 12]
    xn_scr, cv_scr, y_scr = refs[2 * n + 12:]
    b = pl.program_id(0)

    @pl.when(b == 0)
    def _():
        xn_scr[...] = _rms(x_ref[...], npre_ref[layer:layer + 1, :]).astype(BF16)

    w_bg = wbg_ref[...].astype(BF16)
    w_cg = wcg_ref[...].astype(BF16)
    w_v = wv_ref[...].astype(BF16)
    w_out_rows = wout_ref[...].astype(BF16)
    wbg_b_ref[...] = w_bg
    wcg_b_ref[...] = w_cg
    wv_b_ref[...] = w_v
    wout_b_ref[...] = w_out_rows

    xn = xn_scr[...]
    cv = _dot(xn, w_cg) * _dot(xn, w_v)
    convs = []
    for seg, c_in, c_out in zip(segs, conv_in, conv_out):
        P = (CONV_B - 1) * seg.S
        cv_scr[seg.conv0:seg.conv0 + P, :] = c_in[...]
        convs.append(_sconv_conv(seg, cv[seg.row0:seg.row0 + seg.rows], cw_ref[...], cv_scr))
        c_out[...] = cv_scr[seg.conv0:seg.conv0 + P, :]
    m = (_dot(xn, w_bg) * jnp.concatenate(convs, axis=0)).astype(BF16)
    _accumulate(y_scr, slice(None), _dot(m, w_out_rows), b)

    @pl.when(b == N_COL_BLOCKS - 1)
    def _():
        o_ref[...] = x_ref[...] + _rms(y_scr[...], npost_ref[layer:layer + 1, :])


def _sconv_stream_layer(x, conv_in, layer, j, w, *, seqs_and_steps):
    R, D = x.shape
    C = COL_BLOCK
    segs, _, conv_rows, _ = _segments(seqs_and_steps, CONV_B)
    n_layers = w["norm_mix_pre"].shape[0]
    col = lambda rows: pl.BlockSpec((rows, C), lambda b: (0, b))
    w_in_part = lambda k: pl.BlockSpec((None, D, C), lambda b: (j, 0, k * N_COL_BLOCKS + b))
    outs = pl.pallas_call(
        functools.partial(_sconv_stream_kernel, segs, layer, j),
        grid=(N_COL_BLOCKS,),
        in_specs=[_resident((R, D))] + [col(c.shape[0]) for c in conv_in]
                 + [_resident((n_layers, D)), _resident((n_layers, D)), w_in_part(0), w_in_part(1), w_in_part(2),
                    pl.BlockSpec((None, CONV_B, C), lambda b: (j, 0, b)),
                    pl.BlockSpec((None, C, D), lambda b: (j, b, 0))],
        out_specs=[pl.BlockSpec((R, D), lambda b: (0, 0))] + [col(c.shape[0]) for c in conv_in]
                  + [_chunk_major_out(D)] * 3 + [pl.BlockSpec((C, D), lambda b: (b, 0))],
        out_shape=[jax.ShapeDtypeStruct((R, D), F32)] + [jax.ShapeDtypeStruct(c.shape, F32) for c in conv_in]
                  + [jax.ShapeDtypeStruct((N_COL_BLOCKS, D, C), BF16)] * 3 + [jax.ShapeDtypeStruct((D, D), BF16)],
        scratch_shapes=[pltpu.VMEM((R, D), BF16),
                        pltpu.VMEM((conv_rows, C), F32),
                        pltpu.VMEM((R, D), F32)],
        compiler_params=_params(),
        name="sconv_mixer_small",
    )(x, *conv_in, w["norm_mix_pre"], w["norm_mix_post"], w["sc_w_in"], w["sc_w_in"], w["sc_w_in"],
      w["sc_conv_w"], w["sc_w_out"])
    n = len(segs)
    bf16 = dict(w_bg=outs[n + 1], w_cg=outs[n + 2], w_v=outs[n + 3], w_out=outs[n + 4])
    return outs[0], list(outs[1:1 + n]), bf16


def _sconv_small_kernel(segs, layer, j, x_ref, *refs):
    D, C, NB = D_MODEL, COL_BLOCK, N_COL_BLOCKS
    n = len(segs)
    conv_in = refs[0:n]
    npre_ref, npost_ref, cw_ref, win_hbm, wout_hbm = refs[n:n + 5]
    o_ref = refs[n + 5]
    conv_out = refs[n + 6:2 * n + 6]
    wbg_b_hbm, wcg_b_hbm, wv_b_hbm, wout_b_hbm = refs[2 * n + 6:2 * n + 10]
    xn_scr, cv_scr, col_f, row_f, col_h, row_h, in_sems, out_sems = refs[2 * n + 10:]

    def fetch(b):
        return ([pltpu.make_async_copy(win_hbm.at[j, :, pl.ds(k * D + b * C, C)], col_f.at[b, k], in_sems.at[b, k])
                 for k in range(3)]
                + [pltpu.make_async_copy(wout_hbm.at[j, pl.ds(b * C, C), :], row_f.at[b], in_sems.at[b, 3])])

    def emit(b):
        return ([pltpu.make_async_copy(col_h.at[b, k], dst.at[b], out_sems.at[b, k])
                 for k, dst in enumerate((wbg_b_hbm, wcg_b_hbm, wv_b_hbm))]
                + [pltpu.make_async_copy(row_h.at[b], wout_b_hbm.at[pl.ds(b * C, C), :], out_sems.at[b, 3])])

    for b in range(NB):
        _start_all(fetch(b))
    xn_scr[...] = _rms(x_ref[...], npre_ref[layer:layer + 1, :]).astype(BF16)
    xn = xn_scr[...]

    y = None
    for b in range(NB):
        cols = slice(b * C, (b + 1) * C)
        _wait_all(fetch(b))
        w_bg, w_cg, w_v = (col_f[b, k].astype(BF16) for k in range(3))
        w_out_rows = row_f[b].astype(BF16)
        col_h[b, 0], col_h[b, 1], col_h[b, 2], row_h[b] = w_bg, w_cg, w_v, w_out_rows
        _start_all(emit(b))

        cv = _dot(xn, w_cg) * _dot(xn, w_v)
        convs = []
        for seg, c_in, c_out in zip(segs, conv_in, conv_out):
            P = (CONV_B - 1) * seg.S
            cv_scr[seg.conv0:seg.conv0 + P, :] = c_in[:, cols]
            convs.append(_sconv_conv(seg, cv[seg.row0:seg.row0 + seg.rows], cw_ref[:, cols], cv_scr))
            c_out[:, cols] = cv_scr[seg.conv0:seg.conv0 + P, :]
        m = (_dot(xn, w_bg) * jnp.concatenate(convs, axis=0)).astype(BF16)
        part = _dot(m, w_out_rows)
        y = part if y is None else y + part

    for b in range(NB):
        _wait_all(emit(b))
    o_ref[...] = x_ref[...] + _rms(y, npost_ref[layer:layer + 1, :])


def _sconv_small_layer(x, conv_in, layer, j, w, *, seqs_and_steps):
    R, D = x.shape
    C, NB = COL_BLOCK, N_COL_BLOCKS
    segs, _, conv_rows, _ = _segments(seqs_and_steps, CONV_B)
    n_layers = w["norm_mix_pre"].shape[0]
    hbm = pl.BlockSpec(memory_space=pl.ANY)
    outs = pl.pallas_call(
        functools.partial(_sconv_small_kernel, segs, layer, j),
        grid=(1,),
        in_specs=[_resident((R, D))] + [_resident(c.shape) for c in conv_in]
                 + [_resident((n_layers, D)), _resident((n_layers, D)),
                    pl.BlockSpec((None, CONV_B, D), lambda i: (j, 0, 0), pipeline_mode=pl.Buffered(1)), hbm, hbm],
        out_specs=[pl.BlockSpec((R, D), lambda i: (0, 0))]
                  + [pl.BlockSpec(c.shape, lambda i: (0, 0)) for c in conv_in] + [hbm] * 4,
        out_shape=[jax.ShapeDtypeStruct((R, D), F32)] + [jax.ShapeDtypeStruct(c.shape, F32) for c in conv_in]
                  + [jax.ShapeDtypeStruct((NB, D, C), BF16)] * 3 + [jax.ShapeDtypeStruct((D, D), BF16)],
        scratch_shapes=[pltpu.VMEM((R, D), BF16),
                        pltpu.VMEM((conv_rows, C), F32),
                        pltpu.VMEM((NB, 3, D, C), F32),
                        pltpu.VMEM((NB, C, D), F32),
                        pltpu.VMEM((NB, 3, D, C), BF16), pltpu.VMEM((NB, C, D), BF16),
                        pltpu.SemaphoreType.DMA((NB, 4)), pltpu.SemaphoreType.DMA((NB, 4))],
        compiler_params=_params(),
        name="sconv_mixer_small",
    )(x, *conv_in, w["norm_mix_pre"], w["norm_mix_post"], w["sc_conv_w"], w["sc_w_in"], w["sc_w_out"])
    n = len(segs)
    bf16 = dict(w_bg=outs[n + 1], w_cg=outs[n + 2], w_v=outs[n + 3], w_out=outs[n + 4])
    return outs[0], list(outs[1:1 + n]), bf16


def _sconv_kernel(seg, nblk, layer, x_ref, conv_in_ref, npre_ref, npost_ref, wbg_ref, wcg_ref, wv_ref,
                  cw_ref, wout_ref, o_ref, conv_out_ref, xn_scr, cv_scr, m_scr):
    P = (CONV_B - 1) * seg.S
    i = pl.program_id(0)
    blocks = [slice(b * COL_BLOCK, (b + 1) * COL_BLOCK) for b in range(N_COL_BLOCKS)]

    @pl.when(i == 0)
    def _():
        for b, cols in enumerate(blocks):
            cv_scr[b, 0:P, :] = conv_in_ref[:, cols]

    xn_scr[...] = _rms(x_ref[...], npre_ref[layer:layer + 1, :]).astype(BF16)
    xn = xn_scr[...]
    for b, cols in enumerate(blocks):
        cv = _dot(xn, wcg_ref[b]) * _dot(xn, wv_ref[b])
        conv = _sconv_conv(seg, cv, cw_ref[:, cols], cv_scr.at[b])
        m_scr[:, cols] = (_dot(xn, wbg_ref[b]) * conv).astype(BF16)
    y = _dot(m_scr[...], wout_ref[...])
    o_ref[...] = x_ref[...] + _rms(y, npost_ref[layer:layer + 1, :])

    @pl.when(i == nblk - 1)
    def _():
        for b, cols in enumerate(blocks):
            conv_out_ref[:, cols] = cv_scr[b, 0:P, :]


def _sconv_layer(x, conv_in, layer, j, w, bf16, *, S, TB):
    N, D = x.shape
    (seg,), R, conv_rows, _ = _segments(((S, TB),), CONV_B)
    nblk = N // R
    P = (CONV_B - 1) * S
    n_layers = w["norm_mix_pre"].shape[0]
    row_spec = pl.BlockSpec((R, D), lambda i: (i, 0))
    return pl.pallas_call(
        functools.partial(_sconv_kernel, seg, nblk, layer),
        grid=(nblk,),
        in_specs=[row_spec, _resident((P, D)), _resident((n_layers, D)), _resident((n_layers, D)),
                  _resident(bf16["w_bg"].shape), _resident(bf16["w_cg"].shape), _resident(bf16["w_v"].shape),
                  pl.BlockSpec((None, CONV_B, D), lambda i: (j, 0, 0), pipeline_mode=pl.Buffered(1)),
                  _resident((D, D))],
        out_specs=[row_spec, pl.BlockSpec((P, D), lambda i: (0, 0))],
        out_shape=[jax.ShapeDtypeStruct((N, D), F32), jax.ShapeDtypeStruct((P, D), F32)],
        scratch_shapes=[pltpu.VMEM((R, D), BF16),
                        pltpu.VMEM((N_COL_BLOCKS, conv_rows, COL_BLOCK), F32),
                        pltpu.VMEM((R, D), BF16)],
        compiler_params=_params(),
        name="sconv_mixer",
    )(x, conv_in, w["norm_mix_pre"], w["norm_mix_post"], bf16["w_bg"], bf16["w_cg"], bf16["w_v"],
      w["sc_conv_w"], bf16["w_out"])


def _swiglu(g, u):
    return (g * jax.nn.sigmoid(g) * u).astype(BF16)


def _ffn_stream_kernel(layer, nchunk, sample_out, x_ref, npre_ref, npost_ref, wg_hbm, wu_hbm, wd_hbm,
                       o_ref, wg_b_hbm, wu_b_hbm, wd_b_hbm,
                       xn_scr, wg_f, wu_f, wd_f, wg_h, wu_h, wd_h, in_sems, out_sems, *out_staging):
    F = FF_CHUNK

    NB = STREAM_BUFFERS

    def fetch(c):
        k, cols = c % NB, pl.ds(c * F, F)
        return [pltpu.make_async_copy(wg_hbm.at[layer, :, cols], wg_f.at[k], in_sems.at[0, k]),
                pltpu.make_async_copy(wu_hbm.at[layer, :, cols], wu_f.at[k], in_sems.at[1, k]),
                pltpu.make_async_copy(wd_hbm.at[layer, cols, :], wd_f.at[k], in_sems.at[2, k])]

    def emit(c):
        k = c % NB
        return [pltpu.make_async_copy(wg_h.at[k], wg_b_hbm.at[c], out_sems.at[0, k]),
                pltpu.make_async_copy(wu_h.at[k], wu_b_hbm.at[c], out_sems.at[1, k]),
                pltpu.make_async_copy(wd_h.at[k], wd_b_hbm.at[pl.ds(c * F, F), :], out_sems.at[2, k])]

    for c in range(min(NB, nchunk)):
        for cp in fetch(c):
            cp.start()
    xn_scr[...] = _rms(x_ref[...], npre_ref[layer:layer + 1, :]).astype(BF16)
    xn = xn_scr[...]

    y = None
    for c in range(nchunk):
        k = c % NB
        for cp in fetch(c):
            cp.wait()
        if c >= NB:
            for cp in emit(c - NB):
                cp.wait()
        wg, wu, wd = wg_f[k].astype(BF16), wu_f[k].astype(BF16), wd_f[k].astype(BF16)
        wg_h[k], wu_h[k], wd_h[k] = wg, wu, wd
        for cp in emit(c):
            cp.start()
        if c + NB < nchunk:
            for cp in fetch(c + NB):
                cp.start()
        part = _dot(_swiglu(_dot(xn, wg), _dot(xn, wu)), wd)
        y = part if y is None else y + part
    for c in range(max(nchunk - NB, 0), nchunk):
        for cp in emit(c):
            cp.wait()

    out = x_ref[...] + _rms(y, npost_ref[layer:layer + 1, :])
    if sample_out is None:
        o_ref[...] = out
    else:
        row0, S, T = sample_out
        obuf, sems = out_staging
        obuf[...] = out[row0:row0 + T * S].reshape(T, S, out.shape[-1])
        copies = [pltpu.make_async_copy(obuf.at[t], o_ref.at[:, t, :], sems.at[t]) for t in range(T)]
        for cp in copies:
            cp.start()
        for cp in copies:
            cp.wait()


def _ffn_stream_layer(x, layer, w, *, sample_out=None):
    R, D = x.shape
    F, NB = FF_CHUNK, STREAM_BUFFERS
    n_layers = w["norm_ffn_pre"].shape[0]
    nchunk = D_FF // F
    hbm = pl.BlockSpec(memory_space=pl.ANY)
    if sample_out is None:
        o_spec, o_shape, staging = pl.BlockSpec((R, D), lambda i: (0, 0)), jax.ShapeDtypeStruct((R, D), F32), []
    else:
        _, S, T = sample_out
        o_spec, o_shape = hbm, jax.ShapeDtypeStruct((S, T, D), F32)
        staging = [pltpu.VMEM((T, S, D), F32), pltpu.SemaphoreType.DMA((T,))]
    outs = pl.pallas_call(
        functools.partial(_ffn_stream_kernel, layer, nchunk, sample_out),
        grid=(1,),
        in_specs=[_resident((R, D)), _resident((n_layers, D)), _resident((n_layers, D)), hbm, hbm, hbm],
        out_specs=[o_spec, hbm, hbm, hbm],
        out_shape=[o_shape, jax.ShapeDtypeStruct((nchunk, D, F), BF16),
                   jax.ShapeDtypeStruct((nchunk, D, F), BF16), jax.ShapeDtypeStruct((D_FF, D), BF16)],
        scratch_shapes=[pltpu.VMEM((R, D), BF16),
                        pltpu.VMEM((NB, D, F), F32), pltpu.VMEM((NB, D, F), F32), pltpu.VMEM((NB, F, D), F32),
                        pltpu.VMEM((NB, D, F), BF16), pltpu.VMEM((NB, D, F), BF16), pltpu.VMEM((NB, F, D), BF16),
                        pltpu.SemaphoreType.DMA((3, NB)), pltpu.SemaphoreType.DMA((3, NB))] + staging,
        compiler_params=_params(),
        name="swiglu_ffn_small",
    )(x, w["norm_ffn_pre"], w["norm_ffn_post"], w["ffn_w_gate"], w["ffn_w_up"], w["ffn_w_down"])
    return outs[0], dict(wg=outs[1], wu=outs[2], wd=outs[3])


def _ffn_kernel(layer, nblk, batch_major_out, x_ref, npre_ref, npost_ref, wg_ref, wu_ref, wd_ref, o_ref,
                xn_scr, act_scr, *out_staging):
    R = x_ref.shape[0]
    i = pl.program_id(0)
    slot = i % 2
    if batch_major_out:
        obuf, sems = out_staging
        S, TB = o_ref.shape[0], R // o_ref.shape[0]

        def copies(which, step):
            t0 = pl.multiple_of(step * TB, TB)
            return [pltpu.make_async_copy(obuf.at[which, :, s, :],
                                          o_ref.at[s, pl.ds(t0, TB), :], sems.at[which, s]) for s in range(S)]

        @pl.when(i >= 2)
        def _():
            for c in copies(slot, i - 2):
                c.wait()

    halves = [slice(0, R // 2), slice(R // 2, R)]
    for rows in halves:
        xn_scr[rows, :] = _rms(x_ref[rows, :], npre_ref[layer:layer + 1, :]).astype(BF16)
    for rows in halves:
        xn = xn_scr[rows, :]
        for c in range(D_FF // FF_CHUNK):
            cols = slice(c * FF_CHUNK, (c + 1) * FF_CHUNK)
            act_scr[rows, cols] = _swiglu(_dot(xn, wg_ref[c]), _dot(xn, wu_ref[c]))
    ys = [_dot(act_scr[rows, :], wd_ref[...]) for rows in halves]
    for rows, y in zip(halves, ys):
        out = x_ref[rows, :] + _rms(y, npost_ref[layer:layer + 1, :])
        if batch_major_out:
            obuf[slot, rows.start // S:rows.stop // S] = out.reshape(-1, S, out.shape[-1])
        else:
            o_ref[rows, :] = out

    if batch_major_out:
        for c in copies(slot, i):
            c.start()

        @pl.when(i == nblk - 1)
        def _():
            if nblk >= 2:
                for c in copies(1 - slot, i - 1):
                    c.wait()
            for c in copies(slot, i):
                c.wait()


def _ffn_layer(x, layer, w, bf16, *, R, batch_major_seqs=None):
    N, D = x.shape
    n_layers = w["norm_ffn_pre"].shape[0]
    nblk = N // R
    row_spec = pl.BlockSpec((R, D), lambda i: (i, 0))
    if batch_major_seqs is None:
        out_spec, out_shape, staging = row_spec, jax.ShapeDtypeStruct((N, D), F32), []
    else:
        S = batch_major_seqs
        out_spec = pl.BlockSpec(memory_space=pl.ANY)
        out_shape = jax.ShapeDtypeStruct((S, N // S, D), F32)
        staging = [pltpu.VMEM((2, R // S, S, D), F32), pltpu.SemaphoreType.DMA((2, S))]
    return pl.pallas_call(
        functools.partial(_ffn_kernel, layer, nblk, batch_major_seqs is not None),
        grid=(nblk,),
        in_specs=[row_spec, _resident((n_layers, D)), _resident((n_layers, D)),
                  _resident(bf16["wg"].shape), _resident(bf16["wu"].shape), _resident((D_FF, D))],
        out_specs=out_spec,
        out_shape=out_shape,
        scratch_shapes=[pltpu.VMEM((R, D), BF16), pltpu.VMEM((R, D_FF), BF16)] + staging,
        compiler_params=_params(),
        name="swiglu_ffn",
    )(x, w["norm_ffn_pre"], w["norm_ffn_post"], bf16["wg"], bf16["wu"], bf16["wd"])


def _to_time_major(a):
    S, K, D = a.shape
    return jnp.swapaxes(a, 0, 1).reshape(K * S, D)


def _from_time_major(a, S):
    KS, D = a.shape
    return jnp.swapaxes(a.reshape(KS // S, S, D), 0, 1)


def kernel(x_prompt, x_sample, state_rglru_conv, state_rglru_h, state_sconv, meta_tokens, norm_mix_pre, norm_mix_post, norm_ffn_pre, norm_ffn_post, rg_w_in, rg_conv_w, rg_conv_b, rg_gate_a_w, rg_gate_a_b, rg_gate_x_w, rg_gate_x_b, rg_lambda, rg_w_out, sc_w_in, sc_conv_w, sc_w_out, ffn_w_gate, ffn_w_up, ffn_w_down):
    D = D_MODEL
    depth = norm_mix_pre.shape[0]
    batch, seq, _ = x_prompt.shape
    dec_batch, dec_seq, _ = x_sample.shape
    w = dict(norm_mix_pre=norm_mix_pre, norm_mix_post=norm_mix_post, norm_ffn_pre=norm_ffn_pre,
             norm_ffn_post=norm_ffn_post, rg_w_in=rg_w_in, rg_conv_w=rg_conv_w, rg_conv_b=rg_conv_b,
             rg_gate_a_w=rg_gate_a_w, rg_gate_a_b=rg_gate_a_b, rg_gate_x_w=rg_gate_x_w, rg_gate_x_b=rg_gate_x_b,
             rg_lambda=rg_lambda, rg_w_out=rg_w_out, sc_w_in=sc_w_in, sc_conv_w=sc_conv_w, sc_w_out=sc_w_out,
             ffn_w_gate=ffn_w_gate, ffn_w_up=ffn_w_up, ffn_w_down=ffn_w_down)

    x = jnp.broadcast_to(meta_tokens[:, None, :], (N_META, batch, D)).reshape(N_META * batch, D)
    small = ((batch, N_META), (dec_batch, dec_seq))
    mixer_bf16, ffn_bf16 = [], []
    rg_conv_s, rg_h_s, sc_s = [], [], []
    for i in range(depth):
        j = i // 2
        if i % 2 == 0:
            conv0 = [jnp.zeros(((CONV_A - 1) * batch, D), F32), _to_time_major(state_rglru_conv[j])]
            h0 = [jnp.zeros((batch, D), F32), state_rglru_h[j]]
            x, cb, hT, wb = _rglru_small_layer(x, conv0, h0, i, j, w, seqs_and_steps=small,
                                                x_sample=x_sample if i == 0 else None)
            rg_conv_s.append(cb)
            rg_h_s.append(hT)
        else:
            conv0 = [jnp.zeros(((CONV_B - 1) * batch, D), F32), _to_time_major(state_sconv[j])]
            x, cb, wb = _sconv_small_layer(x, conv0, i, j, w, seqs_and_steps=small)
            sc_s.append(cb)
        mixer_bf16.append(wb)
        sample_out = (N_META * batch, dec_batch, dec_seq) if i == depth - 1 else None
        x, wb = _ffn_stream_layer(x, i, w, sample_out=sample_out)
        ffn_bf16.append(wb)
    y_sample = x

    x = x_prompt
    rg_conv_p, rg_h_p, sc_p = [], [], []
    for i in range(depth):
        j = i // 2
        if i % 2 == 0:
            x, cb, hT = _rglru_layer(x, rg_conv_s[j][0], rg_h_s[j][0], i, j, w, mixer_bf16[i], S=batch, TB=PROMPT_TB)
            rg_conv_p.append(cb)
            rg_h_p.append(hT)
        else:
            x, cb = _sconv_layer(x, sc_s[j][0], i, j, w, mixer_bf16[i], S=batch, TB=PROMPT_TB)
            sc_p.append(cb)
        x = _ffn_layer(x, i, w, ffn_bf16[i], R=batch * PROMPT_TB,
                       batch_major_seqs=batch if i == depth - 1 else None)
    y_prompt = x

    return (y_prompt, y_sample,
            jnp.stack([_from_time_major(c, batch) for c in rg_conv_p]), jnp.stack(rg_h_p),
            jnp.stack([_from_time_major(c, batch) for c in sc_p]),
            jnp.stack([_from_time_major(c[1], dec_batch) for c in rg_conv_s]), jnp.stack([h[1] for h in rg_h_s]),
            jnp.stack([_from_time_major(c[1], dec_batch) for c in sc_s]))
```

```python
import functools
from typing import NamedTuple

import jax
import jax.numpy as jnp
from jax import lax
from jax.experimental import pallas as pl
from jax.experimental.pallas import tpu as pltpu

D_MODEL = 1024
D_FF = 2816
N_META = 16
COL_BLOCK = 256
N_COL_BLOCKS = D_MODEL // COL_BLOCK
CONV_A = 4
CONV_B = 3
RG_C = 8.0
EPS = 1e-6

SUBLANES = 8
FF_CHUNK = 256
STREAM_BUFFERS = 3
PROMPT_TB = 128
VMEM_LIMIT_BYTES = 56 * 1024 * 1024

F32 = jnp.float32
BF16 = jnp.bfloat16


class _Seg(NamedTuple):
    S: int
    TB: int
    row0: int
    conv0: int
    h0: int

    @property
    def rows(self):
        return self.S * self.TB


def _segments(seqs_and_steps, taps):
    segs, row0, conv0, h0 = [], 0, 0, 0
    for S, TB in seqs_and_steps:
        segs.append(_Seg(S, TB, row0, conv0, h0))
        row0 += S * TB
        conv0 += (taps - 1) * S + S * TB
        h0 += S
    return tuple(segs), row0, conv0, h0


def _rms(x, w):
    ms = jnp.mean(x * x, axis=-1, keepdims=True)
    return x * lax.rsqrt(ms + EPS) * w


def _dot(a, b):
    return jnp.dot(a, b, preferred_element_type=F32)


def _gelu_tanh(x):
    c = 0.7978845608028654
    hx = 0.5 * x
    return hx + hx * jnp.tanh(x * (c + (c * 0.044715) * (x * x)))


def _resident(shape):
    zeros = (0,) * len(shape)
    return pl.BlockSpec(shape, lambda i: zeros, pipeline_mode=pl.Buffered(1))


def _chunk_major_out(rows, width=COL_BLOCK):
    return pl.BlockSpec((None, rows, width), lambda i: (i, 0, 0))


def _params():
    return pltpu.CompilerParams(dimension_semantics=("arbitrary",), vmem_limit_bytes=VMEM_LIMIT_BYTES)


def _accumulate(y_scr, rows, part, step):
    @pl.when(step == 0)
    def _():
        y_scr[rows, :] = part

    @pl.when(step > 0)
    def _():
        y_scr[rows, :] += part


def _rglru_branch_in(seg, xn, w_xr, region):
    P = (CONV_A - 1) * seg.S
    region[seg.conv0 + P:seg.conv0 + P + seg.rows, :] = _dot(xn, w_xr)


def _rglru_conv(seg, cw, cb, region, carry_to):
    S, R = seg.S, seg.rows
    P = (CONV_A - 1) * S
    c0 = seg.conv0
    xc = region[c0:c0 + R, :] * cw[0:1]
    for k in range(1, CONV_A):
        xc = xc + region[c0 + k * S:c0 + k * S + R, :] * cw[k:k + 1]
    xc = xc + cb
    if carry_to is not None:
        region[carry_to:carry_to + P, :] = region[c0 + R:c0 + R + P, :]
    return xc


def _rglru_gate_math(res, xc, half_gab, half_gxb, lam):
    half_c_sp = (-0.5 * RG_C) * jax.nn.softplus(-lam)
    tr = jnp.tanh(res[:, 0:COL_BLOCK] + half_gab)
    log_a = half_c_sp * tr + half_c_sp
    ig = 0.5 * jnp.tanh(res[:, COL_BLOCK:2 * COL_BLOCK] + half_gxb) + 0.5
    a = jnp.exp(log_a)
    m2 = jnp.tanh(log_a) * (-1.0 - a * a)
    u = jnp.where(m2 > 0.0, m2 * lax.rsqrt(m2), 0.0) * (ig * xc)
    return a, u


def _rglru_scan(seg, a, u, g, h_read, h_write, h_row0):
    S, TB = seg.S, seg.TB
    groups = S // SUBLANES
    pieces = [None] * (TB * groups)
    for c in range(groups):
        hrows = slice(h_row0 + c * SUBLANES, h_row0 + (c + 1) * SUBLANES)
        h = h_read[hrows, :]
        for t in range(TB):
            r = t * S + c * SUBLANES
            h = a[r:r + SUBLANES] * h + u[r:r + SUBLANES]
            pieces[t * groups + c] = h * g[r:r + SUBLANES]
        h_write[hrows, :] = h
    return jnp.concatenate(pieces, axis=0)


def _rglru_stream_kernel(segs, sample_in, layer, j, x_ref, *refs):
    if sample_in is not None:
        xs_hbm, refs = refs[0], refs[1:]
    n = len(segs)
    conv_in, h_in = refs[0:n], refs[n:2 * n]
    (npre_ref, npost_ref, wxr_ref, wgate_ref, cw_ref, cb_ref, gaw_ref, gxw_ref, gab_ref, gxb_ref,
     lam_ref, wout_ref) = refs[2 * n:2 * n + 12]
    o_ref = refs[2 * n + 12]
    conv_out, h_out = refs[2 * n + 13:3 * n + 13], refs[3 * n + 13:4 * n + 13]
    wxr_b_ref, wgate_b_ref, gw_b_ref, wout_b_ref = refs[4 * n + 13:4 * n + 17]
    xn_scr, xr_scr, y_scr, *in_staging = refs[4 * n + 17:]
    b = pl.program_id(0)

    if sample_in is None:
        read_x = lambda: x_ref[...]
    else:
        S, T = sample_in
        xs_scr, sems = in_staging
        read_x = lambda: jnp.concatenate([x_ref[...], xs_scr[...].reshape(T * S, D_MODEL)], axis=0)

    @pl.when(b == 0)
    def _():
        if sample_in is not None:
            copies = [pltpu.make_async_copy(xs_hbm.at[:, t, :], xs_scr.at[t], sems.at[t]) for t in range(T)]
            for cp in copies:
                cp.start()
            for cp in copies:
                cp.wait()
        xn_scr[...] = _rms(read_x(), npre_ref[layer:layer + 1, :]).astype(BF16)

    w_xr = wxr_ref[...].astype(BF16)
    w_gate = wgate_ref[...].astype(BF16)
    gw = (0.5 * jnp.concatenate([gaw_ref[...], gxw_ref[...]], axis=1)).astype(BF16)
    w_out_rows = wout_ref[...].astype(BF16)
    wxr_b_ref[...] = w_xr
    wgate_b_ref[...] = w_gate
    gw_b_ref[...] = gw
    wout_b_ref[...] = w_out_rows

    rows_of = lambda seg: slice(seg.row0, seg.row0 + seg.rows)
    xn = xn_scr[...]
    xr = _dot(xn, w_xr)
    g = _gelu_tanh(_dot(xn, w_gate))
    xcs = []
    for seg, c_in, c_out in zip(segs, conv_in, conv_out):
        P = (CONV_A - 1) * seg.S
        xr_scr[seg.conv0:seg.conv0 + P, :] = c_in[...]
        xr_scr[seg.conv0 + P:seg.conv0 + P + seg.rows, :] = xr[rows_of(seg)]
        xcs.append(_rglru_conv(seg, cw_ref[...], cb_ref[j:j + 1, :], xr_scr, seg.conv0))
        c_out[...] = xr_scr[seg.conv0:seg.conv0 + P, :]
    xc = jnp.concatenate(xcs, axis=0)
    a, u = _rglru_gate_math(_dot(xc.astype(BF16), gw), xc, 0.5 * gab_ref[pl.ds(b, 1), :],
                            0.5 * gxb_ref[pl.ds(b, 1), :], lam_ref[j:j + 1, :])
    hs = [_rglru_scan(seg, a[rows_of(seg)], u[rows_of(seg)], g[rows_of(seg)], hi, ho, 0)
          for seg, hi, ho in zip(segs, h_in, h_out)]
    part = _dot(jnp.concatenate(hs, axis=0).astype(BF16), w_out_rows)
    _accumulate(y_scr, slice(None), part, b)

    @pl.when(b == N_COL_BLOCKS - 1)
    def _():
        o_ref[...] = read_x() + _rms(y_scr[...], npost_ref[layer:layer + 1, :])


def _rglru_stream_layer(x, conv_in, h_in, layer, j, w, *, seqs_and_steps, x_sample=None):
    D, C = D_MODEL, COL_BLOCK
    segs, R, conv_rows, _ = _segments(seqs_and_steps, CONV_A)
    n_layers, n_a = w["norm_mix_pre"].shape[0], w["rg_conv_b"].shape[0]
    col = lambda rows: pl.BlockSpec((rows, C), lambda b: (0, b))
    if x_sample is None:
        x_args, x_specs, sample_in, staging = [x], [_resident(x.shape)], None, []
    else:
        S, T, _ = x_sample.shape
        x_args, x_specs, sample_in = [x, x_sample], [_resident(x.shape), pl.BlockSpec(memory_space=pl.ANY)], (S, T)
        staging = [pltpu.VMEM((T, S, D), F32), pltpu.SemaphoreType.DMA((T,))]
    in_specs = (
        x_specs + [col(c.shape[0]) for c in conv_in] + [col(h.shape[0]) for h in h_in]
        + [_resident((n_layers, D)), _resident((n_layers, D)),
           pl.BlockSpec((None, D, C), lambda b: (j, 0, N_COL_BLOCKS + b)),
           pl.BlockSpec((None, D, C), lambda b: (j, 0, b)),
           pl.BlockSpec((None, CONV_A, C), lambda b: (j, 0, b)),
           col(n_a),
           pl.BlockSpec((None, None, C, C), lambda b: (j, b, 0, 0)),
           pl.BlockSpec((None, None, C, C), lambda b: (j, b, 0, 0)),
           pl.BlockSpec((None, N_COL_BLOCKS, C), lambda b: (j, 0, 0)),
           pl.BlockSpec((None, N_COL_BLOCKS, C), lambda b: (j, 0, 0)),
           col(n_a),
           pl.BlockSpec((None, C, D), lambda b: (j, b, 0))])
    out_specs = ([pl.BlockSpec((R, D), lambda b: (0, 0))] + [col(c.shape[0]) for c in conv_in]
                 + [col(h.shape[0]) for h in h_in]
                 + [_chunk_major_out(D), _chunk_major_out(D), pl.BlockSpec((None, C, 2 * C), lambda b: (b, 0, 0)),
                    pl.BlockSpec((C, D), lambda b: (b, 0))])
    out_shape = ([jax.ShapeDtypeStruct((R, D), F32)] + [jax.ShapeDtypeStruct(s.shape, F32) for s in conv_in + h_in]
                 + [jax.ShapeDtypeStruct((N_COL_BLOCKS, D, C), BF16), jax.ShapeDtypeStruct((N_COL_BLOCKS, D, C), BF16),
                    jax.ShapeDtypeStruct((N_COL_BLOCKS, C, 2 * C), BF16), jax.ShapeDtypeStruct((D, D), BF16)])
    outs = pl.pallas_call(
        functools.partial(_rglru_stream_kernel, segs, sample_in, layer, j),
        grid=(N_COL_BLOCKS,),
        in_specs=in_specs, out_specs=out_specs, out_shape=out_shape,
        scratch_shapes=[pltpu.VMEM((R, D), BF16),
                        pltpu.VMEM((conv_rows, C), F32),
                        pltpu.VMEM((R, D), F32)]
                       + staging,
        compiler_params=_params(),
        name="rglru_mixer_small",
    )(*x_args, *conv_in, *h_in, w["norm_mix_pre"], w["norm_mix_post"], w["rg_w_in"], w["rg_w_in"], w["rg_conv_w"],
      w["rg_conv_b"], w["rg_gate_a_w"], w["rg_gate_x_w"], w["rg_gate_a_b"], w["rg_gate_x_b"], w["rg_lambda"],
      w["rg_w_out"])
    n = len(segs)
    bf16 = dict(w_xr=outs[2 * n + 1], w_gate=outs[2 * n + 2], gw=outs[2 * n + 3], w_out=outs[2 * n + 4])
    return outs[0], list(outs[1:1 + n]), list(outs[1 + n:1 + 2 * n]), bf16


def _start_all(copies):
    for cp in copies:
        cp.start()


def _wait_all(copies):
    for cp in copies:
        cp.wait()


def _rglru_small_kernel(segs, sample_in, layer, j, x_ref, *refs):
    D, C, NB = D_MODEL, COL_BLOCK, N_COL_BLOCKS
    if sample_in is not None:
        xs_hbm, refs = refs[0], refs[1:]
    n = len(segs)
    conv_in, h_in = refs[0:n], refs[n:2 * n]
    (npre_ref, npost_ref, cw_ref, cb_ref, gab_ref, gxb_ref, lam_ref,
     win_hbm, gaw_hbm, gxw_hbm, wout_hbm) = refs[2 * n:2 * n + 11]
    o_ref = refs[2 * n + 11]
    conv_out, h_out = refs[2 * n + 12:3 * n + 12], refs[3 * n + 12:4 * n + 12]
    wxr_b_hbm, wgate_b_hbm, gw_b_hbm, wout_b_hbm = refs[4 * n + 12:4 * n + 16]
    (xn_scr, xr_scr, col_f, sq_f, row_f, col_h, gw_h, row_h, in_sems, out_sems,
     *in_staging) = refs[4 * n + 16:]

    def fetch(b):
        cols = pl.ds(b * C, C)
        return [pltpu.make_async_copy(win_hbm.at[j, :, pl.ds(D + b * C, C)], col_f.at[b, 0], in_sems.at[b, 0]),
                pltpu.make_async_copy(win_hbm.at[j, :, cols], col_f.at[b, 1], in_sems.at[b, 1]),
                pltpu.make_async_copy(gaw_hbm.at[j, b], sq_f.at[b, 0], in_sems.at[b, 2]),
                pltpu.make_async_copy(gxw_hbm.at[j, b], sq_f.at[b, 1], in_sems.at[b, 3]),
                pltpu.make_async_copy(wout_hbm.at[j, cols, :], row_f.at[b], in_sems.at[b, 4])]

    def emit(b):
        return [pltpu.make_async_copy(col_h.at[b, 0], wxr_b_hbm.at[b], out_sems.at[b, 0]),
                pltpu.make_async_copy(col_h.at[b, 1], wgate_b_hbm.at[b], out_sems.at[b, 1]),
                pltpu.make_async_copy(gw_h.at[b], gw_b_hbm.at[b], out_sems.at[b, 2]),
                pltpu.make_async_copy(row_h.at[b], wout_b_hbm.at[pl.ds(b * C, C), :], out_sems.at[b, 3])]

    if sample_in is None:
        read_x = lambda: x_ref[...]
    else:
        S, T = sample_in
        xs_scr, sems = in_staging
        x_copies = [pltpu.make_async_copy(xs_hbm.at[:, t, :], xs_scr.at[t], sems.at[t]) for t in range(T)]
        _start_all(x_copies)
        read_x = lambda: jnp.concatenate([x_ref[...], xs_scr[...].reshape(T * S, D)], axis=0)
    for b in range(NB):
        _start_all(fetch(b))
    if sample_in is not None:
        _wait_all(x_copies)
    xn_scr[...] = _rms(read_x(), npre_ref[layer:layer + 1, :]).astype(BF16)
    xn = xn_scr[...]

    rows_of = lambda seg: slice(seg.row0, seg.row0 + seg.rows)
    y = None
    for b in range(NB):
        cols = slice(b * C, (b + 1) * C)
        _wait_all(fetch(b))
        w_xr, w_gate = col_f[b, 0].astype(BF16), col_f[b, 1].astype(BF16)
        gw = (0.5 * jnp.concatenate([sq_f[b, 0], sq_f[b, 1]], axis=1)).astype(BF16)
        w_out_rows = row_f[b].astype(BF16)
        col_h[b, 0], col_h[b, 1], gw_h[b], row_h[b] = w_xr, w_gate, gw, w_out_rows
        _start_all(emit(b))

        xr = _dot(xn, w_xr)
        g = _gelu_tanh(_dot(xn, w_gate))
        xcs = []
        for seg, c_in, c_out in zip(segs, conv_in, conv_out):
            P = (CONV_A - 1) * seg.S
            xr_scr[seg.conv0:seg.conv0 + P, :] = c_in[:, cols]
            xr_scr[seg.conv0 + P:seg.conv0 + P + seg.rows, :] = xr[rows_of(seg)]
            xcs.append(_rglru_conv(seg, cw_ref[:, cols], cb_ref[j:j + 1, cols], xr_scr, seg.conv0))
            c_out[:, cols] = xr_scr[seg.conv0:seg.conv0 + P, :]
        xc = jnp.concatenate(xcs, axis=0)
        a, u = _rglru_gate_math(_dot(xc.astype(BF16), gw), xc, 0.5 * gab_ref[b:b + 1, :],
                                0.5 * gxb_ref[b:b + 1, :], lam_ref[j:j + 1, cols])
        hs = [_rglru_scan(seg, a[rows_of(seg)], u[rows_of(seg)], g[rows_of(seg)],
                          hi.at[:, cols], ho.at[:, cols], 0)
              for seg, hi, ho in zip(segs, h_in, h_out)]
        part = _dot(jnp.concatenate(hs, axis=0).astype(BF16), w_out_rows)
        y = part if y is None else y + part

    for b in range(NB):
        _wait_all(emit(b))
    o_ref[...] = read_x() + _rms(y, npost_ref[layer:layer + 1, :])


def _rglru_small_layer(x, conv_in, h_in, layer, j, w, *, seqs_and_steps, x_sample=None):
    D, C, NB = D_MODEL, COL_BLOCK, N_COL_BLOCKS
    segs, R, conv_rows, _ = _segments(seqs_and_steps, CONV_A)
    n_layers, n_a = w["norm_mix_pre"].shape[0], w["rg_conv_b"].shape[0]
    hbm = pl.BlockSpec(memory_space=pl.ANY)
    layer_of = lambda arr: pl.BlockSpec((None,) + arr.shape[1:], lambda i: (j,) + (0,) * (arr.ndim - 1),
                                        pipeline_mode=pl.Buffered(1))
    if x_sample is None:
        x_args, x_specs, sample_in, staging = [x], [_resident(x.shape)], None, []
    else:
        S, T, _ = x_sample.shape
        x_args, x_specs, sample_in = [x, x_sample], [_resident(x.shape), hbm], (S, T)
        staging = [pltpu.VMEM((T, S, D), F32), pltpu.SemaphoreType.DMA((T,))]
    states = conv_in + h_in
    outs = pl.pallas_call(
        functools.partial(_rglru_small_kernel, segs, sample_in, layer, j),
        grid=(1,),
        in_specs=x_specs + [_resident(s.shape) for s in states]
                 + [_resident((n_layers, D)), _resident((n_layers, D)), layer_of(w["rg_conv_w"]),
                    _resident((n_a, D)), layer_of(w["rg_gate_a_b"]), layer_of(w["rg_gate_x_b"]),
                    _resident((n_a, D)), hbm, hbm, hbm, hbm],
        out_specs=[pl.BlockSpec((R, D), lambda i: (0, 0))]
                  + [pl.BlockSpec(s.shape, lambda i: (0, 0)) for s in states] + [hbm] * 4,
        out_shape=[jax.ShapeDtypeStruct((R, D), F32)] + [jax.ShapeDtypeStruct(s.shape, F32) for s in states]
                  + [jax.ShapeDtypeStruct((NB, D, C), BF16), jax.ShapeDtypeStruct((NB, D, C), BF16),
                     jax.ShapeDtypeStruct((NB, C, 2 * C), BF16), jax.ShapeDtypeStruct((D, D), BF16)],
        scratch_shapes=[pltpu.VMEM((R, D), BF16),
                        pltpu.VMEM((conv_rows, C), F32),
                        pltpu.VMEM((NB, 2, D, C), F32),
                        pltpu.VMEM((NB, 2, C, C), F32),
                        pltpu.VMEM((NB, C, D), F32),
                        pltpu.VMEM((NB, 2, D, C), BF16), pltpu.VMEM((NB, C, 2 * C), BF16),
                        pltpu.VMEM((NB, C, D), BF16),
                        pltpu.SemaphoreType.DMA((NB, 5)), pltpu.SemaphoreType.DMA((NB, 4))] + staging,
        compiler_params=_params(),
        name="rglru_mixer_small",
    )(*x_args, *states, w["norm_mix_pre"], w["norm_mix_post"], w["rg_conv_w"], w["rg_conv_b"],
      w["rg_gate_a_b"], w["rg_gate_x_b"], w["rg_lambda"], w["rg_w_in"], w["rg_gate_a_w"], w["rg_gate_x_w"],
      w["rg_w_out"])
    n = len(segs)
    bf16 = dict(w_xr=outs[2 * n + 1], w_gate=outs[2 * n + 2], gw=outs[2 * n + 3], w_out=outs[2 * n + 4])
    return outs[0], list(outs[1:1 + n]), list(outs[1 + n:1 + 2 * n]), bf16


def _rglru_kernel(seg, nblk, layer, j, batch_major_in,
                  x_ref, conv_in_ref, h_in_ref, npre_ref, npost_ref, wxr_ref, wgate_ref, cw_ref, cb_ref,
                  gw_ref, gab_ref, gxb_ref, lam_ref, wout_ref,
                  o_ref, conv_out_ref, h_out_ref, xn_scr, xr_scr, h_scr, *in_staging):
    S, TB, R = seg.S, seg.TB, seg.rows
    P = (CONV_A - 1) * S
    i = pl.program_id(0)
    slot = i % 2
    blocks = [slice(b * COL_BLOCK, (b + 1) * COL_BLOCK) for b in range(N_COL_BLOCKS)]

    if batch_major_in:
        xbuf, sems = in_staging

        def copies(which, step):
            t0 = pl.multiple_of(step * TB, TB)
            return [pltpu.make_async_copy(x_ref.at[s, pl.ds(t0, TB), :], xbuf.at[which, :, s, :],
                                          sems.at[which, s]) for s in range(S)]

    @pl.when(i == 0)
    def _():
        for b, cols in enumerate(blocks):
            xr_scr[b, 0:P, :] = conv_in_ref[:, cols]
            h_scr[b] = h_in_ref[:, cols]
        if batch_major_in:
            for c in copies(0, 0):
                c.start()

    if batch_major_in:
        @pl.when(i + 1 < nblk)
        def _():
            for c in copies(1 - slot, i + 1):
                c.start()

        for c in copies(slot, i):
            c.wait()
        read_x = lambda: xbuf[slot].reshape(R, D_MODEL)
    else:
        read_x = lambda: x_ref[...]

    xn_scr[...] = _rms(read_x(), npre_ref[layer:layer + 1, :]).astype(BF16)
    xn = xn_scr[...]

    gate_pre, xc, res = {}, {}, {}

    def input_matmuls(b):
        _rglru_branch_in(seg, xn, wxr_ref[b], xr_scr.at[b])
        gate_pre[b] = _dot(xn, wgate_ref[b])

    def conv_and_gate_matmul(b):
        xc[b] = _rglru_conv(seg, cw_ref[:, blocks[b]], cb_ref[j:j + 1, blocks[b]], xr_scr.at[b], 0)
        res[b] = _dot(xc[b].astype(BF16), gw_ref[b])

    y = None
    input_matmuls(0)
    conv_and_gate_matmul(0)
    input_matmuls(1)
    for b, cols in enumerate(blocks):
        if b + 1 < N_COL_BLOCKS:
            conv_and_gate_matmul(b + 1)
        if b + 2 < N_COL_BLOCKS:
            input_matmuls(b + 2)
        g = _gelu_tanh(gate_pre.pop(b))
        a, u = _rglru_gate_math(res.pop(b), xc.pop(b), 0.5 * gab_ref[b:b + 1, :], 0.5 * gxb_ref[b:b + 1, :],
                                lam_ref[j:j + 1, cols])
        hs = _rglru_scan(seg, a, u, g, h_scr.at[b], h_scr.at[b], 0)
        part = _dot(hs.astype(BF16), wout_ref[cols, :])
        y = part if y is None else y + part

    o_ref[...] = read_x() + _rms(y, npost_ref[layer:layer + 1, :])

    @pl.when(i == nblk - 1)
    def _():
        for b, cols in enumerate(blocks):
            conv_out_ref[:, cols] = xr_scr[b, 0:P, :]
            h_out_ref[:, cols] = h_scr[b]


def _rglru_layer(x, conv_in, h_in, layer, j, w, bf16, *, S, TB):
    D = D_MODEL
    batch_major_in = x.ndim == 3
    (seg,), R, conv_rows, _ = _segments(((S, TB),), CONV_A)
    nblk = x.size // D // R
    P = (CONV_A - 1) * S
    n_layers, n_a = w["norm_mix_pre"].shape[0], w["rg_conv_b"].shape[0]
    row_spec = pl.BlockSpec((R, D), lambda i: (i, 0))
    if batch_major_in:
        x_spec = pl.BlockSpec(memory_space=pl.ANY)
        staging = [pltpu.VMEM((2, TB, S, D), F32), pltpu.SemaphoreType.DMA((2, S))]
    else:
        x_spec, staging = row_spec, []
    layer_of = lambda arr: pl.BlockSpec((None,) + arr.shape[1:], lambda i: (j,) + (0,) * (arr.ndim - 1),
                                        pipeline_mode=pl.Buffered(1))
    return pl.pallas_call(
        functools.partial(_rglru_kernel, seg, nblk, layer, j, batch_major_in),
        grid=(nblk,),
        in_specs=[x_spec, _resident((P, D)), _resident((S, D)),
                  _resident((n_layers, D)), _resident((n_layers, D)),
                  _resident(bf16["w_xr"].shape), _resident(bf16["w_gate"].shape),
                  layer_of(w["rg_conv_w"]), _resident((n_a, D)), _resident(bf16["gw"].shape),
                  layer_of(w["rg_gate_a_b"]), layer_of(w["rg_gate_x_b"]), _resident((n_a, D)),
                  _resident((D, D))],
        out_specs=[row_spec, pl.BlockSpec((P, D), lambda i: (0, 0)), pl.BlockSpec((S, D), lambda i: (0, 0))],
        out_shape=[jax.ShapeDtypeStruct((nblk * R, D), F32), jax.ShapeDtypeStruct((P, D), F32),
                   jax.ShapeDtypeStruct((S, D), F32)],
        scratch_shapes=[pltpu.VMEM((R, D), BF16),
                        pltpu.VMEM((N_COL_BLOCKS, conv_rows, COL_BLOCK), F32),
                        pltpu.VMEM((N_COL_BLOCKS, S, COL_BLOCK), F32)]
                       + staging,
        compiler_params=_params(),
        name="rglru_mixer",
    )(x, conv_in, h_in, w["norm_mix_pre"], w["norm_mix_post"], bf16["w_xr"], bf16["w_gate"], w["rg_conv_w"],
      w["rg_conv_b"], bf16["gw"], w["rg_gate_a_b"], w["rg_gate_x_b"], w["rg_lambda"], bf16["w_out"])


def _sconv_conv(seg, cv, cw, region):
    S, R = seg.S, seg.rows
    P = (CONV_B - 1) * S
    c0 = seg.conv0
    region[c0 + P:c0 + P + R, :] = cv
    conv = region[c0:c0 + R, :] * cw[0:1]
    for k in range(1, CONV_B):
        conv = conv + region[c0 + k * S:c0 + k * S + R, :] * cw[k:k + 1]
    region[c0:c0 + P, :] = region[c0 + R:c0 + R + P, :]
    return conv


def _sconv_stream_kernel(segs, layer, j, x_ref, *refs):
    n = len(segs)
    conv_in = refs[0:n]
    npre_ref, npost_ref, wbg_ref, wcg_ref, wv_ref, cw_ref, wout_ref = refs[n:n + 7]
    o_ref = refs[n + 7]
    conv_out = refs[n + 8:2 * n + 8]
    wbg_b_ref, wcg_b_ref, wv_b_ref, wout_b_ref = refs[2 * n + 8:2 * n + 12]
    xn_scr, cv_scr, y_scr = refs[2 * n + 12:]
    b = pl.program_id(0)

    @pl.when(b == 0)
    def _():
        xn_scr[...] = _rms(x_ref[...], npre_ref[layer:layer + 1, :]).astype(BF16)

    w_bg = wbg_ref[...].astype(BF16)
    w_cg = wcg_ref[...].astype(BF16)
    w_v = wv_ref[...].astype(BF16)
    w_out_rows = wout_ref[...].astype(BF16)
    wbg_b_ref[...] = w_bg
    wcg_b_ref[...] = w_cg
    wv_b_ref[...] = w_v
    wout_b_ref[...] = w_out_rows

    xn = xn_scr[...]
    cv = _dot(xn, w_cg) * _dot(xn, w_v)
    convs = []
    for seg, c_in, c_out in zip(segs, conv_in, conv_out):
        P = (CONV_B - 1) * seg.S
        cv_scr[seg.conv0:seg.conv0 + P, :] = c_in[...]
        convs.append(_sconv_conv(seg, cv[seg.row0:seg.row0 + seg.rows], cw_ref[...], cv_scr))
        c_out[...] = cv_scr[seg.conv0:seg.conv0 + P, :]
    m = (_dot(xn, w_bg) * jnp.concatenate(convs, axis=0)).astype(BF16)
    _accumulate(y_scr, slice(None), _dot(m, w_out_rows), b)

    @pl.when(b == N_COL_BLOCKS - 1)
    def _():
        o_ref[...] = x_ref[...] + _rms(y_scr[...], npost_ref[layer:layer + 1, :])


def _sconv_stream_layer(x, conv_in, layer, j, w, *, seqs_and_steps):
    R, D = x.shape
    C = COL_BLOCK
    segs, _, conv_rows, _ = _segments(seqs_and_steps, CONV_B)
    n_layers = w["norm_mix_pre"].shape[0]
    col = lambda rows: pl.BlockSpec((rows, C), lambda b: (0, b))
    w_in_part = lambda k: pl.BlockSpec((None, D, C), lambda b: (j, 0, k * N_COL_BLOCKS + b))
    outs = pl.pallas_call(
        functools.partial(_sconv_stream_kernel, segs, layer, j),
        grid=(N_COL_BLOCKS,),
        in_specs=[_resident((R, D))] + [col(c.shape[0]) for c in conv_in]
                 + [_resident((n_layers, D)), _resident((n_layers, D)), w_in_part(0), w_in_part(1), w_in_part(2),
                    pl.BlockSpec((None, CONV_B, C), lambda b: (j, 0, b)),
                    pl.BlockSpec((None, C, D), lambda b: (j, b, 0))],
        out_specs=[pl.BlockSpec((R, D), lambda b: (0, 0))] + [col(c.shape[0]) for c in conv_in]
                  + [_chunk_major_out(D)] * 3 + [pl.BlockSpec((C, D), lambda b: (b, 0))],
        out_shape=[jax.ShapeDtypeStruct((R, D), F32)] + [jax.ShapeDtypeStruct(c.shape, F32) for c in conv_in]
                  + [jax.ShapeDtypeStruct((N_COL_BLOCKS, D, C), BF16)] * 3 + [jax.ShapeDtypeStruct((D, D), BF16)],
        scratch_shapes=[pltpu.VMEM((R, D), BF16),
                        pltpu.VMEM((conv_rows, C), F32),
                        pltpu.VMEM((R, D), F32)],
        compiler_params=_params(),
        name="sconv_mixer_small",
    )(x, *conv_in, w["norm_mix_pre"], w["norm_mix_post"], w["sc_w_in"], w["sc_w_in"], w["sc_w_in"],
      w["sc_conv_w"], w["sc_w_out"])
    n = len(segs)
    bf16 = dict(w_bg=outs[n + 1], w_cg=outs[n + 2], w_v=outs[n + 3], w_out=outs[n + 4])
    return outs[0], list(outs[1:1 + n]), bf16


def _sconv_small_kernel(segs, layer, j, x_ref, *refs):
    D, C, NB = D_MODEL, COL_BLOCK, N_COL_BLOCKS
    n = len(segs)
    conv_in = refs[0:n]
    npre_ref, npost_ref, cw_ref, win_hbm, wout_hbm = refs[n:n + 5]
    o_ref = refs[n + 5]
    conv_out = refs[n + 6:2 * n + 6]
    wbg_b_hbm, wcg_b_hbm, wv_b_hbm, wout_b_hbm = refs[2 * n + 6:2 * n + 10]
    xn_scr, cv_scr, col_f, row_f, col_h, row_h, in_sems, out_sems = refs[2 * n + 10:]

    def fetch(b):
        return ([pltpu.make_async_copy(win_hbm.at[j, :, pl.ds(k * D + b * C, C)], col_f.at[b, k], in_sems.at[b, k])
                 for k in range(3)]
                + [pltpu.make_async_copy(wout_hbm.at[j, pl.ds(b * C, C), :], row_f.at[b], in_sems.at[b, 3])])

    def emit(b):
        return ([pltpu.make_async_copy(col_h.at[b, k], dst.at[b], out_sems.at[b, k])
                 for k, dst in enumerate((wbg_b_hbm, wcg_b_hbm, wv_b_hbm))]
                + [pltpu.make_async_copy(row_h.at[b], wout_b_hbm.at[pl.ds(b * C, C), :], out_sems.at[b, 3])])

    for b in range(NB):
        _start_all(fetch(b))
    xn_scr[...] = _rms(x_ref[...], npre_ref[layer:layer + 1, :]).astype(BF16)
    xn = xn_scr[...]

    y = None
    for b in range(NB):
        cols = slice(b * C, (b + 1) * C)
        _wait_all(fetch(b))
        w_bg, w_cg, w_v = (col_f[b, k].astype(BF16) for k in range(3))
        w_out_rows = row_f[b].astype(BF16)
        col_h[b, 0], col_h[b, 1], col_h[b, 2], row_h[b] = w_bg, w_cg, w_v, w_out_rows
        _start_all(emit(b))

        cv = _dot(xn, w_cg) * _dot(xn, w_v)
        convs = []
        for seg, c_in, c_out in zip(segs, conv_in, conv_out):
            P = (CONV_B - 1) * seg.S
            cv_scr[seg.conv0:seg.conv0 + P, :] = c_in[:, cols]
            convs.append(_sconv_conv(seg, cv[seg.row0:seg.row0 + seg.rows], cw_ref[:, cols], cv_scr))
            c_out[:, cols] = cv_scr[seg.conv0:seg.conv0 + P, :]
        m = (_dot(xn, w_bg) * jnp.concatenate(convs, axis=0)).astype(BF16)
        part = _dot(m, w_out_rows)
        y = part if y is None else y + part

    for b in range(NB):
        _wait_all(emit(b))
    o_ref[...] = x_ref[...] + _rms(y, npost_ref[layer:layer + 1, :])


def _sconv_small_layer(x, conv_in, layer, j, w, *, seqs_and_steps):
    R, D = x.shape
    C, NB = COL_BLOCK, N_COL_BLOCKS
    segs, _, conv_rows, _ = _segments(seqs_and_steps, CONV_B)
    n_layers = w["norm_mix_pre"].shape[0]
    hbm = pl.BlockSpec(memory_space=pl.ANY)
    outs = pl.pallas_call(
        functools.partial(_sconv_small_kernel, segs, layer, j),
        grid=(1,),
        in_specs=[_resident((R, D))] + [_resident(c.shape) for c in conv_in]
                 + [_resident((n_layers, D)), _resident((n_layers, D)),
                    pl.BlockSpec((None, CONV_B, D), lambda i: (j, 0, 0), pipeline_mode=pl.Buffered(1)), hbm, hbm],
        out_specs=[pl.BlockSpec((R, D), lambda i: (0, 0))]
                  + [pl.BlockSpec(c.shape, lambda i: (0, 0)) for c in conv_in] + [hbm] * 4,
        out_shape=[jax.ShapeDtypeStruct((R, D), F32)] + [jax.ShapeDtypeStruct(c.shape, F32) for c in conv_in]
                  + [jax.ShapeDtypeStruct((NB, D, C), BF16)] * 3 + [jax.ShapeDtypeStruct((D, D), BF16)],
        scratch_shapes=[pltpu.VMEM((R, D), BF16),
                        pltpu.VMEM((conv_rows, C), F32),
                        pltpu.VMEM((NB, 3, D, C), F32),
                        pltpu.VMEM((NB, C, D), F32),
                        pltpu.VMEM((NB, 3, D, C), BF16), pltpu.VMEM((NB, C, D), BF16),
                        pltpu.SemaphoreType.DMA((NB, 4)), pltpu.SemaphoreType.DMA((NB, 4))],
        compiler_params=_params(),
        name="sconv_mixer_small",
    )(x, *conv_in, w["norm_mix_pre"], w["norm_mix_post"], w["sc_conv_w"], w["sc_w_in"], w["sc_w_out"])
    n = len(segs)
    bf16 = dict(w_bg=outs[n + 1], w_cg=outs[n + 2], w_v=outs[n + 3], w_out=outs[n + 4])
    return outs[0], list(outs[1:1 + n]), bf16


def _sconv_kernel(seg, nblk, layer, x_ref, conv_in_ref, npre_ref, npost_ref, wbg_ref, wcg_ref, wv_ref,
                  cw_ref, wout_ref, o_ref, conv_out_ref, xn_scr, cv_scr, m_scr):
    P = (CONV_B - 1) * seg.S
    i = pl.program_id(0)
    blocks = [slice(b * COL_BLOCK, (b + 1) * COL_BLOCK) for b in range(N_COL_BLOCKS)]

    @pl.when(i == 0)
    def _():
        for b, cols in enumerate(blocks):
            cv_scr[b, 0:P, :] = conv_in_ref[:, cols]

    xn_scr[...] = _rms(x_ref[...], npre_ref[layer:layer + 1, :]).astype(BF16)
    xn = xn_scr[...]
    for b, cols in enumerate(blocks):
        cv = _dot(xn, wcg_ref[b]) * _dot(xn, wv_ref[b])
        conv = _sconv_conv(seg, cv, cw_ref[:, cols], cv_scr.at[b])
        m_scr[:, cols] = (_dot(xn, wbg_ref[b]) * conv).astype(BF16)
    y = _dot(m_scr[...], wout_ref[...])
    o_ref[...] = x_ref[...] + _rms(y, npost_ref[layer:layer + 1, :])

    @pl.when(i == nblk - 1)
    def _():
        for b, cols in enumerate(blocks):
            conv_out_ref[:, cols] = cv_scr[b, 0:P, :]


def _sconv_layer(x, conv_in, layer, j, w, bf16, *, S, TB):
    N, D = x.shape
    (seg,), R, conv_rows, _ = _segments(((S, TB),), CONV_B)
    nblk = N // R
    P = (CONV_B - 1) * S
    n_layers = w["norm_mix_pre"].shape[0]
    row_spec = pl.BlockSpec((R, D), lambda i: (i, 0))
    return pl.pallas_call(
        functools.partial(_sconv_kernel, seg, nblk, layer),
        grid=(nblk,),
        in_specs=[row_spec, _resident((P, D)), _resident((n_layers, D)), _resident((n_layers, D)),
                  _resident(bf16["w_bg"].shape), _resident(bf16["w_cg"].shape), _resident(bf16["w_v"].shape),
                  pl.BlockSpec((None, CONV_B, D), lambda i: (j, 0, 0), pipeline_mode=pl.Buffered(1)),
                  _resident((D, D))],
        out_specs=[row_spec, pl.BlockSpec((P, D), lambda i: (0, 0))],
        out_shape=[jax.ShapeDtypeStruct((N, D), F32), jax.ShapeDtypeStruct((P, D), F32)],
        scratch_shapes=[pltpu.VMEM((R, D), BF16),
                        pltpu.VMEM((N_COL_BLOCKS, conv_rows, COL_BLOCK), F32),
                        pltpu.VMEM((R, D), BF16)],
        compiler_params=_params(),
        name="sconv_mixer",
    )(x, conv_in, w["norm_mix_pre"], w["norm_mix_post"], bf16["w_bg"], bf16["w_cg"], bf16["w_v"],
      w["sc_conv_w"], bf16["w_out"])


def _swiglu(g, u):
    return (g * jax.nn.sigmoid(g) * u).astype(BF16)


def _ffn_stream_kernel(layer, nchunk, sample_out, x_ref, npre_ref, npost_ref, wg_hbm, wu_hbm, wd_hbm,
                       o_ref, wg_b_hbm, wu_b_hbm, wd_b_hbm,
                       xn_scr, wg_f, wu_f, wd_f, wg_h, wu_h, wd_h, in_sems, out_sems, *out_staging):
    F = FF_CHUNK

    NB = STREAM_BUFFERS

    def fetch(c):
        k, cols = c % NB, pl.ds(c * F, F)
        return [pltpu.make_async_copy(wg_hbm.at[layer, :, cols], wg_f.at[k], in_sems.at[0, k]),
                pltpu.make_async_copy(wu_hbm.at[layer, :, cols], wu_f.at[k], in_sems.at[1, k]),
                pltpu.make_async_copy(wd_hbm.at[layer, cols, :], wd_f.at[k], in_sems.at[2, k])]

    def emit(c):
        k = c % NB
        return [pltpu.make_async_copy(wg_h.at[k], wg_b_hbm.at[c], out_sems.at[0, k]),
                pltpu.make_async_copy(wu_h.at[k], wu_b_hbm.at[c], out_sems.at[1, k]),
                pltpu.make_async_copy(wd_h.at[k], wd_b_hbm.at[pl.ds(c * F, F), :], out_sems.at[2, k])]

    for c in range(min(NB, nchunk)):
        for cp in fetch(c):
            cp.start()
    xn_scr[...] = _rms(x_ref[...], npre_ref[layer:layer + 1, :]).astype(BF16)
    xn = xn_scr[...]

    y = None
    for c in range(nchunk):
        k = c % NB
        for cp in fetch(c):
            cp.wait()
        if c >= NB:
            for cp in emit(c - NB):
                cp.wait()
        wg, wu, wd = wg_f[k].astype(BF16), wu_f[k].astype(BF16), wd_f[k].astype(BF16)
        wg_h[k], wu_h[k], wd_h[k] = wg, wu, wd
        for cp in emit(c):
            cp.start()
        if c + NB < nchunk:
            for cp in fetch(c + NB):
                cp.start()
        part = _dot(_swiglu(_dot(xn, wg), _dot(xn, wu)), wd)
        y = part if y is None else y + part
    for c in range(max(nchunk - NB, 0), nchunk):
        for cp in emit(c):
            cp.wait()

    out = x_ref[...] + _rms(y, npost_ref[layer:layer + 1, :])
    if sample_out is None:
        o_ref[...] = out
    else:
        row0, S, T = sample_out
        obuf, sems = out_staging
        obuf[...] = out[row0:row0 + T * S].reshape(T, S, out.shape[-1])
        copies = [pltpu.make_async_copy(obuf.at[t], o_ref.at[:, t, :], sems.at[t]) for t in range(T)]
        for cp in copies:
            cp.start()
        for cp in copies:
            cp.wait()


def _ffn_stream_layer(x, layer, w, *, sample_out=None):
    R, D = x.shape
    F, NB = FF_CHUNK, STREAM_BUFFERS
    n_layers = w["norm_ffn_pre"].shape[0]
    nchunk = D_FF // F
    hbm = pl.BlockSpec(memory_space=pl.ANY)
    if sample_out is None:
        o_spec, o_shape, staging = pl.BlockSpec((R, D), lambda i: (0, 0)), jax.ShapeDtypeStruct((R, D), F32), []
    else:
        _, S, T = sample_out
        o_spec, o_shape = hbm, jax.ShapeDtypeStruct((S, T, D), F32)
        staging = [pltpu.VMEM((T, S, D), F32), pltpu.SemaphoreType.DMA((T,))]
    outs = pl.pallas_call(
        functools.partial(_ffn_stream_kernel, layer, nchunk, sample_out),
        grid=(1,),
        in_specs=[_resident((R, D)), _resident((n_layers, D)), _resident((n_layers, D)), hbm, hbm, hbm],
        out_specs=[o_spec, hbm, hbm, hbm],
        out_shape=[o_shape, jax.ShapeDtypeStruct((nchunk, D, F), BF16),
                   jax.ShapeDtypeStruct((nchunk, D, F), BF16), jax.ShapeDtypeStruct((D_FF, D), BF16)],
        scratch_shapes=[pltpu.VMEM((R, D), BF16),
                        pltpu.VMEM((NB, D, F), F32), pltpu.VMEM((NB, D, F), F32), pltpu.VMEM((NB, F, D), F32),
                        pltpu.VMEM((NB, D, F), BF16), pltpu.VMEM((NB, D, F), BF16), pltpu.VMEM((NB, F, D), BF16),
                        pltpu.SemaphoreType.DMA((3, NB)), pltpu.SemaphoreType.DMA((3, NB))] + staging,
        compiler_params=_params(),
        name="swiglu_ffn_small",
    )(x, w["norm_ffn_pre"], w["norm_ffn_post"], w["ffn_w_gate"], w["ffn_w_up"], w["ffn_w_down"])
    return outs[0], dict(wg=outs[1], wu=outs[2], wd=outs[3])


def _ffn_kernel(layer, nblk, batch_major_out, x_ref, npre_ref, npost_ref, wg_hbm, wu_hbm, wd_hbm, o_ref,
                xn_scr, act_scr, wg_ref, wu_ref, wd_ref, w_sems, *out_staging):
    R = x_ref.shape[0]
    nchunk = D_FF // FF_CHUNK
    i = pl.program_id(0)
    slot = i % 2

    def load(c):
        rows = pl.ds(c * FF_CHUNK, FF_CHUNK)
        return [pltpu.make_async_copy(wg_hbm.at[c], wg_ref.at[c], w_sems.at[0, c]),
                pltpu.make_async_copy(wu_hbm.at[c], wu_ref.at[c], w_sems.at[1, c]),
                pltpu.make_async_copy(wd_hbm.at[rows, :], wd_ref.at[rows, :], w_sems.at[2, c])]

    if batch_major_out:
        obuf, sems = out_staging
        S, TB = o_ref.shape[0], R // o_ref.shape[0]

        def copies(which, step):
            t0 = pl.multiple_of(step * TB, TB)
            return [pltpu.make_async_copy(obuf.at[which, :, s, :],
                                          o_ref.at[s, pl.ds(t0, TB), :], sems.at[which, s]) for s in range(S)]

        @pl.when(i >= 2)
        def _():
            for c in copies(slot, i - 2):
                c.wait()

    def block(first):
        if first:
            for c in range(nchunk):
                for cp in load(c)[:2]:
                    cp.start()
            for c in range(nchunk):
                load(c)[2].start()
        halves = [slice(0, R // 2), slice(R // 2, R)]
        for rows in halves:
            xn_scr[rows, :] = _rms(x_ref[rows, :], npre_ref[layer:layer + 1, :]).astype(BF16)
        for h, rows in enumerate(halves):
            xn = xn_scr[rows, :]
            for c in range(nchunk):
                if first and h == 0:
                    for cp in load(c)[:2]:
                        cp.wait()
                cols = slice(c * FF_CHUNK, (c + 1) * FF_CHUNK)
                act_scr[rows, cols] = _swiglu(_dot(xn, wg_ref[c]), _dot(xn, wu_ref[c]))
        if first:
            for c in range(nchunk):
                load(c)[2].wait()
        ys = [_dot(act_scr[rows, :], wd_ref[...]) for rows in halves]
        for rows, y in zip(halves, ys):
            out = x_ref[rows, :] + _rms(y, npost_ref[layer:layer + 1, :])
            if batch_major_out:
                obuf[slot, rows.start // S:rows.stop // S] = out.reshape(-1, S, out.shape[-1])
            else:
                o_ref[rows, :] = out

    @pl.when(i == 0)
    def _():
        block(True)

    @pl.when(i > 0)
    def _():
        block(False)

    if batch_major_out:
        for c in copies(slot, i):
            c.start()

        @pl.when(i == nblk - 1)
        def _():
            if nblk >= 2:
                for c in copies(1 - slot, i - 1):
                    c.wait()
            for c in copies(slot, i):
                c.wait()


def _ffn_layer(x, layer, w, bf16, *, R, batch_major_seqs=None):
    N, D = x.shape
    n_layers = w["norm_ffn_pre"].shape[0]
    nblk = N // R
    row_spec = pl.BlockSpec((R, D), lambda i: (i, 0))
    hbm = pl.BlockSpec(memory_space=pl.ANY)
    if batch_major_seqs is None:
        out_spec, out_shape, staging = row_spec, jax.ShapeDtypeStruct((N, D), F32), []
    else:
        S = batch_major_seqs
        out_spec = hbm
        out_shape = jax.ShapeDtypeStruct((S, N // S, D), F32)
        staging = [pltpu.VMEM((2, R // S, S, D), F32), pltpu.SemaphoreType.DMA((2, S))]
    return pl.pallas_call(
        functools.partial(_ffn_kernel, layer, nblk, batch_major_seqs is not None),
        grid=(nblk,),
        in_specs=[row_spec, _resident((n_layers, D)), _resident((n_layers, D)), hbm, hbm, hbm],
        out_specs=out_spec,
        out_shape=out_shape,
        scratch_shapes=[pltpu.VMEM((R, D), BF16), pltpu.VMEM((R, D_FF), BF16),
                        pltpu.VMEM(bf16["wg"].shape, BF16), pltpu.VMEM(bf16["wu"].shape, BF16),
                        pltpu.VMEM((D_FF, D), BF16), pltpu.SemaphoreType.DMA((3, D_FF // FF_CHUNK))] + staging,
        compiler_params=_params(),
        name="swiglu_ffn",
    )(x, w["norm_ffn_pre"], w["norm_ffn_post"], bf16["wg"], bf16["wu"], bf16["wd"])


def _to_time_major(a):
    S, K, D = a.shape
    return jnp.swapaxes(a, 0, 1).reshape(K * S, D)


def _from_time_major(a, S):
    KS, D = a.shape
    return jnp.swapaxes(a.reshape(KS // S, S, D), 0, 1)


def kernel(x_prompt, x_sample, state_rglru_conv, state_rglru_h, state_sconv, meta_tokens, norm_mix_pre, norm_mix_post, norm_ffn_pre, norm_ffn_post, rg_w_in, rg_conv_w, rg_conv_b, rg_gate_a_w, rg_gate_a_b, rg_gate_x_w, rg_gate_x_b, rg_lambda, rg_w_out, sc_w_in, sc_conv_w, sc_w_out, ffn_w_gate, ffn_w_up, ffn_w_down):
    D = D_MODEL
    depth = norm_mix_pre.shape[0]
    batch, seq, _ = x_prompt.shape
    dec_batch, dec_seq, _ = x_sample.shape
    w = dict(norm_mix_pre=norm_mix_pre, norm_mix_post=norm_mix_post, norm_ffn_pre=norm_ffn_pre,
             norm_ffn_post=norm_ffn_post, rg_w_in=rg_w_in, rg_conv_w=rg_conv_w, rg_conv_b=rg_conv_b,
             rg_gate_a_w=rg_gate_a_w, rg_gate_a_b=rg_gate_a_b, rg_gate_x_w=rg_gate_x_w, rg_gate_x_b=rg_gate_x_b,
             rg_lambda=rg_lambda, rg_w_out=rg_w_out, sc_w_in=sc_w_in, sc_conv_w=sc_conv_w, sc_w_out=sc_w_out,
             ffn_w_gate=ffn_w_gate, ffn_w_up=ffn_w_up, ffn_w_down=ffn_w_down)

    x = jnp.broadcast_to(meta_tokens[:, None, :], (N_META, batch, D)).reshape(N_META * batch, D)
    small = ((batch, N_META), (dec_batch, dec_seq))
    mixer_bf16, ffn_bf16 = [], []
    rg_conv_s, rg_h_s, sc_s = [], [], []
    for i in range(depth):
        j = i // 2
        if i % 2 == 0:
            conv0 = [jnp.zeros(((CONV_A - 1) * batch, D), F32), _to_time_major(state_rglru_conv[j])]
            h0 = [jnp.zeros((batch, D), F32), state_rglru_h[j]]
            x, cb, hT, wb = _rglru_small_layer(x, conv0, h0, i, j, w, seqs_and_steps=small,
                                                x_sample=x_sample if i == 0 else None)
            rg_conv_s.append(cb)
            rg_h_s.append(hT)
        else:
            conv0 = [jnp.zeros(((CONV_B - 1) * batch, D), F32), _to_time_major(state_sconv[j])]
            x, cb, wb = _sconv_small_layer(x, conv0, i, j, w, seqs_and_steps=small)
            sc_s.append(cb)
        mixer_bf16.append(wb)
        sample_out = (N_META * batch, dec_batch, dec_seq) if i == depth - 1 else None
        x, wb = _ffn_stream_layer(x, i, w, sample_out=sample_out)
        ffn_bf16.append(wb)
    y_sample = x

    x = x_prompt
    rg_conv_p, rg_h_p, sc_p = [], [], []
    for i in range(depth):
        j = i // 2
        if i % 2 == 0:
            x, cb, hT = _rglru_layer(x, rg_conv_s[j][0], rg_h_s[j][0], i, j, w, mixer_bf16[i], S=batch, TB=PROMPT_TB)
            rg_conv_p.append(cb)
            rg_h_p.append(hT)
        else:
            x, cb = _sconv_layer(x, sc_s[j][0], i, j, w, mixer_bf16[i], S=batch, TB=PROMPT_TB)
            sc_p.append(cb)
        x = _ffn_layer(x, i, w, ffn_bf16[i], R=batch * PROMPT_TB,
                       batch_major_seqs=batch if i == depth - 1 else None)
    y_prompt = x

    return (y_prompt, y_sample,
            jnp.stack([_from_time_major(c, batch) for c in rg_conv_p]), jnp.stack(rg_h_p),
            jnp.stack([_from_time_major(c, batch) for c in sc_p]),
            jnp.stack([_from_time_major(c[1], dec_batch) for c in rg_conv_s]), jnp.stack([h[1] for h in rg_h_s]),
            jnp.stack([_from_time_major(c[1], dec_batch) for c in sc_s]))
```

```python
import functools
from typing import NamedTuple

import jax
import jax.numpy as jnp
from jax import lax
from jax.experimental import pallas as pl
from jax.experimental.pallas import tpu as pltpu

D_MODEL = 1024
D_FF = 2816
N_META = 16
COL_BLOCK = 256
N_COL_BLOCKS = D_MODEL // COL_BLOCK
CONV_A = 4
CONV_B = 3
RG_C = 8.0
EPS = 1e-6

SUBLANES = 8
FF_CHUNK = 256
STREAM_BUFFERS = 3
PROMPT_TB = 128
VMEM_LIMIT_BYTES = 56 * 1024 * 1024

F32 = jnp.float32
BF16 = jnp.bfloat16


class _Seg(NamedTuple):
    S: int
    TB: int
    row0: int
    conv0: int
    h0: int

    @property
    def rows(self):
        return self.S * self.TB


def _segments(seqs_and_steps, taps):
    segs, row0, conv0, h0 = [], 0, 0, 0
    for S, TB in seqs_and_steps:
        segs.append(_Seg(S, TB, row0, conv0, h0))
        row0 += S * TB
        conv0 += (taps - 1) * S + S * TB
        h0 += S
    return tuple(segs), row0, conv0, h0


def _rms(x, w):
    ms = jnp.mean(x * x, axis=-1, keepdims=True)
    return x * lax.rsqrt(ms + EPS) * w


def _dot(a, b):
    return jnp.dot(a, b, preferred_element_type=F32)


def _gelu_tanh(x):
    c = 0.7978845608028654
    hx = 0.5 * x
    return hx + hx * jnp.tanh(x * (c + (c * 0.044715) * (x * x)))


def _resident(shape):
    zeros = (0,) * len(shape)
    return pl.BlockSpec(shape, lambda i: zeros, pipeline_mode=pl.Buffered(1))


def _chunk_major_out(rows, width=COL_BLOCK):
    return pl.BlockSpec((None, rows, width), lambda i: (i, 0, 0))


def _params():
    return pltpu.CompilerParams(dimension_semantics=("arbitrary",), vmem_limit_bytes=VMEM_LIMIT_BYTES)


def _accumulate(y_scr, rows, part, step):
    @pl.when(step == 0)
    def _():
        y_scr[rows, :] = part

    @pl.when(step > 0)
    def _():
        y_scr[rows, :] += part


def _rglru_branch_in(seg, xn, w_xr, region):
    P = (CONV_A - 1) * seg.S
    region[seg.conv0 + P:seg.conv0 + P + seg.rows, :] = _dot(xn, w_xr)


def _rglru_conv(seg, cw, cb, region, carry_to):
    S, R = seg.S, seg.rows
    P = (CONV_A - 1) * S
    c0 = seg.conv0
    xc = region[c0:c0 + R, :] * cw[0:1]
    for k in range(1, CONV_A):
        xc = xc + region[c0 + k * S:c0 + k * S + R, :] * cw[k:k + 1]
    xc = xc + cb
    if carry_to is not None:
        region[carry_to:carry_to + P, :] = region[c0 + R:c0 + R + P, :]
    return xc


def _rglru_gate_math(res, xc, half_gab, half_gxb, lam):
    half_c_sp = (-0.5 * RG_C) * jax.nn.softplus(-lam)
    tr = jnp.tanh(res[:, 0:COL_BLOCK] + half_gab)
    log_a = half_c_sp * tr + half_c_sp
    ig = 0.5 * jnp.tanh(res[:, COL_BLOCK:2 * COL_BLOCK] + half_gxb) + 0.5
    a = jnp.exp(log_a)
    m2 = jnp.tanh(log_a) * (-1.0 - a * a)
    u = jnp.where(m2 > 0.0, m2 * lax.rsqrt(m2), 0.0) * (ig * xc)
    return a, u


def _rglru_scan(seg, a, u, g, h_read, h_write, h_row0):
    S, TB = seg.S, seg.TB
    groups = S // SUBLANES
    pieces = [None] * (TB * groups)
    for c in range(groups):
        hrows = slice(h_row0 + c * SUBLANES, h_row0 + (c + 1) * SUBLANES)
        h = h_read[hrows, :]
        for t in range(TB):
            r = t * S + c * SUBLANES
            h = a[r:r + SUBLANES] * h + u[r:r + SUBLANES]
            pieces[t * groups + c] = h * g[r:r + SUBLANES]
        h_write[hrows, :] = h
    return jnp.concatenate(pieces, axis=0)


def _rglru_stream_kernel(segs, sample_in, layer, j, x_ref, *refs):
    if sample_in is not None:
        xs_hbm, refs = refs[0], refs[1:]
    n = len(segs)
    conv_in, h_in = refs[0:n], refs[n:2 * n]
    (npre_ref, npost_ref, wxr_ref, wgate_ref, cw_ref, cb_ref, gaw_ref, gxw_ref, gab_ref, gxb_ref,
     lam_ref, wout_ref) = refs[2 * n:2 * n + 12]
    o_ref = refs[2 * n + 12]
    conv_out, h_out = refs[2 * n + 13:3 * n + 13], refs[3 * n + 13:4 * n + 13]
    wxr_b_ref, wgate_b_ref, gw_b_ref, wout_b_ref = refs[4 * n + 13:4 * n + 17]
    xn_scr, xr_scr, y_scr, *in_staging = refs[4 * n + 17:]
    b = pl.program_id(0)

    if sample_in is None:
        read_x = lambda: x_ref[...]
    else:
        S, T = sample_in
        xs_scr, sems = in_staging
        read_x = lambda: jnp.concatenate([x_ref[...], xs_scr[...].reshape(T * S, D_MODEL)], axis=0)

    @pl.when(b == 0)
    def _():
        if sample_in is not None:
            copies = [pltpu.make_async_copy(xs_hbm.at[:, t, :], xs_scr.at[t], sems.at[t]) for t in range(T)]
            for cp in copies:
                cp.start()
            for cp in copies:
                cp.wait()
        xn_scr[...] = _rms(read_x(), npre_ref[layer:layer + 1, :]).astype(BF16)

    w_xr = wxr_ref[...].astype(BF16)
    w_gate = wgate_ref[...].astype(BF16)
    gw = (0.5 * jnp.concatenate([gaw_ref[...], gxw_ref[...]], axis=1)).astype(BF16)
    w_out_rows = wout_ref[...].astype(BF16)
    wxr_b_ref[...] = w_xr
    wgate_b_ref[...] = w_gate
    gw_b_ref[...] = gw
    wout_b_ref[...] = w_out_rows

    rows_of = lambda seg: slice(seg.row0, seg.row0 + seg.rows)
    xn = xn_scr[...]
    xr = _dot(xn, w_xr)
    g = _gelu_tanh(_dot(xn, w_gate))
    xcs = []
    for seg, c_in, c_out in zip(segs, conv_in, conv_out):
        P = (CONV_A - 1) * seg.S
        xr_scr[seg.conv0:seg.conv0 + P, :] = c_in[...]
        xr_scr[seg.conv0 + P:seg.conv0 + P + seg.rows, :] = xr[rows_of(seg)]
        xcs.append(_rglru_conv(seg, cw_ref[...], cb_ref[j:j + 1, :], xr_scr, seg.conv0))
        c_out[...] = xr_scr[seg.conv0:seg.conv0 + P, :]
    xc = jnp.concatenate(xcs, axis=0)
    a, u = _rglru_gate_math(_dot(xc.astype(BF16), gw), xc, 0.5 * gab_ref[pl.ds(b, 1), :],
                            0.5 * gxb_ref[pl.ds(b, 1), :], lam_ref[j:j + 1, :])
    hs = [_rglru_scan(seg, a[rows_of(seg)], u[rows_of(seg)], g[rows_of(seg)], hi, ho, 0)
          for seg, hi, ho in zip(segs, h_in, h_out)]
    part = _dot(jnp.concatenate(hs, axis=0).astype(BF16), w_out_rows)
    _accumulate(y_scr, slice(None), part, b)

    @pl.when(b == N_COL_BLOCKS - 1)
    def _():
        o_ref[...] = read_x() + _rms(y_scr[...], npost_ref[layer:layer + 1, :])


def _rglru_stream_layer(x, conv_in, h_in, layer, j, w, *, seqs_and_steps, x_sample=None):
    D, C = D_MODEL, COL_BLOCK
    segs, R, conv_rows, _ = _segments(seqs_and_steps, CONV_A)
    n_layers, n_a = w["norm_mix_pre"].shape[0], w["rg_conv_b"].shape[0]
    col = lambda rows: pl.BlockSpec((rows, C), lambda b: (0, b))
    if x_sample is None:
        x_args, x_specs, sample_in, staging = [x], [_resident(x.shape)], None, []
    else:
        S, T, _ = x_sample.shape
        x_args, x_specs, sample_in = [x, x_sample], [_resident(x.shape), pl.BlockSpec(memory_space=pl.ANY)], (S, T)
        staging = [pltpu.VMEM((T, S, D), F32), pltpu.SemaphoreType.DMA((T,))]
    in_specs = (
        x_specs + [col(c.shape[0]) for c in conv_in] + [col(h.shape[0]) for h in h_in]
        + [_resident((n_layers, D)), _resident((n_layers, D)),
           pl.BlockSpec((None, D, C), lambda b: (j, 0, N_COL_BLOCKS + b)),
           pl.BlockSpec((None, D, C), lambda b: (j, 0, b)),
           pl.BlockSpec((None, CONV_A, C), lambda b: (j, 0, b)),
           col(n_a),
           pl.BlockSpec((None, None, C, C), lambda b: (j, b, 0, 0)),
           pl.BlockSpec((None, None, C, C), lambda b: (j, b, 0, 0)),
           pl.BlockSpec((None, N_COL_BLOCKS, C), lambda b: (j, 0, 0)),
           pl.BlockSpec((None, N_COL_BLOCKS, C), lambda b: (j, 0, 0)),
           col(n_a),
           pl.BlockSpec((None, C, D), lambda b: (j, b, 0))])
    out_specs = ([pl.BlockSpec((R, D), lambda b: (0, 0))] + [col(c.shape[0]) for c in conv_in]
                 + [col(h.shape[0]) for h in h_in]
                 + [_chunk_major_out(D), _chunk_major_out(D), pl.BlockSpec((None, C, 2 * C), lambda b: (b, 0, 0)),
                    pl.BlockSpec((C, D), lambda b: (b, 0))])
    out_shape = ([jax.ShapeDtypeStruct((R, D), F32)] + [jax.ShapeDtypeStruct(s.shape, F32) for s in conv_in + h_in]
                 + [jax.ShapeDtypeStruct((N_COL_BLOCKS, D, C), BF16), jax.ShapeDtypeStruct((N_COL_BLOCKS, D, C), BF16),
                    jax.ShapeDtypeStruct((N_COL_BLOCKS, C, 2 * C), BF16), jax.ShapeDtypeStruct((D, D), BF16)])
    outs = pl.pallas_call(
        functools.partial(_rglru_stream_kernel, segs, sample_in, layer, j),
        grid=(N_COL_BLOCKS,),
        in_specs=in_specs, out_specs=out_specs, out_shape=out_shape,
        scratch_shapes=[pltpu.VMEM((R, D), BF16),
                        pltpu.VMEM((conv_rows, C), F32),
                        pltpu.VMEM((R, D), F32)]
                       + staging,
        compiler_params=_params(),
        name="rglru_mixer_small",
    )(*x_args, *conv_in, *h_in, w["norm_mix_pre"], w["norm_mix_post"], w["rg_w_in"], w["rg_w_in"], w["rg_conv_w"],
      w["rg_conv_b"], w["rg_gate_a_w"], w["rg_gate_x_w"], w["rg_gate_a_b"], w["rg_gate_x_b"], w["rg_lambda"],
      w["rg_w_out"])
    n = len(segs)
    bf16 = dict(w_xr=outs[2 * n + 1], w_gate=outs[2 * n + 2], gw=outs[2 * n + 3], w_out=outs[2 * n + 4])
    return outs[0], list(outs[1:1 + n]), list(outs[1 + n:1 + 2 * n]), bf16


def _start_all(copies):
    for cp in copies:
        cp.start()


def _wait_all(copies):
    for cp in copies:
        cp.wait()


def _rglru_small_kernel(segs, sample_in, layer, j, x_ref, *refs):
    D, C, NB = D_MODEL, COL_BLOCK, N_COL_BLOCKS
    if sample_in is not None:
        xs_hbm, refs = refs[0], refs[1:]
    n = len(segs)
    conv_in, h_in = refs[0:n], refs[n:2 * n]
    (npre_ref, npost_ref, cw_ref, cb_ref, gab_ref, gxb_ref, lam_ref,
     win_hbm, gaw_hbm, gxw_hbm, wout_hbm) = refs[2 * n:2 * n + 11]
    o_ref = refs[2 * n + 11]
    conv_out, h_out = refs[2 * n + 12:3 * n + 12], refs[3 * n + 12:4 * n + 12]
    wxr_b_hbm, wgate_b_hbm, gw_b_hbm, wout_b_hbm = refs[4 * n + 12:4 * n + 16]
    (xn_scr, xr_scr, col_f, sq_f, row_f, col_h, gw_h, row_h, in_sems, out_sems,
     *in_staging) = refs[4 * n + 16:]

    def fetch(b):
        cols = pl.ds(b * C, C)
        return [pltpu.make_async_copy(win_hbm.at[j, :, pl.ds(D + b * C, C)], col_f.at[b, 0], in_sems.at[b, 0]),
                pltpu.make_async_copy(win_hbm.at[j, :, cols], col_f.at[b, 1], in_sems.at[b, 1]),
                pltpu.make_async_copy(gaw_hbm.at[j, b], sq_f.at[b, 0], in_sems.at[b, 2]),
                pltpu.make_async_copy(gxw_hbm.at[j, b], sq_f.at[b, 1], in_sems.at[b, 3]),
                pltpu.make_async_copy(wout_hbm.at[j, cols, :], row_f.at[b], in_sems.at[b, 4])]

    def emit(b):
        return [pltpu.make_async_copy(col_h.at[b, 0], wxr_b_hbm.at[b], out_sems.at[b, 0]),
                pltpu.make_async_copy(col_h.at[b, 1], wgate_b_hbm.at[b], out_sems.at[b, 1]),
                pltpu.make_async_copy(gw_h.at[b], gw_b_hbm.at[b], out_sems.at[b, 2]),
                pltpu.make_async_copy(row_h.at[b], wout_b_hbm.at[pl.ds(b * C, C), :], out_sems.at[b, 3])]

    if sample_in is None:
        read_x = lambda: x_ref[...]
    else:
        S, T = sample_in
        xs_scr, sems = in_staging
        x_copies = [pltpu.make_async_copy(xs_hbm.at[:, t, :], xs_scr.at[t], sems.at[t]) for t in range(T)]
        _start_all(x_copies)
        read_x = lambda: jnp.concatenate([x_ref[...], xs_scr[...].reshape(T * S, D)], axis=0)
    for b in range(NB):
        _start_all(fetch(b))
    if sample_in is not None:
        _wait_all(x_copies)
    xn_scr[...] = _rms(read_x(), npre_ref[layer:layer + 1, :]).astype(BF16)
    xn = xn_scr[...]

    rows_of = lambda seg: slice(seg.row0, seg.row0 + seg.rows)
    y = None
    for b in range(NB):
        cols = slice(b * C, (b + 1) * C)
        _wait_all(fetch(b))
        w_xr, w_gate = col_f[b, 0].astype(BF16), col_f[b, 1].astype(BF16)
        gw = (0.5 * jnp.concatenate([sq_f[b, 0], sq_f[b, 1]], axis=1)).astype(BF16)
        w_out_rows = row_f[b].astype(BF16)
        col_h[b, 0], col_h[b, 1], gw_h[b], row_h[b] = w_xr, w_gate, gw, w_out_rows
        _start_all(emit(b))

        xr = _dot(xn, w_xr)
        g = _gelu_tanh(_dot(xn, w_gate))
        xcs = []
        for seg, c_in, c_out in zip(segs, conv_in, conv_out):
            P = (CONV_A - 1) * seg.S
            xr_scr[seg.conv0:seg.conv0 + P, :] = c_in[:, cols]
            xr_scr[seg.conv0 + P:seg.conv0 + P + seg.rows, :] = xr[rows_of(seg)]
            xcs.append(_rglru_conv(seg, cw_ref[:, cols], cb_ref[j:j + 1, cols], xr_scr, seg.conv0))
            c_out[:, cols] = xr_scr[seg.conv0:seg.conv0 + P, :]
        xc = jnp.concatenate(xcs, axis=0)
        a, u = _rglru_gate_math(_dot(xc.astype(BF16), gw), xc, 0.5 * gab_ref[b:b + 1, :],
                                0.5 * gxb_ref[b:b + 1, :], lam_ref[j:j + 1, cols])
        hs = [_rglru_scan(seg, a[rows_of(seg)], u[rows_of(seg)], g[rows_of(seg)],
                          hi.at[:, cols], ho.at[:, cols], 0)
              for seg, hi, ho in zip(segs, h_in, h_out)]
        part = _dot(jnp.concatenate(hs, axis=0).astype(BF16), w_out_rows)
        y = part if y is None else y + part

    for b in range(NB):
        _wait_all(emit(b))
    o_ref[...] = read_x() + _rms(y, npost_ref[layer:layer + 1, :])


def _rglru_small_layer(x, conv_in, h_in, layer, j, w, *, seqs_and_steps, x_sample=None):
    D, C, NB = D_MODEL, COL_BLOCK, N_COL_BLOCKS
    segs, R, conv_rows, _ = _segments(seqs_and_steps, CONV_A)
    n_layers, n_a = w["norm_mix_pre"].shape[0], w["rg_conv_b"].shape[0]
    hbm = pl.BlockSpec(memory_space=pl.ANY)
    layer_of = lambda arr: pl.BlockSpec((None,) + arr.shape[1:], lambda i: (j,) + (0,) * (arr.ndim - 1),
                                        pipeline_mode=pl.Buffered(1))
    if x_sample is None:
        x_args, x_specs, sample_in, staging = [x], [_resident(x.shape)], None, []
    else:
        S, T, _ = x_sample.shape
        x_args, x_specs, sample_in = [x, x_sample], [_resident(x.shape), hbm], (S, T)
        staging = [pltpu.VMEM((T, S, D), F32), pltpu.SemaphoreType.DMA((T,))]
    states = conv_in + h_in
    outs = pl.pallas_call(
        functools.partial(_rglru_small_kernel, segs, sample_in, layer, j),
        grid=(1,),
        in_specs=x_specs + [_resident(s.shape) for s in states]
                 + [_resident((n_layers, D)), _resident((n_layers, D)), layer_of(w["rg_conv_w"]),
                    _resident((n_a, D)), layer_of(w["rg_gate_a_b"]), layer_of(w["rg_gate_x_b"]),
                    _resident((n_a, D)), hbm, hbm, hbm, hbm],
        out_specs=[pl.BlockSpec((R, D), lambda i: (0, 0))]
                  + [pl.BlockSpec(s.shape, lambda i: (0, 0)) for s in states] + [hbm] * 4,
        out_shape=[jax.ShapeDtypeStruct((R, D), F32)] + [jax.ShapeDtypeStruct(s.shape, F32) for s in states]
                  + [jax.ShapeDtypeStruct((NB, D, C), BF16), jax.ShapeDtypeStruct((NB, D, C), BF16),
                     jax.ShapeDtypeStruct((NB, C, 2 * C), BF16), jax.ShapeDtypeStruct((D, D), BF16)],
        scratch_shapes=[pltpu.VMEM((R, D), BF16),
                        pltpu.VMEM((conv_rows, C), F32),
                        pltpu.VMEM((NB, 2, D, C), F32),
                        pltpu.VMEM((NB, 2, C, C), F32),
                        pltpu.VMEM((NB, C, D), F32),
                        pltpu.VMEM((NB, 2, D, C), BF16), pltpu.VMEM((NB, C, 2 * C), BF16),
                        pltpu.VMEM((NB, C, D), BF16),
                        pltpu.SemaphoreType.DMA((NB, 5)), pltpu.SemaphoreType.DMA((NB, 4))] + staging,
        compiler_params=_params(),
        name="rglru_mixer_small",
    )(*x_args, *states, w["norm_mix_pre"], w["norm_mix_post"], w["rg_conv_w"], w["rg_conv_b"],
      w["rg_gate_a_b"], w["rg_gate_x_b"], w["rg_lambda"], w["rg_w_in"], w["rg_gate_a_w"], w["rg_gate_x_w"],
      w["rg_w_out"])
    n = len(segs)
    bf16 = dict(w_xr=outs[2 * n + 1], w_gate=outs[2 * n + 2], gw=outs[2 * n + 3], w_out=outs[2 * n + 4])
    return outs[0], list(outs[1:1 + n]), list(outs[1 + n:1 + 2 * n]), bf16


def _rglru_kernel(seg, nblk, layer, j, batch_major_in,
                  x_ref, conv_in_ref, h_in_ref, npre_ref, npost_ref, wxr_ref, wgate_ref, cw_ref, cb_ref,
                  gw_ref, gab_ref, gxb_ref, lam_ref, wout_ref,
                  o_ref, conv_out_ref, h_out_ref, xn_scr, xr_scr, h_scr, *in_staging):
    S, TB, R = seg.S, seg.TB, seg.rows
    P = (CONV_A - 1) * S
    i = pl.program_id(0)
    slot = i % 2
    blocks = [slice(b * COL_BLOCK, (b + 1) * COL_BLOCK) for b in range(N_COL_BLOCKS)]

    if batch_major_in:
        xbuf, sems = in_staging

        def copies(which, step):
            t0 = pl.multiple_of(step * TB, TB)
            return [pltpu.make_async_copy(x_ref.at[s, pl.ds(t0, TB), :], xbuf.at[which, :, s, :],
                                          sems.at[which, s]) for s in range(S)]

    @pl.when(i == 0)
    def _():
        for b, cols in enumerate(blocks):
            xr_scr[b, 0:P, :] = conv_in_ref[:, cols]
            h_scr[b] = h_in_ref[:, cols]
        if batch_major_in:
            for c in copies(0, 0):
                c.start()

    if batch_major_in:
        @pl.when(i + 1 < nblk)
        def _():
            for c in copies(1 - slot, i + 1):
                c.start()

        for c in copies(slot, i):
            c.wait()
        read_x = lambda: xbuf[slot].reshape(R, D_MODEL)
    else:
        read_x = lambda: x_ref[...]

    xn_scr[...] = _rms(read_x(), npre_ref[layer:layer + 1, :]).astype(BF16)
    xn = xn_scr[...]

    gate_pre, xc, res = {}, {}, {}

    def input_matmuls(b):
        _rglru_branch_in(seg, xn, wxr_ref[b], xr_scr.at[b])
        gate_pre[b] = _dot(xn, wgate_ref[b])

    def conv_and_gate_matmul(b):
        xc[b] = _rglru_conv(seg, cw_ref[:, blocks[b]], cb_ref[j:j + 1, blocks[b]], xr_scr.at[b], 0)
        res[b] = _dot(xc[b].astype(BF16), gw_ref[b])

    y = None
    input_matmuls(0)
    conv_and_gate_matmul(0)
    input_matmuls(1)
    for b, cols in enumerate(blocks):
        if b + 1 < N_COL_BLOCKS:
            conv_and_gate_matmul(b + 1)
        if b + 2 < N_COL_BLOCKS:
            input_matmuls(b + 2)
        g = _gelu_tanh(gate_pre.pop(b))
        a, u = _rglru_gate_math(res.pop(b), xc.pop(b), 0.5 * gab_ref[b:b + 1, :], 0.5 * gxb_ref[b:b + 1, :],
                                lam_ref[j:j + 1, cols])
        hs = _rglru_scan(seg, a, u, g, h_scr.at[b], h_scr.at[b], 0)
        part = _dot(hs.astype(BF16), wout_ref[cols, :])
        y = part if y is None else y + part

    o_ref[...] = read_x() + _rms(y, npost_ref[layer:layer + 1, :])

    @pl.when(i == nblk - 1)
    def _():
        for b, cols in enumerate(blocks):
            conv_out_ref[:, cols] = xr_scr[b, 0:P, :]
            h_out_ref[:, cols] = h_scr[b]


def _rglru_layer(x, conv_in, h_in, layer, j, w, bf16, *, S, TB):
    D = D_MODEL
    batch_major_in = x.ndim == 3
    (seg,), R, conv_rows, _ = _segments(((S, TB),), CONV_A)
    nblk = x.size // D // R
    P = (CONV_A - 1) * S
    n_layers, n_a = w["norm_mix_pre"].shape[0], w["rg_conv_b"].shape[0]
    row_spec = pl.BlockSpec((R, D), lambda i: (i, 0))
    if batch_major_in:
        x_spec = pl.BlockSpec(memory_space=pl.ANY)
        staging = [pltpu.VMEM((2, TB, S, D), F32), pltpu.SemaphoreType.DMA((2, S))]
    else:
        x_spec, staging = row_spec, []
    layer_of = lambda arr: pl.BlockSpec((None,) + arr.shape[1:], lambda i: (j,) + (0,) * (arr.ndim - 1),
                                        pipeline_mode=pl.Buffered(1))
    return pl.pallas_call(
        functools.partial(_rglru_kernel, seg, nblk, layer, j, batch_major_in),
        grid=(nblk,),
        in_specs=[x_spec, _resident((P, D)), _resident((S, D)),
                  _resident((n_layers, D)), _resident((n_layers, D)),
                  _resident(bf16["w_xr"].shape), _resident(bf16["w_gate"].shape),
                  layer_of(w["rg_conv_w"]), _resident((n_a, D)), _resident(bf16["gw"].shape),
                  layer_of(w["rg_gate_a_b"]), layer_of(w["rg_gate_x_b"]), _resident((n_a, D)),
                  _resident((D, D))],
        out_specs=[row_spec, pl.BlockSpec((P, D), lambda i: (0, 0)), pl.BlockSpec((S, D), lambda i: (0, 0))],
        out_shape=[jax.ShapeDtypeStruct((nblk * R, D), F32), jax.ShapeDtypeStruct((P, D), F32),
                   jax.ShapeDtypeStruct((S, D), F32)],
        scratch_shapes=[pltpu.VMEM((R, D), BF16),
                        pltpu.VMEM((N_COL_BLOCKS, conv_rows, COL_BLOCK), F32),
                        pltpu.VMEM((N_COL_BLOCKS, S, COL_BLOCK), F32)]
                       + staging,
        compiler_params=_params(),
        name="rglru_mixer",
    )(x, conv_in, h_in, w["norm_mix_pre"], w["norm_mix_post"], bf16["w_xr"], bf16["w_gate"], w["rg_conv_w"],
      w["rg_conv_b"], bf16["gw"], w["rg_gate_a_b"], w["rg_gate_x_b"], w["rg_lambda"], bf16["w_out"])


def _sconv_conv(seg, cv, cw, region):
    S, R = seg.S, seg.rows
    P = (CONV_B - 1) * S
    c0 = seg.conv0
    region[c0 + P:c0 + P + R, :] = cv
    conv = region[c0:c0 + R, :] * cw[0:1]
    for k in range(1, CONV_B):
        conv = conv + region[c0 + k * S:c0 + k * S + R, :] * cw[k:k + 1]
    region[c0:c0 + P, :] = region[c0 + R:c0 + R + P, :]
    return conv


def _sconv_stream_kernel(segs, layer, j, x_ref, *refs):
    n = len(segs)
    conv_in = refs[0:n]
    npre_ref, npost_ref, wbg_ref, wcg_ref, wv_ref, cw_ref, wout_ref = refs[n:n + 7]
    o_ref = refs[n + 7]
    conv_out = refs[n + 8:2 * n + 8]
    wbg_b_ref, wcg_b_ref, wv_b_ref, wout_b_ref = refs[2 * n + 8:2 * n + 12]
    xn_scr, cv_scr, y_scr = refs[2 * n + 12:]
    b = pl.program_id(0)

    @pl.when(b == 0)
    def _():
        xn_scr[...] = _rms(x_ref[...], npre_ref[layer:layer + 1, :]).astype(BF16)

    w_bg = wbg_ref[...].astype(BF16)
    w_cg = wcg_ref[...].astype(BF16)
    w_v = wv_ref[...].astype(BF16)
    w_out_rows = wout_ref[...].astype(BF16)
    wbg_b_ref[...] = w_bg
    wcg_b_ref[...] = w_cg
    wv_b_ref[...] = w_v
    wout_b_ref[...] = w_out_rows

    xn = xn_scr[...]
    cv = _dot(xn, w_cg) * _dot(xn, w_v)
    convs = []
    for seg, c_in, c_out in zip(segs, conv_in, conv_out):
        P = (CONV_B - 1) * seg.S
        cv_scr[seg.conv0:seg.conv0 + P, :] = c_in[...]
        convs.append(_sconv_conv(seg, cv[seg.row0:seg.row0 + seg.rows], cw_ref[...], cv_scr))
        c_out[...] = cv_scr[seg.conv0:seg.conv0 + P, :]
    m = (_dot(xn, w_bg) * jnp.concatenate(convs, axis=0)).astype(BF16)
    _accumulate(y_scr, slice(None), _dot(m, w_out_rows), b)

    @pl.when(b == N_COL_BLOCKS - 1)
    def _():
        o_ref[...] = x_ref[...] + _rms(y_scr[...], npost_ref[layer:layer + 1, :])


def _sconv_stream_layer(x, conv_in, layer, j, w, *, seqs_and_steps):
    R, D = x.shape
    C = COL_BLOCK
    segs, _, conv_rows, _ = _segments(seqs_and_steps, CONV_B)
    n_layers = w["norm_mix_pre"].shape[0]
    col = lambda rows: pl.BlockSpec((rows, C), lambda b: (0, b))
    w_in_part = lambda k: pl.BlockSpec((None, D, C), lambda b: (j, 0, k * N_COL_BLOCKS + b))
    outs = pl.pallas_call(
        functools.partial(_sconv_stream_kernel, segs, layer, j),
        grid=(N_COL_BLOCKS,),
        in_specs=[_resident((R, D))] + [col(c.shape[0]) for c in conv_in]
                 + [_resident((n_layers, D)), _resident((n_layers, D)), w_in_part(0), w_in_part(1), w_in_part(2),
                    pl.BlockSpec((None, CONV_B, C), lambda b: (j, 0, b)),
                    pl.BlockSpec((None, C, D), lambda b: (j, b, 0))],
        out_specs=[pl.BlockSpec((R, D), lambda b: (0, 0))] + [col(c.shape[0]) for c in conv_in]
                  + [_chunk_major_out(D)] * 3 + [pl.BlockSpec((C, D), lambda b: (b, 0))],
        out_shape=[jax.ShapeDtypeStruct((R, D), F32)] + [jax.ShapeDtypeStruct(c.shape, F32) for c in conv_in]
                  + [jax.ShapeDtypeStruct((N_COL_BLOCKS, D, C), BF16)] * 3 + [jax.ShapeDtypeStruct((D, D), BF16)],
        scratch_shapes=[pltpu.VMEM((R, D), BF16),
                        pltpu.VMEM((conv_rows, C), F32),
                        pltpu.VMEM((R, D), F32)],
        compiler_params=_params(),
        name="sconv_mixer_small",
    )(x, *conv_in, w["norm_mix_pre"], w["norm_mix_post"], w["sc_w_in"], w["sc_w_in"], w["sc_w_in"],
      w["sc_conv_w"], w["sc_w_out"])
    n = len(segs)
    bf16 = dict(w_bg=outs[n + 1], w_cg=outs[n + 2], w_v=outs[n + 3], w_out=outs[n + 4])
    return outs[0], list(outs[1:1 + n]), bf16


def _sconv_small_kernel(segs, layer, j, x_ref, *refs):
    D, C, NB = D_MODEL, COL_BLOCK, N_COL_BLOCKS
    n = len(segs)
    conv_in = refs[0:n]
    npre_ref, npost_ref, cw_ref, win_hbm, wout_hbm = refs[n:n + 5]
    o_ref = refs[n + 5]
    conv_out = refs[n + 6:2 * n + 6]
    wbg_b_hbm, wcg_b_hbm, wv_b_hbm, wout_b_hbm = refs[2 * n + 6:2 * n + 10]
    xn_scr, cv_scr, col_f, row_f, col_h, row_h, in_sems, out_sems = refs[2 * n + 10:]

    def fetch(b):
        return ([pltpu.make_async_copy(win_hbm.at[j, :, pl.ds(k * D + b * C, C)], col_f.at[b, k], in_sems.at[b, k])
                 for k in range(3)]
                + [pltpu.make_async_copy(wout_hbm.at[j, pl.ds(b * C, C), :], row_f.at[b], in_sems.at[b, 3])])

    def emit(b):
        return ([pltpu.make_async_copy(col_h.at[b, k], dst.at[b], out_sems.at[b, k])
                 for k, dst in enumerate((wbg_b_hbm, wcg_b_hbm, wv_b_hbm))]
                + [pltpu.make_async_copy(row_h.at[b], wout_b_hbm.at[pl.ds(b * C, C), :], out_sems.at[b, 3])])

    for b in range(NB):
        _start_all(fetch(b))
    xn_scr[...] = _rms(x_ref[...], npre_ref[layer:layer + 1, :]).astype(BF16)
    xn = xn_scr[...]

    y = None
    for b in range(NB):
        cols = slice(b * C, (b + 1) * C)
        _wait_all(fetch(b))
        w_bg, w_cg, w_v = (col_f[b, k].astype(BF16) for k in range(3))
        w_out_rows = row_f[b].astype(BF16)
        col_h[b, 0], col_h[b, 1], col_h[b, 2], row_h[b] = w_bg, w_cg, w_v, w_out_rows
        _start_all(emit(b))

        cv = _dot(xn, w_cg) * _dot(xn, w_v)
        convs = []
        for seg, c_in, c_out in zip(segs, conv_in, conv_out):
            P = (CONV_B - 1) * seg.S
            cv_scr[seg.conv0:seg.conv0 + P, :] = c_in[:, cols]
            convs.append(_sconv_conv(seg, cv[seg.row0:seg.row0 + seg.rows], cw_ref[:, cols], cv_scr))
            c_out[:, cols] = cv_scr[seg.conv0:seg.conv0 + P, :]
        m = (_dot(xn, w_bg) * jnp.concatenate(convs, axis=0)).astype(BF16)
        part = _dot(m, w_out_rows)
        y = part if y is None else y + part

    for b in range(NB):
        _wait_all(emit(b))
    o_ref[...] = x_ref[...] + _rms(y, npost_ref[layer:layer + 1, :])


def _sconv_small_layer(x, conv_in, layer, j, w, *, seqs_and_steps):
    R, D = x.shape
    C, NB = COL_BLOCK, N_COL_BLOCKS
    segs, _, conv_rows, _ = _segments(seqs_and_steps, CONV_B)
    n_layers = w["norm_mix_pre"].shape[0]
    hbm = pl.BlockSpec(memory_space=pl.ANY)
    outs = pl.pallas_call(
        functools.partial(_sconv_small_kernel, segs, layer, j),
        grid=(1,),
        in_specs=[_resident((R, D))] + [_resident(c.shape) for c in conv_in]
                 + [_resident((n_layers, D)), _resident((n_layers, D)),
                    pl.BlockSpec((None, CONV_B, D), lambda i: (j, 0, 0), pipeline_mode=pl.Buffered(1)), hbm, hbm],
        out_specs=[pl.BlockSpec((R, D), lambda i: (0, 0))]
                  + [pl.BlockSpec(c.shape, lambda i: (0, 0)) for c in conv_in] + [hbm] * 4,
        out_shape=[jax.ShapeDtypeStruct((R, D), F32)] + [jax.ShapeDtypeStruct(c.shape, F32) for c in conv_in]
                  + [jax.ShapeDtypeStruct((NB, D, C), BF16)] * 3 + [jax.ShapeDtypeStruct((D, D), BF16)],
        scratch_shapes=[pltpu.VMEM((R, D), BF16),
                        pltpu.VMEM((conv_rows, C), F32),
                        pltpu.VMEM((NB, 3, D, C), F32),
                        pltpu.VMEM((NB, C, D), F32),
                        pltpu.VMEM((NB, 3, D, C), BF16), pltpu.VMEM((NB, C, D), BF16),
                        pltpu.SemaphoreType.DMA((NB, 4)), pltpu.SemaphoreType.DMA((NB, 4))],
        compiler_params=_params(),
        name="sconv_mixer_small",
    )(x, *conv_in, w["norm_mix_pre"], w["norm_mix_post"], w["sc_conv_w"], w["sc_w_in"], w["sc_w_out"])
    n = len(segs)
    bf16 = dict(w_bg=outs[n + 1], w_cg=outs[n + 2], w_v=outs[n + 3], w_out=outs[n + 4])
    return outs[0], list(outs[1:1 + n]), bf16


def _sconv_kernel(seg, nblk, layer, x_ref, conv_in_ref, npre_ref, npost_ref, wbg_ref, wcg_ref, wv_ref,
                  cw_ref, wout_ref, o_ref, conv_out_ref, xn_scr, cv_scr, m_scr):
    P = (CONV_B - 1) * seg.S
    i = pl.program_id(0)
    blocks = [slice(b * COL_BLOCK, (b + 1) * COL_BLOCK) for b in range(N_COL_BLOCKS)]

    @pl.when(i == 0)
    def _():
        for b, cols in enumerate(blocks):
            cv_scr[b, 0:P, :] = conv_in_ref[:, cols]

    xn_scr[...] = _rms(x_ref[...], npre_ref[layer:layer + 1, :]).astype(BF16)
    xn = xn_scr[...]
    for b, cols in enumerate(blocks):
        cv = _dot(xn, wcg_ref[b]) * _dot(xn, wv_ref[b])
        conv = _sconv_conv(seg, cv, cw_ref[:, cols], cv_scr.at[b])
        m_scr[:, cols] = (_dot(xn, wbg_ref[b]) * conv).astype(BF16)
    y = _dot(m_scr[...], wout_ref[...])
    o_ref[...] = x_ref[...] + _rms(y, npost_ref[layer:layer + 1, :])

    @pl.when(i == nblk - 1)
    def _():
        for b, cols in enumerate(blocks):
            conv_out_ref[:, cols] = cv_scr[b, 0:P, :]


def _sconv_layer(x, conv_in, layer, j, w, bf16, *, S, TB):
    N, D = x.shape
    (seg,), R, conv_rows, _ = _segments(((S, TB),), CONV_B)
    nblk = N // R
    P = (CONV_B - 1) * S
    n_layers = w["norm_mix_pre"].shape[0]
    row_spec = pl.BlockSpec((R, D), lambda i: (i, 0))
    return pl.pallas_call(
        functools.partial(_sconv_kernel, seg, nblk, layer),
        grid=(nblk,),
        in_specs=[row_spec, _resident((P, D)), _resident((n_layers, D)), _resident((n_layers, D)),
                  _resident(bf16["w_bg"].shape), _resident(bf16["w_cg"].shape), _resident(bf16["w_v"].shape),
                  pl.BlockSpec((None, CONV_B, D), lambda i: (j, 0, 0), pipeline_mode=pl.Buffered(1)),
                  _resident((D, D))],
        out_specs=[row_spec, pl.BlockSpec((P, D), lambda i: (0, 0))],
        out_shape=[jax.ShapeDtypeStruct((N, D), F32), jax.ShapeDtypeStruct((P, D), F32)],
        scratch_shapes=[pltpu.VMEM((R, D), BF16),
                        pltpu.VMEM((N_COL_BLOCKS, conv_rows, COL_BLOCK), F32),
                        pltpu.VMEM((R, D), BF16)],
        compiler_params=_params(),
        name="sconv_mixer",
    )(x, conv_in, w["norm_mix_pre"], w["norm_mix_post"], bf16["w_bg"], bf16["w_cg"], bf16["w_v"],
      w["sc_conv_w"], bf16["w_out"])


def _swiglu(g, u):
    return (g * jax.nn.sigmoid(g) * u).astype(BF16)


def _ffn_stream_kernel(layer, nchunk, sample_out, x_ref, npre_ref, npost_ref, wg_hbm, wu_hbm, wd_hbm,
                       o_ref, wg_b_hbm, wu_b_hbm, wd_b_hbm,
                       xn_scr, wg_f, wu_f, wd_f, wg_h, wu_h, wd_h, in_sems, out_sems, *out_staging):
    F = FF_CHUNK

    NB = STREAM_BUFFERS

    def fetch(c):
        k, cols = c % NB, pl.ds(c * F, F)
        return [pltpu.make_async_copy(wg_hbm.at[layer, :, cols], wg_f.at[k], in_sems.at[0, k]),
                pltpu.make_async_copy(wu_hbm.at[layer, :, cols], wu_f.at[k], in_sems.at[1, k]),
                pltpu.make_async_copy(wd_hbm.at[layer, cols, :], wd_f.at[k], in_sems.at[2, k])]

    def emit(c):
        k = c % NB
        return [pltpu.make_async_copy(wg_h.at[k], wg_b_hbm.at[c], out_sems.at[0, k]),
                pltpu.make_async_copy(wu_h.at[k], wu_b_hbm.at[c], out_sems.at[1, k]),
                pltpu.make_async_copy(wd_h.at[k], wd_b_hbm.at[pl.ds(c * F, F), :], out_sems.at[2, k])]

    for c in range(min(NB, nchunk)):
        for cp in fetch(c):
            cp.start()
    xn_scr[...] = _rms(x_ref[...], npre_ref[layer:layer + 1, :]).astype(BF16)
    xn = xn_scr[...]

    y = None
    for c in range(nchunk):
        k = c % NB
        for cp in fetch(c):
            cp.wait()
        if c >= NB:
            for cp in emit(c - NB):
                cp.wait()
        wg, wu, wd = wg_f[k].astype(BF16), wu_f[k].astype(BF16), wd_f[k].astype(BF16)
        wg_h[k], wu_h[k], wd_h[k] = wg, wu, wd
        for cp in emit(c):
            cp.start()
        if c + NB < nchunk:
            for cp in fetch(c + NB):
                cp.start()
        part = _dot(_swiglu(_dot(xn, wg), _dot(xn, wu)), wd)
        y = part if y is None else y + part
    for c in range(max(nchunk - NB, 0), nchunk):
        for cp in emit(c):
            cp.wait()

    out = x_ref[...] + _rms(y, npost_ref[layer:layer + 1, :])
    if sample_out is None:
        o_ref[...] = out
    else:
        row0, S, T = sample_out
        obuf, sems = out_staging
        obuf[...] = out[row0:row0 + T * S].reshape(T, S, out.shape[-1])
        copies = [pltpu.make_async_copy(obuf.at[t], o_ref.at[:, t, :], sems.at[t]) for t in range(T)]
        for cp in copies:
            cp.start()
        for cp in copies:
            cp.wait()


def _ffn_stream_layer(x, layer, w, *, sample_out=None):
    R, D = x.shape
    F, NB = FF_CHUNK, STREAM_BUFFERS
    n_layers = w["norm_ffn_pre"].shape[0]
    nchunk = D_FF // F
    hbm = pl.BlockSpec(memory_space=pl.ANY)
    if sample_out is None:
        o_spec, o_shape, staging = pl.BlockSpec((R, D), lambda i: (0, 0)), jax.ShapeDtypeStruct((R, D), F32), []
    else:
        _, S, T = sample_out
        o_spec, o_shape = hbm, jax.ShapeDtypeStruct((S, T, D), F32)
        staging = [pltpu.VMEM((T, S, D), F32), pltpu.SemaphoreType.DMA((T,))]
    outs = pl.pallas_call(
        functools.partial(_ffn_stream_kernel, layer, nchunk, sample_out),
        grid=(1,),
        in_specs=[_resident((R, D)), _resident((n_layers, D)), _resident((n_layers, D)), hbm, hbm, hbm],
        out_specs=[o_spec, hbm, hbm, hbm],
        out_shape=[o_shape, jax.ShapeDtypeStruct((nchunk, D, F), BF16),
                   jax.ShapeDtypeStruct((nchunk, D, F), BF16), jax.ShapeDtypeStruct((D_FF, D), BF16)],
        scratch_shapes=[pltpu.VMEM((R, D), BF16),
                        pltpu.VMEM((NB, D, F), F32), pltpu.VMEM((NB, D, F), F32), pltpu.VMEM((NB, F, D), F32),
                        pltpu.VMEM((NB, D, F), BF16), pltpu.VMEM((NB, D, F), BF16), pltpu.VMEM((NB, F, D), BF16),
                        pltpu.SemaphoreType.DMA((3, NB)), pltpu.SemaphoreType.DMA((3, NB))] + staging,
        compiler_params=_params(),
        name="swiglu_ffn_small",
    )(x, w["norm_ffn_pre"], w["norm_ffn_post"], w["ffn_w_gate"], w["ffn_w_up"], w["ffn_w_down"])
    return outs[0], dict(wg=outs[1], wu=outs[2], wd=outs[3])


def _ffn_kernel(layer, nblk, batch_major_out, x_ref, npre_ref, npost_ref, wg_ref, wu_ref, wd_ref, o_ref,
                xn_scr, act_scr, *out_staging):
    R = x_ref.shape[0]
    i = pl.program_id(0)
    slot = i % 2
    if batch_major_out:
        obuf, sems = out_staging
        S, TB = o_ref.shape[0], R // o_ref.shape[0]

        def copies(which, step):
            t0 = pl.multiple_of(step * TB, TB)
            return [pltpu.make_async_copy(obuf.at[which, :, s, :],
                                          o_ref.at[s, pl.ds(t0, TB), :], sems.at[which, s]) for s in range(S)]

        @pl.when(i >= 2)
        def _():
            for c in copies(slot, i - 2):
                c.wait()

    halves = [slice(0, R // 2), slice(R // 2, R)]
    for rows in halves:
        xn_scr[rows, :] = _rms(x_ref[rows, :], npre_ref[layer:layer + 1, :]).astype(BF16)
    for rows in halves:
        xn = xn_scr[rows, :]
        for c in range(D_FF // FF_CHUNK):
            cols = slice(c * FF_CHUNK, (c + 1) * FF_CHUNK)
            act_scr[rows, cols] = _swiglu(_dot(xn, wg_ref[c]), _dot(xn, wu_ref[c]))
    ys = [_dot(act_scr[rows, :], wd_ref[...]) for rows in halves]
    for rows, y in zip(halves, ys):
        out = x_ref[rows, :] + _rms(y, npost_ref[layer:layer + 1, :])
        if batch_major_out:
            obuf[slot, rows.start // S:rows.stop // S] = out.reshape(-1, S, out.shape[-1])
        else:
            o_ref[rows, :] = out

    if batch_major_out:
        for c in copies(slot, i):
            c.start()

        @pl.when(i == nblk - 1)
        def _():
            if nblk >= 2:
                for c in copies(1 - slot, i - 1):
                    c.wait()
            for c in copies(slot, i):
                c.wait()


def _ffn_layer(x, layer, w, bf16, *, R, batch_major_seqs=None):
    N, D = x.shape
    n_layers = w["norm_ffn_pre"].shape[0]
    nblk = N // R
    row_spec = pl.BlockSpec((R, D), lambda i: (i, 0))
    if batch_major_seqs is None:
        out_spec, out_shape, staging = row_spec, jax.ShapeDtypeStruct((N, D), F32), []
    else:
        S = batch_major_seqs
        out_spec = pl.BlockSpec(memory_space=pl.ANY)
        out_shape = jax.ShapeDtypeStruct((S, N // S, D), F32)
        staging = [pltpu.VMEM((2, R // S, S, D), F32), pltpu.SemaphoreType.DMA((2, S))]
    return pl.pallas_call(
        functools.partial(_ffn_kernel, layer, nblk, batch_major_seqs is not None),
        grid=(nblk,),
        in_specs=[row_spec, _resident((n_layers, D)), _resident((n_layers, D)),
                  _resident(bf16["wg"].shape), _resident(bf16["wu"].shape), _resident((D_FF, D))],
        out_specs=out_spec,
        out_shape=out_shape,
        scratch_shapes=[pltpu.VMEM((R, D), BF16), pltpu.VMEM((R, D_FF), BF16)] + staging,
        compiler_params=_params(),
        name="swiglu_ffn",
    )(x, w["norm_ffn_pre"], w["norm_ffn_post"], bf16["wg"], bf16["wu"], bf16["wd"])


def _to_time_major(a):
    S, K, D = a.shape
    return jnp.swapaxes(a, 0, 1).reshape(K * S, D)


def _from_time_major(a, S):
    KS, D = a.shape
    return jnp.swapaxes(a.reshape(KS // S, S, D), 0, 1)


def kernel(x_prompt, x_sample, state_rglru_conv, state_rglru_h, state_sconv, meta_tokens, norm_mix_pre, norm_mix_post, norm_ffn_pre, norm_ffn_post, rg_w_in, rg_conv_w, rg_conv_b, rg_gate_a_w, rg_gate_a_b, rg_gate_x_w, rg_gate_x_b, rg_lambda, rg_w_out, sc_w_in, sc_conv_w, sc_w_out, ffn_w_gate, ffn_w_up, ffn_w_down):
    D = D_MODEL
    depth = norm_mix_pre.shape[0]
    batch, seq, _ = x_prompt.shape
    dec_batch, dec_seq, _ = x_sample.shape
    w = dict(norm_mix_pre=norm_mix_pre, norm_mix_post=norm_mix_post, norm_ffn_pre=norm_ffn_pre,
             norm_ffn_post=norm_ffn_post, rg_w_in=rg_w_in, rg_conv_w=rg_conv_w, rg_conv_b=rg_conv_b,
             rg_gate_a_w=rg_gate_a_w, rg_gate_a_b=rg_gate_a_b, rg_gate_x_w=rg_gate_x_w, rg_gate_x_b=rg_gate_x_b,
             rg_lambda=rg_lambda, rg_w_out=rg_w_out, sc_w_in=sc_w_in, sc_conv_w=sc_conv_w, sc_w_out=sc_w_out,
             ffn_w_gate=ffn_w_gate, ffn_w_up=ffn_w_up, ffn_w_down=ffn_w_down)

    x = jnp.broadcast_to(meta_tokens[:, None, :], (N_META, batch, D)).reshape(N_META * batch, D)
    small = ((batch, N_META), (dec_batch, dec_seq))
    mixer_bf16, ffn_bf16 = [], []
    rg_conv_s, rg_h_s, sc_s = [], [], []
    for i in range(depth):
        j = i // 2
        if i % 2 == 0:
            conv0 = [jnp.zeros(((CONV_A - 1) * batch, D), F32), _to_time_major(state_rglru_conv[j])]
            h0 = [jnp.zeros((batch, D), F32), state_rglru_h[j]]
            x, cb, hT, wb = _rglru_small_layer(x, conv0, h0, i, j, w, seqs_and_steps=small,
                                                x_sample=x_sample if i == 0 else None)
            rg_conv_s.append(cb)
            rg_h_s.append(hT)
        else:
            conv0 = [jnp.zeros(((CONV_B - 1) * batch, D), F32), _to_time_major(state_sconv[j])]
            x, cb, wb = _sconv_small_layer(x, conv0, i, j, w, seqs_and_steps=small)
            sc_s.append(cb)
        mixer_bf16.append(wb)
        sample_out = (N_META * batch, dec_batch, dec_seq) if i == depth - 1 else None
        x, wb = _ffn_stream_layer(x, i, w, sample_out=sample_out)
        ffn_bf16.append(wb)
    y_sample = x

    x = x_prompt
    rg_conv_p, rg_h_p, sc_p = [], [], []
    for i in range(depth):
        j = i // 2
        if i % 2 == 0:
            x, cb, hT = _rglru_layer(x, rg_conv_s[j][0], rg_h_s[j][0], i, j, w, mixer_bf16[i], S=batch, TB=PROMPT_TB)
            rg_conv_p.append(cb)
            rg_h_p.append(hT)
        else:
            x, cb = _sconv_layer(x, sc_s[j][0], i, j, w, mixer_bf16[i], S=batch, TB=PROMPT_TB)
            sc_p.append(cb)
        x = _ffn_layer(x, i, w, ffn_bf16[i], R=batch * PROMPT_TB // 2,
                       batch_major_seqs=batch if i == depth - 1 else None)
    y_prompt = x

    return (y_prompt, y_sample,
            jnp.stack([_from_time_major(c, batch) for c in rg_conv_p]), jnp.stack(rg_h_p),
            jnp.stack([_from_time_major(c, batch) for c in sc_p]),
            jnp.stack([_from_time_major(c[1], dec_batch) for c in rg_conv_s]), jnp.stack([h[1] for h in rg_h_s]),
            jnp.stack([_from_time_major(c[1], dec_batch) for c in sc_s]))
```

```python
import functools
from typing import NamedTuple

import jax
import jax.numpy as jnp
from jax import lax
from jax.experimental import pallas as pl
from jax.experimental.pallas import tpu as pltpu

D_MODEL = 1024
D_FF = 2816
N_META = 16
COL_BLOCK = 256
N_COL_BLOCKS = D_MODEL // COL_BLOCK
CONV_A = 4
CONV_B = 3
RG_C = 8.0
EPS = 1e-6

SUBLANES = 8
FF_CHUNK = 256
STREAM_BUFFERS = 3
PROMPT_TB = 128
VMEM_LIMIT_BYTES = 56 * 1024 * 1024

F32 = jnp.float32
BF16 = jnp.bfloat16


class _Seg(NamedTuple):
    S: int
    TB: int
    row0: int
    conv0: int
    h0: int

    @property
    def rows(self):
        return self.S * self.TB


def _segments(seqs_and_steps, taps):
    segs, row0, conv0, h0 = [], 0, 0, 0
    for S, TB in seqs_and_steps:
        segs.append(_Seg(S, TB, row0, conv0, h0))
        row0 += S * TB
        conv0 += (taps - 1) * S + S * TB
        h0 += S
    return tuple(segs), row0, conv0, h0


def _rms(x, w):
    ms = jnp.mean(x * x, axis=-1, keepdims=True)
    return x * lax.rsqrt(ms + EPS) * w


def _dot(a, b):
    return jnp.dot(a, b, preferred_element_type=F32)


def _gelu_tanh(x):
    c = 0.7978845608028654
    hx = 0.5 * x
    return hx + hx * jnp.tanh(x * (c + (c * 0.044715) * (x * x)))


def _resident(shape):
    zeros = (0,) * len(shape)
    return pl.BlockSpec(shape, lambda i: zeros, pipeline_mode=pl.Buffered(1))


def _params():
    return pltpu.CompilerParams(dimension_semantics=("arbitrary",), vmem_limit_bytes=VMEM_LIMIT_BYTES)


def _start_all(copies):
    for cp in copies:
        cp.start()


def _wait_all(copies):
    for cp in copies:
        cp.wait()


def _rglru_branch_in(seg, xn, w_xr, region):
    P = (CONV_A - 1) * seg.S
    region[seg.conv0 + P:seg.conv0 + P + seg.rows, :] = _dot(xn, w_xr)


def _rglru_conv(seg, cw, cb, region, carry_to):
    S, R = seg.S, seg.rows
    P = (CONV_A - 1) * S
    c0 = seg.conv0
    xc = region[c0:c0 + R, :] * cw[0:1]
    for k in range(1, CONV_A):
        xc = xc + region[c0 + k * S:c0 + k * S + R, :] * cw[k:k + 1]
    xc = xc + cb
    if carry_to is not None:
        region[carry_to:carry_to + P, :] = region[c0 + R:c0 + R + P, :]
    return xc


def _rglru_gate_math(res, xc, half_gab, half_gxb, lam):
    half_c_sp = (-0.5 * RG_C) * jax.nn.softplus(-lam)
    tr = jnp.tanh(res[:, 0:COL_BLOCK] + half_gab)
    log_a = half_c_sp * tr + half_c_sp
    ig = 0.5 * jnp.tanh(res[:, COL_BLOCK:2 * COL_BLOCK] + half_gxb) + 0.5
    a = jnp.exp(log_a)
    m2 = jnp.tanh(log_a) * (-1.0 - a * a)
    u = jnp.where(m2 > 0.0, m2 * lax.rsqrt(m2), 0.0) * (ig * xc)
    return a, u


def _rglru_scan(seg, a, u, g, h_read, h_write, h_row0):
    S, TB = seg.S, seg.TB
    groups = S // SUBLANES
    pieces = [None] * (TB * groups)
    for c in range(groups):
        hrows = slice(h_row0 + c * SUBLANES, h_row0 + (c + 1) * SUBLANES)
        h = h_read[hrows, :]
        for t in range(TB):
            r = t * S + c * SUBLANES
            h = a[r:r + SUBLANES] * h + u[r:r + SUBLANES]
            pieces[t * groups + c] = h * g[r:r + SUBLANES]
        h_write[hrows, :] = h
    return jnp.concatenate(pieces, axis=0)


def _rglru_small_kernel(segs, sample_in, layer, j, x_ref, *refs):
    D, C, NB = D_MODEL, COL_BLOCK, N_COL_BLOCKS
    if sample_in is not None:
        xs_hbm, refs = refs[0], refs[1:]
    n = len(segs)
    conv_in, h_in = refs[0:n], refs[n:2 * n]
    (npre_ref, npost_ref, cw_ref, cb_ref, gab_ref, gxb_ref, lam_ref,
     win_hbm, gaw_hbm, gxw_hbm, wout_hbm) = refs[2 * n:2 * n + 11]
    o_ref = refs[2 * n + 11]
    conv_out, h_out = refs[2 * n + 12:3 * n + 12], refs[3 * n + 12:4 * n + 12]
    wxr_b_hbm, wgate_b_hbm, gw_b_hbm, wout_b_hbm = refs[4 * n + 12:4 * n + 16]
    (xn_scr, xr_scr, col_f, sq_f, row_f, col_h, gw_h, row_h, in_sems, out_sems,
     *in_staging) = refs[4 * n + 16:]

    def fetch(b):
        cols = pl.ds(b * C, C)
        return [pltpu.make_async_copy(win_hbm.at[j, :, pl.ds(D + b * C, C)], col_f.at[b, 0], in_sems.at[b, 0]),
                pltpu.make_async_copy(win_hbm.at[j, :, cols], col_f.at[b, 1], in_sems.at[b, 1]),
                pltpu.make_async_copy(gaw_hbm.at[j, b], sq_f.at[b, 0], in_sems.at[b, 2]),
                pltpu.make_async_copy(gxw_hbm.at[j, b], sq_f.at[b, 1], in_sems.at[b, 3]),
                pltpu.make_async_copy(wout_hbm.at[j, cols, :], row_f.at[b], in_sems.at[b, 4])]

    def emit(b):
        return [pltpu.make_async_copy(col_h.at[b, 0], wxr_b_hbm.at[b], out_sems.at[b, 0]),
                pltpu.make_async_copy(col_h.at[b, 1], wgate_b_hbm.at[b], out_sems.at[b, 1]),
                pltpu.make_async_copy(gw_h.at[b], gw_b_hbm.at[b], out_sems.at[b, 2]),
                pltpu.make_async_copy(row_h.at[b], wout_b_hbm.at[pl.ds(b * C, C), :], out_sems.at[b, 3])]

    if sample_in is None:
        read_x = lambda: x_ref[...]
    else:
        S, T = sample_in
        xs_scr, sems = in_staging
        x_copies = [pltpu.make_async_copy(xs_hbm.at[:, t, :], xs_scr.at[t], sems.at[t]) for t in range(T)]
        _start_all(x_copies)
        read_x = lambda: jnp.concatenate([x_ref[...], xs_scr[...].reshape(T * S, D)], axis=0)
    for b in range(NB):
        _start_all(fetch(b))
    if sample_in is not None:
        _wait_all(x_copies)
    xn_scr[...] = _rms(read_x(), npre_ref[layer:layer + 1, :]).astype(BF16)
    xn = xn_scr[...]

    rows_of = lambda seg: slice(seg.row0, seg.row0 + seg.rows)
    y = None
    for b in range(NB):
        cols = slice(b * C, (b + 1) * C)
        _wait_all(fetch(b))
        w_xr, w_gate = col_f[b, 0].astype(BF16), col_f[b, 1].astype(BF16)
        gw = (0.5 * jnp.concatenate([sq_f[b, 0], sq_f[b, 1]], axis=1)).astype(BF16)
        w_out_rows = row_f[b].astype(BF16)
        col_h[b, 0], col_h[b, 1], gw_h[b], row_h[b] = w_xr, w_gate, gw, w_out_rows
        _start_all(emit(b))

        xr = _dot(xn, w_xr)
        g = _gelu_tanh(_dot(xn, w_gate))
        xcs = []
        for seg, c_in, c_out in zip(segs, conv_in, conv_out):
            P = (CONV_A - 1) * seg.S
            xr_scr[seg.conv0:seg.conv0 + P, :] = c_in[:, cols]
            xr_scr[seg.conv0 + P:seg.conv0 + P + seg.rows, :] = xr[rows_of(seg)]
            xcs.append(_rglru_conv(seg, cw_ref[:, cols], cb_ref[j:j + 1, cols], xr_scr, seg.conv0))
            c_out[:, cols] = xr_scr[seg.conv0:seg.conv0 + P, :]
        xc = jnp.concatenate(xcs, axis=0)
        a, u = _rglru_gate_math(_dot(xc.astype(BF16), gw), xc, 0.5 * gab_ref[b:b + 1, :],
                                0.5 * gxb_ref[b:b + 1, :], lam_ref[j:j + 1, cols])
        hs = [_rglru_scan(seg, a[rows_of(seg)], u[rows_of(seg)], g[rows_of(seg)],
                          hi.at[:, cols], ho.at[:, cols], 0)
              for seg, hi, ho in zip(segs, h_in, h_out)]
        part = _dot(jnp.concatenate(hs, axis=0).astype(BF16), w_out_rows)
        y = part if y is None else y + part

    for b in range(NB):
        _wait_all(emit(b))
    o_ref[...] = read_x() + _rms(y, npost_ref[layer:layer + 1, :])


def _rglru_small_layer(x, conv_in, h_in, layer, j, w, *, seqs_and_steps, x_sample=None):
    D, C, NB = D_MODEL, COL_BLOCK, N_COL_BLOCKS
    segs, R, conv_rows, _ = _segments(seqs_and_steps, CONV_A)
    n_layers, n_a = w["norm_mix_pre"].shape[0], w["rg_conv_b"].shape[0]
    hbm = pl.BlockSpec(memory_space=pl.ANY)
    layer_of = lambda arr: pl.BlockSpec((None,) + arr.shape[1:], lambda i: (j,) + (0,) * (arr.ndim - 1),
                                        pipeline_mode=pl.Buffered(1))
    if x_sample is None:
        x_args, x_specs, sample_in, staging = [x], [_resident(x.shape)], None, []
    else:
        S, T, _ = x_sample.shape
        x_args, x_specs, sample_in = [x, x_sample], [_resident(x.shape), hbm], (S, T)
        staging = [pltpu.VMEM((T, S, D), F32), pltpu.SemaphoreType.DMA((T,))]
    states = conv_in + h_in
    outs = pl.pallas_call(
        functools.partial(_rglru_small_kernel, segs, sample_in, layer, j),
        grid=(1,),
        in_specs=x_specs + [_resident(s.shape) for s in states]
                 + [_resident((n_layers, D)), _resident((n_layers, D)), layer_of(w["rg_conv_w"]),
                    _resident((n_a, D)), layer_of(w["rg_gate_a_b"]), layer_of(w["rg_gate_x_b"]),
                    _resident((n_a, D)), hbm, hbm, hbm, hbm],
        out_specs=[pl.BlockSpec((R, D), lambda i: (0, 0))]
                  + [pl.BlockSpec(s.shape, lambda i: (0, 0)) for s in states] + [hbm] * 4,
        out_shape=[jax.ShapeDtypeStruct((R, D), F32)] + [jax.ShapeDtypeStruct(s.shape, F32) for s in states]
                  + [jax.ShapeDtypeStruct((NB, D, C), BF16), jax.ShapeDtypeStruct((NB, D, C), BF16),
                     jax.ShapeDtypeStruct((NB, C, 2 * C), BF16), jax.ShapeDtypeStruct((D, D), BF16)],
        scratch_shapes=[pltpu.VMEM((R, D), BF16),
                        pltpu.VMEM((conv_rows, C), F32),
                        pltpu.VMEM((NB, 2, D, C), F32),
                        pltpu.VMEM((NB, 2, C, C), F32),
                        pltpu.VMEM((NB, C, D), F32),
                        pltpu.VMEM((NB, 2, D, C), BF16), pltpu.VMEM((NB, C, 2 * C), BF16),
                        pltpu.VMEM((NB, C, D), BF16),
                        pltpu.SemaphoreType.DMA((NB, 5)), pltpu.SemaphoreType.DMA((NB, 4))] + staging,
        compiler_params=_params(),
        name="rglru_mixer_small",
    )(*x_args, *states, w["norm_mix_pre"], w["norm_mix_post"], w["rg_conv_w"], w["rg_conv_b"],
      w["rg_gate_a_b"], w["rg_gate_x_b"], w["rg_lambda"], w["rg_w_in"], w["rg_gate_a_w"], w["rg_gate_x_w"],
      w["rg_w_out"])
    n = len(segs)
    bf16 = dict(w_xr=outs[2 * n + 1], w_gate=outs[2 * n + 2], gw=outs[2 * n + 3], w_out=outs[2 * n + 4])
    return outs[0], list(outs[1:1 + n]), list(outs[1 + n:1 + 2 * n]), bf16


def _rglru_kernel(seg, nblk, layer, j, batch_major_in,
                  x_ref, conv_in_ref, h_in_ref, npre_ref, npost_ref, wxr_ref, wgate_ref, cw_ref, cb_ref,
                  gw_ref, gab_ref, gxb_ref, lam_ref, wout_ref,
                  o_ref, conv_out_ref, h_out_ref, xn_scr, xr_scr, h_scr, *in_staging):
    S, TB, R = seg.S, seg.TB, seg.rows
    P = (CONV_A - 1) * S
    i = pl.program_id(0)
    slot = i % 2
    blocks = [slice(b * COL_BLOCK, (b + 1) * COL_BLOCK) for b in range(N_COL_BLOCKS)]

    if batch_major_in:
        xbuf, sems = in_staging

        def copies(which, step):
            t0 = pl.multiple_of(step * TB, TB)
            return [pltpu.make_async_copy(x_ref.at[s, pl.ds(t0, TB), :], xbuf.at[which, :, s, :],
                                          sems.at[which, s]) for s in range(S)]

    @pl.when(i == 0)
    def _():
        for b, cols in enumerate(blocks):
            xr_scr[b, 0:P, :] = conv_in_ref[:, cols]
            h_scr[b] = h_in_ref[:, cols]
        if batch_major_in:
            for c in copies(0, 0):
                c.start()

    if batch_major_in:
        @pl.when(i + 1 < nblk)
        def _():
            for c in copies(1 - slot, i + 1):
                c.start()

        for c in copies(slot, i):
            c.wait()
        read_x = lambda: xbuf[slot].reshape(R, D_MODEL)
    else:
        read_x = lambda: x_ref[...]

    xn_scr[...] = _rms(read_x(), npre_ref[layer:layer + 1, :]).astype(BF16)
    xn = xn_scr[...]

    gate_pre, xc, res = {}, {}, {}

    def input_matmuls(b):
        _rglru_branch_in(seg, xn, wxr_ref[b], xr_scr.at[b])
        gate_pre[b] = _dot(xn, wgate_ref[b])

    def conv_and_gate_matmul(b):
        xc[b] = _rglru_conv(seg, cw_ref[:, blocks[b]], cb_ref[j:j + 1, blocks[b]], xr_scr.at[b], 0)
        res[b] = _dot(xc[b].astype(BF16), gw_ref[b])

    y = None
    input_matmuls(0)
    conv_and_gate_matmul(0)
    input_matmuls(1)
    for b, cols in enumerate(blocks):
        if b + 1 < N_COL_BLOCKS:
            conv_and_gate_matmul(b + 1)
        if b + 2 < N_COL_BLOCKS:
            input_matmuls(b + 2)
        g = _gelu_tanh(gate_pre.pop(b))
        a, u = _rglru_gate_math(res.pop(b), xc.pop(b), 0.5 * gab_ref[b:b + 1, :], 0.5 * gxb_ref[b:b + 1, :],
                                lam_ref[j:j + 1, cols])
        hs = _rglru_scan(seg, a, u, g, h_scr.at[b], h_scr.at[b], 0)
        part = _dot(hs.astype(BF16), wout_ref[cols, :])
        y = part if y is None else y + part

    o_ref[...] = read_x() + _rms(y, npost_ref[layer:layer + 1, :])

    @pl.when(i == nblk - 1)
    def _():
        for b, cols in enumerate(blocks):
            conv_out_ref[:, cols] = xr_scr[b, 0:P, :]
            h_out_ref[:, cols] = h_scr[b]


def _rglru_layer(x, conv_in, h_in, layer, j, w, bf16, *, S, TB):
    D = D_MODEL
    batch_major_in = x.ndim == 3
    (seg,), R, conv_rows, _ = _segments(((S, TB),), CONV_A)
    nblk = x.size // D // R
    P = (CONV_A - 1) * S
    n_layers, n_a = w["norm_mix_pre"].shape[0], w["rg_conv_b"].shape[0]
    row_spec = pl.BlockSpec((R, D), lambda i: (i, 0))
    if batch_major_in:
        x_spec = pl.BlockSpec(memory_space=pl.ANY)
        staging = [pltpu.VMEM((2, TB, S, D), F32), pltpu.SemaphoreType.DMA((2, S))]
    else:
        x_spec, staging = row_spec, []
    layer_of = lambda arr: pl.BlockSpec((None,) + arr.shape[1:], lambda i: (j,) + (0,) * (arr.ndim - 1),
                                        pipeline_mode=pl.Buffered(1))
    return pl.pallas_call(
        functools.partial(_rglru_kernel, seg, nblk, layer, j, batch_major_in),
        grid=(nblk,),
        in_specs=[x_spec, _resident((P, D)), _resident((S, D)),
                  _resident((n_layers, D)), _resident((n_layers, D)),
                  _resident(bf16["w_xr"].shape), _resident(bf16["w_gate"].shape),
                  layer_of(w["rg_conv_w"]), _resident((n_a, D)), _resident(bf16["gw"].shape),
                  layer_of(w["rg_gate_a_b"]), layer_of(w["rg_gate_x_b"]), _resident((n_a, D)),
                  _resident((D, D))],
        out_specs=[row_spec, pl.BlockSpec((P, D), lambda i: (0, 0)), pl.BlockSpec((S, D), lambda i: (0, 0))],
        out_shape=[jax.ShapeDtypeStruct((nblk * R, D), F32), jax.ShapeDtypeStruct((P, D), F32),
                   jax.ShapeDtypeStruct((S, D), F32)],
        scratch_shapes=[pltpu.VMEM((R, D), BF16),
                        pltpu.VMEM((N_COL_BLOCKS, conv_rows, COL_BLOCK), F32),
                        pltpu.VMEM((N_COL_BLOCKS, S, COL_BLOCK), F32)]
                       + staging,
        compiler_params=_params(),
        name="rglru_mixer",
    )(x, conv_in, h_in, w["norm_mix_pre"], w["norm_mix_post"], bf16["w_xr"], bf16["w_gate"], w["rg_conv_w"],
      w["rg_conv_b"], bf16["gw"], w["rg_gate_a_b"], w["rg_gate_x_b"], w["rg_lambda"], bf16["w_out"])


def _sconv_conv(seg, cv, cw, region):
    S, R = seg.S, seg.rows
    P = (CONV_B - 1) * S
    c0 = seg.conv0
    region[c0 + P:c0 + P + R, :] = cv
    conv = region[c0:c0 + R, :] * cw[0:1]
    for k in range(1, CONV_B):
        conv = conv + region[c0 + k * S:c0 + k * S + R, :] * cw[k:k + 1]
    region[c0:c0 + P, :] = region[c0 + R:c0 + R + P, :]
    return conv


def _sconv_small_kernel(segs, layer, j, x_ref, *refs):
    D, C, NB = D_MODEL, COL_BLOCK, N_COL_BLOCKS
    n = len(segs)
    conv_in = refs[0:n]
    npre_ref, npost_ref, cw_ref, win_hbm, wout_hbm = refs[n:n + 5]
    o_ref = refs[n + 5]
    conv_out = refs[n + 6:2 * n + 6]
    wbg_b_hbm, wcg_b_hbm, wv_b_hbm, wout_b_hbm = refs[2 * n + 6:2 * n + 10]
    xn_scr, cv_scr, col_f, row_f, col_h, row_h, in_sems, out_sems = refs[2 * n + 10:]

    def fetch(b):
        return ([pltpu.make_async_copy(win_hbm.at[j, :, pl.ds(k * D + b * C, C)], col_f.at[b, k], in_sems.at[b, k])
                 for k in range(3)]
                + [pltpu.make_async_copy(wout_hbm.at[j, pl.ds(b * C, C), :], row_f.at[b], in_sems.at[b, 3])])

    def emit(b):
        return ([pltpu.make_async_copy(col_h.at[b, k], dst.at[b], out_sems.at[b, k])
                 for k, dst in enumerate((wbg_b_hbm, wcg_b_hbm, wv_b_hbm))]
                + [pltpu.make_async_copy(row_h.at[b], wout_b_hbm.at[pl.ds(b * C, C), :], out_sems.at[b, 3])])

    for b in range(NB):
        _start_all(fetch(b))
    xn_scr[...] = _rms(x_ref[...], npre_ref[layer:layer + 1, :]).astype(BF16)
    xn = xn_scr[...]

    y = None
    for b in range(NB):
        cols = slice(b * C, (b + 1) * C)
        _wait_all(fetch(b))
        w_bg, w_cg, w_v = (col_f[b, k].astype(BF16) for k in range(3))
        w_out_rows = row_f[b].astype(BF16)
        col_h[b, 0], col_h[b, 1], col_h[b, 2], row_h[b] = w_bg, w_cg, w_v, w_out_rows
        _start_all(emit(b))

        cv = _dot(xn, w_cg) * _dot(xn, w_v)
        convs = []
        for seg, c_in, c_out in zip(segs, conv_in, conv_out):
            P = (CONV_B - 1) * seg.S
            cv_scr[seg.conv0:seg.conv0 + P, :] = c_in[:, cols]
            convs.append(_sconv_conv(seg, cv[seg.row0:seg.row0 + seg.rows], cw_ref[:, cols], cv_scr))
            c_out[:, cols] = cv_scr[seg.conv0:seg.conv0 + P, :]
        m = (_dot(xn, w_bg) * jnp.concatenate(convs, axis=0)).astype(BF16)
        part = _dot(m, w_out_rows)
        y = part if y is None else y + part

    for b in range(NB):
        _wait_all(emit(b))
    o_ref[...] = x_ref[...] + _rms(y, npost_ref[layer:layer + 1, :])


def _sconv_small_layer(x, conv_in, layer, j, w, *, seqs_and_steps):
    R, D = x.shape
    C, NB = COL_BLOCK, N_COL_BLOCKS
    segs, _, conv_rows, _ = _segments(seqs_and_steps, CONV_B)
    n_layers = w["norm_mix_pre"].shape[0]
    hbm = pl.BlockSpec(memory_space=pl.ANY)
    outs = pl.pallas_call(
        functools.partial(_sconv_small_kernel, segs, layer, j),
        grid=(1,),
        in_specs=[_resident((R, D))] + [_resident(c.shape) for c in conv_in]
                 + [_resident((n_layers, D)), _resident((n_layers, D)),
                    pl.BlockSpec((None, CONV_B, D), lambda i: (j, 0, 0), pipeline_mode=pl.Buffered(1)), hbm, hbm],
        out_specs=[pl.BlockSpec((R, D), lambda i: (0, 0))]
                  + [pl.BlockSpec(c.shape, lambda i: (0, 0)) for c in conv_in] + [hbm] * 4,
        out_shape=[jax.ShapeDtypeStruct((R, D), F32)] + [jax.ShapeDtypeStruct(c.shape, F32) for c in conv_in]
                  + [jax.ShapeDtypeStruct((NB, D, C), BF16)] * 3 + [jax.ShapeDtypeStruct((D, D), BF16)],
        scratch_shapes=[pltpu.VMEM((R, D), BF16),
                        pltpu.VMEM((conv_rows, C), F32),
                        pltpu.VMEM((NB, 3, D, C), F32),
                        pltpu.VMEM((NB, C, D), F32),
                        pltpu.VMEM((NB, 3, D, C), BF16), pltpu.VMEM((NB, C, D), BF16),
                        pltpu.SemaphoreType.DMA((NB, 4)), pltpu.SemaphoreType.DMA((NB, 4))],
        compiler_params=_params(),
        name="sconv_mixer_small",
    )(x, *conv_in, w["norm_mix_pre"], w["norm_mix_post"], w["sc_conv_w"], w["sc_w_in"], w["sc_w_out"])
    n = len(segs)
    bf16 = dict(w_bg=outs[n + 1], w_cg=outs[n + 2], w_v=outs[n + 3], w_out=outs[n + 4])
    return outs[0], list(outs[1:1 + n]), bf16


def _sconv_kernel(seg, nblk, layer, x_ref, conv_in_ref, npre_ref, npost_ref, wbg_ref, wcg_ref, wv_ref,
                  cw_ref, wout_ref, o_ref, conv_out_ref, xn_scr, cv_scr, m_scr):
    P = (CONV_B - 1) * seg.S
    i = pl.program_id(0)
    blocks = [slice(b * COL_BLOCK, (b + 1) * COL_BLOCK) for b in range(N_COL_BLOCKS)]

    @pl.when(i == 0)
    def _():
        for b, cols in enumerate(blocks):
            cv_scr[b, 0:P, :] = conv_in_ref[:, cols]

    xn_scr[...] = _rms(x_ref[...], npre_ref[layer:layer + 1, :]).astype(BF16)
    xn = xn_scr[...]
    for b, cols in enumerate(blocks):
        cv = _dot(xn, wcg_ref[b]) * _dot(xn, wv_ref[b])
        conv = _sconv_conv(seg, cv, cw_ref[:, cols], cv_scr.at[b])
        m_scr[:, cols] = (_dot(xn, wbg_ref[b]) * conv).astype(BF16)
    y = _dot(m_scr[...], wout_ref[...])
    o_ref[...] = x_ref[...] + _rms(y, npost_ref[layer:layer + 1, :])

    @pl.when(i == nblk - 1)
    def _():
        for b, cols in enumerate(blocks):
            conv_out_ref[:, cols] = cv_scr[b, 0:P, :]


def _sconv_layer(x, conv_in, layer, j, w, bf16, *, S, TB):
    N, D = x.shape
    (seg,), R, conv_rows, _ = _segments(((S, TB),), CONV_B)
    nblk = N // R
    P = (CONV_B - 1) * S
    n_layers = w["norm_mix_pre"].shape[0]
    row_spec = pl.BlockSpec((R, D), lambda i: (i, 0))
    return pl.pallas_call(
        functools.partial(_sconv_kernel, seg, nblk, layer),
        grid=(nblk,),
        in_specs=[row_spec, _resident((P, D)), _resident((n_layers, D)), _resident((n_layers, D)),
                  _resident(bf16["w_bg"].shape), _resident(bf16["w_cg"].shape), _resident(bf16["w_v"].shape),
                  pl.BlockSpec((None, CONV_B, D), lambda i: (j, 0, 0), pipeline_mode=pl.Buffered(1)),
                  _resident((D, D))],
        out_specs=[row_spec, pl.BlockSpec((P, D), lambda i: (0, 0))],
        out_shape=[jax.ShapeDtypeStruct((N, D), F32), jax.ShapeDtypeStruct((P, D), F32)],
        scratch_shapes=[pltpu.VMEM((R, D), BF16),
                        pltpu.VMEM((N_COL_BLOCKS, conv_rows, COL_BLOCK), F32),
                        pltpu.VMEM((R, D), BF16)],
        compiler_params=_params(),
        name="sconv_mixer",
    )(x, conv_in, w["norm_mix_pre"], w["norm_mix_post"], bf16["w_bg"], bf16["w_cg"], bf16["w_v"],
      w["sc_conv_w"], bf16["w_out"])


def _swiglu(g, u):
    return (g * jax.nn.sigmoid(g) * u).astype(BF16)


def _ffn_stream_kernel(layer, nchunk, sample_out, x_ref, npre_ref, npost_ref, wg_hbm, wu_hbm, wd_hbm,
                       o_ref, wg_b_hbm, wu_b_hbm, wd_b_hbm,
                       xn_scr, wg_f, wu_f, wd_f, wg_h, wu_h, wd_h, in_sems, out_sems, *out_staging):
    F = FF_CHUNK

    NB = STREAM_BUFFERS

    def fetch(c):
        k, cols = c % NB, pl.ds(c * F, F)
        return [pltpu.make_async_copy(wg_hbm.at[layer, :, cols], wg_f.at[k], in_sems.at[0, k]),
                pltpu.make_async_copy(wu_hbm.at[layer, :, cols], wu_f.at[k], in_sems.at[1, k]),
                pltpu.make_async_copy(wd_hbm.at[layer, cols, :], wd_f.at[k], in_sems.at[2, k])]

    def emit(c):
        k = c % NB
        return [pltpu.make_async_copy(wg_h.at[k], wg_b_hbm.at[c], out_sems.at[0, k]),
                pltpu.make_async_copy(wu_h.at[k], wu_b_hbm.at[c], out_sems.at[1, k]),
                pltpu.make_async_copy(wd_h.at[k], wd_b_hbm.at[pl.ds(c * F, F), :], out_sems.at[2, k])]

    for c in range(min(NB, nchunk)):
        for cp in fetch(c):
            cp.start()
    xn_scr[...] = _rms(x_ref[...], npre_ref[layer:layer + 1, :]).astype(BF16)
    xn = xn_scr[...]

    y = None
    for c in range(nchunk):
        k = c % NB
        for cp in fetch(c):
            cp.wait()
        if c >= NB:
            for cp in emit(c - NB):
                cp.wait()
        wg, wu, wd = wg_f[k].astype(BF16), wu_f[k].astype(BF16), wd_f[k].astype(BF16)
        wg_h[k], wu_h[k], wd_h[k] = wg, wu, wd
        for cp in emit(c):
            cp.start()
        if c + NB < nchunk:
            for cp in fetch(c + NB):
                cp.start()
        part = _dot(_swiglu(_dot(xn, wg), _dot(xn, wu)), wd)
        y = part if y is None else y + part
    for c in range(max(nchunk - NB, 0), nchunk):
        for cp in emit(c):
            cp.wait()

    out = x_ref[...] + _rms(y, npost_ref[layer:layer + 1, :])
    if sample_out is None:
        o_ref[...] = out
    else:
        row0, S, T = sample_out
        obuf, sems = out_staging
        obuf[...] = out[row0:row0 + T * S].reshape(T, S, out.shape[-1])
        copies = [pltpu.make_async_copy(obuf.at[t], o_ref.at[:, t, :], sems.at[t]) for t in range(T)]
        for cp in copies:
            cp.start()
        for cp in copies:
            cp.wait()


def _ffn_stream_layer(x, layer, w, *, sample_out=None):
    R, D = x.shape
    F, NB = FF_CHUNK, STREAM_BUFFERS
    n_layers = w["norm_ffn_pre"].shape[0]
    nchunk = D_FF // F
    hbm = pl.BlockSpec(memory_space=pl.ANY)
    if sample_out is None:
        o_spec, o_shape, staging = pl.BlockSpec((R, D), lambda i: (0, 0)), jax.ShapeDtypeStruct((R, D), F32), []
    else:
        _, S, T = sample_out
        o_spec, o_shape = hbm, jax.ShapeDtypeStruct((S, T, D), F32)
        staging = [pltpu.VMEM((T, S, D), F32), pltpu.SemaphoreType.DMA((T,))]
    outs = pl.pallas_call(
        functools.partial(_ffn_stream_kernel, layer, nchunk, sample_out),
        grid=(1,),
        in_specs=[_resident((R, D)), _resident((n_layers, D)), _resident((n_layers, D)), hbm, hbm, hbm],
        out_specs=[o_spec, hbm, hbm, hbm],
        out_shape=[o_shape, jax.ShapeDtypeStruct((nchunk, D, F), BF16),
                   jax.ShapeDtypeStruct((nchunk, D, F), BF16), jax.ShapeDtypeStruct((D_FF, D), BF16)],
        scratch_shapes=[pltpu.VMEM((R, D), BF16),
                        pltpu.VMEM((NB, D, F), F32), pltpu.VMEM((NB, D, F), F32), pltpu.VMEM((NB, F, D), F32),
                        pltpu.VMEM((NB, D, F), BF16), pltpu.VMEM((NB, D, F), BF16), pltpu.VMEM((NB, F, D), BF16),
                        pltpu.SemaphoreType.DMA((3, NB)), pltpu.SemaphoreType.DMA((3, NB))] + staging,
        compiler_params=_params(),
        name="swiglu_ffn_small",
    )(x, w["norm_ffn_pre"], w["norm_ffn_post"], w["ffn_w_gate"], w["ffn_w_up"], w["ffn_w_down"])
    return outs[0], dict(wg=outs[1], wu=outs[2], wd=outs[3])


def _ffn_kernel(layer, nblk, batch_major_out, x_ref, npre_ref, npost_ref, wg_ref, wu_ref, wd_ref, o_ref,
                xn_scr, act_scr, *out_staging):
    R = x_ref.shape[0]
    i = pl.program_id(0)
    slot = i % 2
    if batch_major_out:
        obuf, sems = out_staging
        S, TB = o_ref.shape[0], R // o_ref.shape[0]

        def copies(which, step):
            t0 = pl.multiple_of(step * TB, TB)
            return [pltpu.make_async_copy(obuf.at[which, :, s, :],
                                          o_ref.at[s, pl.ds(t0, TB), :], sems.at[which, s]) for s in range(S)]

        @pl.when(i >= 2)
        def _():
            for c in copies(slot, i - 2):
                c.wait()

    halves = [slice(0, R // 2), slice(R // 2, R)]
    for rows in halves:
        xn_scr[rows, :] = _rms(x_ref[rows, :], npre_ref[layer:layer + 1, :]).astype(BF16)
    for rows in halves:
        xn = xn_scr[rows, :]
        for c in range(D_FF // FF_CHUNK):
            cols = slice(c * FF_CHUNK, (c + 1) * FF_CHUNK)
            act_scr[rows, cols] = _swiglu(_dot(xn, wg_ref[c]), _dot(xn, wu_ref[c]))
    ys = [_dot(act_scr[rows, :], wd_ref[...]) for rows in halves]
    for rows, y in zip(halves, ys):
        out = x_ref[rows, :] + _rms(y, npost_ref[layer:layer + 1, :])
        if batch_major_out:
            obuf[slot, rows.start // S:rows.stop // S] = out.reshape(-1, S, out.shape[-1])
        else:
            o_ref[rows, :] = out

    if batch_major_out:
        for c in copies(slot, i):
            c.start()

        @pl.when(i == nblk - 1)
        def _():
            if nblk >= 2:
                for c in copies(1 - slot, i - 1):
                    c.wait()
            for c in copies(slot, i):
                c.wait()


def _ffn_layer(x, layer, w, bf16, *, R, batch_major_seqs=None):
    N, D = x.shape
    n_layers = w["norm_ffn_pre"].shape[0]
    nblk = N // R
    row_spec = pl.BlockSpec((R, D), lambda i: (i, 0))
    if batch_major_seqs is None:
        out_spec, out_shape, staging = row_spec, jax.ShapeDtypeStruct((N, D), F32), []
    else:
        S = batch_major_seqs
        out_spec = pl.BlockSpec(memory_space=pl.ANY)
        out_shape = jax.ShapeDtypeStruct((S, N // S, D), F32)
        staging = [pltpu.VMEM((2, R // S, S, D), F32), pltpu.SemaphoreType.DMA((2, S))]
    return pl.pallas_call(
        functools.partial(_ffn_kernel, layer, nblk, batch_major_seqs is not None),
        grid=(nblk,),
        in_specs=[row_spec, _resident((n_layers, D)), _resident((n_layers, D)),
                  _resident(bf16["wg"].shape), _resident(bf16["wu"].shape), _resident((D_FF, D))],
        out_specs=out_spec,
        out_shape=out_shape,
        scratch_shapes=[pltpu.VMEM((R, D), BF16), pltpu.VMEM((R, D_FF), BF16)] + staging,
        compiler_params=_params(),
        name="swiglu_ffn",
    )(x, w["norm_ffn_pre"], w["norm_ffn_post"], bf16["wg"], bf16["wu"], bf16["wd"])


def _to_time_major(a):
    S, K, D = a.shape
    return jnp.swapaxes(a, 0, 1).reshape(K * S, D)


def _from_time_major(a, S):
    KS, D = a.shape
    return jnp.swapaxes(a.reshape(KS // S, S, D), 0, 1)


def kernel(x_prompt, x_sample, state_rglru_conv, state_rglru_h, state_sconv, meta_tokens, norm_mix_pre, norm_mix_post, norm_ffn_pre, norm_ffn_post, rg_w_in, rg_conv_w, rg_conv_b, rg_gate_a_w, rg_gate_a_b, rg_gate_x_w, rg_gate_x_b, rg_lambda, rg_w_out, sc_w_in, sc_conv_w, sc_w_out, ffn_w_gate, ffn_w_up, ffn_w_down):
    D = D_MODEL
    depth = norm_mix_pre.shape[0]
    batch, seq, _ = x_prompt.shape
    dec_batch, dec_seq, _ = x_sample.shape
    w = dict(norm_mix_pre=norm_mix_pre, norm_mix_post=norm_mix_post, norm_ffn_pre=norm_ffn_pre,
             norm_ffn_post=norm_ffn_post, rg_w_in=rg_w_in, rg_conv_w=rg_conv_w, rg_conv_b=rg_conv_b,
             rg_gate_a_w=rg_gate_a_w, rg_gate_a_b=rg_gate_a_b, rg_gate_x_w=rg_gate_x_w, rg_gate_x_b=rg_gate_x_b,
             rg_lambda=rg_lambda, rg_w_out=rg_w_out, sc_w_in=sc_w_in, sc_conv_w=sc_conv_w, sc_w_out=sc_w_out,
             ffn_w_gate=ffn_w_gate, ffn_w_up=ffn_w_up, ffn_w_down=ffn_w_down)

    x = jnp.broadcast_to(meta_tokens[:, None, :], (N_META, batch, D)).reshape(N_META * batch, D)
    small = ((batch, N_META), (dec_batch, dec_seq))
    mixer_bf16, ffn_bf16 = [], []
    rg_conv_s, rg_h_s, sc_s = [], [], []
    for i in range(depth):
        j = i // 2
        if i % 2 == 0:
            conv0 = [jnp.zeros(((CONV_A - 1) * batch, D), F32), _to_time_major(state_rglru_conv[j])]
            h0 = [jnp.zeros((batch, D), F32), state_rglru_h[j]]
            x, cb, hT, wb = _rglru_small_layer(x, conv0, h0, i, j, w, seqs_and_steps=small,
                                                x_sample=x_sample if i == 0 else None)
            rg_conv_s.append(cb)
            rg_h_s.append(hT)
        else:
            conv0 = [jnp.zeros(((CONV_B - 1) * batch, D), F32), _to_time_major(state_sconv[j])]
            x, cb, wb = _sconv_small_layer(x, conv0, i, j, w, seqs_and_steps=small)
            sc_s.append(cb)
        mixer_bf16.append(wb)
        sample_out = (N_META * batch, dec_batch, dec_seq) if i == depth - 1 else None
        x, wb = _ffn_stream_layer(x, i, w, sample_out=sample_out)
        ffn_bf16.append(wb)
    y_sample = x

    x = x_prompt
    rg_conv_p, rg_h_p, sc_p = [], [], []
    for i in range(depth):
        j = i // 2
        if i % 2 == 0:
            x, cb, hT = _rglru_layer(x, rg_conv_s[j][0], rg_h_s[j][0], i, j, w, mixer_bf16[i], S=batch, TB=PROMPT_TB)
            rg_conv_p.append(cb)
            rg_h_p.append(hT)
        else:
            x, cb = _sconv_layer(x, sc_s[j][0], i, j, w, mixer_bf16[i], S=batch, TB=PROMPT_TB)
            sc_p.append(cb)
        x = _ffn_layer(x, i, w, ffn_bf16[i], R=batch * PROMPT_TB,
                       batch_major_seqs=batch if i == depth - 1 else None)
    y_prompt = x

    return (y_prompt, y_sample,
            jnp.stack([_from_time_major(c, batch) for c in rg_conv_p]), jnp.stack(rg_h_p),
            jnp.stack([_from_time_major(c, batch) for c in sc_p]),
            jnp.stack([_from_time_major(c[1], dec_batch) for c in rg_conv_s]), jnp.stack([h[1] for h in rg_h_s]),
            jnp.stack([_from_time_major(c[1], dec_batch) for c in sc_s]))
```

```python
import functools
from typing import NamedTuple

import jax
import jax.numpy as jnp
from jax import lax
from jax.experimental import pallas as pl
from jax.experimental.pallas import tpu as pltpu

D_MODEL = 1024
D_FF = 2816
N_META = 16
COL_BLOCK = 256
N_COL_BLOCKS = D_MODEL // COL_BLOCK
CONV_A = 4
CONV_B = 3
RG_C = 8.0
EPS = 1e-6

SUBLANES = 8
FF_CHUNK = 256
STREAM_BUFFERS = 3
OUT_SLOTS = 2
PROMPT_TB = 128
VMEM_LIMIT_BYTES = 56 * 1024 * 1024
SMALL_STAGE_VMEM_LIMIT_BYTES = 60 * 1024 * 1024

F32 = jnp.float32
BF16 = jnp.bfloat16


class _Seg(NamedTuple):
    S: int
    TB: int
    row0: int
    conv0: int
    h0: int

    @property
    def rows(self):
        return self.S * self.TB


def _segments(seqs_and_steps, taps):
    segs, row0, conv0, h0 = [], 0, 0, 0
    for S, TB in seqs_and_steps:
        segs.append(_Seg(S, TB, row0, conv0, h0))
        row0 += S * TB
        conv0 += (taps - 1) * S + S * TB
        h0 += S
    return tuple(segs), row0, conv0, h0


def _rms(x, w):
    ms = jnp.mean(x * x, axis=-1, keepdims=True)
    return x * lax.rsqrt(ms + EPS) * w


def _dot(a, b):
    return jnp.dot(a, b, preferred_element_type=F32)


def _gelu_tanh(x):
    c = 0.7978845608028654
    hx = 0.5 * x
    return hx + hx * jnp.tanh(x * (c + (c * 0.044715) * (x * x)))


def _resident(shape):
    zeros = (0,) * len(shape)
    return pl.BlockSpec(shape, lambda i: zeros, pipeline_mode=pl.Buffered(1))


def _params(vmem_limit_bytes=VMEM_LIMIT_BYTES):
    return pltpu.CompilerParams(dimension_semantics=("arbitrary",), vmem_limit_bytes=vmem_limit_bytes)


def _start_all(copies):
    for cp in copies:
        cp.start()


def _wait_all(copies):
    for cp in copies:
        cp.wait()


def _rglru_branch_in(seg, xn, w_xr, region):
    P = (CONV_A - 1) * seg.S
    region[seg.conv0 + P:seg.conv0 + P + seg.rows, :] = _dot(xn, w_xr)


def _rglru_conv(seg, cw, cb, region, carry_to):
    S, R = seg.S, seg.rows
    P = (CONV_A - 1) * S
    c0 = seg.conv0
    xc = region[c0:c0 + R, :] * cw[0:1]
    for k in range(1, CONV_A):
        xc = xc + region[c0 + k * S:c0 + k * S + R, :] * cw[k:k + 1]
    xc = xc + cb
    if carry_to is not None:
        region[carry_to:carry_to + P, :] = region[c0 + R:c0 + R + P, :]
    return xc


def _rglru_gate_math(res, xc, half_gab, half_gxb, lam):
    half_c_sp = (-0.5 * RG_C) * jax.nn.softplus(-lam)
    tr = jnp.tanh(res[:, 0:COL_BLOCK] + half_gab)
    log_a = half_c_sp * tr + half_c_sp
    ig = 0.5 * jnp.tanh(res[:, COL_BLOCK:2 * COL_BLOCK] + half_gxb) + 0.5
    a = jnp.exp(log_a)
    m2 = jnp.tanh(log_a) * (-1.0 - a * a)
    u = jnp.where(m2 > 0.0, m2 * lax.rsqrt(m2), 0.0) * (ig * xc)
    return a, u


def _rglru_scan(seg, a, u, g, h_read, h_write, h_row0):
    S, TB = seg.S, seg.TB
    groups = S // SUBLANES
    pieces = [None] * (TB * groups)
    for c in range(groups):
        hrows = slice(h_row0 + c * SUBLANES, h_row0 + (c + 1) * SUBLANES)
        h = h_read[hrows, :]
        for t in range(TB):
            r = t * S + c * SUBLANES
            h = a[r:r + SUBLANES] * h + u[r:r + SUBLANES]
            pieces[t * groups + c] = h * g[r:r + SUBLANES]
        h_write[hrows, :] = h
    return jnp.concatenate(pieces, axis=0)


class _Part(NamedTuple):
    kernel: object
    args: list
    in_specs: list
    out_specs: list
    out_shape: list
    scratch_shapes: list


def _run_small_stage(mixer, ffn, rows, name):
    n_min, n_fin = len(mixer.args), len(ffn.args) - 1
    n_mout, n_fout = len(mixer.out_shape) - 1, len(ffn.out_shape)
    n_mscr, n_fscr = len(mixer.scratch_shapes), len(ffn.scratch_shapes)

    def kernel(*refs):
        refs = list(refs)
        take = lambda k: [refs.pop(0) for _ in range(k)]
        m_in, f_in, m_out, f_out, m_scr, f_scr = (take(k) for k in (n_min, n_fin, n_mout, n_fout, n_mscr, n_fscr))
        (x_mid,) = refs
        f_refs = [x_mid] + f_in + f_out + f_scr
        mixer.kernel(lambda: ffn.kernel(*f_refs, prefetch_only=True), *m_in, x_mid, *m_out, *m_scr)
        ffn.kernel(*f_refs, prefetched=True)

    outs = pl.pallas_call(
        kernel,
        grid=(1,),
        in_specs=mixer.in_specs + ffn.in_specs[1:],
        out_specs=mixer.out_specs[1:] + ffn.out_specs,
        out_shape=mixer.out_shape[1:] + ffn.out_shape,
        scratch_shapes=mixer.scratch_shapes + ffn.scratch_shapes + [pltpu.VMEM((rows, D_MODEL), F32)],
        compiler_params=_params(SMALL_STAGE_VMEM_LIMIT_BYTES),
        name=name,
    )(*mixer.args, *ffn.args[1:])
    return list(outs[:n_mout]), list(outs[n_mout:])


def _rglru_small_kernel(segs, sample_in, layer, j, after_fetch_issue, x_ref, *refs):
    D, C, NB = D_MODEL, COL_BLOCK, N_COL_BLOCKS
    if sample_in is not None:
        xs_hbm, refs = refs[0], refs[1:]
    n = len(segs)
    conv_in, h_in = refs[0:n], refs[n:2 * n]
    (npre_ref, npost_ref, cw_ref, cb_ref, gab_ref, gxb_ref, lam_ref,
     win_hbm, gaw_hbm, gxw_hbm, wout_hbm) = refs[2 * n:2 * n + 11]
    o_ref = refs[2 * n + 11]
    conv_out, h_out = refs[2 * n + 12:3 * n + 12], refs[3 * n + 12:4 * n + 12]
    wxr_b_hbm, wgate_b_hbm, gw_b_hbm, wout_b_hbm = refs[4 * n + 12:4 * n + 16]
    (xn_scr, xr_scr, col_f, sq_f, row_f, col_h, gw_h, row_h, in_sems, out_sems,
     *in_staging) = refs[4 * n + 16:]

    def fetch(b):
        cols = pl.ds(b * C, C)
        return [pltpu.make_async_copy(win_hbm.at[j, :, pl.ds(D + b * C, C)], col_f.at[b, 0], in_sems.at[b, 0]),
                pltpu.make_async_copy(win_hbm.at[j, :, cols], col_f.at[b, 1], in_sems.at[b, 1]),
                pltpu.make_async_copy(gaw_hbm.at[j, b], sq_f.at[b, 0], in_sems.at[b, 2]),
                pltpu.make_async_copy(gxw_hbm.at[j, b], sq_f.at[b, 1], in_sems.at[b, 3]),
                pltpu.make_async_copy(wout_hbm.at[j, cols, :], row_f.at[b], in_sems.at[b, 4])]

    def emit(b):
        k = b % OUT_SLOTS
        return [pltpu.make_async_copy(col_h.at[k, 0], wxr_b_hbm.at[b], out_sems.at[k, 0]),
                pltpu.make_async_copy(col_h.at[k, 1], wgate_b_hbm.at[b], out_sems.at[k, 1]),
                pltpu.make_async_copy(gw_h.at[k], gw_b_hbm.at[b], out_sems.at[k, 2]),
                pltpu.make_async_copy(row_h.at[k], wout_b_hbm.at[pl.ds(b * C, C), :], out_sems.at[k, 3])]

    if sample_in is None:
        read_x = lambda: x_ref[...]
    else:
        S, T = sample_in
        xs_scr, sems = in_staging
        x_copies = [pltpu.make_async_copy(xs_hbm.at[:, t, :], xs_scr.at[t], sems.at[t]) for t in range(T)]
        _start_all(x_copies)
        read_x = lambda: jnp.concatenate([x_ref[...], xs_scr[...].reshape(T * S, D)], axis=0)
    for b in range(NB):
        _start_all(fetch(b))
    after_fetch_issue()
    if sample_in is not None:
        _wait_all(x_copies)
    xn_scr[...] = _rms(read_x(), npre_ref[layer:layer + 1, :]).astype(BF16)
    xn = xn_scr[...]

    rows_of = lambda seg: slice(seg.row0, seg.row0 + seg.rows)
    y = None
    for b in range(NB):
        cols = slice(b * C, (b + 1) * C)
        _wait_all(fetch(b))
        w_xr, w_gate = col_f[b, 0].astype(BF16), col_f[b, 1].astype(BF16)
        gw = (0.5 * jnp.concatenate([sq_f[b, 0], sq_f[b, 1]], axis=1)).astype(BF16)
        w_out_rows = row_f[b].astype(BF16)
        k = b % OUT_SLOTS
        if b >= OUT_SLOTS:
            _wait_all(emit(b - OUT_SLOTS))
        col_h[k, 0], col_h[k, 1], gw_h[k], row_h[k] = w_xr, w_gate, gw, w_out_rows
        _start_all(emit(b))

        xr = _dot(xn, w_xr)
        g = _gelu_tanh(_dot(xn, w_gate))
        xcs = []
        for seg, c_in, c_out in zip(segs, conv_in, conv_out):
            P = (CONV_A - 1) * seg.S
            xr_scr[seg.conv0:seg.conv0 + P, :] = c_in[:, cols]
            xr_scr[seg.conv0 + P:seg.conv0 + P + seg.rows, :] = xr[rows_of(seg)]
            xcs.append(_rglru_conv(seg, cw_ref[:, cols], cb_ref[j:j + 1, cols], xr_scr, seg.conv0))
            c_out[:, cols] = xr_scr[seg.conv0:seg.conv0 + P, :]
        xc = jnp.concatenate(xcs, axis=0)
        a, u = _rglru_gate_math(_dot(xc.astype(BF16), gw), xc, 0.5 * gab_ref[b:b + 1, :],
                                0.5 * gxb_ref[b:b + 1, :], lam_ref[j:j + 1, cols])
        hs = [_rglru_scan(seg, a[rows_of(seg)], u[rows_of(seg)], g[rows_of(seg)],
                          hi.at[:, cols], ho.at[:, cols], 0)
              for seg, hi, ho in zip(segs, h_in, h_out)]
        part = _dot(jnp.concatenate(hs, axis=0).astype(BF16), w_out_rows)
        y = part if y is None else y + part

    for b in range(max(NB - OUT_SLOTS, 0), NB):
        _wait_all(emit(b))
    o_ref[...] = read_x() + _rms(y, npost_ref[layer:layer + 1, :])


def _rglru_small_part(x, conv_in, h_in, layer, j, w, *, seqs_and_steps, x_sample=None):
    D, C, NB = D_MODEL, COL_BLOCK, N_COL_BLOCKS
    segs, R, conv_rows, _ = _segments(seqs_and_steps, CONV_A)
    n_layers, n_a = w["norm_mix_pre"].shape[0], w["rg_conv_b"].shape[0]
    hbm = pl.BlockSpec(memory_space=pl.ANY)
    layer_of = lambda arr: pl.BlockSpec((None,) + arr.shape[1:], lambda i: (j,) + (0,) * (arr.ndim - 1),
                                        pipeline_mode=pl.Buffered(1))
    if x_sample is None:
        x_args, x_specs, sample_in, staging = [x], [_resident(x.shape)], None, []
    else:
        S, T, _ = x_sample.shape
        x_args, x_specs, sample_in = [x, x_sample], [_resident(x.shape), hbm], (S, T)
        staging = [pltpu.VMEM((T, S, D), F32), pltpu.SemaphoreType.DMA((T,))]
    states = conv_in + h_in
    return _Part(
        kernel=functools.partial(_rglru_small_kernel, segs, sample_in, layer, j),
        args=x_args + states + [w["norm_mix_pre"], w["norm_mix_post"], w["rg_conv_w"], w["rg_conv_b"],
                                w["rg_gate_a_b"], w["rg_gate_x_b"], w["rg_lambda"], w["rg_w_in"],
                                w["rg_gate_a_w"], w["rg_gate_x_w"], w["rg_w_out"]],
        in_specs=x_specs + [_resident(s.shape) for s in states]
                 + [_resident((n_layers, D)), _resident((n_layers, D)), layer_of(w["rg_conv_w"]),
                    _resident((n_a, D)), layer_of(w["rg_gate_a_b"]), layer_of(w["rg_gate_x_b"]),
                    _resident((n_a, D)), hbm, hbm, hbm, hbm],
        out_specs=[pl.BlockSpec((R, D), lambda i: (0, 0))]
                  + [pl.BlockSpec(s.shape, lambda i: (0, 0)) for s in states] + [hbm] * 4,
        out_shape=[jax.ShapeDtypeStruct((R, D), F32)] + [jax.ShapeDtypeStruct(s.shape, F32) for s in states]
                  + [jax.ShapeDtypeStruct((NB, D, C), BF16), jax.ShapeDtypeStruct((NB, D, C), BF16),
                     jax.ShapeDtypeStruct((NB, C, 2 * C), BF16), jax.ShapeDtypeStruct((D, D), BF16)],
        scratch_shapes=[pltpu.VMEM((R, D), BF16),
                        pltpu.VMEM((conv_rows, C), F32),
                        pltpu.VMEM((NB, 2, D, C), F32),
                        pltpu.VMEM((NB, 2, C, C), F32),
                        pltpu.VMEM((NB, C, D), F32),
                        pltpu.VMEM((OUT_SLOTS, 2, D, C), BF16), pltpu.VMEM((OUT_SLOTS, C, 2 * C), BF16),
                        pltpu.VMEM((OUT_SLOTS, C, D), BF16),
                        pltpu.SemaphoreType.DMA((NB, 5)), pltpu.SemaphoreType.DMA((OUT_SLOTS, 4))] + staging)


def _rglru_kernel(seg, nblk, layer, j, batch_major_in,
                  x_ref, conv_in_ref, h_in_ref, npre_ref, npost_ref, wxr_ref, wgate_ref, cw_ref, cb_ref,
                  gw_ref, gab_ref, gxb_ref, lam_ref, wout_ref,
                  o_ref, conv_out_ref, h_out_ref, xn_scr, xr_scr, h_scr, *in_staging):
    S, TB, R = seg.S, seg.TB, seg.rows
    P = (CONV_A - 1) * S
    i = pl.program_id(0)
    slot = i % 2
    blocks = [slice(b * COL_BLOCK, (b + 1) * COL_BLOCK) for b in range(N_COL_BLOCKS)]

    if batch_major_in:
        xbuf, sems = in_staging

        def copies(which, step):
            t0 = pl.multiple_of(step * TB, TB)
            return [pltpu.make_async_copy(x_ref.at[s, pl.ds(t0, TB), :], xbuf.at[which, :, s, :],
                                          sems.at[which, s]) for s in range(S)]

    @pl.when(i == 0)
    def _():
        for b, cols in enumerate(blocks):
            xr_scr[b, 0:P, :] = conv_in_ref[:, cols]
            h_scr[b] = h_in_ref[:, cols]
        if batch_major_in:
            for c in copies(0, 0):
                c.start()

    if batch_major_in:
        @pl.when(i + 1 < nblk)
        def _():
            for c in copies(1 - slot, i + 1):
                c.start()

        for c in copies(slot, i):
            c.wait()
        read_x = lambda: xbuf[slot].reshape(R, D_MODEL)
    else:
        read_x = lambda: x_ref[...]

    xn_scr[...] = _rms(read_x(), npre_ref[layer:layer + 1, :]).astype(BF16)
    xn = xn_scr[...]

    gate_pre, xc, res = {}, {}, {}

    def input_matmuls(b):
        _rglru_branch_in(seg, xn, wxr_ref[b], xr_scr.at[b])
        gate_pre[b] = _dot(xn, wgate_ref[b])

    def conv_and_gate_matmul(b):
        xc[b] = _rglru_conv(seg, cw_ref[:, blocks[b]], cb_ref[j:j + 1, blocks[b]], xr_scr.at[b], 0)
        res[b] = _dot(xc[b].astype(BF16), gw_ref[b])

    y = None
    input_matmuls(0)
    conv_and_gate_matmul(0)
    input_matmuls(1)
    for b, cols in enumerate(blocks):
        if b + 1 < N_COL_BLOCKS:
            conv_and_gate_matmul(b + 1)
        if b + 2 < N_COL_BLOCKS:
            input_matmuls(b + 2)
        g = _gelu_tanh(gate_pre.pop(b))
        a, u = _rglru_gate_math(res.pop(b), xc.pop(b), 0.5 * gab_ref[b:b + 1, :], 0.5 * gxb_ref[b:b + 1, :],
                                lam_ref[j:j + 1, cols])
        hs = _rglru_scan(seg, a, u, g, h_scr.at[b], h_scr.at[b], 0)
        part = _dot(hs.astype(BF16), wout_ref[cols, :])
        y = part if y is None else y + part

    o_ref[...] = read_x() + _rms(y, npost_ref[layer:layer + 1, :])

    @pl.when(i == nblk - 1)
    def _():
        for b, cols in enumerate(blocks):
            conv_out_ref[:, cols] = xr_scr[b, 0:P, :]
            h_out_ref[:, cols] = h_scr[b]


def _rglru_layer(x, conv_in, h_in, layer, j, w, bf16, *, S, TB):
    D = D_MODEL
    batch_major_in = x.ndim == 3
    (seg,), R, conv_rows, _ = _segments(((S, TB),), CONV_A)
    nblk = x.size // D // R
    P = (CONV_A - 1) * S
    n_layers, n_a = w["norm_mix_pre"].shape[0], w["rg_conv_b"].shape[0]
    row_spec = pl.BlockSpec((R, D), lambda i: (i, 0))
    if batch_major_in:
        x_spec = pl.BlockSpec(memory_space=pl.ANY)
        staging = [pltpu.VMEM((2, TB, S, D), F32), pltpu.SemaphoreType.DMA((2, S))]
    else:
        x_spec, staging = row_spec, []
    layer_of = lambda arr: pl.BlockSpec((None,) + arr.shape[1:], lambda i: (j,) + (0,) * (arr.ndim - 1),
                                        pipeline_mode=pl.Buffered(1))
    return pl.pallas_call(
        functools.partial(_rglru_kernel, seg, nblk, layer, j, batch_major_in),
        grid=(nblk,),
        in_specs=[x_spec, _resident((P, D)), _resident((S, D)),
                  _resident((n_layers, D)), _resident((n_layers, D)),
                  _resident(bf16["w_xr"].shape), _resident(bf16["w_gate"].shape),
                  layer_of(w["rg_conv_w"]), _resident((n_a, D)), _resident(bf16["gw"].shape),
                  layer_of(w["rg_gate_a_b"]), layer_of(w["rg_gate_x_b"]), _resident((n_a, D)),
                  _resident((D, D))],
        out_specs=[row_spec, pl.BlockSpec((P, D), lambda i: (0, 0)), pl.BlockSpec((S, D), lambda i: (0, 0))],
        out_shape=[jax.ShapeDtypeStruct((nblk * R, D), F32), jax.ShapeDtypeStruct((P, D), F32),
                   jax.ShapeDtypeStruct((S, D), F32)],
        scratch_shapes=[pltpu.VMEM((R, D), BF16),
                        pltpu.VMEM((N_COL_BLOCKS, conv_rows, COL_BLOCK), F32),
                        pltpu.VMEM((N_COL_BLOCKS, S, COL_BLOCK), F32)]
                       + staging,
        compiler_params=_params(),
        name="rglru_mixer",
    )(x, conv_in, h_in, w["norm_mix_pre"], w["norm_mix_post"], bf16["w_xr"], bf16["w_gate"], w["rg_conv_w"],
      w["rg_conv_b"], bf16["gw"], w["rg_gate_a_b"], w["rg_gate_x_b"], w["rg_lambda"], bf16["w_out"])


def _sconv_conv(seg, cv, cw, region):
    S, R = seg.S, seg.rows
    P = (CONV_B - 1) * S
    c0 = seg.conv0
    region[c0 + P:c0 + P + R, :] = cv
    conv = region[c0:c0 + R, :] * cw[0:1]
    for k in range(1, CONV_B):
        conv = conv + region[c0 + k * S:c0 + k * S + R, :] * cw[k:k + 1]
    region[c0:c0 + P, :] = region[c0 + R:c0 + R + P, :]
    return conv


def _sconv_small_kernel(segs, layer, j, after_fetch_issue, x_ref, *refs):
    D, C, NB = D_MODEL, COL_BLOCK, N_COL_BLOCKS
    n = len(segs)
    conv_in = refs[0:n]
    npre_ref, npost_ref, cw_ref, win_hbm, wout_hbm = refs[n:n + 5]
    o_ref = refs[n + 5]
    conv_out = refs[n + 6:2 * n + 6]
    wbg_b_hbm, wcg_b_hbm, wv_b_hbm, wout_b_hbm = refs[2 * n + 6:2 * n + 10]
    xn_scr, cv_scr, col_f, row_f, col_h, row_h, in_sems, out_sems = refs[2 * n + 10:]

    def fetch(b):
        return ([pltpu.make_async_copy(win_hbm.at[j, :, pl.ds(k * D + b * C, C)], col_f.at[b, k], in_sems.at[b, k])
                 for k in range(3)]
                + [pltpu.make_async_copy(wout_hbm.at[j, pl.ds(b * C, C), :], row_f.at[b], in_sems.at[b, 3])])

    def emit(b):
        s = b % OUT_SLOTS
        return ([pltpu.make_async_copy(col_h.at[s, k], dst.at[b], out_sems.at[s, k])
                 for k, dst in enumerate((wbg_b_hbm, wcg_b_hbm, wv_b_hbm))]
                + [pltpu.make_async_copy(row_h.at[s], wout_b_hbm.at[pl.ds(b * C, C), :], out_sems.at[s, 3])])

    for b in range(NB):
        _start_all(fetch(b))
    after_fetch_issue()
    xn_scr[...] = _rms(x_ref[...], npre_ref[layer:layer + 1, :]).astype(BF16)
    xn = xn_scr[...]

    y = None
    for b in range(NB):
        cols = slice(b * C, (b + 1) * C)
        _wait_all(fetch(b))
        w_bg, w_cg, w_v = (col_f[b, k].astype(BF16) for k in range(3))
        w_out_rows = row_f[b].astype(BF16)
        s = b % OUT_SLOTS
        if b >= OUT_SLOTS:
            _wait_all(emit(b - OUT_SLOTS))
        col_h[s, 0], col_h[s, 1], col_h[s, 2], row_h[s] = w_bg, w_cg, w_v, w_out_rows
        _start_all(emit(b))

        cv = _dot(xn, w_cg) * _dot(xn, w_v)
        convs = []
        for seg, c_in, c_out in zip(segs, conv_in, conv_out):
            P = (CONV_B - 1) * seg.S
            cv_scr[seg.conv0:seg.conv0 + P, :] = c_in[:, cols]
            convs.append(_sconv_conv(seg, cv[seg.row0:seg.row0 + seg.rows], cw_ref[:, cols], cv_scr))
            c_out[:, cols] = cv_scr[seg.conv0:seg.conv0 + P, :]
        m = (_dot(xn, w_bg) * jnp.concatenate(convs, axis=0)).astype(BF16)
        part = _dot(m, w_out_rows)
        y = part if y is None else y + part

    for b in range(max(NB - OUT_SLOTS, 0), NB):
        _wait_all(emit(b))
    o_ref[...] = x_ref[...] + _rms(y, npost_ref[layer:layer + 1, :])


def _sconv_small_part(x, conv_in, layer, j, w, *, seqs_and_steps):
    R, D = x.shape
    C, NB = COL_BLOCK, N_COL_BLOCKS
    segs, _, conv_rows, _ = _segments(seqs_and_steps, CONV_B)
    n_layers = w["norm_mix_pre"].shape[0]
    hbm = pl.BlockSpec(memory_space=pl.ANY)
    return _Part(
        kernel=functools.partial(_sconv_small_kernel, segs, layer, j),
        args=[x] + conv_in + [w["norm_mix_pre"], w["norm_mix_post"], w["sc_conv_w"], w["sc_w_in"], w["sc_w_out"]],
        in_specs=[_resident((R, D))] + [_resident(c.shape) for c in conv_in]
                 + [_resident((n_layers, D)), _resident((n_layers, D)),
                    pl.BlockSpec((None, CONV_B, D), lambda i: (j, 0, 0), pipeline_mode=pl.Buffered(1)), hbm, hbm],
        out_specs=[pl.BlockSpec((R, D), lambda i: (0, 0))]
                  + [pl.BlockSpec(c.shape, lambda i: (0, 0)) for c in conv_in] + [hbm] * 4,
        out_shape=[jax.ShapeDtypeStruct((R, D), F32)] + [jax.ShapeDtypeStruct(c.shape, F32) for c in conv_in]
                  + [jax.ShapeDtypeStruct((NB, D, C), BF16)] * 3 + [jax.ShapeDtypeStruct((D, D), BF16)],
        scratch_shapes=[pltpu.VMEM((R, D), BF16),
                        pltpu.VMEM((conv_rows, C), F32),
                        pltpu.VMEM((NB, 3, D, C), F32),
                        pltpu.VMEM((NB, C, D), F32),
                        pltpu.VMEM((OUT_SLOTS, 3, D, C), BF16), pltpu.VMEM((OUT_SLOTS, C, D), BF16),
                        pltpu.SemaphoreType.DMA((NB, 4)), pltpu.SemaphoreType.DMA((OUT_SLOTS, 4))])


def _sconv_kernel(seg, nblk, layer, x_ref, conv_in_ref, npre_ref, npost_ref, wbg_ref, wcg_ref, wv_ref,
                  cw_ref, wout_ref, o_ref, conv_out_ref, xn_scr, cv_scr, m_scr):
    P = (CONV_B - 1) * seg.S
    i = pl.program_id(0)
    blocks = [slice(b * COL_BLOCK, (b + 1) * COL_BLOCK) for b in range(N_COL_BLOCKS)]

    @pl.when(i == 0)
    def _():
        for b, cols in enumerate(blocks):
            cv_scr[b, 0:P, :] = conv_in_ref[:, cols]

    xn_scr[...] = _rms(x_ref[...], npre_ref[layer:layer + 1, :]).astype(BF16)
    xn = xn_scr[...]
    for b, cols in enumerate(blocks):
        cv = _dot(xn, wcg_ref[b]) * _dot(xn, wv_ref[b])
        conv = _sconv_conv(seg, cv, cw_ref[:, cols], cv_scr.at[b])
        m_scr[:, cols] = (_dot(xn, wbg_ref[b]) * conv).astype(BF16)
    y = _dot(m_scr[...], wout_ref[...])
    o_ref[...] = x_ref[...] + _rms(y, npost_ref[layer:layer + 1, :])

    @pl.when(i == nblk - 1)
    def _():
        for b, cols in enumerate(blocks):
            conv_out_ref[:, cols] = cv_scr[b, 0:P, :]


def _sconv_layer(x, conv_in, layer, j, w, bf16, *, S, TB):
    N, D = x.shape
    (seg,), R, conv_rows, _ = _segments(((S, TB),), CONV_B)
    nblk = N // R
    P = (CONV_B - 1) * S
    n_layers = w["norm_mix_pre"].shape[0]
    row_spec = pl.BlockSpec((R, D), lambda i: (i, 0))
    return pl.pallas_call(
        functools.partial(_sconv_kernel, seg, nblk, layer),
        grid=(nblk,),
        in_specs=[row_spec, _resident((P, D)), _resident((n_layers, D)), _resident((n_layers, D)),
                  _resident(bf16["w_bg"].shape), _resident(bf16["w_cg"].shape), _resident(bf16["w_v"].shape),
                  pl.BlockSpec((None, CONV_B, D), lambda i: (j, 0, 0), pipeline_mode=pl.Buffered(1)),
                  _resident((D, D))],
        out_specs=[row_spec, pl.BlockSpec((P, D), lambda i: (0, 0))],
        out_shape=[jax.ShapeDtypeStruct((N, D), F32), jax.ShapeDtypeStruct((P, D), F32)],
        scratch_shapes=[pltpu.VMEM((R, D), BF16),
                        pltpu.VMEM((N_COL_BLOCKS, conv_rows, COL_BLOCK), F32),
                        pltpu.VMEM((R, D), BF16)],
        compiler_params=_params(),
        name="sconv_mixer",
    )(x, conv_in, w["norm_mix_pre"], w["norm_mix_post"], bf16["w_bg"], bf16["w_cg"], bf16["w_v"],
      w["sc_conv_w"], bf16["w_out"])


def _swiglu(g, u):
    return (g * jax.nn.sigmoid(g) * u).astype(BF16)


def _ffn_stream_kernel(layer, nchunk, sample_out, x_ref, npre_ref, npost_ref, wg_hbm, wu_hbm, wd_hbm,
                       o_ref, wg_b_hbm, wu_b_hbm, wd_b_hbm,
                       xn_scr, wg_f, wu_f, wd_f, wg_h, wu_h, wd_h, in_sems, out_sems, *out_staging,
                       prefetch_only=False, prefetched=False):
    F, NB = FF_CHUNK, STREAM_BUFFERS

    def fetch(c):
        k, cols = c % NB, pl.ds(c * F, F)
        return [pltpu.make_async_copy(wg_hbm.at[layer, :, cols], wg_f.at[k], in_sems.at[0, k]),
                pltpu.make_async_copy(wu_hbm.at[layer, :, cols], wu_f.at[k], in_sems.at[1, k]),
                pltpu.make_async_copy(wd_hbm.at[layer, cols, :], wd_f.at[k], in_sems.at[2, k])]

    def emit(c):
        k = c % NB
        return [pltpu.make_async_copy(wg_h.at[k], wg_b_hbm.at[c], out_sems.at[0, k]),
                pltpu.make_async_copy(wu_h.at[k], wu_b_hbm.at[c], out_sems.at[1, k]),
                pltpu.make_async_copy(wd_h.at[k], wd_b_hbm.at[pl.ds(c * F, F), :], out_sems.at[2, k])]

    if not prefetched:
        for c in range(min(NB, nchunk)):
            _start_all(fetch(c))
    if prefetch_only:
        return
    xn_scr[...] = _rms(x_ref[...], npre_ref[layer:layer + 1, :]).astype(BF16)
    xn = xn_scr[...]

    y = None
    for c in range(nchunk):
        k = c % NB
        for cp in fetch(c):
            cp.wait()
        if c >= NB:
            for cp in emit(c - NB):
                cp.wait()
        wg, wu, wd = wg_f[k].astype(BF16), wu_f[k].astype(BF16), wd_f[k].astype(BF16)
        wg_h[k], wu_h[k], wd_h[k] = wg, wu, wd
        for cp in emit(c):
            cp.start()
        if c + NB < nchunk:
            for cp in fetch(c + NB):
                cp.start()
        part = _dot(_swiglu(_dot(xn, wg), _dot(xn, wu)), wd)
        y = part if y is None else y + part
    for c in range(max(nchunk - NB, 0), nchunk):
        for cp in emit(c):
            cp.wait()

    out = x_ref[...] + _rms(y, npost_ref[layer:layer + 1, :])
    if sample_out is None:
        o_ref[...] = out
    else:
        row0, S, T = sample_out
        obuf, sems = out_staging
        obuf[...] = out[row0:row0 + T * S].reshape(T, S, out.shape[-1])
        copies = [pltpu.make_async_copy(obuf.at[t], o_ref.at[:, t, :], sems.at[t]) for t in range(T)]
        for cp in copies:
            cp.start()
        for cp in copies:
            cp.wait()


def _ffn_small_part(R, layer, w, *, sample_out=None):
    D = D_MODEL
    F, NB = FF_CHUNK, STREAM_BUFFERS
    n_layers = w["norm_ffn_pre"].shape[0]
    nchunk = D_FF // F
    hbm = pl.BlockSpec(memory_space=pl.ANY)
    if sample_out is None:
        o_spec, o_shape, staging = pl.BlockSpec((R, D), lambda i: (0, 0)), jax.ShapeDtypeStruct((R, D), F32), []
    else:
        _, S, T = sample_out
        o_spec, o_shape = hbm, jax.ShapeDtypeStruct((S, T, D), F32)
        staging = [pltpu.VMEM((T, S, D), F32), pltpu.SemaphoreType.DMA((T,))]
    return _Part(
        kernel=functools.partial(_ffn_stream_kernel, layer, nchunk, sample_out),
        args=[None, w["norm_ffn_pre"], w["norm_ffn_post"], w["ffn_w_gate"], w["ffn_w_up"], w["ffn_w_down"]],
        in_specs=[_resident((R, D)), _resident((n_layers, D)), _resident((n_layers, D)), hbm, hbm, hbm],
        out_specs=[o_spec, hbm, hbm, hbm],
        out_shape=[o_shape, jax.ShapeDtypeStruct((nchunk, D, F), BF16),
                   jax.ShapeDtypeStruct((nchunk, D, F), BF16), jax.ShapeDtypeStruct((D_FF, D), BF16)],
        scratch_shapes=[pltpu.VMEM((R, D), BF16),
                        pltpu.VMEM((NB, D, F), F32), pltpu.VMEM((NB, D, F), F32), pltpu.VMEM((NB, F, D), F32),
                        pltpu.VMEM((NB, D, F), BF16), pltpu.VMEM((NB, D, F), BF16), pltpu.VMEM((NB, F, D), BF16),
                        pltpu.SemaphoreType.DMA((3, NB)), pltpu.SemaphoreType.DMA((3, NB))] + staging)


def _ffn_kernel(layer, nblk, batch_major_out, x_ref, npre_ref, npost_ref, wg_ref, wu_ref, wd_ref, o_ref,
                xn_scr, act_scr, *out_staging):
    R = x_ref.shape[0]
    i = pl.program_id(0)
    slot = i % 2
    if batch_major_out:
        obuf, sems = out_staging
        S, TB = o_ref.shape[0], R // o_ref.shape[0]

        def copies(which, step):
            t0 = pl.multiple_of(step * TB, TB)
            return [pltpu.make_async_copy(obuf.at[which, :, s, :],
                                          o_ref.at[s, pl.ds(t0, TB), :], sems.at[which, s]) for s in range(S)]

        @pl.when(i >= 2)
        def _():
            for c in copies(slot, i - 2):
                c.wait()

    halves = [slice(0, R // 2), slice(R // 2, R)]
    for rows in halves:
        xn_scr[rows, :] = _rms(x_ref[rows, :], npre_ref[layer:layer + 1, :]).astype(BF16)
    for rows in halves:
        xn = xn_scr[rows, :]
        for c in range(D_FF // FF_CHUNK):
            cols = slice(c * FF_CHUNK, (c + 1) * FF_CHUNK)
            act_scr[rows, cols] = _swiglu(_dot(xn, wg_ref[c]), _dot(xn, wu_ref[c]))
    ys = [_dot(act_scr[rows, :], wd_ref[...]) for rows in halves]
    for rows, y in zip(halves, ys):
        out = x_ref[rows, :] + _rms(y, npost_ref[layer:layer + 1, :])
        if batch_major_out:
            obuf[slot, rows.start // S:rows.stop // S] = out.reshape(-1, S, out.shape[-1])
        else:
            o_ref[rows, :] = out

    if batch_major_out:
        for c in copies(slot, i):
            c.start()

        @pl.when(i == nblk - 1)
        def _():
            if nblk >= 2:
                for c in copies(1 - slot, i - 1):
                    c.wait()
            for c in copies(slot, i):
                c.wait()


def _ffn_layer(x, layer, w, bf16, *, R, batch_major_seqs=None):
    N, D = x.shape
    n_layers = w["norm_ffn_pre"].shape[0]
    nblk = N // R
    row_spec = pl.BlockSpec((R, D), lambda i: (i, 0))
    if batch_major_seqs is None:
        out_spec, out_shape, staging = row_spec, jax.ShapeDtypeStruct((N, D), F32), []
    else:
        S = batch_major_seqs
        out_spec = pl.BlockSpec(memory_space=pl.ANY)
        out_shape = jax.ShapeDtypeStruct((S, N // S, D), F32)
        staging = [pltpu.VMEM((2, R // S, S, D), F32), pltpu.SemaphoreType.DMA((2, S))]
    return pl.pallas_call(
        functools.partial(_ffn_kernel, layer, nblk, batch_major_seqs is not None),
        grid=(nblk,),
        in_specs=[row_spec, _resident((n_layers, D)), _resident((n_layers, D)),
                  _resident(bf16["wg"].shape), _resident(bf16["wu"].shape), _resident((D_FF, D))],
        out_specs=out_spec,
        out_shape=out_shape,
        scratch_shapes=[pltpu.VMEM((R, D), BF16), pltpu.VMEM((R, D_FF), BF16)] + staging,
        compiler_params=_params(),
        name="swiglu_ffn",
    )(x, w["norm_ffn_pre"], w["norm_ffn_post"], bf16["wg"], bf16["wu"], bf16["wd"])


def _to_time_major(a):
    S, K, D = a.shape
    return jnp.swapaxes(a, 0, 1).reshape(K * S, D)


def _from_time_major(a, S):
    KS, D = a.shape
    return jnp.swapaxes(a.reshape(KS // S, S, D), 0, 1)


def kernel(x_prompt, x_sample, state_rglru_conv, state_rglru_h, state_sconv, meta_tokens, norm_mix_pre, norm_mix_post, norm_ffn_pre, norm_ffn_post, rg_w_in, rg_conv_w, rg_conv_b, rg_gate_a_w, rg_gate_a_b, rg_gate_x_w, rg_gate_x_b, rg_lambda, rg_w_out, sc_w_in, sc_conv_w, sc_w_out, ffn_w_gate, ffn_w_up, ffn_w_down):
    D = D_MODEL
    depth = norm_mix_pre.shape[0]
    batch, seq, _ = x_prompt.shape
    dec_batch, dec_seq, _ = x_sample.shape
    w = dict(norm_mix_pre=norm_mix_pre, norm_mix_post=norm_mix_post, norm_ffn_pre=norm_ffn_pre,
             norm_ffn_post=norm_ffn_post, rg_w_in=rg_w_in, rg_conv_w=rg_conv_w, rg_conv_b=rg_conv_b,
             rg_gate_a_w=rg_gate_a_w, rg_gate_a_b=rg_gate_a_b, rg_gate_x_w=rg_gate_x_w, rg_gate_x_b=rg_gate_x_b,
             rg_lambda=rg_lambda, rg_w_out=rg_w_out, sc_w_in=sc_w_in, sc_conv_w=sc_conv_w, sc_w_out=sc_w_out,
             ffn_w_gate=ffn_w_gate, ffn_w_up=ffn_w_up, ffn_w_down=ffn_w_down)

    x = jnp.broadcast_to(meta_tokens[:, None, :], (N_META, batch, D)).reshape(N_META * batch, D)
    small = ((batch, N_META), (dec_batch, dec_seq))
    small_rows = N_META * batch + dec_batch * dec_seq
    mixer_bf16, ffn_bf16 = [], []
    rg_conv_s, rg_h_s, sc_s = [], [], []
    for i in range(depth):
        j = i // 2
        sample_out = (N_META * batch, dec_batch, dec_seq) if i == depth - 1 else None
        ffn = _ffn_small_part(small_rows, i, w, sample_out=sample_out)
        if i % 2 == 0:
            conv0 = [jnp.zeros(((CONV_A - 1) * batch, D), F32), _to_time_major(state_rglru_conv[j])]
            h0 = [jnp.zeros((batch, D), F32), state_rglru_h[j]]
            mixer = _rglru_small_part(x, conv0, h0, i, j, w, seqs_and_steps=small,
                                      x_sample=x_sample if i == 0 else None)
            (cs0, cs1, hs0, hs1, *wb), (x, *fb) = _run_small_stage(mixer, ffn, small_rows, "rglru_ffn_small")
            rg_conv_s.append([cs0, cs1])
            rg_h_s.append([hs0, hs1])
            mixer_bf16.append(dict(zip(("w_xr", "w_gate", "gw", "w_out"), wb)))
        else:
            conv0 = [jnp.zeros(((CONV_B - 1) * batch, D), F32), _to_time_major(state_sconv[j])]
            mixer = _sconv_small_part(x, conv0, i, j, w, seqs_and_steps=small)
            (cs0, cs1, *wb), (x, *fb) = _run_small_stage(mixer, ffn, small_rows, "sconv_ffn_small")
            sc_s.append([cs0, cs1])
            mixer_bf16.append(dict(zip(("w_bg", "w_cg", "w_v", "w_out"), wb)))
        ffn_bf16.append(dict(zip(("wg", "wu", "wd"), fb)))
    y_sample = x

    x = x_prompt
    rg_conv_p, rg_h_p, sc_p = [], [], []
    for i in range(depth):
        j = i // 2
        if i % 2 == 0:
            x, cb, hT = _rglru_layer(x, rg_conv_s[j][0], rg_h_s[j][0], i, j, w, mixer_bf16[i], S=batch, TB=PROMPT_TB)
            rg_conv_p.append(cb)
            rg_h_p.append(hT)
        else:
            x, cb = _sconv_layer(x, sc_s[j][0], i, j, w, mixer_bf16[i], S=batch, TB=PROMPT_TB)
            sc_p.append(cb)
        x = _ffn_layer(x, i, w, ffn_bf16[i], R=batch * PROMPT_TB,
                       batch_major_seqs=batch if i == depth - 1 else None)
    y_prompt = x

    return (y_prompt, y_sample,
            jnp.stack([_from_time_major(c, batch) for c in rg_conv_p]), jnp.stack(rg_h_p),
            jnp.stack([_from_time_major(c, batch) for c in sc_p]),
            jnp.stack([_from_time_major(c[1], dec_batch) for c in rg_conv_s]), jnp.stack([h[1] for h in rg_h_s]),
            jnp.stack([_from_time_major(c[1], dec_batch) for c in sc_s]))
```

```python
import functools
from typing import NamedTuple

import jax
import jax.numpy as jnp
from jax import lax
from jax.experimental import pallas as pl
from jax.experimental.pallas import tpu as pltpu

D_MODEL = 1024
D_FF = 2816
N_META = 16
COL_BLOCK = 256
N_COL_BLOCKS = D_MODEL // COL_BLOCK
CONV_A = 4
CONV_B = 3
RG_C = 8.0
EPS = 1e-6

SUBLANES = 8
FF_CHUNK = 256
STREAM_BUFFERS = 3
OUT_SLOTS = 2
PROMPT_TB = 128
FFN_ROW_PARTS = 4
VMEM_LIMIT_BYTES = 56 * 1024 * 1024
SMALL_STAGE_VMEM_LIMIT_BYTES = 60 * 1024 * 1024

F32 = jnp.float32
BF16 = jnp.bfloat16


class _Seg(NamedTuple):
    S: int
    TB: int
    row0: int
    conv0: int
    h0: int

    @property
    def rows(self):
        return self.S * self.TB


def _segments(seqs_and_steps, taps):
    segs, row0, conv0, h0 = [], 0, 0, 0
    for S, TB in seqs_and_steps:
        segs.append(_Seg(S, TB, row0, conv0, h0))
        row0 += S * TB
        conv0 += (taps - 1) * S + S * TB
        h0 += S
    return tuple(segs), row0, conv0, h0


def _rms(x, w):
    ms = jnp.mean(x * x, axis=-1, keepdims=True)
    return x * lax.rsqrt(ms + EPS) * w


def _dot(a, b):
    return jnp.dot(a, b, preferred_element_type=F32)


def _gelu_tanh(x):
    c = 0.7978845608028654
    hx = 0.5 * x
    return hx + hx * jnp.tanh(x * (c + (c * 0.044715) * (x * x)))


def _resident(shape):
    zeros = (0,) * len(shape)
    return pl.BlockSpec(shape, lambda i: zeros, pipeline_mode=pl.Buffered(1))


def _params(vmem_limit_bytes=VMEM_LIMIT_BYTES):
    return pltpu.CompilerParams(dimension_semantics=("arbitrary",), vmem_limit_bytes=vmem_limit_bytes)


def _start_all(copies):
    for cp in copies:
        cp.start()


def _wait_all(copies):
    for cp in copies:
        cp.wait()


def _rglru_branch_in(seg, xn, w_xr, region):
    P = (CONV_A - 1) * seg.S
    region[seg.conv0 + P:seg.conv0 + P + seg.rows, :] = _dot(xn, w_xr)


def _rglru_conv(seg, cw, cb, region, carry_to):
    S, R = seg.S, seg.rows
    P = (CONV_A - 1) * S
    c0 = seg.conv0
    xc = region[c0:c0 + R, :] * cw[0:1]
    for k in range(1, CONV_A):
        xc = xc + region[c0 + k * S:c0 + k * S + R, :] * cw[k:k + 1]
    xc = xc + cb
    if carry_to is not None:
        region[carry_to:carry_to + P, :] = region[c0 + R:c0 + R + P, :]
    return xc


def _rglru_gate_math(res, xc, half_gab, half_gxb, lam):
    half_c_sp = (-0.5 * RG_C) * jax.nn.softplus(-lam)
    tr = jnp.tanh(res[:, 0:COL_BLOCK] + half_gab)
    log_a = half_c_sp * tr + half_c_sp
    ig = 0.5 * jnp.tanh(res[:, COL_BLOCK:2 * COL_BLOCK] + half_gxb) + 0.5
    a = jnp.exp(log_a)
    m2 = jnp.tanh(log_a) * (-1.0 - a * a)
    u = jnp.where(m2 > 0.0, m2 * lax.rsqrt(m2), 0.0) * (ig * xc)
    return a, u


def _rglru_scan(seg, a, u, g, h_read, h_write, h_row0):
    S, TB = seg.S, seg.TB
    groups = S // SUBLANES
    pieces = [None] * (TB * groups)
    for c in range(groups):
        hrows = slice(h_row0 + c * SUBLANES, h_row0 + (c + 1) * SUBLANES)
        h = h_read[hrows, :]
        for t in range(TB):
            r = t * S + c * SUBLANES
            h = a[r:r + SUBLANES] * h + u[r:r + SUBLANES]
            pieces[t * groups + c] = h * g[r:r + SUBLANES]
        h_write[hrows, :] = h
    return jnp.concatenate(pieces, axis=0)


class _Part(NamedTuple):
    kernel: object
    args: list
    in_specs: list
    out_specs: list
    out_shape: list
    scratch_shapes: list


def _run_small_stage(mixer, ffn, rows, name):
    n_min, n_fin = len(mixer.args), len(ffn.args) - 1
    n_mout, n_fout = len(mixer.out_shape) - 1, len(ffn.out_shape)
    n_mscr, n_fscr = len(mixer.scratch_shapes), len(ffn.scratch_shapes)

    def kernel(*refs):
        refs = list(refs)
        take = lambda k: [refs.pop(0) for _ in range(k)]
        m_in, f_in, m_out, f_out, m_scr, f_scr = (take(k) for k in (n_min, n_fin, n_mout, n_fout, n_mscr, n_fscr))
        (x_mid,) = refs
        f_refs = [x_mid] + f_in + f_out + f_scr
        mixer.kernel(lambda: ffn.kernel(*f_refs, prefetch_only=True), *m_in, x_mid, *m_out, *m_scr)
        ffn.kernel(*f_refs, prefetched=True)

    outs = pl.pallas_call(
        kernel,
        grid=(1,),
        in_specs=mixer.in_specs + ffn.in_specs[1:],
        out_specs=mixer.out_specs[1:] + ffn.out_specs,
        out_shape=mixer.out_shape[1:] + ffn.out_shape,
        scratch_shapes=mixer.scratch_shapes + ffn.scratch_shapes + [pltpu.VMEM((rows, D_MODEL), F32)],
        compiler_params=_params(SMALL_STAGE_VMEM_LIMIT_BYTES),
        name=name,
    )(*mixer.args, *ffn.args[1:])
    return list(outs[:n_mout]), list(outs[n_mout:])


def _rglru_small_kernel(segs, sample_in, layer, j, after_fetch_issue, x_ref, *refs):
    D, C, NB = D_MODEL, COL_BLOCK, N_COL_BLOCKS
    if sample_in is not None:
        xs_hbm, refs = refs[0], refs[1:]
    n = len(segs)
    conv_in, h_in = refs[0:n], refs[n:2 * n]
    (npre_ref, npost_ref, cw_ref, cb_ref, gab_ref, gxb_ref, lam_ref,
     win_hbm, gaw_hbm, gxw_hbm, wout_hbm) = refs[2 * n:2 * n + 11]
    o_ref = refs[2 * n + 11]
    conv_out, h_out = refs[2 * n + 12:3 * n + 12], refs[3 * n + 12:4 * n + 12]
    wxr_b_hbm, wgate_b_hbm, gw_b_hbm, wout_b_hbm = refs[4 * n + 12:4 * n + 16]
    (xn_scr, xr_scr, col_f, sq_f, row_f, col_h, gw_h, row_h, in_sems, out_sems,
     *in_staging) = refs[4 * n + 16:]

    def fetch(b):
        cols = pl.ds(b * C, C)
        return [pltpu.make_async_copy(win_hbm.at[j, :, pl.ds(D + b * C, C)], col_f.at[b, 0], in_sems.at[b, 0]),
                pltpu.make_async_copy(win_hbm.at[j, :, cols], col_f.at[b, 1], in_sems.at[b, 1]),
                pltpu.make_async_copy(gaw_hbm.at[j, b], sq_f.at[b, 0], in_sems.at[b, 2]),
                pltpu.make_async_copy(gxw_hbm.at[j, b], sq_f.at[b, 1], in_sems.at[b, 3]),
                pltpu.make_async_copy(wout_hbm.at[j, cols, :], row_f.at[b], in_sems.at[b, 4])]

    def emit(b):
        k = b % OUT_SLOTS
        return [pltpu.make_async_copy(col_h.at[k, 0], wxr_b_hbm.at[b], out_sems.at[k, 0]),
                pltpu.make_async_copy(col_h.at[k, 1], wgate_b_hbm.at[b], out_sems.at[k, 1]),
                pltpu.make_async_copy(gw_h.at[k], gw_b_hbm.at[b], out_sems.at[k, 2]),
                pltpu.make_async_copy(row_h.at[k], wout_b_hbm.at[pl.ds(b * C, C), :], out_sems.at[k, 3])]

    if sample_in is None:
        read_x = lambda: x_ref[...]
    else:
        S, T = sample_in
        xs_scr, sems = in_staging
        x_copies = [pltpu.make_async_copy(xs_hbm.at[:, t, :], xs_scr.at[t], sems.at[t]) for t in range(T)]
        _start_all(x_copies)
        read_x = lambda: jnp.concatenate([x_ref[...], xs_scr[...].reshape(T * S, D)], axis=0)
    for b in range(NB):
        _start_all(fetch(b))
    after_fetch_issue()
    if sample_in is not None:
        _wait_all(x_copies)
    xn_scr[...] = _rms(read_x(), npre_ref[layer:layer + 1, :]).astype(BF16)
    xn = xn_scr[...]

    rows_of = lambda seg: slice(seg.row0, seg.row0 + seg.rows)
    y = None
    for b in range(NB):
        cols = slice(b * C, (b + 1) * C)
        _wait_all(fetch(b))
        w_xr, w_gate = col_f[b, 0].astype(BF16), col_f[b, 1].astype(BF16)
        gw = (0.5 * jnp.concatenate([sq_f[b, 0], sq_f[b, 1]], axis=1)).astype(BF16)
        w_out_rows = row_f[b].astype(BF16)
        k = b % OUT_SLOTS
        if b >= OUT_SLOTS:
            _wait_all(emit(b - OUT_SLOTS))
        col_h[k, 0], col_h[k, 1], gw_h[k], row_h[k] = w_xr, w_gate, gw, w_out_rows
        _start_all(emit(b))

        xr = _dot(xn, w_xr)
        g = _gelu_tanh(_dot(xn, w_gate))
        xcs = []
        for seg, c_in, c_out in zip(segs, conv_in, conv_out):
            P = (CONV_A - 1) * seg.S
            xr_scr[seg.conv0:seg.conv0 + P, :] = c_in[:, cols]
            xr_scr[seg.conv0 + P:seg.conv0 + P + seg.rows, :] = xr[rows_of(seg)]
            xcs.append(_rglru_conv(seg, cw_ref[:, cols], cb_ref[j:j + 1, cols], xr_scr, seg.conv0))
            c_out[:, cols] = xr_scr[seg.conv0:seg.conv0 + P, :]
        xc = jnp.concatenate(xcs, axis=0)
        a, u = _rglru_gate_math(_dot(xc.astype(BF16), gw), xc, 0.5 * gab_ref[b:b + 1, :],
                                0.5 * gxb_ref[b:b + 1, :], lam_ref[j:j + 1, cols])
        hs = [_rglru_scan(seg, a[rows_of(seg)], u[rows_of(seg)], g[rows_of(seg)],
                          hi.at[:, cols], ho.at[:, cols], 0)
              for seg, hi, ho in zip(segs, h_in, h_out)]
        part = _dot(jnp.concatenate(hs, axis=0).astype(BF16), w_out_rows)
        y = part if y is None else y + part

    for b in range(max(NB - OUT_SLOTS, 0), NB):
        _wait_all(emit(b))
    o_ref[...] = read_x() + _rms(y, npost_ref[layer:layer + 1, :])


def _rglru_small_part(x, conv_in, h_in, layer, j, w, *, seqs_and_steps, x_sample=None):
    D, C, NB = D_MODEL, COL_BLOCK, N_COL_BLOCKS
    segs, R, conv_rows, _ = _segments(seqs_and_steps, CONV_A)
    n_layers, n_a = w["norm_mix_pre"].shape[0], w["rg_conv_b"].shape[0]
    hbm = pl.BlockSpec(memory_space=pl.ANY)
    layer_of = lambda arr: pl.BlockSpec((None,) + arr.shape[1:], lambda i: (j,) + (0,) * (arr.ndim - 1),
                                        pipeline_mode=pl.Buffered(1))
    if x_sample is None:
        x_args, x_specs, sample_in, staging = [x], [_resident(x.shape)], None, []
    else:
        S, T, _ = x_sample.shape
        x_args, x_specs, sample_in = [x, x_sample], [_resident(x.shape), hbm], (S, T)
        staging = [pltpu.VMEM((T, S, D), F32), pltpu.SemaphoreType.DMA((T,))]
    states = conv_in + h_in
    return _Part(
        kernel=functools.partial(_rglru_small_kernel, segs, sample_in, layer, j),
        args=x_args + states + [w["norm_mix_pre"], w["norm_mix_post"], w["rg_conv_w"], w["rg_conv_b"],
                                w["rg_gate_a_b"], w["rg_gate_x_b"], w["rg_lambda"], w["rg_w_in"],
                                w["rg_gate_a_w"], w["rg_gate_x_w"], w["rg_w_out"]],
        in_specs=x_specs + [_resident(s.shape) for s in states]
                 + [_resident((n_layers, D)), _resident((n_layers, D)), layer_of(w["rg_conv_w"]),
                    _resident((n_a, D)), layer_of(w["rg_gate_a_b"]), layer_of(w["rg_gate_x_b"]),
                    _resident((n_a, D)), hbm, hbm, hbm, hbm],
        out_specs=[pl.BlockSpec((R, D), lambda i: (0, 0))]
                  + [pl.BlockSpec(s.shape, lambda i: (0, 0)) for s in states] + [hbm] * 4,
        out_shape=[jax.ShapeDtypeStruct((R, D), F32)] + [jax.ShapeDtypeStruct(s.shape, F32) for s in states]
                  + [jax.ShapeDtypeStruct((NB, D, C), BF16), jax.ShapeDtypeStruct((NB, D, C), BF16),
                     jax.ShapeDtypeStruct((NB, C, 2 * C), BF16), jax.ShapeDtypeStruct((D, D), BF16)],
        scratch_shapes=[pltpu.VMEM((R, D), BF16),
                        pltpu.VMEM((conv_rows, C), F32),
                        pltpu.VMEM((NB, 2, D, C), F32),
                        pltpu.VMEM((NB, 2, C, C), F32),
                        pltpu.VMEM((NB, C, D), F32),
                        pltpu.VMEM((OUT_SLOTS, 2, D, C), BF16), pltpu.VMEM((OUT_SLOTS, C, 2 * C), BF16),
                        pltpu.VMEM((OUT_SLOTS, C, D), BF16),
                        pltpu.SemaphoreType.DMA((NB, 5)), pltpu.SemaphoreType.DMA((OUT_SLOTS, 4))] + staging)


def _rglru_kernel(seg, nblk, layer, j, batch_major_in,
                  x_ref, conv_in_ref, h_in_ref, npre_ref, npost_ref, wxr_ref, wgate_ref, cw_ref, cb_ref,
                  gw_ref, gab_ref, gxb_ref, lam_ref, wout_ref,
                  o_ref, conv_out_ref, h_out_ref, xn_scr, xr_scr, h_scr, *in_staging):
    S, TB, R = seg.S, seg.TB, seg.rows
    P = (CONV_A - 1) * S
    i = pl.program_id(0)
    slot = i % 2
    blocks = [slice(b * COL_BLOCK, (b + 1) * COL_BLOCK) for b in range(N_COL_BLOCKS)]

    if batch_major_in:
        xbuf, sems = in_staging

        def copies(which, step):
            t0 = pl.multiple_of(step * TB, TB)
            return [pltpu.make_async_copy(x_ref.at[s, pl.ds(t0, TB), :], xbuf.at[which, :, s, :],
                                          sems.at[which, s]) for s in range(S)]

    @pl.when(i == 0)
    def _():
        for b, cols in enumerate(blocks):
            xr_scr[b, 0:P, :] = conv_in_ref[:, cols]
            h_scr[b] = h_in_ref[:, cols]
        if batch_major_in:
            for c in copies(0, 0):
                c.start()

    if batch_major_in:
        @pl.when(i + 1 < nblk)
        def _():
            for c in copies(1 - slot, i + 1):
                c.start()

        for c in copies(slot, i):
            c.wait()
        read_x = lambda: xbuf[slot].reshape(R, D_MODEL)
    else:
        read_x = lambda: x_ref[...]

    xn_scr[...] = _rms(read_x(), npre_ref[layer:layer + 1, :]).astype(BF16)
    xn = xn_scr[...]

    gate_pre, xc, res = {}, {}, {}

    def input_matmuls(b):
        _rglru_branch_in(seg, xn, wxr_ref[b], xr_scr.at[b])
        gate_pre[b] = _dot(xn, wgate_ref[b])

    def conv_and_gate_matmul(b):
        xc[b] = _rglru_conv(seg, cw_ref[:, blocks[b]], cb_ref[j:j + 1, blocks[b]], xr_scr.at[b], 0)
        res[b] = _dot(xc[b].astype(BF16), gw_ref[b])

    y = None
    input_matmuls(0)
    conv_and_gate_matmul(0)
    input_matmuls(1)
    for b, cols in enumerate(blocks):
        if b + 1 < N_COL_BLOCKS:
            conv_and_gate_matmul(b + 1)
        if b + 2 < N_COL_BLOCKS:
            input_matmuls(b + 2)
        g = _gelu_tanh(gate_pre.pop(b))
        a, u = _rglru_gate_math(res.pop(b), xc.pop(b), 0.5 * gab_ref[b:b + 1, :], 0.5 * gxb_ref[b:b + 1, :],
                                lam_ref[j:j + 1, cols])
        hs = _rglru_scan(seg, a, u, g, h_scr.at[b], h_scr.at[b], 0)
        part = _dot(hs.astype(BF16), wout_ref[cols, :])
        y = part if y is None else y + part

    o_ref[...] = read_x() + _rms(y, npost_ref[layer:layer + 1, :])

    @pl.when(i == nblk - 1)
    def _():
        for b, cols in enumerate(blocks):
            conv_out_ref[:, cols] = xr_scr[b, 0:P, :]
            h_out_ref[:, cols] = h_scr[b]


def _rglru_layer(x, conv_in, h_in, layer, j, w, bf16, *, S, TB):
    D = D_MODEL
    batch_major_in = x.ndim == 3
    (seg,), R, conv_rows, _ = _segments(((S, TB),), CONV_A)
    nblk = x.size // D // R
    P = (CONV_A - 1) * S
    n_layers, n_a = w["norm_mix_pre"].shape[0], w["rg_conv_b"].shape[0]
    row_spec = pl.BlockSpec((R, D), lambda i: (i, 0))
    if batch_major_in:
        x_spec = pl.BlockSpec(memory_space=pl.ANY)
        staging = [pltpu.VMEM((2, TB, S, D), F32), pltpu.SemaphoreType.DMA((2, S))]
    else:
        x_spec, staging = row_spec, []
    layer_of = lambda arr: pl.BlockSpec((None,) + arr.shape[1:], lambda i: (j,) + (0,) * (arr.ndim - 1),
                                        pipeline_mode=pl.Buffered(1))
    return pl.pallas_call(
        functools.partial(_rglru_kernel, seg, nblk, layer, j, batch_major_in),
        grid=(nblk,),
        in_specs=[x_spec, _resident((P, D)), _resident((S, D)),
                  _resident((n_layers, D)), _resident((n_layers, D)),
                  _resident(bf16["w_xr"].shape), _resident(bf16["w_gate"].shape),
                  layer_of(w["rg_conv_w"]), _resident((n_a, D)), _resident(bf16["gw"].shape),
                  layer_of(w["rg_gate_a_b"]), layer_of(w["rg_gate_x_b"]), _resident((n_a, D)),
                  _resident((D, D))],
        out_specs=[row_spec, pl.BlockSpec((P, D), lambda i: (0, 0)), pl.BlockSpec((S, D), lambda i: (0, 0))],
        out_shape=[jax.ShapeDtypeStruct((nblk * R, D), F32), jax.ShapeDtypeStruct((P, D), F32),
                   jax.ShapeDtypeStruct((S, D), F32)],
        scratch_shapes=[pltpu.VMEM((R, D), BF16),
                        pltpu.VMEM((N_COL_BLOCKS, conv_rows, COL_BLOCK), F32),
                        pltpu.VMEM((N_COL_BLOCKS, S, COL_BLOCK), F32)]
                       + staging,
        compiler_params=_params(),
        name="rglru_mixer",
    )(x, conv_in, h_in, w["norm_mix_pre"], w["norm_mix_post"], bf16["w_xr"], bf16["w_gate"], w["rg_conv_w"],
      w["rg_conv_b"], bf16["gw"], w["rg_gate_a_b"], w["rg_gate_x_b"], w["rg_lambda"], bf16["w_out"])


def _sconv_conv(seg, cv, cw, region):
    S, R = seg.S, seg.rows
    P = (CONV_B - 1) * S
    c0 = seg.conv0
    region[c0 + P:c0 + P + R, :] = cv
    conv = region[c0:c0 + R, :] * cw[0:1]
    for k in range(1, CONV_B):
        conv = conv + region[c0 + k * S:c0 + k * S + R, :] * cw[k:k + 1]
    region[c0:c0 + P, :] = region[c0 + R:c0 + R + P, :]
    return conv


def _sconv_small_kernel(segs, layer, j, after_fetch_issue, x_ref, *refs):
    D, C, NB = D_MODEL, COL_BLOCK, N_COL_BLOCKS
    n = len(segs)
    conv_in = refs[0:n]
    npre_ref, npost_ref, cw_ref, win_hbm, wout_hbm = refs[n:n + 5]
    o_ref = refs[n + 5]
    conv_out = refs[n + 6:2 * n + 6]
    wbg_b_hbm, wcg_b_hbm, wv_b_hbm, wout_b_hbm = refs[2 * n + 6:2 * n + 10]
    xn_scr, cv_scr, col_f, row_f, col_h, row_h, in_sems, out_sems = refs[2 * n + 10:]

    def fetch(b):
        return ([pltpu.make_async_copy(win_hbm.at[j, :, pl.ds(k * D + b * C, C)], col_f.at[b, k], in_sems.at[b, k])
                 for k in range(3)]
                + [pltpu.make_async_copy(wout_hbm.at[j, pl.ds(b * C, C), :], row_f.at[b], in_sems.at[b, 3])])

    def emit(b):
        s = b % OUT_SLOTS
        return ([pltpu.make_async_copy(col_h.at[s, k], dst.at[b], out_sems.at[s, k])
                 for k, dst in enumerate((wbg_b_hbm, wcg_b_hbm, wv_b_hbm))]
                + [pltpu.make_async_copy(row_h.at[s], wout_b_hbm.at[pl.ds(b * C, C), :], out_sems.at[s, 3])])

    for b in range(NB):
        _start_all(fetch(b))
    after_fetch_issue()
    xn_scr[...] = _rms(x_ref[...], npre_ref[layer:layer + 1, :]).astype(BF16)
    xn = xn_scr[...]

    y = None
    for b in range(NB):
        cols = slice(b * C, (b + 1) * C)
        _wait_all(fetch(b))
        w_bg, w_cg, w_v = (col_f[b, k].astype(BF16) for k in range(3))
        w_out_rows = row_f[b].astype(BF16)
        s = b % OUT_SLOTS
        if b >= OUT_SLOTS:
            _wait_all(emit(b - OUT_SLOTS))
        col_h[s, 0], col_h[s, 1], col_h[s, 2], row_h[s] = w_bg, w_cg, w_v, w_out_rows
        _start_all(emit(b))

        cv = _dot(xn, w_cg) * _dot(xn, w_v)
        convs = []
        for seg, c_in, c_out in zip(segs, conv_in, conv_out):
            P = (CONV_B - 1) * seg.S
            cv_scr[seg.conv0:seg.conv0 + P, :] = c_in[:, cols]
            convs.append(_sconv_conv(seg, cv[seg.row0:seg.row0 + seg.rows], cw_ref[:, cols], cv_scr))
            c_out[:, cols] = cv_scr[seg.conv0:seg.conv0 + P, :]
        m = (_dot(xn, w_bg) * jnp.concatenate(convs, axis=0)).astype(BF16)
        part = _dot(m, w_out_rows)
        y = part if y is None else y + part

    for b in range(max(NB - OUT_SLOTS, 0), NB):
        _wait_all(emit(b))
    o_ref[...] = x_ref[...] + _rms(y, npost_ref[layer:layer + 1, :])


def _sconv_small_part(x, conv_in, layer, j, w, *, seqs_and_steps):
    R, D = x.shape
    C, NB = COL_BLOCK, N_COL_BLOCKS
    segs, _, conv_rows, _ = _segments(seqs_and_steps, CONV_B)
    n_layers = w["norm_mix_pre"].shape[0]
    hbm = pl.BlockSpec(memory_space=pl.ANY)
    return _Part(
        kernel=functools.partial(_sconv_small_kernel, segs, layer, j),
        args=[x] + conv_in + [w["norm_mix_pre"], w["norm_mix_post"], w["sc_conv_w"], w["sc_w_in"], w["sc_w_out"]],
        in_specs=[_resident((R, D))] + [_resident(c.shape) for c in conv_in]
                 + [_resident((n_layers, D)), _resident((n_layers, D)),
                    pl.BlockSpec((None, CONV_B, D), lambda i: (j, 0, 0), pipeline_mode=pl.Buffered(1)), hbm, hbm],
        out_specs=[pl.BlockSpec((R, D), lambda i: (0, 0))]
                  + [pl.BlockSpec(c.shape, lambda i: (0, 0)) for c in conv_in] + [hbm] * 4,
        out_shape=[jax.ShapeDtypeStruct((R, D), F32)] + [jax.ShapeDtypeStruct(c.shape, F32) for c in conv_in]
                  + [jax.ShapeDtypeStruct((NB, D, C), BF16)] * 3 + [jax.ShapeDtypeStruct((D, D), BF16)],
        scratch_shapes=[pltpu.VMEM((R, D), BF16),
                        pltpu.VMEM((conv_rows, C), F32),
                        pltpu.VMEM((NB, 3, D, C), F32),
                        pltpu.VMEM((NB, C, D), F32),
                        pltpu.VMEM((OUT_SLOTS, 3, D, C), BF16), pltpu.VMEM((OUT_SLOTS, C, D), BF16),
                        pltpu.SemaphoreType.DMA((NB, 4)), pltpu.SemaphoreType.DMA((OUT_SLOTS, 4))])


def _sconv_kernel(seg, nblk, layer, x_ref, conv_in_ref, npre_ref, npost_ref, wbg_ref, wcg_ref, wv_ref,
                  cw_ref, wout_ref, o_ref, conv_out_ref, xn_scr, cv_scr, m_scr):
    P = (CONV_B - 1) * seg.S
    i = pl.program_id(0)
    blocks = [slice(b * COL_BLOCK, (b + 1) * COL_BLOCK) for b in range(N_COL_BLOCKS)]

    @pl.when(i == 0)
    def _():
        for b, cols in enumerate(blocks):
            cv_scr[b, 0:P, :] = conv_in_ref[:, cols]

    xn_scr[...] = _rms(x_ref[...], npre_ref[layer:layer + 1, :]).astype(BF16)
    xn = xn_scr[...]
    for b, cols in enumerate(blocks):
        cv = _dot(xn, wcg_ref[b]) * _dot(xn, wv_ref[b])
        conv = _sconv_conv(seg, cv, cw_ref[:, cols], cv_scr.at[b])
        m_scr[:, cols] = (_dot(xn, wbg_ref[b]) * conv).astype(BF16)
    y = _dot(m_scr[...], wout_ref[...])
    o_ref[...] = x_ref[...] + _rms(y, npost_ref[layer:layer + 1, :])

    @pl.when(i == nblk - 1)
    def _():
        for b, cols in enumerate(blocks):
            conv_out_ref[:, cols] = cv_scr[b, 0:P, :]


def _sconv_layer(x, conv_in, layer, j, w, bf16, *, S, TB):
    N, D = x.shape
    (seg,), R, conv_rows, _ = _segments(((S, TB),), CONV_B)
    nblk = N // R
    P = (CONV_B - 1) * S
    n_layers = w["norm_mix_pre"].shape[0]
    row_spec = pl.BlockSpec((R, D), lambda i: (i, 0))
    return pl.pallas_call(
        functools.partial(_sconv_kernel, seg, nblk, layer),
        grid=(nblk,),
        in_specs=[row_spec, _resident((P, D)), _resident((n_layers, D)), _resident((n_layers, D)),
                  _resident(bf16["w_bg"].shape), _resident(bf16["w_cg"].shape), _resident(bf16["w_v"].shape),
                  pl.BlockSpec((None, CONV_B, D), lambda i: (j, 0, 0), pipeline_mode=pl.Buffered(1)),
                  _resident((D, D))],
        out_specs=[row_spec, pl.BlockSpec((P, D), lambda i: (0, 0))],
        out_shape=[jax.ShapeDtypeStruct((N, D), F32), jax.ShapeDtypeStruct((P, D), F32)],
        scratch_shapes=[pltpu.VMEM((R, D), BF16),
                        pltpu.VMEM((N_COL_BLOCKS, conv_rows, COL_BLOCK), F32),
                        pltpu.VMEM((R, D), BF16)],
        compiler_params=_params(),
        name="sconv_mixer",
    )(x, conv_in, w["norm_mix_pre"], w["norm_mix_post"], bf16["w_bg"], bf16["w_cg"], bf16["w_v"],
      w["sc_conv_w"], bf16["w_out"])


def _swiglu(g, u):
    return (g * jax.nn.sigmoid(g) * u).astype(BF16)


def _ffn_stream_kernel(layer, nchunk, sample_out, x_ref, npre_ref, npost_ref, wg_hbm, wu_hbm, wd_hbm,
                       o_ref, wg_b_hbm, wu_b_hbm, wd_b_hbm,
                       xn_scr, wg_f, wu_f, wd_f, wg_h, wu_h, wd_h, in_sems, out_sems, *out_staging,
                       prefetch_only=False, prefetched=False):
    F, NB = FF_CHUNK, STREAM_BUFFERS

    def fetch(c):
        k, cols = c % NB, pl.ds(c * F, F)
        return [pltpu.make_async_copy(wg_hbm.at[layer, :, cols], wg_f.at[k], in_sems.at[0, k]),
                pltpu.make_async_copy(wu_hbm.at[layer, :, cols], wu_f.at[k], in_sems.at[1, k]),
                pltpu.make_async_copy(wd_hbm.at[layer, cols, :], wd_f.at[k], in_sems.at[2, k])]

    def emit(c):
        k = c % NB
        return [pltpu.make_async_copy(wg_h.at[k], wg_b_hbm.at[c], out_sems.at[0, k]),
                pltpu.make_async_copy(wu_h.at[k], wu_b_hbm.at[c], out_sems.at[1, k]),
                pltpu.make_async_copy(wd_h.at[k], wd_b_hbm.at[pl.ds(c * F, F), :], out_sems.at[2, k])]

    if not prefetched:
        for c in range(min(NB, nchunk)):
            _start_all(fetch(c))
    if prefetch_only:
        return
    xn_scr[...] = _rms(x_ref[...], npre_ref[layer:layer + 1, :]).astype(BF16)
    xn = xn_scr[...]

    y = None
    for c in range(nchunk):
        k = c % NB
        for cp in fetch(c):
            cp.wait()
        if c >= NB:
            for cp in emit(c - NB):
                cp.wait()
        wg, wu, wd = wg_f[k].astype(BF16), wu_f[k].astype(BF16), wd_f[k].astype(BF16)
        wg_h[k], wu_h[k], wd_h[k] = wg, wu, wd
        for cp in emit(c):
            cp.start()
        if c + NB < nchunk:
            for cp in fetch(c + NB):
                cp.start()
        part = _dot(_swiglu(_dot(xn, wg), _dot(xn, wu)), wd)
        y = part if y is None else y + part
    for c in range(max(nchunk - NB, 0), nchunk):
        for cp in emit(c):
            cp.wait()

    out = x_ref[...] + _rms(y, npost_ref[layer:layer + 1, :])
    if sample_out is None:
        o_ref[...] = out
    else:
        row0, S, T = sample_out
        obuf, sems = out_staging
        obuf[...] = out[row0:row0 + T * S].reshape(T, S, out.shape[-1])
        copies = [pltpu.make_async_copy(obuf.at[t], o_ref.at[:, t, :], sems.at[t]) for t in range(T)]
        for cp in copies:
            cp.start()
        for cp in copies:
            cp.wait()


def _ffn_small_part(R, layer, w, *, sample_out=None):
    D = D_MODEL
    F, NB = FF_CHUNK, STREAM_BUFFERS
    n_layers = w["norm_ffn_pre"].shape[0]
    nchunk = D_FF // F
    hbm = pl.BlockSpec(memory_space=pl.ANY)
    if sample_out is None:
        o_spec, o_shape, staging = pl.BlockSpec((R, D), lambda i: (0, 0)), jax.ShapeDtypeStruct((R, D), F32), []
    else:
        _, S, T = sample_out
        o_spec, o_shape = hbm, jax.ShapeDtypeStruct((S, T, D), F32)
        staging = [pltpu.VMEM((T, S, D), F32), pltpu.SemaphoreType.DMA((T,))]
    return _Part(
        kernel=functools.partial(_ffn_stream_kernel, layer, nchunk, sample_out),
        args=[None, w["norm_ffn_pre"], w["norm_ffn_post"], w["ffn_w_gate"], w["ffn_w_up"], w["ffn_w_down"]],
        in_specs=[_resident((R, D)), _resident((n_layers, D)), _resident((n_layers, D)), hbm, hbm, hbm],
        out_specs=[o_spec, hbm, hbm, hbm],
        out_shape=[o_shape, jax.ShapeDtypeStruct((nchunk, D, F), BF16),
                   jax.ShapeDtypeStruct((nchunk, D, F), BF16), jax.ShapeDtypeStruct((D_FF, D), BF16)],
        scratch_shapes=[pltpu.VMEM((R, D), BF16),
                        pltpu.VMEM((NB, D, F), F32), pltpu.VMEM((NB, D, F), F32), pltpu.VMEM((NB, F, D), F32),
                        pltpu.VMEM((NB, D, F), BF16), pltpu.VMEM((NB, D, F), BF16), pltpu.VMEM((NB, F, D), BF16),
                        pltpu.SemaphoreType.DMA((3, NB)), pltpu.SemaphoreType.DMA((3, NB))] + staging)


def _ffn_kernel(layer, nblk, batch_major_out, x_ref, npre_ref, npost_ref, wg_ref, wu_ref, wd_ref, o_ref,
                xn_scr, act_scr, *out_staging):
    R = x_ref.shape[0]
    i = pl.program_id(0)
    slot = i % 2
    if batch_major_out:
        obuf, sems = out_staging
        S, TB = o_ref.shape[0], R // o_ref.shape[0]

        def copies(which, step):
            t0 = pl.multiple_of(step * TB, TB)
            return [pltpu.make_async_copy(obuf.at[which, :, s, :],
                                          o_ref.at[s, pl.ds(t0, TB), :], sems.at[which, s]) for s in range(S)]

        @pl.when(i >= 2)
        def _():
            for c in copies(slot, i - 2):
                c.wait()

    parts = [slice(k * R // FFN_ROW_PARTS, (k + 1) * R // FFN_ROW_PARTS) for k in range(FFN_ROW_PARTS)]
    for rows in parts:
        xn_scr[rows, :] = _rms(x_ref[rows, :], npre_ref[layer:layer + 1, :]).astype(BF16)
    for rows in parts:
        xn = xn_scr[rows, :]
        for c in range(D_FF // FF_CHUNK):
            cols = slice(c * FF_CHUNK, (c + 1) * FF_CHUNK)
            act_scr[rows, cols] = _swiglu(_dot(xn, wg_ref[c]), _dot(xn, wu_ref[c]))
    ys = [_dot(act_scr[rows, :], wd_ref[...]) for rows in parts]
    for rows, y in zip(parts, ys):
        out = x_ref[rows, :] + _rms(y, npost_ref[layer:layer + 1, :])
        if batch_major_out:
            obuf[slot, rows.start // S:rows.stop // S] = out.reshape(-1, S, out.shape[-1])
        else:
            o_ref[rows, :] = out

    if batch_major_out:
        for c in copies(slot, i):
            c.start()

        @pl.when(i == nblk - 1)
        def _():
            if nblk >= 2:
                for c in copies(1 - slot, i - 1):
                    c.wait()
            for c in copies(slot, i):
                c.wait()


def _ffn_layer(x, layer, w, bf16, *, R, batch_major_seqs=None):
    N, D = x.shape
    n_layers = w["norm_ffn_pre"].shape[0]
    nblk = N // R
    row_spec = pl.BlockSpec((R, D), lambda i: (i, 0))
    if batch_major_seqs is None:
        out_spec, out_shape, staging = row_spec, jax.ShapeDtypeStruct((N, D), F32), []
    else:
        S = batch_major_seqs
        out_spec = pl.BlockSpec(memory_space=pl.ANY)
        out_shape = jax.ShapeDtypeStruct((S, N // S, D), F32)
        staging = [pltpu.VMEM((2, R // S, S, D), F32), pltpu.SemaphoreType.DMA((2, S))]
    return pl.pallas_call(
        functools.partial(_ffn_kernel, layer, nblk, batch_major_seqs is not None),
        grid=(nblk,),
        in_specs=[row_spec, _resident((n_layers, D)), _resident((n_layers, D)),
                  _resident(bf16["wg"].shape), _resident(bf16["wu"].shape), _resident((D_FF, D))],
        out_specs=out_spec,
        out_shape=out_shape,
        scratch_shapes=[pltpu.VMEM((R, D), BF16), pltpu.VMEM((R, D_FF), BF16)] + staging,
        compiler_params=_params(),
        name="swiglu_ffn",
    )(x, w["norm_ffn_pre"], w["norm_ffn_post"], bf16["wg"], bf16["wu"], bf16["wd"])


def _to_time_major(a):
    S, K, D = a.shape
    return jnp.swapaxes(a, 0, 1).reshape(K * S, D)


def _from_time_major(a, S):
    KS, D = a.shape
    return jnp.swapaxes(a.reshape(KS // S, S, D), 0, 1)


def kernel(x_prompt, x_sample, state_rglru_conv, state_rglru_h, state_sconv, meta_tokens, norm_mix_pre, norm_mix_post, norm_ffn_pre, norm_ffn_post, rg_w_in, rg_conv_w, rg_conv_b, rg_gate_a_w, rg_gate_a_b, rg_gate_x_w, rg_gate_x_b, rg_lambda, rg_w_out, sc_w_in, sc_conv_w, sc_w_out, ffn_w_gate, ffn_w_up, ffn_w_down):
    D = D_MODEL
    depth = norm_mix_pre.shape[0]
    batch, seq, _ = x_prompt.shape
    dec_batch, dec_seq, _ = x_sample.shape
    w = dict(norm_mix_pre=norm_mix_pre, norm_mix_post=norm_mix_post, norm_ffn_pre=norm_ffn_pre,
             norm_ffn_post=norm_ffn_post, rg_w_in=rg_w_in, rg_conv_w=rg_conv_w, rg_conv_b=rg_conv_b,
             rg_gate_a_w=rg_gate_a_w, rg_gate_a_b=rg_gate_a_b, rg_gate_x_w=rg_gate_x_w, rg_gate_x_b=rg_gate_x_b,
             rg_lambda=rg_lambda, rg_w_out=rg_w_out, sc_w_in=sc_w_in, sc_conv_w=sc_conv_w, sc_w_out=sc_w_out,
             ffn_w_gate=ffn_w_gate, ffn_w_up=ffn_w_up, ffn_w_down=ffn_w_down)

    x = jnp.broadcast_to(meta_tokens[:, None, :], (N_META, batch, D)).reshape(N_META * batch, D)
    small = ((batch, N_META), (dec_batch, dec_seq))
    small_rows = N_META * batch + dec_batch * dec_seq
    mixer_bf16, ffn_bf16 = [], []
    rg_conv_s, rg_h_s, sc_s = [], [], []
    for i in range(depth):
        j = i // 2
        sample_out = (N_META * batch, dec_batch, dec_seq) if i == depth - 1 else None
        ffn = _ffn_small_part(small_rows, i, w, sample_out=sample_out)
        if i % 2 == 0:
            conv0 = [jnp.zeros(((CONV_A - 1) * batch, D), F32), _to_time_major(state_rglru_conv[j])]
            h0 = [jnp.zeros((batch, D), F32), state_rglru_h[j]]
            mixer = _rglru_small_part(x, conv0, h0, i, j, w, seqs_and_steps=small,
                                      x_sample=x_sample if i == 0 else None)
            (cs0, cs1, hs0, hs1, *wb), (x, *fb) = _run_small_stage(mixer, ffn, small_rows, "rglru_ffn_small")
            rg_conv_s.append([cs0, cs1])
            rg_h_s.append([hs0, hs1])
            mixer_bf16.append(dict(zip(("w_xr", "w_gate", "gw", "w_out"), wb)))
        else:
            conv0 = [jnp.zeros(((CONV_B - 1) * batch, D), F32), _to_time_major(state_sconv[j])]
            mixer = _sconv_small_part(x, conv0, i, j, w, seqs_and_steps=small)
            (cs0, cs1, *wb), (x, *fb) = _run_small_stage(mixer, ffn, small_rows, "sconv_ffn_small")
            sc_s.append([cs0, cs1])
            mixer_bf16.append(dict(zip(("w_bg", "w_cg", "w_v", "w_out"), wb)))
        ffn_bf16.append(dict(zip(("wg", "wu", "wd"), fb)))
    y_sample = x

    x = x_prompt
    rg_conv_p, rg_h_p, sc_p = [], [], []
    for i in range(depth):
        j = i // 2
        if i % 2 == 0:
            x, cb, hT = _rglru_layer(x, rg_conv_s[j][0], rg_h_s[j][0], i, j, w, mixer_bf16[i], S=batch, TB=PROMPT_TB)
            rg_conv_p.append(cb)
            rg_h_p.append(hT)
        else:
            x, cb = _sconv_layer(x, sc_s[j][0], i, j, w, mixer_bf16[i], S=batch, TB=PROMPT_TB)
            sc_p.append(cb)
        x = _ffn_layer(x, i, w, ffn_bf16[i], R=batch * PROMPT_TB,
                       batch_major_seqs=batch if i == depth - 1 else None)
    y_prompt = x

    return (y_prompt, y_sample,
            jnp.stack([_from_time_major(c, batch) for c in rg_conv_p]), jnp.stack(rg_h_p),
            jnp.stack([_from_time_major(c, batch) for c in sc_p]),
            jnp.stack([_from_time_major(c[1], dec_batch) for c in rg_conv_s]), jnp.stack([h[1] for h in rg_h_s]),
            jnp.stack([_from_time_major(c[1], dec_batch) for c in sc_s]))
```

```python
import functools
from typing import NamedTuple

import jax
import jax.numpy as jnp
from jax import lax
from jax.experimental import pallas as pl
from jax.experimental.pallas import tpu as pltpu

D_MODEL = 1024
D_FF = 2816
N_META = 16
COL_BLOCK = 256
N_COL_BLOCKS = D_MODEL // COL_BLOCK
CONV_A = 4
CONV_B = 3
RG_C = 8.0
EPS = 1e-6

SUBLANES = 8
FF_CHUNK = 256
STREAM_BUFFERS = 3
OUT_SLOTS = 2
PROMPT_TB = 128
FFN_ROW_PARTS = 4
MIXER_ROW_PARTS = 4
VMEM_LIMIT_BYTES = 56 * 1024 * 1024
SMALL_STAGE_VMEM_LIMIT_BYTES = 60 * 1024 * 1024

F32 = jnp.float32
BF16 = jnp.bfloat16


class _Seg(NamedTuple):
    S: int
    TB: int
    row0: int
    conv0: int
    h0: int

    @property
    def rows(self):
        return self.S * self.TB


def _segments(seqs_and_steps, taps):
    segs, row0, conv0, h0 = [], 0, 0, 0
    for S, TB in seqs_and_steps:
        segs.append(_Seg(S, TB, row0, conv0, h0))
        row0 += S * TB
        conv0 += (taps - 1) * S + S * TB
        h0 += S
    return tuple(segs), row0, conv0, h0


def _rms(x, w):
    ms = jnp.mean(x * x, axis=-1, keepdims=True)
    return x * lax.rsqrt(ms + EPS) * w


def _dot(a, b):
    return jnp.dot(a, b, preferred_element_type=F32)


def _gelu_tanh(x):
    c = 0.7978845608028654
    hx = 0.5 * x
    return hx + hx * jnp.tanh(x * (c + (c * 0.044715) * (x * x)))


def _resident(shape):
    zeros = (0,) * len(shape)
    return pl.BlockSpec(shape, lambda i: zeros, pipeline_mode=pl.Buffered(1))


def _params(vmem_limit_bytes=VMEM_LIMIT_BYTES):
    return pltpu.CompilerParams(dimension_semantics=("arbitrary",), vmem_limit_bytes=vmem_limit_bytes)


def _start_all(copies):
    for cp in copies:
        cp.start()


def _wait_all(copies):
    for cp in copies:
        cp.wait()


def _rglru_branch_in(seg, xn, w_xr, region):
    P = (CONV_A - 1) * seg.S
    region[seg.conv0 + P:seg.conv0 + P + seg.rows, :] = _dot(xn, w_xr)


def _rglru_conv(seg, cw, cb, region, carry_to):
    S, R = seg.S, seg.rows
    P = (CONV_A - 1) * S
    c0 = seg.conv0
    xc = region[c0:c0 + R, :] * cw[0:1]
    for k in range(1, CONV_A):
        xc = xc + region[c0 + k * S:c0 + k * S + R, :] * cw[k:k + 1]
    xc = xc + cb
    if carry_to is not None:
        region[carry_to:carry_to + P, :] = region[c0 + R:c0 + R + P, :]
    return xc


def _rglru_gate_math(res, xc, half_gab, half_gxb, lam):
    half_c_sp = (-0.5 * RG_C) * jax.nn.softplus(-lam)
    tr = jnp.tanh(res[:, 0:COL_BLOCK] + half_gab)
    log_a = half_c_sp * tr + half_c_sp
    ig = 0.5 * jnp.tanh(res[:, COL_BLOCK:2 * COL_BLOCK] + half_gxb) + 0.5
    a = jnp.exp(log_a)
    m2 = jnp.tanh(log_a) * (-1.0 - a * a)
    u = jnp.where(m2 > 0.0, m2 * lax.rsqrt(m2), 0.0) * (ig * xc)
    return a, u


def _rglru_scan(seg, a, u, g, h_read, h_write, h_row0):
    S, TB = seg.S, seg.TB
    groups = S // SUBLANES
    pieces = [None] * (TB * groups)
    for c in range(groups):
        hrows = slice(h_row0 + c * SUBLANES, h_row0 + (c + 1) * SUBLANES)
        h = h_read[hrows, :]
        for t in range(TB):
            r = t * S + c * SUBLANES
            h = a[r:r + SUBLANES] * h + u[r:r + SUBLANES]
            pieces[t * groups + c] = h * g[r:r + SUBLANES]
        h_write[hrows, :] = h
    return jnp.concatenate(pieces, axis=0)


class _Part(NamedTuple):
    kernel: object
    args: list
    in_specs: list
    out_specs: list
    out_shape: list
    scratch_shapes: list


def _run_small_stage(mixer, ffn, rows, name):
    n_min, n_fin = len(mixer.args), len(ffn.args) - 1
    n_mout, n_fout = len(mixer.out_shape) - 1, len(ffn.out_shape)
    n_mscr, n_fscr = len(mixer.scratch_shapes), len(ffn.scratch_shapes)

    def kernel(*refs):
        refs = list(refs)
        take = lambda k: [refs.pop(0) for _ in range(k)]
        m_in, f_in, m_out, f_out, m_scr, f_scr = (take(k) for k in (n_min, n_fin, n_mout, n_fout, n_mscr, n_fscr))
        (x_mid,) = refs
        f_refs = [x_mid] + f_in + f_out + f_scr
        mixer.kernel(lambda: ffn.kernel(*f_refs, prefetch_only=True), *m_in, x_mid, *m_out, *m_scr)
        ffn.kernel(*f_refs, prefetched=True)

    outs = pl.pallas_call(
        kernel,
        grid=(1,),
        in_specs=mixer.in_specs + ffn.in_specs[1:],
        out_specs=mixer.out_specs[1:] + ffn.out_specs,
        out_shape=mixer.out_shape[1:] + ffn.out_shape,
        scratch_shapes=mixer.scratch_shapes + ffn.scratch_shapes + [pltpu.VMEM((rows, D_MODEL), F32)],
        compiler_params=_params(SMALL_STAGE_VMEM_LIMIT_BYTES),
        name=name,
    )(*mixer.args, *ffn.args[1:])
    return list(outs[:n_mout]), list(outs[n_mout:])


def _rglru_small_kernel(segs, sample_in, layer, j, after_fetch_issue, x_ref, *refs):
    D, C, NB = D_MODEL, COL_BLOCK, N_COL_BLOCKS
    if sample_in is not None:
        xs_hbm, refs = refs[0], refs[1:]
    n = len(segs)
    conv_in, h_in = refs[0:n], refs[n:2 * n]
    (npre_ref, npost_ref, cw_ref, cb_ref, gab_ref, gxb_ref, lam_ref,
     win_hbm, gaw_hbm, gxw_hbm, wout_hbm) = refs[2 * n:2 * n + 11]
    o_ref = refs[2 * n + 11]
    conv_out, h_out = refs[2 * n + 12:3 * n + 12], refs[3 * n + 12:4 * n + 12]
    wxr_b_hbm, wgate_b_hbm, gw_b_hbm, wout_b_hbm = refs[4 * n + 12:4 * n + 16]
    (xn_scr, xr_scr, col_f, sq_f, row_f, col_h, gw_h, row_h, in_sems, out_sems,
     *in_staging) = refs[4 * n + 16:]

    def fetch(b):
        cols = pl.ds(b * C, C)
        return [pltpu.make_async_copy(win_hbm.at[j, :, pl.ds(D + b * C, C)], col_f.at[b, 0], in_sems.at[b, 0]),
                pltpu.make_async_copy(win_hbm.at[j, :, cols], col_f.at[b, 1], in_sems.at[b, 1]),
                pltpu.make_async_copy(gaw_hbm.at[j, b], sq_f.at[b, 0], in_sems.at[b, 2]),
                pltpu.make_async_copy(gxw_hbm.at[j, b], sq_f.at[b, 1], in_sems.at[b, 3]),
                pltpu.make_async_copy(wout_hbm.at[j, cols, :], row_f.at[b], in_sems.at[b, 4])]

    def emit(b):
        k = b % OUT_SLOTS
        return [pltpu.make_async_copy(col_h.at[k, 0], wxr_b_hbm.at[b], out_sems.at[k, 0]),
                pltpu.make_async_copy(col_h.at[k, 1], wgate_b_hbm.at[b], out_sems.at[k, 1]),
                pltpu.make_async_copy(gw_h.at[k], gw_b_hbm.at[b], out_sems.at[k, 2]),
                pltpu.make_async_copy(row_h.at[k], wout_b_hbm.at[pl.ds(b * C, C), :], out_sems.at[k, 3])]

    if sample_in is None:
        read_x = lambda: x_ref[...]
    else:
        S, T = sample_in
        xs_scr, sems = in_staging
        x_copies = [pltpu.make_async_copy(xs_hbm.at[:, t, :], xs_scr.at[t], sems.at[t]) for t in range(T)]
        _start_all(x_copies)
        read_x = lambda: jnp.concatenate([x_ref[...], xs_scr[...].reshape(T * S, D)], axis=0)
    for b in range(NB):
        _start_all(fetch(b))
    after_fetch_issue()
    if sample_in is not None:
        _wait_all(x_copies)
    xn_scr[...] = _rms(read_x(), npre_ref[layer:layer + 1, :]).astype(BF16)
    xn = xn_scr[...]

    rows_of = lambda seg: slice(seg.row0, seg.row0 + seg.rows)
    y = None
    for b in range(NB):
        cols = slice(b * C, (b + 1) * C)
        _wait_all(fetch(b))
        w_xr, w_gate = col_f[b, 0].astype(BF16), col_f[b, 1].astype(BF16)
        gw = (0.5 * jnp.concatenate([sq_f[b, 0], sq_f[b, 1]], axis=1)).astype(BF16)
        w_out_rows = row_f[b].astype(BF16)
        k = b % OUT_SLOTS
        if b >= OUT_SLOTS:
            _wait_all(emit(b - OUT_SLOTS))
        col_h[k, 0], col_h[k, 1], gw_h[k], row_h[k] = w_xr, w_gate, gw, w_out_rows
        _start_all(emit(b))

        xr = _dot(xn, w_xr)
        g = _gelu_tanh(_dot(xn, w_gate))
        xcs = []
        for seg, c_in, c_out in zip(segs, conv_in, conv_out):
            P = (CONV_A - 1) * seg.S
            xr_scr[seg.conv0:seg.conv0 + P, :] = c_in[:, cols]
            xr_scr[seg.conv0 + P:seg.conv0 + P + seg.rows, :] = xr[rows_of(seg)]
            xcs.append(_rglru_conv(seg, cw_ref[:, cols], cb_ref[j:j + 1, cols], xr_scr, seg.conv0))
            c_out[:, cols] = xr_scr[seg.conv0:seg.conv0 + P, :]
        xc = jnp.concatenate(xcs, axis=0)
        a, u = _rglru_gate_math(_dot(xc.astype(BF16), gw), xc, 0.5 * gab_ref[b:b + 1, :],
                                0.5 * gxb_ref[b:b + 1, :], lam_ref[j:j + 1, cols])
        hs = [_rglru_scan(seg, a[rows_of(seg)], u[rows_of(seg)], g[rows_of(seg)],
                          hi.at[:, cols], ho.at[:, cols], 0)
              for seg, hi, ho in zip(segs, h_in, h_out)]
        part = _dot(jnp.concatenate(hs, axis=0).astype(BF16), w_out_rows)
        y = part if y is None else y + part

    for b in range(max(NB - OUT_SLOTS, 0), NB):
        _wait_all(emit(b))
    o_ref[...] = read_x() + _rms(y, npost_ref[layer:layer + 1, :])


def _rglru_small_part(x, conv_in, h_in, layer, j, w, *, seqs_and_steps, x_sample=None):
    D, C, NB = D_MODEL, COL_BLOCK, N_COL_BLOCKS
    segs, R, conv_rows, _ = _segments(seqs_and_steps, CONV_A)
    n_layers, n_a = w["norm_mix_pre"].shape[0], w["rg_conv_b"].shape[0]
    hbm = pl.BlockSpec(memory_space=pl.ANY)
    layer_of = lambda arr: pl.BlockSpec((None,) + arr.shape[1:], lambda i: (j,) + (0,) * (arr.ndim - 1),
                                        pipeline_mode=pl.Buffered(1))
    if x_sample is None:
        x_args, x_specs, sample_in, staging = [x], [_resident(x.shape)], None, []
    else:
        S, T, _ = x_sample.shape
        x_args, x_specs, sample_in = [x, x_sample], [_resident(x.shape), hbm], (S, T)
        staging = [pltpu.VMEM((T, S, D), F32), pltpu.SemaphoreType.DMA((T,))]
    states = conv_in + h_in
    return _Part(
        kernel=functools.partial(_rglru_small_kernel, segs, sample_in, layer, j),
        args=x_args + states + [w["norm_mix_pre"], w["norm_mix_post"], w["rg_conv_w"], w["rg_conv_b"],
                                w["rg_gate_a_b"], w["rg_gate_x_b"], w["rg_lambda"], w["rg_w_in"],
                                w["rg_gate_a_w"], w["rg_gate_x_w"], w["rg_w_out"]],
        in_specs=x_specs + [_resident(s.shape) for s in states]
                 + [_resident((n_layers, D)), _resident((n_layers, D)), layer_of(w["rg_conv_w"]),
                    _resident((n_a, D)), layer_of(w["rg_gate_a_b"]), layer_of(w["rg_gate_x_b"]),
                    _resident((n_a, D)), hbm, hbm, hbm, hbm],
        out_specs=[pl.BlockSpec((R, D), lambda i: (0, 0))]
                  + [pl.BlockSpec(s.shape, lambda i: (0, 0)) for s in states] + [hbm] * 4,
        out_shape=[jax.ShapeDtypeStruct((R, D), F32)] + [jax.ShapeDtypeStruct(s.shape, F32) for s in states]
                  + [jax.ShapeDtypeStruct((NB, D, C), BF16), jax.ShapeDtypeStruct((NB, D, C), BF16),
                     jax.ShapeDtypeStruct((NB, C, 2 * C), BF16), jax.ShapeDtypeStruct((D, D), BF16)],
        scratch_shapes=[pltpu.VMEM((R, D), BF16),
                        pltpu.VMEM((conv_rows, C), F32),
                        pltpu.VMEM((NB, 2, D, C), F32),
                        pltpu.VMEM((NB, 2, C, C), F32),
                        pltpu.VMEM((NB, C, D), F32),
                        pltpu.VMEM((OUT_SLOTS, 2, D, C), BF16), pltpu.VMEM((OUT_SLOTS, C, 2 * C), BF16),
                        pltpu.VMEM((OUT_SLOTS, C, D), BF16),
                        pltpu.SemaphoreType.DMA((NB, 5)), pltpu.SemaphoreType.DMA((OUT_SLOTS, 4))] + staging)


def _rglru_kernel(seg, nblk, layer, j, batch_major_in,
                  x_ref, conv_in_ref, h_in_ref, npre_ref, npost_ref, wxr_ref, wgate_ref, cw_ref, cb_ref,
                  gw_ref, gab_ref, gxb_ref, lam_ref, wout_ref,
                  o_ref, conv_out_ref, h_out_ref, xn_scr, xr_scr, h_scr, *in_staging):
    S, TB, R = seg.S, seg.TB, seg.rows
    P = (CONV_A - 1) * S
    i = pl.program_id(0)
    slot = i % 2
    blocks = [slice(b * COL_BLOCK, (b + 1) * COL_BLOCK) for b in range(N_COL_BLOCKS)]

    if batch_major_in:
        xbuf, sems = in_staging

        def copies(which, step):
            t0 = pl.multiple_of(step * TB, TB)
            return [pltpu.make_async_copy(x_ref.at[s, pl.ds(t0, TB), :], xbuf.at[which, :, s, :],
                                          sems.at[which, s]) for s in range(S)]

    @pl.when(i == 0)
    def _():
        for b, cols in enumerate(blocks):
            xr_scr[b, 0:P, :] = conv_in_ref[:, cols]
            h_scr[b] = h_in_ref[:, cols]
        if batch_major_in:
            for c in copies(0, 0):
                c.start()

    if batch_major_in:
        @pl.when(i + 1 < nblk)
        def _():
            for c in copies(1 - slot, i + 1):
                c.start()

        for c in copies(slot, i):
            c.wait()
        read_x = lambda: xbuf[slot].reshape(R, D_MODEL)
    else:
        read_x = lambda: x_ref[...]

    xn_scr[...] = _rms(read_x(), npre_ref[layer:layer + 1, :]).astype(BF16)
    xn = xn_scr[...]

    gate_pre, xc, res = {}, {}, {}

    def input_matmuls(b):
        _rglru_branch_in(seg, xn, wxr_ref[b], xr_scr.at[b])
        gate_pre[b] = _dot(xn, wgate_ref[b])

    def conv_and_gate_matmul(b):
        xc[b] = _rglru_conv(seg, cw_ref[:, blocks[b]], cb_ref[j:j + 1, blocks[b]], xr_scr.at[b], 0)
        res[b] = _dot(xc[b].astype(BF16), gw_ref[b])

    y = None
    input_matmuls(0)
    conv_and_gate_matmul(0)
    input_matmuls(1)
    for b, cols in enumerate(blocks):
        if b + 1 < N_COL_BLOCKS:
            conv_and_gate_matmul(b + 1)
        if b + 2 < N_COL_BLOCKS:
            input_matmuls(b + 2)
        g = _gelu_tanh(gate_pre.pop(b))
        a, u = _rglru_gate_math(res.pop(b), xc.pop(b), 0.5 * gab_ref[b:b + 1, :], 0.5 * gxb_ref[b:b + 1, :],
                                lam_ref[j:j + 1, cols])
        hs = _rglru_scan(seg, a, u, g, h_scr.at[b], h_scr.at[b], 0)
        part = _dot(hs.astype(BF16), wout_ref[cols, :])
        y = part if y is None else y + part

    o_ref[...] = read_x() + _rms(y, npost_ref[layer:layer + 1, :])

    @pl.when(i == nblk - 1)
    def _():
        for b, cols in enumerate(blocks):
            conv_out_ref[:, cols] = xr_scr[b, 0:P, :]
            h_out_ref[:, cols] = h_scr[b]


def _rglru_layer(x, conv_in, h_in, layer, j, w, bf16, *, S, TB):
    D = D_MODEL
    batch_major_in = x.ndim == 3
    (seg,), R, conv_rows, _ = _segments(((S, TB),), CONV_A)
    nblk = x.size // D // R
    P = (CONV_A - 1) * S
    n_layers, n_a = w["norm_mix_pre"].shape[0], w["rg_conv_b"].shape[0]
    row_spec = pl.BlockSpec((R, D), lambda i: (i, 0))
    if batch_major_in:
        x_spec = pl.BlockSpec(memory_space=pl.ANY)
        staging = [pltpu.VMEM((2, TB, S, D), F32), pltpu.SemaphoreType.DMA((2, S))]
    else:
        x_spec, staging = row_spec, []
    layer_of = lambda arr: pl.BlockSpec((None,) + arr.shape[1:], lambda i: (j,) + (0,) * (arr.ndim - 1),
                                        pipeline_mode=pl.Buffered(1))
    return pl.pallas_call(
        functools.partial(_rglru_kernel, seg, nblk, layer, j, batch_major_in),
        grid=(nblk,),
        in_specs=[x_spec, _resident((P, D)), _resident((S, D)),
                  _resident((n_layers, D)), _resident((n_layers, D)),
                  _resident(bf16["w_xr"].shape), _resident(bf16["w_gate"].shape),
                  layer_of(w["rg_conv_w"]), _resident((n_a, D)), _resident(bf16["gw"].shape),
                  layer_of(w["rg_gate_a_b"]), layer_of(w["rg_gate_x_b"]), _resident((n_a, D)),
                  _resident((D, D))],
        out_specs=[row_spec, pl.BlockSpec((P, D), lambda i: (0, 0)), pl.BlockSpec((S, D), lambda i: (0, 0))],
        out_shape=[jax.ShapeDtypeStruct((nblk * R, D), F32), jax.ShapeDtypeStruct((P, D), F32),
                   jax.ShapeDtypeStruct((S, D), F32)],
        scratch_shapes=[pltpu.VMEM((R, D), BF16),
                        pltpu.VMEM((N_COL_BLOCKS, conv_rows, COL_BLOCK), F32),
                        pltpu.VMEM((N_COL_BLOCKS, S, COL_BLOCK), F32)]
                       + staging,
        compiler_params=_params(),
        name="rglru_mixer",
    )(x, conv_in, h_in, w["norm_mix_pre"], w["norm_mix_post"], bf16["w_xr"], bf16["w_gate"], w["rg_conv_w"],
      w["rg_conv_b"], bf16["gw"], w["rg_gate_a_b"], w["rg_gate_x_b"], w["rg_lambda"], bf16["w_out"])


def _sconv_conv(seg, cv, cw, region, rows=None):
    S = seg.S
    P = (CONV_B - 1) * S
    rows = rows or slice(0, seg.rows)
    c0, n = seg.conv0 + rows.start, rows.stop - rows.start
    region[c0 + P:c0 + P + n, :] = cv
    conv = region[c0:c0 + n, :] * cw[0:1]
    for k in range(1, CONV_B):
        conv = conv + region[c0 + k * S:c0 + k * S + n, :] * cw[k:k + 1]
    if rows.stop == seg.rows:
        region[seg.conv0:seg.conv0 + P, :] = region[seg.conv0 + seg.rows:seg.conv0 + seg.rows + P, :]
    return conv


def _sconv_small_kernel(segs, layer, j, after_fetch_issue, x_ref, *refs):
    D, C, NB = D_MODEL, COL_BLOCK, N_COL_BLOCKS
    n = len(segs)
    conv_in = refs[0:n]
    npre_ref, npost_ref, cw_ref, win_hbm, wout_hbm = refs[n:n + 5]
    o_ref = refs[n + 5]
    conv_out = refs[n + 6:2 * n + 6]
    wbg_b_hbm, wcg_b_hbm, wv_b_hbm, wout_b_hbm = refs[2 * n + 6:2 * n + 10]
    xn_scr, cv_scr, col_f, row_f, col_h, row_h, in_sems, out_sems = refs[2 * n + 10:]

    def fetch(b):
        return ([pltpu.make_async_copy(win_hbm.at[j, :, pl.ds(k * D + b * C, C)], col_f.at[b, k], in_sems.at[b, k])
                 for k in range(3)]
                + [pltpu.make_async_copy(wout_hbm.at[j, pl.ds(b * C, C), :], row_f.at[b], in_sems.at[b, 3])])

    def emit(b):
        s = b % OUT_SLOTS
        return ([pltpu.make_async_copy(col_h.at[s, k], dst.at[b], out_sems.at[s, k])
                 for k, dst in enumerate((wbg_b_hbm, wcg_b_hbm, wv_b_hbm))]
                + [pltpu.make_async_copy(row_h.at[s], wout_b_hbm.at[pl.ds(b * C, C), :], out_sems.at[s, 3])])

    for b in range(NB):
        _start_all(fetch(b))
    after_fetch_issue()
    xn_scr[...] = _rms(x_ref[...], npre_ref[layer:layer + 1, :]).astype(BF16)
    xn = xn_scr[...]

    y = None
    for b in range(NB):
        cols = slice(b * C, (b + 1) * C)
        _wait_all(fetch(b))
        w_bg, w_cg, w_v = (col_f[b, k].astype(BF16) for k in range(3))
        w_out_rows = row_f[b].astype(BF16)
        s = b % OUT_SLOTS
        if b >= OUT_SLOTS:
            _wait_all(emit(b - OUT_SLOTS))
        col_h[s, 0], col_h[s, 1], col_h[s, 2], row_h[s] = w_bg, w_cg, w_v, w_out_rows
        _start_all(emit(b))

        cv = _dot(xn, w_cg) * _dot(xn, w_v)
        convs = []
        for seg, c_in, c_out in zip(segs, conv_in, conv_out):
            P = (CONV_B - 1) * seg.S
            cv_scr[seg.conv0:seg.conv0 + P, :] = c_in[:, cols]
            convs.append(_sconv_conv(seg, cv[seg.row0:seg.row0 + seg.rows], cw_ref[:, cols], cv_scr))
            c_out[:, cols] = cv_scr[seg.conv0:seg.conv0 + P, :]
        m = (_dot(xn, w_bg) * jnp.concatenate(convs, axis=0)).astype(BF16)
        part = _dot(m, w_out_rows)
        y = part if y is None else y + part

    for b in range(max(NB - OUT_SLOTS, 0), NB):
        _wait_all(emit(b))
    o_ref[...] = x_ref[...] + _rms(y, npost_ref[layer:layer + 1, :])


def _sconv_small_part(x, conv_in, layer, j, w, *, seqs_and_steps):
    R, D = x.shape
    C, NB = COL_BLOCK, N_COL_BLOCKS
    segs, _, conv_rows, _ = _segments(seqs_and_steps, CONV_B)
    n_layers = w["norm_mix_pre"].shape[0]
    hbm = pl.BlockSpec(memory_space=pl.ANY)
    return _Part(
        kernel=functools.partial(_sconv_small_kernel, segs, layer, j),
        args=[x] + conv_in + [w["norm_mix_pre"], w["norm_mix_post"], w["sc_conv_w"], w["sc_w_in"], w["sc_w_out"]],
        in_specs=[_resident((R, D))] + [_resident(c.shape) for c in conv_in]
                 + [_resident((n_layers, D)), _resident((n_layers, D)),
                    pl.BlockSpec((None, CONV_B, D), lambda i: (j, 0, 0), pipeline_mode=pl.Buffered(1)), hbm, hbm],
        out_specs=[pl.BlockSpec((R, D), lambda i: (0, 0))]
                  + [pl.BlockSpec(c.shape, lambda i: (0, 0)) for c in conv_in] + [hbm] * 4,
        out_shape=[jax.ShapeDtypeStruct((R, D), F32)] + [jax.ShapeDtypeStruct(c.shape, F32) for c in conv_in]
                  + [jax.ShapeDtypeStruct((NB, D, C), BF16)] * 3 + [jax.ShapeDtypeStruct((D, D), BF16)],
        scratch_shapes=[pltpu.VMEM((R, D), BF16),
                        pltpu.VMEM((conv_rows, C), F32),
                        pltpu.VMEM((NB, 3, D, C), F32),
                        pltpu.VMEM((NB, C, D), F32),
                        pltpu.VMEM((OUT_SLOTS, 3, D, C), BF16), pltpu.VMEM((OUT_SLOTS, C, D), BF16),
                        pltpu.SemaphoreType.DMA((NB, 4)), pltpu.SemaphoreType.DMA((OUT_SLOTS, 4))])


def _sconv_kernel(seg, nblk, layer, x_ref, conv_in_ref, npre_ref, npost_ref, wbg_ref, wcg_ref, wv_ref,
                  cw_ref, wout_ref, o_ref, conv_out_ref, xn_scr, cv_scr, m_scr):
    P = (CONV_B - 1) * seg.S
    i = pl.program_id(0)
    blocks = [slice(b * COL_BLOCK, (b + 1) * COL_BLOCK) for b in range(N_COL_BLOCKS)]

    @pl.when(i == 0)
    def _():
        for b, cols in enumerate(blocks):
            cv_scr[b, 0:P, :] = conv_in_ref[:, cols]

    R = seg.rows
    parts = [slice(k * R // MIXER_ROW_PARTS, (k + 1) * R // MIXER_ROW_PARTS) for k in range(MIXER_ROW_PARTS)]
    for rows in parts:
        xn_scr[rows, :] = _rms(x_ref[rows, :], npre_ref[layer:layer + 1, :]).astype(BF16)
    for rows in parts:
        xn = xn_scr[rows, :]
        for b, cols in enumerate(blocks):
            cv = _dot(xn, wcg_ref[b]) * _dot(xn, wv_ref[b])
            conv = _sconv_conv(seg, cv, cw_ref[:, cols], cv_scr.at[b], rows)
            m_scr[rows, cols] = (_dot(xn, wbg_ref[b]) * conv).astype(BF16)
    ys = [_dot(m_scr[rows, :], wout_ref[...]) for rows in parts]
    for rows, y in zip(parts, ys):
        o_ref[rows, :] = x_ref[rows, :] + _rms(y, npost_ref[layer:layer + 1, :])

    @pl.when(i == nblk - 1)
    def _():
        for b, cols in enumerate(blocks):
            conv_out_ref[:, cols] = cv_scr[b, 0:P, :]


def _sconv_layer(x, conv_in, layer, j, w, bf16, *, S, TB):
    N, D = x.shape
    (seg,), R, conv_rows, _ = _segments(((S, TB),), CONV_B)
    nblk = N // R
    P = (CONV_B - 1) * S
    n_layers = w["norm_mix_pre"].shape[0]
    row_spec = pl.BlockSpec((R, D), lambda i: (i, 0))
    return pl.pallas_call(
        functools.partial(_sconv_kernel, seg, nblk, layer),
        grid=(nblk,),
        in_specs=[row_spec, _resident((P, D)), _resident((n_layers, D)), _resident((n_layers, D)),
                  _resident(bf16["w_bg"].shape), _resident(bf16["w_cg"].shape), _resident(bf16["w_v"].shape),
                  pl.BlockSpec((None, CONV_B, D), lambda i: (j, 0, 0), pipeline_mode=pl.Buffered(1)),
                  _resident((D, D))],
        out_specs=[row_spec, pl.BlockSpec((P, D), lambda i: (0, 0))],
        out_shape=[jax.ShapeDtypeStruct((N, D), F32), jax.ShapeDtypeStruct((P, D), F32)],
        scratch_shapes=[pltpu.VMEM((R, D), BF16),
                        pltpu.VMEM((N_COL_BLOCKS, conv_rows, COL_BLOCK), F32),
                        pltpu.VMEM((R, D), BF16)],
        compiler_params=_params(),
        name="sconv_mixer",
    )(x, conv_in, w["norm_mix_pre"], w["norm_mix_post"], bf16["w_bg"], bf16["w_cg"], bf16["w_v"],
      w["sc_conv_w"], bf16["w_out"])


def _swiglu(g, u):
    return (g * jax.nn.sigmoid(g) * u).astype(BF16)


def _ffn_stream_kernel(layer, nchunk, sample_out, x_ref, npre_ref, npost_ref, wg_hbm, wu_hbm, wd_hbm,
                       o_ref, wg_b_hbm, wu_b_hbm, wd_b_hbm,
                       xn_scr, wg_f, wu_f, wd_f, wg_h, wu_h, wd_h, in_sems, out_sems, *out_staging,
                       prefetch_only=False, prefetched=False):
    F, NB = FF_CHUNK, STREAM_BUFFERS

    def fetch(c):
        k, cols = c % NB, pl.ds(c * F, F)
        return [pltpu.make_async_copy(wg_hbm.at[layer, :, cols], wg_f.at[k], in_sems.at[0, k]),
                pltpu.make_async_copy(wu_hbm.at[layer, :, cols], wu_f.at[k], in_sems.at[1, k]),
                pltpu.make_async_copy(wd_hbm.at[layer, cols, :], wd_f.at[k], in_sems.at[2, k])]

    def emit(c):
        k = c % NB
        return [pltpu.make_async_copy(wg_h.at[k], wg_b_hbm.at[c], out_sems.at[0, k]),
                pltpu.make_async_copy(wu_h.at[k], wu_b_hbm.at[c], out_sems.at[1, k]),
                pltpu.make_async_copy(wd_h.at[k], wd_b_hbm.at[pl.ds(c * F, F), :], out_sems.at[2, k])]

    if not prefetched:
        for c in range(min(NB, nchunk)):
            _start_all(fetch(c))
    if prefetch_only:
        return
    xn_scr[...] = _rms(x_ref[...], npre_ref[layer:layer + 1, :]).astype(BF16)
    xn = xn_scr[...]

    y = None
    for c in range(nchunk):
        k = c % NB
        for cp in fetch(c):
            cp.wait()
        if c >= NB:
            for cp in emit(c - NB):
                cp.wait()
        wg, wu, wd = wg_f[k].astype(BF16), wu_f[k].astype(BF16), wd_f[k].astype(BF16)
        wg_h[k], wu_h[k], wd_h[k] = wg, wu, wd
        for cp in emit(c):
            cp.start()
        if c + NB < nchunk:
            for cp in fetch(c + NB):
                cp.start()
        part = _dot(_swiglu(_dot(xn, wg), _dot(xn, wu)), wd)
        y = part if y is None else y + part
    for c in range(max(nchunk - NB, 0), nchunk):
        for cp in emit(c):
            cp.wait()

    out = x_ref[...] + _rms(y, npost_ref[layer:layer + 1, :])
    if sample_out is None:
        o_ref[...] = out
    else:
        row0, S, T = sample_out
        obuf, sems = out_staging
        obuf[...] = out[row0:row0 + T * S].reshape(T, S, out.shape[-1])
        copies = [pltpu.make_async_copy(obuf.at[t], o_ref.at[:, t, :], sems.at[t]) for t in range(T)]
        for cp in copies:
            cp.start()
        for cp in copies:
            cp.wait()


def _ffn_small_part(R, layer, w, *, sample_out=None):
    D = D_MODEL
    F, NB = FF_CHUNK, STREAM_BUFFERS
    n_layers = w["norm_ffn_pre"].shape[0]
    nchunk = D_FF // F
    hbm = pl.BlockSpec(memory_space=pl.ANY)
    if sample_out is None:
        o_spec, o_shape, staging = pl.BlockSpec((R, D), lambda i: (0, 0)), jax.ShapeDtypeStruct((R, D), F32), []
    else:
        _, S, T = sample_out
        o_spec, o_shape = hbm, jax.ShapeDtypeStruct((S, T, D), F32)
        staging = [pltpu.VMEM((T, S, D), F32), pltpu.SemaphoreType.DMA((T,))]
    return _Part(
        kernel=functools.partial(_ffn_stream_kernel, layer, nchunk, sample_out),
        args=[None, w["norm_ffn_pre"], w["norm_ffn_post"], w["ffn_w_gate"], w["ffn_w_up"], w["ffn_w_down"]],
        in_specs=[_resident((R, D)), _resident((n_layers, D)), _resident((n_layers, D)), hbm, hbm, hbm],
        out_specs=[o_spec, hbm, hbm, hbm],
        out_shape=[o_shape, jax.ShapeDtypeStruct((nchunk, D, F), BF16),
                   jax.ShapeDtypeStruct((nchunk, D, F), BF16), jax.ShapeDtypeStruct((D_FF, D), BF16)],
        scratch_shapes=[pltpu.VMEM((R, D), BF16),
                        pltpu.VMEM((NB, D, F), F32), pltpu.VMEM((NB, D, F), F32), pltpu.VMEM((NB, F, D), F32),
                        pltpu.VMEM((NB, D, F), BF16), pltpu.VMEM((NB, D, F), BF16), pltpu.VMEM((NB, F, D), BF16),
                        pltpu.SemaphoreType.DMA((3, NB)), pltpu.SemaphoreType.DMA((3, NB))] + staging)


def _ffn_kernel(layer, nblk, batch_major_out, x_ref, npre_ref, npost_ref, wg_ref, wu_ref, wd_ref, o_ref,
                xn_scr, act_scr, *out_staging):
    R = x_ref.shape[0]
    i = pl.program_id(0)
    slot = i % 2
    if batch_major_out:
        obuf, sems = out_staging
        S, TB = o_ref.shape[0], R // o_ref.shape[0]

        def copies(which, step):
            t0 = pl.multiple_of(step * TB, TB)
            return [pltpu.make_async_copy(obuf.at[which, :, s, :],
                                          o_ref.at[s, pl.ds(t0, TB), :], sems.at[which, s]) for s in range(S)]

        @pl.when(i >= 2)
        def _():
            for c in copies(slot, i - 2):
                c.wait()

    parts = [slice(k * R // FFN_ROW_PARTS, (k + 1) * R // FFN_ROW_PARTS) for k in range(FFN_ROW_PARTS)]
    for rows in parts:
        xn_scr[rows, :] = _rms(x_ref[rows, :], npre_ref[layer:layer + 1, :]).astype(BF16)
    for rows in parts:
        xn = xn_scr[rows, :]
        for c in range(D_FF // FF_CHUNK):
            cols = slice(c * FF_CHUNK, (c + 1) * FF_CHUNK)
            act_scr[rows, cols] = _swiglu(_dot(xn, wg_ref[c]), _dot(xn, wu_ref[c]))
    ys = [_dot(act_scr[rows, :], wd_ref[...]) for rows in parts]
    for rows, y in zip(parts, ys):
        out = x_ref[rows, :] + _rms(y, npost_ref[layer:layer + 1, :])
        if batch_major_out:
            obuf[slot, rows.start // S:rows.stop // S] = out.reshape(-1, S, out.shape[-1])
        else:
            o_ref[rows, :] = out

    if batch_major_out:
        for c in copies(slot, i):
            c.start()

        @pl.when(i == nblk - 1)
        def _():
            if nblk >= 2:
                for c in copies(1 - slot, i - 1):
                    c.wait()
            for c in copies(slot, i):
                c.wait()


def _ffn_layer(x, layer, w, bf16, *, R, batch_major_seqs=None):
    N, D = x.shape
    n_layers = w["norm_ffn_pre"].shape[0]
    nblk = N // R
    row_spec = pl.BlockSpec((R, D), lambda i: (i, 0))
    if batch_major_seqs is None:
        out_spec, out_shape, staging = row_spec, jax.ShapeDtypeStruct((N, D), F32), []
    else:
        S = batch_major_seqs
        out_spec = pl.BlockSpec(memory_space=pl.ANY)
        out_shape = jax.ShapeDtypeStruct((S, N // S, D), F32)
        staging = [pltpu.VMEM((2, R // S, S, D), F32), pltpu.SemaphoreType.DMA((2, S))]
    return pl.pallas_call(
        functools.partial(_ffn_kernel, layer, nblk, batch_major_seqs is not None),
        grid=(nblk,),
        in_specs=[row_spec, _resident((n_layers, D)), _resident((n_layers, D)),
                  _resident(bf16["wg"].shape), _resident(bf16["wu"].shape), _resident((D_FF, D))],
        out_specs=out_spec,
        out_shape=out_shape,
        scratch_shapes=[pltpu.VMEM((R, D), BF16), pltpu.VMEM((R, D_FF), BF16)] + staging,
        compiler_params=_params(),
        name="swiglu_ffn",
    )(x, w["norm_ffn_pre"], w["norm_ffn_post"], bf16["wg"], bf16["wu"], bf16["wd"])


def _to_time_major(a):
    S, K, D = a.shape
    return jnp.swapaxes(a, 0, 1).reshape(K * S, D)


def _from_time_major(a, S):
    KS, D = a.shape
    return jnp.swapaxes(a.reshape(KS // S, S, D), 0, 1)


def kernel(x_prompt, x_sample, state_rglru_conv, state_rglru_h, state_sconv, meta_tokens, norm_mix_pre, norm_mix_post, norm_ffn_pre, norm_ffn_post, rg_w_in, rg_conv_w, rg_conv_b, rg_gate_a_w, rg_gate_a_b, rg_gate_x_w, rg_gate_x_b, rg_lambda, rg_w_out, sc_w_in, sc_conv_w, sc_w_out, ffn_w_gate, ffn_w_up, ffn_w_down):
    D = D_MODEL
    depth = norm_mix_pre.shape[0]
    batch, seq, _ = x_prompt.shape
    dec_batch, dec_seq, _ = x_sample.shape
    w = dict(norm_mix_pre=norm_mix_pre, norm_mix_post=norm_mix_post, norm_ffn_pre=norm_ffn_pre,
             norm_ffn_post=norm_ffn_post, rg_w_in=rg_w_in, rg_conv_w=rg_conv_w, rg_conv_b=rg_conv_b,
             rg_gate_a_w=rg_gate_a_w, rg_gate_a_b=rg_gate_a_b, rg_gate_x_w=rg_gate_x_w, rg_gate_x_b=rg_gate_x_b,
             rg_lambda=rg_lambda, rg_w_out=rg_w_out, sc_w_in=sc_w_in, sc_conv_w=sc_conv_w, sc_w_out=sc_w_out,
             ffn_w_gate=ffn_w_gate, ffn_w_up=ffn_w_up, ffn_w_down=ffn_w_down)

    x = jnp.broadcast_to(meta_tokens[:, None, :], (N_META, batch, D)).reshape(N_META * batch, D)
    small = ((batch, N_META), (dec_batch, dec_seq))
    small_rows = N_META * batch + dec_batch * dec_seq
    mixer_bf16, ffn_bf16 = [], []
    rg_conv_s, rg_h_s, sc_s = [], [], []
    for i in range(depth):
        j = i // 2
        sample_out = (N_META * batch, dec_batch, dec_seq) if i == depth - 1 else None
        ffn = _ffn_small_part(small_rows, i, w, sample_out=sample_out)
        if i % 2 == 0:
            conv0 = [jnp.zeros(((CONV_A - 1) * batch, D), F32), _to_time_major(state_rglru_conv[j])]
            h0 = [jnp.zeros((batch, D), F32), state_rglru_h[j]]
            mixer = _rglru_small_part(x, conv0, h0, i, j, w, seqs_and_steps=small,
                                      x_sample=x_sample if i == 0 else None)
            (cs0, cs1, hs0, hs1, *wb), (x, *fb) = _run_small_stage(mixer, ffn, small_rows, "rglru_ffn_small")
            rg_conv_s.append([cs0, cs1])
            rg_h_s.append([hs0, hs1])
            mixer_bf16.append(dict(zip(("w_xr", "w_gate", "gw", "w_out"), wb)))
        else:
            conv0 = [jnp.zeros(((CONV_B - 1) * batch, D), F32), _to_time_major(state_sconv[j])]
            mixer = _sconv_small_part(x, conv0, i, j, w, seqs_and_steps=small)
            (cs0, cs1, *wb), (x, *fb) = _run_small_stage(mixer, ffn, small_rows, "sconv_ffn_small")
            sc_s.append([cs0, cs1])
            mixer_bf16.append(dict(zip(("w_bg", "w_cg", "w_v", "w_out"), wb)))
        ffn_bf16.append(dict(zip(("wg", "wu", "wd"), fb)))
    y_sample = x

    x = x_prompt
    rg_conv_p, rg_h_p, sc_p = [], [], []
    for i in range(depth):
        j = i // 2
        if i % 2 == 0:
            x, cb, hT = _rglru_layer(x, rg_conv_s[j][0], rg_h_s[j][0], i, j, w, mixer_bf16[i], S=batch, TB=PROMPT_TB)
            rg_conv_p.append(cb)
            rg_h_p.append(hT)
        else:
            x, cb = _sconv_layer(x, sc_s[j][0], i, j, w, mixer_bf16[i], S=batch, TB=PROMPT_TB)
            sc_p.append(cb)
        x = _ffn_layer(x, i, w, ffn_bf16[i], R=batch * PROMPT_TB,
                       batch_major_seqs=batch if i == depth - 1 else None)
    y_prompt = x

    return (y_prompt, y_sample,
            jnp.stack([_from_time_major(c, batch) for c in rg_conv_p]), jnp.stack(rg_h_p),
            jnp.stack([_from_time_major(c, batch) for c in sc_p]),
            jnp.stack([_from_time_major(c[1], dec_batch) for c in rg_conv_s]), jnp.stack([h[1] for h in rg_h_s]),
            jnp.stack([_from_time_major(c[1], dec_batch) for c in sc_s]))
```

```python
import functools
from typing import NamedTuple

import jax
import jax.numpy as jnp
from jax import lax
from jax.experimental import pallas as pl
from jax.experimental.pallas import tpu as pltpu

D_MODEL = 1024
D_FF = 2816
N_META = 16
COL_BLOCK = 256
N_COL_BLOCKS = D_MODEL // COL_BLOCK
CONV_A = 4
CONV_B = 3
RG_C = 8.0
EPS = 1e-6

SUBLANES = 8
FF_CHUNK = 256
STREAM_BUFFERS = 3
OUT_SLOTS = 2
PROMPT_TB = 128
X_SLOTS = 3
FFN_ROW_PARTS = 4
MIXER_ROW_PARTS = 4
VMEM_LIMIT_BYTES = 56 * 1024 * 1024
SMALL_STAGE_VMEM_LIMIT_BYTES = 60 * 1024 * 1024

F32 = jnp.float32
BF16 = jnp.bfloat16


class _Seg(NamedTuple):
    S: int
    TB: int
    row0: int
    conv0: int
    h0: int

    @property
    def rows(self):
        return self.S * self.TB


def _segments(seqs_and_steps, taps):
    segs, row0, conv0, h0 = [], 0, 0, 0
    for S, TB in seqs_and_steps:
        segs.append(_Seg(S, TB, row0, conv0, h0))
        row0 += S * TB
        conv0 += (taps - 1) * S + S * TB
        h0 += S
    return tuple(segs), row0, conv0, h0


def _rms(x, w):
    ms = jnp.mean(x * x, axis=-1, keepdims=True)
    return x * lax.rsqrt(ms + EPS) * w


def _dot(a, b):
    return jnp.dot(a, b, preferred_element_type=F32)


def _gelu_tanh(x):
    c = 0.7978845608028654
    hx = 0.5 * x
    return hx + hx * jnp.tanh(x * (c + (c * 0.044715) * (x * x)))


def _resident(shape):
    zeros = (0,) * len(shape)
    return pl.BlockSpec(shape, lambda i: zeros, pipeline_mode=pl.Buffered(1))


def _params(vmem_limit_bytes=VMEM_LIMIT_BYTES):
    return pltpu.CompilerParams(dimension_semantics=("arbitrary",), vmem_limit_bytes=vmem_limit_bytes)


def _start_all(copies):
    for cp in copies:
        cp.start()


def _wait_all(copies):
    for cp in copies:
        cp.wait()


def _rglru_branch_in(seg, xn, w_xr, region):
    P = (CONV_A - 1) * seg.S
    region[seg.conv0 + P:seg.conv0 + P + seg.rows, :] = _dot(xn, w_xr)


def _rglru_conv(seg, cw, cb, region, carry_to):
    S, R = seg.S, seg.rows
    P = (CONV_A - 1) * S
    c0 = seg.conv0
    xc = region[c0:c0 + R, :] * cw[0:1]
    for k in range(1, CONV_A):
        xc = xc + region[c0 + k * S:c0 + k * S + R, :] * cw[k:k + 1]
    xc = xc + cb
    if carry_to is not None:
        region[carry_to:carry_to + P, :] = region[c0 + R:c0 + R + P, :]
    return xc


def _rglru_gate_math(res, xc, half_gab, half_gxb, lam):
    half_c_sp = (-0.5 * RG_C) * jax.nn.softplus(-lam)
    tr = jnp.tanh(res[:, 0:COL_BLOCK] + half_gab)
    log_a = half_c_sp * tr + half_c_sp
    ig = 0.5 * jnp.tanh(res[:, COL_BLOCK:2 * COL_BLOCK] + half_gxb) + 0.5
    a = jnp.exp(log_a)
    m2 = jnp.tanh(log_a) * (-1.0 - a * a)
    u = jnp.where(m2 > 0.0, m2 * lax.rsqrt(m2), 0.0) * (ig * xc)
    return a, u


def _rglru_scan(seg, a, u, g, h_read, h_write, h_row0):
    S, TB = seg.S, seg.TB
    groups = S // SUBLANES
    pieces = [None] * (TB * groups)
    for c in range(groups):
        hrows = slice(h_row0 + c * SUBLANES, h_row0 + (c + 1) * SUBLANES)
        h = h_read[hrows, :]
        for t in range(TB):
            r = t * S + c * SUBLANES
            h = a[r:r + SUBLANES] * h + u[r:r + SUBLANES]
            pieces[t * groups + c] = h * g[r:r + SUBLANES]
        h_write[hrows, :] = h
    return jnp.concatenate(pieces, axis=0)


class _Part(NamedTuple):
    kernel: object
    args: list
    in_specs: list
    out_specs: list
    out_shape: list
    scratch_shapes: list


def _run_small_stage(mixer, ffn, rows, name):
    n_min, n_fin = len(mixer.args), len(ffn.args) - 1
    n_mout, n_fout = len(mixer.out_shape) - 1, len(ffn.out_shape)
    n_mscr, n_fscr = len(mixer.scratch_shapes), len(ffn.scratch_shapes)

    def kernel(*refs):
        refs = list(refs)
        take = lambda k: [refs.pop(0) for _ in range(k)]
        m_in, f_in, m_out, f_out, m_scr, f_scr = (take(k) for k in (n_min, n_fin, n_mout, n_fout, n_mscr, n_fscr))
        (x_mid,) = refs
        f_refs = [x_mid] + f_in + f_out + f_scr
        mixer.kernel(lambda: ffn.kernel(*f_refs, prefetch_only=True), *m_in, x_mid, *m_out, *m_scr)
        ffn.kernel(*f_refs, prefetched=True)

    outs = pl.pallas_call(
        kernel,
        grid=(1,),
        in_specs=mixer.in_specs + ffn.in_specs[1:],
        out_specs=mixer.out_specs[1:] + ffn.out_specs,
        out_shape=mixer.out_shape[1:] + ffn.out_shape,
        scratch_shapes=mixer.scratch_shapes + ffn.scratch_shapes + [pltpu.VMEM((rows, D_MODEL), F32)],
        compiler_params=_params(SMALL_STAGE_VMEM_LIMIT_BYTES),
        name=name,
    )(*mixer.args, *ffn.args[1:])
    return list(outs[:n_mout]), list(outs[n_mout:])


def _rglru_small_kernel(segs, sample_in, layer, j, after_fetch_issue, x_ref, *refs):
    D, C, NB = D_MODEL, COL_BLOCK, N_COL_BLOCKS
    if sample_in is not None:
        xs_hbm, refs = refs[0], refs[1:]
    n = len(segs)
    conv_in, h_in = refs[0:n], refs[n:2 * n]
    (npre_ref, npost_ref, cw_ref, cb_ref, gab_ref, gxb_ref, lam_ref,
     win_hbm, gaw_hbm, gxw_hbm, wout_hbm) = refs[2 * n:2 * n + 11]
    o_ref = refs[2 * n + 11]
    conv_out, h_out = refs[2 * n + 12:3 * n + 12], refs[3 * n + 12:4 * n + 12]
    wxr_b_hbm, wgate_b_hbm, gw_b_hbm, wout_b_hbm = refs[4 * n + 12:4 * n + 16]
    (xn_scr, xr_scr, col_f, sq_f, row_f, col_h, gw_h, row_h, in_sems, out_sems,
     *in_staging) = refs[4 * n + 16:]

    def fetch(b):
        cols = pl.ds(b * C, C)
        return [pltpu.make_async_copy(win_hbm.at[j, :, pl.ds(D + b * C, C)], col_f.at[b, 0], in_sems.at[b, 0]),
                pltpu.make_async_copy(win_hbm.at[j, :, cols], col_f.at[b, 1], in_sems.at[b, 1]),
                pltpu.make_async_copy(gaw_hbm.at[j, b], sq_f.at[b, 0], in_sems.at[b, 2]),
                pltpu.make_async_copy(gxw_hbm.at[j, b], sq_f.at[b, 1], in_sems.at[b, 3]),
                pltpu.make_async_copy(wout_hbm.at[j, cols, :], row_f.at[b], in_sems.at[b, 4])]

    def emit(b):
        k = b % OUT_SLOTS
        return [pltpu.make_async_copy(col_h.at[k, 0], wxr_b_hbm.at[b], out_sems.at[k, 0]),
                pltpu.make_async_copy(col_h.at[k, 1], wgate_b_hbm.at[b], out_sems.at[k, 1]),
                pltpu.make_async_copy(gw_h.at[k], gw_b_hbm.at[b], out_sems.at[k, 2]),
                pltpu.make_async_copy(row_h.at[k], wout_b_hbm.at[pl.ds(b * C, C), :], out_sems.at[k, 3])]

    if sample_in is None:
        read_x = lambda: x_ref[...]
    else:
        S, T = sample_in
        xs_scr, sems = in_staging
        x_copies = [pltpu.make_async_copy(xs_hbm.at[:, t, :], xs_scr.at[t], sems.at[t]) for t in range(T)]
        _start_all(x_copies)
        read_x = lambda: jnp.concatenate([x_ref[...], xs_scr[...].reshape(T * S, D)], axis=0)
    for b in range(NB):
        _start_all(fetch(b))
    after_fetch_issue()
    if sample_in is not None:
        _wait_all(x_copies)
    xn_scr[...] = _rms(read_x(), npre_ref[layer:layer + 1, :]).astype(BF16)
    xn = xn_scr[...]

    rows_of = lambda seg: slice(seg.row0, seg.row0 + seg.rows)
    y = None
    for b in range(NB):
        cols = slice(b * C, (b + 1) * C)
        _wait_all(fetch(b))
        w_xr, w_gate = col_f[b, 0].astype(BF16), col_f[b, 1].astype(BF16)
        gw = (0.5 * jnp.concatenate([sq_f[b, 0], sq_f[b, 1]], axis=1)).astype(BF16)
        w_out_rows = row_f[b].astype(BF16)
        k = b % OUT_SLOTS
        if b >= OUT_SLOTS:
            _wait_all(emit(b - OUT_SLOTS))
        col_h[k, 0], col_h[k, 1], gw_h[k], row_h[k] = w_xr, w_gate, gw, w_out_rows
        _start_all(emit(b))

        xr = _dot(xn, w_xr)
        g = _gelu_tanh(_dot(xn, w_gate))
        xcs = []
        for seg, c_in, c_out in zip(segs, conv_in, conv_out):
            P = (CONV_A - 1) * seg.S
            xr_scr[seg.conv0:seg.conv0 + P, :] = c_in[:, cols]
            xr_scr[seg.conv0 + P:seg.conv0 + P + seg.rows, :] = xr[rows_of(seg)]
            xcs.append(_rglru_conv(seg, cw_ref[:, cols], cb_ref[j:j + 1, cols], xr_scr, seg.conv0))
            c_out[:, cols] = xr_scr[seg.conv0:seg.conv0 + P, :]
        xc = jnp.concatenate(xcs, axis=0)
        a, u = _rglru_gate_math(_dot(xc.astype(BF16), gw), xc, 0.5 * gab_ref[b:b + 1, :],
                                0.5 * gxb_ref[b:b + 1, :], lam_ref[j:j + 1, cols])
        hs = [_rglru_scan(seg, a[rows_of(seg)], u[rows_of(seg)], g[rows_of(seg)],
                          hi.at[:, cols], ho.at[:, cols], 0)
              for seg, hi, ho in zip(segs, h_in, h_out)]
        part = _dot(jnp.concatenate(hs, axis=0).astype(BF16), w_out_rows)
        y = part if y is None else y + part

    for b in range(max(NB - OUT_SLOTS, 0), NB):
        _wait_all(emit(b))
    o_ref[...] = read_x() + _rms(y, npost_ref[layer:layer + 1, :])


def _rglru_small_part(x, conv_in, h_in, layer, j, w, *, seqs_and_steps, x_sample=None):
    D, C, NB = D_MODEL, COL_BLOCK, N_COL_BLOCKS
    segs, R, conv_rows, _ = _segments(seqs_and_steps, CONV_A)
    n_layers, n_a = w["norm_mix_pre"].shape[0], w["rg_conv_b"].shape[0]
    hbm = pl.BlockSpec(memory_space=pl.ANY)
    layer_of = lambda arr: pl.BlockSpec((None,) + arr.shape[1:], lambda i: (j,) + (0,) * (arr.ndim - 1),
                                        pipeline_mode=pl.Buffered(1))
    if x_sample is None:
        x_args, x_specs, sample_in, staging = [x], [_resident(x.shape)], None, []
    else:
        S, T, _ = x_sample.shape
        x_args, x_specs, sample_in = [x, x_sample], [_resident(x.shape), hbm], (S, T)
        staging = [pltpu.VMEM((T, S, D), F32), pltpu.SemaphoreType.DMA((T,))]
    states = conv_in + h_in
    return _Part(
        kernel=functools.partial(_rglru_small_kernel, segs, sample_in, layer, j),
        args=x_args + states + [w["norm_mix_pre"], w["norm_mix_post"], w["rg_conv_w"], w["rg_conv_b"],
                                w["rg_gate_a_b"], w["rg_gate_x_b"], w["rg_lambda"], w["rg_w_in"],
                                w["rg_gate_a_w"], w["rg_gate_x_w"], w["rg_w_out"]],
        in_specs=x_specs + [_resident(s.shape) for s in states]
                 + [_resident((n_layers, D)), _resident((n_layers, D)), layer_of(w["rg_conv_w"]),
                    _resident((n_a, D)), layer_of(w["rg_gate_a_b"]), layer_of(w["rg_gate_x_b"]),
                    _resident((n_a, D)), hbm, hbm, hbm, hbm],
        out_specs=[pl.BlockSpec((R, D), lambda i: (0, 0))]
                  + [pl.BlockSpec(s.shape, lambda i: (0, 0)) for s in states] + [hbm] * 4,
        out_shape=[jax.ShapeDtypeStruct((R, D), F32)] + [jax.ShapeDtypeStruct(s.shape, F32) for s in states]
                  + [jax.ShapeDtypeStruct((NB, D, C), BF16), jax.ShapeDtypeStruct((NB, D, C), BF16),
                     jax.ShapeDtypeStruct((NB, C, 2 * C), BF16), jax.ShapeDtypeStruct((D, D), BF16)],
        scratch_shapes=[pltpu.VMEM((R, D), BF16),
                        pltpu.VMEM((conv_rows, C), F32),
                        pltpu.VMEM((NB, 2, D, C), F32),
                        pltpu.VMEM((NB, 2, C, C), F32),
                        pltpu.VMEM((NB, C, D), F32),
                        pltpu.VMEM((OUT_SLOTS, 2, D, C), BF16), pltpu.VMEM((OUT_SLOTS, C, 2 * C), BF16),
                        pltpu.VMEM((OUT_SLOTS, C, D), BF16),
                        pltpu.SemaphoreType.DMA((NB, 5)), pltpu.SemaphoreType.DMA((OUT_SLOTS, 4))] + staging)


def _rglru_kernel(seg, nblk, layer, j,
                  x_ref, conv_in_ref, h_in_ref, npre_ref, npost_ref, wxr_ref, wgate_ref, cw_ref, cb_ref,
                  gw_ref, gab_ref, gxb_ref, lam_ref, wout_ref,
                  o_ref, conv_out_ref, h_out_ref, xn_scr, xr_scr, h_scr, hs_scr, xbuf, sems):
    S, TB, R = seg.S, seg.TB, seg.rows
    P = (CONV_A - 1) * S
    i = pl.program_id(0)
    blocks = [slice(b * COL_BLOCK, (b + 1) * COL_BLOCK) for b in range(N_COL_BLOCKS)]

    def copies(step):
        which = step % X_SLOTS
        t0 = pl.multiple_of(step * TB, TB)
        return [pltpu.make_async_copy(x_ref.at[s, pl.ds(t0, TB), :], xbuf.at[which, :, s, :],
                                      sems.at[which, s]) for s in range(S)]

    def previous_block_out_proj(q):
        rows = slice(q * (R // N_COL_BLOCKS), (q + 1) * (R // N_COL_BLOCKS))
        return _dot(hs_scr[(i + 1) % 2, rows, :], wout_ref[...])

    def previous_block_store(q, y):
        tq = TB // N_COL_BLOCKS
        x_prev = xbuf[(i + X_SLOTS - 1) % X_SLOTS, q * tq:(q + 1) * tq].reshape(R // N_COL_BLOCKS, D_MODEL)
        o_ref[q * (R // N_COL_BLOCKS):(q + 1) * (R // N_COL_BLOCKS), :] = (
            x_prev + _rms(y, npost_ref[layer:layer + 1, :]))

    @pl.when(i == 0)
    def _():
        for b, cols in enumerate(blocks):
            xr_scr[b, 0:P, :] = conv_in_ref[:, cols]
            h_scr[b] = h_in_ref[:, cols]
        hs_scr[1] = jnp.zeros((R, D_MODEL), BF16)
        xbuf[X_SLOTS - 1] = jnp.zeros((TB, S, D_MODEL), F32)
        _start_all(copies(0))

    @pl.when(i + 1 < nblk)
    def _():
        _start_all(copies(i + 1))

    @pl.when(i < nblk)
    def _():
        _wait_all(copies(i))
        xn_scr[...] = _rms(xbuf[i % X_SLOTS].reshape(R, D_MODEL), npre_ref[layer:layer + 1, :]).astype(BF16)
        xn = xn_scr[...]

        gate_pre, xc, res = {}, {}, {}

        def input_matmuls(b):
            _rglru_branch_in(seg, xn, wxr_ref[b], xr_scr.at[b])
            gate_pre[b] = _dot(xn, wgate_ref[b])

        def conv_and_gate_matmul(b):
            xc[b] = _rglru_conv(seg, cw_ref[:, blocks[b]], cb_ref[j:j + 1, blocks[b]], xr_scr.at[b], 0)
            res[b] = _dot(xc[b].astype(BF16), gw_ref[b])

        input_matmuls(0)
        conv_and_gate_matmul(0)
        input_matmuls(1)
        for b, cols in enumerate(blocks):
            y_prev = previous_block_out_proj(b)
            if b + 2 < N_COL_BLOCKS:
                input_matmuls(b + 2)
            if b + 1 < N_COL_BLOCKS:
                conv_and_gate_matmul(b + 1)
            g = _gelu_tanh(gate_pre.pop(b))
            a, u = _rglru_gate_math(res.pop(b), xc.pop(b), 0.5 * gab_ref[b:b + 1, :],
                                    0.5 * gxb_ref[b:b + 1, :], lam_ref[j:j + 1, cols])
            hs = _rglru_scan(seg, a, u, g, h_scr.at[b], h_scr.at[b], 0)
            hs_scr[i % 2, :, cols] = hs.astype(BF16)
            previous_block_store(b, y_prev)

    @pl.when(i == nblk)
    def _():
        for q in range(N_COL_BLOCKS):
            previous_block_store(q, previous_block_out_proj(q))
        for b, cols in enumerate(blocks):
            conv_out_ref[:, cols] = xr_scr[b, 0:P, :]
            h_out_ref[:, cols] = h_scr[b]


def _rglru_layer(x, conv_in, h_in, layer, j, w, bf16, *, S, TB):
    D = D_MODEL
    (seg,), R, conv_rows, _ = _segments(((S, TB),), CONV_A)
    nblk = x.size // D // R
    P = (CONV_A - 1) * S
    n_layers, n_a = w["norm_mix_pre"].shape[0], w["rg_conv_b"].shape[0]
    row_spec = pl.BlockSpec((R, D), lambda i: (jnp.maximum(i - 1, 0), 0))
    x_spec = pl.BlockSpec(memory_space=pl.ANY)
    staging = [pltpu.VMEM((2, R, D), BF16),
               pltpu.VMEM((X_SLOTS, TB, S, D), F32), pltpu.SemaphoreType.DMA((X_SLOTS, S))]
    layer_of = lambda arr: pl.BlockSpec((None,) + arr.shape[1:], lambda i: (j,) + (0,) * (arr.ndim - 1),
                                        pipeline_mode=pl.Buffered(1))
    return pl.pallas_call(
        functools.partial(_rglru_kernel, seg, nblk, layer, j),
        grid=(nblk + 1,),
        in_specs=[x_spec, _resident((P, D)), _resident((S, D)),
                  _resident((n_layers, D)), _resident((n_layers, D)),
                  _resident(bf16["w_xr"].shape), _resident(bf16["w_gate"].shape),
                  layer_of(w["rg_conv_w"]), _resident((n_a, D)), _resident(bf16["gw"].shape),
                  layer_of(w["rg_gate_a_b"]), layer_of(w["rg_gate_x_b"]), _resident((n_a, D)),
                  _resident((D, D))],
        out_specs=[row_spec, pl.BlockSpec((P, D), lambda i: (0, 0)), pl.BlockSpec((S, D), lambda i: (0, 0))],
        out_shape=[jax.ShapeDtypeStruct((nblk * R, D), F32), jax.ShapeDtypeStruct((P, D), F32),
                   jax.ShapeDtypeStruct((S, D), F32)],
        scratch_shapes=[pltpu.VMEM((R, D), BF16),
                        pltpu.VMEM((N_COL_BLOCKS, conv_rows, COL_BLOCK), F32),
                        pltpu.VMEM((N_COL_BLOCKS, S, COL_BLOCK), F32)]
                       + staging,
        compiler_params=_params(),
        name="rglru_mixer",
    )(x, conv_in, h_in, w["norm_mix_pre"], w["norm_mix_post"], bf16["w_xr"], bf16["w_gate"], w["rg_conv_w"],
      w["rg_conv_b"], bf16["gw"], w["rg_gate_a_b"], w["rg_gate_x_b"], w["rg_lambda"], bf16["w_out"])


def _sconv_conv(seg, cv, cw, region, rows=None):
    S = seg.S
    P = (CONV_B - 1) * S
    rows = rows or slice(0, seg.rows)
    c0, n = seg.conv0 + rows.start, rows.stop - rows.start
    region[c0 + P:c0 + P + n, :] = cv
    conv = region[c0:c0 + n, :] * cw[0:1]
    for k in range(1, CONV_B):
        conv = conv + region[c0 + k * S:c0 + k * S + n, :] * cw[k:k + 1]
    if rows.stop == seg.rows:
        region[seg.conv0:seg.conv0 + P, :] = region[seg.conv0 + seg.rows:seg.conv0 + seg.rows + P, :]
    return conv


def _sconv_small_kernel(segs, layer, j, after_fetch_issue, x_ref, *refs):
    D, C, NB = D_MODEL, COL_BLOCK, N_COL_BLOCKS
    n = len(segs)
    conv_in = refs[0:n]
    npre_ref, npost_ref, cw_ref, win_hbm, wout_hbm = refs[n:n + 5]
    o_ref = refs[n + 5]
    conv_out = refs[n + 6:2 * n + 6]
    wbg_b_hbm, wcg_b_hbm, wv_b_hbm, wout_b_hbm = refs[2 * n + 6:2 * n + 10]
    xn_scr, cv_scr, col_f, row_f, col_h, row_h, in_sems, out_sems = refs[2 * n + 10:]

    def fetch(b):
        return ([pltpu.make_async_copy(win_hbm.at[j, :, pl.ds(k * D + b * C, C)], col_f.at[b, k], in_sems.at[b, k])
                 for k in range(3)]
                + [pltpu.make_async_copy(wout_hbm.at[j, pl.ds(b * C, C), :], row_f.at[b], in_sems.at[b, 3])])

    def emit(b):
        s = b % OUT_SLOTS
        return ([pltpu.make_async_copy(col_h.at[s, k], dst.at[b], out_sems.at[s, k])
                 for k, dst in enumerate((wbg_b_hbm, wcg_b_hbm, wv_b_hbm))]
                + [pltpu.make_async_copy(row_h.at[s], wout_b_hbm.at[pl.ds(b * C, C), :], out_sems.at[s, 3])])

    for b in range(NB):
        _start_all(fetch(b))
    after_fetch_issue()
    xn_scr[...] = _rms(x_ref[...], npre_ref[layer:layer + 1, :]).astype(BF16)
    xn = xn_scr[...]

    y = None
    for b in range(NB):
        cols = slice(b * C, (b + 1) * C)
        _wait_all(fetch(b))
        w_bg, w_cg, w_v = (col_f[b, k].astype(BF16) for k in range(3))
        w_out_rows = row_f[b].astype(BF16)
        s = b % OUT_SLOTS
        if b >= OUT_SLOTS:
            _wait_all(emit(b - OUT_SLOTS))
        col_h[s, 0], col_h[s, 1], col_h[s, 2], row_h[s] = w_bg, w_cg, w_v, w_out_rows
        _start_all(emit(b))

        cv = _dot(xn, w_cg) * _dot(xn, w_v)
        convs = []
        for seg, c_in, c_out in zip(segs, conv_in, conv_out):
            P = (CONV_B - 1) * seg.S
            cv_scr[seg.conv0:seg.conv0 + P, :] = c_in[:, cols]
            convs.append(_sconv_conv(seg, cv[seg.row0:seg.row0 + seg.rows], cw_ref[:, cols], cv_scr))
            c_out[:, cols] = cv_scr[seg.conv0:seg.conv0 + P, :]
        m = (_dot(xn, w_bg) * jnp.concatenate(convs, axis=0)).astype(BF16)
        part = _dot(m, w_out_rows)
        y = part if y is None else y + part

    for b in range(max(NB - OUT_SLOTS, 0), NB):
        _wait_all(emit(b))
    o_ref[...] = x_ref[...] + _rms(y, npost_ref[layer:layer + 1, :])


def _sconv_small_part(x, conv_in, layer, j, w, *, seqs_and_steps):
    R, D = x.shape
    C, NB = COL_BLOCK, N_COL_BLOCKS
    segs, _, conv_rows, _ = _segments(seqs_and_steps, CONV_B)
    n_layers = w["norm_mix_pre"].shape[0]
    hbm = pl.BlockSpec(memory_space=pl.ANY)
    return _Part(
        kernel=functools.partial(_sconv_small_kernel, segs, layer, j),
        args=[x] + conv_in + [w["norm_mix_pre"], w["norm_mix_post"], w["sc_conv_w"], w["sc_w_in"], w["sc_w_out"]],
        in_specs=[_resident((R, D))] + [_resident(c.shape) for c in conv_in]
                 + [_resident((n_layers, D)), _resident((n_layers, D)),
                    pl.BlockSpec((None, CONV_B, D), lambda i: (j, 0, 0), pipeline_mode=pl.Buffered(1)), hbm, hbm],
        out_specs=[pl.BlockSpec((R, D), lambda i: (0, 0))]
                  + [pl.BlockSpec(c.shape, lambda i: (0, 0)) for c in conv_in] + [hbm] * 4,
        out_shape=[jax.ShapeDtypeStruct((R, D), F32)] + [jax.ShapeDtypeStruct(c.shape, F32) for c in conv_in]
                  + [jax.ShapeDtypeStruct((NB, D, C), BF16)] * 3 + [jax.ShapeDtypeStruct((D, D), BF16)],
        scratch_shapes=[pltpu.VMEM((R, D), BF16),
                        pltpu.VMEM((conv_rows, C), F32),
                        pltpu.VMEM((NB, 3, D, C), F32),
                        pltpu.VMEM((NB, C, D), F32),
                        pltpu.VMEM((OUT_SLOTS, 3, D, C), BF16), pltpu.VMEM((OUT_SLOTS, C, D), BF16),
                        pltpu.SemaphoreType.DMA((NB, 4)), pltpu.SemaphoreType.DMA((OUT_SLOTS, 4))])


def _sconv_kernel(seg, nblk, layer, x_ref, conv_in_ref, npre_ref, npost_ref, wbg_ref, wcg_ref, wv_ref,
                  cw_ref, wout_ref, o_ref, conv_out_ref, xn_scr, cv_scr, m_scr):
    P = (CONV_B - 1) * seg.S
    i = pl.program_id(0)
    blocks = [slice(b * COL_BLOCK, (b + 1) * COL_BLOCK) for b in range(N_COL_BLOCKS)]

    @pl.when(i == 0)
    def _():
        for b, cols in enumerate(blocks):
            cv_scr[b, 0:P, :] = conv_in_ref[:, cols]

    R = seg.rows
    parts = [slice(k * R // MIXER_ROW_PARTS, (k + 1) * R // MIXER_ROW_PARTS) for k in range(MIXER_ROW_PARTS)]
    for rows in parts:
        xn_scr[rows, :] = _rms(x_ref[rows, :], npre_ref[layer:layer + 1, :]).astype(BF16)
    for rows in parts:
        xn = xn_scr[rows, :]
        for b, cols in enumerate(blocks):
            cv = _dot(xn, wcg_ref[b]) * _dot(xn, wv_ref[b])
            conv = _sconv_conv(seg, cv, cw_ref[:, cols], cv_scr.at[b], rows)
            m_scr[rows, cols] = (_dot(xn, wbg_ref[b]) * conv).astype(BF16)
    ys = [_dot(m_scr[rows, :], wout_ref[...]) for rows in parts]
    for rows, y in zip(parts, ys):
        o_ref[rows, :] = x_ref[rows, :] + _rms(y, npost_ref[layer:layer + 1, :])

    @pl.when(i == nblk - 1)
    def _():
        for b, cols in enumerate(blocks):
            conv_out_ref[:, cols] = cv_scr[b, 0:P, :]


def _sconv_layer(x, conv_in, layer, j, w, bf16, *, S, TB):
    N, D = x.shape
    (seg,), R, conv_rows, _ = _segments(((S, TB),), CONV_B)
    nblk = N // R
    P = (CONV_B - 1) * S
    n_layers = w["norm_mix_pre"].shape[0]
    row_spec = pl.BlockSpec((R, D), lambda i: (i, 0))
    return pl.pallas_call(
        functools.partial(_sconv_kernel, seg, nblk, layer),
        grid=(nblk,),
        in_specs=[row_spec, _resident((P, D)), _resident((n_layers, D)), _resident((n_layers, D)),
                  _resident(bf16["w_bg"].shape), _resident(bf16["w_cg"].shape), _resident(bf16["w_v"].shape),
                  pl.BlockSpec((None, CONV_B, D), lambda i: (j, 0, 0), pipeline_mode=pl.Buffered(1)),
                  _resident((D, D))],
        out_specs=[row_spec, pl.BlockSpec((P, D), lambda i: (0, 0))],
        out_shape=[jax.ShapeDtypeStruct((N, D), F32), jax.ShapeDtypeStruct((P, D), F32)],
        scratch_shapes=[pltpu.VMEM((R, D), BF16),
                        pltpu.VMEM((N_COL_BLOCKS, conv_rows, COL_BLOCK), F32),
                        pltpu.VMEM((R, D), BF16)],
        compiler_params=_params(),
        name="sconv_mixer",
    )(x, conv_in, w["norm_mix_pre"], w["norm_mix_post"], bf16["w_bg"], bf16["w_cg"], bf16["w_v"],
      w["sc_conv_w"], bf16["w_out"])


def _swiglu(g, u):
    return (g * jax.nn.sigmoid(g) * u).astype(BF16)


def _ffn_stream_kernel(layer, nchunk, sample_out, x_ref, npre_ref, npost_ref, wg_hbm, wu_hbm, wd_hbm,
                       o_ref, wg_b_hbm, wu_b_hbm, wd_b_hbm,
                       xn_scr, wg_f, wu_f, wd_f, wg_h, wu_h, wd_h, in_sems, out_sems, *out_staging,
                       prefetch_only=False, prefetched=False):
    F, NB = FF_CHUNK, STREAM_BUFFERS

    def fetch(c):
        k, cols = c % NB, pl.ds(c * F, F)
        return [pltpu.make_async_copy(wg_hbm.at[layer, :, cols], wg_f.at[k], in_sems.at[0, k]),
                pltpu.make_async_copy(wu_hbm.at[layer, :, cols], wu_f.at[k], in_sems.at[1, k]),
                pltpu.make_async_copy(wd_hbm.at[layer, cols, :], wd_f.at[k], in_sems.at[2, k])]

    def emit(c):
        k = c % NB
        return [pltpu.make_async_copy(wg_h.at[k], wg_b_hbm.at[c], out_sems.at[0, k]),
                pltpu.make_async_copy(wu_h.at[k], wu_b_hbm.at[c], out_sems.at[1, k]),
                pltpu.make_async_copy(wd_h.at[k], wd_b_hbm.at[pl.ds(c * F, F), :], out_sems.at[2, k])]

    if not prefetched:
        for c in range(min(NB, nchunk)):
            _start_all(fetch(c))
    if prefetch_only:
        return
    xn_scr[...] = _rms(x_ref[...], npre_ref[layer:layer + 1, :]).astype(BF16)
    xn = xn_scr[...]

    y = None
    for c in range(nchunk):
        k = c % NB
        for cp in fetch(c):
            cp.wait()
        if c >= NB:
            for cp in emit(c - NB):
                cp.wait()
        wg, wu, wd = wg_f[k].astype(BF16), wu_f[k].astype(BF16), wd_f[k].astype(BF16)
        wg_h[k], wu_h[k], wd_h[k] = wg, wu, wd
        for cp in emit(c):
            cp.start()
        if c + NB < nchunk:
            for cp in fetch(c + NB):
                cp.start()
        part = _dot(_swiglu(_dot(xn, wg), _dot(xn, wu)), wd)
        y = part if y is None else y + part
    for c in range(max(nchunk - NB, 0), nchunk):
        for cp in emit(c):
            cp.wait()

    out = x_ref[...] + _rms(y, npost_ref[layer:layer + 1, :])
    if sample_out is None:
        o_ref[...] = out
    else:
        row0, S, T = sample_out
        obuf, sems = out_staging
        obuf[...] = out[row0:row0 + T * S].reshape(T, S, out.shape[-1])
        copies = [pltpu.make_async_copy(obuf.at[t], o_ref.at[:, t, :], sems.at[t]) for t in range(T)]
        for cp in copies:
            cp.start()
        for cp in copies:
            cp.wait()


def _ffn_small_part(R, layer, w, *, sample_out=None):
    D = D_MODEL
    F, NB = FF_CHUNK, STREAM_BUFFERS
    n_layers = w["norm_ffn_pre"].shape[0]
    nchunk = D_FF // F
    hbm = pl.BlockSpec(memory_space=pl.ANY)
    if sample_out is None:
        o_spec, o_shape, staging = pl.BlockSpec((R, D), lambda i: (0, 0)), jax.ShapeDtypeStruct((R, D), F32), []
    else:
        _, S, T = sample_out
        o_spec, o_shape = hbm, jax.ShapeDtypeStruct((S, T, D), F32)
        staging = [pltpu.VMEM((T, S, D), F32), pltpu.SemaphoreType.DMA((T,))]
    return _Part(
        kernel=functools.partial(_ffn_stream_kernel, layer, nchunk, sample_out),
        args=[None, w["norm_ffn_pre"], w["norm_ffn_post"], w["ffn_w_gate"], w["ffn_w_up"], w["ffn_w_down"]],
        in_specs=[_resident((R, D)), _resident((n_layers, D)), _resident((n_layers, D)), hbm, hbm, hbm],
        out_specs=[o_spec, hbm, hbm, hbm],
        out_shape=[o_shape, jax.ShapeDtypeStruct((nchunk, D, F), BF16),
                   jax.ShapeDtypeStruct((nchunk, D, F), BF16), jax.ShapeDtypeStruct((D_FF, D), BF16)],
        scratch_shapes=[pltpu.VMEM((R, D), BF16),
                        pltpu.VMEM((NB, D, F), F32), pltpu.VMEM((NB, D, F), F32), pltpu.VMEM((NB, F, D), F32),
                        pltpu.VMEM((NB, D, F), BF16), pltpu.VMEM((NB, D, F), BF16), pltpu.VMEM((NB, F, D), BF16),
                        pltpu.SemaphoreType.DMA((3, NB)), pltpu.SemaphoreType.DMA((3, NB))] + staging)


def _ffn_kernel(layer, nblk, batch_major_out, x_ref, npre_ref, npost_ref, wg_ref, wu_ref, wd_ref, o_ref,
                xn_scr, act_scr, *out_staging):
    R = x_ref.shape[0]
    i = pl.program_id(0)
    slot = i % 2
    if batch_major_out:
        obuf, sems = out_staging
        S, TB = o_ref.shape[0], R // o_ref.shape[0]

        def copies(which, step):
            t0 = pl.multiple_of(step * TB, TB)
            return [pltpu.make_async_copy(obuf.at[which, :, s, :],
                                          o_ref.at[s, pl.ds(t0, TB), :], sems.at[which, s]) for s in range(S)]

        @pl.when(i >= 2)
        def _():
            for c in copies(slot, i - 2):
                c.wait()

    parts = [slice(k * R // FFN_ROW_PARTS, (k + 1) * R // FFN_ROW_PARTS) for k in range(FFN_ROW_PARTS)]
    for rows in parts:
        xn_scr[rows, :] = _rms(x_ref[rows, :], npre_ref[layer:layer + 1, :]).astype(BF16)
    for rows in parts:
        xn = xn_scr[rows, :]
        for c in range(D_FF // FF_CHUNK):
            cols = slice(c * FF_CHUNK, (c + 1) * FF_CHUNK)
            act_scr[rows, cols] = _swiglu(_dot(xn, wg_ref[c]), _dot(xn, wu_ref[c]))
    ys = [_dot(act_scr[rows, :], wd_ref[...]) for rows in parts]
    for rows, y in zip(parts, ys):
        out = x_ref[rows, :] + _rms(y, npost_ref[layer:layer + 1, :])
        if batch_major_out:
            obuf[slot, rows.start // S:rows.stop // S] = out.reshape(-1, S, out.shape[-1])
        else:
            o_ref[rows, :] = out

    if batch_major_out:
        for c in copies(slot, i):
            c.start()

        @pl.when(i == nblk - 1)
        def _():
            if nblk >= 2:
                for c in copies(1 - slot, i - 1):
                    c.wait()
            for c in copies(slot, i):
                c.wait()


def _ffn_layer(x, layer, w, bf16, *, R, batch_major_seqs=None):
    N, D = x.shape
    n_layers = w["norm_ffn_pre"].shape[0]
    nblk = N // R
    row_spec = pl.BlockSpec((R, D), lambda i: (i, 0))
    if batch_major_seqs is None:
        out_spec, out_shape, staging = row_spec, jax.ShapeDtypeStruct((N, D), F32), []
    else:
        S = batch_major_seqs
        out_spec = pl.BlockSpec(memory_space=pl.ANY)
        out_shape = jax.ShapeDtypeStruct((S, N // S, D), F32)
        staging = [pltpu.VMEM((2, R // S, S, D), F32), pltpu.SemaphoreType.DMA((2, S))]
    return pl.pallas_call(
        functools.partial(_ffn_kernel, layer, nblk, batch_major_seqs is not None),
        grid=(nblk,),
        in_specs=[row_spec, _resident((n_layers, D)), _resident((n_layers, D)),
                  _resident(bf16["wg"].shape), _resident(bf16["wu"].shape), _resident((D_FF, D))],
        out_specs=out_spec,
        out_shape=out_shape,
        scratch_shapes=[pltpu.VMEM((R, D), BF16), pltpu.VMEM((R, D_FF), BF16)] + staging,
        compiler_params=_params(),
        name="swiglu_ffn",
    )(x, w["norm_ffn_pre"], w["norm_ffn_post"], bf16["wg"], bf16["wu"], bf16["wd"])


def _to_time_major(a):
    S, K, D = a.shape
    return jnp.swapaxes(a, 0, 1).reshape(K * S, D)


def _from_time_major(a, S):
    KS, D = a.shape
    return jnp.swapaxes(a.reshape(KS // S, S, D), 0, 1)


def kernel(x_prompt, x_sample, state_rglru_conv, state_rglru_h, state_sconv, meta_tokens, norm_mix_pre, norm_mix_post, norm_ffn_pre, norm_ffn_post, rg_w_in, rg_conv_w, rg_conv_b, rg_gate_a_w, rg_gate_a_b, rg_gate_x_w, rg_gate_x_b, rg_lambda, rg_w_out, sc_w_in, sc_conv_w, sc_w_out, ffn_w_gate, ffn_w_up, ffn_w_down):
    D = D_MODEL
    depth = norm_mix_pre.shape[0]
    batch, seq, _ = x_prompt.shape
    dec_batch, dec_seq, _ = x_sample.shape
    w = dict(norm_mix_pre=norm_mix_pre, norm_mix_post=norm_mix_post, norm_ffn_pre=norm_ffn_pre,
             norm_ffn_post=norm_ffn_post, rg_w_in=rg_w_in, rg_conv_w=rg_conv_w, rg_conv_b=rg_conv_b,
             rg_gate_a_w=rg_gate_a_w, rg_gate_a_b=rg_gate_a_b, rg_gate_x_w=rg_gate_x_w, rg_gate_x_b=rg_gate_x_b,
             rg_lambda=rg_lambda, rg_w_out=rg_w_out, sc_w_in=sc_w_in, sc_conv_w=sc_conv_w, sc_w_out=sc_w_out,
             ffn_w_gate=ffn_w_gate, ffn_w_up=ffn_w_up, ffn_w_down=ffn_w_down)

    x = jnp.broadcast_to(meta_tokens[:, None, :], (N_META, batch, D)).reshape(N_META * batch, D)
    small = ((batch, N_META), (dec_batch, dec_seq))
    small_rows = N_META * batch + dec_batch * dec_seq
    mixer_bf16, ffn_bf16 = [], []
    rg_conv_s, rg_h_s, sc_s = [], [], []
    for i in range(depth):
        j = i // 2
        sample_out = (N_META * batch, dec_batch, dec_seq) if i == depth - 1 else None
        ffn = _ffn_small_part(small_rows, i, w, sample_out=sample_out)
        if i % 2 == 0:
            conv0 = [jnp.zeros(((CONV_A - 1) * batch, D), F32), _to_time_major(state_rglru_conv[j])]
            h0 = [jnp.zeros((batch, D), F32), state_rglru_h[j]]
            mixer = _rglru_small_part(x, conv0, h0, i, j, w, seqs_and_steps=small,
                                      x_sample=x_sample if i == 0 else None)
            (cs0, cs1, hs0, hs1, *wb), (x, *fb) = _run_small_stage(mixer, ffn, small_rows, "rglru_ffn_small")
            rg_conv_s.append([cs0, cs1])
            rg_h_s.append([hs0, hs1])
            mixer_bf16.append(dict(zip(("w_xr", "w_gate", "gw", "w_out"), wb)))
        else:
            conv0 = [jnp.zeros(((CONV_B - 1) * batch, D), F32), _to_time_major(state_sconv[j])]
            mixer = _sconv_small_part(x, conv0, i, j, w, seqs_and_steps=small)
            (cs0, cs1, *wb), (x, *fb) = _run_small_stage(mixer, ffn, small_rows, "sconv_ffn_small")
            sc_s.append([cs0, cs1])
            mixer_bf16.append(dict(zip(("w_bg", "w_cg", "w_v", "w_out"), wb)))
        ffn_bf16.append(dict(zip(("wg", "wu", "wd"), fb)))
    y_sample = x

    x = x_prompt
    rg_conv_p, rg_h_p, sc_p = [], [], []
    for i in range(depth):
        j = i // 2
        if i % 2 == 0:
            x, cb, hT = _rglru_layer(x, rg_conv_s[j][0], rg_h_s[j][0], i, j, w, mixer_bf16[i], S=batch, TB=PROMPT_TB)
            rg_conv_p.append(cb)
            rg_h_p.append(hT)
        else:
            x, cb = _sconv_layer(x, sc_s[j][0], i, j, w, mixer_bf16[i], S=batch, TB=PROMPT_TB)
            sc_p.append(cb)
        x = _ffn_layer(x, i, w, ffn_bf16[i], R=batch * PROMPT_TB,
                       batch_major_seqs=batch if i == depth - 1 else None)
    y_prompt = x

    return (y_prompt, y_sample,
            jnp.stack([_from_time_major(c, batch) for c in rg_conv_p]), jnp.stack(rg_h_p),
            jnp.stack([_from_time_major(c, batch) for c in sc_p]),
            jnp.stack([_from_time_major(c[1], dec_batch) for c in rg_conv_s]), jnp.stack([h[1] for h in rg_h_s]),
            jnp.stack([_from_time_major(c[1], dec_batch) for c in sc_s]))
```

```python
import functools
from typing import NamedTuple

import jax
import jax.numpy as jnp
from jax import lax
from jax.experimental import pallas as pl
from jax.experimental.pallas import tpu as pltpu

D_MODEL = 1024
D_FF = 2816
N_META = 16
COL_BLOCK = 256
N_COL_BLOCKS = D_MODEL // COL_BLOCK
CONV_A = 4
CONV_B = 3
RG_C = 8.0
EPS = 1e-6

SUBLANES = 8
FF_CHUNK = 256
STREAM_BUFFERS = 3
OUT_SLOTS = 2
WEIGHT_EMIT_DMA_PRIORITY = 1
PROMPT_TB = 128
FFN_ROW_PARTS = 4
MIXER_ROW_PARTS = 4
VMEM_LIMIT_BYTES = 56 * 1024 * 1024
SMALL_STAGE_VMEM_LIMIT_BYTES = 60 * 1024 * 1024

F32 = jnp.float32
BF16 = jnp.bfloat16


class _Seg(NamedTuple):
    S: int
    TB: int
    row0: int
    conv0: int
    h0: int

    @property
    def rows(self):
        return self.S * self.TB


def _segments(seqs_and_steps, taps):
    segs, row0, conv0, h0 = [], 0, 0, 0
    for S, TB in seqs_and_steps:
        segs.append(_Seg(S, TB, row0, conv0, h0))
        row0 += S * TB
        conv0 += (taps - 1) * S + S * TB
        h0 += S
    return tuple(segs), row0, conv0, h0


def _rms(x, w):
    ms = jnp.mean(x * x, axis=-1, keepdims=True)
    return x * lax.rsqrt(ms + EPS) * w


def _dot(a, b):
    return jnp.dot(a, b, preferred_element_type=F32)


def _gelu_tanh(x):
    c = 0.7978845608028654
    hx = 0.5 * x
    return hx + hx * jnp.tanh(x * (c + (c * 0.044715) * (x * x)))


def _resident(shape):
    zeros = (0,) * len(shape)
    return pl.BlockSpec(shape, lambda i: zeros, pipeline_mode=pl.Buffered(1))


def _params(vmem_limit_bytes=VMEM_LIMIT_BYTES):
    return pltpu.CompilerParams(dimension_semantics=("arbitrary",), vmem_limit_bytes=vmem_limit_bytes)


def _start_all(copies, priority=0):
    for cp in copies:
        cp.start(priority=priority)


def _wait_all(copies):
    for cp in copies:
        cp.wait()


def _rglru_branch_in(seg, xn, w_xr, region):
    P = (CONV_A - 1) * seg.S
    region[seg.conv0 + P:seg.conv0 + P + seg.rows, :] = _dot(xn, w_xr)


def _rglru_conv(seg, cw, cb, region, carry_to):
    S, R = seg.S, seg.rows
    P = (CONV_A - 1) * S
    c0 = seg.conv0
    xc = region[c0:c0 + R, :] * cw[0:1]
    for k in range(1, CONV_A):
        xc = xc + region[c0 + k * S:c0 + k * S + R, :] * cw[k:k + 1]
    xc = xc + cb
    if carry_to is not None:
        region[carry_to:carry_to + P, :] = region[c0 + R:c0 + R + P, :]
    return xc


def _rglru_gate_math(res, xc, half_gab, half_gxb, lam):
    half_c_sp = (-0.5 * RG_C) * jax.nn.softplus(-lam)
    tr = jnp.tanh(res[:, 0:COL_BLOCK] + half_gab)
    log_a = half_c_sp * tr + half_c_sp
    ig = 0.5 * jnp.tanh(res[:, COL_BLOCK:2 * COL_BLOCK] + half_gxb) + 0.5
    a = jnp.exp(log_a)
    m2 = jnp.tanh(log_a) * (-1.0 - a * a)
    u = jnp.where(m2 > 0.0, m2 * lax.rsqrt(m2), 0.0) * (ig * xc)
    return a, u


def _rglru_scan(seg, a, u, g, h_read, h_write, h_row0):
    S, TB = seg.S, seg.TB
    groups = S // SUBLANES
    pieces = [None] * (TB * groups)
    for c in range(groups):
        hrows = slice(h_row0 + c * SUBLANES, h_row0 + (c + 1) * SUBLANES)
        h = h_read[hrows, :]
        for t in range(TB):
            r = t * S + c * SUBLANES
            h = a[r:r + SUBLANES] * h + u[r:r + SUBLANES]
            pieces[t * groups + c] = h * g[r:r + SUBLANES]
        h_write[hrows, :] = h
    return jnp.concatenate(pieces, axis=0)


class _Part(NamedTuple):
    kernel: object
    args: list
    in_specs: list
    out_specs: list
    out_shape: list
    scratch_shapes: list


def _run_small_stage(mixer, ffn, rows, name):
    n_min, n_fin = len(mixer.args), len(ffn.args) - 1
    n_mout, n_fout = len(mixer.out_shape) - 1, len(ffn.out_shape)
    n_mscr, n_fscr = len(mixer.scratch_shapes), len(ffn.scratch_shapes)

    def kernel(*refs):
        refs = list(refs)
        take = lambda k: [refs.pop(0) for _ in range(k)]
        m_in, f_in, m_out, f_out, m_scr, f_scr = (take(k) for k in (n_min, n_fin, n_mout, n_fout, n_mscr, n_fscr))
        (x_mid,) = refs
        f_refs = [x_mid] + f_in + f_out + f_scr
        mixer.kernel(lambda: ffn.kernel(*f_refs, prefetch_only=True), *m_in, x_mid, *m_out, *m_scr)
        ffn.kernel(*f_refs, prefetched=True)

    outs = pl.pallas_call(
        kernel,
        grid=(1,),
        in_specs=mixer.in_specs + ffn.in_specs[1:],
        out_specs=mixer.out_specs[1:] + ffn.out_specs,
        out_shape=mixer.out_shape[1:] + ffn.out_shape,
        scratch_shapes=mixer.scratch_shapes + ffn.scratch_shapes + [pltpu.VMEM((rows, D_MODEL), F32)],
        compiler_params=_params(SMALL_STAGE_VMEM_LIMIT_BYTES),
        name=name,
    )(*mixer.args, *ffn.args[1:])
    return list(outs[:n_mout]), list(outs[n_mout:])


def _rglru_small_kernel(segs, sample_in, layer, j, after_fetch_issue, x_ref, *refs):
    D, C, NB = D_MODEL, COL_BLOCK, N_COL_BLOCKS
    if sample_in is not None:
        xs_hbm, refs = refs[0], refs[1:]
    n = len(segs)
    conv_in, h_in = refs[0:n], refs[n:2 * n]
    (npre_ref, npost_ref, cw_ref, cb_ref, gab_ref, gxb_ref, lam_ref,
     win_hbm, gaw_hbm, gxw_hbm, wout_hbm) = refs[2 * n:2 * n + 11]
    o_ref = refs[2 * n + 11]
    conv_out, h_out = refs[2 * n + 12:3 * n + 12], refs[3 * n + 12:4 * n + 12]
    wxr_b_hbm, wgate_b_hbm, gw_b_hbm, wout_b_hbm = refs[4 * n + 12:4 * n + 16]
    (xn_scr, xr_scr, col_f, sq_f, row_f, col_h, gw_h, row_h, in_sems, out_sems,
     *in_staging) = refs[4 * n + 16:]

    def fetch(b):
        cols = pl.ds(b * C, C)
        return [pltpu.make_async_copy(win_hbm.at[j, :, pl.ds(D + b * C, C)], col_f.at[b, 0], in_sems.at[b, 0]),
                pltpu.make_async_copy(win_hbm.at[j, :, cols], col_f.at[b, 1], in_sems.at[b, 1]),
                pltpu.make_async_copy(gaw_hbm.at[j, b], sq_f.at[b, 0], in_sems.at[b, 2]),
                pltpu.make_async_copy(gxw_hbm.at[j, b], sq_f.at[b, 1], in_sems.at[b, 3]),
                pltpu.make_async_copy(wout_hbm.at[j, cols, :], row_f.at[b], in_sems.at[b, 4])]

    def emit(b):
        k = b % OUT_SLOTS
        return [pltpu.make_async_copy(col_h.at[k, 0], wxr_b_hbm.at[b], out_sems.at[k, 0]),
                pltpu.make_async_copy(col_h.at[k, 1], wgate_b_hbm.at[b], out_sems.at[k, 1]),
                pltpu.make_async_copy(gw_h.at[k], gw_b_hbm.at[b], out_sems.at[k, 2]),
                pltpu.make_async_copy(row_h.at[k], wout_b_hbm.at[pl.ds(b * C, C), :], out_sems.at[k, 3])]

    if sample_in is None:
        read_x = lambda: x_ref[...]
    else:
        S, T = sample_in
        xs_scr, sems = in_staging
        x_copies = [pltpu.make_async_copy(xs_hbm.at[:, t, :], xs_scr.at[t], sems.at[t]) for t in range(T)]
        _start_all(x_copies)
        read_x = lambda: jnp.concatenate([x_ref[...], xs_scr[...].reshape(T * S, D)], axis=0)
    for b in range(NB):
        _start_all(fetch(b))
    after_fetch_issue()
    if sample_in is not None:
        _wait_all(x_copies)
    xn_scr[...] = _rms(read_x(), npre_ref[layer:layer + 1, :]).astype(BF16)
    xn = xn_scr[...]

    rows_of = lambda seg: slice(seg.row0, seg.row0 + seg.rows)
    y = None
    for b in range(NB):
        cols = slice(b * C, (b + 1) * C)
        _wait_all(fetch(b))
        w_xr, w_gate = col_f[b, 0].astype(BF16), col_f[b, 1].astype(BF16)
        gw = (0.5 * jnp.concatenate([sq_f[b, 0], sq_f[b, 1]], axis=1)).astype(BF16)
        w_out_rows = row_f[b].astype(BF16)
        k = b % OUT_SLOTS
        if b >= OUT_SLOTS:
            _wait_all(emit(b - OUT_SLOTS))
        col_h[k, 0], col_h[k, 1], gw_h[k], row_h[k] = w_xr, w_gate, gw, w_out_rows
        _start_all(emit(b), WEIGHT_EMIT_DMA_PRIORITY)

        xr = _dot(xn, w_xr)
        g = _gelu_tanh(_dot(xn, w_gate))
        xcs = []
        for seg, c_in, c_out in zip(segs, conv_in, conv_out):
            P = (CONV_A - 1) * seg.S
            xr_scr[seg.conv0:seg.conv0 + P, :] = c_in[:, cols]
            xr_scr[seg.conv0 + P:seg.conv0 + P + seg.rows, :] = xr[rows_of(seg)]
            xcs.append(_rglru_conv(seg, cw_ref[:, cols], cb_ref[j:j + 1, cols], xr_scr, seg.conv0))
            c_out[:, cols] = xr_scr[seg.conv0:seg.conv0 + P, :]
        xc = jnp.concatenate(xcs, axis=0)
        a, u = _rglru_gate_math(_dot(xc.astype(BF16), gw), xc, 0.5 * gab_ref[b:b + 1, :],
                                0.5 * gxb_ref[b:b + 1, :], lam_ref[j:j + 1, cols])
        hs = [_rglru_scan(seg, a[rows_of(seg)], u[rows_of(seg)], g[rows_of(seg)],
                          hi.at[:, cols], ho.at[:, cols], 0)
              for seg, hi, ho in zip(segs, h_in, h_out)]
        part = _dot(jnp.concatenate(hs, axis=0).astype(BF16), w_out_rows)
        y = part if y is None else y + part

    for b in range(max(NB - OUT_SLOTS, 0), NB):
        _wait_all(emit(b))
    o_ref[...] = read_x() + _rms(y, npost_ref[layer:layer + 1, :])


def _rglru_small_part(x, conv_in, h_in, layer, j, w, *, seqs_and_steps, x_sample=None):
    D, C, NB = D_MODEL, COL_BLOCK, N_COL_BLOCKS
    segs, R, conv_rows, _ = _segments(seqs_and_steps, CONV_A)
    n_layers, n_a = w["norm_mix_pre"].shape[0], w["rg_conv_b"].shape[0]
    hbm = pl.BlockSpec(memory_space=pl.ANY)
    layer_of = lambda arr: pl.BlockSpec((None,) + arr.shape[1:], lambda i: (j,) + (0,) * (arr.ndim - 1),
                                        pipeline_mode=pl.Buffered(1))
    if x_sample is None:
        x_args, x_specs, sample_in, staging = [x], [_resident(x.shape)], None, []
    else:
        S, T, _ = x_sample.shape
        x_args, x_specs, sample_in = [x, x_sample], [_resident(x.shape), hbm], (S, T)
        staging = [pltpu.VMEM((T, S, D), F32), pltpu.SemaphoreType.DMA((T,))]
    states = conv_in + h_in
    return _Part(
        kernel=functools.partial(_rglru_small_kernel, segs, sample_in, layer, j),
        args=x_args + states + [w["norm_mix_pre"], w["norm_mix_post"], w["rg_conv_w"], w["rg_conv_b"],
                                w["rg_gate_a_b"], w["rg_gate_x_b"], w["rg_lambda"], w["rg_w_in"],
                                w["rg_gate_a_w"], w["rg_gate_x_w"], w["rg_w_out"]],
        in_specs=x_specs + [_resident(s.shape) for s in states]
                 + [_resident((n_layers, D)), _resident((n_layers, D)), layer_of(w["rg_conv_w"]),
                    _resident((n_a, D)), layer_of(w["rg_gate_a_b"]), layer_of(w["rg_gate_x_b"]),
                    _resident((n_a, D)), hbm, hbm, hbm, hbm],
        out_specs=[pl.BlockSpec((R, D), lambda i: (0, 0))]
                  + [pl.BlockSpec(s.shape, lambda i: (0, 0)) for s in states] + [hbm] * 4,
        out_shape=[jax.ShapeDtypeStruct((R, D), F32)] + [jax.ShapeDtypeStruct(s.shape, F32) for s in states]
                  + [jax.ShapeDtypeStruct((NB, D, C), BF16), jax.ShapeDtypeStruct((NB, D, C), BF16),
                     jax.ShapeDtypeStruct((NB, C, 2 * C), BF16), jax.ShapeDtypeStruct((D, D), BF16)],
        scratch_shapes=[pltpu.VMEM((R, D), BF16),
                        pltpu.VMEM((conv_rows, C), F32),
                        pltpu.VMEM((NB, 2, D, C), F32),
                        pltpu.VMEM((NB, 2, C, C), F32),
                        pltpu.VMEM((NB, C, D), F32),
                        pltpu.VMEM((OUT_SLOTS, 2, D, C), BF16), pltpu.VMEM((OUT_SLOTS, C, 2 * C), BF16),
                        pltpu.VMEM((OUT_SLOTS, C, D), BF16),
                        pltpu.SemaphoreType.DMA((NB, 5)), pltpu.SemaphoreType.DMA((OUT_SLOTS, 4))] + staging)


def _rglru_kernel(seg, nblk, layer, j, batch_major_in,
                  x_ref, conv_in_ref, h_in_ref, npre_ref, npost_ref, wxr_ref, wgate_ref, cw_ref, cb_ref,
                  gw_ref, gab_ref, gxb_ref, lam_ref, wout_ref,
                  o_ref, conv_out_ref, h_out_ref, xn_scr, xr_scr, h_scr, *in_staging):
    S, TB, R = seg.S, seg.TB, seg.rows
    P = (CONV_A - 1) * S
    i = pl.program_id(0)
    slot = i % 2
    blocks = [slice(b * COL_BLOCK, (b + 1) * COL_BLOCK) for b in range(N_COL_BLOCKS)]

    if batch_major_in:
        xbuf, sems = in_staging

        def copies(which, step):
            t0 = pl.multiple_of(step * TB, TB)
            return [pltpu.make_async_copy(x_ref.at[s, pl.ds(t0, TB), :], xbuf.at[which, :, s, :],
                                          sems.at[which, s]) for s in range(S)]

    @pl.when(i == 0)
    def _():
        for b, cols in enumerate(blocks):
            xr_scr[b, 0:P, :] = conv_in_ref[:, cols]
            h_scr[b] = h_in_ref[:, cols]
        if batch_major_in:
            for c in copies(0, 0):
                c.start()

    if batch_major_in:
        @pl.when(i + 1 < nblk)
        def _():
            for c in copies(1 - slot, i + 1):
                c.start()

        for c in copies(slot, i):
            c.wait()
        read_x = lambda: xbuf[slot].reshape(R, D_MODEL)
    else:
        read_x = lambda: x_ref[...]

    xn_scr[...] = _rms(read_x(), npre_ref[layer:layer + 1, :]).astype(BF16)
    xn = xn_scr[...]

    gate_pre, xc, res = {}, {}, {}

    def input_matmuls(b):
        _rglru_branch_in(seg, xn, wxr_ref[b], xr_scr.at[b])
        gate_pre[b] = _dot(xn, wgate_ref[b])

    def conv_and_gate_matmul(b):
        xc[b] = _rglru_conv(seg, cw_ref[:, blocks[b]], cb_ref[j:j + 1, blocks[b]], xr_scr.at[b], 0)
        res[b] = _dot(xc[b].astype(BF16), gw_ref[b])

    y = None
    input_matmuls(0)
    conv_and_gate_matmul(0)
    input_matmuls(1)
    for b, cols in enumerate(blocks):
        if b + 1 < N_COL_BLOCKS:
            conv_and_gate_matmul(b + 1)
        if b + 2 < N_COL_BLOCKS:
            input_matmuls(b + 2)
        g = _gelu_tanh(gate_pre.pop(b))
        a, u = _rglru_gate_math(res.pop(b), xc.pop(b), 0.5 * gab_ref[b:b + 1, :], 0.5 * gxb_ref[b:b + 1, :],
                                lam_ref[j:j + 1, cols])
        hs = _rglru_scan(seg, a, u, g, h_scr.at[b], h_scr.at[b], 0)
        part = _dot(hs.astype(BF16), wout_ref[cols, :])
        y = part if y is None else y + part

    o_ref[...] = read_x() + _rms(y, npost_ref[layer:layer + 1, :])

    @pl.when(i == nblk - 1)
    def _():
        for b, cols in enumerate(blocks):
            conv_out_ref[:, cols] = xr_scr[b, 0:P, :]
            h_out_ref[:, cols] = h_scr[b]


def _rglru_layer(x, conv_in, h_in, layer, j, w, bf16, *, S, TB):
    D = D_MODEL
    batch_major_in = x.ndim == 3
    (seg,), R, conv_rows, _ = _segments(((S, TB),), CONV_A)
    nblk = x.size // D // R
    P = (CONV_A - 1) * S
    n_layers, n_a = w["norm_mix_pre"].shape[0], w["rg_conv_b"].shape[0]
    row_spec = pl.BlockSpec((R, D), lambda i: (i, 0))
    if batch_major_in:
        x_spec = pl.BlockSpec(memory_space=pl.ANY)
        staging = [pltpu.VMEM((2, TB, S, D), F32), pltpu.SemaphoreType.DMA((2, S))]
    else:
        x_spec, staging = row_spec, []
    layer_of = lambda arr: pl.BlockSpec((None,) + arr.shape[1:], lambda i: (j,) + (0,) * (arr.ndim - 1),
                                        pipeline_mode=pl.Buffered(1))
    return pl.pallas_call(
        functools.partial(_rglru_kernel, seg, nblk, layer, j, batch_major_in),
        grid=(nblk,),
        in_specs=[x_spec, _resident((P, D)), _resident((S, D)),
                  _resident((n_layers, D)), _resident((n_layers, D)),
                  _resident(bf16["w_xr"].shape), _resident(bf16["w_gate"].shape),
                  layer_of(w["rg_conv_w"]), _resident((n_a, D)), _resident(bf16["gw"].shape),
                  layer_of(w["rg_gate_a_b"]), layer_of(w["rg_gate_x_b"]), _resident((n_a, D)),
                  _resident((D, D))],
        out_specs=[row_spec, pl.BlockSpec((P, D), lambda i: (0, 0)), pl.BlockSpec((S, D), lambda i: (0, 0))],
        out_shape=[jax.ShapeDtypeStruct((nblk * R, D), F32), jax.ShapeDtypeStruct((P, D), F32),
                   jax.ShapeDtypeStruct((S, D), F32)],
        scratch_shapes=[pltpu.VMEM((R, D), BF16),
                        pltpu.VMEM((N_COL_BLOCKS, conv_rows, COL_BLOCK), F32),
                        pltpu.VMEM((N_COL_BLOCKS, S, COL_BLOCK), F32)]
                       + staging,
        compiler_params=_params(),
        name="rglru_mixer",
    )(x, conv_in, h_in, w["norm_mix_pre"], w["norm_mix_post"], bf16["w_xr"], bf16["w_gate"], w["rg_conv_w"],
      w["rg_conv_b"], bf16["gw"], w["rg_gate_a_b"], w["rg_gate_x_b"], w["rg_lambda"], bf16["w_out"])


def _sconv_conv(seg, cv, cw, region, rows=None):
    S = seg.S
    P = (CONV_B - 1) * S
    rows = rows or slice(0, seg.rows)
    c0, n = seg.conv0 + rows.start, rows.stop - rows.start
    region[c0 + P:c0 + P + n, :] = cv
    conv = region[c0:c0 + n, :] * cw[0:1]
    for k in range(1, CONV_B):
        conv = conv + region[c0 + k * S:c0 + k * S + n, :] * cw[k:k + 1]
    if rows.stop == seg.rows:
        region[seg.conv0:seg.conv0 + P, :] = region[seg.conv0 + seg.rows:seg.conv0 + seg.rows + P, :]
    return conv


def _sconv_small_kernel(segs, layer, j, after_fetch_issue, x_ref, *refs):
    D, C, NB = D_MODEL, COL_BLOCK, N_COL_BLOCKS
    n = len(segs)
    conv_in = refs[0:n]
    npre_ref, npost_ref, cw_ref, win_hbm, wout_hbm = refs[n:n + 5]
    o_ref = refs[n + 5]
    conv_out = refs[n + 6:2 * n + 6]
    wbg_b_hbm, wcg_b_hbm, wv_b_hbm, wout_b_hbm = refs[2 * n + 6:2 * n + 10]
    xn_scr, cv_scr, col_f, row_f, col_h, row_h, in_sems, out_sems = refs[2 * n + 10:]

    def fetch(b):
        return ([pltpu.make_async_copy(win_hbm.at[j, :, pl.ds(k * D + b * C, C)], col_f.at[b, k], in_sems.at[b, k])
                 for k in range(3)]
                + [pltpu.make_async_copy(wout_hbm.at[j, pl.ds(b * C, C), :], row_f.at[b], in_sems.at[b, 3])])

    def emit(b):
        s = b % OUT_SLOTS
        return ([pltpu.make_async_copy(col_h.at[s, k], dst.at[b], out_sems.at[s, k])
                 for k, dst in enumerate((wbg_b_hbm, wcg_b_hbm, wv_b_hbm))]
                + [pltpu.make_async_copy(row_h.at[s], wout_b_hbm.at[pl.ds(b * C, C), :], out_sems.at[s, 3])])

    for b in range(NB):
        _start_all(fetch(b))
    after_fetch_issue()
    xn_scr[...] = _rms(x_ref[...], npre_ref[layer:layer + 1, :]).astype(BF16)
    xn = xn_scr[...]

    y = None
    for b in range(NB):
        cols = slice(b * C, (b + 1) * C)
        _wait_all(fetch(b))
        w_bg, w_cg, w_v = (col_f[b, k].astype(BF16) for k in range(3))
        w_out_rows = row_f[b].astype(BF16)
        s = b % OUT_SLOTS
        if b >= OUT_SLOTS:
            _wait_all(emit(b - OUT_SLOTS))
        col_h[s, 0], col_h[s, 1], col_h[s, 2], row_h[s] = w_bg, w_cg, w_v, w_out_rows
        _start_all(emit(b), WEIGHT_EMIT_DMA_PRIORITY)

        cv = _dot(xn, w_cg) * _dot(xn, w_v)
        convs = []
        for seg, c_in, c_out in zip(segs, conv_in, conv_out):
            P = (CONV_B - 1) * seg.S
            cv_scr[seg.conv0:seg.conv0 + P, :] = c_in[:, cols]
            convs.append(_sconv_conv(seg, cv[seg.row0:seg.row0 + seg.rows], cw_ref[:, cols], cv_scr))
            c_out[:, cols] = cv_scr[seg.conv0:seg.conv0 + P, :]
        m = (_dot(xn, w_bg) * jnp.concatenate(convs, axis=0)).astype(BF16)
        part = _dot(m, w_out_rows)
        y = part if y is None else y + part

    for b in range(max(NB - OUT_SLOTS, 0), NB):
        _wait_all(emit(b))
    o_ref[...] = x_ref[...] + _rms(y, npost_ref[layer:layer + 1, :])


def _sconv_small_part(x, conv_in, layer, j, w, *, seqs_and_steps):
    R, D = x.shape
    C, NB = COL_BLOCK, N_COL_BLOCKS
    segs, _, conv_rows, _ = _segments(seqs_and_steps, CONV_B)
    n_layers = w["norm_mix_pre"].shape[0]
    hbm = pl.BlockSpec(memory_space=pl.ANY)
    return _Part(
        kernel=functools.partial(_sconv_small_kernel, segs, layer, j),
        args=[x] + conv_in + [w["norm_mix_pre"], w["norm_mix_post"], w["sc_conv_w"], w["sc_w_in"], w["sc_w_out"]],
        in_specs=[_resident((R, D))] + [_resident(c.shape) for c in conv_in]
                 + [_resident((n_layers, D)), _resident((n_layers, D)),
                    pl.BlockSpec((None, CONV_B, D), lambda i: (j, 0, 0), pipeline_mode=pl.Buffered(1)), hbm, hbm],
        out_specs=[pl.BlockSpec((R, D), lambda i: (0, 0))]
                  + [pl.BlockSpec(c.shape, lambda i: (0, 0)) for c in conv_in] + [hbm] * 4,
        out_shape=[jax.ShapeDtypeStruct((R, D), F32)] + [jax.ShapeDtypeStruct(c.shape, F32) for c in conv_in]
                  + [jax.ShapeDtypeStruct((NB, D, C), BF16)] * 3 + [jax.ShapeDtypeStruct((D, D), BF16)],
        scratch_shapes=[pltpu.VMEM((R, D), BF16),
                        pltpu.VMEM((conv_rows, C), F32),
                        pltpu.VMEM((NB, 3, D, C), F32),
                        pltpu.VMEM((NB, C, D), F32),
                        pltpu.VMEM((OUT_SLOTS, 3, D, C), BF16), pltpu.VMEM((OUT_SLOTS, C, D), BF16),
                        pltpu.SemaphoreType.DMA((NB, 4)), pltpu.SemaphoreType.DMA((OUT_SLOTS, 4))])


def _sconv_kernel(seg, nblk, layer, x_ref, conv_in_ref, npre_ref, npost_ref, wbg_ref, wcg_ref, wv_ref,
                  cw_ref, wout_ref, o_ref, conv_out_ref, xn_scr, cv_scr, m_scr):
    P = (CONV_B - 1) * seg.S
    i = pl.program_id(0)
    blocks = [slice(b * COL_BLOCK, (b + 1) * COL_BLOCK) for b in range(N_COL_BLOCKS)]

    @pl.when(i == 0)
    def _():
        for b, cols in enumerate(blocks):
            cv_scr[b, 0:P, :] = conv_in_ref[:, cols]

    R = seg.rows
    parts = [slice(k * R // MIXER_ROW_PARTS, (k + 1) * R // MIXER_ROW_PARTS) for k in range(MIXER_ROW_PARTS)]
    for rows in parts:
        xn_scr[rows, :] = _rms(x_ref[rows, :], npre_ref[layer:layer + 1, :]).astype(BF16)
    for rows in parts:
        xn = xn_scr[rows, :]
        for b, cols in enumerate(blocks):
            cv = _dot(xn, wcg_ref[b]) * _dot(xn, wv_ref[b])
            conv = _sconv_conv(seg, cv, cw_ref[:, cols], cv_scr.at[b], rows)
            m_scr[rows, cols] = (_dot(xn, wbg_ref[b]) * conv).astype(BF16)
    ys = [_dot(m_scr[rows, :], wout_ref[...]) for rows in parts]
    for rows, y in zip(parts, ys):
        o_ref[rows, :] = x_ref[rows, :] + _rms(y, npost_ref[layer:layer + 1, :])

    @pl.when(i == nblk - 1)
    def _():
        for b, cols in enumerate(blocks):
            conv_out_ref[:, cols] = cv_scr[b, 0:P, :]


def _sconv_layer(x, conv_in, layer, j, w, bf16, *, S, TB):
    N, D = x.shape
    (seg,), R, conv_rows, _ = _segments(((S, TB),), CONV_B)
    nblk = N // R
    P = (CONV_B - 1) * S
    n_layers = w["norm_mix_pre"].shape[0]
    row_spec = pl.BlockSpec((R, D), lambda i: (i, 0))
    return pl.pallas_call(
        functools.partial(_sconv_kernel, seg, nblk, layer),
        grid=(nblk,),
        in_specs=[row_spec, _resident((P, D)), _resident((n_layers, D)), _resident((n_layers, D)),
                  _resident(bf16["w_bg"].shape), _resident(bf16["w_cg"].shape), _resident(bf16["w_v"].shape),
                  pl.BlockSpec((None, CONV_B, D), lambda i: (j, 0, 0), pipeline_mode=pl.Buffered(1)),
                  _resident((D, D))],
        out_specs=[row_spec, pl.BlockSpec((P, D), lambda i: (0, 0))],
        out_shape=[jax.ShapeDtypeStruct((N, D), F32), jax.ShapeDtypeStruct((P, D), F32)],
        scratch_shapes=[pltpu.VMEM((R, D), BF16),
                        pltpu.VMEM((N_COL_BLOCKS, conv_rows, COL_BLOCK), F32),
                        pltpu.VMEM((R, D), BF16)],
        compiler_params=_params(),
        name="sconv_mixer",
    )(x, conv_in, w["norm_mix_pre"], w["norm_mix_post"], bf16["w_bg"], bf16["w_cg"], bf16["w_v"],
      w["sc_conv_w"], bf16["w_out"])


def _swiglu(g, u):
    return (g * jax.nn.sigmoid(g) * u).astype(BF16)


def _ffn_stream_kernel(layer, nchunk, sample_out, x_ref, npre_ref, npost_ref, wg_hbm, wu_hbm, wd_hbm,
                       o_ref, wg_b_hbm, wu_b_hbm, wd_b_hbm,
                       xn_scr, wg_f, wu_f, wd_f, wg_h, wu_h, wd_h, in_sems, out_sems, *out_staging,
                       prefetch_only=False, prefetched=False):
    F, NB = FF_CHUNK, STREAM_BUFFERS

    def fetch(c):
        k, cols = c % NB, pl.ds(c * F, F)
        return [pltpu.make_async_copy(wg_hbm.at[layer, :, cols], wg_f.at[k], in_sems.at[0, k]),
                pltpu.make_async_copy(wu_hbm.at[layer, :, cols], wu_f.at[k], in_sems.at[1, k]),
                pltpu.make_async_copy(wd_hbm.at[layer, cols, :], wd_f.at[k], in_sems.at[2, k])]

    def emit(c):
        k = c % NB
        return [pltpu.make_async_copy(wg_h.at[k], wg_b_hbm.at[c], out_sems.at[0, k]),
                pltpu.make_async_copy(wu_h.at[k], wu_b_hbm.at[c], out_sems.at[1, k]),
                pltpu.make_async_copy(wd_h.at[k], wd_b_hbm.at[pl.ds(c * F, F), :], out_sems.at[2, k])]

    if not prefetched:
        for c in range(min(NB, nchunk)):
            _start_all(fetch(c))
    if prefetch_only:
        return
    xn_scr[...] = _rms(x_ref[...], npre_ref[layer:layer + 1, :]).astype(BF16)
    xn = xn_scr[...]

    y = None
    for c in range(nchunk):
        k = c % NB
        for cp in fetch(c):
            cp.wait()
        if c >= NB:
            for cp in emit(c - NB):
                cp.wait()
        wg, wu, wd = wg_f[k].astype(BF16), wu_f[k].astype(BF16), wd_f[k].astype(BF16)
        wg_h[k], wu_h[k], wd_h[k] = wg, wu, wd
        _start_all(emit(c), WEIGHT_EMIT_DMA_PRIORITY)
        if c + NB < nchunk:
            for cp in fetch(c + NB):
                cp.start()
        part = _dot(_swiglu(_dot(xn, wg), _dot(xn, wu)), wd)
        y = part if y is None else y + part
    for c in range(max(nchunk - NB, 0), nchunk):
        for cp in emit(c):
            cp.wait()

    out = x_ref[...] + _rms(y, npost_ref[layer:layer + 1, :])
    if sample_out is None:
        o_ref[...] = out
    else:
        row0, S, T = sample_out
        obuf, sems = out_staging
        obuf[...] = out[row0:row0 + T * S].reshape(T, S, out.shape[-1])
        copies = [pltpu.make_async_copy(obuf.at[t], o_ref.at[:, t, :], sems.at[t]) for t in range(T)]
        for cp in copies:
            cp.start()
        for cp in copies:
            cp.wait()


def _ffn_small_part(R, layer, w, *, sample_out=None):
    D = D_MODEL
    F, NB = FF_CHUNK, STREAM_BUFFERS
    n_layers = w["norm_ffn_pre"].shape[0]
    nchunk = D_FF // F
    hbm = pl.BlockSpec(memory_space=pl.ANY)
    if sample_out is None:
        o_spec, o_shape, staging = pl.BlockSpec((R, D), lambda i: (0, 0)), jax.ShapeDtypeStruct((R, D), F32), []
    else:
        _, S, T = sample_out
        o_spec, o_shape = hbm, jax.ShapeDtypeStruct((S, T, D), F32)
        staging = [pltpu.VMEM((T, S, D), F32), pltpu.SemaphoreType.DMA((T,))]
    return _Part(
        kernel=functools.partial(_ffn_stream_kernel, layer, nchunk, sample_out),
        args=[None, w["norm_ffn_pre"], w["norm_ffn_post"], w["ffn_w_gate"], w["ffn_w_up"], w["ffn_w_down"]],
        in_specs=[_resident((R, D)), _resident((n_layers, D)), _resident((n_layers, D)), hbm, hbm, hbm],
        out_specs=[o_spec, hbm, hbm, hbm],
        out_shape=[o_shape, jax.ShapeDtypeStruct((nchunk, D, F), BF16),
                   jax.ShapeDtypeStruct((nchunk, D, F), BF16), jax.ShapeDtypeStruct((D_FF, D), BF16)],
        scratch_shapes=[pltpu.VMEM((R, D), BF16),
                        pltpu.VMEM((NB, D, F), F32), pltpu.VMEM((NB, D, F), F32), pltpu.VMEM((NB, F, D), F32),
                        pltpu.VMEM((NB, D, F), BF16), pltpu.VMEM((NB, D, F), BF16), pltpu.VMEM((NB, F, D), BF16),
                        pltpu.SemaphoreType.DMA((3, NB)), pltpu.SemaphoreType.DMA((3, NB))] + staging)


def _ffn_kernel(layer, nblk, batch_major_out, x_ref, npre_ref, npost_ref, wg_ref, wu_ref, wd_ref, o_ref,
                xn_scr, act_scr, *out_staging):
    R = x_ref.shape[0]
    i = pl.program_id(0)
    slot = i % 2
    if batch_major_out:
        obuf, sems = out_staging
        S, TB = o_ref.shape[0], R // o_ref.shape[0]

        def copies(which, step):
            t0 = pl.multiple_of(step * TB, TB)
            return [pltpu.make_async_copy(obuf.at[which, :, s, :],
                                          o_ref.at[s, pl.ds(t0, TB), :], sems.at[which, s]) for s in range(S)]

        @pl.when(i >= 2)
        def _():
            for c in copies(slot, i - 2):
                c.wait()

    parts = [slice(k * R // FFN_ROW_PARTS, (k + 1) * R // FFN_ROW_PARTS) for k in range(FFN_ROW_PARTS)]
    for rows in parts:
        xn_scr[rows, :] = _rms(x_ref[rows, :], npre_ref[layer:layer + 1, :]).astype(BF16)
    for rows in parts:
        xn = xn_scr[rows, :]
        for c in range(D_FF // FF_CHUNK):
            cols = slice(c * FF_CHUNK, (c + 1) * FF_CHUNK)
            act_scr[rows, cols] = _swiglu(_dot(xn, wg_ref[c]), _dot(xn, wu_ref[c]))
    ys = [_dot(act_scr[rows, :], wd_ref[...]) for rows in parts]
    for rows, y in zip(parts, ys):
        out = x_ref[rows, :] + _rms(y, npost_ref[layer:layer + 1, :])
        if batch_major_out:
            obuf[slot, rows.start // S:rows.stop // S] = out.reshape(-1, S, out.shape[-1])
        else:
            o_ref[rows, :] = out

    if batch_major_out:
        for c in copies(slot, i):
            c.start()

        @pl.when(i == nblk - 1)
        def _():
            if nblk >= 2:
                for c in copies(1 - slot, i - 1):
                    c.wait()
            for c in copies(slot, i):
                c.wait()


def _ffn_layer(x, layer, w, bf16, *, R, batch_major_seqs=None):
    N, D = x.shape
    n_layers = w["norm_ffn_pre"].shape[0]
    nblk = N // R
    row_spec = pl.BlockSpec((R, D), lambda i: (i, 0))
    if batch_major_seqs is None:
        out_spec, out_shape, staging = row_spec, jax.ShapeDtypeStruct((N, D), F32), []
    else:
        S = batch_major_seqs
        out_spec = pl.BlockSpec(memory_space=pl.ANY)
        out_shape = jax.ShapeDtypeStruct((S, N // S, D), F32)
        staging = [pltpu.VMEM((2, R // S, S, D), F32), pltpu.SemaphoreType.DMA((2, S))]
    return pl.pallas_call(
        functools.partial(_ffn_kernel, layer, nblk, batch_major_seqs is not None),
        grid=(nblk,),
        in_specs=[row_spec, _resident((n_layers, D)), _resident((n_layers, D)),
                  _resident(bf16["wg"].shape), _resident(bf16["wu"].shape), _resident((D_FF, D))],
        out_specs=out_spec,
        out_shape=out_shape,
        scratch_shapes=[pltpu.VMEM((R, D), BF16), pltpu.VMEM((R, D_FF), BF16)] + staging,
        compiler_params=_params(),
        name="swiglu_ffn",
    )(x, w["norm_ffn_pre"], w["norm_ffn_post"], bf16["wg"], bf16["wu"], bf16["wd"])


def _to_time_major(a):
    S, K, D = a.shape
    return jnp.swapaxes(a, 0, 1).reshape(K * S, D)


def _from_time_major(a, S):
    KS, D = a.shape
    return jnp.swapaxes(a.reshape(KS // S, S, D), 0, 1)


def kernel(x_prompt, x_sample, state_rglru_conv, state_rglru_h, state_sconv, meta_tokens, norm_mix_pre, norm_mix_post, norm_ffn_pre, norm_ffn_post, rg_w_in, rg_conv_w, rg_conv_b, rg_gate_a_w, rg_gate_a_b, rg_gate_x_w, rg_gate_x_b, rg_lambda, rg_w_out, sc_w_in, sc_conv_w, sc_w_out, ffn_w_gate, ffn_w_up, ffn_w_down):
    D = D_MODEL
    depth = norm_mix_pre.shape[0]
    batch, seq, _ = x_prompt.shape
    dec_batch, dec_seq, _ = x_sample.shape
    w = dict(norm_mix_pre=norm_mix_pre, norm_mix_post=norm_mix_post, norm_ffn_pre=norm_ffn_pre,
             norm_ffn_post=norm_ffn_post, rg_w_in=rg_w_in, rg_conv_w=rg_conv_w, rg_conv_b=rg_conv_b,
             rg_gate_a_w=rg_gate_a_w, rg_gate_a_b=rg_gate_a_b, rg_gate_x_w=rg_gate_x_w, rg_gate_x_b=rg_gate_x_b,
             rg_lambda=rg_lambda, rg_w_out=rg_w_out, sc_w_in=sc_w_in, sc_conv_w=sc_conv_w, sc_w_out=sc_w_out,
             ffn_w_gate=ffn_w_gate, ffn_w_up=ffn_w_up, ffn_w_down=ffn_w_down)

    x = jnp.broadcast_to(meta_tokens[:, None, :], (N_META, batch, D)).reshape(N_META * batch, D)
    small = ((batch, N_META), (dec_batch, dec_seq))
    small_rows = N_META * batch + dec_batch * dec_seq
    mixer_bf16, ffn_bf16 = [], []
    rg_conv_s, rg_h_s, sc_s = [], [], []
    for i in range(depth):
        j = i // 2
        sample_out = (N_META * batch, dec_batch, dec_seq) if i == depth - 1 else None
        ffn = _ffn_small_part(small_rows, i, w, sample_out=sample_out)
        if i % 2 == 0:
            conv0 = [jnp.zeros(((CONV_A - 1) * batch, D), F32), _to_time_major(state_rglru_conv[j])]
            h0 = [jnp.zeros((batch, D), F32), state_rglru_h[j]]
            mixer = _rglru_small_part(x, conv0, h0, i, j, w, seqs_and_steps=small,
                                      x_sample=x_sample if i == 0 else None)
            (cs0, cs1, hs0, hs1, *wb), (x, *fb) = _run_small_stage(mixer, ffn, small_rows, "rglru_ffn_small")
            rg_conv_s.append([cs0, cs1])
            rg_h_s.append([hs0, hs1])
            mixer_bf16.append(dict(zip(("w_xr", "w_gate", "gw", "w_out"), wb)))
        else:
            conv0 = [jnp.zeros(((CONV_B - 1) * batch, D), F32), _to_time_major(state_sconv[j])]
            mixer = _sconv_small_part(x, conv0, i, j, w, seqs_and_steps=small)
            (cs0, cs1, *wb), (x, *fb) = _run_small_stage(mixer, ffn, small_rows, "sconv_ffn_small")
            sc_s.append([cs0, cs1])
            mixer_bf16.append(dict(zip(("w_bg", "w_cg", "w_v", "w_out"), wb)))
        ffn_bf16.append(dict(zip(("wg", "wu", "wd"), fb)))
    y_sample = x

    x = x_prompt
    rg_conv_p, rg_h_p, sc_p = [], [], []
    for i in range(depth):
        j = i // 2
        if i % 2 == 0:
            x, cb, hT = _rglru_layer(x, rg_conv_s[j][0], rg_h_s[j][0], i, j, w, mixer_bf16[i], S=batch, TB=PROMPT_TB)
            rg_conv_p.append(cb)
            rg_h_p.append(hT)
        else:
            x, cb = _sconv_layer(x, sc_s[j][0], i, j, w, mixer_bf16[i], S=batch, TB=PROMPT_TB)
            sc_p.append(cb)
        x = _ffn_layer(x, i, w, ffn_bf16[i], R=batch * PROMPT_TB,
                       batch_major_seqs=batch if i == depth - 1 else None)
    y_prompt = x

    return (y_prompt, y_sample,
            jnp.stack([_from_time_major(c, batch) for c in rg_conv_p]), jnp.stack(rg_h_p),
            jnp.stack([_from_time_major(c, batch) for c in sc_p]),
            jnp.stack([_from_time_major(c[1], dec_batch) for c in rg_conv_s]), jnp.stack([h[1] for h in rg_h_s]),
            jnp.stack([_from_time_major(c[1], dec_batch) for c in sc_s]))
```

```python
import functools
from typing import NamedTuple

import jax
import jax.numpy as jnp
from jax import lax
from jax.experimental import pallas as pl
from jax.experimental.pallas import tpu as pltpu

D_MODEL = 1024
D_FF = 2816
N_META = 16
COL_BLOCK = 256
N_COL_BLOCKS = D_MODEL // COL_BLOCK
CONV_A = 4
CONV_B = 3
RG_C = 8.0
EPS = 1e-6

SUBLANES = 8
FF_CHUNK = 256
STREAM_BUFFERS = 3
OUT_SLOTS = 2
PROMPT_TB = 128
FFN_ROW_PARTS = 4
MIXER_ROW_PARTS = 4
VMEM_LIMIT_BYTES = 56 * 1024 * 1024
SMALL_STAGE_VMEM_LIMIT_BYTES = 60 * 1024 * 1024

F32 = jnp.float32
BF16 = jnp.bfloat16


class _Seg(NamedTuple):
    S: int
    TB: int
    row0: int
    conv0: int
    h0: int

    @property
    def rows(self):
        return self.S * self.TB


def _segments(seqs_and_steps, taps):
    segs, row0, conv0, h0 = [], 0, 0, 0
    for S, TB in seqs_and_steps:
        segs.append(_Seg(S, TB, row0, conv0, h0))
        row0 += S * TB
        conv0 += (taps - 1) * S + S * TB
        h0 += S
    return tuple(segs), row0, conv0, h0


def _rms(x, w):
    ms = jnp.mean(x * x, axis=-1, keepdims=True)
    return x * lax.rsqrt(ms + EPS) * w


def _dot(a, b):
    return jnp.dot(a, b, preferred_element_type=F32)


def _gelu_tanh(x):
    c = 0.7978845608028654
    hx = 0.5 * x
    return hx + hx * jnp.tanh(x * (c + (c * 0.044715) * (x * x)))


def _resident(shape):
    zeros = (0,) * len(shape)
    return pl.BlockSpec(shape, lambda i: zeros, pipeline_mode=pl.Buffered(1))


def _params(vmem_limit_bytes=VMEM_LIMIT_BYTES):
    return pltpu.CompilerParams(dimension_semantics=("arbitrary",), vmem_limit_bytes=vmem_limit_bytes)


def _start_all(copies):
    for cp in copies:
        cp.start()


def _wait_all(copies):
    for cp in copies:
        cp.wait()


def _rglru_branch_in(seg, xn, w_xr, region):
    P = (CONV_A - 1) * seg.S
    region[seg.conv0 + P:seg.conv0 + P + seg.rows, :] = _dot(xn, w_xr)


def _rglru_conv(seg, cw, cb, region, carry_to):
    S, R = seg.S, seg.rows
    P = (CONV_A - 1) * S
    c0 = seg.conv0
    xc = region[c0:c0 + R, :] * cw[0:1]
    for k in range(1, CONV_A):
        xc = xc + region[c0 + k * S:c0 + k * S + R, :] * cw[k:k + 1]
    xc = xc + cb
    if carry_to is not None:
        region[carry_to:carry_to + P, :] = region[c0 + R:c0 + R + P, :]
    return xc


def _rglru_gate_math(res, xc, half_gab, half_gxb, lam):
    half_c_sp = (-0.5 * RG_C) * jax.nn.softplus(-lam)
    tr = jnp.tanh(res[:, 0:COL_BLOCK] + half_gab)
    log_a = half_c_sp * tr + half_c_sp
    ig = 0.5 * jnp.tanh(res[:, COL_BLOCK:2 * COL_BLOCK] + half_gxb) + 0.5
    a = jnp.exp(log_a)
    m2 = jnp.tanh(log_a) * (-1.0 - a * a)
    u = jnp.where(m2 > 0.0, m2 * lax.rsqrt(m2), 0.0) * (ig * xc)
    return a, u


def _rglru_scan(seg, a, u, g, h_read, h_write, h_row0):
    S, TB = seg.S, seg.TB
    groups = S // SUBLANES
    pieces = [None] * (TB * groups)
    for c in range(groups):
        hrows = slice(h_row0 + c * SUBLANES, h_row0 + (c + 1) * SUBLANES)
        h = h_read[hrows, :]
        for t in range(TB):
            r = t * S + c * SUBLANES
            h = a[r:r + SUBLANES] * h + u[r:r + SUBLANES]
            pieces[t * groups + c] = h * g[r:r + SUBLANES]
        h_write[hrows, :] = h
    return jnp.concatenate(pieces, axis=0)


class _Part(NamedTuple):
    kernel: object
    args: list
    in_specs: list
    out_specs: list
    out_shape: list
    scratch_shapes: list


def _run_small_stage(mixer, ffn, rows, name):
    n_min, n_fin = len(mixer.args), len(ffn.args) - 1
    n_mout, n_fout = len(mixer.out_shape) - 1, len(ffn.out_shape)
    n_mscr, n_fscr = len(mixer.scratch_shapes), len(ffn.scratch_shapes)

    def kernel(*refs):
        refs = list(refs)
        take = lambda k: [refs.pop(0) for _ in range(k)]
        m_in, f_in, m_out, f_out, m_scr, f_scr = (take(k) for k in (n_min, n_fin, n_mout, n_fout, n_mscr, n_fscr))
        (x_mid,) = refs
        f_refs = [x_mid] + f_in + f_out + f_scr
        mixer.kernel(lambda: ffn.kernel(*f_refs, prefetch_only=True), *m_in, x_mid, *m_out, *m_scr)
        ffn.kernel(*f_refs, prefetched=True)

    outs = pl.pallas_call(
        kernel,
        grid=(1,),
        in_specs=mixer.in_specs + ffn.in_specs[1:],
        out_specs=mixer.out_specs[1:] + ffn.out_specs,
        out_shape=mixer.out_shape[1:] + ffn.out_shape,
        scratch_shapes=mixer.scratch_shapes + ffn.scratch_shapes + [pltpu.VMEM((rows, D_MODEL), F32)],
        compiler_params=_params(SMALL_STAGE_VMEM_LIMIT_BYTES),
        name=name,
    )(*mixer.args, *ffn.args[1:])
    return list(outs[:n_mout]), list(outs[n_mout:])


def _rglru_small_kernel(segs, sample_in, layer, j, after_fetch_issue, x_ref, *refs):
    D, C, NB = D_MODEL, COL_BLOCK, N_COL_BLOCKS
    if sample_in is not None:
        xs_hbm, refs = refs[0], refs[1:]
    n = len(segs)
    conv_in, h_in = refs[0:n], refs[n:2 * n]
    (npre_ref, npost_ref, cw_ref, cb_ref, gab_ref, gxb_ref, lam_ref,
     win_hbm, gaw_hbm, gxw_hbm, wout_hbm) = refs[2 * n:2 * n + 11]
    o_ref = refs[2 * n + 11]
    conv_out, h_out = refs[2 * n + 12:3 * n + 12], refs[3 * n + 12:4 * n + 12]
    wxr_b_hbm, wgate_b_hbm, gw_b_hbm, wout_b_hbm = refs[4 * n + 12:4 * n + 16]
    (xn_scr, xr_scr, col_f, sq_f, row_f, col_h, gw_h, row_h, in_sems, out_sems,
     *in_staging) = refs[4 * n + 16:]

    def fetch(b):
        cols = pl.ds(b * C, C)
        return [pltpu.make_async_copy(win_hbm.at[j, :, pl.ds(D + b * C, C)], col_f.at[b, 0], in_sems.at[b, 0]),
                pltpu.make_async_copy(win_hbm.at[j, :, cols], col_f.at[b, 1], in_sems.at[b, 1]),
                pltpu.make_async_copy(gaw_hbm.at[j, b], sq_f.at[b, 0], in_sems.at[b, 2]),
                pltpu.make_async_copy(gxw_hbm.at[j, b], sq_f.at[b, 1], in_sems.at[b, 3]),
                pltpu.make_async_copy(wout_hbm.at[j, cols, :], row_f.at[b], in_sems.at[b, 4])]

    def emit(b):
        k = b % OUT_SLOTS
        return [pltpu.make_async_copy(col_h.at[k, 0], wxr_b_hbm.at[b], out_sems.at[k, 0]),
                pltpu.make_async_copy(col_h.at[k, 1], wgate_b_hbm.at[b], out_sems.at[k, 1]),
                pltpu.make_async_copy(gw_h.at[k], gw_b_hbm.at[b], out_sems.at[k, 2]),
                pltpu.make_async_copy(row_h.at[k], wout_b_hbm.at[pl.ds(b * C, C), :], out_sems.at[k, 3])]

    if sample_in is None:
        read_x = lambda: x_ref[...]
    else:
        S, T = sample_in
        xs_scr, sems = in_staging
        x_copies = [pltpu.make_async_copy(xs_hbm.at[:, t, :], xs_scr.at[t], sems.at[t]) for t in range(T)]
        _start_all(x_copies)
        read_x = lambda: jnp.concatenate([x_ref[...], xs_scr[...].reshape(T * S, D)], axis=0)
    for b in range(NB):
        _start_all(fetch(b))
    after_fetch_issue()
    if sample_in is not None:
        _wait_all(x_copies)
    xn_scr[...] = _rms(read_x(), npre_ref[layer:layer + 1, :]).astype(BF16)
    xn = xn_scr[...]

    rows_of = lambda seg: slice(seg.row0, seg.row0 + seg.rows)
    y = None
    for b in range(NB):
        cols = slice(b * C, (b + 1) * C)
        _wait_all(fetch(b))
        w_xr, w_gate = col_f[b, 0].astype(BF16), col_f[b, 1].astype(BF16)
        gw = (0.5 * jnp.concatenate([sq_f[b, 0], sq_f[b, 1]], axis=1)).astype(BF16)
        w_out_rows = row_f[b].astype(BF16)
        k = b % OUT_SLOTS
        if b >= OUT_SLOTS:
            _wait_all(emit(b - OUT_SLOTS))
        col_h[k, 0], col_h[k, 1], gw_h[k], row_h[k] = w_xr, w_gate, gw, w_out_rows
        _start_all(emit(b))

        xr = _dot(xn, w_xr)
        g = _gelu_tanh(_dot(xn, w_gate))
        xcs = []
        for seg, c_in, c_out in zip(segs, conv_in, conv_out):
            P = (CONV_A - 1) * seg.S
            xr_scr[seg.conv0:seg.conv0 + P, :] = c_in[:, cols]
            xr_scr[seg.conv0 + P:seg.conv0 + P + seg.rows, :] = xr[rows_of(seg)]
            xcs.append(_rglru_conv(seg, cw_ref[:, cols], cb_ref[j:j + 1, cols], xr_scr, seg.conv0))
            c_out[:, cols] = xr_scr[seg.conv0:seg.conv0 + P, :]
        xc = jnp.concatenate(xcs, axis=0)
        a, u = _rglru_gate_math(_dot(xc.astype(BF16), gw), xc, 0.5 * gab_ref[b:b + 1, :],
                                0.5 * gxb_ref[b:b + 1, :], lam_ref[j:j + 1, cols])
        hs = [_rglru_scan(seg, a[rows_of(seg)], u[rows_of(seg)], g[rows_of(seg)],
                          hi.at[:, cols], ho.at[:, cols], 0)
              for seg, hi, ho in zip(segs, h_in, h_out)]
        part = _dot(jnp.concatenate(hs, axis=0).astype(BF16), w_out_rows)
        y = part if y is None else y + part

    for b in range(max(NB - OUT_SLOTS, 0), NB):
        _wait_all(emit(b))
    o_ref[...] = read_x() + _rms(y, npost_ref[layer:layer + 1, :])


def _rglru_small_part(x, conv_in, h_in, layer, j, w, *, seqs_and_steps, x_sample=None):
    D, C, NB = D_MODEL, COL_BLOCK, N_COL_BLOCKS
    segs, R, conv_rows, _ = _segments(seqs_and_steps, CONV_A)
    n_layers, n_a = w["norm_mix_pre"].shape[0], w["rg_conv_b"].shape[0]
    hbm = pl.BlockSpec(memory_space=pl.ANY)
    layer_of = lambda arr: pl.BlockSpec((None,) + arr.shape[1:], lambda i: (j,) + (0,) * (arr.ndim - 1),
                                        pipeline_mode=pl.Buffered(1))
    if x_sample is None:
        x_args, x_specs, sample_in, staging = [x], [_resident(x.shape)], None, []
    else:
        S, T, _ = x_sample.shape
        x_args, x_specs, sample_in = [x, x_sample], [_resident(x.shape), hbm], (S, T)
        staging = [pltpu.VMEM((T, S, D), F32), pltpu.SemaphoreType.DMA((T,))]
    states = conv_in + h_in
    return _Part(
        kernel=functools.partial(_rglru_small_kernel, segs, sample_in, layer, j),
        args=x_args + states + [w["norm_mix_pre"], w["norm_mix_post"], w["rg_conv_w"], w["rg_conv_b"],
                                w["rg_gate_a_b"], w["rg_gate_x_b"], w["rg_lambda"], w["rg_w_in"],
                                w["rg_gate_a_w"], w["rg_gate_x_w"], w["rg_w_out"]],
        in_specs=x_specs + [_resident(s.shape) for s in states]
                 + [_resident((n_layers, D)), _resident((n_layers, D)), layer_of(w["rg_conv_w"]),
                    _resident((n_a, D)), layer_of(w["rg_gate_a_b"]), layer_of(w["rg_gate_x_b"]),
                    _resident((n_a, D)), hbm, hbm, hbm, hbm],
        out_specs=[pl.BlockSpec((R, D), lambda i: (0, 0))]
                  + [pl.BlockSpec(s.shape, lambda i: (0, 0)) for s in states] + [hbm] * 4,
        out_shape=[jax.ShapeDtypeStruct((R, D), F32)] + [jax.ShapeDtypeStruct(s.shape, F32) for s in states]
                  + [jax.ShapeDtypeStruct((NB, D, C), BF16), jax.ShapeDtypeStruct((NB, D, C), BF16),
                     jax.ShapeDtypeStruct((NB, C, 2 * C), BF16), jax.ShapeDtypeStruct((D, D), BF16)],
        scratch_shapes=[pltpu.VMEM((R, D), BF16),
                        pltpu.VMEM((conv_rows, C), F32),
                        pltpu.VMEM((NB, 2, D, C), F32),
                        pltpu.VMEM((NB, 2, C, C), F32),
                        pltpu.VMEM((NB, C, D), F32),
                        pltpu.VMEM((OUT_SLOTS, 2, D, C), BF16), pltpu.VMEM((OUT_SLOTS, C, 2 * C), BF16),
                        pltpu.VMEM((OUT_SLOTS, C, D), BF16),
                        pltpu.SemaphoreType.DMA((NB, 5)), pltpu.SemaphoreType.DMA((OUT_SLOTS, 4))] + staging)


def _rglru_kernel(seg, nblk, layer, j, batch_major_in,
                  x_ref, conv_in_ref, h_in_ref, npre_ref, npost_ref, wxr_ref, wgate_ref, cw_ref, cb_ref,
                  gw_ref, gab_ref, gxb_ref, lam_ref, wout_ref,
                  o_ref, conv_out_ref, h_out_ref, xn_scr, xr_scr, h_scr, *in_staging):
    S, TB, R = seg.S, seg.TB, seg.rows
    P = (CONV_A - 1) * S
    i = pl.program_id(0)
    slot = i % 2
    blocks = [slice(b * COL_BLOCK, (b + 1) * COL_BLOCK) for b in range(N_COL_BLOCKS)]

    if batch_major_in:
        xbuf, sems = in_staging

        def copies(which, step):
            t0 = pl.multiple_of(step * TB, TB)
            return [pltpu.make_async_copy(x_ref.at[s, pl.ds(t0, TB), :], xbuf.at[which, :, s, :],
                                          sems.at[which, s]) for s in range(S)]

    @pl.when(i == 0)
    def _():
        for b, cols in enumerate(blocks):
            xr_scr[b, 0:P, :] = conv_in_ref[:, cols]
            h_scr[b] = h_in_ref[:, cols]
        if batch_major_in:
            for c in copies(0, 0):
                c.start()

    if batch_major_in:
        @pl.when(i + 1 < nblk)
        def _():
            for c in copies(1 - slot, i + 1):
                c.start()

        for c in copies(slot, i):
            c.wait()
        read_x = lambda: xbuf[slot].reshape(R, D_MODEL)
    else:
        read_x = lambda: x_ref[...]

    xn_scr[...] = _rms(read_x(), npre_ref[layer:layer + 1, :]).astype(BF16)
    xn = xn_scr[...]

    gate_pre, xc, res = {}, {}, {}

    def input_matmuls(b):
        _rglru_branch_in(seg, xn, wxr_ref[b], xr_scr.at[b])
        gate_pre[b] = _dot(xn, wgate_ref[b])

    def conv_and_gate_matmul(b):
        xc[b] = _rglru_conv(seg, cw_ref[:, blocks[b]], cb_ref[j:j + 1, blocks[b]], xr_scr.at[b], 0)
        res[b] = _dot(xc[b].astype(BF16), gw_ref[b])

    y = None
    input_matmuls(0)
    conv_and_gate_matmul(0)
    input_matmuls(1)
    for b, cols in enumerate(blocks):
        if b + 1 < N_COL_BLOCKS:
            conv_and_gate_matmul(b + 1)
        if b + 2 < N_COL_BLOCKS:
            input_matmuls(b + 2)
        g = _gelu_tanh(gate_pre.pop(b))
        a, u = _rglru_gate_math(res.pop(b), xc.pop(b), 0.5 * gab_ref[b:b + 1, :], 0.5 * gxb_ref[b:b + 1, :],
                                lam_ref[j:j + 1, cols])
        hs = _rglru_scan(seg, a, u, g, h_scr.at[b], h_scr.at[b], 0)
        part = _dot(hs.astype(BF16), wout_ref[cols, :])
        y = part if y is None else y + part

    o_ref[...] = read_x() + _rms(y, npost_ref[layer:layer + 1, :])

    @pl.when(i == nblk - 1)
    def _():
        for b, cols in enumerate(blocks):
            conv_out_ref[:, cols] = xr_scr[b, 0:P, :]
            h_out_ref[:, cols] = h_scr[b]


def _rglru_layer(x, conv_in, h_in, layer, j, w, bf16, *, S, TB):
    D = D_MODEL
    batch_major_in = x.ndim == 3
    (seg,), R, conv_rows, _ = _segments(((S, TB),), CONV_A)
    nblk = x.size // D // R
    P = (CONV_A - 1) * S
    n_layers, n_a = w["norm_mix_pre"].shape[0], w["rg_conv_b"].shape[0]
    row_spec = pl.BlockSpec((R, D), lambda i: (i, 0))
    if batch_major_in:
        x_spec = pl.BlockSpec(memory_space=pl.ANY)
        staging = [pltpu.VMEM((2, TB, S, D), F32), pltpu.SemaphoreType.DMA((2, S))]
    else:
        x_spec, staging = row_spec, []
    layer_of = lambda arr: pl.BlockSpec((None,) + arr.shape[1:], lambda i: (j,) + (0,) * (arr.ndim - 1),
                                        pipeline_mode=pl.Buffered(1))
    return pl.pallas_call(
        functools.partial(_rglru_kernel, seg, nblk, layer, j, batch_major_in),
        grid=(nblk,),
        in_specs=[x_spec, _resident((P, D)), _resident((S, D)),
                  _resident((n_layers, D)), _resident((n_layers, D)),
                  _resident(bf16["w_xr"].shape), _resident(bf16["w_gate"].shape),
                  layer_of(w["rg_conv_w"]), _resident((n_a, D)), _resident(bf16["gw"].shape),
                  layer_of(w["rg_gate_a_b"]), layer_of(w["rg_gate_x_b"]), _resident((n_a, D)),
                  _resident((D, D))],
        out_specs=[row_spec, pl.BlockSpec((P, D), lambda i: (0, 0)), pl.BlockSpec((S, D), lambda i: (0, 0))],
        out_shape=[jax.ShapeDtypeStruct((nblk * R, D), F32), jax.ShapeDtypeStruct((P, D), F32),
                   jax.ShapeDtypeStruct((S, D), F32)],
        scratch_shapes=[pltpu.VMEM((R, D), BF16),
                        pltpu.VMEM((N_COL_BLOCKS, conv_rows, COL_BLOCK), F32),
                        pltpu.VMEM((N_COL_BLOCKS, S, COL_BLOCK), F32)]
                       + staging,
        compiler_params=_params(),
        name="rglru_mixer",
    )(x, conv_in, h_in, w["norm_mix_pre"], w["norm_mix_post"], bf16["w_xr"], bf16["w_gate"], w["rg_conv_w"],
      w["rg_conv_b"], bf16["gw"], w["rg_gate_a_b"], w["rg_gate_x_b"], w["rg_lambda"], bf16["w_out"])


def _sconv_conv(seg, cv, cw, region, rows=None):
    S = seg.S
    P = (CONV_B - 1) * S
    rows = rows or slice(0, seg.rows)
    c0, n = seg.conv0 + rows.start, rows.stop - rows.start
    region[c0 + P:c0 + P + n, :] = cv
    conv = region[c0:c0 + n, :] * cw[0:1]
    for k in range(1, CONV_B):
        conv = conv + region[c0 + k * S:c0 + k * S + n, :] * cw[k:k + 1]
    if rows.stop == seg.rows:
        region[seg.conv0:seg.conv0 + P, :] = region[seg.conv0 + seg.rows:seg.conv0 + seg.rows + P, :]
    return conv


def _sconv_small_kernel(segs, layer, j, after_fetch_issue, x_ref, *refs):
    D, C, NB = D_MODEL, COL_BLOCK, N_COL_BLOCKS
    n = len(segs)
    conv_in = refs[0:n]
    npre_ref, npost_ref, cw_ref, win_hbm, wout_hbm = refs[n:n + 5]
    o_ref = refs[n + 5]
    conv_out = refs[n + 6:2 * n + 6]
    wbg_b_hbm, wcg_b_hbm, wv_b_hbm, wout_b_hbm = refs[2 * n + 6:2 * n + 10]
    xn_scr, cv_scr, col_f, row_f, col_h, row_h, in_sems, out_sems = refs[2 * n + 10:]

    def fetch(b):
        return ([pltpu.make_async_copy(win_hbm.at[j, :, pl.ds(k * D + b * C, C)], col_f.at[b, k], in_sems.at[b, k])
                 for k in range(3)]
                + [pltpu.make_async_copy(wout_hbm.at[j, pl.ds(b * C, C), :], row_f.at[b], in_sems.at[b, 3])])

    def emit(b):
        s = b % OUT_SLOTS
        return ([pltpu.make_async_copy(col_h.at[s, k], dst.at[b], out_sems.at[s, k])
                 for k, dst in enumerate((wbg_b_hbm, wcg_b_hbm, wv_b_hbm))]
                + [pltpu.make_async_copy(row_h.at[s], wout_b_hbm.at[pl.ds(b * C, C), :], out_sems.at[s, 3])])

    for b in range(NB):
        _start_all(fetch(b))
    after_fetch_issue()
    xn_scr[...] = _rms(x_ref[...], npre_ref[layer:layer + 1, :]).astype(BF16)
    xn = xn_scr[...]

    y = None
    for b in range(NB):
        cols = slice(b * C, (b + 1) * C)
        _wait_all(fetch(b))
        w_bg, w_cg, w_v = (col_f[b, k].astype(BF16) for k in range(3))
        w_out_rows = row_f[b].astype(BF16)
        s = b % OUT_SLOTS
        if b >= OUT_SLOTS:
            _wait_all(emit(b - OUT_SLOTS))
        col_h[s, 0], col_h[s, 1], col_h[s, 2], row_h[s] = w_bg, w_cg, w_v, w_out_rows
        _start_all(emit(b))

        cv = _dot(xn, w_cg) * _dot(xn, w_v)
        convs = []
        for seg, c_in, c_out in zip(segs, conv_in, conv_out):
            P = (CONV_B - 1) * seg.S
            cv_scr[seg.conv0:seg.conv0 + P, :] = c_in[:, cols]
            convs.append(_sconv_conv(seg, cv[seg.row0:seg.row0 + seg.rows], cw_ref[:, cols], cv_scr))
            c_out[:, cols] = cv_scr[seg.conv0:seg.conv0 + P, :]
        m = (_dot(xn, w_bg) * jnp.concatenate(convs, axis=0)).astype(BF16)
        part = _dot(m, w_out_rows)
        y = part if y is None else y + part

    for b in range(max(NB - OUT_SLOTS, 0), NB):
        _wait_all(emit(b))
    o_ref[...] = x_ref[...] + _rms(y, npost_ref[layer:layer + 1, :])


def _sconv_small_part(x, conv_in, layer, j, w, *, seqs_and_steps):
    R, D = x.shape
    C, NB = COL_BLOCK, N_COL_BLOCKS
    segs, _, conv_rows, _ = _segments(seqs_and_steps, CONV_B)
    n_layers = w["norm_mix_pre"].shape[0]
    hbm = pl.BlockSpec(memory_space=pl.ANY)
    return _Part(
        kernel=functools.partial(_sconv_small_kernel, segs, layer, j),
        args=[x] + conv_in + [w["norm_mix_pre"], w["norm_mix_post"], w["sc_conv_w"], w["sc_w_in"], w["sc_w_out"]],
        in_specs=[_resident((R, D))] + [_resident(c.shape) for c in conv_in]
                 + [_resident((n_layers, D)), _resident((n_layers, D)),
                    pl.BlockSpec((None, CONV_B, D), lambda i: (j, 0, 0), pipeline_mode=pl.Buffered(1)), hbm, hbm],
        out_specs=[pl.BlockSpec((R, D), lambda i: (0, 0))]
                  + [pl.BlockSpec(c.shape, lambda i: (0, 0)) for c in conv_in] + [hbm] * 4,
        out_shape=[jax.ShapeDtypeStruct((R, D), F32)] + [jax.ShapeDtypeStruct(c.shape, F32) for c in conv_in]
                  + [jax.ShapeDtypeStruct((NB, D, C), BF16)] * 3 + [jax.ShapeDtypeStruct((D, D), BF16)],
        scratch_shapes=[pltpu.VMEM((R, D), BF16),
                        pltpu.VMEM((conv_rows, C), F32),
                        pltpu.VMEM((NB, 3, D, C), F32),
                        pltpu.VMEM((NB, C, D), F32),
                        pltpu.VMEM((OUT_SLOTS, 3, D, C), BF16), pltpu.VMEM((OUT_SLOTS, C, D), BF16),
                        pltpu.SemaphoreType.DMA((NB, 4)), pltpu.SemaphoreType.DMA((OUT_SLOTS, 4))])


def _sconv_kernel(seg, nblk, layer, x_ref, conv_in_ref, npre_ref, npost_ref, wbg_ref, wcg_ref, wv_ref,
                  cw_ref, wout_ref, o_ref, conv_out_ref, xn_scr, cv_scr, m_scr):
    P = (CONV_B - 1) * seg.S
    i = pl.program_id(0)
    blocks = [slice(b * COL_BLOCK, (b + 1) * COL_BLOCK) for b in range(N_COL_BLOCKS)]

    @pl.when(i == 0)
    def _():
        for b, cols in enumerate(blocks):
            cv_scr[b, 0:P, :] = conv_in_ref[:, cols]

    R = seg.rows
    parts = [slice(k * R // MIXER_ROW_PARTS, (k + 1) * R // MIXER_ROW_PARTS) for k in range(MIXER_ROW_PARTS)]
    for rows in parts:
        xn_scr[rows, :] = _rms(x_ref[rows, :], npre_ref[layer:layer + 1, :]).astype(BF16)
    for rows in parts:
        xn = xn_scr[rows, :]
        for b, cols in enumerate(blocks):
            cv = _dot(xn, wcg_ref[b]) * _dot(xn, wv_ref[b])
            conv = _sconv_conv(seg, cv, cw_ref[:, cols], cv_scr.at[b], rows)
            m_scr[rows, cols] = (_dot(xn, wbg_ref[b]) * conv).astype(BF16)
    ys = [_dot(m_scr[rows, :], wout_ref[...]) for rows in parts]
    for rows, y in zip(parts, ys):
        o_ref[rows, :] = x_ref[rows, :] + _rms(y, npost_ref[layer:layer + 1, :])

    @pl.when(i == nblk - 1)
    def _():
        for b, cols in enumerate(blocks):
            conv_out_ref[:, cols] = cv_scr[b, 0:P, :]


def _sconv_layer(x, conv_in, layer, j, w, bf16, *, S, TB):
    N, D = x.shape
    (seg,), R, conv_rows, _ = _segments(((S, TB),), CONV_B)
    nblk = N // R
    P = (CONV_B - 1) * S
    n_layers = w["norm_mix_pre"].shape[0]
    row_spec = pl.BlockSpec((R, D), lambda i: (i, 0))
    return pl.pallas_call(
        functools.partial(_sconv_kernel, seg, nblk, layer),
        grid=(nblk,),
        in_specs=[row_spec, _resident((P, D)), _resident((n_layers, D)), _resident((n_layers, D)),
                  _resident(bf16["w_bg"].shape), _resident(bf16["w_cg"].shape), _resident(bf16["w_v"].shape),
                  pl.BlockSpec((None, CONV_B, D), lambda i: (j, 0, 0), pipeline_mode=pl.Buffered(1)),
                  _resident((D, D))],
        out_specs=[row_spec, pl.BlockSpec((P, D), lambda i: (0, 0))],
        out_shape=[jax.ShapeDtypeStruct((N, D), F32), jax.ShapeDtypeStruct((P, D), F32)],
        scratch_shapes=[pltpu.VMEM((R, D), BF16),
                        pltpu.VMEM((N_COL_BLOCKS, conv_rows, COL_BLOCK), F32),
                        pltpu.VMEM((R, D), BF16)],
        compiler_params=_params(),
        name="sconv_mixer",
    )(x, conv_in, w["norm_mix_pre"], w["norm_mix_post"], bf16["w_bg"], bf16["w_cg"], bf16["w_v"],
      w["sc_conv_w"], bf16["w_out"])


def _swiglu(g, u):
    return (g * jax.nn.sigmoid(g) * u).astype(BF16)


def _ffn_stream_kernel(layer, nchunk, sample_out, x_ref, npre_ref, npost_ref, wg_hbm, wu_hbm, wd_hbm,
                       o_ref, wg_b_hbm, wu_b_hbm, wd_b_hbm,
                       xn_scr, wg_f, wu_f, wd_f, wg_h, wu_h, wd_h, in_sems, out_sems, *out_staging,
                       prefetch_only=False, prefetched=False):
    F, NB = FF_CHUNK, STREAM_BUFFERS

    def fetch(c):
        k, cols = c % NB, pl.ds(c * F, F)
        return [pltpu.make_async_copy(wg_hbm.at[layer, :, cols], wg_f.at[k], in_sems.at[0, k]),
                pltpu.make_async_copy(wu_hbm.at[layer, :, cols], wu_f.at[k], in_sems.at[1, k]),
                pltpu.make_async_copy(wd_hbm.at[layer, cols, :], wd_f.at[k], in_sems.at[2, k])]

    def emit(c):
        k = c % NB
        return [pltpu.make_async_copy(wg_h.at[k], wg_b_hbm.at[c], out_sems.at[0, k]),
                pltpu.make_async_copy(wu_h.at[k], wu_b_hbm.at[c], out_sems.at[1, k]),
                pltpu.make_async_copy(wd_h.at[k], wd_b_hbm.at[pl.ds(c * F, F), :], out_sems.at[2, k])]

    if not prefetched:
        for c in range(min(NB, nchunk)):
            _start_all(fetch(c))
    if prefetch_only:
        return
    xn_scr[...] = _rms(x_ref[...], npre_ref[layer:layer + 1, :]).astype(BF16)
    xn = xn_scr[...]

    y = None
    for c in range(nchunk):
        k = c % NB
        for cp in fetch(c):
            cp.wait()
        if c >= NB:
            for cp in emit(c - NB):
                cp.wait()
        wg, wu, wd = wg_f[k].astype(BF16), wu_f[k].astype(BF16), wd_f[k].astype(BF16)
        wg_h[k], wu_h[k], wd_h[k] = wg, wu, wd
        for cp in emit(c):
            cp.start()
        if c + NB < nchunk:
            for cp in fetch(c + NB):
                cp.start()
        part = _dot(_swiglu(_dot(xn, wg), _dot(xn, wu)), wd)
        y = part if y is None else y + part
    for c in range(max(nchunk - NB, 0), nchunk):
        for cp in emit(c):
            cp.wait()

    out = x_ref[...] + _rms(y, npost_ref[layer:layer + 1, :])
    if sample_out is None:
        o_ref[...] = out
    else:
        row0, S, T = sample_out
        obuf, sems = out_staging
        obuf[...] = out[row0:row0 + T * S].reshape(T, S, out.shape[-1])
        copies = [pltpu.make_async_copy(obuf.at[t], o_ref.at[:, t, :], sems.at[t]) for t in range(T)]
        for cp in copies:
            cp.start()
        for cp in copies:
            cp.wait()


def _ffn_small_part(R, layer, w, *, sample_out=None):
    D = D_MODEL
    F, NB = FF_CHUNK, STREAM_BUFFERS
    n_layers = w["norm_ffn_pre"].shape[0]
    nchunk = D_FF // F
    hbm = pl.BlockSpec(memory_space=pl.ANY)
    if sample_out is None:
        o_spec, o_shape, staging = pl.BlockSpec((R, D), lambda i: (0, 0)), jax.ShapeDtypeStruct((R, D), F32), []
    else:
        _, S, T = sample_out
        o_spec, o_shape = hbm, jax.ShapeDtypeStruct((S, T, D), F32)
        staging = [pltpu.VMEM((T, S, D), F32), pltpu.SemaphoreType.DMA((T,))]
    return _Part(
        kernel=functools.partial(_ffn_stream_kernel, layer, nchunk, sample_out),
        args=[None, w["norm_ffn_pre"], w["norm_ffn_post"], w["ffn_w_gate"], w["ffn_w_up"], w["ffn_w_down"]],
        in_specs=[_resident((R, D)), _resident((n_layers, D)), _resident((n_layers, D)), hbm, hbm, hbm],
        out_specs=[o_spec, hbm, hbm, hbm],
        out_shape=[o_shape, jax.ShapeDtypeStruct((nchunk, D, F), BF16),
                   jax.ShapeDtypeStruct((nchunk, D, F), BF16), jax.ShapeDtypeStruct((D_FF, D), BF16)],
        scratch_shapes=[pltpu.VMEM((R, D), BF16),
                        pltpu.VMEM((NB, D, F), F32), pltpu.VMEM((NB, D, F), F32), pltpu.VMEM((NB, F, D), F32),
                        pltpu.VMEM((NB, D, F), BF16), pltpu.VMEM((NB, D, F), BF16), pltpu.VMEM((NB, F, D), BF16),
                        pltpu.SemaphoreType.DMA((3, NB)), pltpu.SemaphoreType.DMA((3, NB))] + staging)


def _ffn_kernel(layer, nblk, batch_major_out, x_ref, npre_ref, npost_ref, wg_ref, wu_ref, wd_ref, o_ref,
                xn_scr, act_scr, *out_staging):
    R = x_ref.shape[0]
    i = pl.program_id(0)
    slot = i % 2
    if batch_major_out:
        obuf, sems = out_staging
        S, TB = o_ref.shape[0], R // o_ref.shape[0]

        def copies(which, step):
            t0 = pl.multiple_of(step * TB, TB)
            return [pltpu.make_async_copy(obuf.at[which, :, s, :],
                                          o_ref.at[s, pl.ds(t0, TB), :], sems.at[which, s]) for s in range(S)]

        @pl.when(i >= 2)
        def _():
            for c in copies(slot, i - 2):
                c.wait()

    parts = [slice(k * R // FFN_ROW_PARTS, (k + 1) * R // FFN_ROW_PARTS) for k in range(FFN_ROW_PARTS)]
    for rows in parts:
        xn_scr[rows, :] = _rms(x_ref[rows, :], npre_ref[layer:layer + 1, :]).astype(BF16)
    for rows in parts:
        xn = xn_scr[rows, :]
        for c in range(D_FF // FF_CHUNK):
            cols = slice(c * FF_CHUNK, (c + 1) * FF_CHUNK)
            act_scr[rows, cols] = _swiglu(_dot(xn, wg_ref[c]), _dot(xn, wu_ref[c]))
    ys = [_dot(act_scr[rows, :], wd_ref[...]) for rows in parts]
    for rows, y in zip(parts, ys):
        out = x_ref[rows, :] + _rms(y, npost_ref[layer:layer + 1, :])
        if batch_major_out:
            obuf[slot, rows.start // S:rows.stop // S] = out.reshape(-1, S, out.shape[-1])
        else:
            o_ref[rows, :] = out

    if batch_major_out:
        for c in copies(slot, i):
            c.start()

        @pl.when(i == nblk - 1)
        def _():
            if nblk >= 2:
                for c in copies(1 - slot, i - 1):
                    c.wait()
            for c in copies(slot, i):
                c.wait()


def _ffn_layer(x, layer, w, bf16, *, R, batch_major_seqs=None):
    N, D = x.shape
    n_layers = w["norm_ffn_pre"].shape[0]
    nblk = N // R
    row_spec = pl.BlockSpec((R, D), lambda i: (i, 0))
    if batch_major_seqs is None:
        out_spec, out_shape, staging = row_spec, jax.ShapeDtypeStruct((N, D), F32), []
    else:
        S = batch_major_seqs
        out_spec = pl.BlockSpec(memory_space=pl.ANY)
        out_shape = jax.ShapeDtypeStruct((S, N // S, D), F32)
        staging = [pltpu.VMEM((2, R // S, S, D), F32), pltpu.SemaphoreType.DMA((2, S))]
    return pl.pallas_call(
        functools.partial(_ffn_kernel, layer, nblk, batch_major_seqs is not None),
        grid=(nblk,),
        in_specs=[row_spec, _resident((n_layers, D)), _resident((n_layers, D)),
                  _resident(bf16["wg"].shape), _resident(bf16["wu"].shape), _resident((D_FF, D))],
        out_specs=out_spec,
        out_shape=out_shape,
        scratch_shapes=[pltpu.VMEM((R, D), BF16), pltpu.VMEM((R, D_FF), BF16)] + staging,
        compiler_params=_params(),
        name="swiglu_ffn",
    )(x, w["norm_ffn_pre"], w["norm_ffn_post"], bf16["wg"], bf16["wu"], bf16["wd"])


def _to_time_major(a):
    S, K, D = a.shape
    return jnp.swapaxes(a, 0, 1).reshape(K * S, D)


def _from_time_major(a, S):
    KS, D = a.shape
    return jnp.swapaxes(a.reshape(KS // S, S, D), 0, 1)


def kernel(x_prompt, x_sample, state_rglru_conv, state_rglru_h, state_sconv, meta_tokens, norm_mix_pre, norm_mix_post, norm_ffn_pre, norm_ffn_post, rg_w_in, rg_conv_w, rg_conv_b, rg_gate_a_w, rg_gate_a_b, rg_gate_x_w, rg_gate_x_b, rg_lambda, rg_w_out, sc_w_in, sc_conv_w, sc_w_out, ffn_w_gate, ffn_w_up, ffn_w_down):
    D = D_MODEL
    depth = norm_mix_pre.shape[0]
    batch, seq, _ = x_prompt.shape
    dec_batch, dec_seq, _ = x_sample.shape
    w = dict(norm_mix_pre=norm_mix_pre, norm_mix_post=norm_mix_post, norm_ffn_pre=norm_ffn_pre,
             norm_ffn_post=norm_ffn_post, rg_w_in=rg_w_in, rg_conv_w=rg_conv_w, rg_conv_b=rg_conv_b,
             rg_gate_a_w=rg_gate_a_w, rg_gate_a_b=rg_gate_a_b, rg_gate_x_w=rg_gate_x_w, rg_gate_x_b=rg_gate_x_b,
             rg_lambda=rg_lambda, rg_w_out=rg_w_out, sc_w_in=sc_w_in, sc_conv_w=sc_conv_w, sc_w_out=sc_w_out,
             ffn_w_gate=ffn_w_gate, ffn_w_up=ffn_w_up, ffn_w_down=ffn_w_down)

    x = jnp.broadcast_to(meta_tokens[:, None, :], (N_META, batch, D)).reshape(N_META * batch, D)
    small = ((batch, N_META), (dec_batch, dec_seq))
    small_rows = N_META * batch + dec_batch * dec_seq
    mixer_bf16, ffn_bf16 = [], []
    rg_conv_s, rg_h_s, sc_s = [], [], []
    for i in range(depth):
        j = i // 2
        sample_out = (N_META * batch, dec_batch, dec_seq) if i == depth - 1 else None
        ffn = _ffn_small_part(small_rows, i, w, sample_out=sample_out)
        if i % 2 == 0:
            conv0 = [jnp.zeros(((CONV_A - 1) * batch, D), F32), _to_time_major(state_rglru_conv[j])]
            h0 = [jnp.zeros((batch, D), F32), state_rglru_h[j]]
            mixer = _rglru_small_part(x, conv0, h0, i, j, w, seqs_and_steps=small,
                                      x_sample=x_sample if i == 0 else None)
            (cs0, cs1, hs0, hs1, *wb), (x, *fb) = _run_small_stage(mixer, ffn, small_rows, "rglru_ffn_small")
            rg_conv_s.append([cs0, cs1])
            rg_h_s.append([hs0, hs1])
            mixer_bf16.append(dict(zip(("w_xr", "w_gate", "gw", "w_out"), wb)))
        else:
            conv0 = [jnp.zeros(((CONV_B - 1) * batch, D), F32), _to_time_major(state_sconv[j])]
            mixer = _sconv_small_part(x, conv0, i, j, w, seqs_and_steps=small)
            (cs0, cs1, *wb), (x, *fb) = _run_small_stage(mixer, ffn, small_rows, "sconv_ffn_small")
            sc_s.append([cs0, cs1])
            mixer_bf16.append(dict(zip(("w_bg", "w_cg", "w_v", "w_out"), wb)))
        ffn_bf16.append(dict(zip(("wg", "wu", "wd"), fb)))
    y_sample = x

    x = x_prompt
    rg_conv_p, rg_h_p, sc_p = [], [], []
    for i in range(depth):
        j = i // 2
        if i % 2 == 0:
            x, cb, hT = _rglru_layer(x, rg_conv_s[j][0], rg_h_s[j][0], i, j, w, mixer_bf16[i], S=batch, TB=PROMPT_TB)
            rg_conv_p.append(cb)
            rg_h_p.append(hT)
        else:
            x, cb = _sconv_layer(x, sc_s[j][0], i, j, w, mixer_bf16[i], S=batch, TB=2 * PROMPT_TB)
            sc_p.append(cb)
        x = _ffn_layer(x, i, w, ffn_bf16[i], R=batch * PROMPT_TB,
                       batch_major_seqs=batch if i == depth - 1 else None)
    y_prompt = x

    return (y_prompt, y_sample,
            jnp.stack([_from_time_major(c, batch) for c in rg_conv_p]), jnp.stack(rg_h_p),
            jnp.stack([_from_time_major(c, batch) for c in sc_p]),
            jnp.stack([_from_time_major(c[1], dec_batch) for c in rg_conv_s]), jnp.stack([h[1] for h in rg_h_s]),
            jnp.stack([_from_time_major(c[1], dec_batch) for c in sc_s]))
```
